```python
import jax
import jax.numpy as jnp
from jax import lax
import numpy as np

D_MODEL = 1024
BATCH = 4
SEQ = 4096
DEPTH = 1

GRID_W = 64
CTX_LEN = 256
D_MIX = D_MODEL
HG_WIDTH = D_MIX // 2
HG_HEAD_DIM = 128
HG_HEADS = HG_WIDTH // HG_HEAD_DIM
ML_WIDTH = D_MIX - HG_WIDTH
ML_HEADS = 4
ML_V_DIM = ML_WIDTH // ML_HEADS
ML_QK_DIM = ML_V_DIM // 2
ML_QK_WIDTH = ML_HEADS * ML_QK_DIM
N_ML_GATES = 4 * ML_HEADS
IN_SIZES = (HG_WIDTH, HG_WIDTH, HG_WIDTH, HG_WIDTH, HG_WIDTH, ML_QK_WIDTH, ML_QK_WIDTH, ML_WIDTH, ML_WIDTH, N_ML_GATES)
D_IN = 5 * HG_WIDTH + 2 * ML_QK_WIDTH + 2 * ML_WIDTH + N_ML_GATES
CONV_K = 3
D_FF = 2816
CHUNK = 64
N_MOD = 9
EPS = 1e-6

kernel_name = 'hybrid_hgrn2_mlstm_macaron_dit_layer'


def rmsnorm(x, w):
    xf = x.astype(jnp.float32)
    y = xf * lax.rsqrt(jnp.mean(xf * xf, axis=-1, keepdims=True) + EPS)
    return (y * w.astype(jnp.float32)).astype(x.dtype)


def head_rmsnorm(x, w, n_heads):
    shp = x.shape
    xf = x.astype(jnp.float32).reshape(shp[:-1] + (n_heads, shp[-1] // n_heads))
    y = xf * lax.rsqrt(jnp.mean(xf * xf, axis=-1, keepdims=True) + EPS)
    return y.reshape(shp) * w.astype(jnp.float32)


def modulate(xn, shift, scale):
    return xn * (1 + scale) + shift


def swiglu(h, w1, w2):
    g, u = jnp.split(h @ w1, 2, axis=-1)
    return (jax.nn.silu(g) * u) @ w2


def split_heads(a, n_heads):
    return a.reshape(a.shape[:-1] + (n_heads, a.shape[-1] // n_heads))


def seg_reverse(a, n_ctx):
    return jnp.concatenate([jnp.flip(a[:, :n_ctx], axis=1), jnp.flip(a[:, n_ctx:], axis=1)], axis=1)


def to_chunks(a):
    bp, t, h, d = a.shape
    return a.reshape(bp, t // CHUNK, CHUNK, h, d).transpose(1, 0, 3, 2, 4)


def gates_to_chunks(g):
    bp, t, h = g.shape
    return g.reshape(bp, t // CHUNK, CHUNK, h).transpose(1, 0, 3, 2)


def from_chunks(a):
    n, bp, h, l, d = a.shape
    return a.transpose(1, 0, 3, 2, 4).reshape(bp, n * l, h, d)


def hgrn2_chunk_scan(q, k, v, log_f):
    bp, _, h, dk = q.shape
    dv = v.shape[-1]
    scan_order = jnp.tril(jnp.ones((CHUNK, CHUNK), dtype=bool))

    def step(state, inp):
        qb, kb, vb, fb = inp
        b = jnp.cumsum(fb, axis=2)
        diff = b[:, :, :, None, :] - b[:, :, None, :, :]
        decay = jnp.exp(jnp.where(scan_order[:, :, None], diff, -jnp.inf))
        scores = jnp.einsum('bhtd,bhtsd,bhsd->bhts', qb, decay, kb)
        o = jnp.einsum('bhts,bhse->bhte', scores, vb) + jnp.einsum('bhtd,bhde->bhte', qb * jnp.exp(b), state)
        b_last = b[:, :, -1:, :]
        state = jnp.exp(b_last[:, :, 0, :, None]) * state + jnp.einsum('bhsd,bhse->bhde', kb * jnp.exp(b_last - b), vb)
        return state, o

    s0 = jnp.zeros((bp, h, dk, dv), jnp.float32)
    _, out = lax.scan(step, s0, (to_chunks(q), to_chunks(k), to_chunks(v), to_chunks(log_f)))
    return from_chunks(out)


def mlstm_chunk_scan(q, k, v, i_pre, log_f):
    bp, _, h, dqk = q.shape
    dv = v.shape[-1]
    scan_order = jnp.tril(jnp.ones((CHUNK, CHUNK), dtype=bool))

    def step(carry, inp):
        c_mem, n_mem, m = carry
        qb, kb, vb, ib, fb = inp
        b = jnp.cumsum(fb, axis=-1)
        d_log = jnp.where(scan_order, b[..., :, None] - b[..., None, :] + ib[..., None, :], -jnp.inf)
        inter_log = b + m[..., None]
        m_t = jnp.maximum(jnp.max(d_log, axis=-1), inter_log)
        w = jnp.exp(d_log - m_t[..., None])
        inter_w = jnp.exp(inter_log - m_t)
        qk = jnp.einsum('bhtd,bhsd->bhts', qb, kb) * w
        num = jnp.einsum('bhts,bhse->bhte', qk, vb) + inter_w[..., None] * jnp.einsum('bhtd,bhde->bhte', qb, c_mem)
        den = jnp.sum(qk, axis=-1) + inter_w * jnp.einsum('bhtd,bhd->bht', qb, n_mem)
        h_out = num / jnp.maximum(jnp.abs(den), jnp.exp(-m_t))[..., None]
        b_last = b[..., -1]
        state_log = b_last[..., None] - b + ib
        m_new = jnp.maximum(b_last + m, jnp.max(state_log, axis=-1))
        carry_w = jnp.exp(b_last + m - m_new)
        s_w = jnp.exp(state_log - m_new[..., None])
        c_mem = carry_w[..., None, None] * c_mem + jnp.einsum('bhs,bhsd,bhse->bhde', s_w, kb, vb)
        n_mem = carry_w[..., None] * n_mem + jnp.einsum('bhs,bhsd->bhd', s_w, kb)
        return (c_mem, n_mem, m_new), h_out

    init = (jnp.zeros((bp, h, dqk, dv), jnp.float32), jnp.zeros((bp, h, dqk), jnp.float32), jnp.zeros((bp, h), jnp.float32))
    _, out = lax.scan(step, init, (to_chunks(q), to_chunks(k), to_chunks(v), gates_to_chunks(i_pre), gates_to_chunks(log_f)))
    return from_chunks(out)


def short_conv(a, w, b, n_ctx):
    w = w.astype(a.dtype)
    a_ctx, a_lat = a[:, :n_ctx], a[:, n_ctx:]
    bsz, s_lat, ch = a_lat.shape
    rows = s_lat // GRID_W
    lat = lax.conv_general_dilated(a_lat.reshape(bsz, rows, GRID_W, ch), w[:, :, None, :], (1, 1), 'SAME',
                                   dimension_numbers=('NHWC', 'HWIO', 'NHWC'), feature_group_count=ch)
    ctx = lax.conv_general_dilated(a_ctx, w[CONV_K // 2][:, None, :], (1,), 'SAME',
                                   dimension_numbers=('NWC', 'WIO', 'NWC'), feature_group_count=ch)
    return jnp.concatenate([ctx, lat.reshape(bsz, s_lat, ch)], axis=1) + b


def parallel_mixer(h_ctx, h_lat, w_in, ml_gate_b, conv_w, conv_b, lb, hg_norm_w, ml_norm_w, w_out, lat_only):
    f32 = jnp.float32
    n_ctx = h_ctx.shape[1]
    bsz = h_lat.shape[0]
    out_dtype = h_lat.dtype
    h_all = jnp.concatenate([h_ctx, h_lat], axis=1)
    proj = h_all @ w_in
    splits = [int(s) for s in np.cumsum(IN_SIZES)[:-1]]
    hg_q, hg_i, hg_g, hg_f_fwd, hg_f_bwd, ml_q, ml_k, ml_v, ml_o, ml_gates = jnp.split(proj, splits, axis=-1)

    def both_dirs(a):
        return jnp.concatenate([a, seg_reverse(a, n_ctx)], axis=0)

    def merge_dirs(o):
        return o[:bsz] + seg_reverse(o[bsz:], n_ctx)

    q = split_heads(jax.nn.silu(hg_q.astype(f32)) * HG_HEAD_DIM ** -0.5, HG_HEADS)
    v = split_heads(hg_i.astype(f32), HG_HEADS)
    lb_dir = jnp.repeat(lb, bsz, axis=0)[:, None, :]
    f_pre = jnp.concatenate([hg_f_fwd, seg_reverse(hg_f_bwd, n_ctx)], axis=0).astype(f32)
    log_f = jnp.log(lb_dir + (1 - lb_dir) * jax.nn.sigmoid(f_pre))
    k_in = (1 - lb_dir) * jax.nn.sigmoid(-f_pre)
    hg_o = hgrn2_chunk_scan(both_dirs(q), split_heads(k_in, HG_HEADS), both_dirs(v), split_heads(log_f, HG_HEADS))
    hg_o = merge_dirs(hg_o).reshape(bsz, -1, HG_WIDTH)
    hg_out = head_rmsnorm(hg_o, hg_norm_w, HG_HEADS) * jax.nn.silu(hg_g.astype(f32))

    qk = jax.nn.silu(short_conv(jnp.concatenate([ml_q, ml_k], axis=-1), conv_w, conv_b, n_ctx)).astype(f32)
    mq, mk = jnp.split(qk, 2, axis=-1)
    mq = split_heads(mq * ML_QK_DIM ** -0.5, ML_HEADS)
    mk = split_heads(mk, ML_HEADS)
    mv = split_heads(ml_v.astype(f32), ML_HEADS)
    gates = ml_gates.astype(f32) + ml_gate_b.astype(f32)
    i_fwd, f_fwd, i_bwd, f_bwd = jnp.split(gates, 4, axis=-1)
    i_dir = jnp.concatenate([i_fwd, seg_reverse(i_bwd, n_ctx)], axis=0)
    logf_dir = jax.nn.log_sigmoid(jnp.concatenate([f_fwd, seg_reverse(f_bwd, n_ctx)], axis=0))
    ml_h = mlstm_chunk_scan(both_dirs(mq), both_dirs(mk), both_dirs(mv), i_dir, logf_dir)
    ml_h = merge_dirs(ml_h).reshape(bsz, -1, ML_WIDTH)
    ml_out = jax.nn.sigmoid(ml_o.astype(f32)) * head_rmsnorm(ml_h, ml_norm_w, ML_HEADS)

    merged = jnp.concatenate([hg_out, ml_out], axis=-1).astype(out_dtype)
    lat_out = merged[:, n_ctx:] @ w_out
    ctx_out = None if lat_only else merged[:, :n_ctx] @ w_out
    return ctx_out, lat_out


def setup_inputs(seed: int = 0) -> dict:
    key = jax.random.key(seed)
    ks = jax.random.split(key, 26)
    nrm = jax.random.normal
    f32 = jnp.float32

    def gain(k, shape):
        return 1.0 + 0.01 * nrm(k, shape, f32)

    def fbias(k):
        return 3.0 + 3.0 * jax.random.uniform(k, (DEPTH, ML_HEADS), f32)

    ml_gate_b = jnp.concatenate([0.1 * nrm(ks[10], (DEPTH, ML_HEADS), f32), fbias(ks[11]),
                                 0.1 * nrm(ks[12], (DEPTH, ML_HEADS), f32), fbias(ks[13])], axis=-1)
    return {
        'x': nrm(ks[0], (BATCH, SEQ, D_MODEL), f32),
        'c': nrm(ks[1], (BATCH, D_MODEL), f32),
        'ctx': nrm(ks[2], (BATCH, CTX_LEN, D_MODEL), f32),
        'c_ctx': nrm(ks[3], (D_MODEL,), f32),
        'w_mod': nrm(ks[4], (DEPTH, D_MODEL, N_MOD * D_MODEL), f32) * D_MODEL ** -0.5,
        'b_mod': 0.01 * nrm(ks[5], (DEPTH, N_MOD * D_MODEL), f32),
        'norm1_w': gain(ks[6], (DEPTH, D_MODEL)),
        'ffn1_w1': nrm(ks[7], (DEPTH, D_MODEL, 2 * D_FF), f32) * D_MODEL ** -0.5,
        'ffn1_w2': nrm(ks[8], (DEPTH, D_FF, D_MODEL), f32) * D_FF ** -0.5,
        'norm2_w': gain(ks[9], (DEPTH, D_MODEL)),
        'w_in': nrm(ks[14], (DEPTH, D_MODEL, D_IN), f32) * D_MODEL ** -0.5,
        'ml_gate_b': ml_gate_b,
        'ml_conv_w': nrm(ks[15], (DEPTH, CONV_K, CONV_K, 2 * ML_QK_WIDTH), f32) / CONV_K,
        'ml_conv_b': 0.01 * nrm(ks[16], (DEPTH, 2 * ML_QK_WIDTH), f32),
        'hg_lb_logits': 0.5 * nrm(ks[17], (DEPTH + 1, 2, HG_WIDTH), f32),
        'hg_norm_w': gain(ks[18], (DEPTH, HG_WIDTH)),
        'ml_norm_w': gain(ks[19], (DEPTH, ML_WIDTH)),
        'w_out': nrm(ks[20], (DEPTH, D_MIX, D_MODEL), f32) * D_MIX ** -0.5,
        'norm3_w': gain(ks[21], (DEPTH, D_MODEL)),
        'ffn2_w1': nrm(ks[22], (DEPTH, D_MODEL, 2 * D_FF), f32) * D_MODEL ** -0.5,
        'ffn2_w2': nrm(ks[23], (DEPTH, D_FF, D_MODEL), f32) * D_FF ** -0.5,
        'final_norm_w': gain(ks[24], (D_MODEL,)),
    }


def reference(x, c, ctx, c_ctx, w_mod, b_mod, norm1_w, ffn1_w1, ffn1_w2, norm2_w, w_in, ml_gate_b,
              ml_conv_w, ml_conv_b, hg_lb_logits, hg_norm_w, ml_norm_w, w_out, norm3_w, ffn2_w1, ffn2_w2,
              final_norm_w):
    lb_all = jnp.cumsum(jax.nn.softmax(hg_lb_logits.astype(jnp.float32), axis=0), axis=0)
    s_ctx = ctx

    def ffn_sub(s, mods, nw, w1, w2):
        shift, scale, gate = mods
        return s + 0.5 * gate * swiglu(modulate(rmsnorm(s, nw), shift, scale), w1, w2)

    for l in range(DEPTH):
        last = l == DEPTH - 1
        mod_lat = jnp.split((jax.nn.silu(c) @ w_mod[l] + b_mod[l])[:, None, :], N_MOD, axis=-1)
        mod_ctx = jnp.split(jax.nn.silu(c_ctx) @ w_mod[l] + b_mod[l], N_MOD, axis=-1)

        x = ffn_sub(x, mod_lat[0:3], norm1_w[l], ffn1_w1[l], ffn1_w2[l])
        s_ctx = ffn_sub(s_ctx, mod_ctx[0:3], norm1_w[l], ffn1_w1[l], ffn1_w2[l])

        h_lat = modulate(rmsnorm(x, norm2_w[l]), mod_lat[3], mod_lat[4])
        h_ctx = modulate(rmsnorm(s_ctx, norm2_w[l]), mod_ctx[3], mod_ctx[4])
        ctx_mix, lat_mix = parallel_mixer(h_ctx, h_lat, w_in[l], ml_gate_b[l], ml_conv_w[l], ml_conv_b[l],
                                          lb_all[l], hg_norm_w[l], ml_norm_w[l], w_out[l], last)
        x = x + mod_lat[5] * lat_mix

        x = ffn_sub(x, mod_lat[6:9], norm3_w[l], ffn2_w1[l], ffn2_w2[l])
        if not last:
            s_ctx = s_ctx + mod_ctx[5] * ctx_mix
            s_ctx = ffn_sub(s_ctx, mod_ctx[6:9], norm3_w[l], ffn2_w1[l], ffn2_w2[l])

    return rmsnorm(x, final_norm_w)
```

```python
import functools

import numpy as np
import jax
import jax.numpy as jnp
from jax import lax
from jax.experimental import pallas as pl
from jax.experimental.pallas import tpu as pltpu

F32 = jnp.float32
BF16 = jnp.bfloat16

EPS = 1e-6
CHUNK = 64
GRID_W = 64
N_MOD = 9
HG_HEADS = 4
HG_DIM = 128
ML_HEADS = 4
ML_QK = 64
ML_V = 128
SUB = 16
N_SUB = CHUNK // SUB
EXP_CLAMP = 80.0
LANE = 128
VMEM_LIMIT = 56 * 1024 * 1024


def _cparams(sem):
    return pltpu.CompilerParams(dimension_semantics=sem, vmem_limit_bytes=VMEM_LIMIT)


def _dot(a, b):
    return jnp.dot(a, b, preferred_element_type=F32)


def _dot_nt(a, b):
    return lax.dot_general(a, b, (((1,), (1,)), ((), ())), preferred_element_type=F32)


def _dot_tn(a, b):
    return lax.dot_general(a, b, (((0,), (0,)), ((), ())), preferred_element_type=F32)


def _sigmoid(x):
    return 1.0 / (1.0 + jnp.exp(-x))


def _silu(x):
    return x * _sigmoid(x)


def _log_sigmoid(x):
    return jnp.minimum(x, 0.0) - jnp.log(1.0 + jnp.exp(-jnp.abs(x)))


def _split3(x):
    hi = x.astype(BF16)
    r = x - hi.astype(F32)
    mid = r.astype(BF16)
    lo = (r - mid.astype(F32)).astype(BF16)
    return hi, mid, lo


def _rms(x):
    return x * lax.rsqrt(jnp.mean(x * x, axis=-1, keepdims=True) + EPS)


def _mod_kernel(c_ref, w_ref, b_ref, o_ref):
    a = _silu(c_ref[...])
    a_hi = a.astype(BF16)
    a_lo = (a - a_hi.astype(F32)).astype(BF16)
    w = w_ref[...]
    w_hi = w.astype(BF16)
    w_lo = (w - w_hi.astype(F32)).astype(BF16)
    o_ref[...] = _dot(a_hi, w_hi) + _dot(a_hi, w_lo) + _dot(a_lo, w_hi) + b_ref[...]


def _modulation(cvec, w_mod, b_mod):
    rows, d = cvec.shape
    n = w_mod.shape[1]
    tn = 1024
    return pl.pallas_call(
        _mod_kernel,
        grid=(n // tn,),
        in_specs=[pl.BlockSpec((rows, d), lambda j: (0, 0)),
                  pl.BlockSpec((d, tn), lambda j: (0, j)),
                  pl.BlockSpec((1, tn), lambda j: (0, j))],
        out_specs=pl.BlockSpec((rows, tn), lambda j: (0, j)),
        out_shape=jax.ShapeDtypeStruct((rows, n), F32),
        compiler_params=_cparams(("arbitrary",)),
        name="modulation",
    )(cvec, w_mod, b_mod)


def _ffn_kernel(x_ref, mod_ref, nw_ref, w1_ref, w2_ref, fw_ref, o_ref, a_ref, *, d_ff, tf, final):
    x = x_ref[0]
    shift = mod_ref[0, 0:1, :]
    scale = mod_ref[0, 1:2, :]
    gate = mod_ref[0, 2:3, :]
    h = (_rms(x) * nw_ref[...] * (1.0 + scale) + shift).astype(BF16)
    for c in range(d_ff // tf):
        g = _dot(h, w1_ref[:, c * tf:(c + 1) * tf])
        u = _dot(h, w1_ref[:, d_ff + c * tf:d_ff + (c + 1) * tf])
        a_ref[:, c * tf:(c + 1) * tf] = (_silu(g) * u).astype(BF16)
    y = x + 0.5 * gate * _dot(a_ref[...], w2_ref[...])
    if final:
        y = _rms(y) * fw_ref[...]
    o_ref[0] = y


def _ffn(s, mods, mod_row, nw, w1, w2, fw, final):
    bsz, t, d = s.shape
    d_ff = w2.shape[0]
    tm = min(512, t)
    tf = 256
    const = lambda b, i: (0, 0)
    return pl.pallas_call(
        functools.partial(_ffn_kernel, d_ff=d_ff, tf=tf, final=final),
        grid=(bsz, t // tm),
        in_specs=[pl.BlockSpec((1, tm, d), lambda b, i: (b, i, 0)),
                  pl.BlockSpec((1, 3, d), lambda b, i: (mod_row(b), 0, 0)),
                  pl.BlockSpec((1, d), const),
                  pl.BlockSpec((d, 2 * d_ff), const, pipeline_mode=pl.Buffered(1)),
                  pl.BlockSpec((d_ff, d), const, pipeline_mode=pl.Buffered(1)),
                  pl.BlockSpec((1, d), const)],
        out_specs=pl.BlockSpec((1, tm, d), lambda b, i: (b, i, 0)),
        out_shape=jax.ShapeDtypeStruct((bsz, t, d), F32),
        scratch_shapes=[pltpu.VMEM((tm, d_ff), BF16)],
        compiler_params=_cparams(("parallel", "parallel")),
        name="ffn_final" if final else "ffn",
    )(s, mods, nw, w1, w2, fw)


def _inproj_kernel(x_ref, mod_ref, nw_ref, w_ref, wg_ref, gb_ref, lbl_ref,
                   hq_ref, hv_ref, hg_ref, hk_ref, hlf_ref, mqk_ref, mv_ref, mo_ref, gc_ref, *, hw):
    x = x_ref[0]
    shift = mod_ref[0, 0:1, :]
    scale = mod_ref[0, 1:2, :]
    h = (_rms(x) * nw_ref[...] * (1.0 + scale) + shift).astype(BF16)

    def proj(k):
        return _dot(h, w_ref[:, k * hw:(k + 1) * hw])

    hq_ref[0] = _silu(proj(0)) * (HG_DIM ** -0.5)
    hv_ref[0] = proj(1)
    hg_ref[0] = proj(2)
    lbl = lbl_ref[...]
    e = jnp.exp(lbl - jnp.max(lbl, axis=0, keepdims=True))
    lb = e[0] / jnp.sum(e, axis=0)
    for d in range(2):
        f_pre = proj(3 + d)
        lbd = lb[d:d + 1, :]
        hlf_ref[0, d] = jnp.log(lbd + (1.0 - lbd) * _sigmoid(f_pre))
        hk_ref[0, d] = (1.0 - lbd) * _sigmoid(-f_pre)
    mqk_ref[0] = proj(5)
    mv_ref[0] = proj(6)
    mo_ref[0] = proj(7)
    ng = gc_ref.shape[-1]
    gc_ref[0] = _dot(h, wg_ref[...])[:, :ng] + gb_ref[...]


def _inproj(s, mods, mod_row, nw, w_main, w_gate, gate_b, lb_logits):
    bsz, t, d = s.shape
    hw = HG_HEADS * HG_DIM
    ng = gate_b.shape[-1]
    tm = min(512, t)
    const = lambda b, i: (0, 0)
    tok = lambda b, i: (b, i, 0)
    tok2 = lambda b, i: (b, 0, i, 0)
    f = lambda *shape: jax.ShapeDtypeStruct(shape, F32)
    return pl.pallas_call(
        functools.partial(_inproj_kernel, hw=hw),
        grid=(bsz, t // tm),
        in_specs=[pl.BlockSpec((1, tm, d), tok),
                  pl.BlockSpec((1, 2, d), lambda b, i: (mod_row(b), 0, 0)),
                  pl.BlockSpec((1, d), const),
                  pl.BlockSpec(w_main.shape, const, pipeline_mode=pl.Buffered(1)),
                  pl.BlockSpec(w_gate.shape, const, pipeline_mode=pl.Buffered(1)),
                  pl.BlockSpec((1, ng), const),
                  pl.BlockSpec(lb_logits.shape, lambda b, i: (0, 0, 0))],
        out_specs=[pl.BlockSpec((1, tm, hw), tok), pl.BlockSpec((1, tm, hw), tok),
                   pl.BlockSpec((1, tm, hw), tok),
                   pl.BlockSpec((1, 2, tm, hw), tok2), pl.BlockSpec((1, 2, tm, hw), tok2),
                   pl.BlockSpec((1, tm, hw), tok), pl.BlockSpec((1, tm, hw), tok),
                   pl.BlockSpec((1, tm, hw), tok), pl.BlockSpec((1, tm, ng), tok)],
        out_shape=[f(bsz, t, hw), f(bsz, t, hw), f(bsz, t, hw),
                   f(bsz, 2, t, hw), f(bsz, 2, t, hw),
                   f(bsz, t, hw), f(bsz, t, hw), f(bsz, t, hw), f(bsz, t, ng)],
        compiler_params=_cparams(("parallel", "parallel")),
        name="inproj",
    )(s, mods, nw, w_main, w_gate, gate_b, lb_logits)


def _conv_kernel(x_ref, w_ref, b_ref, o_ref, *, width, q_tiles):
    x = x_ref[0]
    t = x.shape[0]
    col = lax.broadcasted_iota(jnp.int32, x.shape, 0) % width
    xl = jnp.where(col == 0, 0.0, pltpu.roll(x, 1, 0))
    xr = jnp.where(col == width - 1, 0.0, pltpu.roll(x, t - 1, 0))

    def row(di):
        return (w_ref[di, 0:1, :] * xl + w_ref[di, 1:2, :] * x + w_ref[di, 2:3, :] * xr)

    y = row(1)
    if t > width:
        pad = jnp.zeros((width, x.shape[1]), F32)
        y = y + jnp.concatenate([pad, row(0)[:t - width]], axis=0)
        y = y + jnp.concatenate([row(2)[width:], pad], axis=0)
    y = _silu(y + b_ref[...])
    qscale = jnp.where(pl.program_id(1) < q_tiles, ML_QK ** -0.5, 1.0)
    o_ref[0] = y * qscale


def _conv(x, w, b, width):
    bsz, t, ch = x.shape
    return pl.pallas_call(
        functools.partial(_conv_kernel, width=width, q_tiles=ML_HEADS * ML_QK // LANE),
        grid=(bsz, ch // LANE),
        in_specs=[pl.BlockSpec((1, t, LANE), lambda b_, c: (b_, 0, c)),
                  pl.BlockSpec((3, 3, LANE), lambda b_, c: (0, 0, c)),
                  pl.BlockSpec((1, LANE), lambda b_, c: (0, c))],
        out_specs=pl.BlockSpec((1, t, LANE), lambda b_, c: (b_, 0, c)),
        out_shape=jax.ShapeDtypeStruct((bsz, t, ch), F32),
        compiler_params=_cparams(("parallel", "parallel")),
        name="conv",
    )(x, w, b)


def _positions():
    t = np.arange(CHUNK)
    return [t, CHUNK - 1 - t]


def _hgrn2_masks():
    ds, ws = [], []
    for p in _positions():
        le = (p[None, :] <= p[:, None]).astype(np.float32)
        mid = SUB * (p // SUB) + SUB // 2 - 1
        dq = le - (p[None, :] <= mid[:, None])
        dk = [(p[None, :] <= (SUB * i + SUB // 2 - 1)) - le for i in range(N_SUB)]
        after = (p[None, :] > p[:, None]).astype(np.float32)
        ones = np.ones((8, CHUNK), np.float32)
        ds.append(np.concatenate([dq] + dk + [le, after, ones], axis=0))
        w = np.concatenate([((p[:, None] // SUB) == i) * le for i in range(N_SUB)], axis=1)
        ws.append(w)
    return np.stack(ds).astype(np.float32), np.stack(ws).astype(np.float32)


def _mlstm_masks():
    ms, mts = [], []
    for p in _positions():
        le = (p[None, :] <= p[:, None]).astype(np.float32)
        ms.append(np.concatenate([le, np.ones((8, CHUNK), np.float32)], axis=0))
        mts.append(le.T)
    return np.stack(ms), np.stack(mts)


def _chunk_rows(c, nc, direction):
    cc = jnp.where(direction == 0, c, nc - 1 - c)
    return pl.multiple_of(cc * CHUNK, CHUNK)


def _hgrn2_kernel(q_ref, v_ref, k_ref, lf_ref, d_ref, w_ref, s0_ref, o_ref, st_ref, *, nc):
    direction = pl.program_id(1)

    @pl.when(pl.program_id(2) == 0)
    def _():
        st_ref[...] = s0_ref[...]

    dmat = d_ref[0].astype(BF16)
    wmask = w_ref[0] > 0.0
    L = CHUNK

    def body(c, carry):
        r0 = _chunk_rows(c, nc, direction)
        rows = pl.ds(r0, L)
        q = q_ref[0, rows, :]
        v = v_ref[0, rows, :]
        k = k_ref[0, 0, rows, :]
        hi, mid, lo = _split3(lf_ref[0, 0, rows, :])
        e = _dot(dmat, hi) + _dot(dmat, mid) + _dot(dmat, lo)
        q_mid = (q * jnp.exp(jnp.minimum(e[0:L], EXP_CLAMP))).astype(BF16)
        k4 = jnp.concatenate([k] * N_SUB, axis=0)
        k_mid = (k4 * jnp.exp(jnp.minimum(e[L:5 * L], EXP_CLAMP))).astype(BF16)
        q_in = (q * jnp.exp(e[5 * L:6 * L])).astype(BF16)
        k_out = (k * jnp.exp(e[6 * L:7 * L])).astype(BF16)
        g = jnp.exp(e[7 * L:7 * L + 1])
        vb = v.astype(BF16)
        v4 = jnp.concatenate([vb] * N_SUB, axis=0)
        for h in range(HG_HEADS):
            ln = slice(h * HG_DIM, (h + 1) * HG_DIM)
            scores = _dot_nt(q_mid[:, ln], k_mid[:, ln])
            a = jnp.where(wmask, scores, 0.0).astype(BF16)
            st = st_ref[0, 0, h]
            o = _dot(a, v4[:, ln]) + _dot_nt(q_in[:, ln], st.astype(BF16))
            o_ref[0, 0, rows, ln] = o
            st_ref[0, 0, h] = g[:, ln] * st + _dot_tn(vb[:, ln], k_out[:, ln])
        return carry

    lax.fori_loop(0, nc, body, 0)


def _hgrn2(q, v, k, lf, s0, dmat, wmask):
    bsz, t, hw = q.shape
    nc = min(8, t // CHUNK)
    tb = nc * CHUNK
    nblk = t // tb

    def blk(j, d):
        return jnp.where(d == 0, j, nblk - 1 - j)

    st_spec = pl.BlockSpec((1, 1) + s0.shape[2:], lambda b, d, j: (b, d, 0, 0, 0))
    return pl.pallas_call(
        functools.partial(_hgrn2_kernel, nc=nc),
        grid=(bsz, 2, nblk),
        in_specs=[pl.BlockSpec((1, tb, hw), lambda b, d, j: (b, blk(j, d), 0)),
                  pl.BlockSpec((1, tb, hw), lambda b, d, j: (b, blk(j, d), 0)),
                  pl.BlockSpec((1, 1, tb, hw), lambda b, d, j: (b, d, blk(j, d), 0)),
                  pl.BlockSpec((1, 1, tb, hw), lambda b, d, j: (b, d, blk(j, d), 0)),
                  pl.BlockSpec((1,) + dmat.shape[1:], lambda b, d, j: (d, 0, 0)),
                  pl.BlockSpec((1,) + wmask.shape[1:], lambda b, d, j: (d, 0, 0)),
                  st_spec],
        out_specs=[pl.BlockSpec((1, 1, tb, hw), lambda b, d, j: (b, d, blk(j, d), 0)), st_spec],
        out_shape=[jax.ShapeDtypeStruct((bsz, 2, t, hw), F32),
                   jax.ShapeDtypeStruct(s0.shape, F32)],
        compiler_params=_cparams(("parallel", "parallel", "arbitrary")),
        name="hgrn2_scan",
    )(q, v, k, lf, dmat, wmask, s0)


def _mlstm_kernel(qk_ref, v_ref, gc_ref, gr_ref, m_ref, mt_ref, c0_ref, m0_ref,
                  o_ref, c_ref, mm_ref, *, nc):
    direction = pl.program_id(1)

    @pl.when(pl.program_id(2) == 0)
    def _():
        c_ref[...] = c0_ref[...]
        mm_ref[...] = m0_ref[...]

    mext = m_ref[0].astype(BF16)
    mt = mt_ref[0].astype(BF16)
    causal = m_ref[0, 0:CHUNK, :] > 0.0
    L = CHUNK
    nh = ML_HEADS
    qw = nh * ML_QK
    ones_v = jnp.ones((L, ML_V), BF16)

    def body(c, carry):
        cc = jnp.where(direction == 0, c, nc - 1 - c)
        r0 = pl.multiple_of(cc * L, L)
        rows = pl.ds(r0, L)
        qk = qk_ref[0, rows, :]
        vb = v_ref[0, rows, :].astype(BF16)
        gc_all = gc_ref[0, rows, :]
        gr_all = gr_ref[0, cc]
        gc = jnp.where(direction == 0, gc_all[:, 0:2 * nh], gc_all[:, 2 * nh:4 * nh])
        gr = jnp.where(direction == 0, gr_all[0:2 * nh], gr_all[2 * nh:4 * nh])
        hi, mid, lo = _split3(_log_sigmoid(gc))
        e_col = _dot(mext, hi) + _dot(mext, mid) + _dot(mext, lo)
        hi, mid, lo = _split3(_log_sigmoid(gr))
        b_row = _dot(hi, mt) + _dot(mid, mt) + _dot(lo, mt)
        for h in range(nh):
            q_h = qk[:, h * ML_QK:(h + 1) * ML_QK].astype(BF16)
            k_h = qk[:, qw + h * ML_QK:qw + (h + 1) * ML_QK]
            c_t = e_col[0:L, nh + h:nh + h + 1]
            b_tot = e_col[L:L + 1, nh + h:nh + h + 1]
            i_col = gc[:, h:h + 1]
            r_s = gr[h:h + 1, :] - b_row[nh + h:nh + h + 1, :]
            m_in = mm_ref[0, 0, h:h + 1, 0:1]
            d_log = jnp.where(causal, c_t + r_s, -jnp.inf)
            m_t = jnp.maximum(jnp.max(d_log, axis=1, keepdims=True), c_t + m_in)
            w = jnp.exp(d_log - m_t)
            inter_w = jnp.exp(c_t + m_in - m_t)
            a = (_dot_nt(q_h, k_h.astype(BF16)) * w).astype(BF16)
            v_aug = jnp.concatenate([vb[:, h * ML_V:(h + 1) * ML_V], ones_v], axis=1)
            c_aug = c_ref[0, 0, h]
            tot = _dot(a, v_aug) + inter_w * _dot(q_h, c_aug.astype(BF16))
            den = jnp.maximum(jnp.abs(tot[:, ML_V:]), jnp.exp(-m_t))
            o_ref[0, 0, rows, h * ML_V:(h + 1) * ML_V] = tot[:, :ML_V] / den
            m_new = jnp.maximum(b_tot + m_in, jnp.max(b_tot + r_s, axis=1, keepdims=True))
            carry_w = jnp.exp(b_tot + m_in - m_new)
            s_w = jnp.exp(b_tot - c_t + i_col - m_new)
            c_ref[0, 0, h] = carry_w * c_aug + _dot_tn((k_h * s_w).astype(BF16), v_aug)
            mm_ref[0, 0, h:h + 1, :] = jnp.broadcast_to(m_new, (1, LANE))
        return carry

    lax.fori_loop(0, nc, body, 0)


def _mlstm(qk, v, gcol, grow, c0, m0, mext, mt):
    bsz, t, hw = v.shape
    nc = min(8, t // CHUNK)
    tb = nc * CHUNK
    nblk = t // tb
    ng = gcol.shape[-1]

    def blk(j, d):
        return jnp.where(d == 0, j, nblk - 1 - j)

    c_spec = pl.BlockSpec((1, 1) + c0.shape[2:], lambda b, d, j: (b, d, 0, 0, 0))
    m_spec = pl.BlockSpec((1, 1) + m0.shape[2:], lambda b, d, j: (b, d, 0, 0))
    return pl.pallas_call(
        functools.partial(_mlstm_kernel, nc=nc),
        grid=(bsz, 2, nblk),
        in_specs=[pl.BlockSpec((1, tb, qk.shape[-1]), lambda b, d, j: (b, blk(j, d), 0)),
                  pl.BlockSpec((1, tb, hw), lambda b, d, j: (b, blk(j, d), 0)),
                  pl.BlockSpec((1, tb, ng), lambda b, d, j: (b, blk(j, d), 0)),
                  pl.BlockSpec((1, nc, ng, CHUNK), lambda b, d, j: (b, blk(j, d), 0, 0)),
                  pl.BlockSpec((1,) + mext.shape[1:], lambda b, d, j: (d, 0, 0)),
                  pl.BlockSpec((1,) + mt.shape[1:], lambda b, d, j: (d, 0, 0)),
                  c_spec, m_spec],
        out_specs=[pl.BlockSpec((1, 1, tb, hw), lambda b, d, j: (b, d, blk(j, d), 0)), c_spec, m_spec],
        out_shape=[jax.ShapeDtypeStruct((bsz, 2, t, hw), F32),
                   jax.ShapeDtypeStruct(c0.shape, F32),
                   jax.ShapeDtypeStruct(m0.shape, F32)],
        compiler_params=_cparams(("parallel", "parallel", "arbitrary")),
        name="mlstm_scan",
    )(qk, v, gcol, grow, mext, mt, c0, m0)


def _head_rms(x, heads):
    dh = x.shape[-1] // heads
    return jnp.concatenate([_rms(x[:, h * dh:(h + 1) * dh]) for h in range(heads)], axis=1)


def _outproj_kernel(oh_ref, om_ref, hg_ref, mo_ref, x_ref, mod_ref, hnw_ref, mnw_ref, w_ref, o_ref):
    oh = oh_ref[0, 0] + oh_ref[0, 1]
    om = om_ref[0, 0] + om_ref[0, 1]
    hg_out = _head_rms(oh, HG_HEADS) * hnw_ref[...] * _silu(hg_ref[0])
    ml_out = _sigmoid(mo_ref[0]) * (_head_rms(om, ML_HEADS) * mnw_ref[...])
    merged = jnp.concatenate([hg_out, ml_out], axis=1).astype(BF16)
    o_ref[0] = x_ref[0] + mod_ref[0] * _dot(merged, w_ref[...])


def _outproj(oh, om, hg, mo, x, mod5, hnw, mnw, w_out):
    bsz, t, d = x.shape
    hw = hg.shape[-1]
    tm = min(512, t)
    tok = lambda b, i: (b, i, 0)
    tok2 = lambda b, i: (b, 0, i, 0)
    const = lambda b, i: (0, 0)
    return pl.pallas_call(
        _outproj_kernel,
        grid=(bsz, t // tm),
        in_specs=[pl.BlockSpec((1, 2, tm, hw), tok2), pl.BlockSpec((1, 2, tm, hw), tok2),
                  pl.BlockSpec((1, tm, hw), tok), pl.BlockSpec((1, tm, hw), tok),
                  pl.BlockSpec((1, tm, d), tok),
                  pl.BlockSpec((1, 1, d), lambda b, i: (b, 0, 0)),
                  pl.BlockSpec((1, hw), const), pl.BlockSpec((1, hw), const),
                  pl.BlockSpec(w_out.shape, const, pipeline_mode=pl.Buffered(1))],
        out_specs=pl.BlockSpec((1, tm, d), tok),
        out_shape=jax.ShapeDtypeStruct((bsz, t, d), F32),
        compiler_params=_cparams(("parallel", "parallel")),
        name="outproj",
    )(oh, om, hg, mo, x, mod5, hnw, mnw, w_out)


def kernel(x, c, ctx, c_ctx, w_mod, b_mod, norm1_w, ffn1_w1, ffn1_w2, norm2_w, w_in, ml_gate_b,
           ml_conv_w, ml_conv_b, hg_lb_logits, hg_norm_w, ml_norm_w, w_out, norm3_w, ffn2_w1, ffn2_w2,
           final_norm_w):
    bsz, seq, d = x.shape
    n_ctx = ctx.shape[1]
    assert w_mod.shape[0] == 1, "single-layer kernel"
    assert seq % (GRID_W * 8) == 0 and n_ctx % CHUNK == 0
    hw = HG_HEADS * HG_DIM
    ng = ml_gate_b.shape[-1]

    rows = -(-(bsz + 1) // 8) * 8
    cvec = jnp.zeros((rows, d), F32).at[:bsz].set(c).at[bsz].set(c_ctx)
    mods = _modulation(cvec, w_mod[0], b_mod[0][None, :]).reshape(rows, N_MOD, d)
    lat_row = lambda b: b
    ctx_row = lambda b: bsz

    row = lambda a: a.reshape(1, -1)
    w1a, w2a = ffn1_w1[0].astype(BF16), ffn1_w2[0].astype(BF16)
    x1 = _ffn(x, mods[:, 0:3], lat_row, row(norm1_w[0]), w1a, w2a, row(final_norm_w), False)
    s1 = _ffn(ctx, mods[:, 0:3], ctx_row, row(norm1_w[0]), w1a, w2a, row(final_norm_w), False)

    w_main = w_in[0][:, :8 * hw].astype(BF16)
    w_gate = jnp.zeros((d, LANE), BF16).at[:, :ng].set(w_in[0][:, 8 * hw:].astype(BF16))
    gate_b = row(ml_gate_b[0])
    dmat, wmask = (jnp.asarray(a) for a in _hgrn2_masks())
    mext, mt = (jnp.asarray(a) for a in _mlstm_masks())
    conv_w, conv_b = ml_conv_w[0], row(ml_conv_b[0])

    def mixer_scans(s, mod_row, width, s0, c0, m0):
        hq, hv, hg, hk, hlf, mqk, mv, mo, gcol = _inproj(
            s, mods[:, 3:5], mod_row, row(norm2_w[0]), w_main, w_gate, gate_b, hg_lb_logits)
        t = s.shape[1]
        grow = gcol.reshape(bsz, t // CHUNK, CHUNK, ng).transpose(0, 1, 3, 2)
        qk = _conv(mqk, conv_w, conv_b, width)
        oh, s_fin = _hgrn2(hq, hv, hk, hlf, s0, dmat, wmask)
        om, c_fin, m_fin = _mlstm(qk, mv, gcol, grow, c0, m0, mext, mt)
        return oh, om, hg, mo, s_fin, c_fin, m_fin

    s0 = jnp.zeros((bsz, 2, HG_HEADS, HG_DIM, HG_DIM), F32)
    c0 = jnp.zeros((bsz, 2, ML_HEADS, ML_QK, 2 * ML_V), F32)
    m0 = jnp.zeros((bsz, 2, 8, LANE), F32)
    _, _, _, _, s_ctx, c_ctx_state, m_ctx = mixer_scans(s1, ctx_row, n_ctx, s0, c0, m0)
    oh, om, hg, mo, _, _, _ = mixer_scans(x1, lat_row, GRID_W, s_ctx, c_ctx_state, m_ctx)

    x2 = _outproj(oh, om, hg, mo, x1, mods[:, 5:6], row(hg_norm_w[0]), row(ml_norm_w[0]),
                  w_out[0].astype(BF16))
    return _ffn(x2, mods[:, 6:9], lat_row, row(norm3_w[0]), ffn2_w1[0].astype(BF16),
                ffn2_w2[0].astype(BF16), row(final_norm_w), True)
```

```python
import functools

import numpy as np
import jax
import jax.numpy as jnp
from jax import lax
from jax.experimental import pallas as pl
from jax.experimental.pallas import tpu as pltpu

F32 = jnp.float32
BF16 = jnp.bfloat16

EPS = 1e-6
CHUNK = 64
GRID_W = 64
N_MOD = 9
HG_HEADS = 4
HG_DIM = 128
ML_HEADS = 4
ML_QK = 64
ML_V = 128
SUB = 16
N_SUB = CHUNK // SUB
EXP_CLAMP = 80.0
LANE = 128
VMEM_LIMIT = 56 * 1024 * 1024


def _cparams(sem):
    return pltpu.CompilerParams(dimension_semantics=sem, vmem_limit_bytes=VMEM_LIMIT)


def _dot(a, b):
    return jnp.dot(a, b, preferred_element_type=F32)


def _dot_nt(a, b):
    return lax.dot_general(a, b, (((1,), (1,)), ((), ())), preferred_element_type=F32)


def _dot_tn(a, b):
    return lax.dot_general(a, b, (((0,), (0,)), ((), ())), preferred_element_type=F32)


def _sigmoid(x):
    return 1.0 / (1.0 + jnp.exp(-x))


def _silu(x):
    return x * _sigmoid(x)


def _log_sigmoid(x):
    return jnp.minimum(x, 0.0) - jnp.log(1.0 + jnp.exp(-jnp.abs(x)))


def _split3(x):
    hi = x.astype(BF16)
    r = x - hi.astype(F32)
    mid = r.astype(BF16)
    lo = (r - mid.astype(F32)).astype(BF16)
    return hi, mid, lo


def _rms(x):
    return x * lax.rsqrt(jnp.mean(x * x, axis=-1, keepdims=True) + EPS)


def _mod_kernel(c_ref, w_ref, b_ref, o_ref):
    a = _silu(c_ref[...])
    a_hi = a.astype(BF16)
    a_lo = (a - a_hi.astype(F32)).astype(BF16)
    w = w_ref[...]
    w_hi = w.astype(BF16)
    w_lo = (w - w_hi.astype(F32)).astype(BF16)
    o_ref[...] = _dot(a_hi, w_hi) + _dot(a_hi, w_lo) + _dot(a_lo, w_hi) + b_ref[...]


def _modulation(cvec, w_mod, b_mod):
    rows, d = cvec.shape
    n = w_mod.shape[1]
    tn = 1024
    return pl.pallas_call(
        _mod_kernel,
        grid=(n // tn,),
        in_specs=[pl.BlockSpec((rows, d), lambda j: (0, 0)),
                  pl.BlockSpec((d, tn), lambda j: (0, j)),
                  pl.BlockSpec((1, tn), lambda j: (0, j))],
        out_specs=pl.BlockSpec((rows, tn), lambda j: (0, j)),
        out_shape=jax.ShapeDtypeStruct((rows, n), F32),
        compiler_params=_cparams(("arbitrary",)),
        name="modulation",
    )(cvec, w_mod, b_mod)


def _ffn_kernel(x_ref, mod_ref, nw_ref, w1_ref, w2_ref, fw_ref, o_ref, a_ref, *, d_ff, tf, final):
    x = x_ref[0]
    shift = mod_ref[0, 0:1, :]
    scale = mod_ref[0, 1:2, :]
    gate = mod_ref[0, 2:3, :]
    h = (_rms(x) * nw_ref[...] * (1.0 + scale) + shift).astype(BF16)
    for c in range(d_ff // tf):
        g = _dot(h, w1_ref[:, c * tf:(c + 1) * tf])
        u = _dot(h, w1_ref[:, d_ff + c * tf:d_ff + (c + 1) * tf])
        a_ref[:, c * tf:(c + 1) * tf] = (_silu(g) * u).astype(BF16)
    y = x + 0.5 * gate * _dot(a_ref[...], w2_ref[...])
    if final:
        y = _rms(y) * fw_ref[...]
    o_ref[0] = y


def _ffn(s, mods, mod_row, nw, w1, w2, fw, final):
    bsz, t, d = s.shape
    d_ff = w2.shape[0]
    tm = min(512, t)
    tf = 256
    const = lambda b, i: (0, 0)
    return pl.pallas_call(
        functools.partial(_ffn_kernel, d_ff=d_ff, tf=tf, final=final),
        grid=(bsz, t // tm),
        in_specs=[pl.BlockSpec((1, tm, d), lambda b, i: (b, i, 0)),
                  pl.BlockSpec((1, 3, d), lambda b, i: (mod_row(b), 0, 0)),
                  pl.BlockSpec((1, d), const),
                  pl.BlockSpec((d, 2 * d_ff), const, pipeline_mode=pl.Buffered(1)),
                  pl.BlockSpec((d_ff, d), const, pipeline_mode=pl.Buffered(1)),
                  pl.BlockSpec((1, d), const)],
        out_specs=pl.BlockSpec((1, tm, d), lambda b, i: (b, i, 0)),
        out_shape=jax.ShapeDtypeStruct((bsz, t, d), F32),
        scratch_shapes=[pltpu.VMEM((tm, d_ff), BF16)],
        compiler_params=_cparams(("parallel", "parallel")),
        name="ffn_final" if final else "ffn",
    )(s, mods, nw, w1, w2, fw)


def _inproj_kernel(x_ref, mod_ref, nw_ref, w_ref, wg_ref, gb_ref, lbl_ref,
                   hq_ref, hv_ref, hg_ref, hk_ref, hlf_ref, mqk_ref, mv_ref, mo_ref, gc_ref, *, hw):
    x = x_ref[0]
    shift = mod_ref[0, 0:1, :]
    scale = mod_ref[0, 1:2, :]
    h = (_rms(x) * nw_ref[...] * (1.0 + scale) + shift).astype(BF16)

    def proj(k):
        return _dot(h, w_ref[:, k * hw:(k + 1) * hw])

    hq_ref[0] = _silu(proj(0)) * (HG_DIM ** -0.5)
    hv_ref[0] = proj(1)
    hg_ref[0] = proj(2)
    lbl = lbl_ref[...]
    e = jnp.exp(lbl - jnp.max(lbl, axis=0, keepdims=True))
    lb = e[0] / jnp.sum(e, axis=0)
    for d in range(2):
        f_pre = proj(3 + d)
        lbd = lb[d:d + 1, :]
        hlf_ref[0, d] = jnp.log(lbd + (1.0 - lbd) * _sigmoid(f_pre))
        hk_ref[0, d] = (1.0 - lbd) * _sigmoid(-f_pre)
    mqk_ref[0] = proj(5)
    mv_ref[0] = proj(6)
    mo_ref[0] = proj(7)
    ng = gc_ref.shape[-1]
    gc_ref[0] = _dot(h, wg_ref[...])[:, :ng] + gb_ref[...]


def _inproj(s, mods, mod_row, nw, w_main, w_gate, gate_b, lb_logits):
    bsz, t, d = s.shape
    hw = HG_HEADS * HG_DIM
    ng = gate_b.shape[-1]
    tm = min(512, t)
    const = lambda b, i: (0, 0)
    tok = lambda b, i: (b, i, 0)
    tok2 = lambda b, i: (b, 0, i, 0)
    f = lambda *shape: jax.ShapeDtypeStruct(shape, F32)
    return pl.pallas_call(
        functools.partial(_inproj_kernel, hw=hw),
        grid=(bsz, t // tm),
        in_specs=[pl.BlockSpec((1, tm, d), tok),
                  pl.BlockSpec((1, 2, d), lambda b, i: (mod_row(b), 0, 0)),
                  pl.BlockSpec((1, d), const),
                  pl.BlockSpec(w_main.shape, const, pipeline_mode=pl.Buffered(1)),
                  pl.BlockSpec(w_gate.shape, const, pipeline_mode=pl.Buffered(1)),
                  pl.BlockSpec((1, ng), const),
                  pl.BlockSpec(lb_logits.shape, lambda b, i: (0, 0, 0))],
        out_specs=[pl.BlockSpec((1, tm, hw), tok), pl.BlockSpec((1, tm, hw), tok),
                   pl.BlockSpec((1, tm, hw), tok),
                   pl.BlockSpec((1, 2, tm, hw), tok2), pl.BlockSpec((1, 2, tm, hw), tok2),
                   pl.BlockSpec((1, tm, hw), tok), pl.BlockSpec((1, tm, hw), tok),
                   pl.BlockSpec((1, tm, hw), tok), pl.BlockSpec((1, tm, ng), tok)],
        out_shape=[f(bsz, t, hw), f(bsz, t, hw), f(bsz, t, hw),
                   f(bsz, 2, t, hw), f(bsz, 2, t, hw),
                   f(bsz, t, hw), f(bsz, t, hw), f(bsz, t, hw), f(bsz, t, ng)],
        compiler_params=_cparams(("parallel", "parallel")),
        name="inproj",
    )(s, mods, nw, w_main, w_gate, gate_b, lb_logits)


def _conv_kernel(x_ref, w_ref, b_ref, o_ref, *, width, q_tiles):
    x = x_ref[0]
    t = x.shape[0]
    col = lax.broadcasted_iota(jnp.int32, x.shape, 0) % width
    xl = jnp.where(col == 0, 0.0, pltpu.roll(x, 1, 0))
    xr = jnp.where(col == width - 1, 0.0, pltpu.roll(x, t - 1, 0))

    def row(di):
        return (w_ref[di, 0:1, :] * xl + w_ref[di, 1:2, :] * x + w_ref[di, 2:3, :] * xr)

    y = row(1)
    if t > width:
        pad = jnp.zeros((width, x.shape[1]), F32)
        y = y + jnp.concatenate([pad, row(0)[:t - width]], axis=0)
        y = y + jnp.concatenate([row(2)[width:], pad], axis=0)
    y = _silu(y + b_ref[...])
    qscale = jnp.where(pl.program_id(1) < q_tiles, ML_QK ** -0.5, 1.0)
    o_ref[0] = y * qscale


def _conv(x, w, b, width):
    bsz, t, ch = x.shape
    return pl.pallas_call(
        functools.partial(_conv_kernel, width=width, q_tiles=ML_HEADS * ML_QK // LANE),
        grid=(bsz, ch // LANE),
        in_specs=[pl.BlockSpec((1, t, LANE), lambda b_, c: (b_, 0, c)),
                  pl.BlockSpec((3, 3, LANE), lambda b_, c: (0, 0, c)),
                  pl.BlockSpec((1, LANE), lambda b_, c: (0, c))],
        out_specs=pl.BlockSpec((1, t, LANE), lambda b_, c: (b_, 0, c)),
        out_shape=jax.ShapeDtypeStruct((bsz, t, ch), F32),
        compiler_params=_cparams(("parallel", "parallel")),
        name="conv",
    )(x, w, b)


def _positions():
    t = np.arange(CHUNK)
    return [t, CHUNK - 1 - t]


def _hgrn2_masks():
    ds, ws = [], []
    for p in _positions():
        le = (p[None, :] <= p[:, None]).astype(np.float32)
        mid = [SUB * (p[SUB * j] // SUB) + SUB // 2 - 1 for j in range(N_SUB)]
        ref = np.zeros((8, CHUNK), np.float32)
        for j in range(N_SUB):
            ref[j] = p <= mid[j]
        ref[N_SUB] = 1.0
        ds.append(np.concatenate([le, ref], axis=0))
        tok_blk = np.arange(CHUNK)[:, None] // SUB
        ws.append(np.concatenate([(tok_blk == j) * le for j in range(N_SUB)], axis=1))
    return np.stack(ds).astype(np.float32), np.stack(ws).astype(np.float32)


def _mlstm_masks():
    le2s, let2s = [], []
    for p in _positions():
        le = (p[None, :] <= p[:, None]).astype(np.float32)
        le2s.append(np.concatenate([le, le], axis=1))
        let2 = np.zeros((2 * CHUNK, 2 * CHUNK), np.float32)
        let2[:CHUNK, :CHUNK] = le.T
        let2[CHUNK:, CHUNK:] = le.T
        let2s.append(let2)
    return np.stack(le2s), np.stack(let2s)


def _chunk_rows(c, nc, direction):
    cc = jnp.where(direction == 0, c, nc - 1 - c)
    return pl.multiple_of(cc * CHUNK, CHUNK)


def _hgrn2_kernel(q_ref, v_ref, k_ref, lf_ref, d_ref, w_ref, s0_ref, o_ref, st_ref,
                  u_ref, qin_ref, g_ref, *, nc):
    direction = pl.program_id(1)

    @pl.when(pl.program_id(2) == 0)
    def _():
        st_ref[...] = s0_ref[...]

    dmat = d_ref[0].astype(BF16)
    wmask = w_ref[0] > 0.0
    L = CHUNK
    hw = q_ref.shape[-1]

    def intra(c, carry):
        rows = pl.ds(_chunk_rows(c, nc, direction), L)
        q = q_ref[0, rows, :]
        k = k_ref[0, 0, rows, :]
        vb = v_ref[0, rows, :].astype(BF16)
        hi, mid, lo = _split3(lf_ref[0, 0, rows, :])
        e = _dot(dmat, hi) + _dot(dmat, mid) + _dot(dmat, lo)
        b = e[0:L]
        b_tot = e[L + N_SUB:L + N_SUB + 1]
        refs = [e[L + j:L + j + 1] for j in range(N_SUB)]
        ref_rows = jnp.concatenate([jnp.broadcast_to(r, (SUB, hw)) for r in refs], axis=0)
        q_mid = (q * jnp.exp(jnp.minimum(b - ref_rows, EXP_CLAMP))).astype(BF16)
        k_mid = jnp.concatenate(
            [(k * jnp.exp(jnp.minimum(r - b, EXP_CLAMP))).astype(BF16) for r in refs], axis=0)
        k_out = (k * jnp.exp(b_tot - b)).astype(BF16)
        v4 = jnp.concatenate([vb] * N_SUB, axis=0)
        outs = []
        for h in range(HG_HEADS):
            ln = slice(h * HG_DIM, (h + 1) * HG_DIM)
            scores = _dot_nt(q_mid[:, ln], k_mid[:, ln])
            a = jnp.where(wmask, scores, 0.0).astype(BF16)
            outs.append(_dot(a, v4[:, ln]))
            u_ref[c, h] = _dot_tn(vb[:, ln], k_out[:, ln])
        o_ref[0, 0, rows, :] = jnp.concatenate(outs, axis=1)
        qin_ref[c] = (q * jnp.exp(b)).astype(BF16)
        g_ref[c] = jnp.broadcast_to(jnp.exp(b_tot), (8, hw))
        return carry

    def carry_state(c, carry):
        rows = pl.ds(_chunk_rows(c, nc, direction), L)
        q_in = qin_ref[c]
        g = g_ref[c][0:1]
        outs, states = [], []
        for h in range(HG_HEADS):
            ln = slice(h * HG_DIM, (h + 1) * HG_DIM)
            st = st_ref[0, 0, h]
            outs.append(_dot_nt(q_in[:, ln], st.astype(BF16)))
            states.append(g[:, ln] * st + u_ref[c, h])
        o_ref[0, 0, rows, :] += jnp.concatenate(outs, axis=1)
        for h in range(HG_HEADS):
            st_ref[0, 0, h] = states[h]
        return carry

    lax.fori_loop(0, nc, intra, 0, unroll=2)
    lax.fori_loop(0, nc, carry_state, 0, unroll=2)


def _hgrn2(q, v, k, lf, s0, dmat, wmask):
    bsz, t, hw = q.shape
    nc = min(8, t // CHUNK)
    tb = nc * CHUNK
    nblk = t // tb

    def blk(j, d):
        return jnp.where(d == 0, j, nblk - 1 - j)

    st_spec = pl.BlockSpec((1, 1) + s0.shape[2:], lambda b, d, j: (b, d, 0, 0, 0))
    return pl.pallas_call(
        functools.partial(_hgrn2_kernel, nc=nc),
        grid=(bsz, 2, nblk),
        in_specs=[pl.BlockSpec((1, tb, hw), lambda b, d, j: (b, blk(j, d), 0)),
                  pl.BlockSpec((1, tb, hw), lambda b, d, j: (b, blk(j, d), 0)),
                  pl.BlockSpec((1, 1, tb, hw), lambda b, d, j: (b, d, blk(j, d), 0)),
                  pl.BlockSpec((1, 1, tb, hw), lambda b, d, j: (b, d, blk(j, d), 0)),
                  pl.BlockSpec((1,) + dmat.shape[1:], lambda b, d, j: (d, 0, 0)),
                  pl.BlockSpec((1,) + wmask.shape[1:], lambda b, d, j: (d, 0, 0)),
                  st_spec],
        out_specs=[pl.BlockSpec((1, 1, tb, hw), lambda b, d, j: (b, d, blk(j, d), 0)), st_spec],
        out_shape=[jax.ShapeDtypeStruct((bsz, 2, t, hw), F32),
                   jax.ShapeDtypeStruct(s0.shape, F32)],
        scratch_shapes=[pltpu.VMEM((nc,) + s0.shape[2:], F32),
                        pltpu.VMEM((nc, CHUNK, hw), BF16),
                        pltpu.VMEM((nc, 8, hw), F32)],
        compiler_params=_cparams(("parallel", "parallel", "arbitrary")),
        name="hgrn2_scan",
    )(q, v, k, lf, dmat, wmask, s0)


def _mlstm_kernel(qk_ref, v_ref, kt_ref, g_ref, le2_ref, let2_ref, c0_ref, m0_ref,
                  o_ref, c_ref, mm_ref,
                  p_ref, u_ref, qs_ref, ct_ref, mloc_ref, sc_ref, *, nc):
    direction = pl.program_id(1)

    @pl.when(pl.program_id(2) == 0)
    def _():
        c_ref[...] = c0_ref[...]
        mm_ref[...] = m0_ref[...]

    le2 = le2_ref[0].astype(BF16)
    let2 = let2_ref[0].astype(BF16)
    causal = le2_ref[0] > 0.0
    L = CHUNK
    neg = -jnp.inf
    lane = lax.broadcasted_iota(jnp.int32, (1, LANE), 1)
    seg = [lane < L, lane >= L]
    lane_t = lax.broadcasted_iota(jnp.int32, (L, LANE), 1) < L
    sub_pair = lax.broadcasted_iota(jnp.int32, (2 * L, LANE), 0) < L
    bd_mask = sub_pair == (lax.broadcasted_iota(jnp.int32, (2 * L, LANE), 1) < L)
    ones_v = jnp.ones((L, ML_V), BF16)
    zpad = jnp.zeros((6, LANE), F32)

    def intra(c, carry):
        cc = jnp.where(direction == 0, c, nc - 1 - c)
        rows = pl.ds(pl.multiple_of(cc * L, L), L)
        g_all = g_ref[0, cc]
        g = jnp.where(direction == 0, g_all[0:4], g_all[4:8])
        i2 = g[0:2]
        lf2 = _log_sigmoid(g[2:4])
        parts = _split3(jnp.concatenate([lf2, zpad], axis=0))
        b2 = sum(_dot(x, let2) for x in parts)[0:2]
        r2 = i2 - b2

        def head_rows(x):
            return jnp.concatenate(
                [jnp.broadcast_to(jnp.where(seg[h % 2], x[h // 2:h // 2 + 1], jnp.zeros_like(x[0:1])),
                                  (LANE, LANE)) for h in range(ML_HEADS)], axis=0)

        c_t = sum(_dot_nt(le2, head_rows(x)) for x in parts)
        ct_ref[c] = c_t
        qk = qk_ref[0, rows, :]
        vb = v_ref[0, rows, :].astype(BF16)
        b_tots, g_maxs = [], []
        for p in range(ML_HEADS // 2):
            ha, hb = 2 * p, 2 * p + 1
            r_p = r2[p:p + 1]
            t1 = jnp.where(lane_t, c_t[:, ha * ML_V:(ha + 1) * ML_V], c_t[:, hb * ML_V:(hb + 1) * ML_V])
            d_log = jnp.where(causal, t1 + r_p, neg)
            m_a = jnp.max(jnp.where(lane_t, d_log, neg), axis=1, keepdims=True)
            m_b = jnp.max(jnp.where(lane_t, neg, d_log), axis=1, keepdims=True)
            mloc_ref[c, :, ha * ML_V:(ha + 1) * ML_V] = jnp.broadcast_to(m_a, (L, ML_V))
            mloc_ref[c, :, hb * ML_V:(hb + 1) * ML_V] = jnp.broadcast_to(m_b, (L, ML_V))
            w = jnp.exp(d_log - jnp.where(lane_t, m_a, m_b))
            q_p = qk[:, p * LANE:(p + 1) * LANE].astype(BF16)
            bd = jnp.where(bd_mask, kt_ref[0, cc, p], 0.0)
            a = (_dot(q_p, bd.astype(BF16)) * w).astype(BF16)
            zero = jnp.zeros_like(a)
            a_stack = jnp.concatenate([jnp.where(lane_t, a, zero), jnp.where(lane_t, zero, a)], axis=0)
            v_stack = jnp.concatenate(
                [jnp.concatenate([vb[:, ha * ML_V:(ha + 1) * ML_V], ones_v], axis=1),
                 jnp.concatenate([vb[:, hb * ML_V:(hb + 1) * ML_V], ones_v], axis=1)], axis=0)
            p_ref[c, p] = _dot(a_stack, v_stack)
            zq = jnp.zeros_like(q_p)
            qs_ref[c, p] = jnp.concatenate([jnp.where(lane_t, q_p, zq), jnp.where(lane_t, zq, q_p)], axis=0)
            g_a = jnp.max(jnp.where(seg[0], r_p, neg), axis=1, keepdims=True)
            g_b = jnp.max(jnp.where(seg[1], r_p, neg), axis=1, keepdims=True)
            e_w = jnp.exp(r_p - jnp.where(seg[0], g_a, g_b))
            u_ref[c, p] = _dot((bd * e_w).astype(BF16), v_stack)
            bt_a = jnp.sum(jnp.where(seg[0], lf2[p:p + 1], 0.0), axis=1, keepdims=True)
            bt_b = jnp.sum(jnp.where(seg[1], lf2[p:p + 1], 0.0), axis=1, keepdims=True)
            b_tots += [bt_a, bt_b]
            g_maxs += [g_a, g_b]
        sc_ref[c] = jnp.concatenate([jnp.broadcast_to(x, (1, LANE)) for x in b_tots + g_maxs], axis=0)
        return carry

    def carry_state(c, carry):
        cc = jnp.where(direction == 0, c, nc - 1 - c)
        rows = pl.ds(pl.multiple_of(cc * L, L), L)
        m_all = mm_ref[0, 0]
        sc = sc_ref[c]
        c_t = ct_ref[c]
        m_loc = mloc_ref[c]
        outs, states, m_rows = [], [], []
        for p in range(ML_HEADS // 2):
            c_pair = c_ref[0, 0, p]
            inter = _dot(qs_ref[c, p], c_pair.astype(BF16))
            intra_p = p_ref[c, p]
            inc = u_ref[c, p]
            new_rows = []
            for hh in range(2):
                h = 2 * p + hh
                rs = slice(hh * L, (hh + 1) * L)
                ln = slice(h * ML_V, (h + 1) * ML_V)
                m_in = m_all[h:h + 1]
                m_t = jnp.maximum(m_loc[:, ln], c_t[:, ln] + m_in)
                alpha = jnp.exp(m_loc[:, ln] - m_t)
                beta = jnp.exp(c_t[:, ln] + m_in - m_t)
                num = alpha * intra_p[rs, :ML_V] + beta * inter[rs, :ML_V]
                den = alpha * intra_p[rs, ML_V:] + beta * inter[rs, ML_V:]
                outs.append(num / jnp.maximum(jnp.abs(den), jnp.exp(-m_t)))
                b_tot = sc[h:h + 1]
                m_new = b_tot + jnp.maximum(m_in, sc[ML_HEADS + h:ML_HEADS + h + 1])
                carry_w = jnp.exp(b_tot + m_in - m_new)
                inc_w = jnp.exp(b_tot + sc[ML_HEADS + h:ML_HEADS + h + 1] - m_new)
                cw2 = jnp.concatenate([carry_w, carry_w], axis=1)
                iw2 = jnp.concatenate([inc_w, inc_w], axis=1)
                new_rows.append(cw2 * c_pair[rs] + iw2 * inc[rs])
                m_rows.append(m_new)
            states.append(jnp.concatenate(new_rows, axis=0))
        o_ref[0, 0, rows, :] = jnp.concatenate(outs, axis=1)
        for p in range(ML_HEADS // 2):
            c_ref[0, 0, p] = states[p]
        mm_ref[0, 0] = jnp.concatenate(m_rows + [m_all[ML_HEADS:]], axis=0)
        return carry

    lax.fori_loop(0, nc, intra, 0, unroll=2)
    lax.fori_loop(0, nc, carry_state, 0, unroll=2)


def _mlstm(qk, v, kt2, gates, c0, m0, le2, let2):
    bsz, t, hw = v.shape
    nc = min(8, t // CHUNK)
    tb = nc * CHUNK
    nblk = t // tb
    npair = ML_HEADS // 2

    def blk(j, d):
        return jnp.where(d == 0, j, nblk - 1 - j)

    c_spec = pl.BlockSpec((1, 1) + c0.shape[2:], lambda b, d, j: (b, d, 0, 0, 0))
    m_spec = pl.BlockSpec((1, 1) + m0.shape[2:], lambda b, d, j: (b, d, 0, 0))
    return pl.pallas_call(
        functools.partial(_mlstm_kernel, nc=nc),
        grid=(bsz, 2, nblk),
        in_specs=[pl.BlockSpec((1, tb, qk.shape[-1]), lambda b, d, j: (b, blk(j, d), 0)),
                  pl.BlockSpec((1, tb, hw), lambda b, d, j: (b, blk(j, d), 0)),
                  pl.BlockSpec((1, nc) + kt2.shape[2:], lambda b, d, j: (b, blk(j, d), 0, 0, 0)),
                  pl.BlockSpec((1, nc) + gates.shape[2:], lambda b, d, j: (b, blk(j, d), 0, 0)),
                  pl.BlockSpec((1,) + le2.shape[1:], lambda b, d, j: (d, 0, 0)),
                  pl.BlockSpec((1,) + let2.shape[1:], lambda b, d, j: (d, 0, 0)),
                  c_spec, m_spec],
        out_specs=[pl.BlockSpec((1, 1, tb, hw), lambda b, d, j: (b, d, blk(j, d), 0)), c_spec, m_spec],
        out_shape=[jax.ShapeDtypeStruct((bsz, 2, t, hw), F32),
                   jax.ShapeDtypeStruct(c0.shape, F32),
                   jax.ShapeDtypeStruct(m0.shape, F32)],
        scratch_shapes=[pltpu.VMEM((nc, npair, 2 * CHUNK, 2 * ML_V), F32),
                        pltpu.VMEM((nc, npair, 2 * ML_QK, 2 * ML_V), F32),
                        pltpu.VMEM((nc, npair, 2 * CHUNK, LANE), BF16),
                        pltpu.VMEM((nc, CHUNK, hw), F32),
                        pltpu.VMEM((nc, CHUNK, hw), F32),
                        pltpu.VMEM((nc, 8, LANE), F32)],
        compiler_params=_cparams(("parallel", "parallel", "arbitrary")),
        name="mlstm_scan",
    )(qk, v, kt2, gates, le2, let2, c0, m0)


def _head_rms(x, heads):
    dh = x.shape[-1] // heads
    return jnp.concatenate([_rms(x[:, h * dh:(h + 1) * dh]) for h in range(heads)], axis=1)


def _outproj_kernel(oh_ref, om_ref, hg_ref, mo_ref, x_ref, mod_ref, hnw_ref, mnw_ref, w_ref, o_ref):
    oh = oh_ref[0, 0] + oh_ref[0, 1]
    om = om_ref[0, 0] + om_ref[0, 1]
    hg_out = _head_rms(oh, HG_HEADS) * hnw_ref[...] * _silu(hg_ref[0])
    ml_out = _sigmoid(mo_ref[0]) * (_head_rms(om, ML_HEADS) * mnw_ref[...])
    merged = jnp.concatenate([hg_out, ml_out], axis=1).astype(BF16)
    o_ref[0] = x_ref[0] + mod_ref[0] * _dot(merged, w_ref[...])


def _outproj(oh, om, hg, mo, x, mod5, hnw, mnw, w_out):
    bsz, t, d = x.shape
    hw = hg.shape[-1]
    tm = min(512, t)
    tok = lambda b, i: (b, i, 0)
    tok2 = lambda b, i: (b, 0, i, 0)
    const = lambda b, i: (0, 0)
    return pl.pallas_call(
        _outproj_kernel,
        grid=(bsz, t // tm),
        in_specs=[pl.BlockSpec((1, 2, tm, hw), tok2), pl.BlockSpec((1, 2, tm, hw), tok2),
                  pl.BlockSpec((1, tm, hw), tok), pl.BlockSpec((1, tm, hw), tok),
                  pl.BlockSpec((1, tm, d), tok),
                  pl.BlockSpec((1, 1, d), lambda b, i: (b, 0, 0)),
                  pl.BlockSpec((1, hw), const), pl.BlockSpec((1, hw), const),
                  pl.BlockSpec(w_out.shape, const, pipeline_mode=pl.Buffered(1))],
        out_specs=pl.BlockSpec((1, tm, d), tok),
        out_shape=jax.ShapeDtypeStruct((bsz, t, d), F32),
        compiler_params=_cparams(("parallel", "parallel")),
        name="outproj",
    )(oh, om, hg, mo, x, mod5, hnw, mnw, w_out)


def kernel(x, c, ctx, c_ctx, w_mod, b_mod, norm1_w, ffn1_w1, ffn1_w2, norm2_w, w_in, ml_gate_b,
           ml_conv_w, ml_conv_b, hg_lb_logits, hg_norm_w, ml_norm_w, w_out, norm3_w, ffn2_w1, ffn2_w2,
           final_norm_w):
    bsz, seq, d = x.shape
    n_ctx = ctx.shape[1]
    assert w_mod.shape[0] == 1, "single-layer kernel"
    assert seq % (GRID_W * 8) == 0 and n_ctx % CHUNK == 0
    hw = HG_HEADS * HG_DIM
    ng = ml_gate_b.shape[-1]

    rows = -(-(bsz + 1) // 8) * 8
    cvec = jnp.zeros((rows, d), F32).at[:bsz].set(c).at[bsz].set(c_ctx)
    mods = _modulation(cvec, w_mod[0], b_mod[0][None, :]).reshape(rows, N_MOD, d)
    lat_row = lambda b: b
    ctx_row = lambda b: bsz

    row = lambda a: a.reshape(1, -1)
    w1a, w2a = ffn1_w1[0].astype(BF16), ffn1_w2[0].astype(BF16)
    x1 = _ffn(x, mods[:, 0:3], lat_row, row(norm1_w[0]), w1a, w2a, row(final_norm_w), False)
    s1 = _ffn(ctx, mods[:, 0:3], ctx_row, row(norm1_w[0]), w1a, w2a, row(final_norm_w), False)

    w_main = w_in[0][:, :8 * hw].astype(BF16)
    w_gate = jnp.zeros((d, LANE), BF16).at[:, :ng].set(w_in[0][:, 8 * hw:].astype(BF16))
    gate_b = row(ml_gate_b[0])
    dmat, wmask = (jnp.asarray(a) for a in _hgrn2_masks())
    le2, let2 = (jnp.asarray(a) for a in _mlstm_masks())
    conv_w, conv_b = ml_conv_w[0], row(ml_conv_b[0])
    npair = ML_HEADS // 2

    def mixer_scans(s, mod_row, width, s0, c0, m0):
        hq, hv, hg, hk, hlf, mqk, mv, mo, gcol = _inproj(
            s, mods[:, 3:5], mod_row, row(norm2_w[0]), w_main, w_gate, gate_b, hg_lb_logits)
        n_chunks = s.shape[1] // CHUNK
        qk = _conv(mqk, conv_w, conv_b, width)
        gates = gcol.reshape(bsz, n_chunks, CHUNK, 2, 2, npair, 2).transpose(0, 1, 3, 4, 5, 6, 2)
        gates = gates.reshape(bsz, n_chunks, 8, 2 * CHUNK)
        kt = qk[:, :, ML_HEADS * ML_QK:].reshape(bsz, n_chunks, CHUNK, npair, 2 * ML_QK)
        kt = kt.transpose(0, 1, 3, 4, 2)
        kt2 = jnp.concatenate([kt, kt], axis=-1)
        oh, s_fin = _hgrn2(hq, hv, hk, hlf, s0, dmat, wmask)
        om, c_fin, m_fin = _mlstm(qk, mv, kt2, gates, c0, m0, le2, let2)
        return oh, om, hg, mo, s_fin, c_fin, m_fin

    s0 = jnp.zeros((bsz, 2, HG_HEADS, HG_DIM, HG_DIM), F32)
    c0 = jnp.zeros((bsz, 2, npair, 2 * ML_QK, 2 * ML_V), F32)
    m0 = jnp.zeros((bsz, 2, 8, LANE), F32)
    _, _, _, _, s_ctx, c_ctx_state, m_ctx = mixer_scans(s1, ctx_row, n_ctx, s0, c0, m0)
    oh, om, hg, mo, _, _, _ = mixer_scans(x1, lat_row, GRID_W, s_ctx, c_ctx_state, m_ctx)

    x2 = _outproj(oh, om, hg, mo, x1, mods[:, 5:6], row(hg_norm_w[0]), row(ml_norm_w[0]),
                  w_out[0].astype(BF16))
    return _ffn(x2, mods[:, 6:9], lat_row, row(norm3_w[0]), ffn2_w1[0].astype(BF16),
                ffn2_w2[0].astype(BF16), row(final_norm_w), True)
```

```python
import functools

import numpy as np
import jax
import jax.numpy as jnp
from jax import lax
from jax.experimental import pallas as pl
from jax.experimental.pallas import tpu as pltpu

F32 = jnp.float32
BF16 = jnp.bfloat16

EPS = 1e-6
CHUNK = 64
GRID_W = 64
N_MOD = 9
HG_HEADS = 4
HG_DIM = 128
ML_HEADS = 4
ML_QK = 64
ML_V = 128
SUB = 16
N_SUB = CHUNK // SUB
GROUP = 4
EXP_CLAMP = 80.0
LANE = 128
VMEM_LIMIT = 56 * 1024 * 1024


def _cparams(sem):
    return pltpu.CompilerParams(dimension_semantics=sem, vmem_limit_bytes=VMEM_LIMIT)


def _dot(a, b):
    return jnp.dot(a, b, preferred_element_type=F32)


def _dot_nt(a, b):
    return lax.dot_general(a, b, (((1,), (1,)), ((), ())), preferred_element_type=F32)


def _dot_tn(a, b):
    return lax.dot_general(a, b, (((0,), (0,)), ((), ())), preferred_element_type=F32)


def _sigmoid(x):
    return 1.0 / (1.0 + jnp.exp(-x))


def _silu(x):
    return x * _sigmoid(x)


def _log_sigmoid(x):
    return jnp.minimum(x, 0.0) - jnp.log(1.0 + jnp.exp(-jnp.abs(x)))


def _split3(x):
    hi = x.astype(BF16)
    r = x - hi.astype(F32)
    mid = r.astype(BF16)
    lo = (r - mid.astype(F32)).astype(BF16)
    return hi, mid, lo


def _rms(x):
    return x * lax.rsqrt(jnp.mean(x * x, axis=-1, keepdims=True) + EPS)


def _mod_kernel(c_ref, w_ref, b_ref, o_ref):
    a = _silu(c_ref[...])
    a_hi = a.astype(BF16)
    a_lo = (a - a_hi.astype(F32)).astype(BF16)
    w = w_ref[...]
    w_hi = w.astype(BF16)
    w_lo = (w - w_hi.astype(F32)).astype(BF16)
    o_ref[...] = _dot(a_hi, w_hi) + _dot(a_hi, w_lo) + _dot(a_lo, w_hi) + b_ref[...]


def _modulation(cvec, w_mod, b_mod):
    rows, d = cvec.shape
    n = w_mod.shape[1]
    tn = 1024
    return pl.pallas_call(
        _mod_kernel,
        grid=(n // tn,),
        in_specs=[pl.BlockSpec((rows, d), lambda j: (0, 0)),
                  pl.BlockSpec((d, tn), lambda j: (0, j)),
                  pl.BlockSpec((1, tn), lambda j: (0, j))],
        out_specs=pl.BlockSpec((rows, tn), lambda j: (0, j)),
        out_shape=jax.ShapeDtypeStruct((rows, n), F32),
        compiler_params=_cparams(("arbitrary",)),
        name="modulation",
    )(cvec, w_mod, b_mod)


def _ffn_kernel(x_ref, mod_ref, nw_ref, w1_ref, w2_ref, fw_ref, o_ref, a_ref, *, d_ff, tf, final):
    x = x_ref[0]
    shift = mod_ref[0, 0:1, :]
    scale = mod_ref[0, 1:2, :]
    gate = mod_ref[0, 2:3, :]
    h = (_rms(x) * nw_ref[...] * (1.0 + scale) + shift).astype(BF16)
    for c in range(d_ff // tf):
        g = _dot(h, w1_ref[:, c * tf:(c + 1) * tf])
        u = _dot(h, w1_ref[:, d_ff + c * tf:d_ff + (c + 1) * tf])
        a_ref[:, c * tf:(c + 1) * tf] = (_silu(g) * u).astype(BF16)
    y = x + 0.5 * gate * _dot(a_ref[...], w2_ref[...])
    if final:
        y = _rms(y) * fw_ref[...]
    o_ref[0] = y


def _ffn(s, mods, mod_row, nw, w1, w2, fw, final):
    bsz, t, d = s.shape
    d_ff = w2.shape[0]
    tm = min(512, t)
    tf = 256
    const = lambda b, i: (0, 0)
    return pl.pallas_call(
        functools.partial(_ffn_kernel, d_ff=d_ff, tf=tf, final=final),
        grid=(bsz, t // tm),
        in_specs=[pl.BlockSpec((1, tm, d), lambda b, i: (b, i, 0)),
                  pl.BlockSpec((1, 3, d), lambda b, i: (mod_row(b), 0, 0)),
                  pl.BlockSpec((1, d), const),
                  pl.BlockSpec((d, 2 * d_ff), const, pipeline_mode=pl.Buffered(1)),
                  pl.BlockSpec((d_ff, d), const, pipeline_mode=pl.Buffered(1)),
                  pl.BlockSpec((1, d), const)],
        out_specs=pl.BlockSpec((1, tm, d), lambda b, i: (b, i, 0)),
        out_shape=jax.ShapeDtypeStruct((bsz, t, d), F32),
        scratch_shapes=[pltpu.VMEM((tm, d_ff), BF16)],
        compiler_params=_cparams(("parallel", "parallel")),
        name="ffn_final" if final else "ffn",
    )(s, mods, nw, w1, w2, fw)


def _inproj_kernel(x_ref, mod_ref, nw_ref, w_ref, wg_ref, gb_ref, lbl_ref,
                   hq_ref, hv_ref, hg_ref, hk_ref, hlf_ref, mqk_ref, mv_ref, mo_ref, gc_ref, *, hw):
    x = x_ref[0]
    shift = mod_ref[0, 0:1, :]
    scale = mod_ref[0, 1:2, :]
    h = (_rms(x) * nw_ref[...] * (1.0 + scale) + shift).astype(BF16)

    def proj(k):
        return _dot(h, w_ref[:, k * hw:(k + 1) * hw])

    hq_ref[0] = _silu(proj(0)) * (HG_DIM ** -0.5)
    hv_ref[0] = proj(1)
    hg_ref[0] = proj(2)
    lbl = lbl_ref[...]
    e = jnp.exp(lbl - jnp.max(lbl, axis=0, keepdims=True))
    lb = e[0] / jnp.sum(e, axis=0)
    for d in range(2):
        f_pre = proj(3 + d)
        lbd = lb[d:d + 1, :]
        hlf_ref[0, d] = jnp.log(lbd + (1.0 - lbd) * _sigmoid(f_pre))
        hk_ref[0, d] = (1.0 - lbd) * _sigmoid(-f_pre)
    mqk_ref[0] = proj(5)
    mv_ref[0] = proj(6)
    mo_ref[0] = proj(7)
    ng = gc_ref.shape[-1]
    gc_ref[0] = _dot(h, wg_ref[...])[:, :ng] + gb_ref[...]


def _inproj(s, mods, mod_row, nw, w_main, w_gate, gate_b, lb_logits):
    bsz, t, d = s.shape
    hw = HG_HEADS * HG_DIM
    ng = gate_b.shape[-1]
    tm = min(512, t)
    const = lambda b, i: (0, 0)
    tok = lambda b, i: (b, i, 0)
    tok2 = lambda b, i: (b, 0, i, 0)
    f = lambda *shape: jax.ShapeDtypeStruct(shape, F32)
    return pl.pallas_call(
        functools.partial(_inproj_kernel, hw=hw),
        grid=(bsz, t // tm),
        in_specs=[pl.BlockSpec((1, tm, d), tok),
                  pl.BlockSpec((1, 2, d), lambda b, i: (mod_row(b), 0, 0)),
                  pl.BlockSpec((1, d), const),
                  pl.BlockSpec(w_main.shape, const, pipeline_mode=pl.Buffered(1)),
                  pl.BlockSpec(w_gate.shape, const, pipeline_mode=pl.Buffered(1)),
                  pl.BlockSpec((1, ng), const),
                  pl.BlockSpec(lb_logits.shape, lambda b, i: (0, 0, 0))],
        out_specs=[pl.BlockSpec((1, tm, hw), tok), pl.BlockSpec((1, tm, hw), tok),
                   pl.BlockSpec((1, tm, hw), tok),
                   pl.BlockSpec((1, 2, tm, hw), tok2), pl.BlockSpec((1, 2, tm, hw), tok2),
                   pl.BlockSpec((1, tm, hw), tok), pl.BlockSpec((1, tm, hw), tok),
                   pl.BlockSpec((1, tm, hw), tok), pl.BlockSpec((1, tm, ng), tok)],
        out_shape=[f(bsz, t, hw), f(bsz, t, hw), f(bsz, t, hw),
                   f(bsz, 2, t, hw), f(bsz, 2, t, hw),
                   f(bsz, t, hw), f(bsz, t, hw), f(bsz, t, hw), f(bsz, t, ng)],
        compiler_params=_cparams(("parallel", "parallel")),
        name="inproj",
    )(s, mods, nw, w_main, w_gate, gate_b, lb_logits)


def _conv_kernel(x_ref, w_ref, b_ref, o_ref, *, width, q_tiles):
    x = x_ref[0]
    t = x.shape[0]
    col = lax.broadcasted_iota(jnp.int32, x.shape, 0) % width
    xl = jnp.where(col == 0, 0.0, pltpu.roll(x, 1, 0))
    xr = jnp.where(col == width - 1, 0.0, pltpu.roll(x, t - 1, 0))

    def row(di):
        return (w_ref[di, 0:1, :] * xl + w_ref[di, 1:2, :] * x + w_ref[di, 2:3, :] * xr)

    y = row(1)
    if t > width:
        pad = jnp.zeros((width, x.shape[1]), F32)
        y = y + jnp.concatenate([pad, row(0)[:t - width]], axis=0)
        y = y + jnp.concatenate([row(2)[width:], pad], axis=0)
    y = _silu(y + b_ref[...])
    qscale = jnp.where(pl.program_id(1) < q_tiles, ML_QK ** -0.5, 1.0)
    o_ref[0] = y * qscale


def _conv(x, w, b, width):
    bsz, t, ch = x.shape
    return pl.pallas_call(
        functools.partial(_conv_kernel, width=width, q_tiles=ML_HEADS * ML_QK // LANE),
        grid=(bsz, ch // LANE),
        in_specs=[pl.BlockSpec((1, t, LANE), lambda b_, c: (b_, 0, c)),
                  pl.BlockSpec((3, 3, LANE), lambda b_, c: (0, 0, c)),
                  pl.BlockSpec((1, LANE), lambda b_, c: (0, c))],
        out_specs=pl.BlockSpec((1, t, LANE), lambda b_, c: (b_, 0, c)),
        out_shape=jax.ShapeDtypeStruct((bsz, t, ch), F32),
        compiler_params=_cparams(("parallel", "parallel")),
        name="conv",
    )(x, w, b)


def _positions():
    t = np.arange(CHUNK)
    return [t, CHUNK - 1 - t]


def _hgrn2_masks():
    ds, ws = [], []
    for p in _positions():
        le = (p[None, :] <= p[:, None]).astype(np.float32)
        mid = [SUB * (p[SUB * j] // SUB) + SUB // 2 - 1 for j in range(N_SUB)]
        ref = np.zeros((8, CHUNK), np.float32)
        for j in range(N_SUB):
            ref[j] = p <= mid[j]
        ref[N_SUB] = 1.0
        ds.append(np.concatenate([le, ref], axis=0))
        tok_blk = np.arange(CHUNK)[:, None] // SUB
        ws.append(np.concatenate([(tok_blk == j) * le for j in range(N_SUB)], axis=1))
    return np.stack(ds).astype(np.float32), np.stack(ws).astype(np.float32)


def _mlstm_masks():
    le2s, let2s = [], []
    for p in _positions():
        le = (p[None, :] <= p[:, None]).astype(np.float32)
        le2s.append(np.concatenate([le, le], axis=1))
        let2 = np.zeros((2 * CHUNK, 2 * CHUNK), np.float32)
        let2[:CHUNK, :CHUNK] = le.T
        let2[CHUNK:, CHUNK:] = le.T
        let2s.append(let2)
    return np.stack(le2s), np.stack(let2s)


def _chunk_rows(c, nc, direction):
    cc = jnp.where(direction == 0, c, nc - 1 - c)
    return pl.multiple_of(cc * CHUNK, CHUNK)


def _hgrn2_kernel(q_ref, v_ref, k_ref, lf_ref, d_ref, w_ref, s0_ref, o_ref, st_ref,
                  u_ref, qin_ref, g_ref, *, nc):
    direction = pl.program_id(1)

    @pl.when(pl.program_id(2) == 0)
    def _():
        st_ref[...] = s0_ref[...]

    dmat = d_ref[0].astype(BF16)
    wmask = w_ref[0] > 0.0
    L = CHUNK
    hw = q_ref.shape[-1]

    group = min(GROUP, nc)

    def intra(grp, carry):
        cs = [grp * group + i for i in range(group)]
        rows = [pl.ds(_chunk_rows(c, nc, direction), L) for c in cs]
        items = [(i, h) for i in range(group) for h in range(HG_HEADS)]
        lanes = [slice(h * HG_DIM, (h + 1) * HG_DIM) for h in range(HG_HEADS)]
        parts = [_split3(lf_ref[0, 0, r, :]) for r in rows]
        e_all = sum(_dot(dmat, jnp.concatenate([parts[i][n] for i in range(group)], axis=1))
                    for n in range(3))
        q_mids, k_mids, k_outs, vbs = [], [], [], []
        for i in range(group):
            e = e_all[:, i * hw:(i + 1) * hw]
            q = q_ref[0, rows[i], :]
            k = k_ref[0, 0, rows[i], :]
            b = e[0:L]
            b_tot = e[L + N_SUB:L + N_SUB + 1]
            refs = [e[L + j:L + j + 1] for j in range(N_SUB)]
            ref_rows = jnp.concatenate([jnp.broadcast_to(r, (SUB, hw)) for r in refs], axis=0)
            q_mids.append((q * jnp.exp(jnp.minimum(b - ref_rows, EXP_CLAMP))).astype(BF16))
            k_mids.append(jnp.concatenate(
                [(k * jnp.exp(jnp.minimum(r - b, EXP_CLAMP))).astype(BF16) for r in refs], axis=0))
            k_outs.append((k * jnp.exp(b_tot - b)).astype(BF16))
            vbs.append(v_ref[0, rows[i], :].astype(BF16))
            qin_ref[cs[i]] = (q * jnp.exp(b)).astype(BF16)
            g_ref[cs[i]] = jnp.broadcast_to(jnp.exp(b_tot), (8, hw))
        scores = [_dot_nt(q_mids[i][:, lanes[h]], k_mids[i][:, lanes[h]]) for i, h in items]
        a_s = [jnp.where(wmask, s, 0.0).astype(BF16) for s in scores]
        v4s = [jnp.concatenate([vb] * N_SUB, axis=0) for vb in vbs]
        outs = [_dot(a_s[n], v4s[i][:, lanes[h]]) for n, (i, h) in enumerate(items)]
        for i in range(group):
            o_ref[0, 0, rows[i], :] = jnp.concatenate(outs[i * HG_HEADS:(i + 1) * HG_HEADS], axis=1)
        for i, h in items:
            u_ref[cs[i], h] = _dot_tn(vbs[i][:, lanes[h]], k_outs[i][:, lanes[h]])
        return carry

    def carry_state(c, carry):
        rows = pl.ds(_chunk_rows(c, nc, direction), L)
        q_in = qin_ref[c]
        g = g_ref[c][0:1]
        outs, states = [], []
        for h in range(HG_HEADS):
            ln = slice(h * HG_DIM, (h + 1) * HG_DIM)
            st = st_ref[0, 0, h]
            outs.append(_dot_nt(q_in[:, ln], st.astype(BF16)))
            states.append(g[:, ln] * st + u_ref[c, h])
        o_ref[0, 0, rows, :] += jnp.concatenate(outs, axis=1)
        for h in range(HG_HEADS):
            st_ref[0, 0, h] = states[h]
        return carry

    lax.fori_loop(0, nc // group, intra, 0)
    lax.fori_loop(0, nc, carry_state, 0, unroll=2)


def _hgrn2(q, v, k, lf, s0, dmat, wmask):
    bsz, t, hw = q.shape
    nc = min(8, t // CHUNK)
    tb = nc * CHUNK
    nblk = t // tb

    def blk(j, d):
        return jnp.where(d == 0, j, nblk - 1 - j)

    st_spec = pl.BlockSpec((1, 1) + s0.shape[2:], lambda b, d, j: (b, d, 0, 0, 0))
    return pl.pallas_call(
        functools.partial(_hgrn2_kernel, nc=nc),
        grid=(bsz, 2, nblk),
        in_specs=[pl.BlockSpec((1, tb, hw), lambda b, d, j: (b, blk(j, d), 0)),
                  pl.BlockSpec((1, tb, hw), lambda b, d, j: (b, blk(j, d), 0)),
                  pl.BlockSpec((1, 1, tb, hw), lambda b, d, j: (b, d, blk(j, d), 0)),
                  pl.BlockSpec((1, 1, tb, hw), lambda b, d, j: (b, d, blk(j, d), 0)),
                  pl.BlockSpec((1,) + dmat.shape[1:], lambda b, d, j: (d, 0, 0)),
                  pl.BlockSpec((1,) + wmask.shape[1:], lambda b, d, j: (d, 0, 0)),
                  st_spec],
        out_specs=[pl.BlockSpec((1, 1, tb, hw), lambda b, d, j: (b, d, blk(j, d), 0)), st_spec],
        out_shape=[jax.ShapeDtypeStruct((bsz, 2, t, hw), F32),
                   jax.ShapeDtypeStruct(s0.shape, F32)],
        scratch_shapes=[pltpu.VMEM((nc,) + s0.shape[2:], F32),
                        pltpu.VMEM((nc, CHUNK, hw), BF16),
                        pltpu.VMEM((nc, 8, hw), F32)],
        compiler_params=_cparams(("parallel", "parallel", "arbitrary")),
        name="hgrn2_scan",
    )(q, v, k, lf, dmat, wmask, s0)


def _mlstm_kernel(qk_ref, v_ref, g_ref, le2_ref, let2_ref, c0_ref, m0_ref,
                  o_ref, c_ref, mm_ref,
                  p_ref, u_ref, qs_ref, ct_ref, mloc_ref, sc_ref, *, nc):
    direction = pl.program_id(1)

    @pl.when(pl.program_id(2) == 0)
    def _():
        c_ref[...] = c0_ref[...]
        mm_ref[...] = m0_ref[...]

    le2 = le2_ref[0].astype(BF16)
    let2 = let2_ref[0].astype(BF16)
    causal = le2_ref[0] > 0.0
    L = CHUNK
    neg = -jnp.inf
    lane = lax.broadcasted_iota(jnp.int32, (1, LANE), 1)
    seg = [lane < L, lane >= L]
    lane_t = lax.broadcasted_iota(jnp.int32, (L, LANE), 1) < L
    eye = (lax.broadcasted_iota(jnp.int32, (LANE, LANE), 0)
           == lax.broadcasted_iota(jnp.int32, (LANE, LANE), 1))
    npair = ML_HEADS // 2
    ones_v = jnp.ones((L, ML_V), BF16)

    group = min(GROUP, nc)

    def intra(grp, carry):
        cs = [grp * group + i for i in range(group)]
        ccs = [jnp.where(direction == 0, c, nc - 1 - c) for c in cs]
        rows = [pl.ds(pl.multiple_of(cc * L, L), L) for cc in ccs]
        pairs = [(i, p) for i in range(group) for p in range(npair)]
        gs = [jnp.where(direction == 0, g_ref[0, cc][0:4], g_ref[0, cc][4:8]) for cc in ccs]
        i2 = jnp.concatenate([g[0:2] for g in gs], axis=0)
        lf2 = _log_sigmoid(jnp.concatenate([g[2:4] for g in gs], axis=0))
        parts = _split3(lf2)
        r2 = i2 - sum(_dot(x, let2) for x in parts)
        g_a = jnp.max(jnp.where(seg[0], r2, neg), axis=1, keepdims=True)
        g_b = jnp.max(jnp.where(seg[1], r2, neg), axis=1, keepdims=True)
        e_w = jnp.exp(r2 - jnp.where(seg[0], g_a, g_b))
        bt_a = jnp.sum(jnp.where(seg[0], lf2, 0.0), axis=1, keepdims=True)
        bt_b = jnp.sum(jnp.where(seg[1], lf2, 0.0), axis=1, keepdims=True)

        def head_rows(x, i):
            zero = jnp.zeros_like(x[0:1])
            return jnp.concatenate(
                [jnp.broadcast_to(jnp.where(seg[h % 2], x[2 * i + h // 2:2 * i + h // 2 + 1], zero),
                                  (LANE, LANE)) for h in range(ML_HEADS)], axis=0)

        c_ts = [sum(_dot_nt(le2, head_rows(x, i)) for x in parts) for i in range(group)]
        for i in range(group):
            ct_ref[cs[i]] = c_ts[i]
            sc_ref[cs[i]] = jnp.concatenate(
                [jnp.broadcast_to(x[2 * i + p:2 * i + p + 1], (1, LANE))
                 for x2 in ((bt_a, bt_b), (g_a, g_b)) for p in range(npair) for x in x2], axis=0)
        qks = [qk_ref[0, r, :] for r in rows]
        vbs = [v_ref[0, r, :].astype(BF16) for r in rows]
        q_ps = [qks[i][:, p * LANE:(p + 1) * LANE].astype(BF16) for i, p in pairs]
        k_ps = [qks[i][:, (npair + p) * LANE:(npair + p + 1) * LANE].astype(BF16) for i, p in pairs]
        zb = jnp.zeros((L, LANE), BF16)
        kbs = [jnp.concatenate([jnp.where(lane_t, k, zb), jnp.where(lane_t, zb, k)], axis=0) for k in k_ps]
        scores = [_dot_nt(q, kb) for q, kb in zip(q_ps, kbs)]
        ws = []
        for n, (i, p) in enumerate(pairs):
            ha, hb = 2 * p, 2 * p + 1
            t1 = jnp.where(lane_t, c_ts[i][:, ha * ML_V:(ha + 1) * ML_V], c_ts[i][:, hb * ML_V:(hb + 1) * ML_V])
            d_log = jnp.where(causal, t1 + r2[2 * i + p:2 * i + p + 1], neg)
            m_a = jnp.max(jnp.where(lane_t, d_log, neg), axis=1, keepdims=True)
            m_b = jnp.max(jnp.where(lane_t, neg, d_log), axis=1, keepdims=True)
            mloc_ref[cs[i], :, ha * ML_V:(ha + 1) * ML_V] = jnp.broadcast_to(m_a, (L, ML_V))
            mloc_ref[cs[i], :, hb * ML_V:(hb + 1) * ML_V] = jnp.broadcast_to(m_b, (L, ML_V))
            ws.append(jnp.exp(d_log - jnp.where(lane_t, m_a, m_b)))
        a_s = [(s_ * w).astype(BF16) for s_, w in zip(scores, ws)]
        a_stacks = [jnp.concatenate([jnp.where(lane_t, a, zb), jnp.where(lane_t, zb, a)], axis=0) for a in a_s]
        v_stacks = [jnp.concatenate(
            [jnp.concatenate([vbs[i][:, 2 * p * ML_V:(2 * p + 1) * ML_V], ones_v], axis=1),
             jnp.concatenate([vbs[i][:, (2 * p + 1) * ML_V:(2 * p + 2) * ML_V], ones_v], axis=1)], axis=0)
            for i, p in pairs]
        for n, (i, p) in enumerate(pairs):
            p_ref[cs[i], p] = _dot(a_stacks[n], v_stacks[n])
            qs_ref[cs[i], p] = jnp.concatenate(
                [jnp.where(lane_t, q_ps[n], zb), jnp.where(lane_t, zb, q_ps[n])], axis=0)
        diags = [jnp.where(eye, jnp.broadcast_to(e_w[2 * i + p:2 * i + p + 1], (LANE, LANE)), 0.0).astype(BF16)
                 for i, p in pairs]
        kb_ws = [_dot(d, kb).astype(BF16) for d, kb in zip(diags, kbs)]
        for n, (i, p) in enumerate(pairs):
            u_ref[cs[i], p] = _dot_tn(kb_ws[n], v_stacks[n])
        return carry

    def carry_state(c, carry):
        cc = jnp.where(direction == 0, c, nc - 1 - c)
        rows = pl.ds(pl.multiple_of(cc * L, L), L)
        m_all = mm_ref[0, 0]
        sc = sc_ref[c]
        c_t = ct_ref[c]
        m_loc = mloc_ref[c]
        outs, states, m_rows = [], [], []
        for p in range(ML_HEADS // 2):
            c_pair = c_ref[0, 0, p]
            inter = _dot(qs_ref[c, p], c_pair.astype(BF16))
            intra_p = p_ref[c, p]
            inc = u_ref[c, p]
            new_rows = []
            for hh in range(2):
                h = 2 * p + hh
                rs = slice(hh * L, (hh + 1) * L)
                ln = slice(h * ML_V, (h + 1) * ML_V)
                m_in = m_all[h:h + 1]
                m_t = jnp.maximum(m_loc[:, ln], c_t[:, ln] + m_in)
                alpha = jnp.exp(m_loc[:, ln] - m_t)
                beta = jnp.exp(c_t[:, ln] + m_in - m_t)
                num = alpha * intra_p[rs, :ML_V] + beta * inter[rs, :ML_V]
                den = alpha * intra_p[rs, ML_V:] + beta * inter[rs, ML_V:]
                outs.append(num / jnp.maximum(jnp.abs(den), jnp.exp(-m_t)))
                b_tot = sc[h:h + 1]
                m_new = b_tot + jnp.maximum(m_in, sc[ML_HEADS + h:ML_HEADS + h + 1])
                carry_w = jnp.exp(b_tot + m_in - m_new)
                inc_w = jnp.exp(b_tot + sc[ML_HEADS + h:ML_HEADS + h + 1] - m_new)
                cw2 = jnp.concatenate([carry_w, carry_w], axis=1)
                iw2 = jnp.concatenate([inc_w, inc_w], axis=1)
                new_rows.append(cw2 * c_pair[rs] + iw2 * inc[rs])
                m_rows.append(m_new)
            states.append(jnp.concatenate(new_rows, axis=0))
        o_ref[0, 0, rows, :] = jnp.concatenate(outs, axis=1)
        for p in range(ML_HEADS // 2):
            c_ref[0, 0, p] = states[p]
        mm_ref[0, 0] = jnp.concatenate(m_rows + [m_all[ML_HEADS:]], axis=0)
        return carry

    lax.fori_loop(0, nc // group, intra, 0)
    lax.fori_loop(0, nc, carry_state, 0, unroll=2)


def _mlstm(qk, v, gates, c0, m0, le2, let2):
    bsz, t, hw = v.shape
    nc = min(8, t // CHUNK)
    tb = nc * CHUNK
    nblk = t // tb
    npair = ML_HEADS // 2

    def blk(j, d):
        return jnp.where(d == 0, j, nblk - 1 - j)

    c_spec = pl.BlockSpec((1, 1) + c0.shape[2:], lambda b, d, j: (b, d, 0, 0, 0))
    m_spec = pl.BlockSpec((1, 1) + m0.shape[2:], lambda b, d, j: (b, d, 0, 0))
    return pl.pallas_call(
        functools.partial(_mlstm_kernel, nc=nc),
        grid=(bsz, 2, nblk),
        in_specs=[pl.BlockSpec((1, tb, qk.shape[-1]), lambda b, d, j: (b, blk(j, d), 0)),
                  pl.BlockSpec((1, tb, hw), lambda b, d, j: (b, blk(j, d), 0)),
                  pl.BlockSpec((1, nc) + gates.shape[2:], lambda b, d, j: (b, blk(j, d), 0, 0)),
                  pl.BlockSpec((1,) + le2.shape[1:], lambda b, d, j: (d, 0, 0)),
                  pl.BlockSpec((1,) + let2.shape[1:], lambda b, d, j: (d, 0, 0)),
                  c_spec, m_spec],
        out_specs=[pl.BlockSpec((1, 1, tb, hw), lambda b, d, j: (b, d, blk(j, d), 0)), c_spec, m_spec],
        out_shape=[jax.ShapeDtypeStruct((bsz, 2, t, hw), F32),
                   jax.ShapeDtypeStruct(c0.shape, F32),
                   jax.ShapeDtypeStruct(m0.shape, F32)],
        scratch_shapes=[pltpu.VMEM((nc, npair, 2 * CHUNK, 2 * ML_V), F32),
                        pltpu.VMEM((nc, npair, 2 * ML_QK, 2 * ML_V), F32),
                        pltpu.VMEM((nc, npair, 2 * CHUNK, LANE), BF16),
                        pltpu.VMEM((nc, CHUNK, hw), F32),
                        pltpu.VMEM((nc, CHUNK, hw), F32),
                        pltpu.VMEM((nc, 8, LANE), F32)],
        compiler_params=_cparams(("parallel", "parallel", "arbitrary")),
        name="mlstm_scan",
    )(qk, v, gates, le2, let2, c0, m0)


def _head_rms(x, heads):
    dh = x.shape[-1] // heads
    return jnp.concatenate([_rms(x[:, h * dh:(h + 1) * dh]) for h in range(heads)], axis=1)


def _outproj_kernel(oh_ref, om_ref, hg_ref, mo_ref, x_ref, mod_ref, hnw_ref, mnw_ref, w_ref, o_ref):
    oh = oh_ref[0, 0] + oh_ref[0, 1]
    om = om_ref[0, 0] + om_ref[0, 1]
    hg_out = _head_rms(oh, HG_HEADS) * hnw_ref[...] * _silu(hg_ref[0])
    ml_out = _sigmoid(mo_ref[0]) * (_head_rms(om, ML_HEADS) * mnw_ref[...])
    merged = jnp.concatenate([hg_out, ml_out], axis=1).astype(BF16)
    o_ref[0] = x_ref[0] + mod_ref[0] * _dot(merged, w_ref[...])


def _outproj(oh, om, hg, mo, x, mod5, hnw, mnw, w_out):
    bsz, t, d = x.shape
    hw = hg.shape[-1]
    tm = min(512, t)
    tok = lambda b, i: (b, i, 0)
    tok2 = lambda b, i: (b, 0, i, 0)
    const = lambda b, i: (0, 0)
    return pl.pallas_call(
        _outproj_kernel,
        grid=(bsz, t // tm),
        in_specs=[pl.BlockSpec((1, 2, tm, hw), tok2), pl.BlockSpec((1, 2, tm, hw), tok2),
                  pl.BlockSpec((1, tm, hw), tok), pl.BlockSpec((1, tm, hw), tok),
                  pl.BlockSpec((1, tm, d), tok),
                  pl.BlockSpec((1, 1, d), lambda b, i: (b, 0, 0)),
                  pl.BlockSpec((1, hw), const), pl.BlockSpec((1, hw), const),
                  pl.BlockSpec(w_out.shape, const, pipeline_mode=pl.Buffered(1))],
        out_specs=pl.BlockSpec((1, tm, d), tok),
        out_shape=jax.ShapeDtypeStruct((bsz, t, d), F32),
        compiler_params=_cparams(("parallel", "parallel")),
        name="outproj",
    )(oh, om, hg, mo, x, mod5, hnw, mnw, w_out)


def kernel(x, c, ctx, c_ctx, w_mod, b_mod, norm1_w, ffn1_w1, ffn1_w2, norm2_w, w_in, ml_gate_b,
           ml_conv_w, ml_conv_b, hg_lb_logits, hg_norm_w, ml_norm_w, w_out, norm3_w, ffn2_w1, ffn2_w2,
           final_norm_w):
    bsz, seq, d = x.shape
    n_ctx = ctx.shape[1]
    assert w_mod.shape[0] == 1, "single-layer kernel"
    assert seq % (GRID_W * 8) == 0 and n_ctx % CHUNK == 0
    hw = HG_HEADS * HG_DIM
    ng = ml_gate_b.shape[-1]

    rows = -(-(bsz + 1) // 8) * 8
    cvec = jnp.zeros((rows, d), F32).at[:bsz].set(c).at[bsz].set(c_ctx)
    mods = _modulation(cvec, w_mod[0], b_mod[0][None, :]).reshape(rows, N_MOD, d)
    lat_row = lambda b: b
    ctx_row = lambda b: bsz

    row = lambda a: a.reshape(1, -1)
    w1a, w2a = ffn1_w1[0].astype(BF16), ffn1_w2[0].astype(BF16)
    x1 = _ffn(x, mods[:, 0:3], lat_row, row(norm1_w[0]), w1a, w2a, row(final_norm_w), False)
    s1 = _ffn(ctx, mods[:, 0:3], ctx_row, row(norm1_w[0]), w1a, w2a, row(final_norm_w), False)

    w_main = w_in[0][:, :8 * hw].astype(BF16)
    w_gate = jnp.zeros((d, LANE), BF16).at[:, :ng].set(w_in[0][:, 8 * hw:].astype(BF16))
    gate_b = row(ml_gate_b[0])
    dmat, wmask = (jnp.asarray(a) for a in _hgrn2_masks())
    le2, let2 = (jnp.asarray(a) for a in _mlstm_masks())
    conv_w, conv_b = ml_conv_w[0], row(ml_conv_b[0])
    npair = ML_HEADS // 2

    def mixer_scans(s, mod_row, width, s0, c0, m0):
        hq, hv, hg, hk, hlf, mqk, mv, mo, gcol = _inproj(
            s, mods[:, 3:5], mod_row, row(norm2_w[0]), w_main, w_gate, gate_b, hg_lb_logits)
        n_chunks = s.shape[1] // CHUNK
        qk = _conv(mqk, conv_w, conv_b, width)
        gates = gcol.reshape(bsz, n_chunks, CHUNK, 2, 2, npair, 2).transpose(0, 1, 3, 4, 5, 6, 2)
        gates = gates.reshape(bsz, n_chunks, 8, 2 * CHUNK)
        oh, s_fin = _hgrn2(hq, hv, hk, hlf, s0, dmat, wmask)
        om, c_fin, m_fin = _mlstm(qk, mv, gates, c0, m0, le2, let2)
        return oh, om, hg, mo, s_fin, c_fin, m_fin

    s0 = jnp.zeros((bsz, 2, HG_HEADS, HG_DIM, HG_DIM), F32)
    c0 = jnp.zeros((bsz, 2, npair, 2 * ML_QK, 2 * ML_V), F32)
    m0 = jnp.zeros((bsz, 2, 8, LANE), F32)
    _, _, _, _, s_ctx, c_ctx_state, m_ctx = mixer_scans(s1, ctx_row, n_ctx, s0, c0, m0)
    oh, om, hg, mo, _, _, _ = mixer_scans(x1, lat_row, GRID_W, s_ctx, c_ctx_state, m_ctx)

    x2 = _outproj(oh, om, hg, mo, x1, mods[:, 5:6], row(hg_norm_w[0]), row(ml_norm_w[0]),
                  w_out[0].astype(BF16))
    return _ffn(x2, mods[:, 6:9], lat_row, row(norm3_w[0]), ffn2_w1[0].astype(BF16),
                ffn2_w2[0].astype(BF16), row(final_norm_w), True)
```

```python
import functools

import numpy as np
import jax
import jax.numpy as jnp
from jax import lax
from jax.experimental import pallas as pl
from jax.experimental.pallas import tpu as pltpu

F32 = jnp.float32
BF16 = jnp.bfloat16

EPS = 1e-6
CHUNK = 64
GRID_W = 64
N_MOD = 9
HG_HEADS = 4
HG_DIM = 128
ML_HEADS = 4
ML_QK = 64
ML_V = 128
SUB = 16
N_SUB = CHUNK // SUB
GROUP = 4
EXP_CLAMP = 80.0
LANE = 128
VMEM_LIMIT = 56 * 1024 * 1024


def _cparams(sem):
    return pltpu.CompilerParams(dimension_semantics=sem, vmem_limit_bytes=VMEM_LIMIT)


def _dot(a, b):
    return jnp.dot(a, b, preferred_element_type=F32)


def _dot_nt(a, b):
    return lax.dot_general(a, b, (((1,), (1,)), ((), ())), preferred_element_type=F32)


def _dot_tn(a, b):
    return lax.dot_general(a, b, (((0,), (0,)), ((), ())), preferred_element_type=F32)


def _sigmoid(x):
    return 1.0 / (1.0 + jnp.exp(-x))


def _silu(x):
    return x * _sigmoid(x)


def _log_sigmoid(x):
    return jnp.minimum(x, 0.0) - jnp.log(1.0 + jnp.exp(-jnp.abs(x)))


def _split3(x):
    hi = x.astype(BF16)
    r = x - hi.astype(F32)
    mid = r.astype(BF16)
    lo = (r - mid.astype(F32)).astype(BF16)
    return hi, mid, lo


def _rms(x):
    return x * lax.rsqrt(jnp.mean(x * x, axis=-1, keepdims=True) + EPS)


def _mod_kernel(c_ref, w_ref, b_ref, o_ref):
    a = _silu(c_ref[...])
    a_hi = a.astype(BF16)
    a_lo = (a - a_hi.astype(F32)).astype(BF16)
    w = w_ref[...]
    w_hi = w.astype(BF16)
    w_lo = (w - w_hi.astype(F32)).astype(BF16)
    o_ref[...] = _dot(a_hi, w_hi) + _dot(a_hi, w_lo) + _dot(a_lo, w_hi) + b_ref[...]


def _modulation(cvec, w_mod, b_mod):
    rows, d = cvec.shape
    n = w_mod.shape[1]
    tn = 1024
    return pl.pallas_call(
        _mod_kernel,
        grid=(n // tn,),
        in_specs=[pl.BlockSpec((rows, d), lambda j: (0, 0)),
                  pl.BlockSpec((d, tn), lambda j: (0, j)),
                  pl.BlockSpec((1, tn), lambda j: (0, j))],
        out_specs=pl.BlockSpec((rows, tn), lambda j: (0, j)),
        out_shape=jax.ShapeDtypeStruct((rows, n), F32),
        compiler_params=_cparams(("arbitrary",)),
        name="modulation",
    )(cvec, w_mod, b_mod)


def _ffn_kernel(x_ref, mod_ref, nw_ref, w1_ref, w2_ref, fw_ref, o_ref, a_ref, *, d_ff, tf, final):
    x = x_ref[0]
    shift = mod_ref[0, 0:1, :]
    scale = mod_ref[0, 1:2, :]
    gate = mod_ref[0, 2:3, :]
    h = (_rms(x) * nw_ref[...] * (1.0 + scale) + shift).astype(BF16)
    for c in range(d_ff // tf):
        g = _dot(h, w1_ref[:, c * tf:(c + 1) * tf])
        u = _dot(h, w1_ref[:, d_ff + c * tf:d_ff + (c + 1) * tf])
        a_ref[:, c * tf:(c + 1) * tf] = (_silu(g) * u).astype(BF16)
    y = x + 0.5 * gate * _dot(a_ref[...], w2_ref[...])
    if final:
        y = _rms(y) * fw_ref[...]
    o_ref[0] = y


def _ffn(s, mods, mod_row, nw, w1, w2, fw, final):
    bsz, t, d = s.shape
    d_ff = w2.shape[0]
    tm = min(512, t)
    tf = 256
    const = lambda b, i: (0, 0)
    return pl.pallas_call(
        functools.partial(_ffn_kernel, d_ff=d_ff, tf=tf, final=final),
        grid=(bsz, t // tm),
        in_specs=[pl.BlockSpec((1, tm, d), lambda b, i: (b, i, 0)),
                  pl.BlockSpec((1, 3, d), lambda b, i: (mod_row(b), 0, 0)),
                  pl.BlockSpec((1, d), const),
                  pl.BlockSpec((d, 2 * d_ff), const, pipeline_mode=pl.Buffered(1)),
                  pl.BlockSpec((d_ff, d), const, pipeline_mode=pl.Buffered(1)),
                  pl.BlockSpec((1, d), const)],
        out_specs=pl.BlockSpec((1, tm, d), lambda b, i: (b, i, 0)),
        out_shape=jax.ShapeDtypeStruct((bsz, t, d), F32),
        scratch_shapes=[pltpu.VMEM((tm, d_ff), BF16)],
        compiler_params=_cparams(("parallel", "parallel")),
        name="ffn_final" if final else "ffn",
    )(s, mods, nw, w1, w2, fw)


def _inproj_kernel(x_ref, mod_ref, nw_ref, w_ref, gb_ref, lbl_ref,
                   hq_ref, hv_ref, hg_ref, hlf_ref, mqk_ref, mv_ref, mo_ref, gc_ref, *, hw):
    x = x_ref[0]
    shift = mod_ref[0, 0:1, :]
    scale = mod_ref[0, 1:2, :]
    h = (_rms(x) * nw_ref[...] * (1.0 + scale) + shift).astype(BF16)

    def proj(k):
        return _dot(h, w_ref[:, k * hw:(k + 1) * hw])

    hq_ref[0] = _silu(proj(0)) * (HG_DIM ** -0.5)
    hv_ref[0] = proj(1).astype(BF16)
    hg_ref[0] = proj(2).astype(BF16)
    lbl = lbl_ref[...]
    e = jnp.exp(lbl - jnp.max(lbl, axis=0, keepdims=True))
    lb = e[0] / jnp.sum(e, axis=0)
    for d in range(2):
        lbd = lb[d:d + 1, :]
        hlf_ref[0, d] = jnp.log(lbd + (1.0 - lbd) * _sigmoid(proj(3 + d)))
    mqk_ref[0] = proj(5).astype(BF16)
    mv_ref[0] = proj(6).astype(BF16)
    mo_ref[0] = proj(7).astype(BF16)
    ng = gc_ref.shape[-1]
    gc_ref[0] = _dot(h, w_ref[:, 8 * hw:8 * hw + ng]) + gb_ref[...]


def _inproj(s, mods, mod_row, nw, w_in, gate_b, lb_logits):
    bsz, t, d = s.shape
    hw = HG_HEADS * HG_DIM
    ng = gate_b.shape[-1]
    tm = min(512, t)
    const = lambda b, i: (0, 0)
    tok = lambda b, i: (b, i, 0)
    tok2 = lambda b, i: (b, 0, i, 0)
    f = lambda dt, *shape: jax.ShapeDtypeStruct(shape, dt)
    return pl.pallas_call(
        functools.partial(_inproj_kernel, hw=hw),
        grid=(bsz, t // tm),
        in_specs=[pl.BlockSpec((1, tm, d), tok),
                  pl.BlockSpec((1, 2, d), lambda b, i: (mod_row(b), 0, 0)),
                  pl.BlockSpec((1, d), const),
                  pl.BlockSpec(w_in.shape, const, pipeline_mode=pl.Buffered(1)),
                  pl.BlockSpec((1, ng), const),
                  pl.BlockSpec(lb_logits.shape, lambda b, i: (0, 0, 0))],
        out_specs=[pl.BlockSpec((1, tm, hw), tok), pl.BlockSpec((1, tm, hw), tok),
                   pl.BlockSpec((1, tm, hw), tok),
                   pl.BlockSpec((1, 2, tm, hw), tok2),
                   pl.BlockSpec((1, tm, hw), tok), pl.BlockSpec((1, tm, hw), tok),
                   pl.BlockSpec((1, tm, hw), tok), pl.BlockSpec((1, tm, ng), tok)],
        out_shape=[f(F32, bsz, t, hw), f(BF16, bsz, t, hw), f(BF16, bsz, t, hw),
                   f(F32, bsz, 2, t, hw),
                   f(BF16, bsz, t, hw), f(BF16, bsz, t, hw), f(BF16, bsz, t, hw), f(F32, bsz, t, ng)],
        compiler_params=_cparams(("parallel", "parallel")),
        name="inproj",
    )(s, mods, nw, w_in, gate_b, lb_logits)


def _conv_kernel(x_ref, w_ref, b_ref, o_ref, *, width, q_tiles):
    x = x_ref[0].astype(F32)
    t = x.shape[0]
    col = lax.broadcasted_iota(jnp.int32, x.shape, 0) % width
    xl = jnp.where(col == 0, 0.0, pltpu.roll(x, 1, 0))
    xr = jnp.where(col == width - 1, 0.0, pltpu.roll(x, t - 1, 0))

    def row(di):
        return (w_ref[di, 0:1, :] * xl + w_ref[di, 1:2, :] * x + w_ref[di, 2:3, :] * xr)

    y = row(1)
    if t > width:
        pad = jnp.zeros((width, x.shape[1]), F32)
        y = y + jnp.concatenate([pad, row(0)[:t - width]], axis=0)
        y = y + jnp.concatenate([row(2)[width:], pad], axis=0)
    y = _silu(y + b_ref[...])
    qscale = jnp.where(pl.program_id(1) < q_tiles, ML_QK ** -0.5, 1.0)
    o_ref[0] = (y * qscale).astype(BF16)


def _conv(x, w, b, width):
    bsz, t, ch = x.shape
    return pl.pallas_call(
        functools.partial(_conv_kernel, width=width, q_tiles=ML_HEADS * ML_QK // LANE),
        grid=(bsz, ch // LANE),
        in_specs=[pl.BlockSpec((1, t, LANE), lambda b_, c: (b_, 0, c)),
                  pl.BlockSpec((3, 3, LANE), lambda b_, c: (0, 0, c)),
                  pl.BlockSpec((1, LANE), lambda b_, c: (0, c))],
        out_specs=pl.BlockSpec((1, t, LANE), lambda b_, c: (b_, 0, c)),
        out_shape=jax.ShapeDtypeStruct((bsz, t, ch), BF16),
        compiler_params=_cparams(("parallel", "parallel")),
        name="conv",
    )(x, w, b)


def _positions():
    t = np.arange(CHUNK)
    return [t, CHUNK - 1 - t]


def _hgrn2_masks():
    ds, ws = [], []
    for p in _positions():
        le = (p[None, :] <= p[:, None]).astype(np.float32)
        mid = [SUB * (p[SUB * j] // SUB) + SUB // 2 - 1 for j in range(N_SUB)]
        ref = np.zeros((8, CHUNK), np.float32)
        for j in range(N_SUB):
            ref[j] = p <= mid[j]
        ref[N_SUB] = 1.0
        ds.append(np.concatenate([le, ref], axis=0))
        tok_blk = np.arange(CHUNK)[:, None] // SUB
        ws.append(np.concatenate([(tok_blk == j) * le for j in range(N_SUB)], axis=1))
    return np.stack(ds).astype(np.float32), np.stack(ws).astype(np.float32)


def _mlstm_masks():
    le2s, let2s = [], []
    for p in _positions():
        le = (p[None, :] <= p[:, None]).astype(np.float32)
        le2s.append(np.concatenate([le, le], axis=1))
        let2 = np.zeros((2 * CHUNK, 2 * CHUNK), np.float32)
        let2[:CHUNK, :CHUNK] = le.T
        let2[CHUNK:, CHUNK:] = le.T
        let2s.append(let2)
    return np.stack(le2s), np.stack(let2s)


def _chunk_rows(c, nc, direction):
    cc = jnp.where(direction == 0, c, nc - 1 - c)
    return pl.multiple_of(cc * CHUNK, CHUNK)


def _hgrn2_kernel(q_ref, v_ref, lf_ref, d_ref, w_ref, s0_ref, o_ref, st_ref,
                  u_ref, qin_ref, g_ref, oi_ref, *, nc):
    direction = pl.program_id(1)

    @pl.when(pl.program_id(2) == 0)
    def _():
        st_ref[...] = s0_ref[...]

    dmat = d_ref[0].astype(BF16)
    wmask = w_ref[0] > 0.0
    L = CHUNK
    hw = q_ref.shape[-1]

    group = min(GROUP, nc)

    def intra(grp, carry):
        cs = [grp * group + i for i in range(group)]
        rows = [pl.ds(_chunk_rows(c, nc, direction), L) for c in cs]
        items = [(i, h) for i in range(group) for h in range(HG_HEADS)]
        lanes = [slice(h * HG_DIM, (h + 1) * HG_DIM) for h in range(HG_HEADS)]
        lfs = [lf_ref[0, 0, r, :] for r in rows]
        parts = [_split3(lf) for lf in lfs]
        e_all = sum(_dot(dmat, jnp.concatenate([parts[i][n] for i in range(group)], axis=1))
                    for n in range(3))
        q_mids, k_mids, k_outs, vbs = [], [], [], []
        for i in range(group):
            e = e_all[:, i * hw:(i + 1) * hw]
            q = q_ref[0, rows[i], :]
            k = 1.0 - jnp.exp(lfs[i])
            b = e[0:L]
            b_tot = e[L + N_SUB:L + N_SUB + 1]
            refs = [e[L + j:L + j + 1] for j in range(N_SUB)]
            ref_rows = jnp.concatenate([jnp.broadcast_to(r, (SUB, hw)) for r in refs], axis=0)
            q_mids.append((q * jnp.exp(jnp.minimum(b - ref_rows, EXP_CLAMP))).astype(BF16))
            k_mids.append(jnp.concatenate(
                [(k * jnp.exp(jnp.minimum(r - b, EXP_CLAMP))).astype(BF16) for r in refs], axis=0))
            k_outs.append((k * jnp.exp(b_tot - b)).astype(BF16))
            vbs.append(v_ref[0, rows[i], :])
            qin_ref[cs[i]] = (q * jnp.exp(b)).astype(BF16)
            g_ref[cs[i]] = jnp.broadcast_to(jnp.exp(b_tot), (8, hw))
        scores = [_dot_nt(q_mids[i][:, lanes[h]], k_mids[i][:, lanes[h]]) for i, h in items]
        a_s = [jnp.where(wmask, s, 0.0).astype(BF16) for s in scores]
        v4s = [jnp.concatenate([vb] * N_SUB, axis=0) for vb in vbs]
        outs = [_dot(a_s[n], v4s[i][:, lanes[h]]) for n, (i, h) in enumerate(items)]
        for i in range(group):
            oi_ref[cs[i]] = jnp.concatenate(outs[i * HG_HEADS:(i + 1) * HG_HEADS], axis=1)
        for i, h in items:
            u_ref[cs[i], h] = _dot_tn(vbs[i][:, lanes[h]], k_outs[i][:, lanes[h]])
        return carry

    def carry_state(c, carry):
        rows = pl.ds(_chunk_rows(c, nc, direction), L)
        q_in = qin_ref[c]
        g = g_ref[c][0:1]
        outs, states = [], []
        for h in range(HG_HEADS):
            ln = slice(h * HG_DIM, (h + 1) * HG_DIM)
            st = st_ref[0, 0, h]
            outs.append(_dot_nt(q_in[:, ln], st.astype(BF16)))
            states.append(g[:, ln] * st + u_ref[c, h])
        o_ref[0, 0, rows, :] = (oi_ref[c] + jnp.concatenate(outs, axis=1)).astype(BF16)
        for h in range(HG_HEADS):
            st_ref[0, 0, h] = states[h]
        return carry

    lax.fori_loop(0, nc // group, intra, 0)
    lax.fori_loop(0, nc, carry_state, 0, unroll=2)


def _hgrn2(q, v, lf, s0, dmat, wmask):
    bsz, t, hw = q.shape
    nc = min(8, t // CHUNK)
    tb = nc * CHUNK
    nblk = t // tb

    def blk(j, d):
        return jnp.where(d == 0, j, nblk - 1 - j)

    st_spec = pl.BlockSpec((1, 1) + s0.shape[2:], lambda b, d, j: (b, d, 0, 0, 0))
    return pl.pallas_call(
        functools.partial(_hgrn2_kernel, nc=nc),
        grid=(bsz, 2, nblk),
        in_specs=[pl.BlockSpec((1, tb, hw), lambda b, d, j: (b, blk(j, d), 0)),
                  pl.BlockSpec((1, tb, hw), lambda b, d, j: (b, blk(j, d), 0)),
                  pl.BlockSpec((1, 1, tb, hw), lambda b, d, j: (b, d, blk(j, d), 0)),
                  pl.BlockSpec((1,) + dmat.shape[1:], lambda b, d, j: (d, 0, 0)),
                  pl.BlockSpec((1,) + wmask.shape[1:], lambda b, d, j: (d, 0, 0)),
                  st_spec],
        out_specs=[pl.BlockSpec((1, 1, tb, hw), lambda b, d, j: (b, d, blk(j, d), 0)), st_spec],
        out_shape=[jax.ShapeDtypeStruct((bsz, 2, t, hw), BF16),
                   jax.ShapeDtypeStruct(s0.shape, F32)],
        scratch_shapes=[pltpu.VMEM((nc,) + s0.shape[2:], F32),
                        pltpu.VMEM((nc, CHUNK, hw), BF16),
                        pltpu.VMEM((nc, 8, hw), F32),
                        pltpu.VMEM((nc, CHUNK, hw), F32)],
        compiler_params=_cparams(("parallel", "parallel", "arbitrary")),
        name="hgrn2_scan",
    )(q, v, lf, dmat, wmask, s0)


def _mlstm_kernel(qk_ref, v_ref, g_ref, le2_ref, let2_ref, c0_ref, m0_ref,
                  o_ref, c_ref, mm_ref,
                  p_ref, u_ref, qs_ref, ct_ref, mloc_ref, sc_ref, *, nc):
    direction = pl.program_id(1)

    @pl.when(pl.program_id(2) == 0)
    def _():
        c_ref[...] = c0_ref[...]
        mm_ref[...] = m0_ref[...]

    le2 = le2_ref[0].astype(BF16)
    let2 = let2_ref[0].astype(BF16)
    causal = le2_ref[0] > 0.0
    L = CHUNK
    neg = -jnp.inf
    lane = lax.broadcasted_iota(jnp.int32, (1, LANE), 1)
    seg = [lane < L, lane >= L]
    lane_t = lax.broadcasted_iota(jnp.int32, (L, LANE), 1) < L
    eye = (lax.broadcasted_iota(jnp.int32, (LANE, LANE), 0)
           == lax.broadcasted_iota(jnp.int32, (LANE, LANE), 1))
    npair = ML_HEADS // 2
    ones_v = jnp.ones((L, ML_V), BF16)

    group = min(GROUP, nc)

    def intra(grp, carry):
        cs = [grp * group + i for i in range(group)]
        ccs = [jnp.where(direction == 0, c, nc - 1 - c) for c in cs]
        rows = [pl.ds(pl.multiple_of(cc * L, L), L) for cc in ccs]
        pairs = [(i, p) for i in range(group) for p in range(npair)]
        gs = [jnp.where(direction == 0, g_ref[0, cc][0:4], g_ref[0, cc][4:8]) for cc in ccs]
        i2 = jnp.concatenate([g[0:2] for g in gs], axis=0)
        lf2 = _log_sigmoid(jnp.concatenate([g[2:4] for g in gs], axis=0))
        parts = _split3(lf2)
        r2 = i2 - sum(_dot(x, let2) for x in parts)
        g_a = jnp.max(jnp.where(seg[0], r2, neg), axis=1, keepdims=True)
        g_b = jnp.max(jnp.where(seg[1], r2, neg), axis=1, keepdims=True)
        e_w = jnp.exp(r2 - jnp.where(seg[0], g_a, g_b))
        bt_a = jnp.sum(jnp.where(seg[0], lf2, 0.0), axis=1, keepdims=True)
        bt_b = jnp.sum(jnp.where(seg[1], lf2, 0.0), axis=1, keepdims=True)

        def head_rows(x, i):
            zero = jnp.zeros_like(x[0:1])
            return jnp.concatenate(
                [jnp.broadcast_to(jnp.where(seg[h % 2], x[2 * i + h // 2:2 * i + h // 2 + 1], zero),
                                  (LANE, LANE)) for h in range(ML_HEADS)], axis=0)

        c_ts = [sum(_dot_nt(le2, head_rows(x, i)) for x in parts) for i in range(group)]
        for i in range(group):
            ct_ref[cs[i]] = c_ts[i]
            sc_ref[cs[i]] = jnp.concatenate(
                [jnp.broadcast_to(x[2 * i + p:2 * i + p + 1], (1, LANE))
                 for x2 in ((bt_a, bt_b), (g_a, g_b)) for p in range(npair) for x in x2], axis=0)
        qks = [qk_ref[0, r, :] for r in rows]
        vbs = [v_ref[0, r, :] for r in rows]
        q_ps = [qks[i][:, p * LANE:(p + 1) * LANE] for i, p in pairs]
        k_ps = [qks[i][:, (npair + p) * LANE:(npair + p + 1) * LANE] for i, p in pairs]
        zb = jnp.zeros((L, LANE), BF16)
        kbs = [jnp.concatenate([jnp.where(lane_t, k, zb), jnp.where(lane_t, zb, k)], axis=0) for k in k_ps]
        scores = [_dot_nt(q, kb) for q, kb in zip(q_ps, kbs)]
        ws = []
        for n, (i, p) in enumerate(pairs):
            ha, hb = 2 * p, 2 * p + 1
            t1 = jnp.where(lane_t, c_ts[i][:, ha * ML_V:(ha + 1) * ML_V], c_ts[i][:, hb * ML_V:(hb + 1) * ML_V])
            d_log = jnp.where(causal, t1 + r2[2 * i + p:2 * i + p + 1], neg)
            m_a = jnp.max(jnp.where(lane_t, d_log, neg), axis=1, keepdims=True)
            m_b = jnp.max(jnp.where(lane_t, neg, d_log), axis=1, keepdims=True)
            mloc_ref[cs[i], :, ha * ML_V:(ha + 1) * ML_V] = jnp.broadcast_to(m_a, (L, ML_V))
            mloc_ref[cs[i], :, hb * ML_V:(hb + 1) * ML_V] = jnp.broadcast_to(m_b, (L, ML_V))
            ws.append(jnp.exp(d_log - jnp.where(lane_t, m_a, m_b)))
        a_s = [(s_ * w).astype(BF16) for s_, w in zip(scores, ws)]
        a_stacks = [jnp.concatenate([jnp.where(lane_t, a, zb), jnp.where(lane_t, zb, a)], axis=0) for a in a_s]
        v_stacks = [jnp.concatenate(
            [jnp.concatenate([vbs[i][:, 2 * p * ML_V:(2 * p + 1) * ML_V], ones_v], axis=1),
             jnp.concatenate([vbs[i][:, (2 * p + 1) * ML_V:(2 * p + 2) * ML_V], ones_v], axis=1)], axis=0)
            for i, p in pairs]
        for n, (i, p) in enumerate(pairs):
            p_ref[cs[i], p] = _dot(a_stacks[n], v_stacks[n])
            qs_ref[cs[i], p] = jnp.concatenate(
                [jnp.where(lane_t, q_ps[n], zb), jnp.where(lane_t, zb, q_ps[n])], axis=0)
        diags = [jnp.where(eye, jnp.broadcast_to(e_w[2 * i + p:2 * i + p + 1], (LANE, LANE)), 0.0).astype(BF16)
                 for i, p in pairs]
        kb_ws = [_dot(d, kb).astype(BF16) for d, kb in zip(diags, kbs)]
        for n, (i, p) in enumerate(pairs):
            u_ref[cs[i], p] = _dot_tn(kb_ws[n], v_stacks[n])
        return carry

    def carry_state(c, carry):
        cc = jnp.where(direction == 0, c, nc - 1 - c)
        rows = pl.ds(pl.multiple_of(cc * L, L), L)
        m_all = mm_ref[0, 0]
        sc = sc_ref[c]
        c_t = ct_ref[c]
        m_loc = mloc_ref[c]
        outs, states, m_rows = [], [], []
        for p in range(ML_HEADS // 2):
            c_pair = c_ref[0, 0, p]
            inter = _dot(qs_ref[c, p], c_pair.astype(BF16))
            intra_p = p_ref[c, p]
            inc = u_ref[c, p]
            new_rows = []
            for hh in range(2):
                h = 2 * p + hh
                rs = slice(hh * L, (hh + 1) * L)
                ln = slice(h * ML_V, (h + 1) * ML_V)
                m_in = m_all[h:h + 1]
                m_t = jnp.maximum(m_loc[:, ln], c_t[:, ln] + m_in)
                alpha = jnp.exp(m_loc[:, ln] - m_t)
                beta = jnp.exp(c_t[:, ln] + m_in - m_t)
                num = alpha * intra_p[rs, :ML_V] + beta * inter[rs, :ML_V]
                den = alpha * intra_p[rs, ML_V:] + beta * inter[rs, ML_V:]
                outs.append(num / jnp.maximum(jnp.abs(den), jnp.exp(-m_t)))
                b_tot = sc[h:h + 1]
                m_new = b_tot + jnp.maximum(m_in, sc[ML_HEADS + h:ML_HEADS + h + 1])
                carry_w = jnp.exp(b_tot + m_in - m_new)
                inc_w = jnp.exp(b_tot + sc[ML_HEADS + h:ML_HEADS + h + 1] - m_new)
                cw2 = jnp.concatenate([carry_w, carry_w], axis=1)
                iw2 = jnp.concatenate([inc_w, inc_w], axis=1)
                new_rows.append(cw2 * c_pair[rs] + iw2 * inc[rs])
                m_rows.append(m_new)
            states.append(jnp.concatenate(new_rows, axis=0))
        o_ref[0, 0, rows, :] = jnp.concatenate(outs, axis=1).astype(BF16)
        for p in range(ML_HEADS // 2):
            c_ref[0, 0, p] = states[p]
        mm_ref[0, 0] = jnp.concatenate(m_rows + [m_all[ML_HEADS:]], axis=0)
        return carry

    lax.fori_loop(0, nc // group, intra, 0)
    lax.fori_loop(0, nc, carry_state, 0, unroll=2)


def _mlstm(qk, v, gates, c0, m0, le2, let2):
    bsz, t, hw = v.shape
    nc = min(8, t // CHUNK)
    tb = nc * CHUNK
    nblk = t // tb
    npair = ML_HEADS // 2

    def blk(j, d):
        return jnp.where(d == 0, j, nblk - 1 - j)

    c_spec = pl.BlockSpec((1, 1) + c0.shape[2:], lambda b, d, j: (b, d, 0, 0, 0))
    m_spec = pl.BlockSpec((1, 1) + m0.shape[2:], lambda b, d, j: (b, d, 0, 0))
    return pl.pallas_call(
        functools.partial(_mlstm_kernel, nc=nc),
        grid=(bsz, 2, nblk),
        in_specs=[pl.BlockSpec((1, tb, qk.shape[-1]), lambda b, d, j: (b, blk(j, d), 0)),
                  pl.BlockSpec((1, tb, hw), lambda b, d, j: (b, blk(j, d), 0)),
                  pl.BlockSpec((1, nc) + gates.shape[2:], lambda b, d, j: (b, blk(j, d), 0, 0)),
                  pl.BlockSpec((1,) + le2.shape[1:], lambda b, d, j: (d, 0, 0)),
                  pl.BlockSpec((1,) + let2.shape[1:], lambda b, d, j: (d, 0, 0)),
                  c_spec, m_spec],
        out_specs=[pl.BlockSpec((1, 1, tb, hw), lambda b, d, j: (b, d, blk(j, d), 0)), c_spec, m_spec],
        out_shape=[jax.ShapeDtypeStruct((bsz, 2, t, hw), BF16),
                   jax.ShapeDtypeStruct(c0.shape, F32),
                   jax.ShapeDtypeStruct(m0.shape, F32)],
        scratch_shapes=[pltpu.VMEM((nc, npair, 2 * CHUNK, 2 * ML_V), F32),
                        pltpu.VMEM((nc, npair, 2 * ML_QK, 2 * ML_V), F32),
                        pltpu.VMEM((nc, npair, 2 * CHUNK, LANE), BF16),
                        pltpu.VMEM((nc, CHUNK, hw), F32),
                        pltpu.VMEM((nc, CHUNK, hw), F32),
                        pltpu.VMEM((nc, 8, LANE), F32)],
        compiler_params=_cparams(("parallel", "parallel", "arbitrary")),
        name="mlstm_scan",
    )(qk, v, gates, le2, let2, c0, m0)


def _head_rms(x, heads):
    dh = x.shape[-1] // heads
    return jnp.concatenate([_rms(x[:, h * dh:(h + 1) * dh]) for h in range(heads)], axis=1)


def _outproj_kernel(oh_ref, om_ref, hg_ref, mo_ref, x_ref, mod_ref, hnw_ref, mnw_ref, w_ref, o_ref):
    oh = oh_ref[0, 0].astype(F32) + oh_ref[0, 1].astype(F32)
    om = om_ref[0, 0].astype(F32) + om_ref[0, 1].astype(F32)
    hg_out = _head_rms(oh, HG_HEADS) * hnw_ref[...] * _silu(hg_ref[0].astype(F32))
    ml_out = _sigmoid(mo_ref[0].astype(F32)) * (_head_rms(om, ML_HEADS) * mnw_ref[...])
    merged = jnp.concatenate([hg_out, ml_out], axis=1).astype(BF16)
    o_ref[0] = x_ref[0] + mod_ref[0] * _dot(merged, w_ref[...])


def _outproj(oh, om, hg, mo, x, mod5, hnw, mnw, w_out):
    bsz, t, d = x.shape
    hw = hg.shape[-1]
    tm = min(512, t)
    tok = lambda b, i: (b, i, 0)
    tok2 = lambda b, i: (b, 0, i, 0)
    const = lambda b, i: (0, 0)
    return pl.pallas_call(
        _outproj_kernel,
        grid=(bsz, t // tm),
        in_specs=[pl.BlockSpec((1, 2, tm, hw), tok2), pl.BlockSpec((1, 2, tm, hw), tok2),
                  pl.BlockSpec((1, tm, hw), tok), pl.BlockSpec((1, tm, hw), tok),
                  pl.BlockSpec((1, tm, d), tok),
                  pl.BlockSpec((1, 1, d), lambda b, i: (b, 0, 0)),
                  pl.BlockSpec((1, hw), const), pl.BlockSpec((1, hw), const),
                  pl.BlockSpec(w_out.shape, const, pipeline_mode=pl.Buffered(1))],
        out_specs=pl.BlockSpec((1, tm, d), tok),
        out_shape=jax.ShapeDtypeStruct((bsz, t, d), F32),
        compiler_params=_cparams(("parallel", "parallel")),
        name="outproj",
    )(oh, om, hg, mo, x, mod5, hnw, mnw, w_out)


def kernel(x, c, ctx, c_ctx, w_mod, b_mod, norm1_w, ffn1_w1, ffn1_w2, norm2_w, w_in, ml_gate_b,
           ml_conv_w, ml_conv_b, hg_lb_logits, hg_norm_w, ml_norm_w, w_out, norm3_w, ffn2_w1, ffn2_w2,
           final_norm_w):
    bsz, seq, d = x.shape
    n_ctx = ctx.shape[1]
    assert w_mod.shape[0] == 1, "single-layer kernel"
    assert seq % (GRID_W * 8) == 0 and n_ctx % CHUNK == 0
    hw = HG_HEADS * HG_DIM
    ng = ml_gate_b.shape[-1]

    rows = -(-(bsz + 1) // 8) * 8
    cvec = jnp.zeros((rows, d), F32).at[:bsz].set(c).at[bsz].set(c_ctx)
    mods = _modulation(cvec, w_mod[0], b_mod[0][None, :]).reshape(rows, N_MOD, d)
    lat_row = lambda b: b
    ctx_row = lambda b: bsz

    row = lambda a: a.reshape(1, -1)
    w1a, w2a = ffn1_w1[0].astype(BF16), ffn1_w2[0].astype(BF16)
    x1 = _ffn(x, mods[:, 0:3], lat_row, row(norm1_w[0]), w1a, w2a, row(final_norm_w), False)
    s1 = _ffn(ctx, mods[:, 0:3], ctx_row, row(norm1_w[0]), w1a, w2a, row(final_norm_w), False)

    w_in_b = w_in[0].astype(BF16)
    gate_b = row(ml_gate_b[0])
    dmat, wmask = (jnp.asarray(a) for a in _hgrn2_masks())
    le2, let2 = (jnp.asarray(a) for a in _mlstm_masks())
    conv_w, conv_b = ml_conv_w[0], row(ml_conv_b[0])
    npair = ML_HEADS // 2

    def mixer_scans(s, mod_row, width, s0, c0, m0):
        hq, hv, hg, hlf, mqk, mv, mo, gcol = _inproj(
            s, mods[:, 3:5], mod_row, row(norm2_w[0]), w_in_b, gate_b, hg_lb_logits)
        n_chunks = s.shape[1] // CHUNK
        qk = _conv(mqk, conv_w, conv_b, width)
        gates = gcol.reshape(bsz, n_chunks, CHUNK, 2, 2, npair, 2).transpose(0, 1, 3, 4, 5, 6, 2)
        gates = gates.reshape(bsz, n_chunks, 8, 2 * CHUNK)
        oh, s_fin = _hgrn2(hq, hv, hlf, s0, dmat, wmask)
        om, c_fin, m_fin = _mlstm(qk, mv, gates, c0, m0, le2, let2)
        return oh, om, hg, mo, s_fin, c_fin, m_fin

    s0 = jnp.zeros((bsz, 2, HG_HEADS, HG_DIM, HG_DIM), F32)
    c0 = jnp.zeros((bsz, 2, npair, 2 * ML_QK, 2 * ML_V), F32)
    m0 = jnp.zeros((bsz, 2, 8, LANE), F32)
    _, _, _, _, s_ctx, c_ctx_state, m_ctx = mixer_scans(s1, ctx_row, n_ctx, s0, c0, m0)
    oh, om, hg, mo, _, _, _ = mixer_scans(x1, lat_row, GRID_W, s_ctx, c_ctx_state, m_ctx)

    x2 = _outproj(oh, om, hg, mo, x1, mods[:, 5:6], row(hg_norm_w[0]), row(ml_norm_w[0]),
                  w_out[0].astype(BF16))
    return _ffn(x2, mods[:, 6:9], lat_row, row(norm3_w[0]), ffn2_w1[0].astype(BF16),
                ffn2_w2[0].astype(BF16), row(final_norm_w), True)
```

```python
import functools

import numpy as np
import jax
import jax.numpy as jnp
from jax import lax
from jax.experimental import pallas as pl
from jax.experimental.pallas import tpu as pltpu

F32 = jnp.float32
BF16 = jnp.bfloat16

EPS = 1e-6
CHUNK = 64
GRID_W = 64
N_MOD = 9
HG_HEADS = 4
HG_DIM = 128
ML_HEADS = 4
ML_QK = 64
ML_V = 128
SUB = 16
N_SUB = CHUNK // SUB
GROUP = 4
TOKEN_TILE = 512
ROW_SUB = 256
FF_TILE = 256
EXP_CLAMP = 80.0
LANE = 128
VMEM_LIMIT = 56 * 1024 * 1024


def _cparams(sem):
    return pltpu.CompilerParams(dimension_semantics=sem, vmem_limit_bytes=VMEM_LIMIT)


def _dot(a, b):
    return jnp.dot(a, b, preferred_element_type=F32)


def _dot_nt(a, b):
    return lax.dot_general(a, b, (((1,), (1,)), ((), ())), preferred_element_type=F32)


def _dot_tn(a, b):
    return lax.dot_general(a, b, (((0,), (0,)), ((), ())), preferred_element_type=F32)


def _sigmoid(x):
    return 1.0 / (1.0 + jnp.exp(-x))


def _silu(x):
    return x * _sigmoid(x)


def _log_sigmoid(x):
    return jnp.minimum(x, 0.0) - jnp.log(1.0 + jnp.exp(-jnp.abs(x)))


def _split3(x):
    hi = x.astype(BF16)
    r = x - hi.astype(F32)
    mid = r.astype(BF16)
    lo = (r - mid.astype(F32)).astype(BF16)
    return hi, mid, lo


def _rms(x):
    return x * lax.rsqrt(jnp.mean(x * x, axis=-1, keepdims=True) + EPS)


def _mod_kernel(c_ref, w_ref, b_ref, o_ref):
    a = _silu(c_ref[...])
    a_hi = a.astype(BF16)
    a_lo = (a - a_hi.astype(F32)).astype(BF16)
    w = w_ref[...]
    w_hi = w.astype(BF16)
    w_lo = (w - w_hi.astype(F32)).astype(BF16)
    o_ref[...] = _dot(a_hi, w_hi) + _dot(a_hi, w_lo) + _dot(a_lo, w_hi) + b_ref[...]


def _modulation(cvec, w_mod, b_mod):
    rows, d = cvec.shape
    n = w_mod.shape[1]
    tn = 1024
    return pl.pallas_call(
        _mod_kernel,
        grid=(n // tn,),
        in_specs=[pl.BlockSpec((rows, d), lambda j: (0, 0)),
                  pl.BlockSpec((d, tn), lambda j: (0, j)),
                  pl.BlockSpec((1, tn), lambda j: (0, j))],
        out_specs=pl.BlockSpec((rows, tn), lambda j: (0, j)),
        out_shape=jax.ShapeDtypeStruct((rows, n), F32),
        compiler_params=_cparams(("arbitrary",)),
        name="modulation",
    )(cvec, w_mod, b_mod)


def _ffn_rows(x, mods, nw_ref, w1_ref, w2_ref, a_ref, rs, d_ff, tf):
    shift, scale, gate = mods
    h = (_rms(x) * nw_ref[...] * (1.0 + scale) + shift).astype(BF16)
    for c in range(d_ff // tf):
        g = _dot(h, w1_ref[:, c * tf:(c + 1) * tf])
        u = _dot(h, w1_ref[:, d_ff + c * tf:d_ff + (c + 1) * tf])
        a_ref[rs, c * tf:(c + 1) * tf] = (_silu(g) * u).astype(BF16)
    return x + 0.5 * gate * _dot(a_ref[rs, :], w2_ref[...])


def _ffn_kernel(x_ref, mod_ref, nw_ref, w1_ref, w2_ref, o_ref, a_ref, *, d_ff, tf):
    mods = [mod_ref[0, k:k + 1, :] for k in range(3)]
    for r0 in range(0, x_ref.shape[1], ROW_SUB):
        rs = slice(r0, r0 + ROW_SUB)
        o_ref[0, rs, :] = _ffn_rows(x_ref[0, rs, :], mods, nw_ref, w1_ref, w2_ref, a_ref, rs, d_ff, tf)


def _ffn(s, mods, mod_row, nw, w1, w2):
    bsz, t, d = s.shape
    d_ff = w2.shape[0]
    tm = min(TOKEN_TILE, t)
    const = lambda b, i: (0, 0)
    return pl.pallas_call(
        functools.partial(_ffn_kernel, d_ff=d_ff, tf=FF_TILE),
        grid=(bsz, t // tm),
        in_specs=[pl.BlockSpec((1, tm, d), lambda b, i: (b, i, 0)),
                  pl.BlockSpec((1, 3, d), lambda b, i: (mod_row(b), 0, 0)),
                  pl.BlockSpec((1, d), const),
                  pl.BlockSpec((d, 2 * d_ff), const, pipeline_mode=pl.Buffered(1)),
                  pl.BlockSpec((d_ff, d), const, pipeline_mode=pl.Buffered(1))],
        out_specs=pl.BlockSpec((1, tm, d), lambda b, i: (b, i, 0)),
        out_shape=jax.ShapeDtypeStruct((bsz, t, d), F32),
        scratch_shapes=[pltpu.VMEM((tm, d_ff), BF16)],
        compiler_params=_cparams(("parallel", "parallel")),
        name="ffn",
    )(s, mods, nw, w1, w2)


def _inproj_kernel(x_ref, mod_ref, nw_ref, w_ref, gb_ref, lbl_ref,
                   hq_ref, hv_ref, hg_ref, hlf_ref, mqk_ref, mv_ref, mo_ref, gc_ref, *, hw):
    shift = mod_ref[0, 0:1, :]
    scale = mod_ref[0, 1:2, :]
    lbl = lbl_ref[...]
    e = jnp.exp(lbl - jnp.max(lbl, axis=0, keepdims=True))
    lb = e[0] / jnp.sum(e, axis=0)
    ng = gc_ref.shape[-1]
    for r0 in range(0, x_ref.shape[1], ROW_SUB):
        rs = slice(r0, r0 + ROW_SUB)
        h = (_rms(x_ref[0, rs, :]) * nw_ref[...] * (1.0 + scale) + shift).astype(BF16)

        def proj(k):
            return _dot(h, w_ref[:, k * hw:(k + 1) * hw])

        hq_ref[0, rs, :] = _silu(proj(0)) * (HG_DIM ** -0.5)
        hv_ref[0, rs, :] = proj(1).astype(BF16)
        hg_ref[0, rs, :] = proj(2).astype(BF16)
        for d in range(2):
            lbd = lb[d:d + 1, :]
            hlf_ref[0, d, rs, :] = jnp.log(lbd + (1.0 - lbd) * _sigmoid(proj(3 + d)))
        mqk_ref[0, rs, :] = proj(5).astype(BF16)
        mv_ref[0, rs, :] = proj(6).astype(BF16)
        mo_ref[0, rs, :] = proj(7).astype(BF16)
        gc_ref[0, rs, :] = _dot(h, w_ref[:, 8 * hw:8 * hw + ng]) + gb_ref[...]


def _inproj(s, mods, mod_row, nw, w_in, gate_b, lb_logits):
    bsz, t, d = s.shape
    hw = HG_HEADS * HG_DIM
    ng = gate_b.shape[-1]
    tm = min(TOKEN_TILE, t)
    const = lambda b, i: (0, 0)
    tok = lambda b, i: (b, i, 0)
    tok2 = lambda b, i: (b, 0, i, 0)
    f = lambda dt, *shape: jax.ShapeDtypeStruct(shape, dt)
    return pl.pallas_call(
        functools.partial(_inproj_kernel, hw=hw),
        grid=(bsz, t // tm),
        in_specs=[pl.BlockSpec((1, tm, d), tok),
                  pl.BlockSpec((1, 2, d), lambda b, i: (mod_row(b), 0, 0)),
                  pl.BlockSpec((1, d), const),
                  pl.BlockSpec(w_in.shape, const, pipeline_mode=pl.Buffered(1)),
                  pl.BlockSpec((1, ng), const),
                  pl.BlockSpec(lb_logits.shape, lambda b, i: (0, 0, 0))],
        out_specs=[pl.BlockSpec((1, tm, hw), tok), pl.BlockSpec((1, tm, hw), tok),
                   pl.BlockSpec((1, tm, hw), tok),
                   pl.BlockSpec((1, 2, tm, hw), tok2),
                   pl.BlockSpec((1, tm, hw), tok), pl.BlockSpec((1, tm, hw), tok),
                   pl.BlockSpec((1, tm, hw), tok), pl.BlockSpec((1, tm, ng), tok)],
        out_shape=[f(F32, bsz, t, hw), f(BF16, bsz, t, hw), f(BF16, bsz, t, hw),
                   f(F32, bsz, 2, t, hw),
                   f(BF16, bsz, t, hw), f(BF16, bsz, t, hw), f(BF16, bsz, t, hw), f(F32, bsz, t, ng)],
        compiler_params=_cparams(("parallel", "parallel")),
        name="inproj",
    )(s, mods, nw, w_in, gate_b, lb_logits)


def _conv_kernel(x_ref, w_ref, b_ref, o_ref, *, width, q_tiles):
    x = x_ref[0].astype(F32)
    t = x.shape[0]
    col = lax.broadcasted_iota(jnp.int32, x.shape, 0) % width
    xl = jnp.where(col == 0, 0.0, pltpu.roll(x, 1, 0))
    xr = jnp.where(col == width - 1, 0.0, pltpu.roll(x, t - 1, 0))

    def row(di):
        return (w_ref[di, 0:1, :] * xl + w_ref[di, 1:2, :] * x + w_ref[di, 2:3, :] * xr)

    y = row(1)
    if t > width:
        pad = jnp.zeros((width, x.shape[1]), F32)
        y = y + jnp.concatenate([pad, row(0)[:t - width]], axis=0)
        y = y + jnp.concatenate([row(2)[width:], pad], axis=0)
    y = _silu(y + b_ref[...])
    qscale = jnp.where(pl.program_id(1) < q_tiles, ML_QK ** -0.5, 1.0)
    o_ref[0] = (y * qscale).astype(BF16)


def _conv(x, w, b, width):
    bsz, t, ch = x.shape
    return pl.pallas_call(
        functools.partial(_conv_kernel, width=width, q_tiles=ML_HEADS * ML_QK // LANE),
        grid=(bsz, ch // LANE),
        in_specs=[pl.BlockSpec((1, t, LANE), lambda b_, c: (b_, 0, c)),
                  pl.BlockSpec((3, 3, LANE), lambda b_, c: (0, 0, c)),
                  pl.BlockSpec((1, LANE), lambda b_, c: (0, c))],
        out_specs=pl.BlockSpec((1, t, LANE), lambda b_, c: (b_, 0, c)),
        out_shape=jax.ShapeDtypeStruct((bsz, t, ch), BF16),
        compiler_params=_cparams(("parallel", "parallel")),
        name="conv",
    )(x, w, b)


def _positions():
    t = np.arange(CHUNK)
    return [t, CHUNK - 1 - t]


def _hgrn2_masks():
    ds, ws = [], []
    for p in _positions():
        le = (p[None, :] <= p[:, None]).astype(np.float32)
        mid = [SUB * (p[SUB * j] // SUB) + SUB // 2 - 1 for j in range(N_SUB)]
        ref = np.zeros((8, CHUNK), np.float32)
        for j in range(N_SUB):
            ref[j] = p <= mid[j]
        ref[N_SUB] = 1.0
        ds.append(np.concatenate([le, ref], axis=0))
        tok_blk = np.arange(CHUNK)[:, None] // SUB
        ws.append(np.concatenate([(tok_blk == j) * le for j in range(N_SUB)], axis=1))
    return np.stack(ds).astype(np.float32), np.stack(ws).astype(np.float32)


def _mlstm_masks():
    le2s, let2s = [], []
    for p in _positions():
        le = (p[None, :] <= p[:, None]).astype(np.float32)
        le2s.append(np.concatenate([le, le], axis=1))
        let2 = np.zeros((2 * CHUNK, 2 * CHUNK), np.float32)
        let2[:CHUNK, :CHUNK] = le.T
        let2[CHUNK:, CHUNK:] = le.T
        let2s.append(let2)
    return np.stack(le2s), np.stack(let2s)


def _chunk_rows(c, nc, direction):
    cc = jnp.where(direction == 0, c, nc - 1 - c)
    return pl.multiple_of(cc * CHUNK, CHUNK)


def _hgrn2_kernel(q_ref, v_ref, lf_ref, d_ref, w_ref, s0_ref, o_ref, st_ref,
                  u_ref, qin_ref, g_ref, oi_ref, *, nc):
    direction = pl.program_id(1)

    @pl.when(pl.program_id(2) == 0)
    def _():
        st_ref[...] = s0_ref[...]

    dmat = d_ref[0].astype(BF16)
    wmask = w_ref[0] > 0.0
    L = CHUNK
    hw = q_ref.shape[-1]

    group = min(GROUP, nc)

    def intra(grp, carry):
        cs = [grp * group + i for i in range(group)]
        rows = [pl.ds(_chunk_rows(c, nc, direction), L) for c in cs]
        items = [(i, h) for i in range(group) for h in range(HG_HEADS)]
        lanes = [slice(h * HG_DIM, (h + 1) * HG_DIM) for h in range(HG_HEADS)]
        lfs = [lf_ref[0, 0, r, :] for r in rows]
        parts = [_split3(lf) for lf in lfs]
        e_all = sum(_dot(dmat, jnp.concatenate([parts[i][n] for i in range(group)], axis=1))
                    for n in range(3))
        q_mids, k_mids, k_outs, vbs = [], [], [], []
        for i in range(group):
            e = e_all[:, i * hw:(i + 1) * hw]
            q = q_ref[0, rows[i], :]
            k = 1.0 - jnp.exp(lfs[i])
            b = e[0:L]
            b_tot = e[L + N_SUB:L + N_SUB + 1]
            refs = [e[L + j:L + j + 1] for j in range(N_SUB)]
            ref_rows = jnp.concatenate([jnp.broadcast_to(r, (SUB, hw)) for r in refs], axis=0)
            q_mids.append((q * jnp.exp(jnp.minimum(b - ref_rows, EXP_CLAMP))).astype(BF16))
            k_mids.append(jnp.concatenate(
                [(k * jnp.exp(jnp.minimum(r - b, EXP_CLAMP))).astype(BF16) for r in refs], axis=0))
            k_outs.append((k * jnp.exp(b_tot - b)).astype(BF16))
            vbs.append(v_ref[0, rows[i], :])
            qin_ref[cs[i]] = (q * jnp.exp(b)).astype(BF16)
            g_ref[cs[i]] = jnp.broadcast_to(jnp.exp(b_tot), (8, hw))
        scores = [_dot_nt(q_mids[i][:, lanes[h]], k_mids[i][:, lanes[h]]) for i, h in items]
        a_s = [jnp.where(wmask, s, 0.0).astype(BF16) for s in scores]
        v4s = [jnp.concatenate([vb] * N_SUB, axis=0) for vb in vbs]
        outs = [_dot(a_s[n], v4s[i][:, lanes[h]]) for n, (i, h) in enumerate(items)]
        for i in range(group):
            oi_ref[cs[i]] = jnp.concatenate(outs[i * HG_HEADS:(i + 1) * HG_HEADS], axis=1)
        for i, h in items:
            u_ref[cs[i], h] = _dot_tn(vbs[i][:, lanes[h]], k_outs[i][:, lanes[h]])
        return carry

    def carry_state(c, carry):
        rows = pl.ds(_chunk_rows(c, nc, direction), L)
        q_in = qin_ref[c]
        g = g_ref[c][0:1]
        outs, states = [], []
        for h in range(HG_HEADS):
            ln = slice(h * HG_DIM, (h + 1) * HG_DIM)
            st = st_ref[0, 0, h]
            outs.append(_dot_nt(q_in[:, ln], st.astype(BF16)))
            states.append(g[:, ln] * st + u_ref[c, h])
        o_ref[0, 0, rows, :] = (oi_ref[c] + jnp.concatenate(outs, axis=1)).astype(BF16)
        for h in range(HG_HEADS):
            st_ref[0, 0, h] = states[h]
        return carry

    lax.fori_loop(0, nc // group, intra, 0)
    lax.fori_loop(0, nc, carry_state, 0, unroll=2)


def _hgrn2(q, v, lf, s0, dmat, wmask):
    bsz, t, hw = q.shape
    nc = min(8, t // CHUNK)
    tb = nc * CHUNK
    nblk = t // tb

    def blk(j, d):
        return jnp.where(d == 0, j, nblk - 1 - j)

    st_spec = pl.BlockSpec((1, 1) + s0.shape[2:], lambda b, d, j: (b, d, 0, 0, 0))
    return pl.pallas_call(
        functools.partial(_hgrn2_kernel, nc=nc),
        grid=(bsz, 2, nblk),
        in_specs=[pl.BlockSpec((1, tb, hw), lambda b, d, j: (b, blk(j, d), 0)),
                  pl.BlockSpec((1, tb, hw), lambda b, d, j: (b, blk(j, d), 0)),
                  pl.BlockSpec((1, 1, tb, hw), lambda b, d, j: (b, d, blk(j, d), 0)),
                  pl.BlockSpec((1,) + dmat.shape[1:], lambda b, d, j: (d, 0, 0)),
                  pl.BlockSpec((1,) + wmask.shape[1:], lambda b, d, j: (d, 0, 0)),
                  st_spec],
        out_specs=[pl.BlockSpec((1, 1, tb, hw), lambda b, d, j: (b, d, blk(j, d), 0)), st_spec],
        out_shape=[jax.ShapeDtypeStruct((bsz, 2, t, hw), BF16),
                   jax.ShapeDtypeStruct(s0.shape, F32)],
        scratch_shapes=[pltpu.VMEM((nc,) + s0.shape[2:], F32),
                        pltpu.VMEM((nc, CHUNK, hw), BF16),
                        pltpu.VMEM((nc, 8, hw), F32),
                        pltpu.VMEM((nc, CHUNK, hw), F32)],
        compiler_params=_cparams(("parallel", "parallel", "arbitrary")),
        name="hgrn2_scan",
    )(q, v, lf, dmat, wmask, s0)


def _mlstm_kernel(qk_ref, v_ref, g_ref, le2_ref, let2_ref, c0_ref, m0_ref,
                  o_ref, c_ref, mm_ref,
                  p_ref, u_ref, qs_ref, ct_ref, mloc_ref, sc_ref, *, nc):
    direction = pl.program_id(1)

    @pl.when(pl.program_id(2) == 0)
    def _():
        c_ref[...] = c0_ref[...]
        mm_ref[...] = m0_ref[...]

    le2 = le2_ref[0].astype(BF16)
    let2 = let2_ref[0].astype(BF16)
    causal = le2_ref[0] > 0.0
    L = CHUNK
    neg = -jnp.inf
    lane = lax.broadcasted_iota(jnp.int32, (1, LANE), 1)
    seg = [lane < L, lane >= L]
    lane_t = lax.broadcasted_iota(jnp.int32, (L, LANE), 1) < L
    eye = (lax.broadcasted_iota(jnp.int32, (LANE, LANE), 0)
           == lax.broadcasted_iota(jnp.int32, (LANE, LANE), 1))
    npair = ML_HEADS // 2
    ones_v = jnp.ones((L, ML_V), BF16)

    group = min(GROUP, nc)

    def intra(grp, carry):
        cs = [grp * group + i for i in range(group)]
        ccs = [jnp.where(direction == 0, c, nc - 1 - c) for c in cs]
        rows = [pl.ds(pl.multiple_of(cc * L, L), L) for cc in ccs]
        pairs = [(i, p) for i in range(group) for p in range(npair)]
        gs = [jnp.where(direction == 0, g_ref[0, cc][0:4], g_ref[0, cc][4:8]) for cc in ccs]
        i2 = jnp.concatenate([g[0:2] for g in gs], axis=0)
        lf2 = _log_sigmoid(jnp.concatenate([g[2:4] for g in gs], axis=0))
        parts = _split3(lf2)
        r2 = i2 - sum(_dot(x, let2) for x in parts)
        g_a = jnp.max(jnp.where(seg[0], r2, neg), axis=1, keepdims=True)
        g_b = jnp.max(jnp.where(seg[1], r2, neg), axis=1, keepdims=True)
        e_w = jnp.exp(r2 - jnp.where(seg[0], g_a, g_b))
        bt_a = jnp.sum(jnp.where(seg[0], lf2, 0.0), axis=1, keepdims=True)
        bt_b = jnp.sum(jnp.where(seg[1], lf2, 0.0), axis=1, keepdims=True)

        def head_rows(x, i):
            zero = jnp.zeros_like(x[0:1])
            return jnp.concatenate(
                [jnp.broadcast_to(jnp.where(seg[h % 2], x[2 * i + h // 2:2 * i + h // 2 + 1], zero),
                                  (LANE, LANE)) for h in range(ML_HEADS)], axis=0)

        c_ts = [sum(_dot_nt(le2, head_rows(x, i)) for x in parts) for i in range(group)]
        for i in range(group):
            ct_ref[cs[i]] = c_ts[i]
            sc_ref[cs[i]] = jnp.concatenate(
                [jnp.broadcast_to(x[2 * i + p:2 * i + p + 1], (1, LANE))
                 for x2 in ((bt_a, bt_b), (g_a, g_b)) for p in range(npair) for x in x2], axis=0)
        qks = [qk_ref[0, r, :] for r in rows]
        vbs = [v_ref[0, r, :] for r in rows]
        q_ps = [qks[i][:, p * LANE:(p + 1) * LANE] for i, p in pairs]
        k_ps = [qks[i][:, (npair + p) * LANE:(npair + p + 1) * LANE] for i, p in pairs]
        zb = jnp.zeros((L, LANE), BF16)
        kbs = [jnp.concatenate([jnp.where(lane_t, k, zb), jnp.where(lane_t, zb, k)], axis=0) for k in k_ps]
        scores = [_dot_nt(q, kb) for q, kb in zip(q_ps, kbs)]
        ws = []
        for n, (i, p) in enumerate(pairs):
            ha, hb = 2 * p, 2 * p + 1
            t1 = jnp.where(lane_t, c_ts[i][:, ha * ML_V:(ha + 1) * ML_V], c_ts[i][:, hb * ML_V:(hb + 1) * ML_V])
            d_log = jnp.where(causal, t1 + r2[2 * i + p:2 * i + p + 1], neg)
            m_a = jnp.max(jnp.where(lane_t, d_log, neg), axis=1, keepdims=True)
            m_b = jnp.max(jnp.where(lane_t, neg, d_log), axis=1, keepdims=True)
            mloc_ref[cs[i], :, ha * ML_V:(ha + 1) * ML_V] = jnp.broadcast_to(m_a, (L, ML_V))
            mloc_ref[cs[i], :, hb * ML_V:(hb + 1) * ML_V] = jnp.broadcast_to(m_b, (L, ML_V))
            ws.append(jnp.exp(d_log - jnp.where(lane_t, m_a, m_b)))
        a_s = [(s_ * w).astype(BF16) for s_, w in zip(scores, ws)]
        a_stacks = [jnp.concatenate([jnp.where(lane_t, a, zb), jnp.where(lane_t, zb, a)], axis=0) for a in a_s]
        v_stacks = [jnp.concatenate(
            [jnp.concatenate([vbs[i][:, 2 * p * ML_V:(2 * p + 1) * ML_V], ones_v], axis=1),
             jnp.concatenate([vbs[i][:, (2 * p + 1) * ML_V:(2 * p + 2) * ML_V], ones_v], axis=1)], axis=0)
            for i, p in pairs]
        for n, (i, p) in enumerate(pairs):
            p_ref[cs[i], p] = _dot(a_stacks[n], v_stacks[n])
            qs_ref[cs[i], p] = jnp.concatenate(
                [jnp.where(lane_t, q_ps[n], zb), jnp.where(lane_t, zb, q_ps[n])], axis=0)
        diags = [jnp.where(eye, jnp.broadcast_to(e_w[2 * i + p:2 * i + p + 1], (LANE, LANE)), 0.0).astype(BF16)
                 for i, p in pairs]
        kb_ws = [_dot(d, kb).astype(BF16) for d, kb in zip(diags, kbs)]
        for n, (i, p) in enumerate(pairs):
            u_ref[cs[i], p] = _dot_tn(kb_ws[n], v_stacks[n])
        return carry

    def carry_state(c, carry):
        cc = jnp.where(direction == 0, c, nc - 1 - c)
        rows = pl.ds(pl.multiple_of(cc * L, L), L)
        m_all = mm_ref[0, 0]
        sc = sc_ref[c]
        c_t = ct_ref[c]
        m_loc = mloc_ref[c]
        outs, states, m_rows = [], [], []
        for p in range(ML_HEADS // 2):
            c_pair = c_ref[0, 0, p]
            inter = _dot(qs_ref[c, p], c_pair.astype(BF16))
            intra_p = p_ref[c, p]
            inc = u_ref[c, p]
            new_rows = []
            for hh in range(2):
                h = 2 * p + hh
                rs = slice(hh * L, (hh + 1) * L)
                ln = slice(h * ML_V, (h + 1) * ML_V)
                m_in = m_all[h:h + 1]
                m_t = jnp.maximum(m_loc[:, ln], c_t[:, ln] + m_in)
                alpha = jnp.exp(m_loc[:, ln] - m_t)
                beta = jnp.exp(c_t[:, ln] + m_in - m_t)
                num = alpha * intra_p[rs, :ML_V] + beta * inter[rs, :ML_V]
                den = alpha * intra_p[rs, ML_V:] + beta * inter[rs, ML_V:]
                outs.append(num / jnp.maximum(jnp.abs(den), jnp.exp(-m_t)))
                b_tot = sc[h:h + 1]
                m_new = b_tot + jnp.maximum(m_in, sc[ML_HEADS + h:ML_HEADS + h + 1])
                carry_w = jnp.exp(b_tot + m_in - m_new)
                inc_w = jnp.exp(b_tot + sc[ML_HEADS + h:ML_HEADS + h + 1] - m_new)
                cw2 = jnp.concatenate([carry_w, carry_w], axis=1)
                iw2 = jnp.concatenate([inc_w, inc_w], axis=1)
                new_rows.append(cw2 * c_pair[rs] + iw2 * inc[rs])
                m_rows.append(m_new)
            states.append(jnp.concatenate(new_rows, axis=0))
        o_ref[0, 0, rows, :] = jnp.concatenate(outs, axis=1).astype(BF16)
        for p in range(ML_HEADS // 2):
            c_ref[0, 0, p] = states[p]
        mm_ref[0, 0] = jnp.concatenate(m_rows + [m_all[ML_HEADS:]], axis=0)
        return carry

    lax.fori_loop(0, nc // group, intra, 0)
    lax.fori_loop(0, nc, carry_state, 0, unroll=2)


def _mlstm(qk, v, gates, c0, m0, le2, let2):
    bsz, t, hw = v.shape
    nc = min(8, t // CHUNK)
    tb = nc * CHUNK
    nblk = t // tb
    npair = ML_HEADS // 2

    def blk(j, d):
        return jnp.where(d == 0, j, nblk - 1 - j)

    c_spec = pl.BlockSpec((1, 1) + c0.shape[2:], lambda b, d, j: (b, d, 0, 0, 0))
    m_spec = pl.BlockSpec((1, 1) + m0.shape[2:], lambda b, d, j: (b, d, 0, 0))
    return pl.pallas_call(
        functools.partial(_mlstm_kernel, nc=nc),
        grid=(bsz, 2, nblk),
        in_specs=[pl.BlockSpec((1, tb, qk.shape[-1]), lambda b, d, j: (b, blk(j, d), 0)),
                  pl.BlockSpec((1, tb, hw), lambda b, d, j: (b, blk(j, d), 0)),
                  pl.BlockSpec((1, nc) + gates.shape[2:], lambda b, d, j: (b, blk(j, d), 0, 0)),
                  pl.BlockSpec((1,) + le2.shape[1:], lambda b, d, j: (d, 0, 0)),
                  pl.BlockSpec((1,) + let2.shape[1:], lambda b, d, j: (d, 0, 0)),
                  c_spec, m_spec],
        out_specs=[pl.BlockSpec((1, 1, tb, hw), lambda b, d, j: (b, d, blk(j, d), 0)), c_spec, m_spec],
        out_shape=[jax.ShapeDtypeStruct((bsz, 2, t, hw), BF16),
                   jax.ShapeDtypeStruct(c0.shape, F32),
                   jax.ShapeDtypeStruct(m0.shape, F32)],
        scratch_shapes=[pltpu.VMEM((nc, npair, 2 * CHUNK, 2 * ML_V), F32),
                        pltpu.VMEM((nc, npair, 2 * ML_QK, 2 * ML_V), F32),
                        pltpu.VMEM((nc, npair, 2 * CHUNK, LANE), BF16),
                        pltpu.VMEM((nc, CHUNK, hw), F32),
                        pltpu.VMEM((nc, CHUNK, hw), F32),
                        pltpu.VMEM((nc, 8, LANE), F32)],
        compiler_params=_cparams(("parallel", "parallel", "arbitrary")),
        name="mlstm_scan",
    )(qk, v, gates, le2, let2, c0, m0)


def _head_rms(x, heads):
    dh = x.shape[-1] // heads
    return jnp.concatenate([_rms(x[:, h * dh:(h + 1) * dh]) for h in range(heads)], axis=1)


def _mix_ffn_kernel(oh_ref, om_ref, hg_ref, mo_ref, x_ref, mod_ref, hnw_ref, mnw_ref, wo_ref,
                    nw_ref, w1_ref, w2_ref, fw_ref, o_ref, a_ref, *, d_ff, tf):
    mix_gate = mod_ref[0, 0:1, :]
    mods = [mod_ref[0, k:k + 1, :] for k in range(1, 4)]
    for r0 in range(0, x_ref.shape[1], ROW_SUB):
        rs = slice(r0, r0 + ROW_SUB)
        oh = oh_ref[0, 0, rs, :].astype(F32) + oh_ref[0, 1, rs, :].astype(F32)
        om = om_ref[0, 0, rs, :].astype(F32) + om_ref[0, 1, rs, :].astype(F32)
        hg_out = _head_rms(oh, HG_HEADS) * hnw_ref[...] * _silu(hg_ref[0, rs, :].astype(F32))
        ml_out = _sigmoid(mo_ref[0, rs, :].astype(F32)) * (_head_rms(om, ML_HEADS) * mnw_ref[...])
        merged = jnp.concatenate([hg_out, ml_out], axis=1).astype(BF16)
        x = x_ref[0, rs, :] + mix_gate * _dot(merged, wo_ref[...])
        y = _ffn_rows(x, mods, nw_ref, w1_ref, w2_ref, a_ref, rs, d_ff, tf)
        o_ref[0, rs, :] = _rms(y) * fw_ref[...]


def _mix_ffn(oh, om, hg, mo, x, mods, hnw, mnw, w_out, nw, w1, w2, fw):
    bsz, t, d = x.shape
    hw = hg.shape[-1]
    d_ff = w2.shape[0]
    tm = min(TOKEN_TILE, t)
    tok = lambda b, i: (b, i, 0)
    tok2 = lambda b, i: (b, 0, i, 0)
    const = lambda b, i: (0, 0)
    resident = lambda a: pl.BlockSpec(a.shape, const, pipeline_mode=pl.Buffered(1))
    return pl.pallas_call(
        functools.partial(_mix_ffn_kernel, d_ff=d_ff, tf=FF_TILE),
        grid=(bsz, t // tm),
        in_specs=[pl.BlockSpec((1, 2, tm, hw), tok2), pl.BlockSpec((1, 2, tm, hw), tok2),
                  pl.BlockSpec((1, tm, hw), tok), pl.BlockSpec((1, tm, hw), tok),
                  pl.BlockSpec((1, tm, d), tok),
                  pl.BlockSpec((1, 4, d), lambda b, i: (b, 0, 0)),
                  pl.BlockSpec((1, hw), const), pl.BlockSpec((1, hw), const), resident(w_out),
                  pl.BlockSpec((1, d), const), resident(w1), resident(w2), pl.BlockSpec((1, d), const)],
        out_specs=pl.BlockSpec((1, tm, d), tok),
        out_shape=jax.ShapeDtypeStruct((bsz, t, d), F32),
        scratch_shapes=[pltpu.VMEM((tm, d_ff), BF16)],
        compiler_params=_cparams(("parallel", "parallel")),
        name="mix_ffn_final",
    )(oh, om, hg, mo, x, mods, hnw, mnw, w_out, nw, w1, w2, fw)


def kernel(x, c, ctx, c_ctx, w_mod, b_mod, norm1_w, ffn1_w1, ffn1_w2, norm2_w, w_in, ml_gate_b,
           ml_conv_w, ml_conv_b, hg_lb_logits, hg_norm_w, ml_norm_w, w_out, norm3_w, ffn2_w1, ffn2_w2,
           final_norm_w):
    bsz, seq, d = x.shape
    n_ctx = ctx.shape[1]
    assert w_mod.shape[0] == 1, "single-layer kernel"
    assert seq % (GRID_W * 8) == 0 and n_ctx % CHUNK == 0
    hw = HG_HEADS * HG_DIM
    ng = ml_gate_b.shape[-1]

    rows = -(-(bsz + 1) // 8) * 8
    cvec = jnp.zeros((rows, d), F32).at[:bsz].set(c).at[bsz].set(c_ctx)
    mods = _modulation(cvec, w_mod[0], b_mod[0][None, :]).reshape(rows, N_MOD, d)
    lat_row = lambda b: b
    ctx_row = lambda b: bsz

    row = lambda a: a.reshape(1, -1)
    w1a, w2a = ffn1_w1[0].astype(BF16), ffn1_w2[0].astype(BF16)
    x1 = _ffn(x, mods[:, 0:3], lat_row, row(norm1_w[0]), w1a, w2a)
    s1 = _ffn(ctx, mods[:, 0:3], ctx_row, row(norm1_w[0]), w1a, w2a)

    w_in_b = w_in[0].astype(BF16)
    gate_b = row(ml_gate_b[0])
    dmat, wmask = (jnp.asarray(a) for a in _hgrn2_masks())
    le2, let2 = (jnp.asarray(a) for a in _mlstm_masks())
    conv_w, conv_b = ml_conv_w[0], row(ml_conv_b[0])
    npair = ML_HEADS // 2

    def mixer_scans(s, mod_row, width, s0, c0, m0):
        hq, hv, hg, hlf, mqk, mv, mo, gcol = _inproj(
            s, mods[:, 3:5], mod_row, row(norm2_w[0]), w_in_b, gate_b, hg_lb_logits)
        n_chunks = s.shape[1] // CHUNK
        qk = _conv(mqk, conv_w, conv_b, width)
        gates = gcol.reshape(bsz, n_chunks, CHUNK, 2, 2, npair, 2).transpose(0, 1, 3, 4, 5, 6, 2)
        gates = gates.reshape(bsz, n_chunks, 8, 2 * CHUNK)
        oh, s_fin = _hgrn2(hq, hv, hlf, s0, dmat, wmask)
        om, c_fin, m_fin = _mlstm(qk, mv, gates, c0, m0, le2, let2)
        return oh, om, hg, mo, s_fin, c_fin, m_fin

    s0 = jnp.zeros((bsz, 2, HG_HEADS, HG_DIM, HG_DIM), F32)
    c0 = jnp.zeros((bsz, 2, npair, 2 * ML_QK, 2 * ML_V), F32)
    m0 = jnp.zeros((bsz, 2, 8, LANE), F32)
    _, _, _, _, s_ctx, c_ctx_state, m_ctx = mixer_scans(s1, ctx_row, n_ctx, s0, c0, m0)
    oh, om, hg, mo, _, _, _ = mixer_scans(x1, lat_row, GRID_W, s_ctx, c_ctx_state, m_ctx)

    return _mix_ffn(oh, om, hg, mo, x1, mods[:, 5:9], row(hg_norm_w[0]), row(ml_norm_w[0]),
                    w_out[0].astype(BF16), row(norm3_w[0]), ffn2_w1[0].astype(BF16),
                    ffn2_w2[0].astype(BF16), row(final_norm_w))
```

```python
import functools

import numpy as np
import jax
import jax.numpy as jnp
from jax import lax
from jax.experimental import pallas as pl
from jax.experimental.pallas import tpu as pltpu

F32 = jnp.float32
BF16 = jnp.bfloat16

EPS = 1e-6
CHUNK = 64
GRID_W = 64
N_MOD = 9
HG_HEADS = 4
HG_DIM = 128
ML_HEADS = 4
ML_QK = 64
ML_V = 128
SUB = 16
N_SUB = CHUNK // SUB
GROUP = 4
ML_GROUP = 8
TOKEN_TILE = 512
ROW_SUB = 256
FF_TILE = 256
LOG2E = 1.4426950408889634
EXP2_CLAMP = 115.0
LANE = 128
VMEM_LIMIT = 56 * 1024 * 1024


def _cparams(sem):
    return pltpu.CompilerParams(dimension_semantics=sem, vmem_limit_bytes=VMEM_LIMIT)


def _dot(a, b):
    return jnp.dot(a, b, preferred_element_type=F32)


def _dot_nt(a, b):
    return lax.dot_general(a, b, (((1,), (1,)), ((), ())), preferred_element_type=F32)


def _dot_tn(a, b):
    return lax.dot_general(a, b, (((0,), (0,)), ((), ())), preferred_element_type=F32)


def _sigmoid(x):
    return 1.0 / (1.0 + jnp.exp(-x))


def _silu(x):
    return x * _sigmoid(x)


def _log_sigmoid(x):
    return jnp.minimum(x, 0.0) - jnp.log(1.0 + jnp.exp(-jnp.abs(x)))


def _split3(x):
    hi = x.astype(BF16)
    r = x - hi.astype(F32)
    mid = r.astype(BF16)
    lo = (r - mid.astype(F32)).astype(BF16)
    return hi, mid, lo


def _split2(x):
    hi = x.astype(BF16)
    return hi, (x - hi.astype(F32)).astype(BF16)


def _rms(x):
    return x * lax.rsqrt(jnp.mean(x * x, axis=-1, keepdims=True) + EPS)


def _mod_kernel(c_ref, w_ref, b_ref, o_ref):
    a = _silu(c_ref[...])
    a_hi = a.astype(BF16)
    a_lo = (a - a_hi.astype(F32)).astype(BF16)
    w = w_ref[...]
    w_hi = w.astype(BF16)
    w_lo = (w - w_hi.astype(F32)).astype(BF16)
    o_ref[...] = _dot(a_hi, w_hi) + _dot(a_hi, w_lo) + _dot(a_lo, w_hi) + b_ref[...]


def _modulation(cvec, w_mod, b_mod):
    rows, d = cvec.shape
    n = w_mod.shape[1]
    tn = 1024
    return pl.pallas_call(
        _mod_kernel,
        grid=(n // tn,),
        in_specs=[pl.BlockSpec((rows, d), lambda j: (0, 0)),
                  pl.BlockSpec((d, tn), lambda j: (0, j)),
                  pl.BlockSpec((1, tn), lambda j: (0, j))],
        out_specs=pl.BlockSpec((rows, tn), lambda j: (0, j)),
        out_shape=jax.ShapeDtypeStruct((rows, n), F32),
        compiler_params=_cparams(("arbitrary",)),
        name="modulation",
    )(cvec, w_mod, b_mod)


def _ffn_rows(x, mods, nw_ref, w1_ref, w2_ref, a_ref, rs, d_ff, tf):
    shift, scale, gate = mods
    h = (_rms(x) * nw_ref[...] * (1.0 + scale) + shift).astype(BF16)
    for c in range(d_ff // tf):
        g = _dot(h, w1_ref[:, c * tf:(c + 1) * tf])
        u = _dot(h, w1_ref[:, d_ff + c * tf:d_ff + (c + 1) * tf])
        a_ref[rs, c * tf:(c + 1) * tf] = (_silu(g) * u).astype(BF16)
    return x + 0.5 * gate * _dot(a_ref[rs, :], w2_ref[...])


def _ffn_kernel(x_ref, mod_ref, nw_ref, w1_ref, w2_ref, o_ref, a_ref, *, d_ff, tf):
    mods = [mod_ref[0, k:k + 1, :] for k in range(3)]
    for r0 in range(0, x_ref.shape[1], ROW_SUB):
        rs = slice(r0, r0 + ROW_SUB)
        o_ref[0, rs, :] = _ffn_rows(x_ref[0, rs, :], mods, nw_ref, w1_ref, w2_ref, a_ref, rs, d_ff, tf)


def _ffn(s, mods, mod_row, nw, w1, w2):
    bsz, t, d = s.shape
    d_ff = w2.shape[0]
    tm = min(TOKEN_TILE, t)
    const = lambda b, i: (0, 0)
    return pl.pallas_call(
        functools.partial(_ffn_kernel, d_ff=d_ff, tf=FF_TILE),
        grid=(bsz, t // tm),
        in_specs=[pl.BlockSpec((1, tm, d), lambda b, i: (b, i, 0)),
                  pl.BlockSpec((1, 3, d), lambda b, i: (mod_row(b), 0, 0)),
                  pl.BlockSpec((1, d), const),
                  pl.BlockSpec((d, 2 * d_ff), const, pipeline_mode=pl.Buffered(1)),
                  pl.BlockSpec((d_ff, d), const, pipeline_mode=pl.Buffered(1))],
        out_specs=pl.BlockSpec((1, tm, d), lambda b, i: (b, i, 0)),
        out_shape=jax.ShapeDtypeStruct((bsz, t, d), F32),
        scratch_shapes=[pltpu.VMEM((tm, d_ff), BF16)],
        compiler_params=_cparams(("parallel", "parallel")),
        name="ffn",
    )(s, mods, nw, w1, w2)


def _inproj_kernel(x_ref, mod_ref, nw_ref, w_ref, gb_ref, lbl_ref,
                   hq_ref, hv_ref, hg_ref, hlf_ref, mqk_ref, mv_ref, mo_ref, gc_ref, *, hw):
    shift = mod_ref[0, 0:1, :]
    scale = mod_ref[0, 1:2, :]
    lbl = lbl_ref[...]
    e = jnp.exp(lbl - jnp.max(lbl, axis=0, keepdims=True))
    lb = e[0] / jnp.sum(e, axis=0)
    ng = gc_ref.shape[-1]
    for r0 in range(0, x_ref.shape[1], ROW_SUB):
        rs = slice(r0, r0 + ROW_SUB)
        h = (_rms(x_ref[0, rs, :]) * nw_ref[...] * (1.0 + scale) + shift).astype(BF16)

        def proj(k):
            return _dot(h, w_ref[:, k * hw:(k + 1) * hw])

        hq_ref[0, rs, :] = _silu(proj(0)) * (HG_DIM ** -0.5)
        hv_ref[0, rs, :] = proj(1).astype(BF16)
        hg_ref[0, rs, :] = proj(2).astype(BF16)
        for d in range(2):
            lbd = lb[d:d + 1, :]
            hlf_ref[0, d, rs, :] = jnp.log(lbd + (1.0 - lbd) * _sigmoid(proj(3 + d))) * LOG2E
        mqk_ref[0, rs, :] = proj(5).astype(BF16)
        mv_ref[0, rs, :] = proj(6).astype(BF16)
        mo_ref[0, rs, :] = proj(7).astype(BF16)
        gc_ref[0, rs, :] = _dot(h, w_ref[:, 8 * hw:8 * hw + ng]) + gb_ref[...]


def _inproj(s, mods, mod_row, nw, w_in, gate_b, lb_logits):
    bsz, t, d = s.shape
    hw = HG_HEADS * HG_DIM
    ng = gate_b.shape[-1]
    tm = min(TOKEN_TILE, t)
    const = lambda b, i: (0, 0)
    tok = lambda b, i: (b, i, 0)
    tok2 = lambda b, i: (b, 0, i, 0)
    f = lambda dt, *shape: jax.ShapeDtypeStruct(shape, dt)
    return pl.pallas_call(
        functools.partial(_inproj_kernel, hw=hw),
        grid=(bsz, t // tm),
        in_specs=[pl.BlockSpec((1, tm, d), tok),
                  pl.BlockSpec((1, 2, d), lambda b, i: (mod_row(b), 0, 0)),
                  pl.BlockSpec((1, d), const),
                  pl.BlockSpec(w_in.shape, const, pipeline_mode=pl.Buffered(1)),
                  pl.BlockSpec((1, ng), const),
                  pl.BlockSpec(lb_logits.shape, lambda b, i: (0, 0, 0))],
        out_specs=[pl.BlockSpec((1, tm, hw), tok), pl.BlockSpec((1, tm, hw), tok),
                   pl.BlockSpec((1, tm, hw), tok),
                   pl.BlockSpec((1, 2, tm, hw), tok2),
                   pl.BlockSpec((1, tm, hw), tok), pl.BlockSpec((1, tm, hw), tok),
                   pl.BlockSpec((1, tm, hw), tok), pl.BlockSpec((1, tm, ng), tok)],
        out_shape=[f(F32, bsz, t, hw), f(BF16, bsz, t, hw), f(BF16, bsz, t, hw),
                   f(F32, bsz, 2, t, hw),
                   f(BF16, bsz, t, hw), f(BF16, bsz, t, hw), f(BF16, bsz, t, hw), f(F32, bsz, t, ng)],
        compiler_params=_cparams(("parallel", "parallel")),
        name="inproj",
    )(s, mods, nw, w_in, gate_b, lb_logits)


def _conv_kernel(x_ref, w_ref, b_ref, o_ref, *, width, q_tiles):
    x = x_ref[0].astype(F32)
    t = x.shape[0]
    col = lax.broadcasted_iota(jnp.int32, x.shape, 0) % width
    xl = jnp.where(col == 0, 0.0, pltpu.roll(x, 1, 0))
    xr = jnp.where(col == width - 1, 0.0, pltpu.roll(x, t - 1, 0))

    def row(di):
        return (w_ref[di, 0:1, :] * xl + w_ref[di, 1:2, :] * x + w_ref[di, 2:3, :] * xr)

    y = row(1)
    if t > width:
        pad = jnp.zeros((width, x.shape[1]), F32)
        y = y + jnp.concatenate([pad, row(0)[:t - width]], axis=0)
        y = y + jnp.concatenate([row(2)[width:], pad], axis=0)
    y = _silu(y + b_ref[...])
    qscale = jnp.where(pl.program_id(1) < q_tiles, ML_QK ** -0.5, 1.0)
    o_ref[0] = (y * qscale).astype(BF16)


def _conv(x, w, b, width):
    bsz, t, ch = x.shape
    return pl.pallas_call(
        functools.partial(_conv_kernel, width=width, q_tiles=ML_HEADS * ML_QK // LANE),
        grid=(bsz, ch // LANE),
        in_specs=[pl.BlockSpec((1, t, LANE), lambda b_, c: (b_, 0, c)),
                  pl.BlockSpec((3, 3, LANE), lambda b_, c: (0, 0, c)),
                  pl.BlockSpec((1, LANE), lambda b_, c: (0, c))],
        out_specs=pl.BlockSpec((1, t, LANE), lambda b_, c: (b_, 0, c)),
        out_shape=jax.ShapeDtypeStruct((bsz, t, ch), BF16),
        compiler_params=_cparams(("parallel", "parallel")),
        name="conv",
    )(x, w, b)


def _positions():
    t = np.arange(CHUNK)
    return [t, CHUNK - 1 - t]


def _hgrn2_masks():
    ds, vs, ws = [], [], []
    for p in _positions():
        le = (p[None, :] <= p[:, None]).astype(np.float32)
        order = [p[SUB * j] // SUB for j in range(N_SUB)]
        ref = np.zeros((32, CHUNK), np.float32)
        valid = np.zeros((N_SUB * N_SUB, 1), np.float32)
        for j in range(N_SUB):
            ref[j] = p <= SUB * order[j] + SUB // 2 - 1
            ref[8 + j] = p <= SUB * order[j] + SUB - 1
        ref[N_SUB] = 1.0
        for j in range(N_SUB):
            for i in range(N_SUB):
                if order[i] < order[j]:
                    ref[16 + N_SUB * j + i] = ref[j] - ref[8 + i]
                    valid[N_SUB * j + i] = 1.0
        ds.append(np.concatenate([le, ref], axis=0))
        vs.append(np.broadcast_to(valid, (N_SUB * N_SUB, HG_HEADS * HG_DIM)))
        tok_blk = np.arange(CHUNK)[:, None] // SUB
        ws.append(np.concatenate([(tok_blk == j) * le for j in range(N_SUB)], axis=1))
    return (np.stack(ds).astype(np.float32), np.stack(vs).astype(np.float32),
            np.stack(ws).astype(np.float32))


def _mlstm_masks():
    le2s, let2s = [], []
    for p in _positions():
        le = (p[None, :] <= p[:, None]).astype(np.float32)
        le2s.append(np.concatenate([le, le], axis=1))
        let2 = np.zeros((2 * CHUNK, 2 * CHUNK), np.float32)
        let2[:CHUNK, :CHUNK] = le.T
        let2[CHUNK:, CHUNK:] = le.T
        let2s.append(let2)
    return np.stack(le2s), np.stack(let2s)


def _chunk_rows(c, nc, direction):
    cc = jnp.where(direction == 0, c, nc - 1 - c)
    return pl.multiple_of(cc * CHUNK, CHUNK)


def _hgrn2_kernel(q_ref, v_ref, lf_ref, d_ref, pv_ref, w_ref, s0_ref, o_ref, st_ref,
                  u_ref, qin_ref, g_ref, oi_ref, *, nc):
    direction = pl.program_id(1)

    @pl.when(pl.program_id(2) == 0)
    def _():
        st_ref[...] = s0_ref[...]

    dmat = d_ref[0].astype(BF16)
    pair_valid = pv_ref[0]
    wmask = w_ref[0] > 0.0
    L = CHUNK
    hw = q_ref.shape[-1]
    upper_rows = lax.broadcasted_iota(jnp.int32, (L, 2 * L), 0) < L // 2

    group = min(GROUP, nc)

    def intra(grp, carry):
        cs = [grp * group + i for i in range(group)]
        rows = [pl.ds(_chunk_rows(c, nc, direction), L) for c in cs]
        items = [(i, h) for i in range(group) for h in range(HG_HEADS)]
        lanes = [slice(h * HG_DIM, (h + 1) * HG_DIM) for h in range(HG_HEADS)]
        lfs = [lf_ref[0, 0, r, :] for r in rows]
        parts = [_split2(lf) for lf in lfs]
        e_all = sum(_dot(dmat, jnp.concatenate([parts[i][n] for i in range(group)], axis=1))
                    for n in range(2))
        q_mids, k_mids, k_outs, vbs = [], [], [], []
        for i in range(group):
            e = e_all[:, i * hw:(i + 1) * hw]
            q = q_ref[0, rows[i], :]
            k = 1.0 - jnp.exp2(lfs[i])
            b = e[0:L]
            b_tot = e[L + N_SUB:L + N_SUB + 1]
            sub_rows = lambda r0: jnp.concatenate(
                [jnp.broadcast_to(e[r0 + j:r0 + j + 1], (SUB, hw)) for j in range(N_SUB)], axis=0)
            mid_rows = sub_rows(L)
            end_rows = sub_rows(L + 8)
            q_mids.append((q * jnp.exp2(jnp.minimum(b - mid_rows, EXP2_CLAMP))).astype(BF16))
            k_diag = k * jnp.exp2(jnp.minimum(mid_rows - b, EXP2_CLAMP))
            k_end = k * jnp.exp2(end_rows - b)
            cross = jnp.exp2(e[L + 16:L + 32]) * pair_valid
            slabs = []
            for j in range(N_SUB):
                for ib in range(N_SUB):
                    rs = slice(ib * SUB, (ib + 1) * SUB)
                    row = N_SUB * j + ib
                    slabs.append(k_diag[rs] if ib == j else k_end[rs] * cross[row:row + 1])
            k_mids.append(jnp.concatenate(slabs, axis=0).astype(BF16))
            k_outs.append((k * jnp.exp2(b_tot - b)).astype(BF16))
            vbs.append(v_ref[0, rows[i], :])
            qin_ref[cs[i]] = (q * jnp.exp2(b)).astype(BF16)
            g_ref[cs[i]] = jnp.broadcast_to(jnp.exp2(b_tot), (8, hw))
        scores = [_dot_nt(q_mids[i][:, lanes[h]], k_mids[i][:, lanes[h]]) for i, h in items]
        a_s = [jnp.where(wmask, s, 0.0) for s in scores]
        a_s = [jnp.where(upper_rows, a[:, :2 * L], a[:, 2 * L:]).astype(BF16) for a in a_s]
        v2s = [jnp.concatenate([vb, vb], axis=0) for vb in vbs]
        outs = [_dot(a_s[n], v2s[i][:, lanes[h]]) for n, (i, h) in enumerate(items)]
        for i in range(group):
            oi_ref[cs[i]] = jnp.concatenate(outs[i * HG_HEADS:(i + 1) * HG_HEADS], axis=1)
        for i, h in items:
            u_ref[cs[i], h] = _dot_tn(vbs[i][:, lanes[h]], k_outs[i][:, lanes[h]])
        return carry

    def carry_state(c, carry):
        rows = pl.ds(_chunk_rows(c, nc, direction), L)
        q_in = qin_ref[c]
        g = g_ref[c][0:1]
        outs, states = [], []
        for h in range(HG_HEADS):
            ln = slice(h * HG_DIM, (h + 1) * HG_DIM)
            st = st_ref[0, 0, h]
            outs.append(_dot_nt(q_in[:, ln], st.astype(BF16)))
            states.append(g[:, ln] * st + u_ref[c, h])
        o_ref[0, 0, rows, :] = (oi_ref[c] + jnp.concatenate(outs, axis=1)).astype(BF16)
        for h in range(HG_HEADS):
            st_ref[0, 0, h] = states[h]
        return carry

    lax.fori_loop(0, nc // group, intra, 0)
    lax.fori_loop(0, nc, carry_state, 0, unroll=2)


def _hgrn2(q, v, lf, s0, dmat, pair_valid, wmask):
    bsz, t, hw = q.shape
    nc = min(8, t // CHUNK)
    tb = nc * CHUNK
    nblk = t // tb

    def blk(j, d):
        return jnp.where(d == 0, j, nblk - 1 - j)

    st_spec = pl.BlockSpec((1, 1) + s0.shape[2:], lambda b, d, j: (b, d, 0, 0, 0))
    return pl.pallas_call(
        functools.partial(_hgrn2_kernel, nc=nc),
        grid=(bsz, 2, nblk),
        in_specs=[pl.BlockSpec((1, tb, hw), lambda b, d, j: (b, blk(j, d), 0)),
                  pl.BlockSpec((1, tb, hw), lambda b, d, j: (b, blk(j, d), 0)),
                  pl.BlockSpec((1, 1, tb, hw), lambda b, d, j: (b, d, blk(j, d), 0)),
                  pl.BlockSpec((1,) + dmat.shape[1:], lambda b, d, j: (d, 0, 0)),
                  pl.BlockSpec((1,) + pair_valid.shape[1:], lambda b, d, j: (d, 0, 0)),
                  pl.BlockSpec((1,) + wmask.shape[1:], lambda b, d, j: (d, 0, 0)),
                  st_spec],
        out_specs=[pl.BlockSpec((1, 1, tb, hw), lambda b, d, j: (b, d, blk(j, d), 0)), st_spec],
        out_shape=[jax.ShapeDtypeStruct((bsz, 2, t, hw), BF16),
                   jax.ShapeDtypeStruct(s0.shape, F32)],
        scratch_shapes=[pltpu.VMEM((nc,) + s0.shape[2:], F32),
                        pltpu.VMEM((nc, CHUNK, hw), BF16),
                        pltpu.VMEM((nc, 8, hw), F32),
                        pltpu.VMEM((nc, CHUNK, hw), F32)],
        compiler_params=_cparams(("parallel", "parallel", "arbitrary")),
        name="hgrn2_scan",
    )(q, v, lf, dmat, pair_valid, wmask, s0)


def _mlstm_kernel(qk_ref, v_ref, g_ref, le2_ref, let2_ref, c0_ref, m0_ref,
                  o_ref, c_ref, mm_ref,
                  p_ref, u_ref, qs_ref, ct_ref, mloc_ref, sc_ref, *, nc):
    direction = pl.program_id(1)

    @pl.when(pl.program_id(2) == 0)
    def _():
        c_ref[...] = c0_ref[...]
        mm_ref[...] = m0_ref[...]

    le2 = le2_ref[0].astype(BF16)
    let2 = let2_ref[0].astype(BF16)
    causal = le2_ref[0] > 0.0
    L = CHUNK
    neg = -jnp.inf
    lane = lax.broadcasted_iota(jnp.int32, (1, LANE), 1)
    seg = [lane < L, lane >= L]
    lane_t = lax.broadcasted_iota(jnp.int32, (L, LANE), 1) < L
    eye = (lax.broadcasted_iota(jnp.int32, (LANE, LANE), 0)
           == lax.broadcasted_iota(jnp.int32, (LANE, LANE), 1))
    npair = ML_HEADS // 2
    ones_v = jnp.ones((L, ML_V), BF16)

    group = min(ML_GROUP, nc)

    def intra(grp, carry):
        cs = [grp * group + i for i in range(group)]
        ccs = [jnp.where(direction == 0, c, nc - 1 - c) for c in cs]
        rows = [pl.ds(pl.multiple_of(cc * L, L), L) for cc in ccs]
        pairs = [(i, p) for i in range(group) for p in range(npair)]
        gs = [jnp.where(direction == 0, g_ref[0, cc][0:4], g_ref[0, cc][4:8]) for cc in ccs]
        i2 = jnp.concatenate([g[0:2] for g in gs], axis=0)
        lf2 = _log_sigmoid(jnp.concatenate([g[2:4] for g in gs], axis=0))
        parts = _split3(lf2)
        r2 = i2 - sum(_dot(x, let2) for x in parts)
        g_a = jnp.max(jnp.where(seg[0], r2, neg), axis=1, keepdims=True)
        g_b = jnp.max(jnp.where(seg[1], r2, neg), axis=1, keepdims=True)
        e_w = jnp.exp(r2 - jnp.where(seg[0], g_a, g_b))
        bt_a = jnp.sum(jnp.where(seg[0], lf2, 0.0), axis=1, keepdims=True)
        bt_b = jnp.sum(jnp.where(seg[1], lf2, 0.0), axis=1, keepdims=True)

        def head_rows(x, i):
            zero = jnp.zeros_like(x[0:1])
            return jnp.concatenate(
                [jnp.broadcast_to(jnp.where(seg[h % 2], x[2 * i + h // 2:2 * i + h // 2 + 1], zero),
                                  (LANE, LANE)) for h in range(ML_HEADS)], axis=0)

        c_ts = [sum(_dot_nt(le2, head_rows(x, i)) for x in parts) for i in range(group)]
        for i in range(group):
            ct_ref[cs[i]] = c_ts[i]
            sc_ref[cs[i]] = jnp.concatenate(
                [jnp.broadcast_to(x[2 * i + p:2 * i + p + 1], (1, LANE))
                 for x2 in ((bt_a, bt_b), (g_a, g_b)) for p in range(npair) for x in x2], axis=0)
        qks = [qk_ref[0, r, :] for r in rows]
        vbs = [v_ref[0, r, :] for r in rows]
        q_ps = [qks[i][:, p * LANE:(p + 1) * LANE] for i, p in pairs]
        k_ps = [qks[i][:, (npair + p) * LANE:(npair + p + 1) * LANE] for i, p in pairs]
        zb = jnp.zeros((L, LANE), BF16)
        kbs = [jnp.concatenate([jnp.where(lane_t, k, zb), jnp.where(lane_t, zb, k)], axis=0) for k in k_ps]
        scores = [_dot_nt(q, kb) for q, kb in zip(q_ps, kbs)]
        ws = []
        for n, (i, p) in enumerate(pairs):
            ha, hb = 2 * p, 2 * p + 1
            t1 = jnp.where(lane_t, c_ts[i][:, ha * ML_V:(ha + 1) * ML_V], c_ts[i][:, hb * ML_V:(hb + 1) * ML_V])
            d_log = jnp.where(causal, t1 + r2[2 * i + p:2 * i + p + 1], neg)
            m_a = jnp.max(jnp.where(lane_t, d_log, neg), axis=1, keepdims=True)
            m_b = jnp.max(jnp.where(lane_t, neg, d_log), axis=1, keepdims=True)
            mloc_ref[cs[i], :, ha * ML_V:(ha + 1) * ML_V] = jnp.broadcast_to(m_a, (L, ML_V))
            mloc_ref[cs[i], :, hb * ML_V:(hb + 1) * ML_V] = jnp.broadcast_to(m_b, (L, ML_V))
            ws.append(jnp.exp(d_log - jnp.where(lane_t, m_a, m_b)))
        a_s = [(s_ * w).astype(BF16) for s_, w in zip(scores, ws)]
        a_stacks = [jnp.concatenate([jnp.where(lane_t, a, zb), jnp.where(lane_t, zb, a)], axis=0) for a in a_s]
        v_stacks = [jnp.concatenate(
            [jnp.concatenate([vbs[i][:, 2 * p * ML_V:(2 * p + 1) * ML_V], ones_v], axis=1),
             jnp.concatenate([vbs[i][:, (2 * p + 1) * ML_V:(2 * p + 2) * ML_V], ones_v], axis=1)], axis=0)
            for i, p in pairs]
        for n, (i, p) in enumerate(pairs):
            p_ref[cs[i], p] = _dot(a_stacks[n], v_stacks[n])
            qs_ref[cs[i], p] = jnp.concatenate(
                [jnp.where(lane_t, q_ps[n], zb), jnp.where(lane_t, zb, q_ps[n])], axis=0)
        diags = [jnp.where(eye, jnp.broadcast_to(e_w[2 * i + p:2 * i + p + 1], (LANE, LANE)), 0.0).astype(BF16)
                 for i, p in pairs]
        kb_ws = [_dot(d, kb).astype(BF16) for d, kb in zip(diags, kbs)]
        for n, (i, p) in enumerate(pairs):
            u_ref[cs[i], p] = _dot_tn(kb_ws[n], v_stacks[n])
        return carry

    def carry_state(c, carry):
        cc = jnp.where(direction == 0, c, nc - 1 - c)
        rows = pl.ds(pl.multiple_of(cc * L, L), L)
        m_all = mm_ref[0, 0]
        sc = sc_ref[c]
        c_t = ct_ref[c]
        m_loc = mloc_ref[c]
        outs, states, m_rows = [], [], []
        for p in range(ML_HEADS // 2):
            c_pair = c_ref[0, 0, p]
            inter = _dot(qs_ref[c, p], c_pair.astype(BF16))
            intra_p = p_ref[c, p]
            inc = u_ref[c, p]
            new_rows = []
            for hh in range(2):
                h = 2 * p + hh
                rs = slice(hh * L, (hh + 1) * L)
                ln = slice(h * ML_V, (h + 1) * ML_V)
                m_in = m_all[h:h + 1]
                m_t = jnp.maximum(m_loc[:, ln], c_t[:, ln] + m_in)
                alpha = jnp.exp(m_loc[:, ln] - m_t)
                beta = jnp.exp(c_t[:, ln] + m_in - m_t)
                num = alpha * intra_p[rs, :ML_V] + beta * inter[rs, :ML_V]
                den = alpha * intra_p[rs, ML_V:] + beta * inter[rs, ML_V:]
                outs.append(num / jnp.maximum(jnp.abs(den), jnp.exp(-m_t)))
                b_tot = sc[h:h + 1]
                m_new = b_tot + jnp.maximum(m_in, sc[ML_HEADS + h:ML_HEADS + h + 1])
                carry_w = jnp.exp(b_tot + m_in - m_new)
                inc_w = jnp.exp(b_tot + sc[ML_HEADS + h:ML_HEADS + h + 1] - m_new)
                cw2 = jnp.concatenate([carry_w, carry_w], axis=1)
                iw2 = jnp.concatenate([inc_w, inc_w], axis=1)
                new_rows.append(cw2 * c_pair[rs] + iw2 * inc[rs])
                m_rows.append(m_new)
            states.append(jnp.concatenate(new_rows, axis=0))
        o_ref[0, 0, rows, :] = jnp.concatenate(outs, axis=1).astype(BF16)
        for p in range(ML_HEADS // 2):
            c_ref[0, 0, p] = states[p]
        mm_ref[0, 0] = jnp.concatenate(m_rows + [m_all[ML_HEADS:]], axis=0)
        return carry

    lax.fori_loop(0, nc // group, intra, 0)
    lax.fori_loop(0, nc, carry_state, 0, unroll=2)


def _mlstm(qk, v, gates, c0, m0, le2, let2):
    bsz, t, hw = v.shape
    nc = min(8, t // CHUNK)
    tb = nc * CHUNK
    nblk = t // tb
    npair = ML_HEADS // 2

    def blk(j, d):
        return jnp.where(d == 0, j, nblk - 1 - j)

    c_spec = pl.BlockSpec((1, 1) + c0.shape[2:], lambda b, d, j: (b, d, 0, 0, 0))
    m_spec = pl.BlockSpec((1, 1) + m0.shape[2:], lambda b, d, j: (b, d, 0, 0))
    return pl.pallas_call(
        functools.partial(_mlstm_kernel, nc=nc),
        grid=(bsz, 2, nblk),
        in_specs=[pl.BlockSpec((1, tb, qk.shape[-1]), lambda b, d, j: (b, blk(j, d), 0)),
                  pl.BlockSpec((1, tb, hw), lambda b, d, j: (b, blk(j, d), 0)),
                  pl.BlockSpec((1, nc) + gates.shape[2:], lambda b, d, j: (b, blk(j, d), 0, 0)),
                  pl.BlockSpec((1,) + le2.shape[1:], lambda b, d, j: (d, 0, 0)),
                  pl.BlockSpec((1,) + let2.shape[1:], lambda b, d, j: (d, 0, 0)),
                  c_spec, m_spec],
        out_specs=[pl.BlockSpec((1, 1, tb, hw), lambda b, d, j: (b, d, blk(j, d), 0)), c_spec, m_spec],
        out_shape=[jax.ShapeDtypeStruct((bsz, 2, t, hw), BF16),
                   jax.ShapeDtypeStruct(c0.shape, F32),
                   jax.ShapeDtypeStruct(m0.shape, F32)],
        scratch_shapes=[pltpu.VMEM((nc, npair, 2 * CHUNK, 2 * ML_V), F32),
                        pltpu.VMEM((nc, npair, 2 * ML_QK, 2 * ML_V), F32),
                        pltpu.VMEM((nc, npair, 2 * CHUNK, LANE), BF16),
                        pltpu.VMEM((nc, CHUNK, hw), F32),
                        pltpu.VMEM((nc, CHUNK, hw), F32),
                        pltpu.VMEM((nc, 8, LANE), F32)],
        compiler_params=_cparams(("parallel", "parallel", "arbitrary")),
        name="mlstm_scan",
    )(qk, v, gates, le2, let2, c0, m0)


def _head_rms(x, heads):
    dh = x.shape[-1] // heads
    return jnp.concatenate([_rms(x[:, h * dh:(h + 1) * dh]) for h in range(heads)], axis=1)


def _mix_ffn_kernel(oh_ref, om_ref, hg_ref, mo_ref, x_ref, mod_ref, hnw_ref, mnw_ref, wo_ref,
                    nw_ref, w1_ref, w2_ref, fw_ref, o_ref, a_ref, *, d_ff, tf):
    mix_gate = mod_ref[0, 0:1, :]
    mods = [mod_ref[0, k:k + 1, :] for k in range(1, 4)]
    for r0 in range(0, x_ref.shape[1], ROW_SUB):
        rs = slice(r0, r0 + ROW_SUB)
        oh = oh_ref[0, 0, rs, :].astype(F32) + oh_ref[0, 1, rs, :].astype(F32)
        om = om_ref[0, 0, rs, :].astype(F32) + om_ref[0, 1, rs, :].astype(F32)
        hg_out = _head_rms(oh, HG_HEADS) * hnw_ref[...] * _silu(hg_ref[0, rs, :].astype(F32))
        ml_out = _sigmoid(mo_ref[0, rs, :].astype(F32)) * (_head_rms(om, ML_HEADS) * mnw_ref[...])
        merged = jnp.concatenate([hg_out, ml_out], axis=1).astype(BF16)
        x = x_ref[0, rs, :] + mix_gate * _dot(merged, wo_ref[...])
        y = _ffn_rows(x, mods, nw_ref, w1_ref, w2_ref, a_ref, rs, d_ff, tf)
        o_ref[0, rs, :] = _rms(y) * fw_ref[...]


def _mix_ffn(oh, om, hg, mo, x, mods, hnw, mnw, w_out, nw, w1, w2, fw):
    bsz, t, d = x.shape
    hw = hg.shape[-1]
    d_ff = w2.shape[0]
    tm = min(TOKEN_TILE, t)
    tok = lambda b, i: (b, i, 0)
    tok2 = lambda b, i: (b, 0, i, 0)
    const = lambda b, i: (0, 0)
    resident = lambda a: pl.BlockSpec(a.shape, const, pipeline_mode=pl.Buffered(1))
    return pl.pallas_call(
        functools.partial(_mix_ffn_kernel, d_ff=d_ff, tf=FF_TILE),
        grid=(bsz, t // tm),
        in_specs=[pl.BlockSpec((1, 2, tm, hw), tok2), pl.BlockSpec((1, 2, tm, hw), tok2),
                  pl.BlockSpec((1, tm, hw), tok), pl.BlockSpec((1, tm, hw), tok),
                  pl.BlockSpec((1, tm, d), tok),
                  pl.BlockSpec((1, 4, d), lambda b, i: (b, 0, 0)),
                  pl.BlockSpec((1, hw), const), pl.BlockSpec((1, hw), const), resident(w_out),
                  pl.BlockSpec((1, d), const), resident(w1), resident(w2), pl.BlockSpec((1, d), const)],
        out_specs=pl.BlockSpec((1, tm, d), tok),
        out_shape=jax.ShapeDtypeStruct((bsz, t, d), F32),
        scratch_shapes=[pltpu.VMEM((tm, d_ff), BF16)],
        compiler_params=_cparams(("parallel", "parallel")),
        name="mix_ffn_final",
    )(oh, om, hg, mo, x, mods, hnw, mnw, w_out, nw, w1, w2, fw)


def kernel(x, c, ctx, c_ctx, w_mod, b_mod, norm1_w, ffn1_w1, ffn1_w2, norm2_w, w_in, ml_gate_b,
           ml_conv_w, ml_conv_b, hg_lb_logits, hg_norm_w, ml_norm_w, w_out, norm3_w, ffn2_w1, ffn2_w2,
           final_norm_w):
    bsz, seq, d = x.shape
    n_ctx = ctx.shape[1]
    assert w_mod.shape[0] == 1, "single-layer kernel"
    assert seq % (GRID_W * 8) == 0 and n_ctx % CHUNK == 0
    hw = HG_HEADS * HG_DIM
    ng = ml_gate_b.shape[-1]

    rows = -(-(bsz + 1) // 8) * 8
    cvec = jnp.zeros((rows, d), F32).at[:bsz].set(c).at[bsz].set(c_ctx)
    mods = _modulation(cvec, w_mod[0], b_mod[0][None, :]).reshape(rows, N_MOD, d)
    lat_row = lambda b: b
    ctx_row = lambda b: bsz

    row = lambda a: a.reshape(1, -1)
    w1a, w2a = ffn1_w1[0].astype(BF16), ffn1_w2[0].astype(BF16)
    x1 = _ffn(x, mods[:, 0:3], lat_row, row(norm1_w[0]), w1a, w2a)
    s1 = _ffn(ctx, mods[:, 0:3], ctx_row, row(norm1_w[0]), w1a, w2a)

    w_in_b = w_in[0].astype(BF16)
    gate_b = row(ml_gate_b[0])
    dmat, pair_valid, wmask = (jnp.asarray(a) for a in _hgrn2_masks())
    le2, let2 = (jnp.asarray(a) for a in _mlstm_masks())
    conv_w, conv_b = ml_conv_w[0], row(ml_conv_b[0])
    npair = ML_HEADS // 2

    def mixer_scans(s, mod_row, width, s0, c0, m0):
        hq, hv, hg, hlf, mqk, mv, mo, gcol = _inproj(
            s, mods[:, 3:5], mod_row, row(norm2_w[0]), w_in_b, gate_b, hg_lb_logits)
        n_chunks = s.shape[1] // CHUNK
        qk = _conv(mqk, conv_w, conv_b, width)
        gates = gcol.reshape(bsz, n_chunks, CHUNK, 2, 2, npair, 2).transpose(0, 1, 3, 4, 5, 6, 2)
        gates = gates.reshape(bsz, n_chunks, 8, 2 * CHUNK)
        oh, s_fin = _hgrn2(hq, hv, hlf, s0, dmat, pair_valid, wmask)
        om, c_fin, m_fin = _mlstm(qk, mv, gates, c0, m0, le2, let2)
        return oh, om, hg, mo, s_fin, c_fin, m_fin

    s0 = jnp.zeros((bsz, 2, HG_HEADS, HG_DIM, HG_DIM), F32)
    c0 = jnp.zeros((bsz, 2, npair, 2 * ML_QK, 2 * ML_V), F32)
    m0 = jnp.zeros((bsz, 2, 8, LANE), F32)
    _, _, _, _, s_ctx, c_ctx_state, m_ctx = mixer_scans(s1, ctx_row, n_ctx, s0, c0, m0)
    oh, om, hg, mo, _, _, _ = mixer_scans(x1, lat_row, GRID_W, s_ctx, c_ctx_state, m_ctx)

    return _mix_ffn(oh, om, hg, mo, x1, mods[:, 5:9], row(hg_norm_w[0]), row(ml_norm_w[0]),
                    w_out[0].astype(BF16), row(norm3_w[0]), ffn2_w1[0].astype(BF16),
                    ffn2_w2[0].astype(BF16), row(final_norm_w))
```

```python
import functools

import numpy as np
import jax
import jax.numpy as jnp
from jax import lax
from jax.experimental import pallas as pl
from jax.experimental.pallas import tpu as pltpu

F32 = jnp.float32
BF16 = jnp.bfloat16

EPS = 1e-6
CHUNK = 64
GRID_W = 64
N_MOD = 9
HG_HEADS = 4
HG_DIM = 128
ML_HEADS = 4
ML_QK = 64
ML_V = 128
SUB = 16
N_SUB = CHUNK // SUB
GROUP = 4
ML_GROUP = 8
TOKEN_TILE = 512
ROW_SUB = 256
FF_TILE = 256
LOG2E = 1.4426950408889634
EXP2_CLAMP = 115.0
LANE = 128
VMEM_LIMIT = 56 * 1024 * 1024


def _cparams(sem):
    return pltpu.CompilerParams(dimension_semantics=sem, vmem_limit_bytes=VMEM_LIMIT)


def _dot(a, b):
    return jnp.dot(a, b, preferred_element_type=F32)


def _dot_nt(a, b):
    return lax.dot_general(a, b, (((1,), (1,)), ((), ())), preferred_element_type=F32)


def _dot_tn(a, b):
    return lax.dot_general(a, b, (((0,), (0,)), ((), ())), preferred_element_type=F32)


def _sigmoid(x):
    return 1.0 / (1.0 + jnp.exp(-x))


def _silu(x):
    return x * _sigmoid(x)


def _log_sigmoid(x):
    return jnp.minimum(x, 0.0) - jnp.log(1.0 + jnp.exp(-jnp.abs(x)))


def _split3(x):
    hi = x.astype(BF16)
    r = x - hi.astype(F32)
    mid = r.astype(BF16)
    lo = (r - mid.astype(F32)).astype(BF16)
    return hi, mid, lo


def _split2(x):
    hi = x.astype(BF16)
    return hi, (x - hi.astype(F32)).astype(BF16)


def _rms(x):
    return x * lax.rsqrt(jnp.mean(x * x, axis=-1, keepdims=True) + EPS)


def _mod_kernel(c_ref, w_ref, b_ref, o_ref):
    a = _silu(c_ref[...])
    a_hi = a.astype(BF16)
    a_lo = (a - a_hi.astype(F32)).astype(BF16)
    w = w_ref[...]
    w_hi = w.astype(BF16)
    w_lo = (w - w_hi.astype(F32)).astype(BF16)
    o_ref[...] = _dot(a_hi, w_hi) + _dot(a_hi, w_lo) + _dot(a_lo, w_hi) + b_ref[...]


def _modulation(cvec, w_mod, b_mod):
    rows, d = cvec.shape
    n = w_mod.shape[1]
    tn = 1024
    return pl.pallas_call(
        _mod_kernel,
        grid=(n // tn,),
        in_specs=[pl.BlockSpec((rows, d), lambda j: (0, 0)),
                  pl.BlockSpec((d, tn), lambda j: (0, j)),
                  pl.BlockSpec((1, tn), lambda j: (0, j))],
        out_specs=pl.BlockSpec((rows, tn), lambda j: (0, j)),
        out_shape=jax.ShapeDtypeStruct((rows, n), F32),
        compiler_params=_cparams(("arbitrary",)),
        name="modulation",
    )(cvec, w_mod, b_mod)


def _ffn_rows(x, mods, nw_ref, w1_ref, w2_ref, a_ref, rs, d_ff, tf):
    shift, scale, gate = mods
    h = (_rms(x) * nw_ref[...] * (1.0 + scale) + shift).astype(BF16)
    for c in range(d_ff // tf):
        g = _dot(h, w1_ref[:, c * tf:(c + 1) * tf])
        u = _dot(h, w1_ref[:, d_ff + c * tf:d_ff + (c + 1) * tf])
        a_ref[rs, c * tf:(c + 1) * tf] = (_silu(g) * u).astype(BF16)
    return x + 0.5 * gate * _dot(a_ref[rs, :], w2_ref[...])


def _ffn_kernel(x_ref, mod_ref, nw_ref, w1_ref, w2_ref, o_ref, a_ref, *, d_ff, tf):
    mods = [mod_ref[0, k:k + 1, :] for k in range(3)]
    for r0 in range(0, x_ref.shape[1], ROW_SUB):
        rs = slice(r0, r0 + ROW_SUB)
        o_ref[0, rs, :] = _ffn_rows(x_ref[0, rs, :], mods, nw_ref, w1_ref, w2_ref, a_ref, rs, d_ff, tf)


def _ffn(s, mods, mod_row, nw, w1, w2):
    bsz, t, d = s.shape
    d_ff = w2.shape[0]
    tm = min(TOKEN_TILE, t)
    const = lambda b, i: (0, 0)
    return pl.pallas_call(
        functools.partial(_ffn_kernel, d_ff=d_ff, tf=FF_TILE),
        grid=(bsz, t // tm),
        in_specs=[pl.BlockSpec((1, tm, d), lambda b, i: (b, i, 0)),
                  pl.BlockSpec((1, 3, d), lambda b, i: (mod_row(b), 0, 0)),
                  pl.BlockSpec((1, d), const),
                  pl.BlockSpec((d, 2 * d_ff), const, pipeline_mode=pl.Buffered(1)),
                  pl.BlockSpec((d_ff, d), const, pipeline_mode=pl.Buffered(1))],
        out_specs=pl.BlockSpec((1, tm, d), lambda b, i: (b, i, 0)),
        out_shape=jax.ShapeDtypeStruct((bsz, t, d), F32),
        scratch_shapes=[pltpu.VMEM((tm, d_ff), BF16)],
        compiler_params=_cparams(("parallel", "parallel")),
        name="ffn",
    )(s, mods, nw, w1, w2)


def _inproj_kernel(x_ref, mod_ref, nw_ref, w_ref, gb_ref, lbl_ref,
                   hq_ref, hv_ref, hg_ref, hlf_ref, mqk_ref, mv_ref, mo_ref, gc_ref, *, hw):
    shift = mod_ref[0, 0:1, :]
    scale = mod_ref[0, 1:2, :]
    lbl = lbl_ref[...]
    e = jnp.exp(lbl - jnp.max(lbl, axis=0, keepdims=True))
    lb = e[0] / jnp.sum(e, axis=0)
    ng = gc_ref.shape[-1]
    for r0 in range(0, x_ref.shape[1], ROW_SUB):
        rs = slice(r0, r0 + ROW_SUB)
        h = (_rms(x_ref[0, rs, :]) * nw_ref[...] * (1.0 + scale) + shift).astype(BF16)

        def proj(k):
            return _dot(h, w_ref[:, k * hw:(k + 1) * hw])

        hq_ref[0, rs, :] = _silu(proj(0)) * (HG_DIM ** -0.5)
        hv_ref[0, rs, :] = proj(1).astype(BF16)
        hg_ref[0, rs, :] = proj(2).astype(BF16)
        for d in range(2):
            lbd = lb[d:d + 1, :]
            hlf_ref[0, d, rs, :] = jnp.log(lbd + (1.0 - lbd) * _sigmoid(proj(3 + d))) * LOG2E
        mqk_ref[0, rs, :] = proj(5).astype(BF16)
        mv_ref[0, rs, :] = proj(6).astype(BF16)
        mo_ref[0, rs, :] = proj(7).astype(BF16)
        gc_ref[0, rs, :] = _dot(h, w_ref[:, 8 * hw:8 * hw + ng]) + gb_ref[...]


def _inproj(s, mods, mod_row, nw, w_in, gate_b, lb_logits):
    bsz, t, d = s.shape
    hw = HG_HEADS * HG_DIM
    ng = gate_b.shape[-1]
    tm = min(TOKEN_TILE, t)
    const = lambda b, i: (0, 0)
    tok = lambda b, i: (b, i, 0)
    tok2 = lambda b, i: (b, 0, i, 0)
    f = lambda dt, *shape: jax.ShapeDtypeStruct(shape, dt)
    return pl.pallas_call(
        functools.partial(_inproj_kernel, hw=hw),
        grid=(bsz, t // tm),
        in_specs=[pl.BlockSpec((1, tm, d), tok),
                  pl.BlockSpec((1, 2, d), lambda b, i: (mod_row(b), 0, 0)),
                  pl.BlockSpec((1, d), const),
                  pl.BlockSpec(w_in.shape, const, pipeline_mode=pl.Buffered(1)),
                  pl.BlockSpec((1, ng), const),
                  pl.BlockSpec(lb_logits.shape, lambda b, i: (0, 0, 0))],
        out_specs=[pl.BlockSpec((1, tm, hw), tok), pl.BlockSpec((1, tm, hw), tok),
                   pl.BlockSpec((1, tm, hw), tok),
                   pl.BlockSpec((1, 2, tm, hw), tok2),
                   pl.BlockSpec((1, tm, hw), tok), pl.BlockSpec((1, tm, hw), tok),
                   pl.BlockSpec((1, tm, hw), tok), pl.BlockSpec((1, tm, ng), tok)],
        out_shape=[f(F32, bsz, t, hw), f(BF16, bsz, t, hw), f(BF16, bsz, t, hw),
                   f(F32, bsz, 2, t, hw),
                   f(BF16, bsz, t, hw), f(BF16, bsz, t, hw), f(BF16, bsz, t, hw), f(F32, bsz, t, ng)],
        compiler_params=_cparams(("parallel", "parallel")),
        name="inproj",
    )(s, mods, nw, w_in, gate_b, lb_logits)


def _conv_kernel(x_ref, w_ref, b_ref, o_ref, *, width, q_tiles):
    x = x_ref[0].astype(F32)
    t = x.shape[0]
    col = lax.broadcasted_iota(jnp.int32, x.shape, 0) % width
    xl = jnp.where(col == 0, 0.0, pltpu.roll(x, 1, 0))
    xr = jnp.where(col == width - 1, 0.0, pltpu.roll(x, t - 1, 0))

    def row(di):
        return (w_ref[di, 0:1, :] * xl + w_ref[di, 1:2, :] * x + w_ref[di, 2:3, :] * xr)

    y = row(1)
    if t > width:
        pad = jnp.zeros((width, x.shape[1]), F32)
        y = y + jnp.concatenate([pad, row(0)[:t - width]], axis=0)
        y = y + jnp.concatenate([row(2)[width:], pad], axis=0)
    y = _silu(y + b_ref[...])
    qscale = jnp.where(pl.program_id(1) < q_tiles, ML_QK ** -0.5, 1.0)
    o_ref[0] = (y * qscale).astype(BF16)


def _conv(x, w, b, width):
    bsz, t, ch = x.shape
    return pl.pallas_call(
        functools.partial(_conv_kernel, width=width, q_tiles=ML_HEADS * ML_QK // LANE),
        grid=(bsz, ch // LANE),
        in_specs=[pl.BlockSpec((1, t, LANE), lambda b_, c: (b_, 0, c)),
                  pl.BlockSpec((3, 3, LANE), lambda b_, c: (0, 0, c)),
                  pl.BlockSpec((1, LANE), lambda b_, c: (0, c))],
        out_specs=pl.BlockSpec((1, t, LANE), lambda b_, c: (b_, 0, c)),
        out_shape=jax.ShapeDtypeStruct((bsz, t, ch), BF16),
        compiler_params=_cparams(("parallel", "parallel")),
        name="conv",
    )(x, w, b)


def _positions():
    t = np.arange(CHUNK)
    return [t, CHUNK - 1 - t]


def _hgrn2_masks():
    ds, vs, ws = [], [], []
    for p in _positions():
        le = (p[None, :] <= p[:, None]).astype(np.float32)
        order = [p[SUB * j] // SUB for j in range(N_SUB)]
        ref = np.zeros((32, CHUNK), np.float32)
        valid = np.zeros((N_SUB * N_SUB, 1), np.float32)
        for j in range(N_SUB):
            ref[j] = p <= SUB * order[j] + SUB // 2 - 1
            ref[8 + j] = p <= SUB * order[j] + SUB - 1
        ref[N_SUB] = 1.0
        for j in range(N_SUB):
            for i in range(N_SUB):
                if order[i] < order[j]:
                    ref[16 + N_SUB * j + i] = ref[j] - ref[8 + i]
                    valid[N_SUB * j + i] = 1.0
        ds.append(np.concatenate([le, ref], axis=0))
        vs.append(np.broadcast_to(valid, (N_SUB * N_SUB, HG_HEADS * HG_DIM)))
        tok_blk = np.arange(CHUNK)[:, None] // SUB
        ws.append(np.concatenate([(tok_blk == j) * le for j in range(N_SUB)], axis=1))
    return (np.stack(ds).astype(np.float32), np.stack(vs).astype(np.float32),
            np.stack(ws).astype(np.float32))


def _mlstm_masks():
    le2s, let2s = [], []
    for p in _positions():
        le = (p[None, :] <= p[:, None]).astype(np.float32)
        le2s.append(np.concatenate([le, le], axis=1))
        let2 = np.zeros((2 * CHUNK, 2 * CHUNK), np.float32)
        let2[:CHUNK, :CHUNK] = le.T
        let2[CHUNK:, CHUNK:] = le.T
        let2s.append(let2)
    return np.stack(le2s), np.stack(let2s)


def _chunk_rows(c, nc, direction):
    cc = jnp.where(direction == 0, c, nc - 1 - c)
    return pl.multiple_of(cc * CHUNK, CHUNK)


def _hgrn2_kernel(q_ref, v_ref, lf_ref, d_ref, pv_ref, w_ref, s0_ref, o_ref, st_ref,
                  u_ref, qin_ref, g_ref, oi_ref, bx_ref, kx_ref, vx_ref, *, nc):
    direction = pl.program_id(1)

    @pl.when(pl.program_id(2) == 0)
    def _():
        st_ref[...] = s0_ref[...]

    dmat = d_ref[0].astype(BF16)
    pair_valid = pv_ref[0]
    wmask = w_ref[0] > 0.0
    L = CHUNK
    hw = q_ref.shape[-1]
    upper_rows = lax.broadcasted_iota(jnp.int32, (L, 2 * L), 0) < L // 2

    group = min(GROUP, nc)

    def intra(grp, carry):
        cs = [grp * group + i for i in range(group)]
        rows = [pl.ds(_chunk_rows(c, nc, direction), L) for c in cs]
        items = [(i, h) for i in range(group) for h in range(HG_HEADS)]
        lanes = [slice(h * HG_DIM, (h + 1) * HG_DIM) for h in range(HG_HEADS)]
        lfs = [lf_ref[0, 0, r, :] for r in rows]
        parts = [_split2(lf) for lf in lfs]
        e_all = sum(_dot(dmat, jnp.concatenate([parts[i][n] for i in range(group)], axis=1))
                    for n in range(2))
        q_mids, k_mids, k_outs, vbs, spans = [], [], [], [], []
        for i in range(group):
            e = e_all[:, i * hw:(i + 1) * hw]
            q = q_ref[0, rows[i], :]
            k = 1.0 - jnp.exp2(lfs[i])
            b = e[0:L]
            b_tot = e[L + N_SUB:L + N_SUB + 1]
            sub_rows = lambda r0: jnp.concatenate(
                [jnp.broadcast_to(e[r0 + j:r0 + j + 1], (SUB, hw)) for j in range(N_SUB)], axis=0)
            mid_rows = sub_rows(L)
            end_rows = sub_rows(L + 8)
            to_mid = b - mid_rows
            spans.append(jnp.max(jnp.abs(to_mid)))
            q_mids.append((q * jnp.exp2(jnp.minimum(to_mid, EXP2_CLAMP))).astype(BF16))
            k_diag = k * jnp.exp2(jnp.minimum(-to_mid, EXP2_CLAMP))
            k_end = k * jnp.exp2(end_rows - b)
            cross = jnp.exp2(e[L + 16:L + 32]) * pair_valid
            slabs = []
            for j in range(N_SUB):
                for ib in range(N_SUB):
                    rs = slice(ib * SUB, (ib + 1) * SUB)
                    row = N_SUB * j + ib
                    slabs.append(k_diag[rs] if ib == j else k_end[rs] * cross[row:row + 1])
            k_mids.append(jnp.concatenate(slabs, axis=0).astype(BF16))
            k_outs.append((k * jnp.exp2(b_tot - b)).astype(BF16))
            vbs.append(v_ref[0, rows[i], :])
            qin_ref[cs[i]] = (q * jnp.exp2(b)).astype(BF16)
            g_ref[cs[i]] = jnp.broadcast_to(jnp.exp2(b_tot), (8, hw))
        scores = [_dot_nt(q_mids[i][:, lanes[h]], k_mids[i][:, lanes[h]]) for i, h in items]
        a_s = [jnp.where(wmask, s, 0.0) for s in scores]
        a_s = [jnp.where(upper_rows, a[:, :2 * L], a[:, 2 * L:]).astype(BF16) for a in a_s]
        v2s = [jnp.concatenate([vb, vb], axis=0) for vb in vbs]
        outs = [_dot(a_s[n], v2s[i][:, lanes[h]]) for n, (i, h) in enumerate(items)]
        for i in range(group):
            oi_ref[cs[i]] = jnp.concatenate(outs[i * HG_HEADS:(i + 1) * HG_HEADS], axis=1)
        for i, h in items:
            u_ref[cs[i], h] = _dot_tn(vbs[i][:, lanes[h]], k_outs[i][:, lanes[h]])

        @pl.when(functools.reduce(jnp.maximum, spans) > EXP2_CLAMP)
        def _():
            for i in range(group):
                exact_intra(cs[i], rows[i])
        return carry

    def exact_intra(c, rows):
        lf = lf_ref[0, 0, rows, :]
        bx_ref[...] = sum(_dot(dmat[0:L], x) for x in _split3(lf))
        kx_ref[...] = 1.0 - jnp.exp2(lf)
        vx_ref[...] = v_ref[0, rows, :].astype(F32)
        q = q_ref[0, rows, :]
        tok = lax.broadcasted_iota(jnp.int32, (L, 1), 0)
        pos = jnp.where(direction == 0, tok, L - 1 - tok)

        def add_source(s, acc):
            b_s = bx_ref[pl.ds(s, 1), :]
            z = q * kx_ref[pl.ds(s, 1), :] * jnp.exp2(jnp.minimum(bx_ref[...] - b_s, 0.0))
            v_s = vx_ref[pl.ds(s, 1), :]
            seen = pos >= jnp.where(direction == 0, s, L - 1 - s)
            cols = [jnp.where(seen, jnp.sum(z[:, ln], axis=1, keepdims=True), 0.0) * v_s[:, ln]
                    for ln in (slice(h * HG_DIM, (h + 1) * HG_DIM) for h in range(HG_HEADS))]
            return acc + jnp.concatenate(cols, axis=1)

        oi_ref[c] = lax.fori_loop(0, L, add_source, jnp.zeros((L, hw), F32))

    def carry_state(c, carry):
        rows = pl.ds(_chunk_rows(c, nc, direction), L)
        q_in = qin_ref[c]
        g = g_ref[c][0:1]
        outs, states = [], []
        for h in range(HG_HEADS):
            ln = slice(h * HG_DIM, (h + 1) * HG_DIM)
            st = st_ref[0, 0, h]
            outs.append(_dot_nt(q_in[:, ln], st.astype(BF16)))
            states.append(g[:, ln] * st + u_ref[c, h])
        o_ref[0, 0, rows, :] = (oi_ref[c] + jnp.concatenate(outs, axis=1)).astype(BF16)
        for h in range(HG_HEADS):
            st_ref[0, 0, h] = states[h]
        return carry

    lax.fori_loop(0, nc // group, intra, 0)
    lax.fori_loop(0, nc, carry_state, 0, unroll=2)


def _hgrn2(q, v, lf, s0, dmat, pair_valid, wmask):
    bsz, t, hw = q.shape
    nc = min(8, t // CHUNK)
    tb = nc * CHUNK
    nblk = t // tb

    def blk(j, d):
        return jnp.where(d == 0, j, nblk - 1 - j)

    st_spec = pl.BlockSpec((1, 1) + s0.shape[2:], lambda b, d, j: (b, d, 0, 0, 0))
    return pl.pallas_call(
        functools.partial(_hgrn2_kernel, nc=nc),
        grid=(bsz, 2, nblk),
        in_specs=[pl.BlockSpec((1, tb, hw), lambda b, d, j: (b, blk(j, d), 0)),
                  pl.BlockSpec((1, tb, hw), lambda b, d, j: (b, blk(j, d), 0)),
                  pl.BlockSpec((1, 1, tb, hw), lambda b, d, j: (b, d, blk(j, d), 0)),
                  pl.BlockSpec((1,) + dmat.shape[1:], lambda b, d, j: (d, 0, 0)),
                  pl.BlockSpec((1,) + pair_valid.shape[1:], lambda b, d, j: (d, 0, 0)),
                  pl.BlockSpec((1,) + wmask.shape[1:], lambda b, d, j: (d, 0, 0)),
                  st_spec],
        out_specs=[pl.BlockSpec((1, 1, tb, hw), lambda b, d, j: (b, d, blk(j, d), 0)), st_spec],
        out_shape=[jax.ShapeDtypeStruct((bsz, 2, t, hw), BF16),
                   jax.ShapeDtypeStruct(s0.shape, F32)],
        scratch_shapes=[pltpu.VMEM((nc,) + s0.shape[2:], F32),
                        pltpu.VMEM((nc, CHUNK, hw), BF16),
                        pltpu.VMEM((nc, 8, hw), F32),
                        pltpu.VMEM((nc, CHUNK, hw), F32),
                        pltpu.VMEM((CHUNK, hw), F32),
                        pltpu.VMEM((CHUNK, hw), F32),
                        pltpu.VMEM((CHUNK, hw), F32)],
        compiler_params=_cparams(("parallel", "parallel", "arbitrary")),
        name="hgrn2_scan",
    )(q, v, lf, dmat, pair_valid, wmask, s0)


def _mlstm_kernel(qk_ref, v_ref, g_ref, le2_ref, let2_ref, c0_ref, m0_ref,
                  o_ref, c_ref, mm_ref,
                  p_ref, u_ref, qs_ref, ct_ref, mloc_ref, sc_ref, *, nc):
    direction = pl.program_id(1)

    @pl.when(pl.program_id(2) == 0)
    def _():
        c_ref[...] = c0_ref[...]
        mm_ref[...] = m0_ref[...]

    le2 = le2_ref[0].astype(BF16)
    let2 = let2_ref[0].astype(BF16)
    causal = le2_ref[0] > 0.0
    L = CHUNK
    neg = -jnp.inf
    lane = lax.broadcasted_iota(jnp.int32, (1, LANE), 1)
    seg = [lane < L, lane >= L]
    lane_t = lax.broadcasted_iota(jnp.int32, (L, LANE), 1) < L
    eye = (lax.broadcasted_iota(jnp.int32, (LANE, LANE), 0)
           == lax.broadcasted_iota(jnp.int32, (LANE, LANE), 1))
    npair = ML_HEADS // 2
    ones_v = jnp.ones((L, ML_V), BF16)

    group = min(ML_GROUP, nc)

    def intra(grp, carry):
        cs = [grp * group + i for i in range(group)]
        ccs = [jnp.where(direction == 0, c, nc - 1 - c) for c in cs]
        rows = [pl.ds(pl.multiple_of(cc * L, L), L) for cc in ccs]
        pairs = [(i, p) for i in range(group) for p in range(npair)]
        gs = [jnp.where(direction == 0, g_ref[0, cc][0:4], g_ref[0, cc][4:8]) for cc in ccs]
        i2 = jnp.concatenate([g[0:2] for g in gs], axis=0)
        lf2 = _log_sigmoid(jnp.concatenate([g[2:4] for g in gs], axis=0))
        parts = _split3(lf2)
        r2 = i2 - sum(_dot(x, let2) for x in parts)
        g_a = jnp.max(jnp.where(seg[0], r2, neg), axis=1, keepdims=True)
        g_b = jnp.max(jnp.where(seg[1], r2, neg), axis=1, keepdims=True)
        e_w = jnp.exp(r2 - jnp.where(seg[0], g_a, g_b))
        bt_a = jnp.sum(jnp.where(seg[0], lf2, 0.0), axis=1, keepdims=True)
        bt_b = jnp.sum(jnp.where(seg[1], lf2, 0.0), axis=1, keepdims=True)

        def head_rows(x, i):
            zero = jnp.zeros_like(x[0:1])
            return jnp.concatenate(
                [jnp.broadcast_to(jnp.where(seg[h % 2], x[2 * i + h // 2:2 * i + h // 2 + 1], zero),
                                  (LANE, LANE)) for h in range(ML_HEADS)], axis=0)

        c_ts = [sum(_dot_nt(le2, head_rows(x, i)) for x in parts) for i in range(group)]
        for i in range(group):
            ct_ref[cs[i]] = c_ts[i]
            sc_ref[cs[i]] = jnp.concatenate(
                [jnp.broadcast_to(x[2 * i + p:2 * i + p + 1], (1, LANE))
                 for x2 in ((bt_a, bt_b), (g_a, g_b)) for p in range(npair) for x in x2], axis=0)
        qks = [qk_ref[0, r, :] for r in rows]
        vbs = [v_ref[0, r, :] for r in rows]
        q_ps = [qks[i][:, p * LANE:(p + 1) * LANE] for i, p in pairs]
        k_ps = [qks[i][:, (npair + p) * LANE:(npair + p + 1) * LANE] for i, p in pairs]
        zb = jnp.zeros((L, LANE), BF16)
        kbs = [jnp.concatenate([jnp.where(lane_t, k, zb), jnp.where(lane_t, zb, k)], axis=0) for k in k_ps]
        scores = [_dot_nt(q, kb) for q, kb in zip(q_ps, kbs)]
        ws = []
        for n, (i, p) in enumerate(pairs):
            ha, hb = 2 * p, 2 * p + 1
            t1 = jnp.where(lane_t, c_ts[i][:, ha * ML_V:(ha + 1) * ML_V], c_ts[i][:, hb * ML_V:(hb + 1) * ML_V])
            d_log = jnp.where(causal, t1 + r2[2 * i + p:2 * i + p + 1], neg)
            m_a = jnp.max(jnp.where(lane_t, d_log, neg), axis=1, keepdims=True)
            m_b = jnp.max(jnp.where(lane_t, neg, d_log), axis=1, keepdims=True)
            mloc_ref[cs[i], :, ha * ML_V:(ha + 1) * ML_V] = jnp.broadcast_to(m_a, (L, ML_V))
            mloc_ref[cs[i], :, hb * ML_V:(hb + 1) * ML_V] = jnp.broadcast_to(m_b, (L, ML_V))
            ws.append(jnp.exp(d_log - jnp.where(lane_t, m_a, m_b)))
        a_s = [(s_ * w).astype(BF16) for s_, w in zip(scores, ws)]
        a_stacks = [jnp.concatenate([jnp.where(lane_t, a, zb), jnp.where(lane_t, zb, a)], axis=0) for a in a_s]
        v_stacks = [jnp.concatenate(
            [jnp.concatenate([vbs[i][:, 2 * p * ML_V:(2 * p + 1) * ML_V], ones_v], axis=1),
             jnp.concatenate([vbs[i][:, (2 * p + 1) * ML_V:(2 * p + 2) * ML_V], ones_v], axis=1)], axis=0)
            for i, p in pairs]
        for n, (i, p) in enumerate(pairs):
            p_ref[cs[i], p] = _dot(a_stacks[n], v_stacks[n])
            qs_ref[cs[i], p] = jnp.concatenate(
                [jnp.where(lane_t, q_ps[n], zb), jnp.where(lane_t, zb, q_ps[n])], axis=0)
        diags = [jnp.where(eye, jnp.broadcast_to(e_w[2 * i + p:2 * i + p + 1], (LANE, LANE)), 0.0).astype(BF16)
                 for i, p in pairs]
        kb_ws = [_dot(d, kb).astype(BF16) for d, kb in zip(diags, kbs)]
        for n, (i, p) in enumerate(pairs):
            u_ref[cs[i], p] = _dot_tn(kb_ws[n], v_stacks[n])
        return carry

    def carry_state(c, carry):
        cc = jnp.where(direction == 0, c, nc - 1 - c)
        rows = pl.ds(pl.multiple_of(cc * L, L), L)
        m_all = mm_ref[0, 0]
        sc = sc_ref[c]
        c_t = ct_ref[c]
        m_loc = mloc_ref[c]
        outs, states, m_rows = [], [], []
        for p in range(ML_HEADS // 2):
            c_pair = c_ref[0, 0, p]
            inter = _dot(qs_ref[c, p], c_pair.astype(BF16))
            intra_p = p_ref[c, p]
            inc = u_ref[c, p]
            new_rows = []
            for hh in range(2):
                h = 2 * p + hh
                rs = slice(hh * L, (hh + 1) * L)
                ln = slice(h * ML_V, (h + 1) * ML_V)
                m_in = m_all[h:h + 1]
                m_t = jnp.maximum(m_loc[:, ln], c_t[:, ln] + m_in)
                alpha = jnp.exp(m_loc[:, ln] - m_t)
                beta = jnp.exp(c_t[:, ln] + m_in - m_t)
                num = alpha * intra_p[rs, :ML_V] + beta * inter[rs, :ML_V]
                den = alpha * intra_p[rs, ML_V:] + beta * inter[rs, ML_V:]
                outs.append(num / jnp.maximum(jnp.abs(den), jnp.exp(-m_t)))
                b_tot = sc[h:h + 1]
                m_new = b_tot + jnp.maximum(m_in, sc[ML_HEADS + h:ML_HEADS + h + 1])
                carry_w = jnp.exp(b_tot + m_in - m_new)
                inc_w = jnp.exp(b_tot + sc[ML_HEADS + h:ML_HEADS + h + 1] - m_new)
                cw2 = jnp.concatenate([carry_w, carry_w], axis=1)
                iw2 = jnp.concatenate([inc_w, inc_w], axis=1)
                new_rows.append(cw2 * c_pair[rs] + iw2 * inc[rs])
                m_rows.append(m_new)
            states.append(jnp.concatenate(new_rows, axis=0))
        o_ref[0, 0, rows, :] = jnp.concatenate(outs, axis=1).astype(BF16)
        for p in range(ML_HEADS // 2):
            c_ref[0, 0, p] = states[p]
        mm_ref[0, 0] = jnp.concatenate(m_rows + [m_all[ML_HEADS:]], axis=0)
        return carry

    lax.fori_loop(0, nc // group, intra, 0)
    lax.fori_loop(0, nc, carry_state, 0, unroll=2)


def _mlstm(qk, v, gates, c0, m0, le2, let2):
    bsz, t, hw = v.shape
    nc = min(8, t // CHUNK)
    tb = nc * CHUNK
    nblk = t // tb
    npair = ML_HEADS // 2

    def blk(j, d):
        return jnp.where(d == 0, j, nblk - 1 - j)

    c_spec = pl.BlockSpec((1, 1) + c0.shape[2:], lambda b, d, j: (b, d, 0, 0, 0))
    m_spec = pl.BlockSpec((1, 1) + m0.shape[2:], lambda b, d, j: (b, d, 0, 0))
    return pl.pallas_call(
        functools.partial(_mlstm_kernel, nc=nc),
        grid=(bsz, 2, nblk),
        in_specs=[pl.BlockSpec((1, tb, qk.shape[-1]), lambda b, d, j: (b, blk(j, d), 0)),
                  pl.BlockSpec((1, tb, hw), lambda b, d, j: (b, blk(j, d), 0)),
                  pl.BlockSpec((1, nc) + gates.shape[2:], lambda b, d, j: (b, blk(j, d), 0, 0)),
                  pl.BlockSpec((1,) + le2.shape[1:], lambda b, d, j: (d, 0, 0)),
                  pl.BlockSpec((1,) + let2.shape[1:], lambda b, d, j: (d, 0, 0)),
                  c_spec, m_spec],
        out_specs=[pl.BlockSpec((1, 1, tb, hw), lambda b, d, j: (b, d, blk(j, d), 0)), c_spec, m_spec],
        out_shape=[jax.ShapeDtypeStruct((bsz, 2, t, hw), BF16),
                   jax.ShapeDtypeStruct(c0.shape, F32),
                   jax.ShapeDtypeStruct(m0.shape, F32)],
        scratch_shapes=[pltpu.VMEM((nc, npair, 2 * CHUNK, 2 * ML_V), F32),
                        pltpu.VMEM((nc, npair, 2 * ML_QK, 2 * ML_V), F32),
                        pltpu.VMEM((nc, npair, 2 * CHUNK, LANE), BF16),
                        pltpu.VMEM((nc, CHUNK, hw), F32),
                        pltpu.VMEM((nc, CHUNK, hw), F32),
                        pltpu.VMEM((nc, 8, LANE), F32)],
        compiler_params=_cparams(("parallel", "parallel", "arbitrary")),
        name="mlstm_scan",
    )(qk, v, gates, le2, let2, c0, m0)


def _head_rms(x, heads):
    dh = x.shape[-1] // heads
    return jnp.concatenate([_rms(x[:, h * dh:(h + 1) * dh]) for h in range(heads)], axis=1)


def _mix_ffn_kernel(oh_ref, om_ref, hg_ref, mo_ref, x_ref, mod_ref, hnw_ref, mnw_ref, wo_ref,
                    nw_ref, w1_ref, w2_ref, fw_ref, o_ref, a_ref, *, d_ff, tf):
    mix_gate = mod_ref[0, 0:1, :]
    mods = [mod_ref[0, k:k + 1, :] for k in range(1, 4)]
    for r0 in range(0, x_ref.shape[1], ROW_SUB):
        rs = slice(r0, r0 + ROW_SUB)
        oh = oh_ref[0, 0, rs, :].astype(F32) + oh_ref[0, 1, rs, :].astype(F32)
        om = om_ref[0, 0, rs, :].astype(F32) + om_ref[0, 1, rs, :].astype(F32)
        hg_out = _head_rms(oh, HG_HEADS) * hnw_ref[...] * _silu(hg_ref[0, rs, :].astype(F32))
        ml_out = _sigmoid(mo_ref[0, rs, :].astype(F32)) * (_head_rms(om, ML_HEADS) * mnw_ref[...])
        merged = jnp.concatenate([hg_out, ml_out], axis=1).astype(BF16)
        x = x_ref[0, rs, :] + mix_gate * _dot(merged, wo_ref[...])
        y = _ffn_rows(x, mods, nw_ref, w1_ref, w2_ref, a_ref, rs, d_ff, tf)
        o_ref[0, rs, :] = _rms(y) * fw_ref[...]


def _mix_ffn(oh, om, hg, mo, x, mods, hnw, mnw, w_out, nw, w1, w2, fw):
    bsz, t, d = x.shape
    hw = hg.shape[-1]
    d_ff = w2.shape[0]
    tm = min(TOKEN_TILE, t)
    tok = lambda b, i: (b, i, 0)
    tok2 = lambda b, i: (b, 0, i, 0)
    const = lambda b, i: (0, 0)
    resident = lambda a: pl.BlockSpec(a.shape, const, pipeline_mode=pl.Buffered(1))
    return pl.pallas_call(
        functools.partial(_mix_ffn_kernel, d_ff=d_ff, tf=FF_TILE),
        grid=(bsz, t // tm),
        in_specs=[pl.BlockSpec((1, 2, tm, hw), tok2), pl.BlockSpec((1, 2, tm, hw), tok2),
                  pl.BlockSpec((1, tm, hw), tok), pl.BlockSpec((1, tm, hw), tok),
                  pl.BlockSpec((1, tm, d), tok),
                  pl.BlockSpec((1, 4, d), lambda b, i: (b, 0, 0)),
                  pl.BlockSpec((1, hw), const), pl.BlockSpec((1, hw), const), resident(w_out),
                  pl.BlockSpec((1, d), const), resident(w1), resident(w2), pl.BlockSpec((1, d), const)],
        out_specs=pl.BlockSpec((1, tm, d), tok),
        out_shape=jax.ShapeDtypeStruct((bsz, t, d), F32),
        scratch_shapes=[pltpu.VMEM((tm, d_ff), BF16)],
        compiler_params=_cparams(("parallel", "parallel")),
        name="mix_ffn_final",
    )(oh, om, hg, mo, x, mods, hnw, mnw, w_out, nw, w1, w2, fw)


def kernel(x, c, ctx, c_ctx, w_mod, b_mod, norm1_w, ffn1_w1, ffn1_w2, norm2_w, w_in, ml_gate_b,
           ml_conv_w, ml_conv_b, hg_lb_logits, hg_norm_w, ml_norm_w, w_out, norm3_w, ffn2_w1, ffn2_w2,
           final_norm_w):
    bsz, seq, d = x.shape
    n_ctx = ctx.shape[1]
    assert w_mod.shape[0] == 1, "single-layer kernel"
    assert seq % (GRID_W * 8) == 0 and n_ctx % CHUNK == 0
    hw = HG_HEADS * HG_DIM
    ng = ml_gate_b.shape[-1]

    rows = -(-(bsz + 1) // 8) * 8
    cvec = jnp.zeros((rows, d), F32).at[:bsz].set(c).at[bsz].set(c_ctx)
    mods = _modulation(cvec, w_mod[0], b_mod[0][None, :]).reshape(rows, N_MOD, d)
    lat_row = lambda b: b
    ctx_row = lambda b: bsz

    row = lambda a: a.reshape(1, -1)
    w1a, w2a = ffn1_w1[0].astype(BF16), ffn1_w2[0].astype(BF16)
    x1 = _ffn(x, mods[:, 0:3], lat_row, row(norm1_w[0]), w1a, w2a)
    s1 = _ffn(ctx, mods[:, 0:3], ctx_row, row(norm1_w[0]), w1a, w2a)

    w_in_b = w_in[0].astype(BF16)
    gate_b = row(ml_gate_b[0])
    dmat, pair_valid, wmask = (jnp.asarray(a) for a in _hgrn2_masks())
    le2, let2 = (jnp.asarray(a) for a in _mlstm_masks())
    conv_w, conv_b = ml_conv_w[0], row(ml_conv_b[0])
    npair = ML_HEADS // 2

    def mixer_scans(s, mod_row, width, s0, c0, m0):
        hq, hv, hg, hlf, mqk, mv, mo, gcol = _inproj(
            s, mods[:, 3:5], mod_row, row(norm2_w[0]), w_in_b, gate_b, hg_lb_logits)
        n_chunks = s.shape[1] // CHUNK
        qk = _conv(mqk, conv_w, conv_b, width)
        gates = gcol.reshape(bsz, n_chunks, CHUNK, 2, 2, npair, 2).transpose(0, 1, 3, 4, 5, 6, 2)
        gates = gates.reshape(bsz, n_chunks, 8, 2 * CHUNK)
        oh, s_fin = _hgrn2(hq, hv, hlf, s0, dmat, pair_valid, wmask)
        om, c_fin, m_fin = _mlstm(qk, mv, gates, c0, m0, le2, let2)
        return oh, om, hg, mo, s_fin, c_fin, m_fin

    s0 = jnp.zeros((bsz, 2, HG_HEADS, HG_DIM, HG_DIM), F32)
    c0 = jnp.zeros((bsz, 2, npair, 2 * ML_QK, 2 * ML_V), F32)
    m0 = jnp.zeros((bsz, 2, 8, LANE), F32)
    _, _, _, _, s_ctx, c_ctx_state, m_ctx = mixer_scans(s1, ctx_row, n_ctx, s0, c0, m0)
    oh, om, hg, mo, _, _, _ = mixer_scans(x1, lat_row, GRID_W, s_ctx, c_ctx_state, m_ctx)

    return _mix_ffn(oh, om, hg, mo, x1, mods[:, 5:9], row(hg_norm_w[0]), row(ml_norm_w[0]),
                    w_out[0].astype(BF16), row(norm3_w[0]), ffn2_w1[0].astype(BF16),
                    ffn2_w2[0].astype(BF16), row(final_norm_w))
```

```python
import functools

import numpy as np
import jax
import jax.numpy as jnp
from jax import lax
from jax.experimental import pallas as pl
from jax.experimental.pallas import tpu as pltpu

F32 = jnp.float32
BF16 = jnp.bfloat16

EPS = 1e-6
CHUNK = 64
GRID_W = 64
N_MOD = 9
HG_HEADS = 4
HG_DIM = 128
ML_HEADS = 4
ML_QK = 64
ML_V = 128
SUB = 16
N_SUB = CHUNK // SUB
GROUP = 4
ML_GROUP = 4
HG_CHUNKS = 8
ML_CHUNKS = 16
TOKEN_TILE = 512
ROW_SUB = 256
FF_TILE = 256
LOG2E = 1.4426950408889634
EXP2_CLAMP = 115.0
LANE = 128
VMEM_LIMIT = 56 * 1024 * 1024


def _cparams(sem):
    return pltpu.CompilerParams(dimension_semantics=sem, vmem_limit_bytes=VMEM_LIMIT)


def _dot(a, b):
    return jnp.dot(a, b, preferred_element_type=F32)


def _dot_nt(a, b):
    return lax.dot_general(a, b, (((1,), (1,)), ((), ())), preferred_element_type=F32)


def _dot_tn(a, b):
    return lax.dot_general(a, b, (((0,), (0,)), ((), ())), preferred_element_type=F32)


def _sigmoid(x):
    return 1.0 / (1.0 + jnp.exp(-x))


def _silu(x):
    return x * _sigmoid(x)


def _log_sigmoid(x):
    return jnp.minimum(x, 0.0) - jnp.log(1.0 + jnp.exp(-jnp.abs(x)))


def _split3(x):
    hi = x.astype(BF16)
    r = x - hi.astype(F32)
    mid = r.astype(BF16)
    lo = (r - mid.astype(F32)).astype(BF16)
    return hi, mid, lo


def _split2(x):
    hi = x.astype(BF16)
    return hi, (x - hi.astype(F32)).astype(BF16)


def _rms(x):
    return x * lax.rsqrt(jnp.mean(x * x, axis=-1, keepdims=True) + EPS)


def _mod_kernel(c_ref, w_ref, b_ref, o_ref):
    a = _silu(c_ref[...])
    a_hi = a.astype(BF16)
    a_lo = (a - a_hi.astype(F32)).astype(BF16)
    w = w_ref[...]
    w_hi = w.astype(BF16)
    w_lo = (w - w_hi.astype(F32)).astype(BF16)
    o_ref[...] = _dot(a_hi, w_hi) + _dot(a_hi, w_lo) + _dot(a_lo, w_hi) + b_ref[...]


def _modulation(cvec, w_mod, b_mod):
    rows, d = cvec.shape
    n = w_mod.shape[1]
    tn = 1024
    return pl.pallas_call(
        _mod_kernel,
        grid=(n // tn,),
        in_specs=[pl.BlockSpec((rows, d), lambda j: (0, 0)),
                  pl.BlockSpec((d, tn), lambda j: (0, j)),
                  pl.BlockSpec((1, tn), lambda j: (0, j))],
        out_specs=pl.BlockSpec((rows, tn), lambda j: (0, j)),
        out_shape=jax.ShapeDtypeStruct((rows, n), F32),
        compiler_params=_cparams(("arbitrary",)),
        name="modulation",
    )(cvec, w_mod, b_mod)


def _ffn_rows(x, mods, nw_ref, w1_ref, w2_ref, a_ref, rs, d_ff, tf):
    shift, scale, gate = mods
    h = (_rms(x) * nw_ref[...] * (1.0 + scale) + shift).astype(BF16)
    for c in range(d_ff // tf):
        g = _dot(h, w1_ref[:, c * tf:(c + 1) * tf])
        u = _dot(h, w1_ref[:, d_ff + c * tf:d_ff + (c + 1) * tf])
        a_ref[rs, c * tf:(c + 1) * tf] = (_silu(g) * u).astype(BF16)
    return x + 0.5 * gate * _dot(a_ref[rs, :], w2_ref[...])


def _ffn_kernel(x_ref, mod_ref, nw_ref, w1_ref, w2_ref, o_ref, a_ref, *, d_ff, tf):
    mods = [mod_ref[0, k:k + 1, :] for k in range(3)]
    for r0 in range(0, x_ref.shape[1], ROW_SUB):
        rs = slice(r0, r0 + ROW_SUB)
        o_ref[0, rs, :] = _ffn_rows(x_ref[0, rs, :], mods, nw_ref, w1_ref, w2_ref, a_ref, rs, d_ff, tf)


def _ffn(s, mods, mod_row, nw, w1, w2):
    bsz, t, d = s.shape
    d_ff = w2.shape[0]
    tm = min(TOKEN_TILE, t)
    const = lambda b, i: (0, 0)
    return pl.pallas_call(
        functools.partial(_ffn_kernel, d_ff=d_ff, tf=FF_TILE),
        grid=(bsz, t // tm),
        in_specs=[pl.BlockSpec((1, tm, d), lambda b, i: (b, i, 0)),
                  pl.BlockSpec((1, 3, d), lambda b, i: (mod_row(b), 0, 0)),
                  pl.BlockSpec((1, d), const),
                  pl.BlockSpec((d, 2 * d_ff), const, pipeline_mode=pl.Buffered(1)),
                  pl.BlockSpec((d_ff, d), const, pipeline_mode=pl.Buffered(1))],
        out_specs=pl.BlockSpec((1, tm, d), lambda b, i: (b, i, 0)),
        out_shape=jax.ShapeDtypeStruct((bsz, t, d), F32),
        scratch_shapes=[pltpu.VMEM((tm, d_ff), BF16)],
        compiler_params=_cparams(("parallel", "parallel")),
        name="ffn",
    )(s, mods, nw, w1, w2)


def _inproj_kernel(x_ref, mod_ref, nw_ref, w_ref, gb_ref, lbl_ref,
                   hq_ref, hv_ref, hg_ref, hlf_ref, mqk_ref, mv_ref, mo_ref, gc_ref, *, hw):
    shift = mod_ref[0, 0:1, :]
    scale = mod_ref[0, 1:2, :]
    lbl = lbl_ref[...]
    e = jnp.exp(lbl - jnp.max(lbl, axis=0, keepdims=True))
    lb = e[0] / jnp.sum(e, axis=0)
    ng = gc_ref.shape[-1]
    for r0 in range(0, x_ref.shape[1], ROW_SUB):
        rs = slice(r0, r0 + ROW_SUB)
        h = (_rms(x_ref[0, rs, :]) * nw_ref[...] * (1.0 + scale) + shift).astype(BF16)

        def proj(k):
            return _dot(h, w_ref[:, k * hw:(k + 1) * hw])

        hq_ref[0, rs, :] = _silu(proj(0)) * (HG_DIM ** -0.5)
        hv_ref[0, rs, :] = proj(1).astype(BF16)
        hg_ref[0, rs, :] = proj(2).astype(BF16)
        for d in range(2):
            lbd = lb[d:d + 1, :]
            hlf_ref[0, d, rs, :] = jnp.log(lbd + (1.0 - lbd) * _sigmoid(proj(3 + d))) * LOG2E
        mqk_ref[0, rs, :] = proj(5).astype(BF16)
        mv_ref[0, rs, :] = proj(6).astype(BF16)
        mo_ref[0, rs, :] = proj(7).astype(BF16)
        gc_ref[0, rs, :] = _dot(h, w_ref[:, 8 * hw:8 * hw + ng]) + gb_ref[...]


def _inproj(s, mods, mod_row, nw, w_in, gate_b, lb_logits):
    bsz, t, d = s.shape
    hw = HG_HEADS * HG_DIM
    ng = gate_b.shape[-1]
    tm = min(TOKEN_TILE, t)
    const = lambda b, i: (0, 0)
    tok = lambda b, i: (b, i, 0)
    tok2 = lambda b, i: (b, 0, i, 0)
    f = lambda dt, *shape: jax.ShapeDtypeStruct(shape, dt)
    return pl.pallas_call(
        functools.partial(_inproj_kernel, hw=hw),
        grid=(bsz, t // tm),
        in_specs=[pl.BlockSpec((1, tm, d), tok),
                  pl.BlockSpec((1, 2, d), lambda b, i: (mod_row(b), 0, 0)),
                  pl.BlockSpec((1, d), const),
                  pl.BlockSpec(w_in.shape, const, pipeline_mode=pl.Buffered(1)),
                  pl.BlockSpec((1, ng), const),
                  pl.BlockSpec(lb_logits.shape, lambda b, i: (0, 0, 0))],
        out_specs=[pl.BlockSpec((1, tm, hw), tok), pl.BlockSpec((1, tm, hw), tok),
                   pl.BlockSpec((1, tm, hw), tok),
                   pl.BlockSpec((1, 2, tm, hw), tok2),
                   pl.BlockSpec((1, tm, hw), tok), pl.BlockSpec((1, tm, hw), tok),
                   pl.BlockSpec((1, tm, hw), tok), pl.BlockSpec((1, tm, ng), tok)],
        out_shape=[f(F32, bsz, t, hw), f(BF16, bsz, t, hw), f(BF16, bsz, t, hw),
                   f(F32, bsz, 2, t, hw),
                   f(BF16, bsz, t, hw), f(BF16, bsz, t, hw), f(BF16, bsz, t, hw), f(F32, bsz, t, ng)],
        compiler_params=_cparams(("parallel", "parallel")),
        name="inproj",
    )(s, mods, nw, w_in, gate_b, lb_logits)


def _conv_kernel(x_ref, w_ref, b_ref, o_ref, *, width, q_tiles):
    x = x_ref[0].astype(F32)
    t = x.shape[0]
    col = lax.broadcasted_iota(jnp.int32, x.shape, 0) % width
    xl = jnp.where(col == 0, 0.0, pltpu.roll(x, 1, 0))
    xr = jnp.where(col == width - 1, 0.0, pltpu.roll(x, t - 1, 0))

    def row(di):
        return (w_ref[di, 0:1, :] * xl + w_ref[di, 1:2, :] * x + w_ref[di, 2:3, :] * xr)

    y = row(1)
    if t > width:
        pad = jnp.zeros((width, x.shape[1]), F32)
        y = y + jnp.concatenate([pad, row(0)[:t - width]], axis=0)
        y = y + jnp.concatenate([row(2)[width:], pad], axis=0)
    y = _silu(y + b_ref[...])
    qscale = jnp.where(pl.program_id(1) < q_tiles, ML_QK ** -0.5, 1.0)
    o_ref[0] = (y * qscale).astype(BF16)


def _conv(x, w, b, width):
    bsz, t, ch = x.shape
    return pl.pallas_call(
        functools.partial(_conv_kernel, width=width, q_tiles=ML_HEADS * ML_QK // LANE),
        grid=(bsz, ch // LANE),
        in_specs=[pl.BlockSpec((1, t, LANE), lambda b_, c: (b_, 0, c)),
                  pl.BlockSpec((3, 3, LANE), lambda b_, c: (0, 0, c)),
                  pl.BlockSpec((1, LANE), lambda b_, c: (0, c))],
        out_specs=pl.BlockSpec((1, t, LANE), lambda b_, c: (b_, 0, c)),
        out_shape=jax.ShapeDtypeStruct((bsz, t, ch), BF16),
        compiler_params=_cparams(("parallel", "parallel")),
        name="conv",
    )(x, w, b)


def _positions():
    t = np.arange(CHUNK)
    return [t, CHUNK - 1 - t]


def _hgrn2_masks():
    ds, vs, ws = [], [], []
    for p in _positions():
        le = (p[None, :] <= p[:, None]).astype(np.float32)
        order = [p[SUB * j] // SUB for j in range(N_SUB)]
        ref = np.zeros((32, CHUNK), np.float32)
        valid = np.zeros((N_SUB * N_SUB, 1), np.float32)
        for j in range(N_SUB):
            ref[j] = p <= SUB * order[j] + SUB // 2 - 1
            ref[8 + j] = p <= SUB * order[j] + SUB - 1
        ref[N_SUB] = 1.0
        for j in range(N_SUB):
            for i in range(N_SUB):
                if order[i] < order[j]:
                    ref[16 + N_SUB * j + i] = ref[j] - ref[8 + i]
                    valid[N_SUB * j + i] = 1.0
        ds.append(np.concatenate([le, ref], axis=0))
        vs.append(np.broadcast_to(valid, (N_SUB * N_SUB, HG_HEADS * HG_DIM)))
        tok_blk = np.arange(CHUNK)[:, None] // SUB
        ws.append(np.concatenate([(tok_blk == j) * le for j in range(N_SUB)], axis=1))
    return (np.stack(ds).astype(np.float32), np.stack(vs).astype(np.float32),
            np.stack(ws).astype(np.float32))


def _mlstm_masks():
    le2s, let2s = [], []
    for p in _positions():
        le = (p[None, :] <= p[:, None]).astype(np.float32)
        le2s.append(np.concatenate([le, le], axis=1))
        let2 = np.zeros((2 * CHUNK, 2 * CHUNK), np.float32)
        let2[:CHUNK, :CHUNK] = le.T
        let2[CHUNK:, CHUNK:] = le.T
        let2s.append(let2)
    return np.stack(le2s), np.stack(let2s)


def _run_pipelined(n_groups, group, stages_of, step_of):
    queue = []
    for g in range(n_groups):
        for _ in stages_of(g):
            if queue:
                queue.pop(0)()
        while queue:
            queue.pop(0)()
        queue = [functools.partial(step_of, g * group + i) for i in range(group)]
    for step in queue:
        step()


def _chunk_rows(c, nc, direction):
    cc = jnp.where(direction == 0, c, nc - 1 - c)
    return pl.multiple_of(cc * CHUNK, CHUNK)


def _hgrn2_kernel(q_ref, v_ref, lf_ref, d_ref, pv_ref, w_ref, s0_ref, o_ref, st_ref,
                  u_ref, qin_ref, g_ref, oi_ref, bx_ref, kx_ref, vx_ref, *, nc):
    direction = pl.program_id(1)

    @pl.when(pl.program_id(2) == 0)
    def _():
        st_ref[...] = s0_ref[...]

    dmat = d_ref[0].astype(BF16)
    pair_valid = pv_ref[0]
    wmask = w_ref[0] > 0.0
    L = CHUNK
    hw = q_ref.shape[-1]
    upper_rows = lax.broadcasted_iota(jnp.int32, (L, 2 * L), 0) < L // 2

    group = min(GROUP, nc)

    def intra(grp):
        cs = [grp * group + i for i in range(group)]
        rows = [pl.ds(_chunk_rows(c, nc, direction), L) for c in cs]
        items = [(i, h) for i in range(group) for h in range(HG_HEADS)]
        lanes = [slice(h * HG_DIM, (h + 1) * HG_DIM) for h in range(HG_HEADS)]
        lfs = [lf_ref[0, 0, r, :] for r in rows]
        parts = [_split2(lf) for lf in lfs]
        e_all = sum(_dot(dmat, jnp.concatenate([parts[i][n] for i in range(group)], axis=1))
                    for n in range(2))
        yield
        q_mids, k_mids, k_outs, vbs, spans = [], [], [], [], []
        for i in range(group):
            e = e_all[:, i * hw:(i + 1) * hw]
            q = q_ref[0, rows[i], :]
            k = 1.0 - jnp.exp2(lfs[i])
            b = e[0:L]
            b_tot = e[L + N_SUB:L + N_SUB + 1]
            sub_rows = lambda r0: jnp.concatenate(
                [jnp.broadcast_to(e[r0 + j:r0 + j + 1], (SUB, hw)) for j in range(N_SUB)], axis=0)
            mid_rows = sub_rows(L)
            end_rows = sub_rows(L + 8)
            to_mid = b - mid_rows
            spans.append(jnp.max(jnp.abs(to_mid)))
            q_mids.append((q * jnp.exp2(jnp.minimum(to_mid, EXP2_CLAMP))).astype(BF16))
            k_diag = k * jnp.exp2(jnp.minimum(-to_mid, EXP2_CLAMP))
            k_end = k * jnp.exp2(end_rows - b)
            cross = jnp.exp2(e[L + 16:L + 32]) * pair_valid
            slabs = []
            for j in range(N_SUB):
                for ib in range(N_SUB):
                    rs = slice(ib * SUB, (ib + 1) * SUB)
                    row = N_SUB * j + ib
                    slabs.append(k_diag[rs] if ib == j else k_end[rs] * cross[row:row + 1])
            k_mids.append(jnp.concatenate(slabs, axis=0).astype(BF16))
            k_outs.append((k * jnp.exp2(b_tot - b)).astype(BF16))
            vbs.append(v_ref[0, rows[i], :])
            qin_ref[cs[i]] = (q * jnp.exp2(b)).astype(BF16)
            g_ref[cs[i]] = jnp.broadcast_to(jnp.exp2(b_tot), (8, hw))
            yield
        scores = [_dot_nt(q_mids[i][:, lanes[h]], k_mids[i][:, lanes[h]]) for i, h in items]
        yield
        a_s = [jnp.where(wmask, s, 0.0) for s in scores]
        a_s = [jnp.where(upper_rows, a[:, :2 * L], a[:, 2 * L:]).astype(BF16) for a in a_s]
        v2s = [jnp.concatenate([vb, vb], axis=0) for vb in vbs]
        outs = [_dot(a_s[n], v2s[i][:, lanes[h]]) for n, (i, h) in enumerate(items)]
        yield
        for i in range(group):
            oi_ref[cs[i]] = jnp.concatenate(outs[i * HG_HEADS:(i + 1) * HG_HEADS], axis=1)
        yield
        for i, h in items:
            u_ref[cs[i], h] = _dot_tn(vbs[i][:, lanes[h]], k_outs[i][:, lanes[h]])

        @pl.when(functools.reduce(jnp.maximum, spans) > EXP2_CLAMP)
        def _():
            for i in range(group):
                exact_intra(cs[i], rows[i])
        yield

    def exact_intra(c, rows):
        lf = lf_ref[0, 0, rows, :]
        bx_ref[...] = sum(_dot(dmat[0:L], x) for x in _split3(lf))
        kx_ref[...] = 1.0 - jnp.exp2(lf)
        vx_ref[...] = v_ref[0, rows, :].astype(F32)
        q = q_ref[0, rows, :]
        tok = lax.broadcasted_iota(jnp.int32, (L, 1), 0)
        pos = jnp.where(direction == 0, tok, L - 1 - tok)

        def add_source(s, acc):
            b_s = bx_ref[pl.ds(s, 1), :]
            z = q * kx_ref[pl.ds(s, 1), :] * jnp.exp2(jnp.minimum(bx_ref[...] - b_s, 0.0))
            v_s = vx_ref[pl.ds(s, 1), :]
            seen = pos >= jnp.where(direction == 0, s, L - 1 - s)
            cols = [jnp.where(seen, jnp.sum(z[:, ln], axis=1, keepdims=True), 0.0) * v_s[:, ln]
                    for ln in (slice(h * HG_DIM, (h + 1) * HG_DIM) for h in range(HG_HEADS))]
            return acc + jnp.concatenate(cols, axis=1)

        oi_ref[c] = lax.fori_loop(0, L, add_source, jnp.zeros((L, hw), F32))

    def carry_state(c):
        rows = pl.ds(_chunk_rows(c, nc, direction), L)
        q_in = qin_ref[c]
        g = g_ref[c][0:1]
        outs, states = [], []
        for h in range(HG_HEADS):
            ln = slice(h * HG_DIM, (h + 1) * HG_DIM)
            st = st_ref[0, 0, h]
            outs.append(_dot_nt(q_in[:, ln], st.astype(BF16)))
            states.append(g[:, ln] * st + u_ref[c, h])
        o_ref[0, 0, rows, :] = (oi_ref[c] + jnp.concatenate(outs, axis=1)).astype(BF16)
        for h in range(HG_HEADS):
            st_ref[0, 0, h] = states[h]

    _run_pipelined(nc // group, group, intra, carry_state)


def _hgrn2(q, v, lf, s0, dmat, pair_valid, wmask):
    bsz, t, hw = q.shape
    nc = min(HG_CHUNKS, t // CHUNK)
    tb = nc * CHUNK
    nblk = t // tb

    def blk(j, d):
        return jnp.where(d == 0, j, nblk - 1 - j)

    st_spec = pl.BlockSpec((1, 1) + s0.shape[2:], lambda b, d, j: (b, d, 0, 0, 0))
    return pl.pallas_call(
        functools.partial(_hgrn2_kernel, nc=nc),
        grid=(bsz, 2, nblk),
        in_specs=[pl.BlockSpec((1, tb, hw), lambda b, d, j: (b, blk(j, d), 0)),
                  pl.BlockSpec((1, tb, hw), lambda b, d, j: (b, blk(j, d), 0)),
                  pl.BlockSpec((1, 1, tb, hw), lambda b, d, j: (b, d, blk(j, d), 0)),
                  pl.BlockSpec((1,) + dmat.shape[1:], lambda b, d, j: (d, 0, 0)),
                  pl.BlockSpec((1,) + pair_valid.shape[1:], lambda b, d, j: (d, 0, 0)),
                  pl.BlockSpec((1,) + wmask.shape[1:], lambda b, d, j: (d, 0, 0)),
                  st_spec],
        out_specs=[pl.BlockSpec((1, 1, tb, hw), lambda b, d, j: (b, d, blk(j, d), 0)), st_spec],
        out_shape=[jax.ShapeDtypeStruct((bsz, 2, t, hw), BF16),
                   jax.ShapeDtypeStruct(s0.shape, F32)],
        scratch_shapes=[pltpu.VMEM((nc,) + s0.shape[2:], F32),
                        pltpu.VMEM((nc, CHUNK, hw), BF16),
                        pltpu.VMEM((nc, 8, hw), F32),
                        pltpu.VMEM((nc, CHUNK, hw), F32),
                        pltpu.VMEM((CHUNK, hw), F32),
                        pltpu.VMEM((CHUNK, hw), F32),
                        pltpu.VMEM((CHUNK, hw), F32)],
        compiler_params=_cparams(("parallel", "parallel", "arbitrary")),
        name="hgrn2_scan",
    )(q, v, lf, dmat, pair_valid, wmask, s0)


def _mlstm_kernel(qk_ref, v_ref, g_ref, le2_ref, let2_ref, c0_ref, m0_ref,
                  o_ref, c_ref, mm_ref,
                  p_ref, u_ref, qs_ref, ct_ref, mloc_ref, sc_ref, *, nc):
    direction = pl.program_id(1)

    @pl.when(pl.program_id(2) == 0)
    def _():
        c_ref[...] = c0_ref[...]
        mm_ref[...] = m0_ref[...]

    le2 = le2_ref[0].astype(BF16)
    let2 = let2_ref[0].astype(BF16)
    causal = le2_ref[0] > 0.0
    L = CHUNK
    neg = -jnp.inf
    lane = lax.broadcasted_iota(jnp.int32, (1, LANE), 1)
    seg = [lane < L, lane >= L]
    lane_t = lax.broadcasted_iota(jnp.int32, (L, LANE), 1) < L
    eye = (lax.broadcasted_iota(jnp.int32, (LANE, LANE), 0)
           == lax.broadcasted_iota(jnp.int32, (LANE, LANE), 1))
    npair = ML_HEADS // 2
    ones_v = jnp.ones((L, ML_V), BF16)

    group = min(ML_GROUP, nc)

    def intra(grp):
        cs = [grp * group + i for i in range(group)]
        ccs = [jnp.where(direction == 0, c, nc - 1 - c) for c in cs]
        rows = [pl.ds(pl.multiple_of(cc * L, L), L) for cc in ccs]
        pairs = [(i, p) for i in range(group) for p in range(npair)]
        gs = [jnp.where(direction == 0, g_ref[0, cc][0:4], g_ref[0, cc][4:8]) for cc in ccs]
        pad = [jnp.zeros((-2 * group % 8, LANE), F32)] if 2 * group % 8 else []
        i2 = jnp.concatenate([g[0:2] for g in gs] + pad, axis=0)
        lf2 = _log_sigmoid(jnp.concatenate([g[2:4] for g in gs] + pad, axis=0))
        parts = _split3(lf2)
        r2 = i2 - sum(_dot(x, let2) for x in parts)
        g_a = jnp.max(jnp.where(seg[0], r2, neg), axis=1, keepdims=True)
        g_b = jnp.max(jnp.where(seg[1], r2, neg), axis=1, keepdims=True)
        e_w = jnp.exp(r2 - jnp.where(seg[0], g_a, g_b))
        bt_a = jnp.sum(jnp.where(seg[0], lf2, 0.0), axis=1, keepdims=True)
        bt_b = jnp.sum(jnp.where(seg[1], lf2, 0.0), axis=1, keepdims=True)
        yield

        def head_rows(x, i):
            zero = jnp.zeros_like(x[0:1])
            return jnp.concatenate(
                [jnp.broadcast_to(jnp.where(seg[h % 2], x[2 * i + h // 2:2 * i + h // 2 + 1], zero),
                                  (LANE, LANE)) for h in range(ML_HEADS)], axis=0)

        c_ts = [sum(_dot_nt(le2, head_rows(x, i)) for x in parts) for i in range(group)]
        for i in range(group):
            ct_ref[cs[i]] = c_ts[i]
            sc_ref[cs[i]] = jnp.concatenate(
                [jnp.broadcast_to(x[2 * i + p:2 * i + p + 1], (1, LANE))
                 for x2 in ((bt_a, bt_b), (g_a, g_b)) for p in range(npair) for x in x2], axis=0)
        yield
        qks = [qk_ref[0, r, :] for r in rows]
        vbs = [v_ref[0, r, :] for r in rows]
        q_ps = [qks[i][:, p * LANE:(p + 1) * LANE] for i, p in pairs]
        k_ps = [qks[i][:, (npair + p) * LANE:(npair + p + 1) * LANE] for i, p in pairs]
        zb = jnp.zeros((L, LANE), BF16)
        kbs = [jnp.concatenate([jnp.where(lane_t, k, zb), jnp.where(lane_t, zb, k)], axis=0) for k in k_ps]
        scores = [_dot_nt(q, kb) for q, kb in zip(q_ps, kbs)]
        yield
        ws = []
        for n, (i, p) in enumerate(pairs):
            ha, hb = 2 * p, 2 * p + 1
            t1 = jnp.where(lane_t, c_ts[i][:, ha * ML_V:(ha + 1) * ML_V], c_ts[i][:, hb * ML_V:(hb + 1) * ML_V])
            d_log = jnp.where(causal, t1 + r2[2 * i + p:2 * i + p + 1], neg)
            m_a = jnp.max(jnp.where(lane_t, d_log, neg), axis=1, keepdims=True)
            m_b = jnp.max(jnp.where(lane_t, neg, d_log), axis=1, keepdims=True)
            mloc_ref[cs[i], :, ha * ML_V:(ha + 1) * ML_V] = jnp.broadcast_to(m_a, (L, ML_V))
            mloc_ref[cs[i], :, hb * ML_V:(hb + 1) * ML_V] = jnp.broadcast_to(m_b, (L, ML_V))
            ws.append(jnp.exp(d_log - jnp.where(lane_t, m_a, m_b)))
        yield
        a_s = [(s_ * w).astype(BF16) for s_, w in zip(scores, ws)]
        a_stacks = [jnp.concatenate([jnp.where(lane_t, a, zb), jnp.where(lane_t, zb, a)], axis=0) for a in a_s]
        v_stacks = [jnp.concatenate(
            [jnp.concatenate([vbs[i][:, 2 * p * ML_V:(2 * p + 1) * ML_V], ones_v], axis=1),
             jnp.concatenate([vbs[i][:, (2 * p + 1) * ML_V:(2 * p + 2) * ML_V], ones_v], axis=1)], axis=0)
            for i, p in pairs]
        for n, (i, p) in enumerate(pairs):
            p_ref[cs[i], p] = _dot(a_stacks[n], v_stacks[n])
            qs_ref[cs[i], p] = jnp.concatenate(
                [jnp.where(lane_t, q_ps[n], zb), jnp.where(lane_t, zb, q_ps[n])], axis=0)
        yield
        diags = [jnp.where(eye, jnp.broadcast_to(e_w[2 * i + p:2 * i + p + 1], (LANE, LANE)), 0.0).astype(BF16)
                 for i, p in pairs]
        kb_ws = [_dot(d, kb).astype(BF16) for d, kb in zip(diags, kbs)]
        for n, (i, p) in enumerate(pairs):
            u_ref[cs[i], p] = _dot_tn(kb_ws[n], v_stacks[n])
        yield

    def carry_state(c):
        cc = jnp.where(direction == 0, c, nc - 1 - c)
        rows = pl.ds(pl.multiple_of(cc * L, L), L)
        m_all = mm_ref[0, 0]
        sc = sc_ref[c]
        c_t = ct_ref[c]
        m_loc = mloc_ref[c]
        outs, states, m_rows = [], [], []
        for p in range(ML_HEADS // 2):
            c_pair = c_ref[0, 0, p]
            inter = _dot(qs_ref[c, p], c_pair.astype(BF16))
            intra_p = p_ref[c, p]
            inc = u_ref[c, p]
            new_rows = []
            for hh in range(2):
                h = 2 * p + hh
                rs = slice(hh * L, (hh + 1) * L)
                ln = slice(h * ML_V, (h + 1) * ML_V)
                m_in = m_all[h:h + 1]
                m_t = jnp.maximum(m_loc[:, ln], c_t[:, ln] + m_in)
                alpha = jnp.exp(m_loc[:, ln] - m_t)
                beta = jnp.exp(c_t[:, ln] + m_in - m_t)
                num = alpha * intra_p[rs, :ML_V] + beta * inter[rs, :ML_V]
                den = alpha * intra_p[rs, ML_V:] + beta * inter[rs, ML_V:]
                outs.append(num / jnp.maximum(jnp.abs(den), jnp.exp(-m_t)))
                b_tot = sc[h:h + 1]
                m_new = b_tot + jnp.maximum(m_in, sc[ML_HEADS + h:ML_HEADS + h + 1])
                carry_w = jnp.exp(b_tot + m_in - m_new)
                inc_w = jnp.exp(b_tot + sc[ML_HEADS + h:ML_HEADS + h + 1] - m_new)
                cw2 = jnp.concatenate([carry_w, carry_w], axis=1)
                iw2 = jnp.concatenate([inc_w, inc_w], axis=1)
                new_rows.append(cw2 * c_pair[rs] + iw2 * inc[rs])
                m_rows.append(m_new)
            states.append(jnp.concatenate(new_rows, axis=0))
        o_ref[0, 0, rows, :] = jnp.concatenate(outs, axis=1).astype(BF16)
        for p in range(ML_HEADS // 2):
            c_ref[0, 0, p] = states[p]
        mm_ref[0, 0] = jnp.concatenate(m_rows + [m_all[ML_HEADS:]], axis=0)

    _run_pipelined(nc // group, group, intra, carry_state)


def _mlstm(qk, v, gates, c0, m0, le2, let2):
    bsz, t, hw = v.shape
    nc = min(ML_CHUNKS, t // CHUNK)
    tb = nc * CHUNK
    nblk = t // tb
    npair = ML_HEADS // 2

    def blk(j, d):
        return jnp.where(d == 0, j, nblk - 1 - j)

    c_spec = pl.BlockSpec((1, 1) + c0.shape[2:], lambda b, d, j: (b, d, 0, 0, 0))
    m_spec = pl.BlockSpec((1, 1) + m0.shape[2:], lambda b, d, j: (b, d, 0, 0))
    return pl.pallas_call(
        functools.partial(_mlstm_kernel, nc=nc),
        grid=(bsz, 2, nblk),
        in_specs=[pl.BlockSpec((1, tb, qk.shape[-1]), lambda b, d, j: (b, blk(j, d), 0)),
                  pl.BlockSpec((1, tb, hw), lambda b, d, j: (b, blk(j, d), 0)),
                  pl.BlockSpec((1, nc) + gates.shape[2:], lambda b, d, j: (b, blk(j, d), 0, 0)),
                  pl.BlockSpec((1,) + le2.shape[1:], lambda b, d, j: (d, 0, 0)),
                  pl.BlockSpec((1,) + let2.shape[1:], lambda b, d, j: (d, 0, 0)),
                  c_spec, m_spec],
        out_specs=[pl.BlockSpec((1, 1, tb, hw), lambda b, d, j: (b, d, blk(j, d), 0)), c_spec, m_spec],
        out_shape=[jax.ShapeDtypeStruct((bsz, 2, t, hw), BF16),
                   jax.ShapeDtypeStruct(c0.shape, F32),
                   jax.ShapeDtypeStruct(m0.shape, F32)],
        scratch_shapes=[pltpu.VMEM((nc, npair, 2 * CHUNK, 2 * ML_V), F32),
                        pltpu.VMEM((nc, npair, 2 * ML_QK, 2 * ML_V), F32),
                        pltpu.VMEM((nc, npair, 2 * CHUNK, LANE), BF16),
                        pltpu.VMEM((nc, CHUNK, hw), F32),
                        pltpu.VMEM((nc, CHUNK, hw), F32),
                        pltpu.VMEM((nc, 8, LANE), F32)],
        compiler_params=_cparams(("parallel", "parallel", "arbitrary")),
        name="mlstm_scan",
    )(qk, v, gates, le2, let2, c0, m0)


def _head_rms(x, heads):
    dh = x.shape[-1] // heads
    return jnp.concatenate([_rms(x[:, h * dh:(h + 1) * dh]) for h in range(heads)], axis=1)


def _mix_ffn_kernel(oh_ref, om_ref, hg_ref, mo_ref, x_ref, mod_ref, hnw_ref, mnw_ref, wo_ref,
                    nw_ref, w1_ref, w2_ref, fw_ref, o_ref, a_ref, *, d_ff, tf):
    mix_gate = mod_ref[0, 0:1, :]
    mods = [mod_ref[0, k:k + 1, :] for k in range(1, 4)]
    for r0 in range(0, x_ref.shape[1], ROW_SUB):
        rs = slice(r0, r0 + ROW_SUB)
        oh = oh_ref[0, 0, rs, :].astype(F32) + oh_ref[0, 1, rs, :].astype(F32)
        om = om_ref[0, 0, rs, :].astype(F32) + om_ref[0, 1, rs, :].astype(F32)
        hg_out = _head_rms(oh, HG_HEADS) * hnw_ref[...] * _silu(hg_ref[0, rs, :].astype(F32))
        ml_out = _sigmoid(mo_ref[0, rs, :].astype(F32)) * (_head_rms(om, ML_HEADS) * mnw_ref[...])
        merged = jnp.concatenate([hg_out, ml_out], axis=1).astype(BF16)
        x = x_ref[0, rs, :] + mix_gate * _dot(merged, wo_ref[...])
        y = _ffn_rows(x, mods, nw_ref, w1_ref, w2_ref, a_ref, rs, d_ff, tf)
        o_ref[0, rs, :] = _rms(y) * fw_ref[...]


def _mix_ffn(oh, om, hg, mo, x, mods, hnw, mnw, w_out, nw, w1, w2, fw):
    bsz, t, d = x.shape
    hw = hg.shape[-1]
    d_ff = w2.shape[0]
    tm = min(TOKEN_TILE, t)
    tok = lambda b, i: (b, i, 0)
    tok2 = lambda b, i: (b, 0, i, 0)
    const = lambda b, i: (0, 0)
    resident = lambda a: pl.BlockSpec(a.shape, const, pipeline_mode=pl.Buffered(1))
    return pl.pallas_call(
        functools.partial(_mix_ffn_kernel, d_ff=d_ff, tf=FF_TILE),
        grid=(bsz, t // tm),
        in_specs=[pl.BlockSpec((1, 2, tm, hw), tok2), pl.BlockSpec((1, 2, tm, hw), tok2),
                  pl.BlockSpec((1, tm, hw), tok), pl.BlockSpec((1, tm, hw), tok),
                  pl.BlockSpec((1, tm, d), tok),
                  pl.BlockSpec((1, 4, d), lambda b, i: (b, 0, 0)),
                  pl.BlockSpec((1, hw), const), pl.BlockSpec((1, hw), const), resident(w_out),
                  pl.BlockSpec((1, d), const), resident(w1), resident(w2), pl.BlockSpec((1, d), const)],
        out_specs=pl.BlockSpec((1, tm, d), tok),
        out_shape=jax.ShapeDtypeStruct((bsz, t, d), F32),
        scratch_shapes=[pltpu.VMEM((tm, d_ff), BF16)],
        compiler_params=_cparams(("parallel", "parallel")),
        name="mix_ffn_final",
    )(oh, om, hg, mo, x, mods, hnw, mnw, w_out, nw, w1, w2, fw)


def kernel(x, c, ctx, c_ctx, w_mod, b_mod, norm1_w, ffn1_w1, ffn1_w2, norm2_w, w_in, ml_gate_b,
           ml_conv_w, ml_conv_b, hg_lb_logits, hg_norm_w, ml_norm_w, w_out, norm3_w, ffn2_w1, ffn2_w2,
           final_norm_w):
    bsz, seq, d = x.shape
    n_ctx = ctx.shape[1]
    assert w_mod.shape[0] == 1, "single-layer kernel"
    assert seq % (GRID_W * 8) == 0 and n_ctx % CHUNK == 0
    hw = HG_HEADS * HG_DIM
    ng = ml_gate_b.shape[-1]

    rows = -(-(bsz + 1) // 8) * 8
    cvec = jnp.zeros((rows, d), F32).at[:bsz].set(c).at[bsz].set(c_ctx)
    mods = _modulation(cvec, w_mod[0], b_mod[0][None, :]).reshape(rows, N_MOD, d)
    lat_row = lambda b: b
    ctx_row = lambda b: bsz

    row = lambda a: a.reshape(1, -1)
    w1a, w2a = ffn1_w1[0].astype(BF16), ffn1_w2[0].astype(BF16)
    x1 = _ffn(x, mods[:, 0:3], lat_row, row(norm1_w[0]), w1a, w2a)
    s1 = _ffn(ctx, mods[:, 0:3], ctx_row, row(norm1_w[0]), w1a, w2a)

    w_in_b = w_in[0].astype(BF16)
    gate_b = row(ml_gate_b[0])
    dmat, pair_valid, wmask = (jnp.asarray(a) for a in _hgrn2_masks())
    le2, let2 = (jnp.asarray(a) for a in _mlstm_masks())
    conv_w, conv_b = ml_conv_w[0], row(ml_conv_b[0])
    npair = ML_HEADS // 2

    def mixer_scans(s, mod_row, width, s0, c0, m0):
        hq, hv, hg, hlf, mqk, mv, mo, gcol = _inproj(
            s, mods[:, 3:5], mod_row, row(norm2_w[0]), w_in_b, gate_b, hg_lb_logits)
        n_chunks = s.shape[1] // CHUNK
        qk = _conv(mqk, conv_w, conv_b, width)
        gates = gcol.reshape(bsz, n_chunks, CHUNK, 2, 2, npair, 2).transpose(0, 1, 3, 4, 5, 6, 2)
        gates = gates.reshape(bsz, n_chunks, 8, 2 * CHUNK)
        oh, s_fin = _hgrn2(hq, hv, hlf, s0, dmat, pair_valid, wmask)
        om, c_fin, m_fin = _mlstm(qk, mv, gates, c0, m0, le2, let2)
        return oh, om, hg, mo, s_fin, c_fin, m_fin

    s0 = jnp.zeros((bsz, 2, HG_HEADS, HG_DIM, HG_DIM), F32)
    c0 = jnp.zeros((bsz, 2, npair, 2 * ML_QK, 2 * ML_V), F32)
    m0 = jnp.zeros((bsz, 2, 8, LANE), F32)
    _, _, _, _, s_ctx, c_ctx_state, m_ctx = mixer_scans(s1, ctx_row, n_ctx, s0, c0, m0)
    oh, om, hg, mo, _, _, _ = mixer_scans(x1, lat_row, GRID_W, s_ctx, c_ctx_state, m_ctx)

    return _mix_ffn(oh, om, hg, mo, x1, mods[:, 5:9], row(hg_norm_w[0]), row(ml_norm_w[0]),
                    w_out[0].astype(BF16), row(norm3_w[0]), ffn2_w1[0].astype(BF16),
                    ffn2_w2[0].astype(BF16), row(final_norm_w))
```

```python
import functools

import numpy as np
import jax
import jax.numpy as jnp
from jax import lax
from jax.experimental import pallas as pl
from jax.experimental.pallas import tpu as pltpu

F32 = jnp.float32
BF16 = jnp.bfloat16

EPS = 1e-6
CHUNK = 64
GRID_W = 64
N_MOD = 9
HG_HEADS = 4
HG_DIM = 128
ML_HEADS = 4
ML_QK = 64
ML_V = 128
SUB = 16
N_SUB = CHUNK // SUB
GROUP = 4
ML_GROUP = 4
HG_CHUNKS = 8
ML_CHUNKS = 16
TOKEN_TILE = 512
ROW_SUB = 256
FF_TILE = 256
LOG2E = 1.4426950408889634
EXP2_CLAMP = 115.0
LANE = 128
VMEM_LIMIT = 56 * 1024 * 1024


def _cparams(sem):
    return pltpu.CompilerParams(dimension_semantics=sem, vmem_limit_bytes=VMEM_LIMIT)


def _dot(a, b):
    return jnp.dot(a, b, preferred_element_type=F32)


def _dot_nt(a, b):
    return lax.dot_general(a, b, (((1,), (1,)), ((), ())), preferred_element_type=F32)


def _dot_tn(a, b):
    return lax.dot_general(a, b, (((0,), (0,)), ((), ())), preferred_element_type=F32)


def _sigmoid(x):
    return 1.0 / (1.0 + jnp.exp(-x))


def _silu(x):
    return x * _sigmoid(x)


def _log_sigmoid(x):
    return jnp.minimum(x, 0.0) - jnp.log(1.0 + jnp.exp(-jnp.abs(x)))


def _split3(x):
    hi = x.astype(BF16)
    r = x - hi.astype(F32)
    mid = r.astype(BF16)
    lo = (r - mid.astype(F32)).astype(BF16)
    return hi, mid, lo


def _split2(x):
    hi = x.astype(BF16)
    return hi, (x - hi.astype(F32)).astype(BF16)


def _rms(x):
    return x * lax.rsqrt(jnp.mean(x * x, axis=-1, keepdims=True) + EPS)


def _mod_kernel(c_ref, w_ref, b_ref, o_ref):
    a = _silu(c_ref[...])
    a_hi = a.astype(BF16)
    a_lo = (a - a_hi.astype(F32)).astype(BF16)
    w = w_ref[...]
    w_hi = w.astype(BF16)
    w_lo = (w - w_hi.astype(F32)).astype(BF16)
    o_ref[...] = _dot(a_hi, w_hi) + _dot(a_hi, w_lo) + _dot(a_lo, w_hi) + b_ref[...]


def _modulation(cvec, w_mod, b_mod):
    rows, d = cvec.shape
    n = w_mod.shape[1]
    tn = 1024
    return pl.pallas_call(
        _mod_kernel,
        grid=(n // tn,),
        in_specs=[pl.BlockSpec((rows, d), lambda j: (0, 0)),
                  pl.BlockSpec((d, tn), lambda j: (0, j)),
                  pl.BlockSpec((1, tn), lambda j: (0, j))],
        out_specs=pl.BlockSpec((rows, tn), lambda j: (0, j)),
        out_shape=jax.ShapeDtypeStruct((rows, n), F32),
        compiler_params=_cparams(("arbitrary",)),
        name="modulation",
    )(cvec, w_mod, b_mod)


def _ffn_rows(x, mods, nw_ref, w1_ref, w2_ref, a_ref, rs, d_ff, tf):
    shift, scale, gate = mods
    h = (_rms(x) * nw_ref[...] * (1.0 + scale) + shift).astype(BF16)
    for c in range(d_ff // tf):
        g = _dot(h, w1_ref[:, c * tf:(c + 1) * tf])
        u = _dot(h, w1_ref[:, d_ff + c * tf:d_ff + (c + 1) * tf])
        a_ref[rs, c * tf:(c + 1) * tf] = (_silu(g) * u).astype(BF16)
    return x + 0.5 * gate * _dot(a_ref[rs, :], w2_ref[...])


def _ffn_kernel(x_ref, mod_ref, nw_ref, w1_ref, w2_ref, o_ref, a_ref, *, d_ff, tf):
    mods = [mod_ref[0, k:k + 1, :] for k in range(3)]
    for r0 in range(0, x_ref.shape[1], ROW_SUB):
        rs = slice(r0, r0 + ROW_SUB)
        o_ref[0, rs, :] = _ffn_rows(x_ref[0, rs, :], mods, nw_ref, w1_ref, w2_ref, a_ref, rs, d_ff, tf)


def _ffn(s, mods, mod_row, nw, w1, w2):
    bsz, t, d = s.shape
    d_ff = w2.shape[0]
    tm = min(TOKEN_TILE, t)
    const = lambda b, i: (0, 0)
    return pl.pallas_call(
        functools.partial(_ffn_kernel, d_ff=d_ff, tf=FF_TILE),
        grid=(bsz, t // tm),
        in_specs=[pl.BlockSpec((1, tm, d), lambda b, i: (b, i, 0)),
                  pl.BlockSpec((1, 3, d), lambda b, i: (mod_row(b), 0, 0)),
                  pl.BlockSpec((1, d), const),
                  pl.BlockSpec((d, 2 * d_ff), const, pipeline_mode=pl.Buffered(1)),
                  pl.BlockSpec((d_ff, d), const, pipeline_mode=pl.Buffered(1))],
        out_specs=pl.BlockSpec((1, tm, d), lambda b, i: (b, i, 0)),
        out_shape=jax.ShapeDtypeStruct((bsz, t, d), F32),
        scratch_shapes=[pltpu.VMEM((tm, d_ff), BF16)],
        compiler_params=_cparams(("parallel", "parallel")),
        name="ffn",
    )(s, mods, nw, w1, w2)


def _inproj_kernel(x_ref, mod_ref, nw_ref, w_ref, wg_ref, gb_ref, lbl_ref,
                   hq_ref, hv_ref, hg_ref, hlf_ref, mqk_ref, mv_ref, mo_ref, gc_ref, *, hw):
    shift = mod_ref[0, 0:1, :]
    scale = mod_ref[0, 1:2, :]
    lbl = lbl_ref[...]
    e = jnp.exp(lbl - jnp.max(lbl, axis=0, keepdims=True))
    lb = e[0] / jnp.sum(e, axis=0)
    first_half = lax.broadcasted_iota(jnp.int32, (8, LANE), 1) < CHUNK
    sub = min(ROW_SUB, x_ref.shape[1])
    for r0 in range(0, x_ref.shape[1], sub):
        rs = slice(r0, r0 + sub)
        h = (_rms(x_ref[0, rs, :]) * nw_ref[...] * (1.0 + scale) + shift).astype(BF16)

        def proj(k):
            return _dot(h, w_ref[:, k * hw:(k + 1) * hw])

        hq_ref[0, rs, :] = _silu(proj(0)) * (HG_DIM ** -0.5)
        hv_ref[0, rs, :] = proj(1).astype(BF16)
        hg_ref[0, rs, :] = proj(2).astype(BF16)
        for d in range(2):
            lbd = lb[d:d + 1, :]
            hlf_ref[0, d, rs, :] = jnp.log(lbd + (1.0 - lbd) * _sigmoid(proj(3 + d))) * LOG2E
        mqk_ref[0, rs, :] = proj(5).astype(BF16)
        mv_ref[0, rs, :] = proj(6).astype(BF16)
        mo_ref[0, rs, :] = proj(7).astype(BF16)
        gt = _dot_nt(wg_ref[...], h) + gb_ref[...]
        for m in range(sub // LANE):
            top = gt[0:8, m * LANE:(m + 1) * LANE]
            bot = gt[8:16, m * LANE:(m + 1) * LANE]
            c0 = r0 // CHUNK + 2 * m
            gc_ref[0, c0] = jnp.where(first_half, top, pltpu.roll(bot, CHUNK, 1))
            gc_ref[0, c0 + 1] = jnp.where(first_half, pltpu.roll(top, CHUNK, 1), bot)


def _inproj(s, mods, mod_row, nw, w_in, w_gate_t, gate_b, lb_logits):
    bsz, t, d = s.shape
    hw = HG_HEADS * HG_DIM
    ng = gate_b.shape[0]
    tm = min(TOKEN_TILE, t)
    tok4 = lambda b, i: (b, i, 0, 0)
    const = lambda b, i: (0, 0)
    tok = lambda b, i: (b, i, 0)
    tok2 = lambda b, i: (b, 0, i, 0)
    f = lambda dt, *shape: jax.ShapeDtypeStruct(shape, dt)
    return pl.pallas_call(
        functools.partial(_inproj_kernel, hw=hw),
        grid=(bsz, t // tm),
        in_specs=[pl.BlockSpec((1, tm, d), tok),
                  pl.BlockSpec((1, 2, d), lambda b, i: (mod_row(b), 0, 0)),
                  pl.BlockSpec((1, d), const),
                  pl.BlockSpec(w_in.shape, const, pipeline_mode=pl.Buffered(1)),
                  pl.BlockSpec(w_gate_t.shape, const),
                  pl.BlockSpec((ng, 1), const),
                  pl.BlockSpec(lb_logits.shape, lambda b, i: (0, 0, 0))],
        out_specs=[pl.BlockSpec((1, tm, hw), tok), pl.BlockSpec((1, tm, hw), tok),
                   pl.BlockSpec((1, tm, hw), tok),
                   pl.BlockSpec((1, 2, tm, hw), tok2),
                   pl.BlockSpec((1, tm, hw), tok), pl.BlockSpec((1, tm, hw), tok),
                   pl.BlockSpec((1, tm, hw), tok), pl.BlockSpec((1, tm // CHUNK, ng // 2, LANE), tok4)],
        out_shape=[f(F32, bsz, t, hw), f(BF16, bsz, t, hw), f(BF16, bsz, t, hw),
                   f(F32, bsz, 2, t, hw),
                   f(BF16, bsz, t, hw), f(BF16, bsz, t, hw), f(BF16, bsz, t, hw),
                   f(F32, bsz, t // CHUNK, ng // 2, LANE)],
        compiler_params=_cparams(("parallel", "parallel")),
        name="inproj",
    )(s, mods, nw, w_in, w_gate_t, gate_b, lb_logits)


def _conv_kernel(x_ref, w_ref, b_ref, o_ref, *, width, q_tiles):
    x = x_ref[0].astype(F32)
    t = x.shape[0]
    col = lax.broadcasted_iota(jnp.int32, x.shape, 0) % width
    xl = jnp.where(col == 0, 0.0, pltpu.roll(x, 1, 0))
    xr = jnp.where(col == width - 1, 0.0, pltpu.roll(x, t - 1, 0))

    def row(di):
        return (w_ref[di, 0:1, :] * xl + w_ref[di, 1:2, :] * x + w_ref[di, 2:3, :] * xr)

    y = row(1)
    if t > width:
        pad = jnp.zeros((width, x.shape[1]), F32)
        y = y + jnp.concatenate([pad, row(0)[:t - width]], axis=0)
        y = y + jnp.concatenate([row(2)[width:], pad], axis=0)
    y = _silu(y + b_ref[...])
    qscale = jnp.where(pl.program_id(1) < q_tiles, ML_QK ** -0.5, 1.0)
    o_ref[0] = (y * qscale).astype(BF16)


def _conv(x, w, b, width):
    bsz, t, ch = x.shape
    return pl.pallas_call(
        functools.partial(_conv_kernel, width=width, q_tiles=ML_HEADS * ML_QK // LANE),
        grid=(bsz, ch // LANE),
        in_specs=[pl.BlockSpec((1, t, LANE), lambda b_, c: (b_, 0, c)),
                  pl.BlockSpec((3, 3, LANE), lambda b_, c: (0, 0, c)),
                  pl.BlockSpec((1, LANE), lambda b_, c: (0, c))],
        out_specs=pl.BlockSpec((1, t, LANE), lambda b_, c: (b_, 0, c)),
        out_shape=jax.ShapeDtypeStruct((bsz, t, ch), BF16),
        compiler_params=_cparams(("parallel", "parallel")),
        name="conv",
    )(x, w, b)


def _positions():
    t = np.arange(CHUNK)
    return [t, CHUNK - 1 - t]


def _hgrn2_masks():
    ds, vs, ws = [], [], []
    for p in _positions():
        le = (p[None, :] <= p[:, None]).astype(np.float32)
        order = [p[SUB * j] // SUB for j in range(N_SUB)]
        ref = np.zeros((32, CHUNK), np.float32)
        valid = np.zeros((N_SUB * N_SUB, 1), np.float32)
        for j in range(N_SUB):
            ref[j] = p <= SUB * order[j] + SUB // 2 - 1
            ref[8 + j] = p <= SUB * order[j] + SUB - 1
        ref[N_SUB] = 1.0
        for j in range(N_SUB):
            for i in range(N_SUB):
                if order[i] < order[j]:
                    ref[16 + N_SUB * j + i] = ref[j] - ref[8 + i]
                    valid[N_SUB * j + i] = 1.0
        ds.append(np.concatenate([le, ref], axis=0))
        vs.append(np.broadcast_to(valid, (N_SUB * N_SUB, HG_HEADS * HG_DIM)))
        tok_blk = np.arange(CHUNK)[:, None] // SUB
        ws.append(np.concatenate([(tok_blk == j) * le for j in range(N_SUB)], axis=1))
    return (np.stack(ds).astype(np.float32), np.stack(vs).astype(np.float32),
            np.stack(ws).astype(np.float32))


def _mlstm_masks():
    le2s, let2s = [], []
    for p in _positions():
        le = (p[None, :] <= p[:, None]).astype(np.float32)
        le2s.append(np.concatenate([le, le], axis=1))
        let2 = np.zeros((2 * CHUNK, 2 * CHUNK), np.float32)
        let2[:CHUNK, :CHUNK] = le.T
        let2[CHUNK:, CHUNK:] = le.T
        let2s.append(let2)
    return np.stack(le2s), np.stack(let2s)


def _run_pipelined(n_groups, group, stages_of, step_of):
    queue = []
    for g in range(n_groups):
        for _ in stages_of(g):
            if queue:
                queue.pop(0)()
        while queue:
            queue.pop(0)()
        queue = [functools.partial(step_of, g * group + i) for i in range(group)]
    for step in queue:
        step()


def _chunk_rows(c, nc, direction):
    cc = jnp.where(direction == 0, c, nc - 1 - c)
    return pl.multiple_of(cc * CHUNK, CHUNK)


def _hgrn2_kernel(q_ref, v_ref, lf_ref, d_ref, pv_ref, w_ref, s0_ref, o_ref, st_ref,
                  u_ref, qin_ref, g_ref, oi_ref, bx_ref, kx_ref, vx_ref, *, nc):
    direction = pl.program_id(1)

    @pl.when(pl.program_id(2) == 0)
    def _():
        st_ref[...] = s0_ref[...]

    dmat = d_ref[0].astype(BF16)
    pair_valid = pv_ref[0]
    wmask = w_ref[0] > 0.0
    L = CHUNK
    hw = q_ref.shape[-1]
    upper_rows = lax.broadcasted_iota(jnp.int32, (L, 2 * L), 0) < L // 2

    group = min(GROUP, nc)

    def intra(grp):
        cs = [grp * group + i for i in range(group)]
        rows = [pl.ds(_chunk_rows(c, nc, direction), L) for c in cs]
        items = [(i, h) for i in range(group) for h in range(HG_HEADS)]
        lanes = [slice(h * HG_DIM, (h + 1) * HG_DIM) for h in range(HG_HEADS)]
        lfs = [lf_ref[0, 0, r, :] for r in rows]
        parts = [_split2(lf) for lf in lfs]
        e_all = sum(_dot(dmat, jnp.concatenate([parts[i][n] for i in range(group)], axis=1))
                    for n in range(2))
        yield
        q_mids, k_mids, k_outs, vbs, spans = [], [], [], [], []
        for i in range(group):
            e = e_all[:, i * hw:(i + 1) * hw]
            q = q_ref[0, rows[i], :]
            k = 1.0 - jnp.exp2(lfs[i])
            b = e[0:L]
            b_tot = e[L + N_SUB:L + N_SUB + 1]
            sub_rows = lambda r0: jnp.concatenate(
                [jnp.broadcast_to(e[r0 + j:r0 + j + 1], (SUB, hw)) for j in range(N_SUB)], axis=0)
            mid_rows = sub_rows(L)
            end_rows = sub_rows(L + 8)
            to_mid = b - mid_rows
            spans.append(jnp.max(jnp.abs(to_mid)))
            q_mids.append((q * jnp.exp2(jnp.minimum(to_mid, EXP2_CLAMP))).astype(BF16))
            k_diag = k * jnp.exp2(jnp.minimum(-to_mid, EXP2_CLAMP))
            k_end = k * jnp.exp2(end_rows - b)
            cross = jnp.exp2(e[L + 16:L + 32]) * pair_valid
            slabs = []
            for j in range(N_SUB):
                for ib in range(N_SUB):
                    rs = slice(ib * SUB, (ib + 1) * SUB)
                    row = N_SUB * j + ib
                    slabs.append(k_diag[rs] if ib == j else k_end[rs] * cross[row:row + 1])
            k_mids.append(jnp.concatenate(slabs, axis=0).astype(BF16))
            k_outs.append((k * jnp.exp2(b_tot - b)).astype(BF16))
            vbs.append(v_ref[0, rows[i], :])
            qin_ref[cs[i]] = (q * jnp.exp2(b)).astype(BF16)
            g_ref[cs[i]] = jnp.broadcast_to(jnp.exp2(b_tot), (8, hw))
            yield
        scores = [_dot_nt(q_mids[i][:, lanes[h]], k_mids[i][:, lanes[h]]) for i, h in items]
        yield
        a_s = [jnp.where(wmask, s, 0.0) for s in scores]
        a_s = [jnp.where(upper_rows, a[:, :2 * L], a[:, 2 * L:]).astype(BF16) for a in a_s]
        v2s = [jnp.concatenate([vb, vb], axis=0) for vb in vbs]
        outs = [_dot(a_s[n], v2s[i][:, lanes[h]]) for n, (i, h) in enumerate(items)]
        yield
        for i in range(group):
            oi_ref[cs[i]] = jnp.concatenate(outs[i * HG_HEADS:(i + 1) * HG_HEADS], axis=1)
        yield
        for i, h in items:
            u_ref[cs[i], h] = _dot_tn(vbs[i][:, lanes[h]], k_outs[i][:, lanes[h]])

        @pl.when(functools.reduce(jnp.maximum, spans) > EXP2_CLAMP)
        def _():
            for i in range(group):
                exact_intra(cs[i], rows[i])
        yield

    def exact_intra(c, rows):
        lf = lf_ref[0, 0, rows, :]
        bx_ref[...] = sum(_dot(dmat[0:L], x) for x in _split3(lf))
        kx_ref[...] = 1.0 - jnp.exp2(lf)
        vx_ref[...] = v_ref[0, rows, :].astype(F32)
        q = q_ref[0, rows, :]
        tok = lax.broadcasted_iota(jnp.int32, (L, 1), 0)
        pos = jnp.where(direction == 0, tok, L - 1 - tok)

        def add_source(s, acc):
            b_s = bx_ref[pl.ds(s, 1), :]
            z = q * kx_ref[pl.ds(s, 1), :] * jnp.exp2(jnp.minimum(bx_ref[...] - b_s, 0.0))
            v_s = vx_ref[pl.ds(s, 1), :]
            seen = pos >= jnp.where(direction == 0, s, L - 1 - s)
            cols = [jnp.where(seen, jnp.sum(z[:, ln], axis=1, keepdims=True), 0.0) * v_s[:, ln]
                    for ln in (slice(h * HG_DIM, (h + 1) * HG_DIM) for h in range(HG_HEADS))]
            return acc + jnp.concatenate(cols, axis=1)

        oi_ref[c] = lax.fori_loop(0, L, add_source, jnp.zeros((L, hw), F32))

    def carry_state(c):
        rows = pl.ds(_chunk_rows(c, nc, direction), L)
        q_in = qin_ref[c]
        g = g_ref[c][0:1]
        outs, states = [], []
        for h in range(HG_HEADS):
            ln = slice(h * HG_DIM, (h + 1) * HG_DIM)
            st = st_ref[0, 0, h]
            outs.append(_dot_nt(q_in[:, ln], st.astype(BF16)))
            states.append(g[:, ln] * st + u_ref[c, h])
        o_ref[0, 0, rows, :] = (oi_ref[c] + jnp.concatenate(outs, axis=1)).astype(BF16)
        for h in range(HG_HEADS):
            st_ref[0, 0, h] = states[h]

    _run_pipelined(nc // group, group, intra, carry_state)


def _hgrn2(q, v, lf, s0, dmat, pair_valid, wmask):
    bsz, t, hw = q.shape
    nc = min(HG_CHUNKS, t // CHUNK)
    tb = nc * CHUNK
    nblk = t // tb

    def blk(j, d):
        return jnp.where(d == 0, j, nblk - 1 - j)

    st_spec = pl.BlockSpec((1, 1) + s0.shape[2:], lambda b, d, j: (b, d, 0, 0, 0))
    return pl.pallas_call(
        functools.partial(_hgrn2_kernel, nc=nc),
        grid=(bsz, 2, nblk),
        in_specs=[pl.BlockSpec((1, tb, hw), lambda b, d, j: (b, blk(j, d), 0)),
                  pl.BlockSpec((1, tb, hw), lambda b, d, j: (b, blk(j, d), 0)),
                  pl.BlockSpec((1, 1, tb, hw), lambda b, d, j: (b, d, blk(j, d), 0)),
                  pl.BlockSpec((1,) + dmat.shape[1:], lambda b, d, j: (d, 0, 0)),
                  pl.BlockSpec((1,) + pair_valid.shape[1:], lambda b, d, j: (d, 0, 0)),
                  pl.BlockSpec((1,) + wmask.shape[1:], lambda b, d, j: (d, 0, 0)),
                  st_spec],
        out_specs=[pl.BlockSpec((1, 1, tb, hw), lambda b, d, j: (b, d, blk(j, d), 0)), st_spec],
        out_shape=[jax.ShapeDtypeStruct((bsz, 2, t, hw), BF16),
                   jax.ShapeDtypeStruct(s0.shape, F32)],
        scratch_shapes=[pltpu.VMEM((nc,) + s0.shape[2:], F32),
                        pltpu.VMEM((nc, CHUNK, hw), BF16),
                        pltpu.VMEM((nc, 8, hw), F32),
                        pltpu.VMEM((nc, CHUNK, hw), F32),
                        pltpu.VMEM((CHUNK, hw), F32),
                        pltpu.VMEM((CHUNK, hw), F32),
                        pltpu.VMEM((CHUNK, hw), F32)],
        compiler_params=_cparams(("parallel", "parallel", "arbitrary")),
        name="hgrn2_scan",
    )(q, v, lf, dmat, pair_valid, wmask, s0)


def _mlstm_kernel(qk_ref, v_ref, g_ref, le2_ref, let2_ref, c0_ref, m0_ref,
                  o_ref, c_ref, mm_ref,
                  p_ref, u_ref, qs_ref, ct_ref, mloc_ref, sc_ref, *, nc):
    direction = pl.program_id(1)

    @pl.when(pl.program_id(2) == 0)
    def _():
        c_ref[...] = c0_ref[...]
        mm_ref[...] = m0_ref[...]

    le2 = le2_ref[0].astype(BF16)
    let2 = let2_ref[0].astype(BF16)
    causal = le2_ref[0] > 0.0
    L = CHUNK
    neg = -jnp.inf
    lane = lax.broadcasted_iota(jnp.int32, (1, LANE), 1)
    seg = [lane < L, lane >= L]
    lane_t = lax.broadcasted_iota(jnp.int32, (L, LANE), 1) < L
    eye = (lax.broadcasted_iota(jnp.int32, (LANE, LANE), 0)
           == lax.broadcasted_iota(jnp.int32, (LANE, LANE), 1))
    npair = ML_HEADS // 2
    ones_v = jnp.ones((L, ML_V), BF16)

    group = min(ML_GROUP, nc)

    def intra(grp):
        cs = [grp * group + i for i in range(group)]
        ccs = [jnp.where(direction == 0, c, nc - 1 - c) for c in cs]
        rows = [pl.ds(pl.multiple_of(cc * L, L), L) for cc in ccs]
        pairs = [(i, p) for i in range(group) for p in range(npair)]
        gs = [jnp.where(direction == 0, g_ref[0, cc][0:4], g_ref[0, cc][4:8]) for cc in ccs]
        pad = [jnp.zeros((-2 * group % 8, LANE), F32)] if 2 * group % 8 else []
        i2 = jnp.concatenate([g[0:2] for g in gs] + pad, axis=0)
        lf2 = _log_sigmoid(jnp.concatenate([g[2:4] for g in gs] + pad, axis=0))
        parts = _split3(lf2)
        r2 = i2 - sum(_dot(x, let2) for x in parts)
        g_a = jnp.max(jnp.where(seg[0], r2, neg), axis=1, keepdims=True)
        g_b = jnp.max(jnp.where(seg[1], r2, neg), axis=1, keepdims=True)
        e_w = jnp.exp(r2 - jnp.where(seg[0], g_a, g_b))
        bt_a = jnp.sum(jnp.where(seg[0], lf2, 0.0), axis=1, keepdims=True)
        bt_b = jnp.sum(jnp.where(seg[1], lf2, 0.0), axis=1, keepdims=True)
        yield

        def head_rows(x, i):
            zero = jnp.zeros_like(x[0:1])
            return jnp.concatenate(
                [jnp.broadcast_to(jnp.where(seg[h % 2], x[2 * i + h // 2:2 * i + h // 2 + 1], zero),
                                  (LANE, LANE)) for h in range(ML_HEADS)], axis=0)

        c_ts = [sum(_dot_nt(le2, head_rows(x, i)) for x in parts) for i in range(group)]
        for i in range(group):
            ct_ref[cs[i]] = c_ts[i]
            sc_ref[cs[i]] = jnp.concatenate(
                [jnp.broadcast_to(x[2 * i + p:2 * i + p + 1], (1, LANE))
                 for x2 in ((bt_a, bt_b), (g_a, g_b)) for p in range(npair) for x in x2], axis=0)
        yield
        qks = [qk_ref[0, r, :] for r in rows]
        vbs = [v_ref[0, r, :] for r in rows]
        q_ps = [qks[i][:, p * LANE:(p + 1) * LANE] for i, p in pairs]
        k_ps = [qks[i][:, (npair + p) * LANE:(npair + p + 1) * LANE] for i, p in pairs]
        zb = jnp.zeros((L, LANE), BF16)
        kbs = [jnp.concatenate([jnp.where(lane_t, k, zb), jnp.where(lane_t, zb, k)], axis=0) for k in k_ps]
        scores = [_dot_nt(q, kb) for q, kb in zip(q_ps, kbs)]
        yield
        ws = []
        for n, (i, p) in enumerate(pairs):
            ha, hb = 2 * p, 2 * p + 1
            t1 = jnp.where(lane_t, c_ts[i][:, ha * ML_V:(ha + 1) * ML_V], c_ts[i][:, hb * ML_V:(hb + 1) * ML_V])
            d_log = jnp.where(causal, t1 + r2[2 * i + p:2 * i + p + 1], neg)
            m_a = jnp.max(jnp.where(lane_t, d_log, neg), axis=1, keepdims=True)
            m_b = jnp.max(jnp.where(lane_t, neg, d_log), axis=1, keepdims=True)
            mloc_ref[cs[i], :, ha * ML_V:(ha + 1) * ML_V] = jnp.broadcast_to(m_a, (L, ML_V))
            mloc_ref[cs[i], :, hb * ML_V:(hb + 1) * ML_V] = jnp.broadcast_to(m_b, (L, ML_V))
            ws.append(jnp.exp(d_log - jnp.where(lane_t, m_a, m_b)))
        yield
        a_s = [(s_ * w).astype(BF16) for s_, w in zip(scores, ws)]
        a_stacks = [jnp.concatenate([jnp.where(lane_t, a, zb), jnp.where(lane_t, zb, a)], axis=0) for a in a_s]
        v_stacks = [jnp.concatenate(
            [jnp.concatenate([vbs[i][:, 2 * p * ML_V:(2 * p + 1) * ML_V], ones_v], axis=1),
             jnp.concatenate([vbs[i][:, (2 * p + 1) * ML_V:(2 * p + 2) * ML_V], ones_v], axis=1)], axis=0)
            for i, p in pairs]
        for n, (i, p) in enumerate(pairs):
            p_ref[cs[i], p] = _dot(a_stacks[n], v_stacks[n])
            qs_ref[cs[i], p] = jnp.concatenate(
                [jnp.where(lane_t, q_ps[n], zb), jnp.where(lane_t, zb, q_ps[n])], axis=0)
        yield
        diags = [jnp.where(eye, jnp.broadcast_to(e_w[2 * i + p:2 * i + p + 1], (LANE, LANE)), 0.0).astype(BF16)
                 for i, p in pairs]
        kb_ws = [_dot(d, kb).astype(BF16) for d, kb in zip(diags, kbs)]
        for n, (i, p) in enumerate(pairs):
            u_ref[cs[i], p] = _dot_tn(kb_ws[n], v_stacks[n])
        yield

    def carry_state(c):
        cc = jnp.where(direction == 0, c, nc - 1 - c)
        rows = pl.ds(pl.multiple_of(cc * L, L), L)
        m_all = mm_ref[0, 0]
        sc = sc_ref[c]
        c_t = ct_ref[c]
        m_loc = mloc_ref[c]
        outs, states, m_rows = [], [], []
        for p in range(ML_HEADS // 2):
            c_pair = c_ref[0, 0, p]
            inter = _dot(qs_ref[c, p], c_pair.astype(BF16))
            intra_p = p_ref[c, p]
            inc = u_ref[c, p]
            new_rows = []
            for hh in range(2):
                h = 2 * p + hh
                rs = slice(hh * L, (hh + 1) * L)
                ln = slice(h * ML_V, (h + 1) * ML_V)
                m_in = m_all[h:h + 1]
                m_t = jnp.maximum(m_loc[:, ln], c_t[:, ln] + m_in)
                alpha = jnp.exp(m_loc[:, ln] - m_t)
                beta = jnp.exp(c_t[:, ln] + m_in - m_t)
                num = alpha * intra_p[rs, :ML_V] + beta * inter[rs, :ML_V]
                den = alpha * intra_p[rs, ML_V:] + beta * inter[rs, ML_V:]
                outs.append(num / jnp.maximum(jnp.abs(den), jnp.exp(-m_t)))
                b_tot = sc[h:h + 1]
                m_new = b_tot + jnp.maximum(m_in, sc[ML_HEADS + h:ML_HEADS + h + 1])
                carry_w = jnp.exp(b_tot + m_in - m_new)
                inc_w = jnp.exp(b_tot + sc[ML_HEADS + h:ML_HEADS + h + 1] - m_new)
                cw2 = jnp.concatenate([carry_w, carry_w], axis=1)
                iw2 = jnp.concatenate([inc_w, inc_w], axis=1)
                new_rows.append(cw2 * c_pair[rs] + iw2 * inc[rs])
                m_rows.append(m_new)
            states.append(jnp.concatenate(new_rows, axis=0))
        o_ref[0, 0, rows, :] = jnp.concatenate(outs, axis=1).astype(BF16)
        for p in range(ML_HEADS // 2):
            c_ref[0, 0, p] = states[p]
        mm_ref[0, 0] = jnp.concatenate(m_rows + [m_all[ML_HEADS:]], axis=0)

    _run_pipelined(nc // group, group, intra, carry_state)


def _mlstm(qk, v, gates, c0, m0, le2, let2):
    bsz, t, hw = v.shape
    nc = min(ML_CHUNKS, t // CHUNK)
    tb = nc * CHUNK
    nblk = t // tb
    npair = ML_HEADS // 2

    def blk(j, d):
        return jnp.where(d == 0, j, nblk - 1 - j)

    c_spec = pl.BlockSpec((1, 1) + c0.shape[2:], lambda b, d, j: (b, d, 0, 0, 0))
    m_spec = pl.BlockSpec((1, 1) + m0.shape[2:], lambda b, d, j: (b, d, 0, 0))
    return pl.pallas_call(
        functools.partial(_mlstm_kernel, nc=nc),
        grid=(bsz, 2, nblk),
        in_specs=[pl.BlockSpec((1, tb, qk.shape[-1]), lambda b, d, j: (b, blk(j, d), 0)),
                  pl.BlockSpec((1, tb, hw), lambda b, d, j: (b, blk(j, d), 0)),
                  pl.BlockSpec((1, nc) + gates.shape[2:], lambda b, d, j: (b, blk(j, d), 0, 0)),
                  pl.BlockSpec((1,) + le2.shape[1:], lambda b, d, j: (d, 0, 0)),
                  pl.BlockSpec((1,) + let2.shape[1:], lambda b, d, j: (d, 0, 0)),
                  c_spec, m_spec],
        out_specs=[pl.BlockSpec((1, 1, tb, hw), lambda b, d, j: (b, d, blk(j, d), 0)), c_spec, m_spec],
        out_shape=[jax.ShapeDtypeStruct((bsz, 2, t, hw), BF16),
                   jax.ShapeDtypeStruct(c0.shape, F32),
                   jax.ShapeDtypeStruct(m0.shape, F32)],
        scratch_shapes=[pltpu.VMEM((nc, npair, 2 * CHUNK, 2 * ML_V), F32),
                        pltpu.VMEM((nc, npair, 2 * ML_QK, 2 * ML_V), F32),
                        pltpu.VMEM((nc, npair, 2 * CHUNK, LANE), BF16),
                        pltpu.VMEM((nc, CHUNK, hw), F32),
                        pltpu.VMEM((nc, CHUNK, hw), F32),
                        pltpu.VMEM((nc, 8, LANE), F32)],
        compiler_params=_cparams(("parallel", "parallel", "arbitrary")),
        name="mlstm_scan",
    )(qk, v, gates, le2, let2, c0, m0)


def _head_rms(x, heads):
    dh = x.shape[-1] // heads
    return jnp.concatenate([_rms(x[:, h * dh:(h + 1) * dh]) for h in range(heads)], axis=1)


def _mix_ffn_kernel(oh_ref, om_ref, hg_ref, mo_ref, x_ref, mod_ref, hnw_ref, mnw_ref, wo_ref,
                    nw_ref, w1_ref, w2_ref, fw_ref, o_ref, a_ref, *, d_ff, tf):
    mix_gate = mod_ref[0, 0:1, :]
    mods = [mod_ref[0, k:k + 1, :] for k in range(1, 4)]
    for r0 in range(0, x_ref.shape[1], ROW_SUB):
        rs = slice(r0, r0 + ROW_SUB)
        oh = oh_ref[0, 0, rs, :].astype(F32) + oh_ref[0, 1, rs, :].astype(F32)
        om = om_ref[0, 0, rs, :].astype(F32) + om_ref[0, 1, rs, :].astype(F32)
        hg_out = _head_rms(oh, HG_HEADS) * hnw_ref[...] * _silu(hg_ref[0, rs, :].astype(F32))
        ml_out = _sigmoid(mo_ref[0, rs, :].astype(F32)) * (_head_rms(om, ML_HEADS) * mnw_ref[...])
        merged = jnp.concatenate([hg_out, ml_out], axis=1).astype(BF16)
        x = x_ref[0, rs, :] + mix_gate * _dot(merged, wo_ref[...])
        y = _ffn_rows(x, mods, nw_ref, w1_ref, w2_ref, a_ref, rs, d_ff, tf)
        o_ref[0, rs, :] = _rms(y) * fw_ref[...]


def _mix_ffn(oh, om, hg, mo, x, mods, hnw, mnw, w_out, nw, w1, w2, fw):
    bsz, t, d = x.shape
    hw = hg.shape[-1]
    d_ff = w2.shape[0]
    tm = min(TOKEN_TILE, t)
    tok = lambda b, i: (b, i, 0)
    tok2 = lambda b, i: (b, 0, i, 0)
    const = lambda b, i: (0, 0)
    resident = lambda a: pl.BlockSpec(a.shape, const, pipeline_mode=pl.Buffered(1))
    return pl.pallas_call(
        functools.partial(_mix_ffn_kernel, d_ff=d_ff, tf=FF_TILE),
        grid=(bsz, t // tm),
        in_specs=[pl.BlockSpec((1, 2, tm, hw), tok2), pl.BlockSpec((1, 2, tm, hw), tok2),
                  pl.BlockSpec((1, tm, hw), tok), pl.BlockSpec((1, tm, hw), tok),
                  pl.BlockSpec((1, tm, d), tok),
                  pl.BlockSpec((1, 4, d), lambda b, i: (b, 0, 0)),
                  pl.BlockSpec((1, hw), const), pl.BlockSpec((1, hw), const), resident(w_out),
                  pl.BlockSpec((1, d), const), resident(w1), resident(w2), pl.BlockSpec((1, d), const)],
        out_specs=pl.BlockSpec((1, tm, d), tok),
        out_shape=jax.ShapeDtypeStruct((bsz, t, d), F32),
        scratch_shapes=[pltpu.VMEM((tm, d_ff), BF16)],
        compiler_params=_cparams(("parallel", "parallel")),
        name="mix_ffn_final",
    )(oh, om, hg, mo, x, mods, hnw, mnw, w_out, nw, w1, w2, fw)


def kernel(x, c, ctx, c_ctx, w_mod, b_mod, norm1_w, ffn1_w1, ffn1_w2, norm2_w, w_in, ml_gate_b,
           ml_conv_w, ml_conv_b, hg_lb_logits, hg_norm_w, ml_norm_w, w_out, norm3_w, ffn2_w1, ffn2_w2,
           final_norm_w):
    bsz, seq, d = x.shape
    n_ctx = ctx.shape[1]
    assert w_mod.shape[0] == 1, "single-layer kernel"
    assert seq % (GRID_W * 8) == 0 and n_ctx % CHUNK == 0
    hw = HG_HEADS * HG_DIM
    ng = ml_gate_b.shape[-1]

    rows = -(-(bsz + 1) // 8) * 8
    cvec = jnp.zeros((rows, d), F32).at[:bsz].set(c).at[bsz].set(c_ctx)
    mods = _modulation(cvec, w_mod[0], b_mod[0][None, :]).reshape(rows, N_MOD, d)
    lat_row = lambda b: b
    ctx_row = lambda b: bsz

    row = lambda a: a.reshape(1, -1)
    w1a, w2a = ffn1_w1[0].astype(BF16), ffn1_w2[0].astype(BF16)
    x1 = _ffn(x, mods[:, 0:3], lat_row, row(norm1_w[0]), w1a, w2a)
    s1 = _ffn(ctx, mods[:, 0:3], ctx_row, row(norm1_w[0]), w1a, w2a)

    w_in_b = w_in[0][:, :8 * hw].astype(BF16)
    gate_order = np.arange(ng).reshape(2, 2, ML_HEADS // 2, 2).transpose(3, 0, 1, 2).reshape(-1)
    w_gate_t = w_in[0][:, 8 * hw:].T[gate_order].astype(BF16)
    gate_b = ml_gate_b[0][gate_order].reshape(ng, 1)
    dmat, pair_valid, wmask = (jnp.asarray(a) for a in _hgrn2_masks())
    le2, let2 = (jnp.asarray(a) for a in _mlstm_masks())
    conv_w, conv_b = ml_conv_w[0], row(ml_conv_b[0])
    npair = ML_HEADS // 2

    def mixer_scans(s, mod_row, width, s0, c0, m0):
        hq, hv, hg, hlf, mqk, mv, mo, gates = _inproj(
            s, mods[:, 3:5], mod_row, row(norm2_w[0]), w_in_b, w_gate_t, gate_b, hg_lb_logits)
        qk = _conv(mqk, conv_w, conv_b, width)
        oh, s_fin = _hgrn2(hq, hv, hlf, s0, dmat, pair_valid, wmask)
        om, c_fin, m_fin = _mlstm(qk, mv, gates, c0, m0, le2, let2)
        return oh, om, hg, mo, s_fin, c_fin, m_fin

    s0 = jnp.zeros((bsz, 2, HG_HEADS, HG_DIM, HG_DIM), F32)
    c0 = jnp.zeros((bsz, 2, npair, 2 * ML_QK, 2 * ML_V), F32)
    m0 = jnp.zeros((bsz, 2, 8, LANE), F32)
    _, _, _, _, s_ctx, c_ctx_state, m_ctx = mixer_scans(s1, ctx_row, n_ctx, s0, c0, m0)
    oh, om, hg, mo, _, _, _ = mixer_scans(x1, lat_row, GRID_W, s_ctx, c_ctx_state, m_ctx)

    return _mix_ffn(oh, om, hg, mo, x1, mods[:, 5:9], row(hg_norm_w[0]), row(ml_norm_w[0]),
                    w_out[0].astype(BF16), row(norm3_w[0]), ffn2_w1[0].astype(BF16),
                    ffn2_w2[0].astype(BF16), row(final_norm_w))
```

```python
import functools

import numpy as np
import jax
import jax.numpy as jnp
from jax import lax
from jax.experimental import pallas as pl
from jax.experimental.pallas import tpu as pltpu

F32 = jnp.float32
BF16 = jnp.bfloat16

EPS = 1e-6
CHUNK = 64
GRID_W = 64
N_MOD = 9
HG_HEADS = 4
HG_DIM = 128
ML_HEADS = 4
ML_QK = 64
ML_V = 128
SUB = 16
N_SUB = CHUNK // SUB
GROUP = 4
ML_GROUP = 4
HG_CHUNKS = 8
ML_CHUNKS = 16
TOKEN_TILE = 512
ROW_SUB = 256
FF_TILE = 256
LOG2E = 1.4426950408889634
EXP2_CLAMP = 115.0
LANE = 128
VMEM_LIMIT = 56 * 1024 * 1024


def _cparams(sem):
    return pltpu.CompilerParams(dimension_semantics=sem, vmem_limit_bytes=VMEM_LIMIT)


def _dot(a, b):
    return jnp.dot(a, b, preferred_element_type=F32)


def _dot_nt(a, b):
    return lax.dot_general(a, b, (((1,), (1,)), ((), ())), preferred_element_type=F32)


def _dot_tn(a, b):
    return lax.dot_general(a, b, (((0,), (0,)), ((), ())), preferred_element_type=F32)


def _sigmoid(x):
    return 1.0 / (1.0 + jnp.exp(-x))


def _silu(x):
    return x * _sigmoid(x)


def _log_sigmoid(x):
    return jnp.minimum(x, 0.0) - jnp.log(1.0 + jnp.exp(-jnp.abs(x)))


def _split3(x):
    hi = x.astype(BF16)
    r = x - hi.astype(F32)
    mid = r.astype(BF16)
    lo = (r - mid.astype(F32)).astype(BF16)
    return hi, mid, lo


def _split2(x):
    hi = x.astype(BF16)
    return hi, (x - hi.astype(F32)).astype(BF16)


def _rms(x):
    return x * lax.rsqrt(jnp.mean(x * x, axis=-1, keepdims=True) + EPS)


def _mod_kernel(c_ref, w_ref, b_ref, o_ref):
    a = _silu(c_ref[...])
    a_hi = a.astype(BF16)
    a_lo = (a - a_hi.astype(F32)).astype(BF16)
    w = w_ref[...]
    w_hi = w.astype(BF16)
    w_lo = (w - w_hi.astype(F32)).astype(BF16)
    o_ref[...] = _dot(a_hi, w_hi) + _dot(a_hi, w_lo) + _dot(a_lo, w_hi) + b_ref[...]


def _modulation(cvec, w_mod, b_mod):
    rows, d = cvec.shape
    n = w_mod.shape[1]
    tn = 1024
    return pl.pallas_call(
        _mod_kernel,
        grid=(n // tn,),
        in_specs=[pl.BlockSpec((rows, d), lambda j: (0, 0)),
                  pl.BlockSpec((d, tn), lambda j: (0, j)),
                  pl.BlockSpec((1, tn), lambda j: (0, j))],
        out_specs=pl.BlockSpec((rows, tn), lambda j: (0, j)),
        out_shape=jax.ShapeDtypeStruct((rows, n), F32),
        compiler_params=_cparams(("arbitrary",)),
        name="modulation",
    )(cvec, w_mod, b_mod)


def _ffn_rows(x, mods, nw_ref, w1_ref, w2_ref, a_ref, rs, d_ff, tf):
    shift, scale, gate = mods
    h = (_rms(x) * nw_ref[...] * (1.0 + scale) + shift).astype(BF16)
    for c in range(d_ff // tf):
        g = _dot(h, w1_ref[:, c * tf:(c + 1) * tf].astype(BF16))
        u = _dot(h, w1_ref[:, d_ff + c * tf:d_ff + (c + 1) * tf].astype(BF16))
        a_ref[rs, c * tf:(c + 1) * tf] = (_silu(g) * u).astype(BF16)
    return x + 0.5 * gate * _dot(a_ref[rs, :], w2_ref[...].astype(BF16))


def _ffn_kernel(x_ref, mod_ref, nw_ref, w1_ref, w2_ref, o_ref, a_ref, *, d_ff, tf):
    mods = [mod_ref[0, k:k + 1, :] for k in range(3)]
    for r0 in range(0, x_ref.shape[1], ROW_SUB):
        rs = slice(r0, r0 + ROW_SUB)
        o_ref[0, rs, :] = _ffn_rows(x_ref[0, rs, :], mods, nw_ref, w1_ref, w2_ref, a_ref, rs, d_ff, tf)


def _ffn(s, mods, mod_row, nw, w1, w2):
    bsz, t, d = s.shape
    d_ff = w2.shape[0]
    tm = min(TOKEN_TILE, t)
    const = lambda b, i: (0, 0)
    return pl.pallas_call(
        functools.partial(_ffn_kernel, d_ff=d_ff, tf=FF_TILE),
        grid=(bsz, t // tm),
        in_specs=[pl.BlockSpec((1, tm, d), lambda b, i: (b, i, 0)),
                  pl.BlockSpec((1, 3, d), lambda b, i: (mod_row(b), 0, 0)),
                  pl.BlockSpec((1, d), const),
                  pl.BlockSpec((d, 2 * d_ff), const, pipeline_mode=pl.Buffered(1)),
                  pl.BlockSpec((d_ff, d), const, pipeline_mode=pl.Buffered(1))],
        out_specs=pl.BlockSpec((1, tm, d), lambda b, i: (b, i, 0)),
        out_shape=jax.ShapeDtypeStruct((bsz, t, d), F32),
        scratch_shapes=[pltpu.VMEM((tm, d_ff), BF16)],
        compiler_params=_cparams(("parallel", "parallel")),
        name="ffn",
    )(s, mods, nw, w1, w2)


def _inproj_kernel(x_ref, mod_ref, nw_ref, w_ref, wg_ref, gb_ref, lbl_ref,
                   hq_ref, hv_ref, hg_ref, hlf_ref, mqk_ref, mv_ref, mo_ref, gc_ref, *, hw):
    shift = mod_ref[0, 0:1, :]
    scale = mod_ref[0, 1:2, :]
    lbl = lbl_ref[...]
    e = jnp.exp(lbl - jnp.max(lbl, axis=0, keepdims=True))
    lb = e[0] / jnp.sum(e, axis=0)
    first_half = lax.broadcasted_iota(jnp.int32, (8, LANE), 1) < CHUNK
    sub = min(ROW_SUB, x_ref.shape[1])
    for r0 in range(0, x_ref.shape[1], sub):
        rs = slice(r0, r0 + sub)
        h = (_rms(x_ref[0, rs, :]) * nw_ref[...] * (1.0 + scale) + shift).astype(BF16)

        def proj(k):
            return _dot(h, w_ref[:, k * hw:(k + 1) * hw])

        hq_ref[0, rs, :] = _silu(proj(0)) * (HG_DIM ** -0.5)
        hv_ref[0, rs, :] = proj(1).astype(BF16)
        hg_ref[0, rs, :] = proj(2).astype(BF16)
        for d in range(2):
            lbd = lb[d:d + 1, :]
            hlf_ref[0, d, rs, :] = jnp.log(lbd + (1.0 - lbd) * _sigmoid(proj(3 + d))) * LOG2E
        mqk_ref[0, rs, :] = proj(5).astype(BF16)
        mv_ref[0, rs, :] = proj(6).astype(BF16)
        mo_ref[0, rs, :] = proj(7).astype(BF16)
        gt = _dot_nt(wg_ref[...], h) + gb_ref[...]
        for m in range(sub // LANE):
            top = gt[0:8, m * LANE:(m + 1) * LANE]
            bot = gt[8:16, m * LANE:(m + 1) * LANE]
            c0 = r0 // CHUNK + 2 * m
            gc_ref[0, c0] = jnp.where(first_half, top, pltpu.roll(bot, CHUNK, 1))
            gc_ref[0, c0 + 1] = jnp.where(first_half, pltpu.roll(top, CHUNK, 1), bot)


def _inproj(s, mods, mod_row, nw, w_in, w_gate_t, gate_b, lb_logits):
    bsz, t, d = s.shape
    hw = HG_HEADS * HG_DIM
    ng = gate_b.shape[0]
    tm = min(TOKEN_TILE, t)
    tok4 = lambda b, i: (b, i, 0, 0)
    const = lambda b, i: (0, 0)
    tok = lambda b, i: (b, i, 0)
    tok2 = lambda b, i: (b, 0, i, 0)
    f = lambda dt, *shape: jax.ShapeDtypeStruct(shape, dt)
    return pl.pallas_call(
        functools.partial(_inproj_kernel, hw=hw),
        grid=(bsz, t // tm),
        in_specs=[pl.BlockSpec((1, tm, d), tok),
                  pl.BlockSpec((1, 2, d), lambda b, i: (mod_row(b), 0, 0)),
                  pl.BlockSpec((1, d), const),
                  pl.BlockSpec(w_in.shape, const, pipeline_mode=pl.Buffered(1)),
                  pl.BlockSpec(w_gate_t.shape, const),
                  pl.BlockSpec((ng, 1), const),
                  pl.BlockSpec(lb_logits.shape, lambda b, i: (0, 0, 0))],
        out_specs=[pl.BlockSpec((1, tm, hw), tok), pl.BlockSpec((1, tm, hw), tok),
                   pl.BlockSpec((1, tm, hw), tok),
                   pl.BlockSpec((1, 2, tm, hw), tok2),
                   pl.BlockSpec((1, tm, hw), tok), pl.BlockSpec((1, tm, hw), tok),
                   pl.BlockSpec((1, tm, hw), tok), pl.BlockSpec((1, tm // CHUNK, ng // 2, LANE), tok4)],
        out_shape=[f(F32, bsz, t, hw), f(BF16, bsz, t, hw), f(BF16, bsz, t, hw),
                   f(F32, bsz, 2, t, hw),
                   f(BF16, bsz, t, hw), f(BF16, bsz, t, hw), f(BF16, bsz, t, hw),
                   f(F32, bsz, t // CHUNK, ng // 2, LANE)],
        compiler_params=_cparams(("parallel", "parallel")),
        name="inproj",
    )(s, mods, nw, w_in, w_gate_t, gate_b, lb_logits)


def _conv_kernel(x_ref, w_ref, b_ref, o_ref, *, width, q_tiles):
    x = x_ref[0].astype(F32)
    t = x.shape[0]
    col = lax.broadcasted_iota(jnp.int32, x.shape, 0) % width
    xl = jnp.where(col == 0, 0.0, pltpu.roll(x, 1, 0))
    xr = jnp.where(col == width - 1, 0.0, pltpu.roll(x, t - 1, 0))

    def row(di):
        return (w_ref[di, 0:1, :] * xl + w_ref[di, 1:2, :] * x + w_ref[di, 2:3, :] * xr)

    y = row(1)
    if t > width:
        pad = jnp.zeros((width, x.shape[1]), F32)
        y = y + jnp.concatenate([pad, row(0)[:t - width]], axis=0)
        y = y + jnp.concatenate([row(2)[width:], pad], axis=0)
    y = _silu(y + b_ref[...])
    qscale = jnp.where(pl.program_id(1) < q_tiles, ML_QK ** -0.5, 1.0)
    o_ref[0] = (y * qscale).astype(BF16)


def _conv(x, w, b, width):
    bsz, t, ch = x.shape
    return pl.pallas_call(
        functools.partial(_conv_kernel, width=width, q_tiles=ML_HEADS * ML_QK // LANE),
        grid=(bsz, ch // LANE),
        in_specs=[pl.BlockSpec((1, t, LANE), lambda b_, c: (b_, 0, c)),
                  pl.BlockSpec((3, 3, LANE), lambda b_, c: (0, 0, c)),
                  pl.BlockSpec((1, LANE), lambda b_, c: (0, c))],
        out_specs=pl.BlockSpec((1, t, LANE), lambda b_, c: (b_, 0, c)),
        out_shape=jax.ShapeDtypeStruct((bsz, t, ch), BF16),
        compiler_params=_cparams(("parallel", "parallel")),
        name="conv",
    )(x, w, b)


def _positions():
    t = np.arange(CHUNK)
    return [t, CHUNK - 1 - t]


def _hgrn2_masks():
    ds, vs, ws = [], [], []
    for p in _positions():
        le = (p[None, :] <= p[:, None]).astype(np.float32)
        order = [p[SUB * j] // SUB for j in range(N_SUB)]
        ref = np.zeros((32, CHUNK), np.float32)
        valid = np.zeros((N_SUB * N_SUB, 1), np.float32)
        for j in range(N_SUB):
            ref[j] = p <= SUB * order[j] + SUB // 2 - 1
            ref[8 + j] = p <= SUB * order[j] + SUB - 1
        ref[N_SUB] = 1.0
        for j in range(N_SUB):
            for i in range(N_SUB):
                if order[i] < order[j]:
                    ref[16 + N_SUB * j + i] = ref[j] - ref[8 + i]
                    valid[N_SUB * j + i] = 1.0
        ds.append(np.concatenate([le, ref], axis=0))
        vs.append(np.broadcast_to(valid, (N_SUB * N_SUB, HG_HEADS * HG_DIM)))
        tok_blk = np.arange(CHUNK)[:, None] // SUB
        ws.append(np.concatenate([(tok_blk == j) * le for j in range(N_SUB)], axis=1))
    return (np.stack(ds).astype(np.float32), np.stack(vs).astype(np.float32),
            np.stack(ws).astype(np.float32))


def _mlstm_masks():
    le2s, let2s = [], []
    for p in _positions():
        le = (p[None, :] <= p[:, None]).astype(np.float32)
        le2s.append(np.concatenate([le, le], axis=1))
        let2 = np.zeros((2 * CHUNK, 2 * CHUNK), np.float32)
        let2[:CHUNK, :CHUNK] = le.T
        let2[CHUNK:, CHUNK:] = le.T
        let2s.append(let2)
    return np.stack(le2s), np.stack(let2s)


def _run_pipelined(n_groups, group, stages_of, step_of):
    queue = []
    for g in range(n_groups):
        for _ in stages_of(g):
            if queue:
                queue.pop(0)()
        while queue:
            queue.pop(0)()
        queue = [functools.partial(step_of, g * group + i) for i in range(group)]
    for step in queue:
        step()


def _chunk_rows(c, nc, direction):
    cc = jnp.where(direction == 0, c, nc - 1 - c)
    return pl.multiple_of(cc * CHUNK, CHUNK)


def _hgrn2_kernel(q_ref, v_ref, lf_ref, d_ref, pv_ref, w_ref, s0_ref, o_ref, st_ref,
                  u_ref, qin_ref, g_ref, oi_ref, bx_ref, kx_ref, vx_ref, *, nc):
    direction = pl.program_id(1)

    @pl.when(pl.program_id(2) == 0)
    def _():
        st_ref[...] = s0_ref[...]

    dmat = d_ref[0].astype(BF16)
    pair_valid = pv_ref[0]
    wmask = w_ref[0] > 0.0
    L = CHUNK
    hw = q_ref.shape[-1]
    upper_rows = lax.broadcasted_iota(jnp.int32, (L, 2 * L), 0) < L // 2

    group = min(GROUP, nc)

    def intra(grp):
        cs = [grp * group + i for i in range(group)]
        rows = [pl.ds(_chunk_rows(c, nc, direction), L) for c in cs]
        items = [(i, h) for i in range(group) for h in range(HG_HEADS)]
        lanes = [slice(h * HG_DIM, (h + 1) * HG_DIM) for h in range(HG_HEADS)]
        lfs = [lf_ref[0, 0, r, :] for r in rows]
        parts = [_split2(lf) for lf in lfs]
        e_all = sum(_dot(dmat, jnp.concatenate([parts[i][n] for i in range(group)], axis=1))
                    for n in range(2))
        yield
        q_mids, k_mids, k_outs, vbs, spans = [], [], [], [], []
        for i in range(group):
            e = e_all[:, i * hw:(i + 1) * hw]
            q = q_ref[0, rows[i], :]
            k = 1.0 - jnp.exp2(lfs[i])
            b = e[0:L]
            b_tot = e[L + N_SUB:L + N_SUB + 1]
            sub_rows = lambda r0: jnp.concatenate(
                [jnp.broadcast_to(e[r0 + j:r0 + j + 1], (SUB, hw)) for j in range(N_SUB)], axis=0)
            mid_rows = sub_rows(L)
            end_rows = sub_rows(L + 8)
            to_mid = b - mid_rows
            spans.append(jnp.max(jnp.abs(to_mid)))
            q_mids.append((q * jnp.exp2(jnp.minimum(to_mid, EXP2_CLAMP))).astype(BF16))
            k_diag = k * jnp.exp2(jnp.minimum(-to_mid, EXP2_CLAMP))
            k_end = k * jnp.exp2(end_rows - b)
            cross = jnp.exp2(e[L + 16:L + 32]) * pair_valid
            slabs = []
            for j in range(N_SUB):
                for ib in range(N_SUB):
                    rs = slice(ib * SUB, (ib + 1) * SUB)
                    row = N_SUB * j + ib
                    slabs.append(k_diag[rs] if ib == j else k_end[rs] * cross[row:row + 1])
            k_mids.append(jnp.concatenate(slabs, axis=0).astype(BF16))
            k_outs.append((k * jnp.exp2(b_tot - b)).astype(BF16))
            vbs.append(v_ref[0, rows[i], :])
            qin_ref[cs[i]] = (q * jnp.exp2(b)).astype(BF16)
            g_ref[cs[i]] = jnp.broadcast_to(jnp.exp2(b_tot), (8, hw))
            yield
        scores = [_dot_nt(q_mids[i][:, lanes[h]], k_mids[i][:, lanes[h]]) for i, h in items]
        yield
        a_s = [jnp.where(wmask, s, 0.0) for s in scores]
        a_s = [jnp.where(upper_rows, a[:, :2 * L], a[:, 2 * L:]).astype(BF16) for a in a_s]
        v2s = [jnp.concatenate([vb, vb], axis=0) for vb in vbs]
        outs = [_dot(a_s[n], v2s[i][:, lanes[h]]) for n, (i, h) in enumerate(items)]
        yield
        for i in range(group):
            oi_ref[cs[i]] = jnp.concatenate(outs[i * HG_HEADS:(i + 1) * HG_HEADS], axis=1)
        yield
        for i, h in items:
            u_ref[cs[i], h] = _dot_tn(vbs[i][:, lanes[h]], k_outs[i][:, lanes[h]])

        @pl.when(functools.reduce(jnp.maximum, spans) > EXP2_CLAMP)
        def _():
            for i in range(group):
                exact_intra(cs[i], rows[i])
        yield

    def exact_intra(c, rows):
        lf = lf_ref[0, 0, rows, :]
        bx_ref[...] = sum(_dot(dmat[0:L], x) for x in _split3(lf))
        kx_ref[...] = 1.0 - jnp.exp2(lf)
        vx_ref[...] = v_ref[0, rows, :].astype(F32)
        q = q_ref[0, rows, :]
        tok = lax.broadcasted_iota(jnp.int32, (L, 1), 0)
        pos = jnp.where(direction == 0, tok, L - 1 - tok)

        def add_source(s, acc):
            b_s = bx_ref[pl.ds(s, 1), :]
            z = q * kx_ref[pl.ds(s, 1), :] * jnp.exp2(jnp.minimum(bx_ref[...] - b_s, 0.0))
            v_s = vx_ref[pl.ds(s, 1), :]
            seen = pos >= jnp.where(direction == 0, s, L - 1 - s)
            cols = [jnp.where(seen, jnp.sum(z[:, ln], axis=1, keepdims=True), 0.0) * v_s[:, ln]
                    for ln in (slice(h * HG_DIM, (h + 1) * HG_DIM) for h in range(HG_HEADS))]
            return acc + jnp.concatenate(cols, axis=1)

        oi_ref[c] = lax.fori_loop(0, L, add_source, jnp.zeros((L, hw), F32))

    def carry_state(c):
        rows = pl.ds(_chunk_rows(c, nc, direction), L)
        q_in = qin_ref[c]
        g = g_ref[c][0:1]
        outs, states = [], []
        for h in range(HG_HEADS):
            ln = slice(h * HG_DIM, (h + 1) * HG_DIM)
            st = st_ref[0, 0, h]
            outs.append(_dot_nt(q_in[:, ln], st.astype(BF16)))
            states.append(g[:, ln] * st + u_ref[c, h])
        o_ref[0, 0, rows, :] = (oi_ref[c] + jnp.concatenate(outs, axis=1)).astype(BF16)
        for h in range(HG_HEADS):
            st_ref[0, 0, h] = states[h]

    _run_pipelined(nc // group, group, intra, carry_state)


def _hgrn2(q, v, lf, s0, dmat, pair_valid, wmask):
    bsz, t, hw = q.shape
    nc = min(HG_CHUNKS, t // CHUNK)
    tb = nc * CHUNK
    nblk = t // tb

    def blk(j, d):
        return jnp.where(d == 0, j, nblk - 1 - j)

    st_spec = pl.BlockSpec((1, 1) + s0.shape[2:], lambda b, d, j: (b, d, 0, 0, 0))
    return pl.pallas_call(
        functools.partial(_hgrn2_kernel, nc=nc),
        grid=(bsz, 2, nblk),
        in_specs=[pl.BlockSpec((1, tb, hw), lambda b, d, j: (b, blk(j, d), 0)),
                  pl.BlockSpec((1, tb, hw), lambda b, d, j: (b, blk(j, d), 0)),
                  pl.BlockSpec((1, 1, tb, hw), lambda b, d, j: (b, d, blk(j, d), 0)),
                  pl.BlockSpec((1,) + dmat.shape[1:], lambda b, d, j: (d, 0, 0)),
                  pl.BlockSpec((1,) + pair_valid.shape[1:], lambda b, d, j: (d, 0, 0)),
                  pl.BlockSpec((1,) + wmask.shape[1:], lambda b, d, j: (d, 0, 0)),
                  st_spec],
        out_specs=[pl.BlockSpec((1, 1, tb, hw), lambda b, d, j: (b, d, blk(j, d), 0)), st_spec],
        out_shape=[jax.ShapeDtypeStruct((bsz, 2, t, hw), BF16),
                   jax.ShapeDtypeStruct(s0.shape, F32)],
        scratch_shapes=[pltpu.VMEM((nc,) + s0.shape[2:], F32),
                        pltpu.VMEM((nc, CHUNK, hw), BF16),
                        pltpu.VMEM((nc, 8, hw), F32),
                        pltpu.VMEM((nc, CHUNK, hw), F32),
                        pltpu.VMEM((CHUNK, hw), F32),
                        pltpu.VMEM((CHUNK, hw), F32),
                        pltpu.VMEM((CHUNK, hw), F32)],
        compiler_params=_cparams(("parallel", "parallel", "arbitrary")),
        name="hgrn2_scan",
    )(q, v, lf, dmat, pair_valid, wmask, s0)


def _mlstm_kernel(qk_ref, v_ref, g_ref, le2_ref, let2_ref, c0_ref, m0_ref,
                  o_ref, c_ref, mm_ref,
                  p_ref, u_ref, qs_ref, ct_ref, mloc_ref, sc_ref, *, nc):
    direction = pl.program_id(1)

    @pl.when(pl.program_id(2) == 0)
    def _():
        c_ref[...] = c0_ref[...]
        mm_ref[...] = m0_ref[...]

    le2 = le2_ref[0].astype(BF16)
    let2 = let2_ref[0].astype(BF16)
    causal = le2_ref[0] > 0.0
    L = CHUNK
    neg = -jnp.inf
    lane = lax.broadcasted_iota(jnp.int32, (1, LANE), 1)
    seg = [lane < L, lane >= L]
    lane_t = lax.broadcasted_iota(jnp.int32, (L, LANE), 1) < L
    eye = (lax.broadcasted_iota(jnp.int32, (LANE, LANE), 0)
           == lax.broadcasted_iota(jnp.int32, (LANE, LANE), 1))
    npair = ML_HEADS // 2
    ones_v = jnp.ones((L, ML_V), BF16)

    group = min(ML_GROUP, nc)

    def intra(grp):
        cs = [grp * group + i for i in range(group)]
        ccs = [jnp.where(direction == 0, c, nc - 1 - c) for c in cs]
        rows = [pl.ds(pl.multiple_of(cc * L, L), L) for cc in ccs]
        pairs = [(i, p) for i in range(group) for p in range(npair)]
        gs = [jnp.where(direction == 0, g_ref[0, cc][0:4], g_ref[0, cc][4:8]) for cc in ccs]
        pad = [jnp.zeros((-2 * group % 8, LANE), F32)] if 2 * group % 8 else []
        i2 = jnp.concatenate([g[0:2] for g in gs] + pad, axis=0)
        lf2 = _log_sigmoid(jnp.concatenate([g[2:4] for g in gs] + pad, axis=0))
        parts = _split3(lf2)
        r2 = i2 - sum(_dot(x, let2) for x in parts)
        g_a = jnp.max(jnp.where(seg[0], r2, neg), axis=1, keepdims=True)
        g_b = jnp.max(jnp.where(seg[1], r2, neg), axis=1, keepdims=True)
        e_w = jnp.exp(r2 - jnp.where(seg[0], g_a, g_b))
        bt_a = jnp.sum(jnp.where(seg[0], lf2, 0.0), axis=1, keepdims=True)
        bt_b = jnp.sum(jnp.where(seg[1], lf2, 0.0), axis=1, keepdims=True)
        yield

        def head_rows(x, i):
            zero = jnp.zeros_like(x[0:1])
            return jnp.concatenate(
                [jnp.broadcast_to(jnp.where(seg[h % 2], x[2 * i + h // 2:2 * i + h // 2 + 1], zero),
                                  (LANE, LANE)) for h in range(ML_HEADS)], axis=0)

        c_ts = [sum(_dot_nt(le2, head_rows(x, i)) for x in parts) for i in range(group)]
        for i in range(group):
            ct_ref[cs[i]] = c_ts[i]
            sc_ref[cs[i]] = jnp.concatenate(
                [jnp.broadcast_to(x[2 * i + p:2 * i + p + 1], (1, LANE))
                 for x2 in ((bt_a, bt_b), (g_a, g_b)) for p in range(npair) for x in x2], axis=0)
        yield
        qks = [qk_ref[0, r, :] for r in rows]
        vbs = [v_ref[0, r, :] for r in rows]
        q_ps = [qks[i][:, p * LANE:(p + 1) * LANE] for i, p in pairs]
        k_ps = [qks[i][:, (npair + p) * LANE:(npair + p + 1) * LANE] for i, p in pairs]
        zb = jnp.zeros((L, LANE), BF16)
        kbs = [jnp.concatenate([jnp.where(lane_t, k, zb), jnp.where(lane_t, zb, k)], axis=0) for k in k_ps]
        scores = [_dot_nt(q, kb) for q, kb in zip(q_ps, kbs)]
        yield
        ws = []
        for n, (i, p) in enumerate(pairs):
            ha, hb = 2 * p, 2 * p + 1
            t1 = jnp.where(lane_t, c_ts[i][:, ha * ML_V:(ha + 1) * ML_V], c_ts[i][:, hb * ML_V:(hb + 1) * ML_V])
            d_log = jnp.where(causal, t1 + r2[2 * i + p:2 * i + p + 1], neg)
            m_a = jnp.max(jnp.where(lane_t, d_log, neg), axis=1, keepdims=True)
            m_b = jnp.max(jnp.where(lane_t, neg, d_log), axis=1, keepdims=True)
            mloc_ref[cs[i], :, ha * ML_V:(ha + 1) * ML_V] = jnp.broadcast_to(m_a, (L, ML_V))
            mloc_ref[cs[i], :, hb * ML_V:(hb + 1) * ML_V] = jnp.broadcast_to(m_b, (L, ML_V))
            ws.append(jnp.exp(d_log - jnp.where(lane_t, m_a, m_b)))
        yield
        a_s = [(s_ * w).astype(BF16) for s_, w in zip(scores, ws)]
        a_stacks = [jnp.concatenate([jnp.where(lane_t, a, zb), jnp.where(lane_t, zb, a)], axis=0) for a in a_s]
        v_stacks = [jnp.concatenate(
            [jnp.concatenate([vbs[i][:, 2 * p * ML_V:(2 * p + 1) * ML_V], ones_v], axis=1),
             jnp.concatenate([vbs[i][:, (2 * p + 1) * ML_V:(2 * p + 2) * ML_V], ones_v], axis=1)], axis=0)
            for i, p in pairs]
        for n, (i, p) in enumerate(pairs):
            p_ref[cs[i], p] = _dot(a_stacks[n], v_stacks[n])
            qs_ref[cs[i], p] = jnp.concatenate(
                [jnp.where(lane_t, q_ps[n], zb), jnp.where(lane_t, zb, q_ps[n])], axis=0)
        yield
        diags = [jnp.where(eye, jnp.broadcast_to(e_w[2 * i + p:2 * i + p + 1], (LANE, LANE)), 0.0).astype(BF16)
                 for i, p in pairs]
        kb_ws = [_dot(d, kb).astype(BF16) for d, kb in zip(diags, kbs)]
        for n, (i, p) in enumerate(pairs):
            u_ref[cs[i], p] = _dot_tn(kb_ws[n], v_stacks[n])
        yield

    def carry_state(c):
        cc = jnp.where(direction == 0, c, nc - 1 - c)
        rows = pl.ds(pl.multiple_of(cc * L, L), L)
        m_all = mm_ref[0, 0]
        sc = sc_ref[c]
        c_t = ct_ref[c]
        m_loc = mloc_ref[c]
        outs, states, m_rows = [], [], []
        for p in range(ML_HEADS // 2):
            c_pair = c_ref[0, 0, p]
            inter = _dot(qs_ref[c, p], c_pair.astype(BF16))
            intra_p = p_ref[c, p]
            inc = u_ref[c, p]
            new_rows = []
            for hh in range(2):
                h = 2 * p + hh
                rs = slice(hh * L, (hh + 1) * L)
                ln = slice(h * ML_V, (h + 1) * ML_V)
                m_in = m_all[h:h + 1]
                m_t = jnp.maximum(m_loc[:, ln], c_t[:, ln] + m_in)
                alpha = jnp.exp(m_loc[:, ln] - m_t)
                beta = jnp.exp(c_t[:, ln] + m_in - m_t)
                num = alpha * intra_p[rs, :ML_V] + beta * inter[rs, :ML_V]
                den = alpha * intra_p[rs, ML_V:] + beta * inter[rs, ML_V:]
                outs.append(num / jnp.maximum(jnp.abs(den), jnp.exp(-m_t)))
                b_tot = sc[h:h + 1]
                m_new = b_tot + jnp.maximum(m_in, sc[ML_HEADS + h:ML_HEADS + h + 1])
                carry_w = jnp.exp(b_tot + m_in - m_new)
                inc_w = jnp.exp(b_tot + sc[ML_HEADS + h:ML_HEADS + h + 1] - m_new)
                cw2 = jnp.concatenate([carry_w, carry_w], axis=1)
                iw2 = jnp.concatenate([inc_w, inc_w], axis=1)
                new_rows.append(cw2 * c_pair[rs] + iw2 * inc[rs])
                m_rows.append(m_new)
            states.append(jnp.concatenate(new_rows, axis=0))
        o_ref[0, 0, rows, :] = jnp.concatenate(outs, axis=1).astype(BF16)
        for p in range(ML_HEADS // 2):
            c_ref[0, 0, p] = states[p]
        mm_ref[0, 0] = jnp.concatenate(m_rows + [m_all[ML_HEADS:]], axis=0)

    _run_pipelined(nc // group, group, intra, carry_state)


def _mlstm(qk, v, gates, c0, m0, le2, let2):
    bsz, t, hw = v.shape
    nc = min(ML_CHUNKS, t // CHUNK)
    tb = nc * CHUNK
    nblk = t // tb
    npair = ML_HEADS // 2

    def blk(j, d):
        return jnp.where(d == 0, j, nblk - 1 - j)

    c_spec = pl.BlockSpec((1, 1) + c0.shape[2:], lambda b, d, j: (b, d, 0, 0, 0))
    m_spec = pl.BlockSpec((1, 1) + m0.shape[2:], lambda b, d, j: (b, d, 0, 0))
    return pl.pallas_call(
        functools.partial(_mlstm_kernel, nc=nc),
        grid=(bsz, 2, nblk),
        in_specs=[pl.BlockSpec((1, tb, qk.shape[-1]), lambda b, d, j: (b, blk(j, d), 0)),
                  pl.BlockSpec((1, tb, hw), lambda b, d, j: (b, blk(j, d), 0)),
                  pl.BlockSpec((1, nc) + gates.shape[2:], lambda b, d, j: (b, blk(j, d), 0, 0)),
                  pl.BlockSpec((1,) + le2.shape[1:], lambda b, d, j: (d, 0, 0)),
                  pl.BlockSpec((1,) + let2.shape[1:], lambda b, d, j: (d, 0, 0)),
                  c_spec, m_spec],
        out_specs=[pl.BlockSpec((1, 1, tb, hw), lambda b, d, j: (b, d, blk(j, d), 0)), c_spec, m_spec],
        out_shape=[jax.ShapeDtypeStruct((bsz, 2, t, hw), BF16),
                   jax.ShapeDtypeStruct(c0.shape, F32),
                   jax.ShapeDtypeStruct(m0.shape, F32)],
        scratch_shapes=[pltpu.VMEM((nc, npair, 2 * CHUNK, 2 * ML_V), F32),
                        pltpu.VMEM((nc, npair, 2 * ML_QK, 2 * ML_V), F32),
                        pltpu.VMEM((nc, npair, 2 * CHUNK, LANE), BF16),
                        pltpu.VMEM((nc, CHUNK, hw), F32),
                        pltpu.VMEM((nc, CHUNK, hw), F32),
                        pltpu.VMEM((nc, 8, LANE), F32)],
        compiler_params=_cparams(("parallel", "parallel", "arbitrary")),
        name="mlstm_scan",
    )(qk, v, gates, le2, let2, c0, m0)


def _head_rms(x, heads):
    dh = x.shape[-1] // heads
    return jnp.concatenate([_rms(x[:, h * dh:(h + 1) * dh]) for h in range(heads)], axis=1)


def _mix_ffn_kernel(oh_ref, om_ref, hg_ref, mo_ref, x_ref, mod_ref, hnw_ref, mnw_ref, wo_ref,
                    nw_ref, w1_ref, w2_ref, fw_ref, o_ref, a_ref, *, d_ff, tf):
    mix_gate = mod_ref[0, 0:1, :]
    mods = [mod_ref[0, k:k + 1, :] for k in range(1, 4)]
    for r0 in range(0, x_ref.shape[1], ROW_SUB):
        rs = slice(r0, r0 + ROW_SUB)
        oh = oh_ref[0, 0, rs, :].astype(F32) + oh_ref[0, 1, rs, :].astype(F32)
        om = om_ref[0, 0, rs, :].astype(F32) + om_ref[0, 1, rs, :].astype(F32)
        hg_out = _head_rms(oh, HG_HEADS) * hnw_ref[...] * _silu(hg_ref[0, rs, :].astype(F32))
        ml_out = _sigmoid(mo_ref[0, rs, :].astype(F32)) * (_head_rms(om, ML_HEADS) * mnw_ref[...])
        merged = jnp.concatenate([hg_out, ml_out], axis=1).astype(BF16)
        x = x_ref[0, rs, :] + mix_gate * _dot(merged, wo_ref[...])
        y = _ffn_rows(x, mods, nw_ref, w1_ref, w2_ref, a_ref, rs, d_ff, tf)
        o_ref[0, rs, :] = _rms(y) * fw_ref[...]


def _mix_ffn(oh, om, hg, mo, x, mods, hnw, mnw, w_out, nw, w1, w2, fw):
    bsz, t, d = x.shape
    hw = hg.shape[-1]
    d_ff = w2.shape[0]
    tm = min(TOKEN_TILE, t)
    tok = lambda b, i: (b, i, 0)
    tok2 = lambda b, i: (b, 0, i, 0)
    const = lambda b, i: (0, 0)
    resident = lambda a: pl.BlockSpec(a.shape, const, pipeline_mode=pl.Buffered(1))
    return pl.pallas_call(
        functools.partial(_mix_ffn_kernel, d_ff=d_ff, tf=FF_TILE),
        grid=(bsz, t // tm),
        in_specs=[pl.BlockSpec((1, 2, tm, hw), tok2), pl.BlockSpec((1, 2, tm, hw), tok2),
                  pl.BlockSpec((1, tm, hw), tok), pl.BlockSpec((1, tm, hw), tok),
                  pl.BlockSpec((1, tm, d), tok),
                  pl.BlockSpec((1, 4, d), lambda b, i: (b, 0, 0)),
                  pl.BlockSpec((1, hw), const), pl.BlockSpec((1, hw), const), resident(w_out),
                  pl.BlockSpec((1, d), const), resident(w1), resident(w2), pl.BlockSpec((1, d), const)],
        out_specs=pl.BlockSpec((1, tm, d), tok),
        out_shape=jax.ShapeDtypeStruct((bsz, t, d), F32),
        scratch_shapes=[pltpu.VMEM((tm, d_ff), BF16)],
        compiler_params=_cparams(("parallel", "parallel")),
        name="mix_ffn_final",
    )(oh, om, hg, mo, x, mods, hnw, mnw, w_out, nw, w1, w2, fw)


def kernel(x, c, ctx, c_ctx, w_mod, b_mod, norm1_w, ffn1_w1, ffn1_w2, norm2_w, w_in, ml_gate_b,
           ml_conv_w, ml_conv_b, hg_lb_logits, hg_norm_w, ml_norm_w, w_out, norm3_w, ffn2_w1, ffn2_w2,
           final_norm_w):
    bsz, seq, d = x.shape
    n_ctx = ctx.shape[1]
    assert w_mod.shape[0] == 1, "single-layer kernel"
    assert seq % (GRID_W * 8) == 0 and n_ctx % CHUNK == 0
    hw = HG_HEADS * HG_DIM
    ng = ml_gate_b.shape[-1]

    rows = -(-(bsz + 1) // 8) * 8
    cvec = jnp.zeros((rows, d), F32).at[:bsz].set(c).at[bsz].set(c_ctx)
    mods = _modulation(cvec, w_mod[0], b_mod[0][None, :]).reshape(rows, N_MOD, d)
    lat_row = lambda b: b
    ctx_row = lambda b: bsz

    row = lambda a: a.reshape(1, -1)
    w1a, w2a = ffn1_w1[0], ffn1_w2[0]
    x1 = _ffn(x, mods[:, 0:3], lat_row, row(norm1_w[0]), w1a, w2a)
    s1 = _ffn(ctx, mods[:, 0:3], ctx_row, row(norm1_w[0]), w1a, w2a)

    w_in_b = w_in[0][:, :8 * hw].astype(BF16)
    gate_order = np.arange(ng).reshape(2, 2, ML_HEADS // 2, 2).transpose(3, 0, 1, 2).reshape(-1)
    w_gate_t = w_in[0][:, 8 * hw:].T[gate_order].astype(BF16)
    gate_b = ml_gate_b[0][gate_order].reshape(ng, 1)
    dmat, pair_valid, wmask = (jnp.asarray(a) for a in _hgrn2_masks())
    le2, let2 = (jnp.asarray(a) for a in _mlstm_masks())
    conv_w, conv_b = ml_conv_w[0], row(ml_conv_b[0])
    npair = ML_HEADS // 2

    def mixer_scans(s, mod_row, width, s0, c0, m0):
        hq, hv, hg, hlf, mqk, mv, mo, gates = _inproj(
            s, mods[:, 3:5], mod_row, row(norm2_w[0]), w_in_b, w_gate_t, gate_b, hg_lb_logits)
        qk = _conv(mqk, conv_w, conv_b, width)
        oh, s_fin = _hgrn2(hq, hv, hlf, s0, dmat, pair_valid, wmask)
        om, c_fin, m_fin = _mlstm(qk, mv, gates, c0, m0, le2, let2)
        return oh, om, hg, mo, s_fin, c_fin, m_fin

    s0 = jnp.zeros((bsz, 2, HG_HEADS, HG_DIM, HG_DIM), F32)
    c0 = jnp.zeros((bsz, 2, npair, 2 * ML_QK, 2 * ML_V), F32)
    m0 = jnp.zeros((bsz, 2, 8, LANE), F32)
    _, _, _, _, s_ctx, c_ctx_state, m_ctx = mixer_scans(s1, ctx_row, n_ctx, s0, c0, m0)
    oh, om, hg, mo, _, _, _ = mixer_scans(x1, lat_row, GRID_W, s_ctx, c_ctx_state, m_ctx)

    return _mix_ffn(oh, om, hg, mo, x1, mods[:, 5:9], row(hg_norm_w[0]), row(ml_norm_w[0]),
                    w_out[0].astype(BF16), row(norm3_w[0]), ffn2_w1[0], ffn2_w2[0], row(final_norm_w))
```

```python
import functools

import numpy as np
import jax
import jax.numpy as jnp
from jax import lax
from jax.experimental import pallas as pl
from jax.experimental.pallas import tpu as pltpu

F32 = jnp.float32
BF16 = jnp.bfloat16

EPS = 1e-6
CHUNK = 64
GRID_W = 64
N_MOD = 9
HG_HEADS = 4
HG_DIM = 128
ML_HEADS = 4
ML_QK = 64
ML_V = 128
SUB = 16
N_SUB = CHUNK // SUB
GROUP = 4
ML_GROUP = 4
HG_CHUNKS = 8
ML_CHUNKS = 16
TOKEN_TILE = 512
ROW_SUB = 256
FF_TILE = 256
PIECE = 128
LOG2E = 1.4426950408889634
EXP2_CLAMP = 115.0
LANE = 128
VMEM_LIMIT = 56 * 1024 * 1024


def _cparams(sem):
    return pltpu.CompilerParams(dimension_semantics=sem, vmem_limit_bytes=VMEM_LIMIT)


def _dot(a, b):
    return jnp.dot(a, b, preferred_element_type=F32)


def _dot_nt(a, b):
    return lax.dot_general(a, b, (((1,), (1,)), ((), ())), preferred_element_type=F32)


def _dot_tn(a, b):
    return lax.dot_general(a, b, (((0,), (0,)), ((), ())), preferred_element_type=F32)


def _sigmoid(x):
    return 1.0 / (1.0 + jnp.exp(-x))


def _silu(x):
    return x * _sigmoid(x)


def _log_sigmoid(x):
    return jnp.minimum(x, 0.0) - jnp.log(1.0 + jnp.exp(-jnp.abs(x)))


def _split3(x):
    hi = x.astype(BF16)
    r = x - hi.astype(F32)
    mid = r.astype(BF16)
    lo = (r - mid.astype(F32)).astype(BF16)
    return hi, mid, lo


def _split2(x):
    hi = x.astype(BF16)
    return hi, (x - hi.astype(F32)).astype(BF16)


def _rms(x):
    return x * lax.rsqrt(jnp.mean(x * x, axis=-1, keepdims=True) + EPS)


def _mod_kernel(c_ref, w_ref, b_ref, o_ref):
    a = _silu(c_ref[...])
    a_hi = a.astype(BF16)
    a_lo = (a - a_hi.astype(F32)).astype(BF16)
    w = w_ref[...]
    w_hi = w.astype(BF16)
    w_lo = (w - w_hi.astype(F32)).astype(BF16)
    o_ref[...] = _dot(a_hi, w_hi) + _dot(a_hi, w_lo) + _dot(a_lo, w_hi) + b_ref[...]


def _modulation(cvec, w_mod, b_mod):
    rows, d = cvec.shape
    n = w_mod.shape[1]
    tn = 1024
    return pl.pallas_call(
        _mod_kernel,
        grid=(n // tn,),
        in_specs=[pl.BlockSpec((rows, d), lambda j: (0, 0)),
                  pl.BlockSpec((d, tn), lambda j: (0, j)),
                  pl.BlockSpec((1, tn), lambda j: (0, j))],
        out_specs=pl.BlockSpec((rows, tn), lambda j: (0, j)),
        out_shape=jax.ShapeDtypeStruct((rows, n), F32),
        compiler_params=_cparams(("arbitrary",)),
        name="modulation",
    )(cvec, w_mod, b_mod)


def _staggered(tiles):
    for _ in range(tiles[0].lead):
        next(tiles[0].gen)
    live = [t.gen for t in tiles]
    while live:
        live = [g for g in live if next(g, None) is not None]


class _Stages:
    def __init__(self, gen, lead):
        self.gen, self.lead = gen, lead


def _ffn_stages(get_x, put_y, mods, nw_ref, w1_ref, w2_ref, a_ref, x_scr, h_scr, rs, d_ff, tf):
    shift, scale, gate = mods
    for p0 in range(rs.start, rs.stop, PIECE):
        pr = slice(p0, min(p0 + PIECE, rs.stop))
        x = get_x(pr)
        x_scr[pr, :] = x
        h_scr[pr, :] = (_rms(x) * nw_ref[...] * (1.0 + scale) + shift).astype(BF16)
        yield True
    h = h_scr[rs, :]
    for c in range(d_ff // tf):
        g = _dot(h, w1_ref[:, c * tf:(c + 1) * tf].astype(BF16))
        u = _dot(h, w1_ref[:, d_ff + c * tf:d_ff + (c + 1) * tf].astype(BF16))
        a_ref[rs, c * tf:(c + 1) * tf] = (_silu(g) * u).astype(BF16)
        yield True
    y = x_scr[rs, :] + 0.5 * gate * _dot(a_ref[rs, :], w2_ref[...].astype(BF16))
    yield True
    put_y(y)
    yield True


def _sub_tiles(rows):
    sub = min(ROW_SUB, rows)
    return [slice(r0, r0 + sub) for r0 in range(0, rows, sub)]


def _ffn_lead(rs, d_ff, tf):
    return -(-(rs.stop - rs.start) // PIECE) + d_ff // tf // 2


def _ffn_kernel(x_ref, mod_ref, nw_ref, w1_ref, w2_ref, o_ref, a_ref, x_scr, h_scr, *, d_ff, tf):
    mods = [mod_ref[0, k:k + 1, :] for k in range(3)]

    def stages(rs):
        def put(y):
            o_ref[0, rs, :] = y
        return _Stages(_ffn_stages(lambda pr: x_ref[0, pr, :], put, mods, nw_ref, w1_ref, w2_ref, a_ref,
                                   x_scr, h_scr, rs, d_ff, tf), _ffn_lead(rs, d_ff, tf))

    _staggered([stages(rs) for rs in _sub_tiles(x_ref.shape[1])])


def _ffn(s, mods, mod_row, nw, w1, w2):
    bsz, t, d = s.shape
    d_ff = w2.shape[0]
    tm = min(TOKEN_TILE, t)
    const = lambda b, i: (0, 0)
    return pl.pallas_call(
        functools.partial(_ffn_kernel, d_ff=d_ff, tf=FF_TILE),
        grid=(bsz, t // tm),
        in_specs=[pl.BlockSpec((1, tm, d), lambda b, i: (b, i, 0)),
                  pl.BlockSpec((1, 3, d), lambda b, i: (mod_row(b), 0, 0)),
                  pl.BlockSpec((1, d), const),
                  pl.BlockSpec((d, 2 * d_ff), const, pipeline_mode=pl.Buffered(1)),
                  pl.BlockSpec((d_ff, d), const, pipeline_mode=pl.Buffered(1))],
        out_specs=pl.BlockSpec((1, tm, d), lambda b, i: (b, i, 0)),
        out_shape=jax.ShapeDtypeStruct((bsz, t, d), F32),
        scratch_shapes=[pltpu.VMEM((tm, d_ff), BF16),
                        pltpu.VMEM((tm, d), F32),
                        pltpu.VMEM((tm, d), BF16)],
        compiler_params=_cparams(("parallel", "parallel")),
        name="ffn",
    )(s, mods, nw, w1, w2)


def _inproj_kernel(x_ref, mod_ref, nw_ref, w_ref, wg_ref, gb_ref, lbl_ref,
                   hq_ref, hv_ref, hg_ref, hlf_ref, mqk_ref, mv_ref, mo_ref, gc_ref, *, hw):
    shift = mod_ref[0, 0:1, :]
    scale = mod_ref[0, 1:2, :]
    lbl = lbl_ref[...]
    e = jnp.exp(lbl - jnp.max(lbl, axis=0, keepdims=True))
    lb = e[0] / jnp.sum(e, axis=0)
    first_half = lax.broadcasted_iota(jnp.int32, (8, LANE), 1) < CHUNK
    def stages(rs):
        h = (_rms(x_ref[0, rs, :]) * nw_ref[...] * (1.0 + scale) + shift).astype(BF16)
        yield True

        def proj(k):
            return _dot(h, w_ref[:, k * hw:(k + 1) * hw].astype(BF16))

        hq_ref[0, rs, :] = _silu(proj(0)) * (HG_DIM ** -0.5)
        yield True
        hv_ref[0, rs, :] = proj(1).astype(BF16)
        yield True
        hg_ref[0, rs, :] = proj(2).astype(BF16)
        yield True
        for d in range(2):
            lbd = lb[d:d + 1, :]
            hlf_ref[0, d, rs, :] = jnp.log(lbd + (1.0 - lbd) * _sigmoid(proj(3 + d))) * LOG2E
            yield True
        mqk_ref[0, rs, :] = proj(5).astype(BF16)
        yield True
        mv_ref[0, rs, :] = proj(6).astype(BF16)
        yield True
        mo_ref[0, rs, :] = proj(7).astype(BF16)
        gt = _dot_nt(wg_ref[...], h) + gb_ref[...]
        for m in range((rs.stop - rs.start) // LANE):
            top = gt[0:8, m * LANE:(m + 1) * LANE]
            bot = gt[8:16, m * LANE:(m + 1) * LANE]
            c0 = rs.start // CHUNK + 2 * m
            gc_ref[0, c0] = jnp.where(first_half, top, pltpu.roll(bot, CHUNK, 1))
            gc_ref[0, c0 + 1] = jnp.where(first_half, pltpu.roll(top, CHUNK, 1), bot)
        yield True

    _staggered([_Stages(stages(rs), 8) for rs in _sub_tiles(x_ref.shape[1])])


def _inproj(s, mods, mod_row, nw, w_in, w_gate_t, gate_b, lb_logits):
    bsz, t, d = s.shape
    hw = HG_HEADS * HG_DIM
    ng = gate_b.shape[0]
    tm = min(TOKEN_TILE, t)
    tok4 = lambda b, i: (b, i, 0, 0)
    const = lambda b, i: (0, 0)
    tok = lambda b, i: (b, i, 0)
    tok2 = lambda b, i: (b, 0, i, 0)
    f = lambda dt, *shape: jax.ShapeDtypeStruct(shape, dt)
    return pl.pallas_call(
        functools.partial(_inproj_kernel, hw=hw),
        grid=(bsz, t // tm),
        in_specs=[pl.BlockSpec((1, tm, d), tok),
                  pl.BlockSpec((1, 2, d), lambda b, i: (mod_row(b), 0, 0)),
                  pl.BlockSpec((1, d), const),
                  pl.BlockSpec(w_in.shape, const, pipeline_mode=pl.Buffered(1)),
                  pl.BlockSpec(w_gate_t.shape, const),
                  pl.BlockSpec((ng, 1), const),
                  pl.BlockSpec(lb_logits.shape, lambda b, i: (0, 0, 0))],
        out_specs=[pl.BlockSpec((1, tm, hw), tok), pl.BlockSpec((1, tm, hw), tok),
                   pl.BlockSpec((1, tm, hw), tok),
                   pl.BlockSpec((1, 2, tm, hw), tok2),
                   pl.BlockSpec((1, tm, hw), tok), pl.BlockSpec((1, tm, hw), tok),
                   pl.BlockSpec((1, tm, hw), tok), pl.BlockSpec((1, tm // CHUNK, ng // 2, LANE), tok4)],
        out_shape=[f(F32, bsz, t, hw), f(BF16, bsz, t, hw), f(BF16, bsz, t, hw),
                   f(F32, bsz, 2, t, hw),
                   f(BF16, bsz, t, hw), f(BF16, bsz, t, hw), f(BF16, bsz, t, hw),
                   f(F32, bsz, t // CHUNK, ng // 2, LANE)],
        compiler_params=_cparams(("parallel", "parallel")),
        name="inproj",
    )(s, mods, nw, w_in, w_gate_t, gate_b, lb_logits)


def _conv_kernel(x_ref, w_ref, b_ref, o_ref, *, width, q_tiles):
    x = x_ref[0].astype(F32)
    t = x.shape[0]
    col = lax.broadcasted_iota(jnp.int32, x.shape, 0) % width
    xl = jnp.where(col == 0, 0.0, pltpu.roll(x, 1, 0))
    xr = jnp.where(col == width - 1, 0.0, pltpu.roll(x, t - 1, 0))

    def row(di):
        return (w_ref[di, 0:1, :] * xl + w_ref[di, 1:2, :] * x + w_ref[di, 2:3, :] * xr)

    y = row(1)
    if t > width:
        pad = jnp.zeros((width, x.shape[1]), F32)
        y = y + jnp.concatenate([pad, row(0)[:t - width]], axis=0)
        y = y + jnp.concatenate([row(2)[width:], pad], axis=0)
    y = _silu(y + b_ref[...])
    qscale = jnp.where(pl.program_id(1) < q_tiles, ML_QK ** -0.5, 1.0)
    o_ref[0] = (y * qscale).astype(BF16)


def _conv(x, w, b, width):
    bsz, t, ch = x.shape
    return pl.pallas_call(
        functools.partial(_conv_kernel, width=width, q_tiles=ML_HEADS * ML_QK // LANE),
        grid=(bsz, ch // LANE),
        in_specs=[pl.BlockSpec((1, t, LANE), lambda b_, c: (b_, 0, c)),
                  pl.BlockSpec((3, 3, LANE), lambda b_, c: (0, 0, c)),
                  pl.BlockSpec((1, LANE), lambda b_, c: (0, c))],
        out_specs=pl.BlockSpec((1, t, LANE), lambda b_, c: (b_, 0, c)),
        out_shape=jax.ShapeDtypeStruct((bsz, t, ch), BF16),
        compiler_params=_cparams(("parallel", "parallel")),
        name="conv",
    )(x, w, b)


def _positions():
    t = np.arange(CHUNK)
    return [t, CHUNK - 1 - t]


def _hgrn2_masks():
    ds, vs, ws = [], [], []
    for p in _positions():
        le = (p[None, :] <= p[:, None]).astype(np.float32)
        order = [p[SUB * j] // SUB for j in range(N_SUB)]
        ref = np.zeros((32, CHUNK), np.float32)
        valid = np.zeros((N_SUB * N_SUB, 1), np.float32)
        for j in range(N_SUB):
            ref[j] = p <= SUB * order[j] + SUB // 2 - 1
            ref[8 + j] = p <= SUB * order[j] + SUB - 1
        ref[N_SUB] = 1.0
        for j in range(N_SUB):
            for i in range(N_SUB):
                if order[i] < order[j]:
                    ref[16 + N_SUB * j + i] = ref[j] - ref[8 + i]
                    valid[N_SUB * j + i] = 1.0
        ds.append(np.concatenate([le, ref], axis=0))
        vs.append(np.broadcast_to(valid, (N_SUB * N_SUB, HG_HEADS * HG_DIM)))
        tok_blk = np.arange(CHUNK)[:, None] // SUB
        ws.append(np.concatenate([(tok_blk == j) * le for j in range(N_SUB)], axis=1))
    return (np.stack(ds).astype(np.float32), np.stack(vs).astype(np.float32),
            np.stack(ws).astype(np.float32))


def _mlstm_masks():
    le2s, let2s = [], []
    for p in _positions():
        le = (p[None, :] <= p[:, None]).astype(np.float32)
        le2s.append(np.concatenate([le, le], axis=1))
        let2 = np.zeros((2 * CHUNK, 2 * CHUNK), np.float32)
        let2[:CHUNK, :CHUNK] = le.T
        let2[CHUNK:, CHUNK:] = le.T
        let2s.append(let2)
    return np.stack(le2s), np.stack(let2s)


def _run_pipelined(n_groups, group, stages_of, step_of):
    queue = []
    for g in range(n_groups):
        for _ in stages_of(g):
            if queue:
                queue.pop(0)()
        while queue:
            queue.pop(0)()
        queue = [functools.partial(step_of, g * group + i) for i in range(group)]
    for step in queue:
        step()


def _chunk_rows(c, nc, direction):
    cc = jnp.where(direction == 0, c, nc - 1 - c)
    return pl.multiple_of(cc * CHUNK, CHUNK)


def _hgrn2_kernel(q_ref, v_ref, lf_ref, d_ref, pv_ref, w_ref, s0_ref, o_ref, st_ref,
                  u_ref, qin_ref, g_ref, oi_ref, bx_ref, kx_ref, vx_ref, *, nc):
    direction = pl.program_id(1)

    @pl.when(pl.program_id(2) == 0)
    def _():
        st_ref[...] = s0_ref[...]

    dmat = d_ref[0].astype(BF16)
    pair_valid = pv_ref[0]
    wmask = w_ref[0] > 0.0
    L = CHUNK
    hw = q_ref.shape[-1]
    upper_rows = lax.broadcasted_iota(jnp.int32, (L, 2 * L), 0) < L // 2

    group = min(GROUP, nc)

    def intra(grp):
        cs = [grp * group + i for i in range(group)]
        rows = [pl.ds(_chunk_rows(c, nc, direction), L) for c in cs]
        items = [(i, h) for i in range(group) for h in range(HG_HEADS)]
        lanes = [slice(h * HG_DIM, (h + 1) * HG_DIM) for h in range(HG_HEADS)]
        lfs = [lf_ref[0, 0, r, :] for r in rows]
        parts = [_split2(lf) for lf in lfs]
        e_all = sum(_dot(dmat, jnp.concatenate([parts[i][n] for i in range(group)], axis=1))
                    for n in range(2))
        yield
        q_mids, k_mids, k_outs, vbs, spans = [], [], [], [], []
        for i in range(group):
            e = e_all[:, i * hw:(i + 1) * hw]
            q = q_ref[0, rows[i], :]
            k = 1.0 - jnp.exp2(lfs[i])
            b = e[0:L]
            b_tot = e[L + N_SUB:L + N_SUB + 1]
            sub_rows = lambda r0: jnp.concatenate(
                [jnp.broadcast_to(e[r0 + j:r0 + j + 1], (SUB, hw)) for j in range(N_SUB)], axis=0)
            mid_rows = sub_rows(L)
            end_rows = sub_rows(L + 8)
            to_mid = b - mid_rows
            spans.append(jnp.max(jnp.abs(to_mid)))
            q_mids.append((q * jnp.exp2(jnp.minimum(to_mid, EXP2_CLAMP))).astype(BF16))
            k_diag = k * jnp.exp2(jnp.minimum(-to_mid, EXP2_CLAMP))
            k_end = k * jnp.exp2(end_rows - b)
            cross = jnp.exp2(e[L + 16:L + 32]) * pair_valid
            slabs = []
            for j in range(N_SUB):
                for ib in range(N_SUB):
                    rs = slice(ib * SUB, (ib + 1) * SUB)
                    row = N_SUB * j + ib
                    slabs.append(k_diag[rs] if ib == j else k_end[rs] * cross[row:row + 1])
            k_mids.append(jnp.concatenate(slabs, axis=0).astype(BF16))
            k_outs.append((k * jnp.exp2(b_tot - b)).astype(BF16))
            vbs.append(v_ref[0, rows[i], :])
            qin_ref[cs[i]] = (q * jnp.exp2(b)).astype(BF16)
            g_ref[cs[i]] = jnp.broadcast_to(jnp.exp2(b_tot), (8, hw))
            yield
        scores = [_dot_nt(q_mids[i][:, lanes[h]], k_mids[i][:, lanes[h]]) for i, h in items]
        yield
        a_s = [jnp.where(wmask, s, 0.0) for s in scores]
        a_s = [jnp.where(upper_rows, a[:, :2 * L], a[:, 2 * L:]).astype(BF16) for a in a_s]
        v2s = [jnp.concatenate([vb, vb], axis=0) for vb in vbs]
        outs = [_dot(a_s[n], v2s[i][:, lanes[h]]) for n, (i, h) in enumerate(items)]
        yield
        for i in range(group):
            oi_ref[cs[i]] = jnp.concatenate(outs[i * HG_HEADS:(i + 1) * HG_HEADS], axis=1)
        yield
        for i, h in items:
            u_ref[cs[i], h] = _dot_tn(vbs[i][:, lanes[h]], k_outs[i][:, lanes[h]])

        @pl.when(functools.reduce(jnp.maximum, spans) > EXP2_CLAMP)
        def _():
            for i in range(group):
                exact_intra(cs[i], rows[i])
        yield

    def exact_intra(c, rows):
        lf = lf_ref[0, 0, rows, :]
        bx_ref[...] = sum(_dot(dmat[0:L], x) for x in _split3(lf))
        kx_ref[...] = 1.0 - jnp.exp2(lf)
        vx_ref[...] = v_ref[0, rows, :].astype(F32)
        q = q_ref[0, rows, :]
        tok = lax.broadcasted_iota(jnp.int32, (L, 1), 0)
        pos = jnp.where(direction == 0, tok, L - 1 - tok)

        def add_source(s, acc):
            b_s = bx_ref[pl.ds(s, 1), :]
            z = q * kx_ref[pl.ds(s, 1), :] * jnp.exp2(jnp.minimum(bx_ref[...] - b_s, 0.0))
            v_s = vx_ref[pl.ds(s, 1), :]
            seen = pos >= jnp.where(direction == 0, s, L - 1 - s)
            cols = [jnp.where(seen, jnp.sum(z[:, ln], axis=1, keepdims=True), 0.0) * v_s[:, ln]
                    for ln in (slice(h * HG_DIM, (h + 1) * HG_DIM) for h in range(HG_HEADS))]
            return acc + jnp.concatenate(cols, axis=1)

        oi_ref[c] = lax.fori_loop(0, L, add_source, jnp.zeros((L, hw), F32))

    def carry_state(c):
        rows = pl.ds(_chunk_rows(c, nc, direction), L)
        q_in = qin_ref[c]
        g = g_ref[c][0:1]
        outs, states = [], []
        for h in range(HG_HEADS):
            ln = slice(h * HG_DIM, (h + 1) * HG_DIM)
            st = st_ref[0, 0, h]
            outs.append(_dot_nt(q_in[:, ln], st.astype(BF16)))
            states.append(g[:, ln] * st + u_ref[c, h])
        o_ref[0, 0, rows, :] = (oi_ref[c] + jnp.concatenate(outs, axis=1)).astype(BF16)
        for h in range(HG_HEADS):
            st_ref[0, 0, h] = states[h]

    _run_pipelined(nc // group, group, intra, carry_state)


def _hgrn2(q, v, lf, s0, dmat, pair_valid, wmask):
    bsz, t, hw = q.shape
    nc = min(HG_CHUNKS, t // CHUNK)
    tb = nc * CHUNK
    nblk = t // tb

    def blk(j, d):
        return jnp.where(d == 0, j, nblk - 1 - j)

    st_spec = pl.BlockSpec((1, 1) + s0.shape[2:], lambda b, d, j: (b, d, 0, 0, 0))
    return pl.pallas_call(
        functools.partial(_hgrn2_kernel, nc=nc),
        grid=(bsz, 2, nblk),
        in_specs=[pl.BlockSpec((1, tb, hw), lambda b, d, j: (b, blk(j, d), 0)),
                  pl.BlockSpec((1, tb, hw), lambda b, d, j: (b, blk(j, d), 0)),
                  pl.BlockSpec((1, 1, tb, hw), lambda b, d, j: (b, d, blk(j, d), 0)),
                  pl.BlockSpec((1,) + dmat.shape[1:], lambda b, d, j: (d, 0, 0)),
                  pl.BlockSpec((1,) + pair_valid.shape[1:], lambda b, d, j: (d, 0, 0)),
                  pl.BlockSpec((1,) + wmask.shape[1:], lambda b, d, j: (d, 0, 0)),
                  st_spec],
        out_specs=[pl.BlockSpec((1, 1, tb, hw), lambda b, d, j: (b, d, blk(j, d), 0)), st_spec],
        out_shape=[jax.ShapeDtypeStruct((bsz, 2, t, hw), BF16),
                   jax.ShapeDtypeStruct(s0.shape, F32)],
        scratch_shapes=[pltpu.VMEM((nc,) + s0.shape[2:], F32),
                        pltpu.VMEM((nc, CHUNK, hw), BF16),
                        pltpu.VMEM((nc, 8, hw), F32),
                        pltpu.VMEM((nc, CHUNK, hw), F32),
                        pltpu.VMEM((CHUNK, hw), F32),
                        pltpu.VMEM((CHUNK, hw), F32),
                        pltpu.VMEM((CHUNK, hw), F32)],
        compiler_params=_cparams(("parallel", "parallel", "arbitrary")),
        name="hgrn2_scan",
    )(q, v, lf, dmat, pair_valid, wmask, s0)


def _mlstm_kernel(qk_ref, v_ref, g_ref, le2_ref, let2_ref, c0_ref, m0_ref,
                  o_ref, c_ref, mm_ref,
                  p_ref, u_ref, qs_ref, ct_ref, mloc_ref, sc_ref, *, nc):
    direction = pl.program_id(1)

    @pl.when(pl.program_id(2) == 0)
    def _():
        c_ref[...] = c0_ref[...]
        mm_ref[...] = m0_ref[...]

    le2 = le2_ref[0].astype(BF16)
    let2 = let2_ref[0].astype(BF16)
    causal = le2_ref[0] > 0.0
    L = CHUNK
    neg = -jnp.inf
    lane = lax.broadcasted_iota(jnp.int32, (1, LANE), 1)
    seg = [lane < L, lane >= L]
    lane_t = lax.broadcasted_iota(jnp.int32, (L, LANE), 1) < L
    eye = (lax.broadcasted_iota(jnp.int32, (LANE, LANE), 0)
           == lax.broadcasted_iota(jnp.int32, (LANE, LANE), 1))
    npair = ML_HEADS // 2
    ones_v = jnp.ones((L, ML_V), BF16)

    group = min(ML_GROUP, nc)

    def intra(grp):
        cs = [grp * group + i for i in range(group)]
        ccs = [jnp.where(direction == 0, c, nc - 1 - c) for c in cs]
        rows = [pl.ds(pl.multiple_of(cc * L, L), L) for cc in ccs]
        pairs = [(i, p) for i in range(group) for p in range(npair)]
        gs = [jnp.where(direction == 0, g_ref[0, cc][0:4], g_ref[0, cc][4:8]) for cc in ccs]
        pad = [jnp.zeros((-2 * group % 8, LANE), F32)] if 2 * group % 8 else []
        i2 = jnp.concatenate([g[0:2] for g in gs] + pad, axis=0)
        lf2 = _log_sigmoid(jnp.concatenate([g[2:4] for g in gs] + pad, axis=0))
        parts = _split3(lf2)
        r2 = i2 - sum(_dot(x, let2) for x in parts)
        g_a = jnp.max(jnp.where(seg[0], r2, neg), axis=1, keepdims=True)
        g_b = jnp.max(jnp.where(seg[1], r2, neg), axis=1, keepdims=True)
        e_w = jnp.exp(r2 - jnp.where(seg[0], g_a, g_b))
        bt_a = jnp.sum(jnp.where(seg[0], lf2, 0.0), axis=1, keepdims=True)
        bt_b = jnp.sum(jnp.where(seg[1], lf2, 0.0), axis=1, keepdims=True)
        yield

        def head_rows(x, i):
            zero = jnp.zeros_like(x[0:1])
            return jnp.concatenate(
                [jnp.broadcast_to(jnp.where(seg[h % 2], x[2 * i + h // 2:2 * i + h // 2 + 1], zero),
                                  (LANE, LANE)) for h in range(ML_HEADS)], axis=0)

        c_ts = [sum(_dot_nt(le2, head_rows(x, i)) for x in parts) for i in range(group)]
        for i in range(group):
            ct_ref[cs[i]] = c_ts[i]
            sc_ref[cs[i]] = jnp.concatenate(
                [jnp.broadcast_to(x[2 * i + p:2 * i + p + 1], (1, LANE))
                 for x2 in ((bt_a, bt_b), (g_a, g_b)) for p in range(npair) for x in x2], axis=0)
        yield
        qks = [qk_ref[0, r, :] for r in rows]
        vbs = [v_ref[0, r, :] for r in rows]
        q_ps = [qks[i][:, p * LANE:(p + 1) * LANE] for i, p in pairs]
        k_ps = [qks[i][:, (npair + p) * LANE:(npair + p + 1) * LANE] for i, p in pairs]
        zb = jnp.zeros((L, LANE), BF16)
        kbs = [jnp.concatenate([jnp.where(lane_t, k, zb), jnp.where(lane_t, zb, k)], axis=0) for k in k_ps]
        scores = [_dot_nt(q, kb) for q, kb in zip(q_ps, kbs)]
        yield
        ws = []
        for n, (i, p) in enumerate(pairs):
            ha, hb = 2 * p, 2 * p + 1
            t1 = jnp.where(lane_t, c_ts[i][:, ha * ML_V:(ha + 1) * ML_V], c_ts[i][:, hb * ML_V:(hb + 1) * ML_V])
            d_log = jnp.where(causal, t1 + r2[2 * i + p:2 * i + p + 1], neg)
            m_a = jnp.max(jnp.where(lane_t, d_log, neg), axis=1, keepdims=True)
            m_b = jnp.max(jnp.where(lane_t, neg, d_log), axis=1, keepdims=True)
            mloc_ref[cs[i], :, ha * ML_V:(ha + 1) * ML_V] = jnp.broadcast_to(m_a, (L, ML_V))
            mloc_ref[cs[i], :, hb * ML_V:(hb + 1) * ML_V] = jnp.broadcast_to(m_b, (L, ML_V))
            ws.append(jnp.exp(d_log - jnp.where(lane_t, m_a, m_b)))
        yield
        a_s = [(s_ * w).astype(BF16) for s_, w in zip(scores, ws)]
        a_stacks = [jnp.concatenate([jnp.where(lane_t, a, zb), jnp.where(lane_t, zb, a)], axis=0) for a in a_s]
        v_stacks = [jnp.concatenate(
            [jnp.concatenate([vbs[i][:, 2 * p * ML_V:(2 * p + 1) * ML_V], ones_v], axis=1),
             jnp.concatenate([vbs[i][:, (2 * p + 1) * ML_V:(2 * p + 2) * ML_V], ones_v], axis=1)], axis=0)
            for i, p in pairs]
        for n, (i, p) in enumerate(pairs):
            p_ref[cs[i], p] = _dot(a_stacks[n], v_stacks[n])
            qs_ref[cs[i], p] = jnp.concatenate(
                [jnp.where(lane_t, q_ps[n], zb), jnp.where(lane_t, zb, q_ps[n])], axis=0)
        yield
        diags = [jnp.where(eye, jnp.broadcast_to(e_w[2 * i + p:2 * i + p + 1], (LANE, LANE)), 0.0).astype(BF16)
                 for i, p in pairs]
        kb_ws = [_dot(d, kb).astype(BF16) for d, kb in zip(diags, kbs)]
        for n, (i, p) in enumerate(pairs):
            u_ref[cs[i], p] = _dot_tn(kb_ws[n], v_stacks[n])
        yield

    def carry_state(c):
        cc = jnp.where(direction == 0, c, nc - 1 - c)
        rows = pl.ds(pl.multiple_of(cc * L, L), L)
        m_all = mm_ref[0, 0]
        sc = sc_ref[c]
        c_t = ct_ref[c]
        m_loc = mloc_ref[c]
        outs, states, m_rows = [], [], []
        for p in range(ML_HEADS // 2):
            c_pair = c_ref[0, 0, p]
            inter = _dot(qs_ref[c, p], c_pair.astype(BF16))
            intra_p = p_ref[c, p]
            inc = u_ref[c, p]
            new_rows = []
            for hh in range(2):
                h = 2 * p + hh
                rs = slice(hh * L, (hh + 1) * L)
                ln = slice(h * ML_V, (h + 1) * ML_V)
                m_in = m_all[h:h + 1]
                m_t = jnp.maximum(m_loc[:, ln], c_t[:, ln] + m_in)
                alpha = jnp.exp(m_loc[:, ln] - m_t)
                beta = jnp.exp(c_t[:, ln] + m_in - m_t)
                num = alpha * intra_p[rs, :ML_V] + beta * inter[rs, :ML_V]
                den = alpha * intra_p[rs, ML_V:] + beta * inter[rs, ML_V:]
                outs.append(num / jnp.maximum(jnp.abs(den), jnp.exp(-m_t)))
                b_tot = sc[h:h + 1]
                m_new = b_tot + jnp.maximum(m_in, sc[ML_HEADS + h:ML_HEADS + h + 1])
                carry_w = jnp.exp(b_tot + m_in - m_new)
                inc_w = jnp.exp(b_tot + sc[ML_HEADS + h:ML_HEADS + h + 1] - m_new)
                cw2 = jnp.concatenate([carry_w, carry_w], axis=1)
                iw2 = jnp.concatenate([inc_w, inc_w], axis=1)
                new_rows.append(cw2 * c_pair[rs] + iw2 * inc[rs])
                m_rows.append(m_new)
            states.append(jnp.concatenate(new_rows, axis=0))
        o_ref[0, 0, rows, :] = jnp.concatenate(outs, axis=1).astype(BF16)
        for p in range(ML_HEADS // 2):
            c_ref[0, 0, p] = states[p]
        mm_ref[0, 0] = jnp.concatenate(m_rows + [m_all[ML_HEADS:]], axis=0)

    _run_pipelined(nc // group, group, intra, carry_state)


def _mlstm(qk, v, gates, c0, m0, le2, let2):
    bsz, t, hw = v.shape
    nc = min(ML_CHUNKS, t // CHUNK)
    tb = nc * CHUNK
    nblk = t // tb
    npair = ML_HEADS // 2

    def blk(j, d):
        return jnp.where(d == 0, j, nblk - 1 - j)

    c_spec = pl.BlockSpec((1, 1) + c0.shape[2:], lambda b, d, j: (b, d, 0, 0, 0))
    m_spec = pl.BlockSpec((1, 1) + m0.shape[2:], lambda b, d, j: (b, d, 0, 0))
    return pl.pallas_call(
        functools.partial(_mlstm_kernel, nc=nc),
        grid=(bsz, 2, nblk),
        in_specs=[pl.BlockSpec((1, tb, qk.shape[-1]), lambda b, d, j: (b, blk(j, d), 0)),
                  pl.BlockSpec((1, tb, hw), lambda b, d, j: (b, blk(j, d), 0)),
                  pl.BlockSpec((1, nc) + gates.shape[2:], lambda b, d, j: (b, blk(j, d), 0, 0)),
                  pl.BlockSpec((1,) + le2.shape[1:], lambda b, d, j: (d, 0, 0)),
                  pl.BlockSpec((1,) + let2.shape[1:], lambda b, d, j: (d, 0, 0)),
                  c_spec, m_spec],
        out_specs=[pl.BlockSpec((1, 1, tb, hw), lambda b, d, j: (b, d, blk(j, d), 0)), c_spec, m_spec],
        out_shape=[jax.ShapeDtypeStruct((bsz, 2, t, hw), BF16),
                   jax.ShapeDtypeStruct(c0.shape, F32),
                   jax.ShapeDtypeStruct(m0.shape, F32)],
        scratch_shapes=[pltpu.VMEM((nc, npair, 2 * CHUNK, 2 * ML_V), F32),
                        pltpu.VMEM((nc, npair, 2 * ML_QK, 2 * ML_V), F32),
                        pltpu.VMEM((nc, npair, 2 * CHUNK, LANE), BF16),
                        pltpu.VMEM((nc, CHUNK, hw), F32),
                        pltpu.VMEM((nc, CHUNK, hw), F32),
                        pltpu.VMEM((nc, 8, LANE), F32)],
        compiler_params=_cparams(("parallel", "parallel", "arbitrary")),
        name="mlstm_scan",
    )(qk, v, gates, le2, let2, c0, m0)


def _head_rms(x, heads):
    dh = x.shape[-1] // heads
    return jnp.concatenate([_rms(x[:, h * dh:(h + 1) * dh]) for h in range(heads)], axis=1)


def _mix_ffn_kernel(oh_ref, om_ref, hg_ref, mo_ref, x_ref, mod_ref, hnw_ref, mnw_ref, wo_ref,
                    nw_ref, w1_ref, w2_ref, fw_ref, o_ref, a_ref, x_scr, h_scr, *, d_ff, tf):
    mix_gate = mod_ref[0, 0:1, :]
    mods = [mod_ref[0, k:k + 1, :] for k in range(1, 4)]

    def stages(rs):
        def mixed(pr):
            oh = oh_ref[0, 0, pr, :].astype(F32) + oh_ref[0, 1, pr, :].astype(F32)
            om = om_ref[0, 0, pr, :].astype(F32) + om_ref[0, 1, pr, :].astype(F32)
            hg_out = _head_rms(oh, HG_HEADS) * hnw_ref[...] * _silu(hg_ref[0, pr, :].astype(F32))
            ml_out = _sigmoid(mo_ref[0, pr, :].astype(F32)) * (_head_rms(om, ML_HEADS) * mnw_ref[...])
            merged = jnp.concatenate([hg_out, ml_out], axis=1).astype(BF16)
            return x_ref[0, pr, :] + mix_gate * _dot(merged, wo_ref[...])

        def put(y):
            o_ref[0, rs, :] = _rms(y) * fw_ref[...]

        return _Stages(_ffn_stages(mixed, put, mods, nw_ref, w1_ref, w2_ref, a_ref, x_scr, h_scr,
                                   rs, d_ff, tf), _ffn_lead(rs, d_ff, tf))

    _staggered([stages(rs) for rs in _sub_tiles(x_ref.shape[1])])


def _mix_ffn(oh, om, hg, mo, x, mods, hnw, mnw, w_out, nw, w1, w2, fw):
    bsz, t, d = x.shape
    hw = hg.shape[-1]
    d_ff = w2.shape[0]
    tm = min(TOKEN_TILE, t)
    tok = lambda b, i: (b, i, 0)
    tok2 = lambda b, i: (b, 0, i, 0)
    const = lambda b, i: (0, 0)
    resident = lambda a: pl.BlockSpec(a.shape, const, pipeline_mode=pl.Buffered(1))
    return pl.pallas_call(
        functools.partial(_mix_ffn_kernel, d_ff=d_ff, tf=FF_TILE),
        grid=(bsz, t // tm),
        in_specs=[pl.BlockSpec((1, 2, tm, hw), tok2), pl.BlockSpec((1, 2, tm, hw), tok2),
                  pl.BlockSpec((1, tm, hw), tok), pl.BlockSpec((1, tm, hw), tok),
                  pl.BlockSpec((1, tm, d), tok),
                  pl.BlockSpec((1, 4, d), lambda b, i: (b, 0, 0)),
                  pl.BlockSpec((1, hw), const), pl.BlockSpec((1, hw), const), resident(w_out),
                  pl.BlockSpec((1, d), const), resident(w1), resident(w2), pl.BlockSpec((1, d), const)],
        out_specs=pl.BlockSpec((1, tm, d), tok),
        out_shape=jax.ShapeDtypeStruct((bsz, t, d), F32),
        scratch_shapes=[pltpu.VMEM((tm, d_ff), BF16),
                        pltpu.VMEM((tm, d), F32),
                        pltpu.VMEM((tm, d), BF16)],
        compiler_params=_cparams(("parallel", "parallel")),
        name="mix_ffn_final",
    )(oh, om, hg, mo, x, mods, hnw, mnw, w_out, nw, w1, w2, fw)


def kernel(x, c, ctx, c_ctx, w_mod, b_mod, norm1_w, ffn1_w1, ffn1_w2, norm2_w, w_in, ml_gate_b,
           ml_conv_w, ml_conv_b, hg_lb_logits, hg_norm_w, ml_norm_w, w_out, norm3_w, ffn2_w1, ffn2_w2,
           final_norm_w):
    bsz, seq, d = x.shape
    n_ctx = ctx.shape[1]
    assert w_mod.shape[0] == 1, "single-layer kernel"
    assert seq % (GRID_W * 8) == 0 and n_ctx % CHUNK == 0
    hw = HG_HEADS * HG_DIM
    ng = ml_gate_b.shape[-1]

    rows = -(-(bsz + 1) // 8) * 8
    cvec = jnp.zeros((rows, d), F32).at[:bsz].set(c).at[bsz].set(c_ctx)
    mods = _modulation(cvec, w_mod[0], b_mod[0][None, :]).reshape(rows, N_MOD, d)
    lat_row = lambda b: b
    ctx_row = lambda b: bsz

    row = lambda a: a.reshape(1, -1)
    w1a, w2a = ffn1_w1[0], ffn1_w2[0]
    x1 = _ffn(x, mods[:, 0:3], lat_row, row(norm1_w[0]), w1a, w2a)
    s1 = _ffn(ctx, mods[:, 0:3], ctx_row, row(norm1_w[0]), w1a, w2a)

    w_in_b = w_in[0]
    gate_order = np.arange(ng).reshape(2, 2, ML_HEADS // 2, 2).transpose(3, 0, 1, 2).reshape(-1)
    w_gate_t = w_in[0][:, 8 * hw:].T[gate_order].astype(BF16)
    gate_b = ml_gate_b[0][gate_order].reshape(ng, 1)
    dmat, pair_valid, wmask = (jnp.asarray(a) for a in _hgrn2_masks())
    le2, let2 = (jnp.asarray(a) for a in _mlstm_masks())
    conv_w, conv_b = ml_conv_w[0], row(ml_conv_b[0])
    npair = ML_HEADS // 2

    def mixer_scans(s, mod_row, width, s0, c0, m0):
        hq, hv, hg, hlf, mqk, mv, mo, gates = _inproj(
            s, mods[:, 3:5], mod_row, row(norm2_w[0]), w_in_b, w_gate_t, gate_b, hg_lb_logits)
        qk = _conv(mqk, conv_w, conv_b, width)
        oh, s_fin = _hgrn2(hq, hv, hlf, s0, dmat, pair_valid, wmask)
        om, c_fin, m_fin = _mlstm(qk, mv, gates, c0, m0, le2, let2)
        return oh, om, hg, mo, s_fin, c_fin, m_fin

    s0 = jnp.zeros((bsz, 2, HG_HEADS, HG_DIM, HG_DIM), F32)
    c0 = jnp.zeros((bsz, 2, npair, 2 * ML_QK, 2 * ML_V), F32)
    m0 = jnp.zeros((bsz, 2, 8, LANE), F32)
    _, _, _, _, s_ctx, c_ctx_state, m_ctx = mixer_scans(s1, ctx_row, n_ctx, s0, c0, m0)
    oh, om, hg, mo, _, _, _ = mixer_scans(x1, lat_row, GRID_W, s_ctx, c_ctx_state, m_ctx)

    return _mix_ffn(oh, om, hg, mo, x1, mods[:, 5:9], row(hg_norm_w[0]), row(ml_norm_w[0]),
                    w_out[0].astype(BF16), row(norm3_w[0]), ffn2_w1[0], ffn2_w2[0], row(final_norm_w))
```

```python
import functools

import numpy as np
import jax
import jax.numpy as jnp
from jax import lax
from jax.experimental import pallas as pl
from jax.experimental.pallas import tpu as pltpu

F32 = jnp.float32
BF16 = jnp.bfloat16

EPS = 1e-6
CHUNK = 64
GRID_W = 64
N_MOD = 9
HG_HEADS = 4
HG_DIM = 128
ML_HEADS = 4
ML_QK = 64
ML_V = 128
SUB = 16
N_SUB = CHUNK // SUB
GROUP = 4
ML_GROUP = 4
HG_CHUNKS = 8
ML_CHUNKS = 16
TOKEN_TILE = 512
ROW_SUB = 256
FF_TILE = 256
PIECE = 128
LOG2E = 1.4426950408889634
EXP2_CLAMP = 115.0
LANE = 128
VMEM_LIMIT = 56 * 1024 * 1024


def _cparams(sem):
    return pltpu.CompilerParams(dimension_semantics=sem, vmem_limit_bytes=VMEM_LIMIT)


def _dot(a, b):
    return jnp.dot(a, b, preferred_element_type=F32)


def _dot_nt(a, b):
    return lax.dot_general(a, b, (((1,), (1,)), ((), ())), preferred_element_type=F32)


def _dot_tn(a, b):
    return lax.dot_general(a, b, (((0,), (0,)), ((), ())), preferred_element_type=F32)


def _sigmoid(x):
    return 1.0 / (1.0 + jnp.exp(-x))


def _silu(x):
    return x * _sigmoid(x)


def _log_sigmoid(x):
    return jnp.minimum(x, 0.0) - jnp.log(1.0 + jnp.exp(-jnp.abs(x)))


def _split3(x):
    hi = x.astype(BF16)
    r = x - hi.astype(F32)
    mid = r.astype(BF16)
    lo = (r - mid.astype(F32)).astype(BF16)
    return hi, mid, lo


def _split2(x):
    hi = x.astype(BF16)
    return hi, (x - hi.astype(F32)).astype(BF16)


def _rms(x):
    return x * lax.rsqrt(jnp.mean(x * x, axis=-1, keepdims=True) + EPS)


def _mod_kernel(c_ref, w_ref, b_ref, o_ref):
    a = _silu(c_ref[...])
    a_hi = a.astype(BF16)
    a_lo = (a - a_hi.astype(F32)).astype(BF16)
    w = w_ref[...]
    w_hi = w.astype(BF16)
    w_lo = (w - w_hi.astype(F32)).astype(BF16)
    o_ref[...] = _dot(a_hi, w_hi) + _dot(a_hi, w_lo) + _dot(a_lo, w_hi) + b_ref[...]


def _modulation(cvec, w_mod, b_mod):
    rows, d = cvec.shape
    n = w_mod.shape[1]
    tn = 1024
    return pl.pallas_call(
        _mod_kernel,
        grid=(n // tn,),
        in_specs=[pl.BlockSpec((rows, d), lambda j: (0, 0)),
                  pl.BlockSpec((d, tn), lambda j: (0, j)),
                  pl.BlockSpec((1, tn), lambda j: (0, j))],
        out_specs=pl.BlockSpec((rows, tn), lambda j: (0, j)),
        out_shape=jax.ShapeDtypeStruct((rows, n), F32),
        compiler_params=_cparams(("arbitrary",)),
        name="modulation",
    )(cvec, w_mod, b_mod)


def _staggered(tiles):
    for _ in range(tiles[0].lead):
        next(tiles[0].gen)
    live = [t.gen for t in tiles]
    while live:
        live = [g for g in live if next(g, None) is not None]


class _Stages:
    def __init__(self, gen, lead):
        self.gen, self.lead = gen, lead


def _ffn_stages(get_x, put_y, mods, nw_ref, w1_ref, w2_ref, a_ref, x_scr, h_scr, rs, d_ff, tf):
    shift, scale, gate = mods
    for p0 in range(rs.start, rs.stop, PIECE):
        pr = slice(p0, min(p0 + PIECE, rs.stop))
        x = get_x(pr)
        x_scr[pr, :] = x
        h_scr[pr, :] = (_rms(x) * nw_ref[...] * (1.0 + scale) + shift).astype(BF16)
        yield True
    h = h_scr[rs, :]
    for c in range(d_ff // tf):
        g = _dot(h, w1_ref[:, c * tf:(c + 1) * tf].astype(BF16))
        u = _dot(h, w1_ref[:, d_ff + c * tf:d_ff + (c + 1) * tf].astype(BF16))
        a_ref[rs, c * tf:(c + 1) * tf] = (_silu(g) * u).astype(BF16)
        yield True
    y = x_scr[rs, :] + 0.5 * gate * _dot(a_ref[rs, :], w2_ref[...].astype(BF16))
    yield True
    put_y(y)
    yield True


def _sub_tiles(rows):
    sub = min(ROW_SUB, rows)
    return [slice(r0, r0 + sub) for r0 in range(0, rows, sub)]


def _ffn_lead(rs, d_ff, tf):
    return -(-(rs.stop - rs.start) // PIECE) + d_ff // tf // 2


def _ffn_kernel(x_ref, mod_ref, nw_ref, w1_ref, w2_ref, o_ref, a_ref, x_scr, h_scr, *, d_ff, tf):
    mods = [mod_ref[0, k:k + 1, :] for k in range(3)]

    def stages(rs):
        def put(y):
            o_ref[0, rs, :] = y
        return _Stages(_ffn_stages(lambda pr: x_ref[0, pr, :], put, mods, nw_ref, w1_ref, w2_ref, a_ref,
                                   x_scr, h_scr, rs, d_ff, tf), _ffn_lead(rs, d_ff, tf))

    _staggered([stages(rs) for rs in _sub_tiles(x_ref.shape[1])])


def _ffn(s, mods, mod_row, nw, w1, w2):
    bsz, t, d = s.shape
    d_ff = w2.shape[0]
    tm = min(TOKEN_TILE, t)
    const = lambda b, i: (0, 0)
    return pl.pallas_call(
        functools.partial(_ffn_kernel, d_ff=d_ff, tf=FF_TILE),
        grid=(bsz, t // tm),
        in_specs=[pl.BlockSpec((1, tm, d), lambda b, i: (b, i, 0)),
                  pl.BlockSpec((1, 3, d), lambda b, i: (mod_row(b), 0, 0)),
                  pl.BlockSpec((1, d), const),
                  pl.BlockSpec((d, 2 * d_ff), const, pipeline_mode=pl.Buffered(1)),
                  pl.BlockSpec((d_ff, d), const, pipeline_mode=pl.Buffered(1))],
        out_specs=pl.BlockSpec((1, tm, d), lambda b, i: (b, i, 0)),
        out_shape=jax.ShapeDtypeStruct((bsz, t, d), F32),
        scratch_shapes=[pltpu.VMEM((tm, d_ff), BF16),
                        pltpu.VMEM((tm, d), F32),
                        pltpu.VMEM((tm, d), BF16)],
        compiler_params=_cparams(("parallel", "parallel")),
        name="ffn",
    )(s, mods, nw, w1, w2)


def _inproj_kernel(x_ref, mod_ref, nw_ref, w_ref, wg_ref, gb_ref, lbl_ref,
                   hq_ref, hv_ref, hg_ref, hlf_ref, mqk_ref, mv_ref, mo_ref, gc_ref, *, hw):
    shift = mod_ref[0, 0:1, :]
    scale = mod_ref[0, 1:2, :]
    lbl = lbl_ref[...]
    e = jnp.exp(lbl - jnp.max(lbl, axis=0, keepdims=True))
    lb = e[0] / jnp.sum(e, axis=0)
    first_half = lax.broadcasted_iota(jnp.int32, (8, LANE), 1) < CHUNK
    def stages(rs):
        h = (_rms(x_ref[0, rs, :]) * nw_ref[...] * (1.0 + scale) + shift).astype(BF16)
        yield True

        def proj(k):
            return _dot_nt(h, w_ref[k * hw:(k + 1) * hw, :].astype(BF16))

        hq_ref[0, rs, :] = _silu(proj(0)) * (HG_DIM ** -0.5)
        yield True
        hv_ref[0, rs, :] = proj(1).astype(BF16)
        yield True
        hg_ref[0, rs, :] = proj(2).astype(BF16)
        yield True
        for d in range(2):
            lbd = lb[d:d + 1, :]
            hlf_ref[0, d, rs, :] = jnp.log(lbd + (1.0 - lbd) * _sigmoid(proj(3 + d))) * LOG2E
            yield True
        mqk_ref[0, rs, :] = proj(5).astype(BF16)
        yield True
        mv_ref[0, rs, :] = proj(6).astype(BF16)
        yield True
        mo_ref[0, rs, :] = proj(7).astype(BF16)
        gt = _dot_nt(wg_ref[...], h) + gb_ref[...]
        for m in range((rs.stop - rs.start) // LANE):
            top = gt[0:8, m * LANE:(m + 1) * LANE]
            bot = gt[8:16, m * LANE:(m + 1) * LANE]
            c0 = rs.start // CHUNK + 2 * m
            gc_ref[0, c0] = jnp.where(first_half, top, pltpu.roll(bot, CHUNK, 1))
            gc_ref[0, c0 + 1] = jnp.where(first_half, pltpu.roll(top, CHUNK, 1), bot)
        yield True

    _staggered([_Stages(stages(rs), 8) for rs in _sub_tiles(x_ref.shape[1])])


def _inproj(s, mods, mod_row, nw, w_in, w_gate_t, gate_b, lb_logits):
    bsz, t, d = s.shape
    hw = HG_HEADS * HG_DIM
    ng = gate_b.shape[0]
    tm = min(TOKEN_TILE, t)
    tok4 = lambda b, i: (b, i, 0, 0)
    const = lambda b, i: (0, 0)
    tok = lambda b, i: (b, i, 0)
    tok2 = lambda b, i: (b, 0, i, 0)
    f = lambda dt, *shape: jax.ShapeDtypeStruct(shape, dt)
    return pl.pallas_call(
        functools.partial(_inproj_kernel, hw=hw),
        grid=(bsz, t // tm),
        in_specs=[pl.BlockSpec((1, tm, d), tok),
                  pl.BlockSpec((1, 2, d), lambda b, i: (mod_row(b), 0, 0)),
                  pl.BlockSpec((1, d), const),
                  pl.BlockSpec(w_in.shape, const, pipeline_mode=pl.Buffered(1)),
                  pl.BlockSpec(w_gate_t.shape, const),
                  pl.BlockSpec((ng, 1), const),
                  pl.BlockSpec(lb_logits.shape, lambda b, i: (0, 0, 0))],
        out_specs=[pl.BlockSpec((1, tm, hw), tok), pl.BlockSpec((1, tm, hw), tok),
                   pl.BlockSpec((1, tm, hw), tok),
                   pl.BlockSpec((1, 2, tm, hw), tok2),
                   pl.BlockSpec((1, tm, hw), tok), pl.BlockSpec((1, tm, hw), tok),
                   pl.BlockSpec((1, tm, hw), tok), pl.BlockSpec((1, tm // CHUNK, ng // 2, LANE), tok4)],
        out_shape=[f(F32, bsz, t, hw), f(BF16, bsz, t, hw), f(BF16, bsz, t, hw),
                   f(F32, bsz, 2, t, hw),
                   f(BF16, bsz, t, hw), f(BF16, bsz, t, hw), f(BF16, bsz, t, hw),
                   f(F32, bsz, t // CHUNK, ng // 2, LANE)],
        compiler_params=_cparams(("parallel", "parallel")),
        name="inproj",
    )(s, mods, nw, w_in, w_gate_t, gate_b, lb_logits)


def _conv_kernel(x_ref, w_ref, b_ref, o_ref, *, width, q_tiles):
    x = x_ref[0].astype(F32)
    t = x.shape[0]
    col = lax.broadcasted_iota(jnp.int32, x.shape, 0) % width
    xl = jnp.where(col == 0, 0.0, pltpu.roll(x, 1, 0))
    xr = jnp.where(col == width - 1, 0.0, pltpu.roll(x, t - 1, 0))

    def row(di):
        return (w_ref[di, 0:1, :] * xl + w_ref[di, 1:2, :] * x + w_ref[di, 2:3, :] * xr)

    y = row(1)
    if t > width:
        pad = jnp.zeros((width, x.shape[1]), F32)
        y = y + jnp.concatenate([pad, row(0)[:t - width]], axis=0)
        y = y + jnp.concatenate([row(2)[width:], pad], axis=0)
    y = _silu(y + b_ref[...])
    qscale = jnp.where(pl.program_id(1) < q_tiles, ML_QK ** -0.5, 1.0)
    o_ref[0] = (y * qscale).astype(BF16)


def _conv(x, w, b, width):
    bsz, t, ch = x.shape
    return pl.pallas_call(
        functools.partial(_conv_kernel, width=width, q_tiles=ML_HEADS * ML_QK // LANE),
        grid=(bsz, ch // LANE),
        in_specs=[pl.BlockSpec((1, t, LANE), lambda b_, c: (b_, 0, c)),
                  pl.BlockSpec((3, 3, LANE), lambda b_, c: (0, 0, c)),
                  pl.BlockSpec((1, LANE), lambda b_, c: (0, c))],
        out_specs=pl.BlockSpec((1, t, LANE), lambda b_, c: (b_, 0, c)),
        out_shape=jax.ShapeDtypeStruct((bsz, t, ch), BF16),
        compiler_params=_cparams(("parallel", "parallel")),
        name="conv",
    )(x, w, b)


def _positions():
    t = np.arange(CHUNK)
    return [t, CHUNK - 1 - t]


def _hgrn2_masks():
    ds, vs, ws = [], [], []
    for p in _positions():
        le = (p[None, :] <= p[:, None]).astype(np.float32)
        order = [p[SUB * j] // SUB for j in range(N_SUB)]
        ref = np.zeros((32, CHUNK), np.float32)
        valid = np.zeros((N_SUB * N_SUB, 1), np.float32)
        for j in range(N_SUB):
            ref[j] = p <= SUB * order[j] + SUB // 2 - 1
            ref[8 + j] = p <= SUB * order[j] + SUB - 1
        ref[N_SUB] = 1.0
        for j in range(N_SUB):
            for i in range(N_SUB):
                if order[i] < order[j]:
                    ref[16 + N_SUB * j + i] = ref[j] - ref[8 + i]
                    valid[N_SUB * j + i] = 1.0
        ds.append(np.concatenate([le, ref], axis=0))
        vs.append(np.broadcast_to(valid, (N_SUB * N_SUB, HG_HEADS * HG_DIM)))
        tok_blk = np.arange(CHUNK)[:, None] // SUB
        ws.append(np.concatenate([(tok_blk == j) * le for j in range(N_SUB)], axis=1))
    return (np.stack(ds).astype(np.float32), np.stack(vs).astype(np.float32),
            np.stack(ws).astype(np.float32))


def _mlstm_masks():
    le2s, let2s = [], []
    for p in _positions():
        le = (p[None, :] <= p[:, None]).astype(np.float32)
        le2s.append(np.concatenate([le, le], axis=1))
        let2 = np.zeros((2 * CHUNK, 2 * CHUNK), np.float32)
        let2[:CHUNK, :CHUNK] = le.T
        let2[CHUNK:, CHUNK:] = le.T
        let2s.append(let2)
    return np.stack(le2s), np.stack(let2s)


def _run_pipelined(n_groups, group, stages_of, step_of):
    queue = []
    for g in range(n_groups):
        for _ in stages_of(g):
            if queue:
                queue.pop(0)()
        while queue:
            queue.pop(0)()
        queue = [functools.partial(step_of, g * group + i) for i in range(group)]
    for step in queue:
        step()


def _chunk_rows(c, nc, direction):
    cc = jnp.where(direction == 0, c, nc - 1 - c)
    return pl.multiple_of(cc * CHUNK, CHUNK)


def _hgrn2_kernel(q_ref, v_ref, lf_ref, d_ref, pv_ref, w_ref, s0_ref, o_ref, st_ref,
                  u_ref, qin_ref, g_ref, oi_ref, bx_ref, kx_ref, vx_ref, *, nc):
    direction = pl.program_id(1)

    @pl.when(pl.program_id(2) == 0)
    def _():
        st_ref[...] = s0_ref[...]

    dmat = d_ref[0].astype(BF16)
    pair_valid = pv_ref[0]
    wmask = w_ref[0] > 0.0
    L = CHUNK
    hw = q_ref.shape[-1]
    upper_rows = lax.broadcasted_iota(jnp.int32, (L, 2 * L), 0) < L // 2

    group = min(GROUP, nc)

    def intra(grp):
        cs = [grp * group + i for i in range(group)]
        rows = [pl.ds(_chunk_rows(c, nc, direction), L) for c in cs]
        items = [(i, h) for i in range(group) for h in range(HG_HEADS)]
        lanes = [slice(h * HG_DIM, (h + 1) * HG_DIM) for h in range(HG_HEADS)]
        lfs = [lf_ref[0, 0, r, :] for r in rows]
        parts = [_split2(lf) for lf in lfs]
        e_all = sum(_dot(dmat, jnp.concatenate([parts[i][n] for i in range(group)], axis=1))
                    for n in range(2))
        yield
        q_mids, k_mids, k_outs, vbs, spans = [], [], [], [], []
        for i in range(group):
            e = e_all[:, i * hw:(i + 1) * hw]
            q = q_ref[0, rows[i], :]
            k = 1.0 - jnp.exp2(lfs[i])
            b = e[0:L]
            b_tot = e[L + N_SUB:L + N_SUB + 1]
            sub_rows = lambda r0: jnp.concatenate(
                [jnp.broadcast_to(e[r0 + j:r0 + j + 1], (SUB, hw)) for j in range(N_SUB)], axis=0)
            mid_rows = sub_rows(L)
            end_rows = sub_rows(L + 8)
            to_mid = b - mid_rows
            spans.append(jnp.max(jnp.abs(to_mid)))
            q_mids.append((q * jnp.exp2(jnp.minimum(to_mid, EXP2_CLAMP))).astype(BF16))
            k_diag = k * jnp.exp2(jnp.minimum(-to_mid, EXP2_CLAMP))
            k_end = k * jnp.exp2(end_rows - b)
            cross = jnp.exp2(e[L + 16:L + 32]) * pair_valid
            slabs = []
            for j in range(N_SUB):
                for ib in range(N_SUB):
                    rs = slice(ib * SUB, (ib + 1) * SUB)
                    row = N_SUB * j + ib
                    slabs.append(k_diag[rs] if ib == j else k_end[rs] * cross[row:row + 1])
            k_mids.append(jnp.concatenate(slabs, axis=0).astype(BF16))
            k_outs.append((k * jnp.exp2(b_tot - b)).astype(BF16))
            vbs.append(v_ref[0, rows[i], :])
            qin_ref[cs[i]] = (q * jnp.exp2(b)).astype(BF16)
            g_ref[cs[i]] = jnp.broadcast_to(jnp.exp2(b_tot), (8, hw))
            yield
        scores = [_dot_nt(q_mids[i][:, lanes[h]], k_mids[i][:, lanes[h]]) for i, h in items]
        yield
        a_s = [jnp.where(wmask, s, 0.0) for s in scores]
        a_s = [jnp.where(upper_rows, a[:, :2 * L], a[:, 2 * L:]).astype(BF16) for a in a_s]
        v2s = [jnp.concatenate([vb, vb], axis=0) for vb in vbs]
        outs = [_dot(a_s[n], v2s[i][:, lanes[h]]) for n, (i, h) in enumerate(items)]
        yield
        for i in range(group):
            oi_ref[cs[i]] = jnp.concatenate(outs[i * HG_HEADS:(i + 1) * HG_HEADS], axis=1)
        yield
        for i, h in items:
            u_ref[cs[i], h] = _dot_tn(vbs[i][:, lanes[h]], k_outs[i][:, lanes[h]])

        @pl.when(functools.reduce(jnp.maximum, spans) > EXP2_CLAMP)
        def _():
            for i in range(group):
                exact_intra(cs[i], rows[i])
        yield

    def exact_intra(c, rows):
        lf = lf_ref[0, 0, rows, :]
        bx_ref[...] = sum(_dot(dmat[0:L], x) for x in _split3(lf))
        kx_ref[...] = 1.0 - jnp.exp2(lf)
        vx_ref[...] = v_ref[0, rows, :].astype(F32)
        q = q_ref[0, rows, :]
        tok = lax.broadcasted_iota(jnp.int32, (L, 1), 0)
        pos = jnp.where(direction == 0, tok, L - 1 - tok)

        def add_source(s, acc):
            b_s = bx_ref[pl.ds(s, 1), :]
            z = q * kx_ref[pl.ds(s, 1), :] * jnp.exp2(jnp.minimum(bx_ref[...] - b_s, 0.0))
            v_s = vx_ref[pl.ds(s, 1), :]
            seen = pos >= jnp.where(direction == 0, s, L - 1 - s)
            cols = [jnp.where(seen, jnp.sum(z[:, ln], axis=1, keepdims=True), 0.0) * v_s[:, ln]
                    for ln in (slice(h * HG_DIM, (h + 1) * HG_DIM) for h in range(HG_HEADS))]
            return acc + jnp.concatenate(cols, axis=1)

        oi_ref[c] = lax.fori_loop(0, L, add_source, jnp.zeros((L, hw), F32))

    def carry_state(c):
        rows = pl.ds(_chunk_rows(c, nc, direction), L)
        q_in = qin_ref[c]
        g = g_ref[c][0:1]
        outs, states = [], []
        for h in range(HG_HEADS):
            ln = slice(h * HG_DIM, (h + 1) * HG_DIM)
            st = st_ref[0, 0, h]
            outs.append(_dot_nt(q_in[:, ln], st.astype(BF16)))
            states.append(g[:, ln] * st + u_ref[c, h])
        o_ref[0, 0, rows, :] = (oi_ref[c] + jnp.concatenate(outs, axis=1)).astype(BF16)
        for h in range(HG_HEADS):
            st_ref[0, 0, h] = states[h]

    _run_pipelined(nc // group, group, intra, carry_state)


def _hgrn2(q, v, lf, s0, dmat, pair_valid, wmask):
    bsz, t, hw = q.shape
    nc = min(HG_CHUNKS, t // CHUNK)
    tb = nc * CHUNK
    nblk = t // tb

    def blk(j, d):
        return jnp.where(d == 0, j, nblk - 1 - j)

    st_spec = pl.BlockSpec((1, 1) + s0.shape[2:], lambda b, d, j: (b, d, 0, 0, 0))
    return pl.pallas_call(
        functools.partial(_hgrn2_kernel, nc=nc),
        grid=(bsz, 2, nblk),
        in_specs=[pl.BlockSpec((1, tb, hw), lambda b, d, j: (b, blk(j, d), 0)),
                  pl.BlockSpec((1, tb, hw), lambda b, d, j: (b, blk(j, d), 0)),
                  pl.BlockSpec((1, 1, tb, hw), lambda b, d, j: (b, d, blk(j, d), 0)),
                  pl.BlockSpec((1,) + dmat.shape[1:], lambda b, d, j: (d, 0, 0)),
                  pl.BlockSpec((1,) + pair_valid.shape[1:], lambda b, d, j: (d, 0, 0)),
                  pl.BlockSpec((1,) + wmask.shape[1:], lambda b, d, j: (d, 0, 0)),
                  st_spec],
        out_specs=[pl.BlockSpec((1, 1, tb, hw), lambda b, d, j: (b, d, blk(j, d), 0)), st_spec],
        out_shape=[jax.ShapeDtypeStruct((bsz, 2, t, hw), BF16),
                   jax.ShapeDtypeStruct(s0.shape, F32)],
        scratch_shapes=[pltpu.VMEM((nc,) + s0.shape[2:], F32),
                        pltpu.VMEM((nc, CHUNK, hw), BF16),
                        pltpu.VMEM((nc, 8, hw), F32),
                        pltpu.VMEM((nc, CHUNK, hw), F32),
                        pltpu.VMEM((CHUNK, hw), F32),
                        pltpu.VMEM((CHUNK, hw), F32),
                        pltpu.VMEM((CHUNK, hw), F32)],
        compiler_params=_cparams(("parallel", "parallel", "arbitrary")),
        name="hgrn2_scan",
    )(q, v, lf, dmat, pair_valid, wmask, s0)


def _mlstm_kernel(qk_ref, v_ref, g_ref, le2_ref, let2_ref, c0_ref, m0_ref,
                  o_ref, c_ref, mm_ref,
                  p_ref, u_ref, qs_ref, ct_ref, mloc_ref, sc_ref, *, nc):
    direction = pl.program_id(1)

    @pl.when(pl.program_id(2) == 0)
    def _():
        c_ref[...] = c0_ref[...]
        mm_ref[...] = m0_ref[...]

    le2 = le2_ref[0].astype(BF16)
    let2 = let2_ref[0].astype(BF16)
    causal = le2_ref[0] > 0.0
    L = CHUNK
    neg = -jnp.inf
    lane = lax.broadcasted_iota(jnp.int32, (1, LANE), 1)
    seg = [lane < L, lane >= L]
    lane_t = lax.broadcasted_iota(jnp.int32, (L, LANE), 1) < L
    eye = (lax.broadcasted_iota(jnp.int32, (LANE, LANE), 0)
           == lax.broadcasted_iota(jnp.int32, (LANE, LANE), 1))
    npair = ML_HEADS // 2
    ones_v = jnp.ones((L, ML_V), BF16)

    group = min(ML_GROUP, nc)

    def intra(grp):
        cs = [grp * group + i for i in range(group)]
        ccs = [jnp.where(direction == 0, c, nc - 1 - c) for c in cs]
        rows = [pl.ds(pl.multiple_of(cc * L, L), L) for cc in ccs]
        pairs = [(i, p) for i in range(group) for p in range(npair)]
        gs = [jnp.where(direction == 0, g_ref[0, cc][0:4], g_ref[0, cc][4:8]) for cc in ccs]
        pad = [jnp.zeros((-2 * group % 8, LANE), F32)] if 2 * group % 8 else []
        i2 = jnp.concatenate([g[0:2] for g in gs] + pad, axis=0)
        lf2 = _log_sigmoid(jnp.concatenate([g[2:4] for g in gs] + pad, axis=0))
        parts = _split3(lf2)
        r2 = i2 - sum(_dot(x, let2) for x in parts)
        g_a = jnp.max(jnp.where(seg[0], r2, neg), axis=1, keepdims=True)
        g_b = jnp.max(jnp.where(seg[1], r2, neg), axis=1, keepdims=True)
        e_w = jnp.exp(r2 - jnp.where(seg[0], g_a, g_b))
        bt_a = jnp.sum(jnp.where(seg[0], lf2, 0.0), axis=1, keepdims=True)
        bt_b = jnp.sum(jnp.where(seg[1], lf2, 0.0), axis=1, keepdims=True)
        yield

        def head_rows(x, i):
            zero = jnp.zeros_like(x[0:1])
            return jnp.concatenate(
                [jnp.broadcast_to(jnp.where(seg[h % 2], x[2 * i + h // 2:2 * i + h // 2 + 1], zero),
                                  (LANE, LANE)) for h in range(ML_HEADS)], axis=0)

        c_ts = [sum(_dot_nt(le2, head_rows(x, i)) for x in parts) for i in range(group)]
        for i in range(group):
            ct_ref[cs[i]] = c_ts[i]
            sc_ref[cs[i]] = jnp.concatenate(
                [jnp.broadcast_to(x[2 * i + p:2 * i + p + 1], (1, LANE))
                 for x2 in ((bt_a, bt_b), (g_a, g_b)) for p in range(npair) for x in x2], axis=0)
        yield
        qks = [qk_ref[0, r, :] for r in rows]
        vbs = [v_ref[0, r, :] for r in rows]
        q_ps = [qks[i][:, p * LANE:(p + 1) * LANE] for i, p in pairs]
        k_ps = [qks[i][:, (npair + p) * LANE:(npair + p + 1) * LANE] for i, p in pairs]
        zb = jnp.zeros((L, LANE), BF16)
        kbs = [jnp.concatenate([jnp.where(lane_t, k, zb), jnp.where(lane_t, zb, k)], axis=0) for k in k_ps]
        scores = [_dot_nt(q, kb) for q, kb in zip(q_ps, kbs)]
        yield
        ws = []
        for n, (i, p) in enumerate(pairs):
            ha, hb = 2 * p, 2 * p + 1
            t1 = jnp.where(lane_t, c_ts[i][:, ha * ML_V:(ha + 1) * ML_V], c_ts[i][:, hb * ML_V:(hb + 1) * ML_V])
            d_log = jnp.where(causal, t1 + r2[2 * i + p:2 * i + p + 1], neg)
            m_a = jnp.max(jnp.where(lane_t, d_log, neg), axis=1, keepdims=True)
            m_b = jnp.max(jnp.where(lane_t, neg, d_log), axis=1, keepdims=True)
            mloc_ref[cs[i], :, ha * ML_V:(ha + 1) * ML_V] = jnp.broadcast_to(m_a, (L, ML_V))
            mloc_ref[cs[i], :, hb * ML_V:(hb + 1) * ML_V] = jnp.broadcast_to(m_b, (L, ML_V))
            ws.append(jnp.exp(d_log - jnp.where(lane_t, m_a, m_b)))
        yield
        a_s = [(s_ * w).astype(BF16) for s_, w in zip(scores, ws)]
        a_stacks = [jnp.concatenate([jnp.where(lane_t, a, zb), jnp.where(lane_t, zb, a)], axis=0) for a in a_s]
        v_stacks = [jnp.concatenate(
            [jnp.concatenate([vbs[i][:, 2 * p * ML_V:(2 * p + 1) * ML_V], ones_v], axis=1),
             jnp.concatenate([vbs[i][:, (2 * p + 1) * ML_V:(2 * p + 2) * ML_V], ones_v], axis=1)], axis=0)
            for i, p in pairs]
        for n, (i, p) in enumerate(pairs):
            p_ref[cs[i], p] = _dot(a_stacks[n], v_stacks[n])
            qs_ref[cs[i], p] = jnp.concatenate(
                [jnp.where(lane_t, q_ps[n], zb), jnp.where(lane_t, zb, q_ps[n])], axis=0)
        yield
        diags = [jnp.where(eye, jnp.broadcast_to(e_w[2 * i + p:2 * i + p + 1], (LANE, LANE)), 0.0).astype(BF16)
                 for i, p in pairs]
        kb_ws = [_dot(d, kb).astype(BF16) for d, kb in zip(diags, kbs)]
        for n, (i, p) in enumerate(pairs):
            u_ref[cs[i], p] = _dot_tn(kb_ws[n], v_stacks[n])
        yield

    def carry_state(c):
        cc = jnp.where(direction == 0, c, nc - 1 - c)
        rows = pl.ds(pl.multiple_of(cc * L, L), L)
        m_all = mm_ref[0, 0]
        sc = sc_ref[c]
        c_t = ct_ref[c]
        m_loc = mloc_ref[c]
        outs, states, m_rows = [], [], []
        for p in range(ML_HEADS // 2):
            c_pair = c_ref[0, 0, p]
            inter = _dot(qs_ref[c, p], c_pair.astype(BF16))
            intra_p = p_ref[c, p]
            inc = u_ref[c, p]
            new_rows = []
            for hh in range(2):
                h = 2 * p + hh
                rs = slice(hh * L, (hh + 1) * L)
                ln = slice(h * ML_V, (h + 1) * ML_V)
                m_in = m_all[h:h + 1]
                m_t = jnp.maximum(m_loc[:, ln], c_t[:, ln] + m_in)
                alpha = jnp.exp(m_loc[:, ln] - m_t)
                beta = jnp.exp(c_t[:, ln] + m_in - m_t)
                num = alpha * intra_p[rs, :ML_V] + beta * inter[rs, :ML_V]
                den = alpha * intra_p[rs, ML_V:] + beta * inter[rs, ML_V:]
                outs.append(num / jnp.maximum(jnp.abs(den), jnp.exp(-m_t)))
                b_tot = sc[h:h + 1]
                m_new = b_tot + jnp.maximum(m_in, sc[ML_HEADS + h:ML_HEADS + h + 1])
                carry_w = jnp.exp(b_tot + m_in - m_new)
                inc_w = jnp.exp(b_tot + sc[ML_HEADS + h:ML_HEADS + h + 1] - m_new)
                cw2 = jnp.concatenate([carry_w, carry_w], axis=1)
                iw2 = jnp.concatenate([inc_w, inc_w], axis=1)
                new_rows.append(cw2 * c_pair[rs] + iw2 * inc[rs])
                m_rows.append(m_new)
            states.append(jnp.concatenate(new_rows, axis=0))
        o_ref[0, 0, rows, :] = jnp.concatenate(outs, axis=1).astype(BF16)
        for p in range(ML_HEADS // 2):
            c_ref[0, 0, p] = states[p]
        mm_ref[0, 0] = jnp.concatenate(m_rows + [m_all[ML_HEADS:]], axis=0)

    _run_pipelined(nc // group, group, intra, carry_state)


def _mlstm(qk, v, gates, c0, m0, le2, let2):
    bsz, t, hw = v.shape
    nc = min(ML_CHUNKS, t // CHUNK)
    tb = nc * CHUNK
    nblk = t // tb
    npair = ML_HEADS // 2

    def blk(j, d):
        return jnp.where(d == 0, j, nblk - 1 - j)

    c_spec = pl.BlockSpec((1, 1) + c0.shape[2:], lambda b, d, j: (b, d, 0, 0, 0))
    m_spec = pl.BlockSpec((1, 1) + m0.shape[2:], lambda b, d, j: (b, d, 0, 0))
    return pl.pallas_call(
        functools.partial(_mlstm_kernel, nc=nc),
        grid=(bsz, 2, nblk),
        in_specs=[pl.BlockSpec((1, tb, qk.shape[-1]), lambda b, d, j: (b, blk(j, d), 0)),
                  pl.BlockSpec((1, tb, hw), lambda b, d, j: (b, blk(j, d), 0)),
                  pl.BlockSpec((1, nc) + gates.shape[2:], lambda b, d, j: (b, blk(j, d), 0, 0)),
                  pl.BlockSpec((1,) + le2.shape[1:], lambda b, d, j: (d, 0, 0)),
                  pl.BlockSpec((1,) + let2.shape[1:], lambda b, d, j: (d, 0, 0)),
                  c_spec, m_spec],
        out_specs=[pl.BlockSpec((1, 1, tb, hw), lambda b, d, j: (b, d, blk(j, d), 0)), c_spec, m_spec],
        out_shape=[jax.ShapeDtypeStruct((bsz, 2, t, hw), BF16),
                   jax.ShapeDtypeStruct(c0.shape, F32),
                   jax.ShapeDtypeStruct(m0.shape, F32)],
        scratch_shapes=[pltpu.VMEM((nc, npair, 2 * CHUNK, 2 * ML_V), F32),
                        pltpu.VMEM((nc, npair, 2 * ML_QK, 2 * ML_V), F32),
                        pltpu.VMEM((nc, npair, 2 * CHUNK, LANE), BF16),
                        pltpu.VMEM((nc, CHUNK, hw), F32),
                        pltpu.VMEM((nc, CHUNK, hw), F32),
                        pltpu.VMEM((nc, 8, LANE), F32)],
        compiler_params=_cparams(("parallel", "parallel", "arbitrary")),
        name="mlstm_scan",
    )(qk, v, gates, le2, let2, c0, m0)


def _head_rms(x, heads):
    dh = x.shape[-1] // heads
    return jnp.concatenate([_rms(x[:, h * dh:(h + 1) * dh]) for h in range(heads)], axis=1)


def _mix_ffn_kernel(oh_ref, om_ref, hg_ref, mo_ref, x_ref, mod_ref, hnw_ref, mnw_ref, wo_ref,
                    nw_ref, w1_ref, w2_ref, fw_ref, o_ref, a_ref, x_scr, h_scr, *, d_ff, tf):
    mix_gate = mod_ref[0, 0:1, :]
    mods = [mod_ref[0, k:k + 1, :] for k in range(1, 4)]

    def stages(rs):
        def mixed(pr):
            oh = oh_ref[0, 0, pr, :].astype(F32) + oh_ref[0, 1, pr, :].astype(F32)
            om = om_ref[0, 0, pr, :].astype(F32) + om_ref[0, 1, pr, :].astype(F32)
            hg_out = _head_rms(oh, HG_HEADS) * hnw_ref[...] * _silu(hg_ref[0, pr, :].astype(F32))
            ml_out = _sigmoid(mo_ref[0, pr, :].astype(F32)) * (_head_rms(om, ML_HEADS) * mnw_ref[...])
            merged = jnp.concatenate([hg_out, ml_out], axis=1).astype(BF16)
            return x_ref[0, pr, :] + mix_gate * _dot(merged, wo_ref[...])

        def put(y):
            o_ref[0, rs, :] = _rms(y) * fw_ref[...]

        return _Stages(_ffn_stages(mixed, put, mods, nw_ref, w1_ref, w2_ref, a_ref, x_scr, h_scr,
                                   rs, d_ff, tf), _ffn_lead(rs, d_ff, tf))

    _staggered([stages(rs) for rs in _sub_tiles(x_ref.shape[1])])


def _mix_ffn(oh, om, hg, mo, x, mods, hnw, mnw, w_out, nw, w1, w2, fw):
    bsz, t, d = x.shape
    hw = hg.shape[-1]
    d_ff = w2.shape[0]
    tm = min(TOKEN_TILE, t)
    tok = lambda b, i: (b, i, 0)
    tok2 = lambda b, i: (b, 0, i, 0)
    const = lambda b, i: (0, 0)
    resident = lambda a: pl.BlockSpec(a.shape, const, pipeline_mode=pl.Buffered(1))
    return pl.pallas_call(
        functools.partial(_mix_ffn_kernel, d_ff=d_ff, tf=FF_TILE),
        grid=(bsz, t // tm),
        in_specs=[pl.BlockSpec((1, 2, tm, hw), tok2), pl.BlockSpec((1, 2, tm, hw), tok2),
                  pl.BlockSpec((1, tm, hw), tok), pl.BlockSpec((1, tm, hw), tok),
                  pl.BlockSpec((1, tm, d), tok),
                  pl.BlockSpec((1, 4, d), lambda b, i: (b, 0, 0)),
                  pl.BlockSpec((1, hw), const), pl.BlockSpec((1, hw), const), resident(w_out),
                  pl.BlockSpec((1, d), const), resident(w1), resident(w2), pl.BlockSpec((1, d), const)],
        out_specs=pl.BlockSpec((1, tm, d), tok),
        out_shape=jax.ShapeDtypeStruct((bsz, t, d), F32),
        scratch_shapes=[pltpu.VMEM((tm, d_ff), BF16),
                        pltpu.VMEM((tm, d), F32),
                        pltpu.VMEM((tm, d), BF16)],
        compiler_params=_cparams(("parallel", "parallel")),
        name="mix_ffn_final",
    )(oh, om, hg, mo, x, mods, hnw, mnw, w_out, nw, w1, w2, fw)


def kernel(x, c, ctx, c_ctx, w_mod, b_mod, norm1_w, ffn1_w1, ffn1_w2, norm2_w, w_in, ml_gate_b,
           ml_conv_w, ml_conv_b, hg_lb_logits, hg_norm_w, ml_norm_w, w_out, norm3_w, ffn2_w1, ffn2_w2,
           final_norm_w):
    bsz, seq, d = x.shape
    n_ctx = ctx.shape[1]
    assert w_mod.shape[0] == 1, "single-layer kernel"
    assert seq % (GRID_W * 8) == 0 and n_ctx % CHUNK == 0
    hw = HG_HEADS * HG_DIM
    ng = ml_gate_b.shape[-1]

    rows = -(-(bsz + 1) // 8) * 8
    cvec = jnp.zeros((rows, d), F32).at[:bsz].set(c).at[bsz].set(c_ctx)
    mods = _modulation(cvec, w_mod[0], b_mod[0][None, :]).reshape(rows, N_MOD, d)
    lat_row = lambda b: b
    ctx_row = lambda b: bsz

    row = lambda a: a.reshape(1, -1)
    w1a, w2a = ffn1_w1[0], ffn1_w2[0]
    x1 = _ffn(x, mods[:, 0:3], lat_row, row(norm1_w[0]), w1a, w2a)
    s1 = _ffn(ctx, mods[:, 0:3], ctx_row, row(norm1_w[0]), w1a, w2a)

    w_in_b = w_in[0].T
    gate_order = np.arange(ng).reshape(2, 2, ML_HEADS // 2, 2).transpose(3, 0, 1, 2).reshape(-1)
    w_gate_t = w_in[0][:, 8 * hw:].T[gate_order].astype(BF16)
    gate_b = ml_gate_b[0][gate_order].reshape(ng, 1)
    dmat, pair_valid, wmask = (jnp.asarray(a) for a in _hgrn2_masks())
    le2, let2 = (jnp.asarray(a) for a in _mlstm_masks())
    conv_w, conv_b = ml_conv_w[0], row(ml_conv_b[0])
    npair = ML_HEADS // 2

    def mixer_scans(s, mod_row, width, s0, c0, m0):
        hq, hv, hg, hlf, mqk, mv, mo, gates = _inproj(
            s, mods[:, 3:5], mod_row, row(norm2_w[0]), w_in_b, w_gate_t, gate_b, hg_lb_logits)
        qk = _conv(mqk, conv_w, conv_b, width)
        oh, s_fin = _hgrn2(hq, hv, hlf, s0, dmat, pair_valid, wmask)
        om, c_fin, m_fin = _mlstm(qk, mv, gates, c0, m0, le2, let2)
        return oh, om, hg, mo, s_fin, c_fin, m_fin

    s0 = jnp.zeros((bsz, 2, HG_HEADS, HG_DIM, HG_DIM), F32)
    c0 = jnp.zeros((bsz, 2, npair, 2 * ML_QK, 2 * ML_V), F32)
    m0 = jnp.zeros((bsz, 2, 8, LANE), F32)
    _, _, _, _, s_ctx, c_ctx_state, m_ctx = mixer_scans(s1, ctx_row, n_ctx, s0, c0, m0)
    oh, om, hg, mo, _, _, _ = mixer_scans(x1, lat_row, GRID_W, s_ctx, c_ctx_state, m_ctx)

    return _mix_ffn(oh, om, hg, mo, x1, mods[:, 5:9], row(hg_norm_w[0]), row(ml_norm_w[0]),
                    w_out[0].astype(BF16), row(norm3_w[0]), ffn2_w1[0], ffn2_w2[0], row(final_norm_w))
```

```python
import functools

import numpy as np
import jax
import jax.numpy as jnp
from jax import lax
from jax.experimental import pallas as pl
from jax.experimental.pallas import tpu as pltpu

F32 = jnp.float32
BF16 = jnp.bfloat16

EPS = 1e-6
CHUNK = 64
GRID_W = 64
N_MOD = 9
HG_HEADS = 4
HG_DIM = 128
ML_HEADS = 4
ML_QK = 64
ML_V = 128
SUB = 16
N_SUB = CHUNK // SUB
GROUP = 4
ML_GROUP = 4
HG_CHUNKS = 8
ML_CHUNKS = 16
TOKEN_TILE = 512
ROW_SUB = 256
FF_TILE = 256
PIECE = 128
W_SLABS = 8
LOG2E = 1.4426950408889634
EXP2_CLAMP = 115.0
LANE = 128
VMEM_LIMIT = 56 * 1024 * 1024


def _cparams(sem):
    return pltpu.CompilerParams(dimension_semantics=sem, vmem_limit_bytes=VMEM_LIMIT)


def _dot(a, b):
    return jnp.dot(a, b, preferred_element_type=F32)


def _dot_nt(a, b):
    return lax.dot_general(a, b, (((1,), (1,)), ((), ())), preferred_element_type=F32)


def _dot_tn(a, b):
    return lax.dot_general(a, b, (((0,), (0,)), ((), ())), preferred_element_type=F32)


def _sigmoid(x):
    return 1.0 / (1.0 + jnp.exp(-x))


def _silu(x):
    return x * _sigmoid(x)


def _log_sigmoid(x):
    return jnp.minimum(x, 0.0) - jnp.log(1.0 + jnp.exp(-jnp.abs(x)))


def _split3(x):
    hi = x.astype(BF16)
    r = x - hi.astype(F32)
    mid = r.astype(BF16)
    lo = (r - mid.astype(F32)).astype(BF16)
    return hi, mid, lo


def _split2(x):
    hi = x.astype(BF16)
    return hi, (x - hi.astype(F32)).astype(BF16)


def _rms(x):
    return x * lax.rsqrt(jnp.mean(x * x, axis=-1, keepdims=True) + EPS)


def _mod_kernel(c_ref, w_ref, b_ref, o_ref):
    a = _silu(c_ref[...])
    a_hi = a.astype(BF16)
    a_lo = (a - a_hi.astype(F32)).astype(BF16)
    w = w_ref[...]
    w_hi = w.astype(BF16)
    w_lo = (w - w_hi.astype(F32)).astype(BF16)
    o_ref[...] = _dot(a_hi, w_hi) + _dot(a_hi, w_lo) + _dot(a_lo, w_hi) + b_ref[...]


def _modulation(cvec, w_mod, b_mod):
    rows, d = cvec.shape
    n = w_mod.shape[1]
    tn = 1024
    return pl.pallas_call(
        _mod_kernel,
        grid=(n // tn,),
        in_specs=[pl.BlockSpec((rows, d), lambda j: (0, 0)),
                  pl.BlockSpec((d, tn), lambda j: (0, j)),
                  pl.BlockSpec((1, tn), lambda j: (0, j))],
        out_specs=pl.BlockSpec((rows, tn), lambda j: (0, j)),
        out_shape=jax.ShapeDtypeStruct((rows, n), F32),
        compiler_params=_cparams(("arbitrary",)),
        name="modulation",
    )(cvec, w_mod, b_mod)


def _staggered(tiles):
    for _ in range(tiles[0].lead):
        next(tiles[0].gen)
    live = [t.gen for t in tiles]
    while live:
        live = [g for g in live if next(g, None) is not None]


class _Stages:
    def __init__(self, gen, lead):
        self.gen, self.lead = gen, lead


def _ffn_stages(get_x, put_y, mods, nw_ref, w1_ref, w2_ref, a_ref, x_scr, h_scr, rs, d_ff, tf):
    shift, scale, gate = mods
    for p0 in range(rs.start, rs.stop, PIECE):
        pr = slice(p0, min(p0 + PIECE, rs.stop))
        x = get_x(pr)
        x_scr[pr, :] = x
        h_scr[pr, :] = (_rms(x) * nw_ref[...] * (1.0 + scale) + shift).astype(BF16)
        yield True
    h = h_scr[rs, :]
    for c in range(d_ff // tf):
        g = _dot(h, w1_ref[:, c * tf:(c + 1) * tf])
        u = _dot(h, w1_ref[:, d_ff + c * tf:d_ff + (c + 1) * tf])
        a_ref[rs, c * tf:(c + 1) * tf] = (_silu(g) * u).astype(BF16)
        yield True
    y = x_scr[rs, :] + 0.5 * gate * _dot(a_ref[rs, :], w2_ref[...])
    yield True
    put_y(y)
    yield True


def _sub_tiles(rows):
    sub = min(ROW_SUB, rows)
    return [slice(r0, r0 + sub) for r0 in range(0, rows, sub)]


def _ffn_lead(rs, d_ff, tf):
    return -(-(rs.stop - rs.start) // PIECE) + d_ff // tf // 2


def _load_as_bf16(src_hbm, dst, stage, sem):
    rows = stage.shape[1]
    n = src_hbm.shape[0] // rows

    def slab_copy(i):
        return pltpu.make_async_copy(src_hbm.at[pl.ds(i * rows, rows), :], stage.at[i % 2], sem.at[i % 2])

    slab_copy(0).start()
    for i in range(n):
        if i + 1 < n:
            slab_copy(i + 1).start()
        slab_copy(i).wait()
        dst[pl.ds(i * rows, rows), :] = stage[i % 2].astype(BF16)


def _load_ffn_weights(w1_hbm, w2_hbm, w1_ref, w2_ref, st1, st2, sem):
    @pl.when((pl.program_id(0) == 0) & (pl.program_id(1) == 0))
    def _():
        _load_as_bf16(w1_hbm, w1_ref, st1, sem.at[0])
        _load_as_bf16(w2_hbm, w2_ref, st2, sem.at[1])


def _ffn_weight_scratch(w1, w2):
    return [pltpu.VMEM(w1.shape, BF16), pltpu.VMEM(w2.shape, BF16),
            pltpu.VMEM((2, w1.shape[0] // W_SLABS, w1.shape[1]), F32),
            pltpu.VMEM((2, w2.shape[0] // W_SLABS, w2.shape[1]), F32),
            pltpu.SemaphoreType.DMA((2, 2))]


def _ffn_kernel(x_ref, mod_ref, nw_ref, w1_hbm, w2_hbm, o_ref, a_ref, x_scr, h_scr,
                w1_ref, w2_ref, st1, st2, sem, *, d_ff, tf):
    _load_ffn_weights(w1_hbm, w2_hbm, w1_ref, w2_ref, st1, st2, sem)
    mods = [mod_ref[0, k:k + 1, :] for k in range(3)]

    def stages(rs):
        def put(y):
            o_ref[0, rs, :] = y
        return _Stages(_ffn_stages(lambda pr: x_ref[0, pr, :], put, mods, nw_ref, w1_ref, w2_ref, a_ref,
                                   x_scr, h_scr, rs, d_ff, tf), _ffn_lead(rs, d_ff, tf))

    _staggered([stages(rs) for rs in _sub_tiles(x_ref.shape[1])])


def _ffn(s, mods, mod_row, nw, w1, w2):
    bsz, t, d = s.shape
    d_ff = w2.shape[0]
    tm = min(TOKEN_TILE, t)
    const = lambda b, i: (0, 0)
    return pl.pallas_call(
        functools.partial(_ffn_kernel, d_ff=d_ff, tf=FF_TILE),
        grid=(bsz, t // tm),
        in_specs=[pl.BlockSpec((1, tm, d), lambda b, i: (b, i, 0)),
                  pl.BlockSpec((1, 3, d), lambda b, i: (mod_row(b), 0, 0)),
                  pl.BlockSpec((1, d), const),
                  pl.BlockSpec(memory_space=pl.ANY), pl.BlockSpec(memory_space=pl.ANY)],
        out_specs=pl.BlockSpec((1, tm, d), lambda b, i: (b, i, 0)),
        out_shape=jax.ShapeDtypeStruct((bsz, t, d), F32),
        scratch_shapes=[pltpu.VMEM((tm, d_ff), BF16),
                        pltpu.VMEM((tm, d), F32),
                        pltpu.VMEM((tm, d), BF16)]
        + _ffn_weight_scratch(w1, w2),
        compiler_params=_cparams(("arbitrary", "arbitrary")),
        name="ffn",
    )(s, mods, nw, w1, w2)


def _inproj_kernel(x_ref, mod_ref, nw_ref, w_ref, wg_ref, gb_ref, lbl_ref,
                   hq_ref, hv_ref, hg_ref, hlf_ref, mqk_ref, mv_ref, mo_ref, gc_ref, *, hw):
    shift = mod_ref[0, 0:1, :]
    scale = mod_ref[0, 1:2, :]
    lbl = lbl_ref[...]
    e = jnp.exp(lbl - jnp.max(lbl, axis=0, keepdims=True))
    lb = e[0] / jnp.sum(e, axis=0)
    first_half = lax.broadcasted_iota(jnp.int32, (8, LANE), 1) < CHUNK
    def stages(rs):
        h = (_rms(x_ref[0, rs, :]) * nw_ref[...] * (1.0 + scale) + shift).astype(BF16)
        yield True

        def proj(k):
            return _dot_nt(h, w_ref[k * hw:(k + 1) * hw, :].astype(BF16))

        hq_ref[0, rs, :] = _silu(proj(0)) * (HG_DIM ** -0.5)
        yield True
        hv_ref[0, rs, :] = proj(1).astype(BF16)
        yield True
        hg_ref[0, rs, :] = proj(2).astype(BF16)
        yield True
        for d in range(2):
            lbd = lb[d:d + 1, :]
            hlf_ref[0, d, rs, :] = jnp.log(lbd + (1.0 - lbd) * _sigmoid(proj(3 + d))) * LOG2E
            yield True
        mqk_ref[0, rs, :] = proj(5).astype(BF16)
        yield True
        mv_ref[0, rs, :] = proj(6).astype(BF16)
        yield True
        mo_ref[0, rs, :] = proj(7).astype(BF16)
        gt = _dot_nt(wg_ref[...], h) + gb_ref[...]
        for m in range((rs.stop - rs.start) // LANE):
            top = gt[0:8, m * LANE:(m + 1) * LANE]
            bot = gt[8:16, m * LANE:(m + 1) * LANE]
            c0 = rs.start // CHUNK + 2 * m
            gc_ref[0, c0] = jnp.where(first_half, top, pltpu.roll(bot, CHUNK, 1))
            gc_ref[0, c0 + 1] = jnp.where(first_half, pltpu.roll(top, CHUNK, 1), bot)
        yield True

    _staggered([_Stages(stages(rs), 8) for rs in _sub_tiles(x_ref.shape[1])])


def _inproj(s, mods, mod_row, nw, w_in, w_gate_t, gate_b, lb_logits):
    bsz, t, d = s.shape
    hw = HG_HEADS * HG_DIM
    ng = gate_b.shape[0]
    tm = min(TOKEN_TILE, t)
    tok4 = lambda b, i: (b, i, 0, 0)
    const = lambda b, i: (0, 0)
    tok = lambda b, i: (b, i, 0)
    tok2 = lambda b, i: (b, 0, i, 0)
    f = lambda dt, *shape: jax.ShapeDtypeStruct(shape, dt)
    return pl.pallas_call(
        functools.partial(_inproj_kernel, hw=hw),
        grid=(bsz, t // tm),
        in_specs=[pl.BlockSpec((1, tm, d), tok),
                  pl.BlockSpec((1, 2, d), lambda b, i: (mod_row(b), 0, 0)),
                  pl.BlockSpec((1, d), const),
                  pl.BlockSpec(w_in.shape, const, pipeline_mode=pl.Buffered(1)),
                  pl.BlockSpec(w_gate_t.shape, const),
                  pl.BlockSpec((ng, 1), const),
                  pl.BlockSpec(lb_logits.shape, lambda b, i: (0, 0, 0))],
        out_specs=[pl.BlockSpec((1, tm, hw), tok), pl.BlockSpec((1, tm, hw), tok),
                   pl.BlockSpec((1, tm, hw), tok),
                   pl.BlockSpec((1, 2, tm, hw), tok2),
                   pl.BlockSpec((1, tm, hw), tok), pl.BlockSpec((1, tm, hw), tok),
                   pl.BlockSpec((1, tm, hw), tok), pl.BlockSpec((1, tm // CHUNK, ng // 2, LANE), tok4)],
        out_shape=[f(F32, bsz, t, hw), f(BF16, bsz, t, hw), f(BF16, bsz, t, hw),
                   f(F32, bsz, 2, t, hw),
                   f(BF16, bsz, t, hw), f(BF16, bsz, t, hw), f(BF16, bsz, t, hw),
                   f(F32, bsz, t // CHUNK, ng // 2, LANE)],
        compiler_params=_cparams(("parallel", "parallel")),
        name="inproj",
    )(s, mods, nw, w_in, w_gate_t, gate_b, lb_logits)


def _conv_kernel(x_ref, w_ref, b_ref, o_ref, *, width, q_tiles):
    x = x_ref[0].astype(F32)
    t = x.shape[0]
    col = lax.broadcasted_iota(jnp.int32, x.shape, 0) % width
    xl = jnp.where(col == 0, 0.0, pltpu.roll(x, 1, 0))
    xr = jnp.where(col == width - 1, 0.0, pltpu.roll(x, t - 1, 0))

    def row(di):
        return (w_ref[di, 0:1, :] * xl + w_ref[di, 1:2, :] * x + w_ref[di, 2:3, :] * xr)

    y = row(1)
    if t > width:
        pad = jnp.zeros((width, x.shape[1]), F32)
        y = y + jnp.concatenate([pad, row(0)[:t - width]], axis=0)
        y = y + jnp.concatenate([row(2)[width:], pad], axis=0)
    y = _silu(y + b_ref[...])
    qscale = jnp.where(pl.program_id(1) < q_tiles, ML_QK ** -0.5, 1.0)
    o_ref[0] = (y * qscale).astype(BF16)


def _conv(x, w, b, width):
    bsz, t, ch = x.shape
    return pl.pallas_call(
        functools.partial(_conv_kernel, width=width, q_tiles=ML_HEADS * ML_QK // LANE),
        grid=(bsz, ch // LANE),
        in_specs=[pl.BlockSpec((1, t, LANE), lambda b_, c: (b_, 0, c)),
                  pl.BlockSpec((3, 3, LANE), lambda b_, c: (0, 0, c)),
                  pl.BlockSpec((1, LANE), lambda b_, c: (0, c))],
        out_specs=pl.BlockSpec((1, t, LANE), lambda b_, c: (b_, 0, c)),
        out_shape=jax.ShapeDtypeStruct((bsz, t, ch), BF16),
        compiler_params=_cparams(("parallel", "parallel")),
        name="conv",
    )(x, w, b)


def _positions():
    t = np.arange(CHUNK)
    return [t, CHUNK - 1 - t]


def _hgrn2_masks():
    ds, vs, ws = [], [], []
    for p in _positions():
        le = (p[None, :] <= p[:, None]).astype(np.float32)
        order = [p[SUB * j] // SUB for j in range(N_SUB)]
        ref = np.zeros((32, CHUNK), np.float32)
        valid = np.zeros((N_SUB * N_SUB, 1), np.float32)
        for j in range(N_SUB):
            ref[j] = p <= SUB * order[j] + SUB // 2 - 1
            ref[8 + j] = p <= SUB * order[j] + SUB - 1
        ref[N_SUB] = 1.0
        for j in range(N_SUB):
            for i in range(N_SUB):
                if order[i] < order[j]:
                    ref[16 + N_SUB * j + i] = ref[j] - ref[8 + i]
                    valid[N_SUB * j + i] = 1.0
        ds.append(np.concatenate([le, ref], axis=0))
        vs.append(np.broadcast_to(valid, (N_SUB * N_SUB, HG_HEADS * HG_DIM)))
        tok_blk = np.arange(CHUNK)[:, None] // SUB
        ws.append(np.concatenate([(tok_blk == j) * le for j in range(N_SUB)], axis=1))
    return (np.stack(ds).astype(np.float32), np.stack(vs).astype(np.float32),
            np.stack(ws).astype(np.float32))


def _mlstm_masks():
    le2s, let2s = [], []
    for p in _positions():
        le = (p[None, :] <= p[:, None]).astype(np.float32)
        le2s.append(np.concatenate([le, le], axis=1))
        let2 = np.zeros((2 * CHUNK, 2 * CHUNK), np.float32)
        let2[:CHUNK, :CHUNK] = le.T
        let2[CHUNK:, CHUNK:] = le.T
        let2s.append(let2)
    return np.stack(le2s), np.stack(let2s)


def _run_pipelined(n_groups, group, stages_of, step_of):
    queue = []
    for g in range(n_groups):
        for _ in stages_of(g):
            if queue:
                queue.pop(0)()
        while queue:
            queue.pop(0)()
        queue = [functools.partial(step_of, g * group + i) for i in range(group)]
    for step in queue:
        step()


def _chunk_rows(c, nc, direction):
    cc = jnp.where(direction == 0, c, nc - 1 - c)
    return pl.multiple_of(cc * CHUNK, CHUNK)


def _hgrn2_kernel(q_ref, v_ref, lf_ref, d_ref, pv_ref, w_ref, s0_ref, o_ref, st_ref,
                  u_ref, qin_ref, g_ref, oi_ref, bx_ref, kx_ref, vx_ref, *, nc):
    direction = pl.program_id(1)

    @pl.when(pl.program_id(2) == 0)
    def _():
        st_ref[...] = s0_ref[...]

    dmat = d_ref[0].astype(BF16)
    pair_valid = pv_ref[0]
    wmask = w_ref[0] > 0.0
    L = CHUNK
    hw = q_ref.shape[-1]
    upper_rows = lax.broadcasted_iota(jnp.int32, (L, 2 * L), 0) < L // 2

    group = min(GROUP, nc)

    def intra(grp):
        cs = [grp * group + i for i in range(group)]
        rows = [pl.ds(_chunk_rows(c, nc, direction), L) for c in cs]
        items = [(i, h) for i in range(group) for h in range(HG_HEADS)]
        lanes = [slice(h * HG_DIM, (h + 1) * HG_DIM) for h in range(HG_HEADS)]
        lfs = [lf_ref[0, 0, r, :] for r in rows]
        parts = [_split2(lf) for lf in lfs]
        e_all = sum(_dot(dmat, jnp.concatenate([parts[i][n] for i in range(group)], axis=1))
                    for n in range(2))
        yield
        q_mids, k_mids, k_outs, vbs, spans = [], [], [], [], []
        for i in range(group):
            e = e_all[:, i * hw:(i + 1) * hw]
            q = q_ref[0, rows[i], :]
            k = 1.0 - jnp.exp2(lfs[i])
            b = e[0:L]
            b_tot = e[L + N_SUB:L + N_SUB + 1]
            sub_rows = lambda r0: jnp.concatenate(
                [jnp.broadcast_to(e[r0 + j:r0 + j + 1], (SUB, hw)) for j in range(N_SUB)], axis=0)
            mid_rows = sub_rows(L)
            end_rows = sub_rows(L + 8)
            to_mid = b - mid_rows
            spans.append(jnp.max(jnp.abs(to_mid)))
            q_mids.append((q * jnp.exp2(jnp.minimum(to_mid, EXP2_CLAMP))).astype(BF16))
            k_diag = k * jnp.exp2(jnp.minimum(-to_mid, EXP2_CLAMP))
            k_end = k * jnp.exp2(end_rows - b)
            cross = jnp.exp2(e[L + 16:L + 32]) * pair_valid
            slabs = []
            for j in range(N_SUB):
                for ib in range(N_SUB):
                    rs = slice(ib * SUB, (ib + 1) * SUB)
                    row = N_SUB * j + ib
                    slabs.append(k_diag[rs] if ib == j else k_end[rs] * cross[row:row + 1])
            k_mids.append(jnp.concatenate(slabs, axis=0).astype(BF16))
            k_outs.append((k * jnp.exp2(b_tot - b)).astype(BF16))
            vbs.append(v_ref[0, rows[i], :])
            qin_ref[cs[i]] = (q * jnp.exp2(b)).astype(BF16)
            g_ref[cs[i]] = jnp.broadcast_to(jnp.exp2(b_tot), (8, hw))
            yield
        scores = [_dot_nt(q_mids[i][:, lanes[h]], k_mids[i][:, lanes[h]]) for i, h in items]
        yield
        a_s = [jnp.where(wmask, s, 0.0) for s in scores]
        a_s = [jnp.where(upper_rows, a[:, :2 * L], a[:, 2 * L:]).astype(BF16) for a in a_s]
        v2s = [jnp.concatenate([vb, vb], axis=0) for vb in vbs]
        outs = [_dot(a_s[n], v2s[i][:, lanes[h]]) for n, (i, h) in enumerate(items)]
        yield
        for i in range(group):
            oi_ref[cs[i]] = jnp.concatenate(outs[i * HG_HEADS:(i + 1) * HG_HEADS], axis=1)
        yield
        for i, h in items:
            u_ref[cs[i], h] = _dot_tn(vbs[i][:, lanes[h]], k_outs[i][:, lanes[h]])

        @pl.when(functools.reduce(jnp.maximum, spans) > EXP2_CLAMP)
        def _():
            for i in range(group):
                exact_intra(cs[i], rows[i])
        yield

    def exact_intra(c, rows):
        lf = lf_ref[0, 0, rows, :]
        bx_ref[...] = sum(_dot(dmat[0:L], x) for x in _split3(lf))
        kx_ref[...] = 1.0 - jnp.exp2(lf)
        vx_ref[...] = v_ref[0, rows, :].astype(F32)
        q = q_ref[0, rows, :]
        tok = lax.broadcasted_iota(jnp.int32, (L, 1), 0)
        pos = jnp.where(direction == 0, tok, L - 1 - tok)

        def add_source(s, acc):
            b_s = bx_ref[pl.ds(s, 1), :]
            z = q * kx_ref[pl.ds(s, 1), :] * jnp.exp2(jnp.minimum(bx_ref[...] - b_s, 0.0))
            v_s = vx_ref[pl.ds(s, 1), :]
            seen = pos >= jnp.where(direction == 0, s, L - 1 - s)
            cols = [jnp.where(seen, jnp.sum(z[:, ln], axis=1, keepdims=True), 0.0) * v_s[:, ln]
                    for ln in (slice(h * HG_DIM, (h + 1) * HG_DIM) for h in range(HG_HEADS))]
            return acc + jnp.concatenate(cols, axis=1)

        oi_ref[c] = lax.fori_loop(0, L, add_source, jnp.zeros((L, hw), F32))

    def carry_state(c):
        rows = pl.ds(_chunk_rows(c, nc, direction), L)
        q_in = qin_ref[c]
        g = g_ref[c][0:1]
        outs, states = [], []
        for h in range(HG_HEADS):
            ln = slice(h * HG_DIM, (h + 1) * HG_DIM)
            st = st_ref[0, 0, h]
            outs.append(_dot_nt(q_in[:, ln], st.astype(BF16)))
            states.append(g[:, ln] * st + u_ref[c, h])
        o_ref[0, 0, rows, :] = (oi_ref[c] + jnp.concatenate(outs, axis=1)).astype(BF16)
        for h in range(HG_HEADS):
            st_ref[0, 0, h] = states[h]

    _run_pipelined(nc // group, group, intra, carry_state)


def _hgrn2(q, v, lf, s0, dmat, pair_valid, wmask):
    bsz, t, hw = q.shape
    nc = min(HG_CHUNKS, t // CHUNK)
    tb = nc * CHUNK
    nblk = t // tb

    def blk(j, d):
        return jnp.where(d == 0, j, nblk - 1 - j)

    st_spec = pl.BlockSpec((1, 1) + s0.shape[2:], lambda b, d, j: (b, d, 0, 0, 0))
    return pl.pallas_call(
        functools.partial(_hgrn2_kernel, nc=nc),
        grid=(bsz, 2, nblk),
        in_specs=[pl.BlockSpec((1, tb, hw), lambda b, d, j: (b, blk(j, d), 0)),
                  pl.BlockSpec((1, tb, hw), lambda b, d, j: (b, blk(j, d), 0)),
                  pl.BlockSpec((1, 1, tb, hw), lambda b, d, j: (b, d, blk(j, d), 0)),
                  pl.BlockSpec((1,) + dmat.shape[1:], lambda b, d, j: (d, 0, 0)),
                  pl.BlockSpec((1,) + pair_valid.shape[1:], lambda b, d, j: (d, 0, 0)),
                  pl.BlockSpec((1,) + wmask.shape[1:], lambda b, d, j: (d, 0, 0)),
                  st_spec],
        out_specs=[pl.BlockSpec((1, 1, tb, hw), lambda b, d, j: (b, d, blk(j, d), 0)), st_spec],
        out_shape=[jax.ShapeDtypeStruct((bsz, 2, t, hw), BF16),
                   jax.ShapeDtypeStruct(s0.shape, F32)],
        scratch_shapes=[pltpu.VMEM((nc,) + s0.shape[2:], F32),
                        pltpu.VMEM((nc, CHUNK, hw), BF16),
                        pltpu.VMEM((nc, 8, hw), F32),
                        pltpu.VMEM((nc, CHUNK, hw), F32),
                        pltpu.VMEM((CHUNK, hw), F32),
                        pltpu.VMEM((CHUNK, hw), F32),
                        pltpu.VMEM((CHUNK, hw), F32)],
        compiler_params=_cparams(("parallel", "parallel", "arbitrary")),
        name="hgrn2_scan",
    )(q, v, lf, dmat, pair_valid, wmask, s0)


def _mlstm_kernel(qk_ref, v_ref, g_ref, le2_ref, let2_ref, c0_ref, m0_ref,
                  o_ref, c_ref, mm_ref,
                  p_ref, u_ref, qs_ref, ct_ref, mloc_ref, sc_ref, *, nc):
    direction = pl.program_id(1)

    @pl.when(pl.program_id(2) == 0)
    def _():
        c_ref[...] = c0_ref[...]
        mm_ref[...] = m0_ref[...]

    le2 = le2_ref[0].astype(BF16)
    let2 = let2_ref[0].astype(BF16)
    causal = le2_ref[0] > 0.0
    L = CHUNK
    neg = -jnp.inf
    lane = lax.broadcasted_iota(jnp.int32, (1, LANE), 1)
    seg = [lane < L, lane >= L]
    lane_t = lax.broadcasted_iota(jnp.int32, (L, LANE), 1) < L
    eye = (lax.broadcasted_iota(jnp.int32, (LANE, LANE), 0)
           == lax.broadcasted_iota(jnp.int32, (LANE, LANE), 1))
    npair = ML_HEADS // 2
    ones_v = jnp.ones((L, ML_V), BF16)

    group = min(ML_GROUP, nc)

    def intra(grp):
        cs = [grp * group + i for i in range(group)]
        ccs = [jnp.where(direction == 0, c, nc - 1 - c) for c in cs]
        rows = [pl.ds(pl.multiple_of(cc * L, L), L) for cc in ccs]
        pairs = [(i, p) for i in range(group) for p in range(npair)]
        gs = [jnp.where(direction == 0, g_ref[0, cc][0:4], g_ref[0, cc][4:8]) for cc in ccs]
        pad = [jnp.zeros((-2 * group % 8, LANE), F32)] if 2 * group % 8 else []
        i2 = jnp.concatenate([g[0:2] for g in gs] + pad, axis=0)
        lf2 = _log_sigmoid(jnp.concatenate([g[2:4] for g in gs] + pad, axis=0))
        parts = _split3(lf2)
        r2 = i2 - sum(_dot(x, let2) for x in parts)
        g_a = jnp.max(jnp.where(seg[0], r2, neg), axis=1, keepdims=True)
        g_b = jnp.max(jnp.where(seg[1], r2, neg), axis=1, keepdims=True)
        e_w = jnp.exp(r2 - jnp.where(seg[0], g_a, g_b))
        bt_a = jnp.sum(jnp.where(seg[0], lf2, 0.0), axis=1, keepdims=True)
        bt_b = jnp.sum(jnp.where(seg[1], lf2, 0.0), axis=1, keepdims=True)
        yield

        def head_rows(x, i):
            zero = jnp.zeros_like(x[0:1])
            return jnp.concatenate(
                [jnp.broadcast_to(jnp.where(seg[h % 2], x[2 * i + h // 2:2 * i + h // 2 + 1], zero),
                                  (LANE, LANE)) for h in range(ML_HEADS)], axis=0)

        c_ts = [sum(_dot_nt(le2, head_rows(x, i)) for x in parts) for i in range(group)]
        for i in range(group):
            ct_ref[cs[i]] = c_ts[i]
            sc_ref[cs[i]] = jnp.concatenate(
                [jnp.broadcast_to(x[2 * i + p:2 * i + p + 1], (1, LANE))
                 for x2 in ((bt_a, bt_b), (g_a, g_b)) for p in range(npair) for x in x2], axis=0)
        yield
        qks = [qk_ref[0, r, :] for r in rows]
        vbs = [v_ref[0, r, :] for r in rows]
        q_ps = [qks[i][:, p * LANE:(p + 1) * LANE] for i, p in pairs]
        k_ps = [qks[i][:, (npair + p) * LANE:(npair + p + 1) * LANE] for i, p in pairs]
        zb = jnp.zeros((L, LANE), BF16)
        kbs = [jnp.concatenate([jnp.where(lane_t, k, zb), jnp.where(lane_t, zb, k)], axis=0) for k in k_ps]
        scores = [_dot_nt(q, kb) for q, kb in zip(q_ps, kbs)]
        yield
        ws = []
        for n, (i, p) in enumerate(pairs):
            ha, hb = 2 * p, 2 * p + 1
            t1 = jnp.where(lane_t, c_ts[i][:, ha * ML_V:(ha + 1) * ML_V], c_ts[i][:, hb * ML_V:(hb + 1) * ML_V])
            d_log = jnp.where(causal, t1 + r2[2 * i + p:2 * i + p + 1], neg)
            m_a = jnp.max(jnp.where(lane_t, d_log, neg), axis=1, keepdims=True)
            m_b = jnp.max(jnp.where(lane_t, neg, d_log), axis=1, keepdims=True)
            mloc_ref[cs[i], :, ha * ML_V:(ha + 1) * ML_V] = jnp.broadcast_to(m_a, (L, ML_V))
            mloc_ref[cs[i], :, hb * ML_V:(hb + 1) * ML_V] = jnp.broadcast_to(m_b, (L, ML_V))
            ws.append(jnp.exp(d_log - jnp.where(lane_t, m_a, m_b)))
        yield
        a_s = [(s_ * w).astype(BF16) for s_, w in zip(scores, ws)]
        a_stacks = [jnp.concatenate([jnp.where(lane_t, a, zb), jnp.where(lane_t, zb, a)], axis=0) for a in a_s]
        v_stacks = [jnp.concatenate(
            [jnp.concatenate([vbs[i][:, 2 * p * ML_V:(2 * p + 1) * ML_V], ones_v], axis=1),
             jnp.concatenate([vbs[i][:, (2 * p + 1) * ML_V:(2 * p + 2) * ML_V], ones_v], axis=1)], axis=0)
            for i, p in pairs]
        for n, (i, p) in enumerate(pairs):
            p_ref[cs[i], p] = _dot(a_stacks[n], v_stacks[n])
            qs_ref[cs[i], p] = jnp.concatenate(
                [jnp.where(lane_t, q_ps[n], zb), jnp.where(lane_t, zb, q_ps[n])], axis=0)
        yield
        diags = [jnp.where(eye, jnp.broadcast_to(e_w[2 * i + p:2 * i + p + 1], (LANE, LANE)), 0.0).astype(BF16)
                 for i, p in pairs]
        kb_ws = [_dot(d, kb).astype(BF16) for d, kb in zip(diags, kbs)]
        for n, (i, p) in enumerate(pairs):
            u_ref[cs[i], p] = _dot_tn(kb_ws[n], v_stacks[n])
        yield

    def carry_state(c):
        cc = jnp.where(direction == 0, c, nc - 1 - c)
        rows = pl.ds(pl.multiple_of(cc * L, L), L)
        m_all = mm_ref[0, 0]
        sc = sc_ref[c]
        c_t = ct_ref[c]
        m_loc = mloc_ref[c]
        outs, states, m_rows = [], [], []
        for p in range(ML_HEADS // 2):
            c_pair = c_ref[0, 0, p]
            inter = _dot(qs_ref[c, p], c_pair.astype(BF16))
            intra_p = p_ref[c, p]
            inc = u_ref[c, p]
            new_rows = []
            for hh in range(2):
                h = 2 * p + hh
                rs = slice(hh * L, (hh + 1) * L)
                ln = slice(h * ML_V, (h + 1) * ML_V)
                m_in = m_all[h:h + 1]
                m_t = jnp.maximum(m_loc[:, ln], c_t[:, ln] + m_in)
                alpha = jnp.exp(m_loc[:, ln] - m_t)
                beta = jnp.exp(c_t[:, ln] + m_in - m_t)
                num = alpha * intra_p[rs, :ML_V] + beta * inter[rs, :ML_V]
                den = alpha * intra_p[rs, ML_V:] + beta * inter[rs, ML_V:]
                outs.append(num / jnp.maximum(jnp.abs(den), jnp.exp(-m_t)))
                b_tot = sc[h:h + 1]
                m_new = b_tot + jnp.maximum(m_in, sc[ML_HEADS + h:ML_HEADS + h + 1])
                carry_w = jnp.exp(b_tot + m_in - m_new)
                inc_w = jnp.exp(b_tot + sc[ML_HEADS + h:ML_HEADS + h + 1] - m_new)
                cw2 = jnp.concatenate([carry_w, carry_w], axis=1)
                iw2 = jnp.concatenate([inc_w, inc_w], axis=1)
                new_rows.append(cw2 * c_pair[rs] + iw2 * inc[rs])
                m_rows.append(m_new)
            states.append(jnp.concatenate(new_rows, axis=0))
        o_ref[0, 0, rows, :] = jnp.concatenate(outs, axis=1).astype(BF16)
        for p in range(ML_HEADS // 2):
            c_ref[0, 0, p] = states[p]
        mm_ref[0, 0] = jnp.concatenate(m_rows + [m_all[ML_HEADS:]], axis=0)

    _run_pipelined(nc // group, group, intra, carry_state)


def _mlstm(qk, v, gates, c0, m0, le2, let2):
    bsz, t, hw = v.shape
    nc = min(ML_CHUNKS, t // CHUNK)
    tb = nc * CHUNK
    nblk = t // tb
    npair = ML_HEADS // 2

    def blk(j, d):
        return jnp.where(d == 0, j, nblk - 1 - j)

    c_spec = pl.BlockSpec((1, 1) + c0.shape[2:], lambda b, d, j: (b, d, 0, 0, 0))
    m_spec = pl.BlockSpec((1, 1) + m0.shape[2:], lambda b, d, j: (b, d, 0, 0))
    return pl.pallas_call(
        functools.partial(_mlstm_kernel, nc=nc),
        grid=(bsz, 2, nblk),
        in_specs=[pl.BlockSpec((1, tb, qk.shape[-1]), lambda b, d, j: (b, blk(j, d), 0)),
                  pl.BlockSpec((1, tb, hw), lambda b, d, j: (b, blk(j, d), 0)),
                  pl.BlockSpec((1, nc) + gates.shape[2:], lambda b, d, j: (b, blk(j, d), 0, 0)),
                  pl.BlockSpec((1,) + le2.shape[1:], lambda b, d, j: (d, 0, 0)),
                  pl.BlockSpec((1,) + let2.shape[1:], lambda b, d, j: (d, 0, 0)),
                  c_spec, m_spec],
        out_specs=[pl.BlockSpec((1, 1, tb, hw), lambda b, d, j: (b, d, blk(j, d), 0)), c_spec, m_spec],
        out_shape=[jax.ShapeDtypeStruct((bsz, 2, t, hw), BF16),
                   jax.ShapeDtypeStruct(c0.shape, F32),
                   jax.ShapeDtypeStruct(m0.shape, F32)],
        scratch_shapes=[pltpu.VMEM((nc, npair, 2 * CHUNK, 2 * ML_V), F32),
                        pltpu.VMEM((nc, npair, 2 * ML_QK, 2 * ML_V), F32),
                        pltpu.VMEM((nc, npair, 2 * CHUNK, LANE), BF16),
                        pltpu.VMEM((nc, CHUNK, hw), F32),
                        pltpu.VMEM((nc, CHUNK, hw), F32),
                        pltpu.VMEM((nc, 8, LANE), F32)],
        compiler_params=_cparams(("parallel", "parallel", "arbitrary")),
        name="mlstm_scan",
    )(qk, v, gates, le2, let2, c0, m0)


def _head_rms(x, heads):
    dh = x.shape[-1] // heads
    return jnp.concatenate([_rms(x[:, h * dh:(h + 1) * dh]) for h in range(heads)], axis=1)


def _mix_ffn_kernel(oh_ref, om_ref, hg_ref, mo_ref, x_ref, mod_ref, hnw_ref, mnw_ref, wo_ref,
                    nw_ref, w1_hbm, w2_hbm, fw_ref, o_ref, a_ref, x_scr, h_scr,
                    w1_ref, w2_ref, st1, st2, sem, *, d_ff, tf):
    _load_ffn_weights(w1_hbm, w2_hbm, w1_ref, w2_ref, st1, st2, sem)
    mix_gate = mod_ref[0, 0:1, :]
    mods = [mod_ref[0, k:k + 1, :] for k in range(1, 4)]

    def stages(rs):
        def mixed(pr):
            oh = oh_ref[0, 0, pr, :].astype(F32) + oh_ref[0, 1, pr, :].astype(F32)
            om = om_ref[0, 0, pr, :].astype(F32) + om_ref[0, 1, pr, :].astype(F32)
            hg_out = _head_rms(oh, HG_HEADS) * hnw_ref[...] * _silu(hg_ref[0, pr, :].astype(F32))
            ml_out = _sigmoid(mo_ref[0, pr, :].astype(F32)) * (_head_rms(om, ML_HEADS) * mnw_ref[...])
            merged = jnp.concatenate([hg_out, ml_out], axis=1).astype(BF16)
            return x_ref[0, pr, :] + mix_gate * _dot(merged, wo_ref[...])

        def put(y):
            o_ref[0, rs, :] = _rms(y) * fw_ref[...]

        return _Stages(_ffn_stages(mixed, put, mods, nw_ref, w1_ref, w2_ref, a_ref, x_scr, h_scr,
                                   rs, d_ff, tf), _ffn_lead(rs, d_ff, tf))

    _staggered([stages(rs) for rs in _sub_tiles(x_ref.shape[1])])


def _mix_ffn(oh, om, hg, mo, x, mods, hnw, mnw, w_out, nw, w1, w2, fw):
    bsz, t, d = x.shape
    hw = hg.shape[-1]
    d_ff = w2.shape[0]
    tm = min(TOKEN_TILE, t)
    tok = lambda b, i: (b, i, 0)
    tok2 = lambda b, i: (b, 0, i, 0)
    const = lambda b, i: (0, 0)
    resident = lambda a: pl.BlockSpec(a.shape, const, pipeline_mode=pl.Buffered(1))
    return pl.pallas_call(
        functools.partial(_mix_ffn_kernel, d_ff=d_ff, tf=FF_TILE),
        grid=(bsz, t // tm),
        in_specs=[pl.BlockSpec((1, 2, tm, hw), tok2), pl.BlockSpec((1, 2, tm, hw), tok2),
                  pl.BlockSpec((1, tm, hw), tok), pl.BlockSpec((1, tm, hw), tok),
                  pl.BlockSpec((1, tm, d), tok),
                  pl.BlockSpec((1, 4, d), lambda b, i: (b, 0, 0)),
                  pl.BlockSpec((1, hw), const), pl.BlockSpec((1, hw), const), resident(w_out),
                  pl.BlockSpec((1, d), const), pl.BlockSpec(memory_space=pl.ANY),
                  pl.BlockSpec(memory_space=pl.ANY), pl.BlockSpec((1, d), const)],
        out_specs=pl.BlockSpec((1, tm, d), tok),
        out_shape=jax.ShapeDtypeStruct((bsz, t, d), F32),
        scratch_shapes=[pltpu.VMEM((tm, d_ff), BF16),
                        pltpu.VMEM((tm, d), F32),
                        pltpu.VMEM((tm, d), BF16)]
        + _ffn_weight_scratch(w1, w2),
        compiler_params=_cparams(("arbitrary", "arbitrary")),
        name="mix_ffn_final",
    )(oh, om, hg, mo, x, mods, hnw, mnw, w_out, nw, w1, w2, fw)


def kernel(x, c, ctx, c_ctx, w_mod, b_mod, norm1_w, ffn1_w1, ffn1_w2, norm2_w, w_in, ml_gate_b,
           ml_conv_w, ml_conv_b, hg_lb_logits, hg_norm_w, ml_norm_w, w_out, norm3_w, ffn2_w1, ffn2_w2,
           final_norm_w):
    bsz, seq, d = x.shape
    n_ctx = ctx.shape[1]
    assert w_mod.shape[0] == 1, "single-layer kernel"
    assert seq % (GRID_W * 8) == 0 and n_ctx % CHUNK == 0
    hw = HG_HEADS * HG_DIM
    ng = ml_gate_b.shape[-1]

    rows = -(-(bsz + 1) // 8) * 8
    cvec = jnp.zeros((rows, d), F32).at[:bsz].set(c).at[bsz].set(c_ctx)
    mods = _modulation(cvec, w_mod[0], b_mod[0][None, :]).reshape(rows, N_MOD, d)
    lat_row = lambda b: b
    ctx_row = lambda b: bsz

    row = lambda a: a.reshape(1, -1)
    w1a, w2a = ffn1_w1[0], ffn1_w2[0]
    x1 = _ffn(x, mods[:, 0:3], lat_row, row(norm1_w[0]), w1a, w2a)
    s1 = _ffn(ctx, mods[:, 0:3], ctx_row, row(norm1_w[0]), w1a, w2a)

    w_in_b = w_in[0].T
    gate_order = np.arange(ng).reshape(2, 2, ML_HEADS // 2, 2).transpose(3, 0, 1, 2).reshape(-1)
    w_gate_t = w_in[0][:, 8 * hw:].T[gate_order].astype(BF16)
    gate_b = ml_gate_b[0][gate_order].reshape(ng, 1)
    dmat, pair_valid, wmask = (jnp.asarray(a) for a in _hgrn2_masks())
    le2, let2 = (jnp.asarray(a) for a in _mlstm_masks())
    conv_w, conv_b = ml_conv_w[0], row(ml_conv_b[0])
    npair = ML_HEADS // 2

    def mixer_scans(s, mod_row, width, s0, c0, m0):
        hq, hv, hg, hlf, mqk, mv, mo, gates = _inproj(
            s, mods[:, 3:5], mod_row, row(norm2_w[0]), w_in_b, w_gate_t, gate_b, hg_lb_logits)
        qk = _conv(mqk, conv_w, conv_b, width)
        oh, s_fin = _hgrn2(hq, hv, hlf, s0, dmat, pair_valid, wmask)
        om, c_fin, m_fin = _mlstm(qk, mv, gates, c0, m0, le2, let2)
        return oh, om, hg, mo, s_fin, c_fin, m_fin

    s0 = jnp.zeros((bsz, 2, HG_HEADS, HG_DIM, HG_DIM), F32)
    c0 = jnp.zeros((bsz, 2, npair, 2 * ML_QK, 2 * ML_V), F32)
    m0 = jnp.zeros((bsz, 2, 8, LANE), F32)
    _, _, _, _, s_ctx, c_ctx_state, m_ctx = mixer_scans(s1, ctx_row, n_ctx, s0, c0, m0)
    oh, om, hg, mo, _, _, _ = mixer_scans(x1, lat_row, GRID_W, s_ctx, c_ctx_state, m_ctx)

    return _mix_ffn(oh, om, hg, mo, x1, mods[:, 5:9], row(hg_norm_w[0]), row(ml_norm_w[0]),
                    w_out[0].astype(BF16), row(norm3_w[0]), ffn2_w1[0], ffn2_w2[0], row(final_norm_w))
```

```python
import functools

import numpy as np
import jax
import jax.numpy as jnp
from jax import lax
from jax.experimental import pallas as pl
from jax.experimental.pallas import tpu as pltpu

F32 = jnp.float32
BF16 = jnp.bfloat16

EPS = 1e-6
CHUNK = 64
GRID_W = 64
N_MOD = 9
HG_HEADS = 4
HG_DIM = 128
ML_HEADS = 4
ML_QK = 64
ML_V = 128
SUB = 16
N_SUB = CHUNK // SUB
GROUP = 4
ML_GROUP = 4
HG_CHUNKS = 8
ML_CHUNKS = 16
TOKEN_TILE = 512
ROW_SUB = 256
FF_TILE = 256
PIECE = 128
LOG2E = 1.4426950408889634
EXP2_CLAMP = 115.0
LANE = 128
VMEM_LIMIT = 56 * 1024 * 1024


def _cparams(sem):
    return pltpu.CompilerParams(dimension_semantics=sem, vmem_limit_bytes=VMEM_LIMIT)


def _dot(a, b):
    return jnp.dot(a, b, preferred_element_type=F32)


def _dot_nt(a, b):
    return lax.dot_general(a, b, (((1,), (1,)), ((), ())), preferred_element_type=F32)


def _dot_tn(a, b):
    return lax.dot_general(a, b, (((0,), (0,)), ((), ())), preferred_element_type=F32)


def _sigmoid(x):
    return 1.0 / (1.0 + jnp.exp(-x))


def _silu(x):
    return x * _sigmoid(x)


def _log_sigmoid(x):
    return jnp.minimum(x, 0.0) - jnp.log(1.0 + jnp.exp(-jnp.abs(x)))


def _split3(x):
    hi = x.astype(BF16)
    r = x - hi.astype(F32)
    mid = r.astype(BF16)
    lo = (r - mid.astype(F32)).astype(BF16)
    return hi, mid, lo


def _split2(x):
    hi = x.astype(BF16)
    return hi, (x - hi.astype(F32)).astype(BF16)


def _rms(x):
    return x * lax.rsqrt(jnp.mean(x * x, axis=-1, keepdims=True) + EPS)


def _mod_kernel(c_ref, w_ref, b_ref, o_ref):
    a = _silu(c_ref[...])
    a_hi = a.astype(BF16)
    a_lo = (a - a_hi.astype(F32)).astype(BF16)
    w = w_ref[...]
    w_hi = w.astype(BF16)
    w_lo = (w - w_hi.astype(F32)).astype(BF16)
    o_ref[...] = _dot(a_hi, w_hi) + _dot(a_hi, w_lo) + _dot(a_lo, w_hi) + b_ref[...]


def _modulation(cvec, w_mod, b_mod):
    rows, d = cvec.shape
    n = w_mod.shape[1]
    tn = 1024
    return pl.pallas_call(
        _mod_kernel,
        grid=(n // tn,),
        in_specs=[pl.BlockSpec((rows, d), lambda j: (0, 0)),
                  pl.BlockSpec((d, tn), lambda j: (0, j)),
                  pl.BlockSpec((1, tn), lambda j: (0, j))],
        out_specs=pl.BlockSpec((rows, tn), lambda j: (0, j)),
        out_shape=jax.ShapeDtypeStruct((rows, n), F32),
        compiler_params=_cparams(("arbitrary",)),
        name="modulation",
    )(cvec, w_mod, b_mod)


def _staggered(tiles):
    for _ in range(tiles[0].lead):
        next(tiles[0].gen)
    live = [t.gen for t in tiles]
    while live:
        live = [g for g in live if next(g, None) is not None]


class _Stages:
    def __init__(self, gen, lead):
        self.gen, self.lead = gen, lead


def _ffn_stages(get_x, put_y, mods, nw_ref, w1_ref, w2_ref, a_ref, x_scr, h_scr, rs, d_ff, tf):
    shift, scale, gate = mods
    for p0 in range(rs.start, rs.stop, PIECE):
        pr = slice(p0, min(p0 + PIECE, rs.stop))
        x = get_x(pr)
        x_scr[pr, :] = x
        h_scr[pr, :] = (_rms(x) * nw_ref[...] * (1.0 + scale) + shift).astype(BF16)
        yield True
    h = h_scr[rs, :]
    for c in range(d_ff // tf):
        g = _dot(h, w1_ref[:, c * tf:(c + 1) * tf].astype(BF16))
        u = _dot(h, w1_ref[:, d_ff + c * tf:d_ff + (c + 1) * tf].astype(BF16))
        a_ref[rs, c * tf:(c + 1) * tf] = (_silu(g) * u).astype(BF16)
        yield True
    y = x_scr[rs, :] + 0.5 * gate * _dot(a_ref[rs, :], w2_ref[...].astype(BF16))
    yield True
    put_y(y)
    yield True


def _sub_tiles(rows):
    sub = min(ROW_SUB, rows)
    return [slice(r0, r0 + sub) for r0 in range(0, rows, sub)]


def _ffn_lead(rs, d_ff, tf):
    return -(-(rs.stop - rs.start) // PIECE) + d_ff // tf // 2


def _ffn_kernel(x_ref, c_ref, mod_ref, nw_ref, w1_ref, w2_ref, o_ref, oc_ref, a_ref, x_scr, h_scr,
                *, d_ff, tf):
    mods = [mod_ref[0, k:k + 1, :] for k in range(3)]

    def run(src, dst):
        def stages(rs):
            def put(y):
                dst[0, rs, :] = y
            return _Stages(_ffn_stages(lambda pr: src[0, pr, :], put, mods, nw_ref, w1_ref, w2_ref, a_ref,
                                       x_scr, h_scr, rs, d_ff, tf), _ffn_lead(rs, d_ff, tf))

        _staggered([stages(rs) for rs in _sub_tiles(src.shape[1])])

    pl.when(pl.program_id(1) == 0)(lambda: run(c_ref, oc_ref))
    pl.when(pl.program_id(1) > 0)(lambda: run(x_ref, o_ref))


def _ffn(s, s_ctx, mods, nw, w1, w2):
    bsz, t, d = s.shape
    n_ctx = s_ctx.shape[1]
    d_ff = w2.shape[0]
    tm = min(TOKEN_TILE, t)
    assert n_ctx <= tm
    const = lambda b, j: (0, 0)
    lat = lambda b, j: (b, jnp.maximum(j - 1, 0), 0)
    return pl.pallas_call(
        functools.partial(_ffn_kernel, d_ff=d_ff, tf=FF_TILE),
        grid=(bsz, 1 + t // tm),
        in_specs=[pl.BlockSpec((1, tm, d), lat),
                  pl.BlockSpec((1, n_ctx, d), lambda b, j: (b, 0, 0)),
                  pl.BlockSpec((1, 3, d), lambda b, j: (jnp.where(j == 0, bsz, b), 0, 0)),
                  pl.BlockSpec((1, d), const),
                  pl.BlockSpec((d, 2 * d_ff), const, pipeline_mode=pl.Buffered(1)),
                  pl.BlockSpec((d_ff, d), const, pipeline_mode=pl.Buffered(1))],
        out_specs=[pl.BlockSpec((1, tm, d), lat), pl.BlockSpec((1, n_ctx, d), lambda b, j: (b, 0, 0))],
        out_shape=[jax.ShapeDtypeStruct((bsz, t, d), F32), jax.ShapeDtypeStruct((bsz, n_ctx, d), F32)],
        scratch_shapes=[pltpu.VMEM((tm, d_ff), BF16),
                        pltpu.VMEM((tm, d), F32),
                        pltpu.VMEM((tm, d), BF16)],
        compiler_params=_cparams(("parallel", "arbitrary")),
        name="ffn",
    )(s, s_ctx, mods, nw, w1, w2)


def _inproj_kernel(x_ref, mod_ref, nw_ref, w_ref, wg_ref, gb_ref, lbl_ref,
                   hq_ref, hv_ref, hg_ref, hlf_ref, mqk_ref, mv_ref, mo_ref, gc_ref, *, hw):
    shift = mod_ref[0, 0:1, :]
    scale = mod_ref[0, 1:2, :]
    lbl = lbl_ref[...]
    e = jnp.exp(lbl - jnp.max(lbl, axis=0, keepdims=True))
    lb = e[0] / jnp.sum(e, axis=0)
    first_half = lax.broadcasted_iota(jnp.int32, (8, LANE), 1) < CHUNK
    def stages(rs):
        h = (_rms(x_ref[0, rs, :]) * nw_ref[...] * (1.0 + scale) + shift).astype(BF16)
        yield True

        def proj(k):
            return _dot_nt(h, w_ref[k * hw:(k + 1) * hw, :].astype(BF16))

        hq_ref[0, rs, :] = _silu(proj(0)) * (HG_DIM ** -0.5)
        yield True
        hv_ref[0, rs, :] = proj(1).astype(BF16)
        yield True
        hg_ref[0, rs, :] = proj(2).astype(BF16)
        yield True
        for d in range(2):
            lbd = lb[d:d + 1, :]
            hlf_ref[0, d, rs, :] = jnp.log(lbd + (1.0 - lbd) * _sigmoid(proj(3 + d))) * LOG2E
            yield True
        mqk_ref[0, rs, :] = proj(5).astype(BF16)
        yield True
        mv_ref[0, rs, :] = proj(6).astype(BF16)
        yield True
        mo_ref[0, rs, :] = proj(7).astype(BF16)
        gt = _dot_nt(wg_ref[...], h) + gb_ref[...]
        for m in range((rs.stop - rs.start) // LANE):
            top = gt[0:8, m * LANE:(m + 1) * LANE]
            bot = gt[8:16, m * LANE:(m + 1) * LANE]
            c0 = rs.start // CHUNK + 2 * m
            gc_ref[0, c0] = jnp.where(first_half, top, pltpu.roll(bot, CHUNK, 1))
            gc_ref[0, c0 + 1] = jnp.where(first_half, pltpu.roll(top, CHUNK, 1), bot)
        yield True

    _staggered([_Stages(stages(rs), 8) for rs in _sub_tiles(x_ref.shape[1])])


def _inproj(s, mods, mod_row, nw, w_in, w_gate_t, gate_b, lb_logits):
    bsz, t, d = s.shape
    hw = HG_HEADS * HG_DIM
    ng = gate_b.shape[0]
    tm = min(TOKEN_TILE, t)
    tok4 = lambda b, i: (b, i, 0, 0)
    const = lambda b, i: (0, 0)
    tok = lambda b, i: (b, i, 0)
    tok2 = lambda b, i: (b, 0, i, 0)
    f = lambda dt, *shape: jax.ShapeDtypeStruct(shape, dt)
    return pl.pallas_call(
        functools.partial(_inproj_kernel, hw=hw),
        grid=(bsz, t // tm),
        in_specs=[pl.BlockSpec((1, tm, d), tok),
                  pl.BlockSpec((1, 2, d), lambda b, i: (mod_row(b), 0, 0)),
                  pl.BlockSpec((1, d), const),
                  pl.BlockSpec(w_in.shape, const, pipeline_mode=pl.Buffered(1)),
                  pl.BlockSpec(w_gate_t.shape, const),
                  pl.BlockSpec((ng, 1), const),
                  pl.BlockSpec(lb_logits.shape, lambda b, i: (0, 0, 0))],
        out_specs=[pl.BlockSpec((1, tm, hw), tok), pl.BlockSpec((1, tm, hw), tok),
                   pl.BlockSpec((1, tm, hw), tok),
                   pl.BlockSpec((1, 2, tm, hw), tok2),
                   pl.BlockSpec((1, tm, hw), tok), pl.BlockSpec((1, tm, hw), tok),
                   pl.BlockSpec((1, tm, hw), tok), pl.BlockSpec((1, tm // CHUNK, ng // 2, LANE), tok4)],
        out_shape=[f(F32, bsz, t, hw), f(BF16, bsz, t, hw), f(BF16, bsz, t, hw),
                   f(F32, bsz, 2, t, hw),
                   f(BF16, bsz, t, hw), f(BF16, bsz, t, hw), f(BF16, bsz, t, hw),
                   f(F32, bsz, t // CHUNK, ng // 2, LANE)],
        compiler_params=_cparams(("parallel", "parallel")),
        name="inproj",
    )(s, mods, nw, w_in, w_gate_t, gate_b, lb_logits)


def _conv_kernel(x_ref, w_ref, b_ref, o_ref, *, width, q_tiles):
    x = x_ref[0].astype(F32)
    t = x.shape[0]
    col = lax.broadcasted_iota(jnp.int32, x.shape, 0) % width
    xl = jnp.where(col == 0, 0.0, pltpu.roll(x, 1, 0))
    xr = jnp.where(col == width - 1, 0.0, pltpu.roll(x, t - 1, 0))

    def row(di):
        return (w_ref[di, 0:1, :] * xl + w_ref[di, 1:2, :] * x + w_ref[di, 2:3, :] * xr)

    y = row(1)
    if t > width:
        pad = jnp.zeros((width, x.shape[1]), F32)
        y = y + jnp.concatenate([pad, row(0)[:t - width]], axis=0)
        y = y + jnp.concatenate([row(2)[width:], pad], axis=0)
    y = _silu(y + b_ref[...])
    qscale = jnp.where(pl.program_id(1) < q_tiles, ML_QK ** -0.5, 1.0)
    o_ref[0] = (y * qscale).astype(BF16)


def _conv(x, w, b, width):
    bsz, t, ch = x.shape
    return pl.pallas_call(
        functools.partial(_conv_kernel, width=width, q_tiles=ML_HEADS * ML_QK // LANE),
        grid=(bsz, ch // LANE),
        in_specs=[pl.BlockSpec((1, t, LANE), lambda b_, c: (b_, 0, c)),
                  pl.BlockSpec((3, 3, LANE), lambda b_, c: (0, 0, c)),
                  pl.BlockSpec((1, LANE), lambda b_, c: (0, c))],
        out_specs=pl.BlockSpec((1, t, LANE), lambda b_, c: (b_, 0, c)),
        out_shape=jax.ShapeDtypeStruct((bsz, t, ch), BF16),
        compiler_params=_cparams(("parallel", "parallel")),
        name="conv",
    )(x, w, b)


def _positions():
    t = np.arange(CHUNK)
    return [t, CHUNK - 1 - t]


def _hgrn2_masks():
    ds, vs, ws = [], [], []
    for p in _positions():
        le = (p[None, :] <= p[:, None]).astype(np.float32)
        order = [p[SUB * j] // SUB for j in range(N_SUB)]
        ref = np.zeros((32, CHUNK), np.float32)
        valid = np.zeros((N_SUB * N_SUB, 1), np.float32)
        for j in range(N_SUB):
            ref[j] = p <= SUB * order[j] + SUB // 2 - 1
            ref[8 + j] = p <= SUB * order[j] + SUB - 1
        ref[N_SUB] = 1.0
        for j in range(N_SUB):
            for i in range(N_SUB):
                if order[i] < order[j]:
                    ref[16 + N_SUB * j + i] = ref[j] - ref[8 + i]
                    valid[N_SUB * j + i] = 1.0
        ds.append(np.concatenate([le, ref], axis=0))
        vs.append(np.broadcast_to(valid, (N_SUB * N_SUB, HG_HEADS * HG_DIM)))
        tok_blk = np.arange(CHUNK)[:, None] // SUB
        ws.append(np.concatenate([(tok_blk == j) * le for j in range(N_SUB)], axis=1))
    return (np.stack(ds).astype(np.float32), np.stack(vs).astype(np.float32),
            np.stack(ws).astype(np.float32))


def _mlstm_masks():
    le2s, let2s = [], []
    for p in _positions():
        le = (p[None, :] <= p[:, None]).astype(np.float32)
        le2s.append(np.concatenate([le, le], axis=1))
        let2 = np.zeros((2 * CHUNK, 2 * CHUNK), np.float32)
        let2[:CHUNK, :CHUNK] = le.T
        let2[CHUNK:, CHUNK:] = le.T
        let2s.append(let2)
    return np.stack(le2s), np.stack(let2s)


def _run_pipelined(n_groups, group, stages_of, step_of):
    queue = []
    for g in range(n_groups):
        for _ in stages_of(g):
            if queue:
                queue.pop(0)()
        while queue:
            queue.pop(0)()
        queue = [functools.partial(step_of, g * group + i) for i in range(group)]
    for step in queue:
        step()


def _chunk_rows(c, nc, direction):
    cc = jnp.where(direction == 0, c, nc - 1 - c)
    return pl.multiple_of(cc * CHUNK, CHUNK)


def _hgrn2_kernel(q_ref, v_ref, lf_ref, d_ref, pv_ref, w_ref, s0_ref, o_ref, st_ref,
                  u_ref, qin_ref, g_ref, oi_ref, bx_ref, kx_ref, vx_ref, *, nc):
    direction = pl.program_id(1)

    @pl.when(pl.program_id(2) == 0)
    def _():
        st_ref[...] = s0_ref[...]

    dmat = d_ref[0].astype(BF16)
    pair_valid = pv_ref[0]
    wmask = w_ref[0] > 0.0
    L = CHUNK
    hw = q_ref.shape[-1]
    upper_rows = lax.broadcasted_iota(jnp.int32, (L, 2 * L), 0) < L // 2

    group = min(GROUP, nc)

    def intra(grp):
        cs = [grp * group + i for i in range(group)]
        rows = [pl.ds(_chunk_rows(c, nc, direction), L) for c in cs]
        items = [(i, h) for i in range(group) for h in range(HG_HEADS)]
        lanes = [slice(h * HG_DIM, (h + 1) * HG_DIM) for h in range(HG_HEADS)]
        lfs = [lf_ref[0, 0, r, :] for r in rows]
        parts = [_split2(lf) for lf in lfs]
        e_all = sum(_dot(dmat, jnp.concatenate([parts[i][n] for i in range(group)], axis=1))
                    for n in range(2))
        yield
        q_mids, k_mids, k_outs, vbs, spans = [], [], [], [], []
        for i in range(group):
            e = e_all[:, i * hw:(i + 1) * hw]
            q = q_ref[0, rows[i], :]
            k = 1.0 - jnp.exp2(lfs[i])
            b = e[0:L]
            b_tot = e[L + N_SUB:L + N_SUB + 1]
            sub_rows = lambda r0: jnp.concatenate(
                [jnp.broadcast_to(e[r0 + j:r0 + j + 1], (SUB, hw)) for j in range(N_SUB)], axis=0)
            mid_rows = sub_rows(L)
            end_rows = sub_rows(L + 8)
            to_mid = b - mid_rows
            spans.append(jnp.max(jnp.abs(to_mid)))
            q_mids.append((q * jnp.exp2(jnp.minimum(to_mid, EXP2_CLAMP))).astype(BF16))
            k_diag = k * jnp.exp2(jnp.minimum(-to_mid, EXP2_CLAMP))
            k_end = k * jnp.exp2(end_rows - b)
            cross = jnp.exp2(e[L + 16:L + 32]) * pair_valid
            slabs = []
            for j in range(N_SUB):
                for ib in range(N_SUB):
                    rs = slice(ib * SUB, (ib + 1) * SUB)
                    row = N_SUB * j + ib
                    slabs.append(k_diag[rs] if ib == j else k_end[rs] * cross[row:row + 1])
            k_mids.append(jnp.concatenate(slabs, axis=0).astype(BF16))
            k_outs.append((k * jnp.exp2(b_tot - b)).astype(BF16))
            vbs.append(v_ref[0, rows[i], :])
            qin_ref[cs[i]] = (q * jnp.exp2(b)).astype(BF16)
            g_ref[cs[i]] = jnp.broadcast_to(jnp.exp2(b_tot), (8, hw))
            yield
        scores = [_dot_nt(q_mids[i][:, lanes[h]], k_mids[i][:, lanes[h]]) for i, h in items]
        yield
        a_s = [jnp.where(wmask, s, 0.0) for s in scores]
        a_s = [jnp.where(upper_rows, a[:, :2 * L], a[:, 2 * L:]).astype(BF16) for a in a_s]
        v2s = [jnp.concatenate([vb, vb], axis=0) for vb in vbs]
        outs = [_dot(a_s[n], v2s[i][:, lanes[h]]) for n, (i, h) in enumerate(items)]
        yield
        for i in range(group):
            oi_ref[cs[i]] = jnp.concatenate(outs[i * HG_HEADS:(i + 1) * HG_HEADS], axis=1)
        yield
        for i, h in items:
            u_ref[cs[i], h] = _dot_tn(vbs[i][:, lanes[h]], k_outs[i][:, lanes[h]])

        @pl.when(functools.reduce(jnp.maximum, spans) > EXP2_CLAMP)
        def _():
            for i in range(group):
                exact_intra(cs[i], rows[i])
        yield

    def exact_intra(c, rows):
        lf = lf_ref[0, 0, rows, :]
        bx_ref[...] = sum(_dot(dmat[0:L], x) for x in _split3(lf))
        kx_ref[...] = 1.0 - jnp.exp2(lf)
        vx_ref[...] = v_ref[0, rows, :].astype(F32)
        q = q_ref[0, rows, :]
        tok = lax.broadcasted_iota(jnp.int32, (L, 1), 0)
        pos = jnp.where(direction == 0, tok, L - 1 - tok)

        def add_source(s, acc):
            b_s = bx_ref[pl.ds(s, 1), :]
            z = q * kx_ref[pl.ds(s, 1), :] * jnp.exp2(jnp.minimum(bx_ref[...] - b_s, 0.0))
            v_s = vx_ref[pl.ds(s, 1), :]
            seen = pos >= jnp.where(direction == 0, s, L - 1 - s)
            cols = [jnp.where(seen, jnp.sum(z[:, ln], axis=1, keepdims=True), 0.0) * v_s[:, ln]
                    for ln in (slice(h * HG_DIM, (h + 1) * HG_DIM) for h in range(HG_HEADS))]
            return acc + jnp.concatenate(cols, axis=1)

        oi_ref[c] = lax.fori_loop(0, L, add_source, jnp.zeros((L, hw), F32))

    def carry_state(c):
        rows = pl.ds(_chunk_rows(c, nc, direction), L)
        q_in = qin_ref[c]
        g = g_ref[c][0:1]
        outs, states = [], []
        for h in range(HG_HEADS):
            ln = slice(h * HG_DIM, (h + 1) * HG_DIM)
            st = st_ref[0, 0, h]
            outs.append(_dot_nt(q_in[:, ln], st.astype(BF16)))
            states.append(g[:, ln] * st + u_ref[c, h])
        o_ref[0, 0, rows, :] = (oi_ref[c] + jnp.concatenate(outs, axis=1)).astype(BF16)
        for h in range(HG_HEADS):
            st_ref[0, 0, h] = states[h]

    _run_pipelined(nc // group, group, intra, carry_state)


def _hgrn2(q, v, lf, s0, dmat, pair_valid, wmask):
    bsz, t, hw = q.shape
    nc = min(HG_CHUNKS, t // CHUNK)
    tb = nc * CHUNK
    nblk = t // tb

    def blk(j, d):
        return jnp.where(d == 0, j, nblk - 1 - j)

    st_spec = pl.BlockSpec((1, 1) + s0.shape[2:], lambda b, d, j: (b, d, 0, 0, 0))
    return pl.pallas_call(
        functools.partial(_hgrn2_kernel, nc=nc),
        grid=(bsz, 2, nblk),
        in_specs=[pl.BlockSpec((1, tb, hw), lambda b, d, j: (b, blk(j, d), 0)),
                  pl.BlockSpec((1, tb, hw), lambda b, d, j: (b, blk(j, d), 0)),
                  pl.BlockSpec((1, 1, tb, hw), lambda b, d, j: (b, d, blk(j, d), 0)),
                  pl.BlockSpec((1,) + dmat.shape[1:], lambda b, d, j: (d, 0, 0)),
                  pl.BlockSpec((1,) + pair_valid.shape[1:], lambda b, d, j: (d, 0, 0)),
                  pl.BlockSpec((1,) + wmask.shape[1:], lambda b, d, j: (d, 0, 0)),
                  st_spec],
        out_specs=[pl.BlockSpec((1, 1, tb, hw), lambda b, d, j: (b, d, blk(j, d), 0)), st_spec],
        out_shape=[jax.ShapeDtypeStruct((bsz, 2, t, hw), BF16),
                   jax.ShapeDtypeStruct(s0.shape, F32)],
        scratch_shapes=[pltpu.VMEM((nc,) + s0.shape[2:], F32),
                        pltpu.VMEM((nc, CHUNK, hw), BF16),
                        pltpu.VMEM((nc, 8, hw), F32),
                        pltpu.VMEM((nc, CHUNK, hw), F32),
                        pltpu.VMEM((CHUNK, hw), F32),
                        pltpu.VMEM((CHUNK, hw), F32),
                        pltpu.VMEM((CHUNK, hw), F32)],
        compiler_params=_cparams(("parallel", "parallel", "arbitrary")),
        name="hgrn2_scan",
    )(q, v, lf, dmat, pair_valid, wmask, s0)


def _mlstm_kernel(qk_ref, v_ref, g_ref, le2_ref, let2_ref, c0_ref, m0_ref,
                  o_ref, c_ref, mm_ref,
                  p_ref, u_ref, qs_ref, ct_ref, mloc_ref, sc_ref, *, nc):
    direction = pl.program_id(1)

    @pl.when(pl.program_id(2) == 0)
    def _():
        c_ref[...] = c0_ref[...]
        mm_ref[...] = m0_ref[...]

    le2 = le2_ref[0].astype(BF16)
    let2 = let2_ref[0].astype(BF16)
    causal = le2_ref[0] > 0.0
    L = CHUNK
    neg = -jnp.inf
    lane = lax.broadcasted_iota(jnp.int32, (1, LANE), 1)
    seg = [lane < L, lane >= L]
    lane_t = lax.broadcasted_iota(jnp.int32, (L, LANE), 1) < L
    eye = (lax.broadcasted_iota(jnp.int32, (LANE, LANE), 0)
           == lax.broadcasted_iota(jnp.int32, (LANE, LANE), 1))
    npair = ML_HEADS // 2
    ones_v = jnp.ones((L, ML_V), BF16)

    group = min(ML_GROUP, nc)

    def intra(grp):
        cs = [grp * group + i for i in range(group)]
        ccs = [jnp.where(direction == 0, c, nc - 1 - c) for c in cs]
        rows = [pl.ds(pl.multiple_of(cc * L, L), L) for cc in ccs]
        pairs = [(i, p) for i in range(group) for p in range(npair)]
        gs = [jnp.where(direction == 0, g_ref[0, cc][0:4], g_ref[0, cc][4:8]) for cc in ccs]
        pad = [jnp.zeros((-2 * group % 8, LANE), F32)] if 2 * group % 8 else []
        i2 = jnp.concatenate([g[0:2] for g in gs] + pad, axis=0)
        lf2 = _log_sigmoid(jnp.concatenate([g[2:4] for g in gs] + pad, axis=0))
        parts = _split3(lf2)
        r2 = i2 - sum(_dot(x, let2) for x in parts)
        g_a = jnp.max(jnp.where(seg[0], r2, neg), axis=1, keepdims=True)
        g_b = jnp.max(jnp.where(seg[1], r2, neg), axis=1, keepdims=True)
        e_w = jnp.exp(r2 - jnp.where(seg[0], g_a, g_b))
        bt_a = jnp.sum(jnp.where(seg[0], lf2, 0.0), axis=1, keepdims=True)
        bt_b = jnp.sum(jnp.where(seg[1], lf2, 0.0), axis=1, keepdims=True)
        yield

        def head_rows(x, i):
            zero = jnp.zeros_like(x[0:1])
            return jnp.concatenate(
                [jnp.broadcast_to(jnp.where(seg[h % 2], x[2 * i + h // 2:2 * i + h // 2 + 1], zero),
                                  (LANE, LANE)) for h in range(ML_HEADS)], axis=0)

        c_ts = [sum(_dot_nt(le2, head_rows(x, i)) for x in parts) for i in range(group)]
        for i in range(group):
            ct_ref[cs[i]] = c_ts[i]
            sc_ref[cs[i]] = jnp.concatenate(
                [jnp.broadcast_to(x[2 * i + p:2 * i + p + 1], (1, LANE))
                 for x2 in ((bt_a, bt_b), (g_a, g_b)) for p in range(npair) for x in x2], axis=0)
        yield
        qks = [qk_ref[0, r, :] for r in rows]
        vbs = [v_ref[0, r, :] for r in rows]
        q_ps = [qks[i][:, p * LANE:(p + 1) * LANE] for i, p in pairs]
        k_ps = [qks[i][:, (npair + p) * LANE:(npair + p + 1) * LANE] for i, p in pairs]
        zb = jnp.zeros((L, LANE), BF16)
        kbs = [jnp.concatenate([jnp.where(lane_t, k, zb), jnp.where(lane_t, zb, k)], axis=0) for k in k_ps]
        scores = [_dot_nt(q, kb) for q, kb in zip(q_ps, kbs)]
        yield
        ws = []
        for n, (i, p) in enumerate(pairs):
            ha, hb = 2 * p, 2 * p + 1
            t1 = jnp.where(lane_t, c_ts[i][:, ha * ML_V:(ha + 1) * ML_V], c_ts[i][:, hb * ML_V:(hb + 1) * ML_V])
            d_log = jnp.where(causal, t1 + r2[2 * i + p:2 * i + p + 1], neg)
            m_a = jnp.max(jnp.where(lane_t, d_log, neg), axis=1, keepdims=True)
            m_b = jnp.max(jnp.where(lane_t, neg, d_log), axis=1, keepdims=True)
            mloc_ref[cs[i], :, ha * ML_V:(ha + 1) * ML_V] = jnp.broadcast_to(m_a, (L, ML_V))
            mloc_ref[cs[i], :, hb * ML_V:(hb + 1) * ML_V] = jnp.broadcast_to(m_b, (L, ML_V))
            ws.append(jnp.exp(d_log - jnp.where(lane_t, m_a, m_b)))
        yield
        a_s = [(s_ * w).astype(BF16) for s_, w in zip(scores, ws)]
        a_stacks = [jnp.concatenate([jnp.where(lane_t, a, zb), jnp.where(lane_t, zb, a)], axis=0) for a in a_s]
        v_stacks = [jnp.concatenate(
            [jnp.concatenate([vbs[i][:, 2 * p * ML_V:(2 * p + 1) * ML_V], ones_v], axis=1),
             jnp.concatenate([vbs[i][:, (2 * p + 1) * ML_V:(2 * p + 2) * ML_V], ones_v], axis=1)], axis=0)
            for i, p in pairs]
        for n, (i, p) in enumerate(pairs):
            p_ref[cs[i], p] = _dot(a_stacks[n], v_stacks[n])
            qs_ref[cs[i], p] = jnp.concatenate(
                [jnp.where(lane_t, q_ps[n], zb), jnp.where(lane_t, zb, q_ps[n])], axis=0)
        yield
        diags = [jnp.where(eye, jnp.broadcast_to(e_w[2 * i + p:2 * i + p + 1], (LANE, LANE)), 0.0).astype(BF16)
                 for i, p in pairs]
        kb_ws = [_dot(d, kb).astype(BF16) for d, kb in zip(diags, kbs)]
        for n, (i, p) in enumerate(pairs):
            u_ref[cs[i], p] = _dot_tn(kb_ws[n], v_stacks[n])
        yield

    def carry_state(c):
        cc = jnp.where(direction == 0, c, nc - 1 - c)
        rows = pl.ds(pl.multiple_of(cc * L, L), L)
        m_all = mm_ref[0, 0]
        sc = sc_ref[c]
        c_t = ct_ref[c]
        m_loc = mloc_ref[c]
        outs, states, m_rows = [], [], []
        for p in range(ML_HEADS // 2):
            c_pair = c_ref[0, 0, p]
            inter = _dot(qs_ref[c, p], c_pair.astype(BF16))
            intra_p = p_ref[c, p]
            inc = u_ref[c, p]
            new_rows = []
            for hh in range(2):
                h = 2 * p + hh
                rs = slice(hh * L, (hh + 1) * L)
                ln = slice(h * ML_V, (h + 1) * ML_V)
                m_in = m_all[h:h + 1]
                m_t = jnp.maximum(m_loc[:, ln], c_t[:, ln] + m_in)
                alpha = jnp.exp(m_loc[:, ln] - m_t)
                beta = jnp.exp(c_t[:, ln] + m_in - m_t)
                num = alpha * intra_p[rs, :ML_V] + beta * inter[rs, :ML_V]
                den = alpha * intra_p[rs, ML_V:] + beta * inter[rs, ML_V:]
                outs.append(num / jnp.maximum(jnp.abs(den), jnp.exp(-m_t)))
                b_tot = sc[h:h + 1]
                m_new = b_tot + jnp.maximum(m_in, sc[ML_HEADS + h:ML_HEADS + h + 1])
                carry_w = jnp.exp(b_tot + m_in - m_new)
                inc_w = jnp.exp(b_tot + sc[ML_HEADS + h:ML_HEADS + h + 1] - m_new)
                cw2 = jnp.concatenate([carry_w, carry_w], axis=1)
                iw2 = jnp.concatenate([inc_w, inc_w], axis=1)
                new_rows.append(cw2 * c_pair[rs] + iw2 * inc[rs])
                m_rows.append(m_new)
            states.append(jnp.concatenate(new_rows, axis=0))
        o_ref[0, 0, rows, :] = jnp.concatenate(outs, axis=1).astype(BF16)
        for p in range(ML_HEADS // 2):
            c_ref[0, 0, p] = states[p]
        mm_ref[0, 0] = jnp.concatenate(m_rows + [m_all[ML_HEADS:]], axis=0)

    _run_pipelined(nc // group, group, intra, carry_state)


def _mlstm(qk, v, gates, c0, m0, le2, let2):
    bsz, t, hw = v.shape
    nc = min(ML_CHUNKS, t // CHUNK)
    tb = nc * CHUNK
    nblk = t // tb
    npair = ML_HEADS // 2

    def blk(j, d):
        return jnp.where(d == 0, j, nblk - 1 - j)

    c_spec = pl.BlockSpec((1, 1) + c0.shape[2:], lambda b, d, j: (b, d, 0, 0, 0))
    m_spec = pl.BlockSpec((1, 1) + m0.shape[2:], lambda b, d, j: (b, d, 0, 0))
    return pl.pallas_call(
        functools.partial(_mlstm_kernel, nc=nc),
        grid=(bsz, 2, nblk),
        in_specs=[pl.BlockSpec((1, tb, qk.shape[-1]), lambda b, d, j: (b, blk(j, d), 0)),
                  pl.BlockSpec((1, tb, hw), lambda b, d, j: (b, blk(j, d), 0)),
                  pl.BlockSpec((1, nc) + gates.shape[2:], lambda b, d, j: (b, blk(j, d), 0, 0)),
                  pl.BlockSpec((1,) + le2.shape[1:], lambda b, d, j: (d, 0, 0)),
                  pl.BlockSpec((1,) + let2.shape[1:], lambda b, d, j: (d, 0, 0)),
                  c_spec, m_spec],
        out_specs=[pl.BlockSpec((1, 1, tb, hw), lambda b, d, j: (b, d, blk(j, d), 0)), c_spec, m_spec],
        out_shape=[jax.ShapeDtypeStruct((bsz, 2, t, hw), BF16),
                   jax.ShapeDtypeStruct(c0.shape, F32),
                   jax.ShapeDtypeStruct(m0.shape, F32)],
        scratch_shapes=[pltpu.VMEM((nc, npair, 2 * CHUNK, 2 * ML_V), F32),
                        pltpu.VMEM((nc, npair, 2 * ML_QK, 2 * ML_V), F32),
                        pltpu.VMEM((nc, npair, 2 * CHUNK, LANE), BF16),
                        pltpu.VMEM((nc, CHUNK, hw), F32),
                        pltpu.VMEM((nc, CHUNK, hw), F32),
                        pltpu.VMEM((nc, 8, LANE), F32)],
        compiler_params=_cparams(("parallel", "parallel", "arbitrary")),
        name="mlstm_scan",
    )(qk, v, gates, le2, let2, c0, m0)


def _head_rms(x, heads):
    dh = x.shape[-1] // heads
    return jnp.concatenate([_rms(x[:, h * dh:(h + 1) * dh]) for h in range(heads)], axis=1)


def _mix_ffn_kernel(oh_ref, om_ref, hg_ref, mo_ref, x_ref, mod_ref, hnw_ref, mnw_ref, wo_ref,
                    nw_ref, w1_ref, w2_ref, fw_ref, o_ref, a_ref, x_scr, h_scr, *, d_ff, tf):
    mix_gate = mod_ref[0, 0:1, :]
    mods = [mod_ref[0, k:k + 1, :] for k in range(1, 4)]

    def stages(rs):
        def mixed(pr):
            oh = oh_ref[0, 0, pr, :].astype(F32) + oh_ref[0, 1, pr, :].astype(F32)
            om = om_ref[0, 0, pr, :].astype(F32) + om_ref[0, 1, pr, :].astype(F32)
            hg_out = _head_rms(oh, HG_HEADS) * hnw_ref[...] * _silu(hg_ref[0, pr, :].astype(F32))
            ml_out = _sigmoid(mo_ref[0, pr, :].astype(F32)) * (_head_rms(om, ML_HEADS) * mnw_ref[...])
            merged = jnp.concatenate([hg_out, ml_out], axis=1).astype(BF16)
            return x_ref[0, pr, :] + mix_gate * _dot(merged, wo_ref[...])

        def put(y):
            o_ref[0, rs, :] = _rms(y) * fw_ref[...]

        return _Stages(_ffn_stages(mixed, put, mods, nw_ref, w1_ref, w2_ref, a_ref, x_scr, h_scr,
                                   rs, d_ff, tf), _ffn_lead(rs, d_ff, tf))

    _staggered([stages(rs) for rs in _sub_tiles(x_ref.shape[1])])


def _mix_ffn(oh, om, hg, mo, x, mods, hnw, mnw, w_out, nw, w1, w2, fw):
    bsz, t, d = x.shape
    hw = hg.shape[-1]
    d_ff = w2.shape[0]
    tm = min(TOKEN_TILE, t)
    tok = lambda b, i: (b, i, 0)
    tok2 = lambda b, i: (b, 0, i, 0)
    const = lambda b, i: (0, 0)
    resident = lambda a: pl.BlockSpec(a.shape, const, pipeline_mode=pl.Buffered(1))
    return pl.pallas_call(
        functools.partial(_mix_ffn_kernel, d_ff=d_ff, tf=FF_TILE),
        grid=(bsz, t // tm),
        in_specs=[pl.BlockSpec((1, 2, tm, hw), tok2), pl.BlockSpec((1, 2, tm, hw), tok2),
                  pl.BlockSpec((1, tm, hw), tok), pl.BlockSpec((1, tm, hw), tok),
                  pl.BlockSpec((1, tm, d), tok),
                  pl.BlockSpec((1, 4, d), lambda b, i: (b, 0, 0)),
                  pl.BlockSpec((1, hw), const), pl.BlockSpec((1, hw), const), resident(w_out),
                  pl.BlockSpec((1, d), const), resident(w1), resident(w2), pl.BlockSpec((1, d), const)],
        out_specs=pl.BlockSpec((1, tm, d), tok),
        out_shape=jax.ShapeDtypeStruct((bsz, t, d), F32),
        scratch_shapes=[pltpu.VMEM((tm, d_ff), BF16),
                        pltpu.VMEM((tm, d), F32),
                        pltpu.VMEM((tm, d), BF16)],
        compiler_params=_cparams(("parallel", "parallel")),
        name="mix_ffn_final",
    )(oh, om, hg, mo, x, mods, hnw, mnw, w_out, nw, w1, w2, fw)


def kernel(x, c, ctx, c_ctx, w_mod, b_mod, norm1_w, ffn1_w1, ffn1_w2, norm2_w, w_in, ml_gate_b,
           ml_conv_w, ml_conv_b, hg_lb_logits, hg_norm_w, ml_norm_w, w_out, norm3_w, ffn2_w1, ffn2_w2,
           final_norm_w):
    bsz, seq, d = x.shape
    n_ctx = ctx.shape[1]
    assert w_mod.shape[0] == 1, "single-layer kernel"
    assert seq % (GRID_W * 8) == 0 and n_ctx % CHUNK == 0
    hw = HG_HEADS * HG_DIM
    ng = ml_gate_b.shape[-1]

    rows = -(-(bsz + 1) // 8) * 8
    cvec = jnp.zeros((rows, d), F32).at[:bsz].set(c).at[bsz].set(c_ctx)
    mods = _modulation(cvec, w_mod[0], b_mod[0][None, :]).reshape(rows, N_MOD, d)
    lat_row = lambda b: b
    ctx_row = lambda b: bsz

    row = lambda a: a.reshape(1, -1)
    w1a, w2a = ffn1_w1[0], ffn1_w2[0]
    x1, s1 = _ffn(x, ctx, mods[:, 0:3], row(norm1_w[0]), w1a, w2a)

    w_in_b = w_in[0].T
    gate_order = np.arange(ng).reshape(2, 2, ML_HEADS // 2, 2).transpose(3, 0, 1, 2).reshape(-1)
    w_gate_t = w_in[0][:, 8 * hw:].T[gate_order].astype(BF16)
    gate_b = ml_gate_b[0][gate_order].reshape(ng, 1)
    dmat, pair_valid, wmask = (jnp.asarray(a) for a in _hgrn2_masks())
    le2, let2 = (jnp.asarray(a) for a in _mlstm_masks())
    conv_w, conv_b = ml_conv_w[0], row(ml_conv_b[0])
    npair = ML_HEADS // 2

    def mixer_scans(s, mod_row, width, s0, c0, m0):
        hq, hv, hg, hlf, mqk, mv, mo, gates = _inproj(
            s, mods[:, 3:5], mod_row, row(norm2_w[0]), w_in_b, w_gate_t, gate_b, hg_lb_logits)
        qk = _conv(mqk, conv_w, conv_b, width)
        oh, s_fin = _hgrn2(hq, hv, hlf, s0, dmat, pair_valid, wmask)
        om, c_fin, m_fin = _mlstm(qk, mv, gates, c0, m0, le2, let2)
        return oh, om, hg, mo, s_fin, c_fin, m_fin

    s0 = jnp.zeros((bsz, 2, HG_HEADS, HG_DIM, HG_DIM), F32)
    c0 = jnp.zeros((bsz, 2, npair, 2 * ML_QK, 2 * ML_V), F32)
    m0 = jnp.zeros((bsz, 2, 8, LANE), F32)
    _, _, _, _, s_ctx, c_ctx_state, m_ctx = mixer_scans(s1, ctx_row, n_ctx, s0, c0, m0)
    oh, om, hg, mo, _, _, _ = mixer_scans(x1, lat_row, GRID_W, s_ctx, c_ctx_state, m_ctx)

    return _mix_ffn(oh, om, hg, mo, x1, mods[:, 5:9], row(hg_norm_w[0]), row(ml_norm_w[0]),
                    w_out[0].astype(BF16), row(norm3_w[0]), ffn2_w1[0], ffn2_w2[0], row(final_norm_w))
```

```python
import functools

import numpy as np
import jax
import jax.numpy as jnp
from jax import lax
from jax.experimental import pallas as pl
from jax.experimental.pallas import tpu as pltpu

F32 = jnp.float32
BF16 = jnp.bfloat16

EPS = 1e-6
CHUNK = 64
GRID_W = 64
N_MOD = 9
HG_HEADS = 4
HG_DIM = 128
ML_HEADS = 4
ML_QK = 64
ML_V = 128
SUB = 16
N_SUB = CHUNK // SUB
GROUP = 4
ML_GROUP = 4
SCAN_CHUNKS = 8
TOKEN_TILE = 512
ROW_SUB = 256
FF_TILE = 256
PIECE = 128
LOG2E = 1.4426950408889634
EXP2_CLAMP = 115.0
LANE = 128
VMEM_LIMIT = 56 * 1024 * 1024


def _cparams(sem):
    return pltpu.CompilerParams(dimension_semantics=sem, vmem_limit_bytes=VMEM_LIMIT)


def _dot(a, b):
    return jnp.dot(a, b, preferred_element_type=F32)


def _dot_nt(a, b):
    return lax.dot_general(a, b, (((1,), (1,)), ((), ())), preferred_element_type=F32)


def _dot_tn(a, b):
    return lax.dot_general(a, b, (((0,), (0,)), ((), ())), preferred_element_type=F32)


def _sigmoid(x):
    return 1.0 / (1.0 + jnp.exp(-x))


def _silu(x):
    return x * _sigmoid(x)


def _log_sigmoid(x):
    return jnp.minimum(x, 0.0) - jnp.log(1.0 + jnp.exp(-jnp.abs(x)))


def _split3(x):
    hi = x.astype(BF16)
    r = x - hi.astype(F32)
    mid = r.astype(BF16)
    lo = (r - mid.astype(F32)).astype(BF16)
    return hi, mid, lo


def _split2(x):
    hi = x.astype(BF16)
    return hi, (x - hi.astype(F32)).astype(BF16)


def _rms(x):
    return x * lax.rsqrt(jnp.mean(x * x, axis=-1, keepdims=True) + EPS)


def _mod_kernel(c_ref, w_ref, b_ref, o_ref):
    a = _silu(c_ref[...])
    a_hi = a.astype(BF16)
    a_lo = (a - a_hi.astype(F32)).astype(BF16)
    w = w_ref[...]
    w_hi = w.astype(BF16)
    w_lo = (w - w_hi.astype(F32)).astype(BF16)
    o_ref[...] = _dot(a_hi, w_hi) + _dot(a_hi, w_lo) + _dot(a_lo, w_hi) + b_ref[...]


def _modulation(cvec, w_mod, b_mod):
    rows, d = cvec.shape
    n = w_mod.shape[1]
    tn = 1024
    return pl.pallas_call(
        _mod_kernel,
        grid=(n // tn,),
        in_specs=[pl.BlockSpec((rows, d), lambda j: (0, 0)),
                  pl.BlockSpec((d, tn), lambda j: (0, j)),
                  pl.BlockSpec((1, tn), lambda j: (0, j))],
        out_specs=pl.BlockSpec((rows, tn), lambda j: (0, j)),
        out_shape=jax.ShapeDtypeStruct((rows, n), F32),
        compiler_params=_cparams(("arbitrary",)),
        name="modulation",
    )(cvec, w_mod, b_mod)


def _staggered(tiles):
    for _ in range(tiles[0].lead):
        next(tiles[0].gen)
    live = [t.gen for t in tiles]
    while live:
        live = [g for g in live if next(g, None) is not None]


class _Stages:
    def __init__(self, gen, lead):
        self.gen, self.lead = gen, lead


def _ffn_stages(get_x, put_y, mods, nw_ref, w1_ref, w2_ref, a_ref, x_scr, h_scr, rs, d_ff, tf):
    shift, scale, gate = mods
    for p0 in range(rs.start, rs.stop, PIECE):
        pr = slice(p0, min(p0 + PIECE, rs.stop))
        x = get_x(pr)
        x_scr[pr, :] = x
        h_scr[pr, :] = (_rms(x) * nw_ref[...] * (1.0 + scale) + shift).astype(BF16)
        yield True
    h = h_scr[rs, :]
    for c in range(d_ff // tf):
        g = _dot(h, w1_ref[:, c * tf:(c + 1) * tf].astype(BF16))
        u = _dot(h, w1_ref[:, d_ff + c * tf:d_ff + (c + 1) * tf].astype(BF16))
        a_ref[rs, c * tf:(c + 1) * tf] = (_silu(g) * u).astype(BF16)
        yield True
    y = x_scr[rs, :] + 0.5 * gate * _dot(a_ref[rs, :], w2_ref[...].astype(BF16))
    yield True
    put_y(y)
    yield True


def _sub_tiles(rows):
    sub = min(ROW_SUB, rows)
    return [slice(r0, r0 + sub) for r0 in range(0, rows, sub)]


def _ffn_lead(rs, d_ff, tf):
    return -(-(rs.stop - rs.start) // PIECE) + d_ff // tf // 2


def _ffn_kernel(x_ref, mod_ref, nw_ref, w1_ref, w2_ref, o_ref, a_ref, x_scr, h_scr, *, d_ff, tf):
    mods = [mod_ref[0, k:k + 1, :] for k in range(3)]

    def stages(rs):
        def put(y):
            o_ref[0, rs, :] = y
        return _Stages(_ffn_stages(lambda pr: x_ref[0, pr, :], put, mods, nw_ref, w1_ref, w2_ref, a_ref,
                                   x_scr, h_scr, rs, d_ff, tf), _ffn_lead(rs, d_ff, tf))

    _staggered([stages(rs) for rs in _sub_tiles(x_ref.shape[1])])


def _ffn(s, mods, mod_row, nw, w1, w2):
    bsz, t, d = s.shape
    d_ff = w2.shape[0]
    tm = min(TOKEN_TILE, t)
    const = lambda b, i: (0, 0)
    return pl.pallas_call(
        functools.partial(_ffn_kernel, d_ff=d_ff, tf=FF_TILE),
        grid=(bsz, t // tm),
        in_specs=[pl.BlockSpec((1, tm, d), lambda b, i: (b, i, 0)),
                  pl.BlockSpec((1, 3, d), lambda b, i: (mod_row(b), 0, 0)),
                  pl.BlockSpec((1, d), const),
                  pl.BlockSpec((d, 2 * d_ff), const, pipeline_mode=pl.Buffered(1)),
                  pl.BlockSpec((d_ff, d), const, pipeline_mode=pl.Buffered(1))],
        out_specs=pl.BlockSpec((1, tm, d), lambda b, i: (b, i, 0)),
        out_shape=jax.ShapeDtypeStruct((bsz, t, d), F32),
        scratch_shapes=[pltpu.VMEM((tm, d_ff), BF16),
                        pltpu.VMEM((tm, d), F32),
                        pltpu.VMEM((tm, d), BF16)],
        compiler_params=_cparams(("parallel", "parallel")),
        name="ffn",
    )(s, mods, nw, w1, w2)


def _inproj_kernel(x_ref, mod_ref, nw_ref, w_ref, wg_ref, gb_ref, lbl_ref,
                   hq_ref, hv_ref, hg_ref, hlf_ref, mqk_ref, mv_ref, mo_ref, gc_ref, *, hw):
    shift = mod_ref[0, 0:1, :]
    scale = mod_ref[0, 1:2, :]
    lbl = lbl_ref[...]
    e = jnp.exp(lbl - jnp.max(lbl, axis=0, keepdims=True))
    lb = e[0] / jnp.sum(e, axis=0)
    first_half = lax.broadcasted_iota(jnp.int32, (8, LANE), 1) < CHUNK
    def stages(rs):
        h = (_rms(x_ref[0, rs, :]) * nw_ref[...] * (1.0 + scale) + shift).astype(BF16)
        yield True

        def proj(k):
            return _dot_nt(h, w_ref[k * hw:(k + 1) * hw, :].astype(BF16))

        hq_ref[0, rs, :] = _silu(proj(0)) * (HG_DIM ** -0.5)
        yield True
        hv_ref[0, rs, :] = proj(1).astype(BF16)
        yield True
        hg_ref[0, rs, :] = proj(2).astype(BF16)
        yield True
        for d in range(2):
            lbd = lb[d:d + 1, :]
            hlf_ref[0, d, rs, :] = jnp.log(lbd + (1.0 - lbd) * _sigmoid(proj(3 + d))) * LOG2E
            yield True
        mqk_ref[0, rs, :] = proj(5).astype(BF16)
        yield True
        mv_ref[0, rs, :] = proj(6).astype(BF16)
        yield True
        mo_ref[0, rs, :] = proj(7).astype(BF16)
        gt = _dot_nt(wg_ref[...], h) + gb_ref[...]
        for m in range((rs.stop - rs.start) // LANE):
            top = gt[0:8, m * LANE:(m + 1) * LANE]
            bot = gt[8:16, m * LANE:(m + 1) * LANE]
            c0 = rs.start // CHUNK + 2 * m
            gc_ref[0, c0] = jnp.where(first_half, top, pltpu.roll(bot, CHUNK, 1))
            gc_ref[0, c0 + 1] = jnp.where(first_half, pltpu.roll(top, CHUNK, 1), bot)
        yield True

    _staggered([_Stages(stages(rs), 8) for rs in _sub_tiles(x_ref.shape[1])])


def _inproj(s, mods, mod_row, nw, w_in, w_gate_t, gate_b, lb_logits):
    bsz, t, d = s.shape
    hw = HG_HEADS * HG_DIM
    ng = gate_b.shape[0]
    tm = min(TOKEN_TILE, t)
    tok4 = lambda b, i: (b, i, 0, 0)
    const = lambda b, i: (0, 0)
    tok = lambda b, i: (b, i, 0)
    tok2 = lambda b, i: (b, 0, i, 0)
    f = lambda dt, *shape: jax.ShapeDtypeStruct(shape, dt)
    return pl.pallas_call(
        functools.partial(_inproj_kernel, hw=hw),
        grid=(bsz, t // tm),
        in_specs=[pl.BlockSpec((1, tm, d), tok),
                  pl.BlockSpec((1, 2, d), lambda b, i: (mod_row(b), 0, 0)),
                  pl.BlockSpec((1, d), const),
                  pl.BlockSpec(w_in.shape, const, pipeline_mode=pl.Buffered(1)),
                  pl.BlockSpec(w_gate_t.shape, const),
                  pl.BlockSpec((ng, 1), const),
                  pl.BlockSpec(lb_logits.shape, lambda b, i: (0, 0, 0))],
        out_specs=[pl.BlockSpec((1, tm, hw), tok), pl.BlockSpec((1, tm, hw), tok),
                   pl.BlockSpec((1, tm, hw), tok),
                   pl.BlockSpec((1, 2, tm, hw), tok2),
                   pl.BlockSpec((1, tm, hw), tok), pl.BlockSpec((1, tm, hw), tok),
                   pl.BlockSpec((1, tm, hw), tok), pl.BlockSpec((1, tm // CHUNK, ng // 2, LANE), tok4)],
        out_shape=[f(F32, bsz, t, hw), f(BF16, bsz, t, hw), f(BF16, bsz, t, hw),
                   f(F32, bsz, 2, t, hw),
                   f(BF16, bsz, t, hw), f(BF16, bsz, t, hw), f(BF16, bsz, t, hw),
                   f(F32, bsz, t // CHUNK, ng // 2, LANE)],
        compiler_params=_cparams(("parallel", "parallel")),
        name="inproj",
    )(s, mods, nw, w_in, w_gate_t, gate_b, lb_logits)


def _conv_kernel(x_ref, w_ref, b_ref, o_ref, *, width, q_tiles):
    x = x_ref[0].astype(F32)
    t = x.shape[0]
    col = lax.broadcasted_iota(jnp.int32, x.shape, 0) % width
    xl = jnp.where(col == 0, 0.0, pltpu.roll(x, 1, 0))
    xr = jnp.where(col == width - 1, 0.0, pltpu.roll(x, t - 1, 0))

    def row(di):
        return (w_ref[di, 0:1, :] * xl + w_ref[di, 1:2, :] * x + w_ref[di, 2:3, :] * xr)

    y = row(1)
    if t > width:
        pad = jnp.zeros((width, x.shape[1]), F32)
        y = y + jnp.concatenate([pad, row(0)[:t - width]], axis=0)
        y = y + jnp.concatenate([row(2)[width:], pad], axis=0)
    y = _silu(y + b_ref[...])
    qscale = jnp.where(pl.program_id(1) < q_tiles, ML_QK ** -0.5, 1.0)
    o_ref[0] = (y * qscale).astype(BF16)


def _conv(x, w, b, width):
    bsz, t, ch = x.shape
    return pl.pallas_call(
        functools.partial(_conv_kernel, width=width, q_tiles=ML_HEADS * ML_QK // LANE),
        grid=(bsz, ch // LANE),
        in_specs=[pl.BlockSpec((1, t, LANE), lambda b_, c: (b_, 0, c)),
                  pl.BlockSpec((3, 3, LANE), lambda b_, c: (0, 0, c)),
                  pl.BlockSpec((1, LANE), lambda b_, c: (0, c))],
        out_specs=pl.BlockSpec((1, t, LANE), lambda b_, c: (b_, 0, c)),
        out_shape=jax.ShapeDtypeStruct((bsz, t, ch), BF16),
        compiler_params=_cparams(("parallel", "parallel")),
        name="conv",
    )(x, w, b)


def _positions():
    t = np.arange(CHUNK)
    return [t, CHUNK - 1 - t]


def _hgrn2_masks():
    ds, vs, ws = [], [], []
    for p in _positions():
        le = (p[None, :] <= p[:, None]).astype(np.float32)
        order = [p[SUB * j] // SUB for j in range(N_SUB)]
        ref = np.zeros((32, CHUNK), np.float32)
        valid = np.zeros((N_SUB * N_SUB, 1), np.float32)
        for j in range(N_SUB):
            ref[j] = p <= SUB * order[j] + SUB // 2 - 1
            ref[8 + j] = p <= SUB * order[j] + SUB - 1
        ref[N_SUB] = 1.0
        for j in range(N_SUB):
            for i in range(N_SUB):
                if order[i] < order[j]:
                    ref[16 + N_SUB * j + i] = ref[j] - ref[8 + i]
                    valid[N_SUB * j + i] = 1.0
        ds.append(np.concatenate([le, ref], axis=0))
        vs.append(np.broadcast_to(valid, (N_SUB * N_SUB, HG_HEADS * HG_DIM)))
        tok_blk = np.arange(CHUNK)[:, None] // SUB
        ws.append(np.concatenate([(tok_blk == j) * le for j in range(N_SUB)], axis=1))
    return (np.stack(ds).astype(np.float32), np.stack(vs).astype(np.float32),
            np.stack(ws).astype(np.float32))


def _mlstm_masks():
    le2s, let2s = [], []
    for p in _positions():
        le = (p[None, :] <= p[:, None]).astype(np.float32)
        le2s.append(np.concatenate([le, le], axis=1))
        let2 = np.zeros((2 * CHUNK, 2 * CHUNK), np.float32)
        let2[:CHUNK, :CHUNK] = le.T
        let2[CHUNK:, CHUNK:] = le.T
        let2s.append(let2)
    return np.stack(le2s), np.stack(let2s)


def _pipelined(n_groups, group, stages_of, step_of):
    queue = []
    for g in range(n_groups):
        for _ in stages_of(g):
            if queue:
                queue.pop(0)()
            yield True
        while queue:
            queue.pop(0)()
            yield True
        queue = [functools.partial(step_of, g * group + i) for i in range(group)]
    for step in queue:
        step()
        yield True


def _interleave(programs):
    live = list(programs)
    while live:
        live = [p for p in live if next(p, None) is not None]


def _chunk_rows(c, nc, direction):
    cc = jnp.where(direction == 0, c, nc - 1 - c)
    return pl.multiple_of(cc * CHUNK, CHUNK)


def _hgrn2_program(q_ref, v_ref, lf_ref, d_ref, pv_ref, w_ref, s0_ref, o_ref, st_ref,
                   u_ref, qin_ref, g_ref, oi_ref, bx_ref, kx_ref, vx_ref, *, nc):
    direction = pl.program_id(1)

    @pl.when(pl.program_id(2) == 0)
    def _():
        st_ref[...] = s0_ref[...]

    dmat = d_ref[0].astype(BF16)
    pair_valid = pv_ref[0]
    wmask = w_ref[0] > 0.0
    L = CHUNK
    hw = q_ref.shape[-1]
    upper_rows = lax.broadcasted_iota(jnp.int32, (L, 2 * L), 0) < L // 2

    group = min(GROUP, nc)

    def intra(grp):
        cs = [grp * group + i for i in range(group)]
        rows = [pl.ds(_chunk_rows(c, nc, direction), L) for c in cs]
        items = [(i, h) for i in range(group) for h in range(HG_HEADS)]
        lanes = [slice(h * HG_DIM, (h + 1) * HG_DIM) for h in range(HG_HEADS)]
        lfs = [lf_ref[0, 0, r, :] for r in rows]
        parts = [_split2(lf) for lf in lfs]
        e_all = sum(_dot(dmat, jnp.concatenate([parts[i][n] for i in range(group)], axis=1))
                    for n in range(2))
        yield
        q_mids, k_mids, k_outs, vbs, spans = [], [], [], [], []
        for i in range(group):
            e = e_all[:, i * hw:(i + 1) * hw]
            q = q_ref[0, rows[i], :]
            k = 1.0 - jnp.exp2(lfs[i])
            b = e[0:L]
            b_tot = e[L + N_SUB:L + N_SUB + 1]
            sub_rows = lambda r0: jnp.concatenate(
                [jnp.broadcast_to(e[r0 + j:r0 + j + 1], (SUB, hw)) for j in range(N_SUB)], axis=0)
            mid_rows = sub_rows(L)
            end_rows = sub_rows(L + 8)
            to_mid = b - mid_rows
            spans.append(jnp.max(jnp.abs(to_mid)))
            q_mids.append((q * jnp.exp2(jnp.minimum(to_mid, EXP2_CLAMP))).astype(BF16))
            k_diag = k * jnp.exp2(jnp.minimum(-to_mid, EXP2_CLAMP))
            k_end = k * jnp.exp2(end_rows - b)
            cross = jnp.exp2(e[L + 16:L + 32]) * pair_valid
            slabs = []
            for j in range(N_SUB):
                for ib in range(N_SUB):
                    rs = slice(ib * SUB, (ib + 1) * SUB)
                    row = N_SUB * j + ib
                    slabs.append(k_diag[rs] if ib == j else k_end[rs] * cross[row:row + 1])
            k_mids.append(jnp.concatenate(slabs, axis=0).astype(BF16))
            k_outs.append((k * jnp.exp2(b_tot - b)).astype(BF16))
            vbs.append(v_ref[0, rows[i], :])
            qin_ref[cs[i]] = (q * jnp.exp2(b)).astype(BF16)
            g_ref[cs[i]] = jnp.broadcast_to(jnp.exp2(b_tot), (8, hw))
            yield
        scores = [_dot_nt(q_mids[i][:, lanes[h]], k_mids[i][:, lanes[h]]) for i, h in items]
        yield
        a_s = [jnp.where(wmask, s, 0.0) for s in scores]
        a_s = [jnp.where(upper_rows, a[:, :2 * L], a[:, 2 * L:]).astype(BF16) for a in a_s]
        v2s = [jnp.concatenate([vb, vb], axis=0) for vb in vbs]
        outs = [_dot(a_s[n], v2s[i][:, lanes[h]]) for n, (i, h) in enumerate(items)]
        yield
        for i in range(group):
            oi_ref[cs[i]] = jnp.concatenate(outs[i * HG_HEADS:(i + 1) * HG_HEADS], axis=1)
        yield
        for i, h in items:
            u_ref[cs[i], h] = _dot_tn(vbs[i][:, lanes[h]], k_outs[i][:, lanes[h]])

        @pl.when(functools.reduce(jnp.maximum, spans) > EXP2_CLAMP)
        def _():
            for i in range(group):
                exact_intra(cs[i], rows[i])
        yield

    def exact_intra(c, rows):
        lf = lf_ref[0, 0, rows, :]
        bx_ref[...] = sum(_dot(dmat[0:L], x) for x in _split3(lf))
        kx_ref[...] = 1.0 - jnp.exp2(lf)
        vx_ref[...] = v_ref[0, rows, :].astype(F32)
        q = q_ref[0, rows, :]
        tok = lax.broadcasted_iota(jnp.int32, (L, 1), 0)
        pos = jnp.where(direction == 0, tok, L - 1 - tok)

        def add_source(s, acc):
            b_s = bx_ref[pl.ds(s, 1), :]
            z = q * kx_ref[pl.ds(s, 1), :] * jnp.exp2(jnp.minimum(bx_ref[...] - b_s, 0.0))
            v_s = vx_ref[pl.ds(s, 1), :]
            seen = pos >= jnp.where(direction == 0, s, L - 1 - s)
            cols = [jnp.where(seen, jnp.sum(z[:, ln], axis=1, keepdims=True), 0.0) * v_s[:, ln]
                    for ln in (slice(h * HG_DIM, (h + 1) * HG_DIM) for h in range(HG_HEADS))]
            return acc + jnp.concatenate(cols, axis=1)

        oi_ref[c] = lax.fori_loop(0, L, add_source, jnp.zeros((L, hw), F32))

    def carry_state(c):
        rows = pl.ds(_chunk_rows(c, nc, direction), L)
        q_in = qin_ref[c]
        g = g_ref[c][0:1]
        outs, states = [], []
        for h in range(HG_HEADS):
            ln = slice(h * HG_DIM, (h + 1) * HG_DIM)
            st = st_ref[0, 0, h]
            outs.append(_dot_nt(q_in[:, ln], st.astype(BF16)))
            states.append(g[:, ln] * st + u_ref[c, h])
        o_ref[0, 0, rows, :] = (oi_ref[c] + jnp.concatenate(outs, axis=1)).astype(BF16)
        for h in range(HG_HEADS):
            st_ref[0, 0, h] = states[h]

    return _pipelined(nc // group, group, intra, carry_state)


N_HG = (7, 2, 7)
N_ML = (7, 3, 6)


def _mlstm_program(qk_ref, v_ref, g_ref, le2_ref, let2_ref, c0_ref, m0_ref,
                   o_ref, c_ref, mm_ref,
                   p_ref, u_ref, qs_ref, ct_ref, mloc_ref, sc_ref, *, nc):
    direction = pl.program_id(1)

    @pl.when(pl.program_id(2) == 0)
    def _():
        c_ref[...] = c0_ref[...]
        mm_ref[...] = m0_ref[...]

    le2 = le2_ref[0].astype(BF16)
    let2 = let2_ref[0].astype(BF16)
    causal = le2_ref[0] > 0.0
    L = CHUNK
    neg = -jnp.inf
    lane = lax.broadcasted_iota(jnp.int32, (1, LANE), 1)
    seg = [lane < L, lane >= L]
    lane_t = lax.broadcasted_iota(jnp.int32, (L, LANE), 1) < L
    eye = (lax.broadcasted_iota(jnp.int32, (LANE, LANE), 0)
           == lax.broadcasted_iota(jnp.int32, (LANE, LANE), 1))
    npair = ML_HEADS // 2
    ones_v = jnp.ones((L, ML_V), BF16)

    group = min(ML_GROUP, nc)

    def intra(grp):
        cs = [grp * group + i for i in range(group)]
        ccs = [jnp.where(direction == 0, c, nc - 1 - c) for c in cs]
        rows = [pl.ds(pl.multiple_of(cc * L, L), L) for cc in ccs]
        pairs = [(i, p) for i in range(group) for p in range(npair)]
        gs = [jnp.where(direction == 0, g_ref[0, cc][0:4], g_ref[0, cc][4:8]) for cc in ccs]
        pad = [jnp.zeros((-2 * group % 8, LANE), F32)] if 2 * group % 8 else []
        i2 = jnp.concatenate([g[0:2] for g in gs] + pad, axis=0)
        lf2 = _log_sigmoid(jnp.concatenate([g[2:4] for g in gs] + pad, axis=0))
        parts = _split3(lf2)
        r2 = i2 - sum(_dot(x, let2) for x in parts)
        g_a = jnp.max(jnp.where(seg[0], r2, neg), axis=1, keepdims=True)
        g_b = jnp.max(jnp.where(seg[1], r2, neg), axis=1, keepdims=True)
        e_w = jnp.exp(r2 - jnp.where(seg[0], g_a, g_b))
        bt_a = jnp.sum(jnp.where(seg[0], lf2, 0.0), axis=1, keepdims=True)
        bt_b = jnp.sum(jnp.where(seg[1], lf2, 0.0), axis=1, keepdims=True)
        yield

        def head_rows(x, i):
            zero = jnp.zeros_like(x[0:1])
            return jnp.concatenate(
                [jnp.broadcast_to(jnp.where(seg[h % 2], x[2 * i + h // 2:2 * i + h // 2 + 1], zero),
                                  (LANE, LANE)) for h in range(ML_HEADS)], axis=0)

        c_ts = [sum(_dot_nt(le2, head_rows(x, i)) for x in parts) for i in range(group)]
        for i in range(group):
            ct_ref[cs[i]] = c_ts[i]
            sc_ref[cs[i]] = jnp.concatenate(
                [jnp.broadcast_to(x[2 * i + p:2 * i + p + 1], (1, LANE))
                 for x2 in ((bt_a, bt_b), (g_a, g_b)) for p in range(npair) for x in x2], axis=0)
        yield
        qks = [qk_ref[0, r, :] for r in rows]
        vbs = [v_ref[0, r, :] for r in rows]
        q_ps = [qks[i][:, p * LANE:(p + 1) * LANE] for i, p in pairs]
        k_ps = [qks[i][:, (npair + p) * LANE:(npair + p + 1) * LANE] for i, p in pairs]
        zb = jnp.zeros((L, LANE), BF16)
        kbs = [jnp.concatenate([jnp.where(lane_t, k, zb), jnp.where(lane_t, zb, k)], axis=0) for k in k_ps]
        scores = [_dot_nt(q, kb) for q, kb in zip(q_ps, kbs)]
        yield
        ws = []
        for n, (i, p) in enumerate(pairs):
            ha, hb = 2 * p, 2 * p + 1
            t1 = jnp.where(lane_t, c_ts[i][:, ha * ML_V:(ha + 1) * ML_V], c_ts[i][:, hb * ML_V:(hb + 1) * ML_V])
            d_log = jnp.where(causal, t1 + r2[2 * i + p:2 * i + p + 1], neg)
            m_a = jnp.max(jnp.where(lane_t, d_log, neg), axis=1, keepdims=True)
            m_b = jnp.max(jnp.where(lane_t, neg, d_log), axis=1, keepdims=True)
            mloc_ref[cs[i], :, ha * ML_V:(ha + 1) * ML_V] = jnp.broadcast_to(m_a, (L, ML_V))
            mloc_ref[cs[i], :, hb * ML_V:(hb + 1) * ML_V] = jnp.broadcast_to(m_b, (L, ML_V))
            ws.append(jnp.exp(d_log - jnp.where(lane_t, m_a, m_b)))
        yield
        a_s = [(s_ * w).astype(BF16) for s_, w in zip(scores, ws)]
        a_stacks = [jnp.concatenate([jnp.where(lane_t, a, zb), jnp.where(lane_t, zb, a)], axis=0) for a in a_s]
        v_stacks = [jnp.concatenate(
            [jnp.concatenate([vbs[i][:, 2 * p * ML_V:(2 * p + 1) * ML_V], ones_v], axis=1),
             jnp.concatenate([vbs[i][:, (2 * p + 1) * ML_V:(2 * p + 2) * ML_V], ones_v], axis=1)], axis=0)
            for i, p in pairs]
        for n, (i, p) in enumerate(pairs):
            p_ref[cs[i], p] = _dot(a_stacks[n], v_stacks[n])
            qs_ref[cs[i], p] = jnp.concatenate(
                [jnp.where(lane_t, q_ps[n], zb), jnp.where(lane_t, zb, q_ps[n])], axis=0)
        yield
        diags = [jnp.where(eye, jnp.broadcast_to(e_w[2 * i + p:2 * i + p + 1], (LANE, LANE)), 0.0).astype(BF16)
                 for i, p in pairs]
        kb_ws = [_dot(d, kb).astype(BF16) for d, kb in zip(diags, kbs)]
        for n, (i, p) in enumerate(pairs):
            u_ref[cs[i], p] = _dot_tn(kb_ws[n], v_stacks[n])
        yield

    def carry_state(c):
        cc = jnp.where(direction == 0, c, nc - 1 - c)
        rows = pl.ds(pl.multiple_of(cc * L, L), L)
        m_all = mm_ref[0, 0]
        sc = sc_ref[c]
        c_t = ct_ref[c]
        m_loc = mloc_ref[c]
        outs, states, m_rows = [], [], []
        for p in range(ML_HEADS // 2):
            c_pair = c_ref[0, 0, p]
            inter = _dot(qs_ref[c, p], c_pair.astype(BF16))
            intra_p = p_ref[c, p]
            inc = u_ref[c, p]
            new_rows = []
            for hh in range(2):
                h = 2 * p + hh
                rs = slice(hh * L, (hh + 1) * L)
                ln = slice(h * ML_V, (h + 1) * ML_V)
                m_in = m_all[h:h + 1]
                m_t = jnp.maximum(m_loc[:, ln], c_t[:, ln] + m_in)
                alpha = jnp.exp(m_loc[:, ln] - m_t)
                beta = jnp.exp(c_t[:, ln] + m_in - m_t)
                num = alpha * intra_p[rs, :ML_V] + beta * inter[rs, :ML_V]
                den = alpha * intra_p[rs, ML_V:] + beta * inter[rs, ML_V:]
                outs.append(num / jnp.maximum(jnp.abs(den), jnp.exp(-m_t)))
                b_tot = sc[h:h + 1]
                m_new = b_tot + jnp.maximum(m_in, sc[ML_HEADS + h:ML_HEADS + h + 1])
                carry_w = jnp.exp(b_tot + m_in - m_new)
                inc_w = jnp.exp(b_tot + sc[ML_HEADS + h:ML_HEADS + h + 1] - m_new)
                cw2 = jnp.concatenate([carry_w, carry_w], axis=1)
                iw2 = jnp.concatenate([inc_w, inc_w], axis=1)
                new_rows.append(cw2 * c_pair[rs] + iw2 * inc[rs])
                m_rows.append(m_new)
            states.append(jnp.concatenate(new_rows, axis=0))
        o_ref[0, 0, rows, :] = jnp.concatenate(outs, axis=1).astype(BF16)
        for p in range(ML_HEADS // 2):
            c_ref[0, 0, p] = states[p]
        mm_ref[0, 0] = jnp.concatenate(m_rows + [m_all[ML_HEADS:]], axis=0)

    return _pipelined(nc // group, group, intra, carry_state)


def _scan_kernel(*refs, nc):
    hg_in, ml_in, hg_out, ml_out, hg_scr, ml_scr = (
        refs[a:b] for a, b in zip(np.cumsum([0, N_HG[0], N_ML[0], N_HG[1], N_ML[1], N_HG[2]]),
                                  np.cumsum([N_HG[0], N_ML[0], N_HG[1], N_ML[1], N_HG[2], N_ML[2]])))
    _interleave([_hgrn2_program(*hg_in, *hg_out, *hg_scr, nc=nc),
                 _mlstm_program(*ml_in, *ml_out, *ml_scr, nc=nc)])


def _scans(hq, hv, hlf, qk, mv, gates, s0, c0, m0, dmat, pair_valid, wmask, le2, let2):
    bsz, t, hw = hq.shape
    nc = min(SCAN_CHUNKS, t // CHUNK)
    tb = nc * CHUNK
    nblk = t // tb
    npair = ML_HEADS // 2

    def blk(j, d):
        return jnp.where(d == 0, j, nblk - 1 - j)

    tok = lambda b, d, j: (b, blk(j, d), 0)
    tok_d = lambda b, d, j: (b, d, blk(j, d), 0)
    per_dir = lambda a: pl.BlockSpec((1,) + a.shape[1:], lambda b, d, j: (d,) + (0,) * (a.ndim - 1))
    state = lambda a: pl.BlockSpec((1, 1) + a.shape[2:], lambda b, d, j: (b, d) + (0,) * (a.ndim - 2))
    out_tok = pl.BlockSpec((1, 1, tb, hw), tok_d)
    hg_in = [pl.BlockSpec((1, tb, hw), tok), pl.BlockSpec((1, tb, hw), tok), pl.BlockSpec((1, 1, tb, hw), tok_d),
             per_dir(dmat), per_dir(pair_valid), per_dir(wmask), state(s0)]
    ml_in = [pl.BlockSpec((1, tb, qk.shape[-1]), tok), pl.BlockSpec((1, tb, hw), tok),
             pl.BlockSpec((1, nc) + gates.shape[2:], lambda b, d, j: (b, blk(j, d), 0, 0)),
             per_dir(le2), per_dir(let2), state(c0), state(m0)]
    sds = jax.ShapeDtypeStruct
    hg_scr = [pltpu.VMEM((nc,) + s0.shape[2:], F32),
              pltpu.VMEM((nc, CHUNK, hw), BF16),
              pltpu.VMEM((nc, 8, hw), F32),
              pltpu.VMEM((nc, CHUNK, hw), F32),
              pltpu.VMEM((CHUNK, hw), F32),
              pltpu.VMEM((CHUNK, hw), F32),
              pltpu.VMEM((CHUNK, hw), F32)]
    ml_scr = [pltpu.VMEM((nc, npair, 2 * CHUNK, 2 * ML_V), F32),
              pltpu.VMEM((nc, npair, 2 * ML_QK, 2 * ML_V), F32),
              pltpu.VMEM((nc, npair, 2 * CHUNK, LANE), BF16),
              pltpu.VMEM((nc, CHUNK, hw), F32),
              pltpu.VMEM((nc, CHUNK, hw), F32),
              pltpu.VMEM((nc, 8, LANE), F32)]
    assert (len(hg_in), 2, len(hg_scr)) == N_HG and (len(ml_in), 3, len(ml_scr)) == N_ML
    return pl.pallas_call(
        functools.partial(_scan_kernel, nc=nc),
        grid=(bsz, 2, nblk),
        in_specs=hg_in + ml_in,
        out_specs=[out_tok, state(s0), out_tok, state(c0), state(m0)],
        out_shape=[sds((bsz, 2, t, hw), BF16), sds(s0.shape, F32),
                   sds((bsz, 2, t, hw), BF16), sds(c0.shape, F32), sds(m0.shape, F32)],
        scratch_shapes=hg_scr + ml_scr,
        compiler_params=_cparams(("parallel", "parallel", "arbitrary")),
        name="scans",
    )(hq, hv, hlf, dmat, pair_valid, wmask, s0, qk, mv, gates, le2, let2, c0, m0)


def _head_rms(x, heads):
    dh = x.shape[-1] // heads
    return jnp.concatenate([_rms(x[:, h * dh:(h + 1) * dh]) for h in range(heads)], axis=1)


def _mix_ffn_kernel(oh_ref, om_ref, hg_ref, mo_ref, x_ref, mod_ref, hnw_ref, mnw_ref, wo_ref,
                    nw_ref, w1_ref, w2_ref, fw_ref, o_ref, a_ref, x_scr, h_scr, *, d_ff, tf):
    mix_gate = mod_ref[0, 0:1, :]
    mods = [mod_ref[0, k:k + 1, :] for k in range(1, 4)]

    def stages(rs):
        def mixed(pr):
            oh = oh_ref[0, 0, pr, :].astype(F32) + oh_ref[0, 1, pr, :].astype(F32)
            om = om_ref[0, 0, pr, :].astype(F32) + om_ref[0, 1, pr, :].astype(F32)
            hg_out = _head_rms(oh, HG_HEADS) * hnw_ref[...] * _silu(hg_ref[0, pr, :].astype(F32))
            ml_out = _sigmoid(mo_ref[0, pr, :].astype(F32)) * (_head_rms(om, ML_HEADS) * mnw_ref[...])
            merged = jnp.concatenate([hg_out, ml_out], axis=1).astype(BF16)
            return x_ref[0, pr, :] + mix_gate * _dot(merged, wo_ref[...])

        def put(y):
            o_ref[0, rs, :] = _rms(y) * fw_ref[...]

        return _Stages(_ffn_stages(mixed, put, mods, nw_ref, w1_ref, w2_ref, a_ref, x_scr, h_scr,
                                   rs, d_ff, tf), _ffn_lead(rs, d_ff, tf))

    _staggered([stages(rs) for rs in _sub_tiles(x_ref.shape[1])])


def _mix_ffn(oh, om, hg, mo, x, mods, hnw, mnw, w_out, nw, w1, w2, fw):
    bsz, t, d = x.shape
    hw = hg.shape[-1]
    d_ff = w2.shape[0]
    tm = min(TOKEN_TILE, t)
    tok = lambda b, i: (b, i, 0)
    tok2 = lambda b, i: (b, 0, i, 0)
    const = lambda b, i: (0, 0)
    resident = lambda a: pl.BlockSpec(a.shape, const, pipeline_mode=pl.Buffered(1))
    return pl.pallas_call(
        functools.partial(_mix_ffn_kernel, d_ff=d_ff, tf=FF_TILE),
        grid=(bsz, t // tm),
        in_specs=[pl.BlockSpec((1, 2, tm, hw), tok2), pl.BlockSpec((1, 2, tm, hw), tok2),
                  pl.BlockSpec((1, tm, hw), tok), pl.BlockSpec((1, tm, hw), tok),
                  pl.BlockSpec((1, tm, d), tok),
                  pl.BlockSpec((1, 4, d), lambda b, i: (b, 0, 0)),
                  pl.BlockSpec((1, hw), const), pl.BlockSpec((1, hw), const), resident(w_out),
                  pl.BlockSpec((1, d), const), resident(w1), resident(w2), pl.BlockSpec((1, d), const)],
        out_specs=pl.BlockSpec((1, tm, d), tok),
        out_shape=jax.ShapeDtypeStruct((bsz, t, d), F32),
        scratch_shapes=[pltpu.VMEM((tm, d_ff), BF16),
                        pltpu.VMEM((tm, d), F32),
                        pltpu.VMEM((tm, d), BF16)],
        compiler_params=_cparams(("parallel", "parallel")),
        name="mix_ffn_final",
    )(oh, om, hg, mo, x, mods, hnw, mnw, w_out, nw, w1, w2, fw)


def kernel(x, c, ctx, c_ctx, w_mod, b_mod, norm1_w, ffn1_w1, ffn1_w2, norm2_w, w_in, ml_gate_b,
           ml_conv_w, ml_conv_b, hg_lb_logits, hg_norm_w, ml_norm_w, w_out, norm3_w, ffn2_w1, ffn2_w2,
           final_norm_w):
    bsz, seq, d = x.shape
    n_ctx = ctx.shape[1]
    assert w_mod.shape[0] == 1, "single-layer kernel"
    assert seq % (GRID_W * 8) == 0 and n_ctx % CHUNK == 0
    hw = HG_HEADS * HG_DIM
    ng = ml_gate_b.shape[-1]

    rows = -(-(bsz + 1) // 8) * 8
    cvec = jnp.zeros((rows, d), F32).at[:bsz].set(c).at[bsz].set(c_ctx)
    mods = _modulation(cvec, w_mod[0], b_mod[0][None, :]).reshape(rows, N_MOD, d)
    lat_row = lambda b: b
    ctx_row = lambda b: bsz

    row = lambda a: a.reshape(1, -1)
    w1a, w2a = ffn1_w1[0], ffn1_w2[0]
    x1 = _ffn(x, mods[:, 0:3], lat_row, row(norm1_w[0]), w1a, w2a)
    s1 = _ffn(ctx, mods[:, 0:3], ctx_row, row(norm1_w[0]), w1a, w2a)

    w_in_b = w_in[0].T
    gate_order = np.arange(ng).reshape(2, 2, ML_HEADS // 2, 2).transpose(3, 0, 1, 2).reshape(-1)
    w_gate_t = w_in[0][:, 8 * hw:].T[gate_order].astype(BF16)
    gate_b = ml_gate_b[0][gate_order].reshape(ng, 1)
    dmat, pair_valid, wmask = (jnp.asarray(a) for a in _hgrn2_masks())
    le2, let2 = (jnp.asarray(a) for a in _mlstm_masks())
    conv_w, conv_b = ml_conv_w[0], row(ml_conv_b[0])
    npair = ML_HEADS // 2

    def mixer_scans(s, mod_row, width, s0, c0, m0):
        hq, hv, hg, hlf, mqk, mv, mo, gates = _inproj(
            s, mods[:, 3:5], mod_row, row(norm2_w[0]), w_in_b, w_gate_t, gate_b, hg_lb_logits)
        qk = _conv(mqk, conv_w, conv_b, width)
        oh, s_fin, om, c_fin, m_fin = _scans(hq, hv, hlf, qk, mv, gates, s0, c0, m0,
                                             dmat, pair_valid, wmask, le2, let2)
        return oh, om, hg, mo, s_fin, c_fin, m_fin

    s0 = jnp.zeros((bsz, 2, HG_HEADS, HG_DIM, HG_DIM), F32)
    c0 = jnp.zeros((bsz, 2, npair, 2 * ML_QK, 2 * ML_V), F32)
    m0 = jnp.zeros((bsz, 2, 8, LANE), F32)
    _, _, _, _, s_ctx, c_ctx_state, m_ctx = mixer_scans(s1, ctx_row, n_ctx, s0, c0, m0)
    oh, om, hg, mo, _, _, _ = mixer_scans(x1, lat_row, GRID_W, s_ctx, c_ctx_state, m_ctx)

    return _mix_ffn(oh, om, hg, mo, x1, mods[:, 5:9], row(hg_norm_w[0]), row(ml_norm_w[0]),
                    w_out[0].astype(BF16), row(norm3_w[0]), ffn2_w1[0], ffn2_w2[0], row(final_norm_w))
```

```python
import functools

import numpy as np
import jax
import jax.numpy as jnp
from jax import lax
from jax.experimental import pallas as pl
from jax.experimental.pallas import tpu as pltpu

F32 = jnp.float32
BF16 = jnp.bfloat16

EPS = 1e-6
CHUNK = 64
GRID_W = 64
N_MOD = 9
HG_HEADS = 4
HG_DIM = 128
ML_HEADS = 4
ML_QK = 64
ML_V = 128
SUB = 16
N_SUB = CHUNK // SUB
GROUP = 4
ML_GROUP = 4
SCAN_CHUNKS = 8
TOKEN_TILE = 512
ROW_SUB = 256
FF_TILE = 256
PIECE = 128
LOG2E = 1.4426950408889634
EXP2_CLAMP = 115.0
LANE = 128
SUBLANE = 8
VMEM_LIMIT = 56 * 1024 * 1024


def _cparams(sem):
    return pltpu.CompilerParams(dimension_semantics=sem, vmem_limit_bytes=VMEM_LIMIT)


def _dot(a, b):
    return jnp.dot(a, b, preferred_element_type=F32)


def _dot_nt(a, b):
    return lax.dot_general(a, b, (((1,), (1,)), ((), ())), preferred_element_type=F32)


def _dot_tn(a, b):
    return lax.dot_general(a, b, (((0,), (0,)), ((), ())), preferred_element_type=F32)


def _sigmoid(x):
    return 1.0 / (1.0 + jnp.exp(-x))


def _silu(x):
    return x * _sigmoid(x)


def _log_sigmoid(x):
    return jnp.minimum(x, 0.0) - jnp.log(1.0 + jnp.exp(-jnp.abs(x)))


def _split3(x):
    hi = x.astype(BF16)
    r = x - hi.astype(F32)
    mid = r.astype(BF16)
    lo = (r - mid.astype(F32)).astype(BF16)
    return hi, mid, lo


def _split2(x):
    hi = x.astype(BF16)
    return hi, (x - hi.astype(F32)).astype(BF16)


def _rms(x):
    return x * lax.rsqrt(jnp.mean(x * x, axis=-1, keepdims=True) + EPS)


def _mod_kernel(c_ref, w_ref, b_ref, o_ref):
    a = _silu(c_ref[...])
    a_hi = a.astype(BF16)
    a_lo = (a - a_hi.astype(F32)).astype(BF16)
    w = w_ref[...]
    w_hi = w.astype(BF16)
    w_lo = (w - w_hi.astype(F32)).astype(BF16)
    o_ref[...] = _dot(a_hi, w_hi) + _dot(a_hi, w_lo) + _dot(a_lo, w_hi) + b_ref[...]


def _modulation(cvec, w_mod, b_mod):
    rows, d = cvec.shape
    n = w_mod.shape[1]
    tn = 1024
    return pl.pallas_call(
        _mod_kernel,
        grid=(n // tn,),
        in_specs=[pl.BlockSpec((rows, d), lambda j: (0, 0)),
                  pl.BlockSpec((d, tn), lambda j: (0, j)),
                  pl.BlockSpec((1, tn), lambda j: (0, j))],
        out_specs=pl.BlockSpec((rows, tn), lambda j: (0, j)),
        out_shape=jax.ShapeDtypeStruct((rows, n), F32),
        compiler_params=_cparams(("arbitrary",)),
        name="modulation",
    )(cvec, w_mod, b_mod)


def _staggered(tiles):
    for _ in range(tiles[0].lead):
        next(tiles[0].gen)
    live = [t.gen for t in tiles]
    while live:
        live = [g for g in live if next(g, None) is not None]


class _Stages:
    def __init__(self, gen, lead):
        self.gen, self.lead = gen, lead


def _ffn_stages(get_x, put_y, mods, nw_ref, w1_ref, w2_ref, a_ref, x_scr, h_scr, rs, d_ff, tf):
    shift, scale, gate = mods
    for p0 in range(rs.start, rs.stop, PIECE):
        pr = slice(p0, min(p0 + PIECE, rs.stop))
        x = get_x(pr)
        x_scr[pr, :] = x
        h_scr[pr, :] = (_rms(x) * nw_ref[...] * (1.0 + scale) + shift).astype(BF16)
        yield True
    h = h_scr[rs, :]
    for c in range(d_ff // tf):
        g = _dot(h, w1_ref[:, c * tf:(c + 1) * tf].astype(BF16))
        u = _dot(h, w1_ref[:, d_ff + c * tf:d_ff + (c + 1) * tf].astype(BF16))
        a_ref[rs, c * tf:(c + 1) * tf] = (_silu(g) * u).astype(BF16)
        yield True
    y = x_scr[rs, :] + 0.5 * gate * _dot(a_ref[rs, :], w2_ref[...].astype(BF16))
    yield True
    put_y(y)
    yield True


def _sub_tiles(rows):
    sub = min(ROW_SUB, rows)
    return [slice(r0, r0 + sub) for r0 in range(0, rows, sub)]


def _ffn_lead(rs, d_ff, tf):
    return -(-(rs.stop - rs.start) // PIECE) + d_ff // tf // 2


def _ffn_kernel(x_ref, mod_ref, nw_ref, w1_ref, w2_ref, o_ref, a_ref, x_scr, h_scr, *, d_ff, tf):
    mods = [mod_ref[0, k:k + 1, :] for k in range(3)]

    def stages(rs):
        def put(y):
            o_ref[0, rs, :] = y
        return _Stages(_ffn_stages(lambda pr: x_ref[0, pr, :], put, mods, nw_ref, w1_ref, w2_ref, a_ref,
                                   x_scr, h_scr, rs, d_ff, tf), _ffn_lead(rs, d_ff, tf))

    _staggered([stages(rs) for rs in _sub_tiles(x_ref.shape[1])])


def _ffn(s, mods, mod_row, nw, w1, w2):
    bsz, t, d = s.shape
    d_ff = w2.shape[0]
    tm = min(TOKEN_TILE, t)
    const = lambda b, i: (0, 0)
    return pl.pallas_call(
        functools.partial(_ffn_kernel, d_ff=d_ff, tf=FF_TILE),
        grid=(bsz, t // tm),
        in_specs=[pl.BlockSpec((1, tm, d), lambda b, i: (b, i, 0)),
                  pl.BlockSpec((1, 3, d), lambda b, i: (mod_row(b), 0, 0)),
                  pl.BlockSpec((1, d), const),
                  pl.BlockSpec((d, 2 * d_ff), const, pipeline_mode=pl.Buffered(1)),
                  pl.BlockSpec((d_ff, d), const, pipeline_mode=pl.Buffered(1))],
        out_specs=pl.BlockSpec((1, tm, d), lambda b, i: (b, i, 0)),
        out_shape=jax.ShapeDtypeStruct((bsz, t, d), F32),
        scratch_shapes=[pltpu.VMEM((tm, d_ff), BF16),
                        pltpu.VMEM((tm, d), F32),
                        pltpu.VMEM((tm, d), BF16)],
        compiler_params=_cparams(("parallel", "parallel")),
        name="ffn",
    )(s, mods, nw, w1, w2)


def _inproj_kernel(x_ref, mod_ref, nw_ref, w_ref, wg_ref, gb_ref, lbl_ref,
                   hq_ref, hv_ref, hg_ref, hlf_ref, mqk_ref, mv_ref, mo_ref, gc_ref, *, hw):
    shift = mod_ref[0, 0:1, :]
    scale = mod_ref[0, 1:2, :]
    lbl = lbl_ref[...]
    e = jnp.exp(lbl - jnp.max(lbl, axis=0, keepdims=True))
    lb = e[0] / jnp.sum(e, axis=0)
    first_half = lax.broadcasted_iota(jnp.int32, (SUBLANE, LANE), 1) < CHUNK
    def stages(rs):
        h = (_rms(x_ref[0, rs, :]) * nw_ref[...] * (1.0 + scale) + shift).astype(BF16)
        yield True

        def proj(k):
            return _dot_nt(h, w_ref[k * hw:(k + 1) * hw, :].astype(BF16))

        hq_ref[0, rs, :] = _silu(proj(0)) * (HG_DIM ** -0.5)
        yield True
        hv_ref[0, rs, :] = proj(1).astype(BF16)
        yield True
        hg_ref[0, rs, :] = proj(2).astype(BF16)
        yield True
        for d in range(2):
            lbd = lb[d:d + 1, :]
            hlf_ref[0, d, rs, :] = jnp.log(lbd + (1.0 - lbd) * _sigmoid(proj(3 + d))) * LOG2E
            yield True
        mqk_ref[0, rs, :] = proj(5).astype(BF16)
        yield True
        mv_ref[0, rs, :] = proj(6).astype(BF16)
        yield True
        mo_ref[0, rs, :] = proj(7).astype(BF16)
        gt = _dot_nt(wg_ref[...], h) + gb_ref[...]
        for m in range((rs.stop - rs.start) // LANE):
            top = gt[0:8, m * LANE:(m + 1) * LANE]
            bot = gt[8:16, m * LANE:(m + 1) * LANE]
            c0 = rs.start // CHUNK + 2 * m
            gc_ref[0, c0] = jnp.where(first_half, top, pltpu.roll(bot, CHUNK, 1))
            gc_ref[0, c0 + 1] = jnp.where(first_half, pltpu.roll(top, CHUNK, 1), bot)
        yield True

    _staggered([_Stages(stages(rs), 8) for rs in _sub_tiles(x_ref.shape[1])])


def _inproj(s, mods, mod_row, nw, w_in, w_gate_t, gate_b, lb_logits):
    bsz, t, d = s.shape
    hw = HG_HEADS * HG_DIM
    ng = gate_b.shape[0]
    tm = min(TOKEN_TILE, t)
    tok4 = lambda b, i: (b, i, 0, 0)
    const = lambda b, i: (0, 0)
    tok = lambda b, i: (b, i, 0)
    tok2 = lambda b, i: (b, 0, i, 0)
    f = lambda dt, *shape: jax.ShapeDtypeStruct(shape, dt)
    return pl.pallas_call(
        functools.partial(_inproj_kernel, hw=hw),
        grid=(bsz, t // tm),
        in_specs=[pl.BlockSpec((1, tm, d), tok),
                  pl.BlockSpec((1, 2, d), lambda b, i: (mod_row(b), 0, 0)),
                  pl.BlockSpec((1, d), const),
                  pl.BlockSpec(w_in.shape, const, pipeline_mode=pl.Buffered(1)),
                  pl.BlockSpec(w_gate_t.shape, const),
                  pl.BlockSpec((ng, 1), const),
                  pl.BlockSpec(lb_logits.shape, lambda b, i: (0, 0, 0))],
        out_specs=[pl.BlockSpec((1, tm, hw), tok), pl.BlockSpec((1, tm, hw), tok),
                   pl.BlockSpec((1, tm, hw), tok),
                   pl.BlockSpec((1, 2, tm, hw), tok2),
                   pl.BlockSpec((1, tm, hw), tok), pl.BlockSpec((1, tm, hw), tok),
                   pl.BlockSpec((1, tm, hw), tok), pl.BlockSpec((1, tm // CHUNK, ng // 2, LANE), tok4)],
        out_shape=[f(F32, bsz, t, hw), f(BF16, bsz, t, hw), f(BF16, bsz, t, hw),
                   f(F32, bsz, 2, t, hw),
                   f(BF16, bsz, t, hw), f(BF16, bsz, t, hw), f(BF16, bsz, t, hw),
                   f(F32, bsz, t // CHUNK, ng // 2, LANE)],
        compiler_params=_cparams(("parallel", "parallel")),
        name="inproj",
    )(s, mods, nw, w_in, w_gate_t, gate_b, lb_logits)


def _conv_kernel(x_ref, w_ref, b_ref, o_ref, *, width, q_tiles):
    x = x_ref[0].astype(F32)
    t = x.shape[0]
    col = lax.broadcasted_iota(jnp.int32, x.shape, 0) % width
    xl = jnp.where(col == 0, 0.0, pltpu.roll(x, 1, 0))
    xr = jnp.where(col == width - 1, 0.0, pltpu.roll(x, t - 1, 0))

    def row(di):
        return (w_ref[di, 0:1, :] * xl + w_ref[di, 1:2, :] * x + w_ref[di, 2:3, :] * xr)

    y = row(1)
    if t > width:
        pad = jnp.zeros((width, x.shape[1]), F32)
        y = y + jnp.concatenate([pad, row(0)[:t - width]], axis=0)
        y = y + jnp.concatenate([row(2)[width:], pad], axis=0)
    y = _silu(y + b_ref[...])
    qscale = jnp.where(pl.program_id(1) < q_tiles, ML_QK ** -0.5, 1.0)
    o_ref[0] = (y * qscale).astype(BF16)


def _conv(x, w, b, width):
    bsz, t, ch = x.shape
    return pl.pallas_call(
        functools.partial(_conv_kernel, width=width, q_tiles=ML_HEADS * ML_QK // LANE),
        grid=(bsz, ch // LANE),
        in_specs=[pl.BlockSpec((1, t, LANE), lambda b_, c: (b_, 0, c)),
                  pl.BlockSpec((3, 3, LANE), lambda b_, c: (0, 0, c)),
                  pl.BlockSpec((1, LANE), lambda b_, c: (0, c))],
        out_specs=pl.BlockSpec((1, t, LANE), lambda b_, c: (b_, 0, c)),
        out_shape=jax.ShapeDtypeStruct((bsz, t, ch), BF16),
        compiler_params=_cparams(("parallel", "parallel")),
        name="conv",
    )(x, w, b)


def _positions():
    t = np.arange(CHUNK)
    return [t, CHUNK - 1 - t]


def _hgrn2_masks():
    ds, vs, ws = [], [], []
    for p in _positions():
        le = (p[None, :] <= p[:, None]).astype(np.float32)
        order = [p[SUB * j] // SUB for j in range(N_SUB)]
        ref = np.zeros((32, CHUNK), np.float32)
        valid = np.zeros((N_SUB * N_SUB, 1), np.float32)
        for j in range(N_SUB):
            ref[j] = p <= SUB * order[j] + SUB // 2 - 1
            ref[8 + j] = p <= SUB * order[j] + SUB - 1
        ref[N_SUB] = 1.0
        for j in range(N_SUB):
            for i in range(N_SUB):
                if order[i] < order[j]:
                    ref[16 + N_SUB * j + i] = ref[j] - ref[8 + i]
                    valid[N_SUB * j + i] = 1.0
        ds.append(np.concatenate([le, ref], axis=0))
        vs.append(np.broadcast_to(valid, (N_SUB * N_SUB, HG_HEADS * HG_DIM)))
        tok_blk = np.arange(CHUNK)[:, None] // SUB
        ws.append(np.concatenate([(tok_blk == j) * le for j in range(N_SUB)], axis=1))
    return (np.stack(ds).astype(np.float32), np.stack(vs).astype(np.float32),
            np.stack(ws).astype(np.float32))


def _mlstm_masks():
    le2s, let2s = [], []
    for p in _positions():
        le = (p[None, :] <= p[:, None]).astype(np.float32)
        le2s.append(np.concatenate([le, le], axis=1))
        let2 = np.zeros((2 * CHUNK, 2 * CHUNK), np.float32)
        let2[:CHUNK, :CHUNK] = le.T
        let2[CHUNK:, CHUNK:] = le.T
        let2s.append(let2)
    return np.stack(le2s), np.stack(let2s)


def _pipelined(n_groups, group, stages_of, step_of):
    queue = []
    for g in range(n_groups):
        for _ in stages_of(g):
            if queue:
                queue.pop(0)()
            yield True
        while queue:
            queue.pop(0)()
            yield True
        queue = [functools.partial(step_of, g * group + i) for i in range(group)]
    for step in queue:
        step()
        yield True


def _interleave(programs):
    live = list(programs)
    while live:
        live = [p for p in live if next(p, None) is not None]


def _chunk_rows(c, nc, direction):
    cc = jnp.where(direction == 0, c, nc - 1 - c)
    return pl.multiple_of(cc * CHUNK, CHUNK)


def _hgrn2_program(q_ref, v_ref, lf_ref, d_ref, pv_ref, w_ref, s0_ref, o_ref, st_ref,
                   u_ref, qin_ref, g_ref, oi_ref, bx_ref, kx_ref, vx_ref, *, nc):
    direction = pl.program_id(1)

    @pl.when(pl.program_id(2) == 0)
    def _():
        st_ref[...] = s0_ref[...]

    dmat = d_ref[0].astype(BF16)
    pair_valid = pv_ref[0]
    wmask = w_ref[0] > 0.0
    L = CHUNK
    hw = q_ref.shape[-1]
    upper_rows = lax.broadcasted_iota(jnp.int32, (L, 2 * L), 0) < L // 2

    group = min(GROUP, nc)

    def intra(grp):
        cs = [grp * group + i for i in range(group)]
        rows = [pl.ds(_chunk_rows(c, nc, direction), L) for c in cs]
        items = [(i, h) for i in range(group) for h in range(HG_HEADS)]
        lanes = [slice(h * HG_DIM, (h + 1) * HG_DIM) for h in range(HG_HEADS)]
        lfs = [lf_ref[0, 0, r, :] for r in rows]
        parts = [_split2(lf) for lf in lfs]
        e_all = sum(_dot(dmat, jnp.concatenate([parts[i][n] for i in range(group)], axis=1))
                    for n in range(2))
        yield
        q_mids, k_mids, k_outs, vbs, spans = [], [], [], [], []
        for i in range(group):
            e = e_all[:, i * hw:(i + 1) * hw]
            q = q_ref[0, rows[i], :]
            k = 1.0 - jnp.exp2(lfs[i])
            b = e[0:L]
            b_tot = e[L + N_SUB:L + N_SUB + 1]
            sub_rows = lambda r0: jnp.concatenate(
                [jnp.broadcast_to(e[r0 + j:r0 + j + 1], (SUB, hw)) for j in range(N_SUB)], axis=0)
            mid_rows = sub_rows(L)
            end_rows = sub_rows(L + 8)
            to_mid = b - mid_rows
            spans.append(jnp.max(jnp.abs(to_mid)))
            q_mids.append((q * jnp.exp2(jnp.minimum(to_mid, EXP2_CLAMP))).astype(BF16))
            k_diag = k * jnp.exp2(jnp.minimum(-to_mid, EXP2_CLAMP))
            k_end = k * jnp.exp2(end_rows - b)
            cross = jnp.exp2(e[L + 16:L + 32]) * pair_valid
            slabs = []
            for j in range(N_SUB):
                for ib in range(N_SUB):
                    rs = slice(ib * SUB, (ib + 1) * SUB)
                    row = N_SUB * j + ib
                    slabs.append(k_diag[rs] if ib == j else k_end[rs] * cross[row:row + 1])
            k_mids.append(jnp.concatenate(slabs, axis=0).astype(BF16))
            k_outs.append((k * jnp.exp2(b_tot - b)).astype(BF16))
            vbs.append(v_ref[0, rows[i], :])
            qin_ref[cs[i]] = (q * jnp.exp2(b)).astype(BF16)
            g_ref[cs[i]] = jnp.broadcast_to(jnp.exp2(b_tot), (SUBLANE, hw))
            yield
        scores = [_dot_nt(q_mids[i][:, lanes[h]], k_mids[i][:, lanes[h]]) for i, h in items]
        yield
        a_s = [jnp.where(wmask, s, 0.0) for s in scores]
        a_s = [jnp.where(upper_rows, a[:, :2 * L], a[:, 2 * L:]).astype(BF16) for a in a_s]
        v2s = [jnp.concatenate([vb, vb], axis=0) for vb in vbs]
        outs = [_dot(a_s[n], v2s[i][:, lanes[h]]) for n, (i, h) in enumerate(items)]
        yield
        for i in range(group):
            oi_ref[cs[i]] = jnp.concatenate(outs[i * HG_HEADS:(i + 1) * HG_HEADS], axis=1)
        yield
        for i, h in items:
            u_ref[cs[i], h] = _dot_tn(vbs[i][:, lanes[h]], k_outs[i][:, lanes[h]])

        @pl.when(functools.reduce(jnp.maximum, spans) > EXP2_CLAMP)
        def _():
            for i in range(group):
                exact_intra(cs[i], rows[i])
        yield

    def exact_intra(c, rows):
        lf = lf_ref[0, 0, rows, :]
        bx_ref[...] = sum(_dot(dmat[0:L], x) for x in _split3(lf))
        kx_ref[...] = 1.0 - jnp.exp2(lf)
        vx_ref[...] = v_ref[0, rows, :].astype(F32)
        q = q_ref[0, rows, :]
        tok = lax.broadcasted_iota(jnp.int32, (L, 1), 0)
        pos = jnp.where(direction == 0, tok, L - 1 - tok)

        def add_source(s, acc):
            b_s = bx_ref[pl.ds(s, 1), :]
            z = q * kx_ref[pl.ds(s, 1), :] * jnp.exp2(jnp.minimum(bx_ref[...] - b_s, 0.0))
            v_s = vx_ref[pl.ds(s, 1), :]
            seen = pos >= jnp.where(direction == 0, s, L - 1 - s)
            cols = [jnp.where(seen, jnp.sum(z[:, ln], axis=1, keepdims=True), 0.0) * v_s[:, ln]
                    for ln in (slice(h * HG_DIM, (h + 1) * HG_DIM) for h in range(HG_HEADS))]
            return acc + jnp.concatenate(cols, axis=1)

        oi_ref[c] = lax.fori_loop(0, L, add_source, jnp.zeros((L, hw), F32))

    def carry_state(c):
        rows = pl.ds(_chunk_rows(c, nc, direction), L)
        q_in = qin_ref[c]
        g = g_ref[c][0:1]
        outs, states = [], []
        for h in range(HG_HEADS):
            ln = slice(h * HG_DIM, (h + 1) * HG_DIM)
            st = st_ref[0, 0, h]
            outs.append(_dot_nt(q_in[:, ln], st.astype(BF16)))
            states.append(g[:, ln] * st + u_ref[c, h])
        o_ref[0, 0, rows, :] = (oi_ref[c] + jnp.concatenate(outs, axis=1)).astype(BF16)
        for h in range(HG_HEADS):
            st_ref[0, 0, h] = states[h]

    return _pipelined(nc // group, group, intra, carry_state)


N_HG = (7, 2, 7)
N_ML = (7, 3, 6)


def _mlstm_program(qk_ref, v_ref, g_ref, le2_ref, let2_ref, c0_ref, m0_ref,
                   o_ref, c_ref, mm_ref,
                   p_ref, u_ref, qs_ref, ct_ref, mloc_ref, sc_ref, *, nc):
    direction = pl.program_id(1)

    @pl.when(pl.program_id(2) == 0)
    def _():
        c_ref[...] = c0_ref[...]
        mm_ref[...] = m0_ref[...]

    le2 = le2_ref[0].astype(BF16)
    let2 = let2_ref[0].astype(BF16)
    causal = le2_ref[0] > 0.0
    L = CHUNK
    neg = -jnp.inf
    lane = lax.broadcasted_iota(jnp.int32, (1, LANE), 1)
    seg = [lane < L, lane >= L]
    lane_t = lax.broadcasted_iota(jnp.int32, (L, LANE), 1) < L
    eye = (lax.broadcasted_iota(jnp.int32, (LANE, LANE), 0)
           == lax.broadcasted_iota(jnp.int32, (LANE, LANE), 1))
    npair = ML_HEADS // 2
    ones_v = jnp.ones((L, ML_V), BF16)

    group = min(ML_GROUP, nc)

    def intra(grp):
        cs = [grp * group + i for i in range(group)]
        ccs = [jnp.where(direction == 0, c, nc - 1 - c) for c in cs]
        rows = [pl.ds(pl.multiple_of(cc * L, L), L) for cc in ccs]
        pairs = [(i, p) for i in range(group) for p in range(npair)]
        gs = [jnp.where(direction == 0, g_ref[0, cc][0:4], g_ref[0, cc][4:8]) for cc in ccs]
        pad = [jnp.zeros((-2 * group % SUBLANE, LANE), F32)] if 2 * group % SUBLANE else []
        i2 = jnp.concatenate([g[0:2] for g in gs] + pad, axis=0)
        lf2 = _log_sigmoid(jnp.concatenate([g[2:4] for g in gs] + pad, axis=0))
        parts = _split3(lf2)
        r2 = i2 - sum(_dot(x, let2) for x in parts)
        g_a = jnp.max(jnp.where(seg[0], r2, neg), axis=1, keepdims=True)
        g_b = jnp.max(jnp.where(seg[1], r2, neg), axis=1, keepdims=True)
        e_w = jnp.exp(r2 - jnp.where(seg[0], g_a, g_b))
        bt_a = jnp.sum(jnp.where(seg[0], lf2, 0.0), axis=1, keepdims=True)
        bt_b = jnp.sum(jnp.where(seg[1], lf2, 0.0), axis=1, keepdims=True)
        yield

        def head_rows(x, i):
            zero = jnp.zeros_like(x[0:1])
            return jnp.concatenate(
                [jnp.broadcast_to(jnp.where(seg[h % 2], x[2 * i + h // 2:2 * i + h // 2 + 1], zero),
                                  (LANE, LANE)) for h in range(ML_HEADS)], axis=0)

        c_ts = [sum(_dot_nt(le2, head_rows(x, i)) for x in parts[:2]) for i in range(group)]
        for i in range(group):
            ct_ref[cs[i]] = c_ts[i]
            sc_ref[cs[i]] = jnp.concatenate(
                [jnp.broadcast_to(x[2 * i + p:2 * i + p + 1], (1, LANE))
                 for x2 in ((bt_a, bt_b), (g_a, g_b)) for p in range(npair) for x in x2], axis=0)
        yield
        qks = [qk_ref[0, r, :] for r in rows]
        vbs = [v_ref[0, r, :] for r in rows]
        q_ps = [qks[i][:, p * LANE:(p + 1) * LANE] for i, p in pairs]
        k_ps = [qks[i][:, (npair + p) * LANE:(npair + p + 1) * LANE] for i, p in pairs]
        zb = jnp.zeros((L, LANE), BF16)
        kbs = [jnp.concatenate([jnp.where(lane_t, k, zb), jnp.where(lane_t, zb, k)], axis=0) for k in k_ps]
        scores = [_dot_nt(q, kb) for q, kb in zip(q_ps, kbs)]
        yield
        ws = []
        for n, (i, p) in enumerate(pairs):
            ha, hb = 2 * p, 2 * p + 1
            t1 = jnp.where(lane_t, c_ts[i][:, ha * ML_V:(ha + 1) * ML_V], c_ts[i][:, hb * ML_V:(hb + 1) * ML_V])
            d_log = jnp.where(causal, t1 + r2[2 * i + p:2 * i + p + 1], neg)
            m_a = jnp.max(jnp.where(lane_t, d_log, neg), axis=1, keepdims=True)
            m_b = jnp.max(jnp.where(lane_t, neg, d_log), axis=1, keepdims=True)
            mloc_ref[cs[i], :, ha * ML_V:(ha + 1) * ML_V] = jnp.broadcast_to(m_a, (L, ML_V))
            mloc_ref[cs[i], :, hb * ML_V:(hb + 1) * ML_V] = jnp.broadcast_to(m_b, (L, ML_V))
            ws.append(jnp.exp(d_log - jnp.where(lane_t, m_a, m_b)))
        yield
        a_s = [(s_ * w).astype(BF16) for s_, w in zip(scores, ws)]
        a_stacks = [jnp.concatenate([jnp.where(lane_t, a, zb), jnp.where(lane_t, zb, a)], axis=0) for a in a_s]
        v_stacks = [jnp.concatenate(
            [jnp.concatenate([vbs[i][:, 2 * p * ML_V:(2 * p + 1) * ML_V], ones_v], axis=1),
             jnp.concatenate([vbs[i][:, (2 * p + 1) * ML_V:(2 * p + 2) * ML_V], ones_v], axis=1)], axis=0)
            for i, p in pairs]
        for n, (i, p) in enumerate(pairs):
            p_ref[cs[i], p] = _dot(a_stacks[n], v_stacks[n])
            qs_ref[cs[i], p] = jnp.concatenate(
                [jnp.where(lane_t, q_ps[n], zb), jnp.where(lane_t, zb, q_ps[n])], axis=0)
        yield
        diags = [jnp.where(eye, jnp.broadcast_to(e_w[2 * i + p:2 * i + p + 1], (LANE, LANE)), 0.0).astype(BF16)
                 for i, p in pairs]
        kb_ws = [_dot(d, kb).astype(BF16) for d, kb in zip(diags, kbs)]
        for n, (i, p) in enumerate(pairs):
            u_ref[cs[i], p] = _dot_tn(kb_ws[n], v_stacks[n])
        yield

    def carry_state(c):
        cc = jnp.where(direction == 0, c, nc - 1 - c)
        rows = pl.ds(pl.multiple_of(cc * L, L), L)
        m_all = mm_ref[0, 0]
        sc = sc_ref[c]
        c_t = ct_ref[c]
        m_loc = mloc_ref[c]
        outs, states, m_rows = [], [], []
        for p in range(ML_HEADS // 2):
            c_pair = c_ref[0, 0, p]
            inter = _dot(qs_ref[c, p], c_pair.astype(BF16))
            intra_p = p_ref[c, p]
            inc = u_ref[c, p]
            new_rows = []
            for hh in range(2):
                h = 2 * p + hh
                rs = slice(hh * L, (hh + 1) * L)
                ln = slice(h * ML_V, (h + 1) * ML_V)
                m_in = m_all[h:h + 1]
                m_t = jnp.maximum(m_loc[:, ln], c_t[:, ln] + m_in)
                alpha = jnp.exp(m_loc[:, ln] - m_t)
                beta = jnp.exp(c_t[:, ln] + m_in - m_t)
                num = alpha * intra_p[rs, :ML_V] + beta * inter[rs, :ML_V]
                den = alpha * intra_p[rs, ML_V:] + beta * inter[rs, ML_V:]
                outs.append(num / jnp.maximum(jnp.abs(den), jnp.exp(-m_t)))
                b_tot = sc[h:h + 1]
                m_new = b_tot + jnp.maximum(m_in, sc[ML_HEADS + h:ML_HEADS + h + 1])
                carry_w = jnp.exp(b_tot + m_in - m_new)
                inc_w = jnp.exp(b_tot + sc[ML_HEADS + h:ML_HEADS + h + 1] - m_new)
                cw2 = jnp.concatenate([carry_w, carry_w], axis=1)
                iw2 = jnp.concatenate([inc_w, inc_w], axis=1)
                new_rows.append(cw2 * c_pair[rs] + iw2 * inc[rs])
                m_rows.append(m_new)
            states.append(jnp.concatenate(new_rows, axis=0))
        o_ref[0, 0, rows, :] = jnp.concatenate(outs, axis=1).astype(BF16)
        for p in range(ML_HEADS // 2):
            c_ref[0, 0, p] = states[p]
        mm_ref[0, 0] = jnp.concatenate(m_rows + [m_all[ML_HEADS:]], axis=0)

    return _pipelined(nc // group, group, intra, carry_state)


def _scan_kernel(*refs, nc):
    hg_in, ml_in, hg_out, ml_out, hg_scr, ml_scr = (
        refs[a:b] for a, b in zip(np.cumsum([0, N_HG[0], N_ML[0], N_HG[1], N_ML[1], N_HG[2]]),
                                  np.cumsum([N_HG[0], N_ML[0], N_HG[1], N_ML[1], N_HG[2], N_ML[2]])))
    _interleave([_hgrn2_program(*hg_in, *hg_out, *hg_scr, nc=nc),
                 _mlstm_program(*ml_in, *ml_out, *ml_scr, nc=nc)])


def _scans(hq, hv, hlf, qk, mv, gates, s0, c0, m0, dmat, pair_valid, wmask, le2, let2):
    bsz, t, hw = hq.shape
    nc = min(SCAN_CHUNKS, t // CHUNK)
    tb = nc * CHUNK
    nblk = t // tb
    npair = ML_HEADS // 2

    def blk(j, d):
        return jnp.where(d == 0, j, nblk - 1 - j)

    tok = lambda b, d, j: (b, blk(j, d), 0)
    tok_d = lambda b, d, j: (b, d, blk(j, d), 0)
    per_dir = lambda a: pl.BlockSpec((1,) + a.shape[1:], lambda b, d, j: (d,) + (0,) * (a.ndim - 1))
    state = lambda a: pl.BlockSpec((1, 1) + a.shape[2:], lambda b, d, j: (b, d) + (0,) * (a.ndim - 2))
    out_tok = pl.BlockSpec((1, 1, tb, hw), tok_d)
    hg_in = [pl.BlockSpec((1, tb, hw), tok), pl.BlockSpec((1, tb, hw), tok), pl.BlockSpec((1, 1, tb, hw), tok_d),
             per_dir(dmat), per_dir(pair_valid), per_dir(wmask), state(s0)]
    ml_in = [pl.BlockSpec((1, tb, qk.shape[-1]), tok), pl.BlockSpec((1, tb, hw), tok),
             pl.BlockSpec((1, nc) + gates.shape[2:], lambda b, d, j: (b, blk(j, d), 0, 0)),
             per_dir(le2), per_dir(let2), state(c0), state(m0)]
    sds = jax.ShapeDtypeStruct
    hg_scr = [pltpu.VMEM((nc,) + s0.shape[2:], F32),
              pltpu.VMEM((nc, CHUNK, hw), BF16),
              pltpu.VMEM((nc, SUBLANE, hw), F32),
              pltpu.VMEM((nc, CHUNK, hw), F32),
              pltpu.VMEM((CHUNK, hw), F32),
              pltpu.VMEM((CHUNK, hw), F32),
              pltpu.VMEM((CHUNK, hw), F32)]
    ml_scr = [pltpu.VMEM((nc, npair, 2 * CHUNK, 2 * ML_V), F32),
              pltpu.VMEM((nc, npair, 2 * ML_QK, 2 * ML_V), F32),
              pltpu.VMEM((nc, npair, 2 * CHUNK, LANE), BF16),
              pltpu.VMEM((nc, CHUNK, hw), F32),
              pltpu.VMEM((nc, CHUNK, hw), F32),
              pltpu.VMEM((nc, SUBLANE, LANE), F32)]
    assert (len(hg_in), 2, len(hg_scr)) == N_HG and (len(ml_in), 3, len(ml_scr)) == N_ML
    return pl.pallas_call(
        functools.partial(_scan_kernel, nc=nc),
        grid=(bsz, 2, nblk),
        in_specs=hg_in + ml_in,
        out_specs=[out_tok, state(s0), out_tok, state(c0), state(m0)],
        out_shape=[sds((bsz, 2, t, hw), BF16), sds(s0.shape, F32),
                   sds((bsz, 2, t, hw), BF16), sds(c0.shape, F32), sds(m0.shape, F32)],
        scratch_shapes=hg_scr + ml_scr,
        compiler_params=_cparams(("parallel", "parallel", "arbitrary")),
        name="scans",
    )(hq, hv, hlf, dmat, pair_valid, wmask, s0, qk, mv, gates, le2, let2, c0, m0)


def _head_rms(x, heads):
    dh = x.shape[-1] // heads
    return jnp.concatenate([_rms(x[:, h * dh:(h + 1) * dh]) for h in range(heads)], axis=1)


def _mix_ffn_kernel(oh_ref, om_ref, hg_ref, mo_ref, x_ref, mod_ref, hnw_ref, mnw_ref, wo_ref,
                    nw_ref, w1_ref, w2_ref, fw_ref, o_ref, a_ref, x_scr, h_scr, *, d_ff, tf):
    mix_gate = mod_ref[0, 0:1, :]
    mods = [mod_ref[0, k:k + 1, :] for k in range(1, 4)]

    def stages(rs):
        def mixed(pr):
            oh = oh_ref[0, 0, pr, :].astype(F32) + oh_ref[0, 1, pr, :].astype(F32)
            om = om_ref[0, 0, pr, :].astype(F32) + om_ref[0, 1, pr, :].astype(F32)
            hg_out = _head_rms(oh, HG_HEADS) * hnw_ref[...] * _silu(hg_ref[0, pr, :].astype(F32))
            ml_out = _sigmoid(mo_ref[0, pr, :].astype(F32)) * (_head_rms(om, ML_HEADS) * mnw_ref[...])
            merged = jnp.concatenate([hg_out, ml_out], axis=1).astype(BF16)
            return x_ref[0, pr, :] + mix_gate * _dot(merged, wo_ref[...])

        def put(y):
            o_ref[0, rs, :] = _rms(y) * fw_ref[...]

        return _Stages(_ffn_stages(mixed, put, mods, nw_ref, w1_ref, w2_ref, a_ref, x_scr, h_scr,
                                   rs, d_ff, tf), _ffn_lead(rs, d_ff, tf))

    _staggered([stages(rs) for rs in _sub_tiles(x_ref.shape[1])])


def _mix_ffn(oh, om, hg, mo, x, mods, hnw, mnw, w_out, nw, w1, w2, fw):
    bsz, t, d = x.shape
    hw = hg.shape[-1]
    d_ff = w2.shape[0]
    tm = min(TOKEN_TILE, t)
    tok = lambda b, i: (b, i, 0)
    tok2 = lambda b, i: (b, 0, i, 0)
    const = lambda b, i: (0, 0)
    resident = lambda a: pl.BlockSpec(a.shape, const, pipeline_mode=pl.Buffered(1))
    return pl.pallas_call(
        functools.partial(_mix_ffn_kernel, d_ff=d_ff, tf=FF_TILE),
        grid=(bsz, t // tm),
        in_specs=[pl.BlockSpec((1, 2, tm, hw), tok2), pl.BlockSpec((1, 2, tm, hw), tok2),
                  pl.BlockSpec((1, tm, hw), tok), pl.BlockSpec((1, tm, hw), tok),
                  pl.BlockSpec((1, tm, d), tok),
                  pl.BlockSpec((1, 4, d), lambda b, i: (b, 0, 0)),
                  pl.BlockSpec((1, hw), const), pl.BlockSpec((1, hw), const), resident(w_out),
                  pl.BlockSpec((1, d), const), resident(w1), resident(w2), pl.BlockSpec((1, d), const)],
        out_specs=pl.BlockSpec((1, tm, d), tok),
        out_shape=jax.ShapeDtypeStruct((bsz, t, d), F32),
        scratch_shapes=[pltpu.VMEM((tm, d_ff), BF16),
                        pltpu.VMEM((tm, d), F32),
                        pltpu.VMEM((tm, d), BF16)],
        compiler_params=_cparams(("parallel", "parallel")),
        name="mix_ffn_final",
    )(oh, om, hg, mo, x, mods, hnw, mnw, w_out, nw, w1, w2, fw)


def kernel(x, c, ctx, c_ctx, w_mod, b_mod, norm1_w, ffn1_w1, ffn1_w2, norm2_w, w_in, ml_gate_b,
           ml_conv_w, ml_conv_b, hg_lb_logits, hg_norm_w, ml_norm_w, w_out, norm3_w, ffn2_w1, ffn2_w2,
           final_norm_w):
    bsz, seq, d = x.shape
    n_ctx = ctx.shape[1]
    assert w_mod.shape[0] == 1, "single-layer kernel"
    assert seq % max(TOKEN_TILE, SCAN_CHUNKS * CHUNK) == 0 and n_ctx % CHUNK == 0 and GRID_W == CHUNK
    hw = HG_HEADS * HG_DIM
    ng = ml_gate_b.shape[-1]

    rows = -(-(bsz + 1) // SUBLANE) * SUBLANE
    cvec = jnp.zeros((rows, d), F32).at[:bsz].set(c).at[bsz].set(c_ctx)
    mods = _modulation(cvec, w_mod[0], b_mod[0][None, :]).reshape(rows, N_MOD, d)
    lat_row = lambda b: b
    ctx_row = lambda b: bsz

    row = lambda a: a.reshape(1, -1)
    w1a, w2a = ffn1_w1[0], ffn1_w2[0]
    x1 = _ffn(x, mods[:, 0:3], lat_row, row(norm1_w[0]), w1a, w2a)
    s1 = _ffn(ctx, mods[:, 0:3], ctx_row, row(norm1_w[0]), w1a, w2a)

    w_in_b = w_in[0].T
    gate_order = np.arange(ng).reshape(2, 2, ML_HEADS // 2, 2).transpose(3, 0, 1, 2).reshape(-1)
    w_gate_t = w_in[0][:, 8 * hw:].T[gate_order].astype(BF16)
    gate_b = ml_gate_b[0][gate_order].reshape(ng, 1)
    dmat, pair_valid, wmask = (jnp.asarray(a) for a in _hgrn2_masks())
    le2, let2 = (jnp.asarray(a) for a in _mlstm_masks())
    conv_w, conv_b = ml_conv_w[0], row(ml_conv_b[0])
    npair = ML_HEADS // 2

    def mixer_scans(s, mod_row, width, s0, c0, m0):
        hq, hv, hg, hlf, mqk, mv, mo, gates = _inproj(
            s, mods[:, 3:5], mod_row, row(norm2_w[0]), w_in_b, w_gate_t, gate_b, hg_lb_logits)
        qk = _conv(mqk, conv_w, conv_b, width)
        oh, s_fin, om, c_fin, m_fin = _scans(hq, hv, hlf, qk, mv, gates, s0, c0, m0,
                                             dmat, pair_valid, wmask, le2, let2)
        return oh, om, hg, mo, s_fin, c_fin, m_fin

    s0 = jnp.zeros((bsz, 2, HG_HEADS, HG_DIM, HG_DIM), F32)
    c0 = jnp.zeros((bsz, 2, npair, 2 * ML_QK, 2 * ML_V), F32)
    m0 = jnp.zeros((bsz, 2, SUBLANE, LANE), F32)
    _, _, _, _, s_ctx, c_ctx_state, m_ctx = mixer_scans(s1, ctx_row, n_ctx, s0, c0, m0)
    oh, om, hg, mo, _, _, _ = mixer_scans(x1, lat_row, GRID_W, s_ctx, c_ctx_state, m_ctx)

    return _mix_ffn(oh, om, hg, mo, x1, mods[:, 5:9], row(hg_norm_w[0]), row(ml_norm_w[0]),
                    w_out[0].astype(BF16), row(norm3_w[0]), ffn2_w1[0], ffn2_w2[0], row(final_norm_w))
```

```python
import functools

import numpy as np
import jax
import jax.numpy as jnp
from jax import lax
from jax.experimental import pallas as pl
from jax.experimental.pallas import tpu as pltpu

F32 = jnp.float32
BF16 = jnp.bfloat16

EPS = 1e-6
CHUNK = 64
GRID_W = 64
N_MOD = 9
HG_HEADS = 4
HG_DIM = 128
ML_HEADS = 4
ML_QK = 64
ML_V = 128
SUB = 16
N_SUB = CHUNK // SUB
GROUP = 4
ML_GROUP = 4
SCAN_CHUNKS = 8
TOKEN_TILE = 512
ROW_SUB = 256
FF_TILE = 256
PIECE = 128
LOG2E = 1.4426950408889634
EXP2_CLAMP = 115.0
LANE = 128
SUBLANE = 8
VMEM_LIMIT = 56 * 1024 * 1024


def _cparams(sem):
    return pltpu.CompilerParams(dimension_semantics=sem, vmem_limit_bytes=VMEM_LIMIT)


def _dot(a, b):
    return jnp.dot(a, b, preferred_element_type=F32)


def _dot_nt(a, b):
    return lax.dot_general(a, b, (((1,), (1,)), ((), ())), preferred_element_type=F32)


def _dot_tn(a, b):
    return lax.dot_general(a, b, (((0,), (0,)), ((), ())), preferred_element_type=F32)


def _sigmoid(x):
    return 1.0 / (1.0 + jnp.exp(-x))


def _silu(x):
    return x * _sigmoid(x)


def _log_sigmoid(x):
    return jnp.minimum(x, 0.0) - jnp.log(1.0 + jnp.exp(-jnp.abs(x)))


def _split3(x):
    hi = x.astype(BF16)
    r = x - hi.astype(F32)
    mid = r.astype(BF16)
    lo = (r - mid.astype(F32)).astype(BF16)
    return hi, mid, lo


def _split2(x):
    hi = x.astype(BF16)
    return hi, (x - hi.astype(F32)).astype(BF16)


def _rms(x):
    return x * lax.rsqrt(jnp.mean(x * x, axis=-1, keepdims=True) + EPS)


def _mod_kernel(c_ref, w_ref, b_ref, o_ref):
    a = _silu(c_ref[...])
    a_hi = a.astype(BF16)
    a_lo = (a - a_hi.astype(F32)).astype(BF16)
    w = w_ref[...]
    w_hi = w.astype(BF16)
    w_lo = (w - w_hi.astype(F32)).astype(BF16)
    o_ref[...] = _dot(a_hi, w_hi) + _dot(a_hi, w_lo) + _dot(a_lo, w_hi) + b_ref[...]


def _modulation(cvec, w_mod, b_mod):
    rows, d = cvec.shape
    n = w_mod.shape[1]
    tn = 1024
    return pl.pallas_call(
        _mod_kernel,
        grid=(n // tn,),
        in_specs=[pl.BlockSpec((rows, d), lambda j: (0, 0)),
                  pl.BlockSpec((d, tn), lambda j: (0, j)),
                  pl.BlockSpec((1, tn), lambda j: (0, j))],
        out_specs=pl.BlockSpec((rows, tn), lambda j: (0, j)),
        out_shape=jax.ShapeDtypeStruct((rows, n), F32),
        compiler_params=_cparams(("arbitrary",)),
        name="modulation",
    )(cvec, w_mod, b_mod)


def _staggered(tiles):
    for _ in range(tiles[0].lead):
        next(tiles[0].gen)
    live = [t.gen for t in tiles]
    while live:
        live = [g for g in live if next(g, None) is not None]


class _Stages:
    def __init__(self, gen, lead):
        self.gen, self.lead = gen, lead


def _ffn_stages(get_x, put_y, mods, nw_ref, w1_ref, w2_ref, a_ref, x_scr, h_scr, rs, d_ff, tf):
    shift, scale, gate = mods
    for p0 in range(rs.start, rs.stop, PIECE):
        pr = slice(p0, min(p0 + PIECE, rs.stop))
        x = get_x(pr)
        x_scr[pr, :] = x
        h_scr[pr, :] = (_rms(x) * nw_ref[...] * (1.0 + scale) + shift).astype(BF16)
        yield True
    h = h_scr[rs, :]
    for c in range(d_ff // tf):
        g = _dot(h, w1_ref[:, c * tf:(c + 1) * tf].astype(BF16))
        u = _dot(h, w1_ref[:, d_ff + c * tf:d_ff + (c + 1) * tf].astype(BF16))
        a_ref[rs, c * tf:(c + 1) * tf] = (_silu(g) * u).astype(BF16)
        yield True
    y = x_scr[rs, :] + 0.5 * gate * _dot(a_ref[rs, :], w2_ref[...].astype(BF16))
    yield True
    put_y(y)
    yield True


def _sub_tiles(rows):
    sub = min(ROW_SUB, rows)
    return [slice(r0, r0 + sub) for r0 in range(0, rows, sub)]


def _ffn_lead(rs, d_ff, tf):
    return -(-(rs.stop - rs.start) // PIECE) + d_ff // tf // 2


def _ffn_kernel(x_ref, mod_ref, nw_ref, w1_ref, w2_ref, o_ref, a_ref, x_scr, h_scr, *, d_ff, tf):
    mods = [mod_ref[0, k:k + 1, :] for k in range(3)]

    def stages(rs):
        def put(y):
            o_ref[0, rs, :] = y
        return _Stages(_ffn_stages(lambda pr: x_ref[0, pr, :], put, mods, nw_ref, w1_ref, w2_ref, a_ref,
                                   x_scr, h_scr, rs, d_ff, tf), _ffn_lead(rs, d_ff, tf))

    _staggered([stages(rs) for rs in _sub_tiles(x_ref.shape[1])])


def _ffn(s, mods, mod_row, nw, w1, w2):
    bsz, t, d = s.shape
    d_ff = w2.shape[0]
    tm = min(TOKEN_TILE, t)
    const = lambda b, i: (0, 0)
    return pl.pallas_call(
        functools.partial(_ffn_kernel, d_ff=d_ff, tf=FF_TILE),
        grid=(bsz, t // tm),
        in_specs=[pl.BlockSpec((1, tm, d), lambda b, i: (b, i, 0)),
                  pl.BlockSpec((1, 3, d), lambda b, i: (mod_row(b), 0, 0)),
                  pl.BlockSpec((1, d), const),
                  pl.BlockSpec((d, 2 * d_ff), const, pipeline_mode=pl.Buffered(1)),
                  pl.BlockSpec((d_ff, d), const, pipeline_mode=pl.Buffered(1))],
        out_specs=pl.BlockSpec((1, tm, d), lambda b, i: (b, i, 0)),
        out_shape=jax.ShapeDtypeStruct((bsz, t, d), F32),
        scratch_shapes=[pltpu.VMEM((tm, d_ff), BF16),
                        pltpu.VMEM((tm, d), F32),
                        pltpu.VMEM((tm, d), BF16)],
        compiler_params=_cparams(("parallel", "parallel")),
        name="ffn",
    )(s, mods, nw, w1, w2)


def _inproj_kernel(x_ref, mod_ref, nw_ref, w_ref, wg_ref, gb_ref, lbl_ref,
                   hq_ref, hv_ref, hg_ref, hlf_ref, mqk_ref, mv_ref, mo_ref, gc_ref, *, hw):
    shift = mod_ref[0, 0:1, :]
    scale = mod_ref[0, 1:2, :]
    lbl = lbl_ref[...]
    e = jnp.exp(lbl - jnp.max(lbl, axis=0, keepdims=True))
    lb = e[0] / jnp.sum(e, axis=0)
    first_half = lax.broadcasted_iota(jnp.int32, (SUBLANE, LANE), 1) < CHUNK
    def stages(rs):
        h = (_rms(x_ref[0, rs, :]) * nw_ref[...] * (1.0 + scale) + shift).astype(BF16)
        yield True

        def proj(k):
            return _dot_nt(h, w_ref[k * hw:(k + 1) * hw, :].astype(BF16))

        hq_ref[0, rs, :] = _silu(proj(0)) * (HG_DIM ** -0.5)
        yield True
        hv_ref[0, rs, :] = proj(1).astype(BF16)
        yield True
        hg_ref[0, rs, :] = proj(2).astype(BF16)
        yield True
        for d in range(2):
            lbd = lb[d:d + 1, :]
            hlf_ref[0, d, rs, :] = jnp.log(lbd + (1.0 - lbd) * _sigmoid(proj(3 + d))) * LOG2E
            yield True
        mqk_ref[0, rs, :] = proj(5).astype(BF16)
        yield True
        mv_ref[0, rs, :] = proj(6).astype(BF16)
        yield True
        mo_ref[0, rs, :] = proj(7).astype(BF16)
        gt = _dot_nt(wg_ref[...], h) + gb_ref[...]
        for m in range((rs.stop - rs.start) // LANE):
            top = gt[0:8, m * LANE:(m + 1) * LANE]
            bot = gt[8:16, m * LANE:(m + 1) * LANE]
            c0 = rs.start // CHUNK + 2 * m
            gc_ref[0, c0] = jnp.where(first_half, top, pltpu.roll(bot, CHUNK, 1))
            gc_ref[0, c0 + 1] = jnp.where(first_half, pltpu.roll(top, CHUNK, 1), bot)
        yield True

    _staggered([_Stages(stages(rs), 8) for rs in _sub_tiles(x_ref.shape[1])])


def _inproj(s, mods, mod_row, nw, w_in, w_gate_t, gate_b, lb_logits):
    bsz, t, d = s.shape
    hw = HG_HEADS * HG_DIM
    ng = gate_b.shape[0]
    tm = min(TOKEN_TILE, t)
    tok4 = lambda b, i: (b, i, 0, 0)
    const = lambda b, i: (0, 0)
    tok = lambda b, i: (b, i, 0)
    tok2 = lambda b, i: (b, 0, i, 0)
    f = lambda dt, *shape: jax.ShapeDtypeStruct(shape, dt)
    return pl.pallas_call(
        functools.partial(_inproj_kernel, hw=hw),
        grid=(bsz, t // tm),
        in_specs=[pl.BlockSpec((1, tm, d), tok),
                  pl.BlockSpec((1, 2, d), lambda b, i: (mod_row(b), 0, 0)),
                  pl.BlockSpec((1, d), const),
                  pl.BlockSpec(w_in.shape, const, pipeline_mode=pl.Buffered(1)),
                  pl.BlockSpec(w_gate_t.shape, const),
                  pl.BlockSpec((ng, 1), const),
                  pl.BlockSpec(lb_logits.shape, lambda b, i: (0, 0, 0))],
        out_specs=[pl.BlockSpec((1, tm, hw), tok), pl.BlockSpec((1, tm, hw), tok),
                   pl.BlockSpec((1, tm, hw), tok),
                   pl.BlockSpec((1, 2, tm, hw), tok2),
                   pl.BlockSpec((1, tm, hw), tok), pl.BlockSpec((1, tm, hw), tok),
                   pl.BlockSpec((1, tm, hw), tok), pl.BlockSpec((1, tm // CHUNK, ng // 2, LANE), tok4)],
        out_shape=[f(F32, bsz, t, hw), f(BF16, bsz, t, hw), f(BF16, bsz, t, hw),
                   f(F32, bsz, 2, t, hw),
                   f(BF16, bsz, t, hw), f(BF16, bsz, t, hw), f(BF16, bsz, t, hw),
                   f(F32, bsz, t // CHUNK, ng // 2, LANE)],
        compiler_params=_cparams(("parallel", "parallel")),
        name="inproj",
    )(s, mods, nw, w_in, w_gate_t, gate_b, lb_logits)


def _conv_kernel(x_ref, w_ref, b_ref, o_ref, *, width, q_tiles):
    x = x_ref[0].astype(F32)
    t = x.shape[0]
    col = lax.broadcasted_iota(jnp.int32, x.shape, 0) % width
    xl = jnp.where(col == 0, 0.0, pltpu.roll(x, 1, 0))
    xr = jnp.where(col == width - 1, 0.0, pltpu.roll(x, t - 1, 0))

    def row(di):
        return (w_ref[di, 0:1, :] * xl + w_ref[di, 1:2, :] * x + w_ref[di, 2:3, :] * xr)

    y = row(1)
    if t > width:
        pad = jnp.zeros((width, x.shape[1]), F32)
        y = y + jnp.concatenate([pad, row(0)[:t - width]], axis=0)
        y = y + jnp.concatenate([row(2)[width:], pad], axis=0)
    y = _silu(y + b_ref[...])
    qscale = jnp.where(pl.program_id(1) < q_tiles, ML_QK ** -0.5, 1.0)
    o_ref[0] = (y * qscale).astype(BF16)


def _conv(x, w, b, width):
    bsz, t, ch = x.shape
    return pl.pallas_call(
        functools.partial(_conv_kernel, width=width, q_tiles=ML_HEADS * ML_QK // LANE),
        grid=(bsz, ch // LANE),
        in_specs=[pl.BlockSpec((1, t, LANE), lambda b_, c: (b_, 0, c)),
                  pl.BlockSpec((3, 3, LANE), lambda b_, c: (0, 0, c)),
                  pl.BlockSpec((1, LANE), lambda b_, c: (0, c))],
        out_specs=pl.BlockSpec((1, t, LANE), lambda b_, c: (b_, 0, c)),
        out_shape=jax.ShapeDtypeStruct((bsz, t, ch), BF16),
        compiler_params=_cparams(("parallel", "parallel")),
        name="conv",
    )(x, w, b)


def _positions():
    t = np.arange(CHUNK)
    return [t, CHUNK - 1 - t]


def _hgrn2_masks():
    ds, vs, ws = [], [], []
    for p in _positions():
        le = (p[None, :] <= p[:, None]).astype(np.float32)
        order = [p[SUB * j] // SUB for j in range(N_SUB)]
        ref = np.zeros((32, CHUNK), np.float32)
        valid = np.zeros((N_SUB * N_SUB, 1), np.float32)
        for j in range(N_SUB):
            ref[j] = p <= SUB * order[j] + SUB // 2 - 1
            ref[8 + j] = p <= SUB * order[j] + SUB - 1
        ref[N_SUB] = 1.0
        for j in range(N_SUB):
            for i in range(N_SUB):
                if order[i] < order[j]:
                    ref[16 + N_SUB * j + i] = ref[j] - ref[8 + i]
                    valid[N_SUB * j + i] = 1.0
        ds.append(np.concatenate([le, ref], axis=0))
        vs.append(np.broadcast_to(valid, (N_SUB * N_SUB, HG_HEADS * HG_DIM)))
        tok_blk = np.arange(CHUNK)[:, None] // SUB
        ws.append(np.concatenate([(tok_blk == j) * le for j in range(N_SUB)], axis=1))
    return (np.stack(ds).astype(np.float32), np.stack(vs).astype(np.float32),
            np.stack(ws).astype(np.float32))


def _mlstm_masks():
    le2s, let2s = [], []
    for p in _positions():
        le = (p[None, :] <= p[:, None]).astype(np.float32)
        le2s.append(np.concatenate([le, le], axis=1))
        let2 = np.zeros((2 * CHUNK, 2 * CHUNK), np.float32)
        let2[:CHUNK, :CHUNK] = le.T
        let2[CHUNK:, CHUNK:] = le.T
        let2s.append(let2)
    return np.stack(le2s), np.stack(let2s)


def _pipelined(n_groups, group, stages_of, step_of):
    queue = []
    for g in range(n_groups):
        for _ in stages_of(g):
            if queue:
                queue.pop(0)()
            yield True
        while queue:
            queue.pop(0)()
            yield True
        queue = [functools.partial(step_of, g * group + i) for i in range(group)]
    for step in queue:
        step()
        yield True


def _interleave(programs):
    live = list(programs)
    while live:
        live = [p for p in live if next(p, None) is not None]


def _blockdiag(a, b):
    z = jnp.zeros_like(a)
    return jnp.concatenate([jnp.concatenate([a, z], axis=1), jnp.concatenate([z, b], axis=1)], axis=0)


def _chunk_rows(c, nc, direction):
    cc = jnp.where(direction == 0, c, nc - 1 - c)
    return pl.multiple_of(cc * CHUNK, CHUNK)


def _hgrn2_program(q_ref, v_ref, lf_ref, d_ref, pv_ref, w_ref, s0_ref, o_ref, st_ref,
                   u_ref, qin_ref, g_ref, oi_ref, bx_ref, kx_ref, vx_ref, *, nc):
    direction = pl.program_id(1)

    @pl.when(pl.program_id(2) == 0)
    def _():
        st_ref[...] = s0_ref[...]

    dmat = d_ref[0].astype(BF16)
    pair_valid = pv_ref[0]
    wmask = w_ref[0] > 0.0
    L = CHUNK
    hw = q_ref.shape[-1]
    upper_rows = lax.broadcasted_iota(jnp.int32, (L, 2 * L), 0) < L // 2

    group = min(GROUP, nc)

    def intra(grp):
        cs = [grp * group + i for i in range(group)]
        rows = [pl.ds(_chunk_rows(c, nc, direction), L) for c in cs]
        items = [(i, h) for i in range(group) for h in range(HG_HEADS)]
        lanes = [slice(h * HG_DIM, (h + 1) * HG_DIM) for h in range(HG_HEADS)]
        lfs = [lf_ref[0, 0, r, :] for r in rows]
        parts = [_split2(lf) for lf in lfs]
        e_all = sum(_dot(dmat, jnp.concatenate([parts[i][n] for i in range(group)], axis=1))
                    for n in range(2))
        yield
        q_mids, k_mids, k_outs, vbs, spans = [], [], [], [], []
        for i in range(group):
            e = e_all[:, i * hw:(i + 1) * hw]
            q = q_ref[0, rows[i], :]
            k = 1.0 - jnp.exp2(lfs[i])
            b = e[0:L]
            b_tot = e[L + N_SUB:L + N_SUB + 1]
            sub_rows = lambda r0: jnp.concatenate(
                [jnp.broadcast_to(e[r0 + j:r0 + j + 1], (SUB, hw)) for j in range(N_SUB)], axis=0)
            mid_rows = sub_rows(L)
            end_rows = sub_rows(L + 8)
            to_mid = b - mid_rows
            spans.append(jnp.max(jnp.abs(to_mid)))
            q_mids.append((q * jnp.exp2(jnp.minimum(to_mid, EXP2_CLAMP))).astype(BF16))
            k_diag = k * jnp.exp2(jnp.minimum(-to_mid, EXP2_CLAMP))
            k_end = k * jnp.exp2(end_rows - b)
            cross = jnp.exp2(e[L + 16:L + 32]) * pair_valid
            slabs = []
            for j in range(N_SUB):
                for ib in range(N_SUB):
                    rs = slice(ib * SUB, (ib + 1) * SUB)
                    row = N_SUB * j + ib
                    slabs.append(k_diag[rs] if ib == j else k_end[rs] * cross[row:row + 1])
            k_mids.append(jnp.concatenate(slabs, axis=0).astype(BF16))
            k_outs.append((k * jnp.exp2(b_tot - b)).astype(BF16))
            vbs.append(v_ref[0, rows[i], :])
            qin_ref[cs[i]] = (q * jnp.exp2(b)).astype(BF16)
            g_ref[cs[i]] = jnp.broadcast_to(jnp.exp2(b_tot), (SUBLANE, hw))
            yield
        scores = [_dot_nt(q_mids[i][:, lanes[h]], k_mids[i][:, lanes[h]]) for i, h in items]
        yield
        a_s = [jnp.where(wmask, s, 0.0) for s in scores]
        a_s = [jnp.where(upper_rows, a[:, :2 * L], a[:, 2 * L:]).astype(BF16) for a in a_s]
        v2s = [jnp.concatenate([vb, vb], axis=0) for vb in vbs]
        outs = [_dot(jnp.concatenate([a_s[i * HG_HEADS + h], a_s[i * HG_HEADS + h + 1]], axis=1),
                     _blockdiag(v2s[i][:, lanes[h]], v2s[i][:, lanes[h + 1]]))
                for i in range(group) for h in range(0, HG_HEADS, 2)]
        yield
        for i in range(group):
            oi_ref[cs[i]] = jnp.concatenate(outs[i * HG_HEADS // 2:(i + 1) * HG_HEADS // 2], axis=1)
        yield
        for i, h in items:
            u_ref[cs[i], h] = _dot_tn(vbs[i][:, lanes[h]], k_outs[i][:, lanes[h]])

        @pl.when(functools.reduce(jnp.maximum, spans) > EXP2_CLAMP)
        def _():
            for i in range(group):
                exact_intra(cs[i], rows[i])
        yield

    def exact_intra(c, rows):
        lf = lf_ref[0, 0, rows, :]
        bx_ref[...] = sum(_dot(dmat[0:L], x) for x in _split3(lf))
        kx_ref[...] = 1.0 - jnp.exp2(lf)
        vx_ref[...] = v_ref[0, rows, :].astype(F32)
        q = q_ref[0, rows, :]
        tok = lax.broadcasted_iota(jnp.int32, (L, 1), 0)
        pos = jnp.where(direction == 0, tok, L - 1 - tok)

        def add_source(s, acc):
            b_s = bx_ref[pl.ds(s, 1), :]
            z = q * kx_ref[pl.ds(s, 1), :] * jnp.exp2(jnp.minimum(bx_ref[...] - b_s, 0.0))
            v_s = vx_ref[pl.ds(s, 1), :]
            seen = pos >= jnp.where(direction == 0, s, L - 1 - s)
            cols = [jnp.where(seen, jnp.sum(z[:, ln], axis=1, keepdims=True), 0.0) * v_s[:, ln]
                    for ln in (slice(h * HG_DIM, (h + 1) * HG_DIM) for h in range(HG_HEADS))]
            return acc + jnp.concatenate(cols, axis=1)

        oi_ref[c] = lax.fori_loop(0, L, add_source, jnp.zeros((L, hw), F32))

    def carry_state(c):
        rows = pl.ds(_chunk_rows(c, nc, direction), L)
        q_in = qin_ref[c]
        g = g_ref[c][0:1]
        sts = [st_ref[0, 0, h] for h in range(HG_HEADS)]
        outs = [_dot_nt(q_in[:, h * HG_DIM:(h + 2) * HG_DIM],
                        _blockdiag(sts[h].astype(BF16), sts[h + 1].astype(BF16)))
                for h in range(0, HG_HEADS, 2)]
        states = [g[:, h * HG_DIM:(h + 1) * HG_DIM] * sts[h] + u_ref[c, h] for h in range(HG_HEADS)]
        o_ref[0, 0, rows, :] = (oi_ref[c] + jnp.concatenate(outs, axis=1)).astype(BF16)
        for h in range(HG_HEADS):
            st_ref[0, 0, h] = states[h]

    return _pipelined(nc // group, group, intra, carry_state)


N_HG = (7, 2, 7)
N_ML = (7, 3, 6)


def _mlstm_program(qk_ref, v_ref, g_ref, le2_ref, let2_ref, c0_ref, m0_ref,
                   o_ref, c_ref, mm_ref,
                   p_ref, u_ref, qs_ref, ct_ref, mloc_ref, sc_ref, *, nc):
    direction = pl.program_id(1)

    @pl.when(pl.program_id(2) == 0)
    def _():
        c_ref[...] = c0_ref[...]
        mm_ref[...] = m0_ref[...]

    le2 = le2_ref[0].astype(BF16)
    let2 = let2_ref[0].astype(BF16)
    causal = le2_ref[0] > 0.0
    L = CHUNK
    neg = -jnp.inf
    lane = lax.broadcasted_iota(jnp.int32, (1, LANE), 1)
    seg = [lane < L, lane >= L]
    lane_t = lax.broadcasted_iota(jnp.int32, (L, LANE), 1) < L
    eye = (lax.broadcasted_iota(jnp.int32, (LANE, LANE), 0)
           == lax.broadcasted_iota(jnp.int32, (LANE, LANE), 1))
    npair = ML_HEADS // 2
    ones_v = jnp.ones((L, ML_V), BF16)

    group = min(ML_GROUP, nc)

    def intra(grp):
        cs = [grp * group + i for i in range(group)]
        ccs = [jnp.where(direction == 0, c, nc - 1 - c) for c in cs]
        rows = [pl.ds(pl.multiple_of(cc * L, L), L) for cc in ccs]
        pairs = [(i, p) for i in range(group) for p in range(npair)]
        gs = [jnp.where(direction == 0, g_ref[0, cc][0:4], g_ref[0, cc][4:8]) for cc in ccs]
        pad = [jnp.zeros((-2 * group % SUBLANE, LANE), F32)] if 2 * group % SUBLANE else []
        i2 = jnp.concatenate([g[0:2] for g in gs] + pad, axis=0)
        lf2 = _log_sigmoid(jnp.concatenate([g[2:4] for g in gs] + pad, axis=0))
        parts = _split3(lf2)
        r2 = i2 - sum(_dot(x, let2) for x in parts)
        g_a = jnp.max(jnp.where(seg[0], r2, neg), axis=1, keepdims=True)
        g_b = jnp.max(jnp.where(seg[1], r2, neg), axis=1, keepdims=True)
        e_w = jnp.exp(r2 - jnp.where(seg[0], g_a, g_b))
        bt_a = jnp.sum(jnp.where(seg[0], lf2, 0.0), axis=1, keepdims=True)
        bt_b = jnp.sum(jnp.where(seg[1], lf2, 0.0), axis=1, keepdims=True)
        yield

        def head_rows(x, i):
            zero = jnp.zeros_like(x[0:1])
            return jnp.concatenate(
                [jnp.broadcast_to(jnp.where(seg[h % 2], x[2 * i + h // 2:2 * i + h // 2 + 1], zero),
                                  (LANE, LANE)) for h in range(ML_HEADS)], axis=0)

        c_ts = [sum(_dot_nt(le2, head_rows(x, i)) for x in parts[:2]) for i in range(group)]
        for i in range(group):
            ct_ref[cs[i]] = c_ts[i]
            sc_ref[cs[i]] = jnp.concatenate(
                [jnp.broadcast_to(x[2 * i + p:2 * i + p + 1], (1, LANE))
                 for x2 in ((bt_a, bt_b), (g_a, g_b)) for p in range(npair) for x in x2], axis=0)
        yield
        qks = [qk_ref[0, r, :] for r in rows]
        vbs = [v_ref[0, r, :] for r in rows]
        q_ps = [qks[i][:, p * LANE:(p + 1) * LANE] for i, p in pairs]
        k_ps = [qks[i][:, (npair + p) * LANE:(npair + p + 1) * LANE] for i, p in pairs]
        zb = jnp.zeros((L, LANE), BF16)
        kbs = [jnp.concatenate([jnp.where(lane_t, k, zb), jnp.where(lane_t, zb, k)], axis=0) for k in k_ps]
        scores = [_dot_nt(q, kb) for q, kb in zip(q_ps, kbs)]
        yield
        ws = []
        for n, (i, p) in enumerate(pairs):
            ha, hb = 2 * p, 2 * p + 1
            t1 = jnp.where(lane_t, c_ts[i][:, ha * ML_V:(ha + 1) * ML_V], c_ts[i][:, hb * ML_V:(hb + 1) * ML_V])
            d_log = jnp.where(causal, t1 + r2[2 * i + p:2 * i + p + 1], neg)
            m_a = jnp.max(jnp.where(lane_t, d_log, neg), axis=1, keepdims=True)
            m_b = jnp.max(jnp.where(lane_t, neg, d_log), axis=1, keepdims=True)
            mloc_ref[cs[i], :, ha * ML_V:(ha + 1) * ML_V] = jnp.broadcast_to(m_a, (L, ML_V))
            mloc_ref[cs[i], :, hb * ML_V:(hb + 1) * ML_V] = jnp.broadcast_to(m_b, (L, ML_V))
            ws.append(jnp.exp(d_log - jnp.where(lane_t, m_a, m_b)))
        yield
        a_s = [(s_ * w).astype(BF16) for s_, w in zip(scores, ws)]
        a_stacks = [jnp.concatenate([jnp.where(lane_t, a, zb), jnp.where(lane_t, zb, a)], axis=0) for a in a_s]
        v_stacks = [jnp.concatenate(
            [jnp.concatenate([vbs[i][:, 2 * p * ML_V:(2 * p + 1) * ML_V], ones_v], axis=1),
             jnp.concatenate([vbs[i][:, (2 * p + 1) * ML_V:(2 * p + 2) * ML_V], ones_v], axis=1)], axis=0)
            for i, p in pairs]
        for n, (i, p) in enumerate(pairs):
            p_ref[cs[i], p] = _dot(a_stacks[n], v_stacks[n])
            qs_ref[cs[i], p] = jnp.concatenate(
                [jnp.where(lane_t, q_ps[n], zb), jnp.where(lane_t, zb, q_ps[n])], axis=0)
        yield
        diags = [jnp.where(eye, jnp.broadcast_to(e_w[2 * i + p:2 * i + p + 1], (LANE, LANE)), 0.0).astype(BF16)
                 for i, p in pairs]
        kb_ws = [_dot(d, kb).astype(BF16) for d, kb in zip(diags, kbs)]
        for n, (i, p) in enumerate(pairs):
            u_ref[cs[i], p] = _dot_tn(kb_ws[n], v_stacks[n])
        yield

    def carry_state(c):
        cc = jnp.where(direction == 0, c, nc - 1 - c)
        rows = pl.ds(pl.multiple_of(cc * L, L), L)
        m_all = mm_ref[0, 0]
        sc = sc_ref[c]
        c_t = ct_ref[c]
        m_loc = mloc_ref[c]
        outs, states, m_rows = [], [], []
        for p in range(ML_HEADS // 2):
            c_pair = c_ref[0, 0, p]
            inter = _dot(qs_ref[c, p], c_pair.astype(BF16))
            intra_p = p_ref[c, p]
            inc = u_ref[c, p]
            new_rows = []
            for hh in range(2):
                h = 2 * p + hh
                rs = slice(hh * L, (hh + 1) * L)
                ln = slice(h * ML_V, (h + 1) * ML_V)
                m_in = m_all[h:h + 1]
                m_t = jnp.maximum(m_loc[:, ln], c_t[:, ln] + m_in)
                alpha = jnp.exp(m_loc[:, ln] - m_t)
                beta = jnp.exp(c_t[:, ln] + m_in - m_t)
                num = alpha * intra_p[rs, :ML_V] + beta * inter[rs, :ML_V]
                den = alpha * intra_p[rs, ML_V:] + beta * inter[rs, ML_V:]
                outs.append(num / jnp.maximum(jnp.abs(den), jnp.exp(-m_t)))
                b_tot = sc[h:h + 1]
                m_new = b_tot + jnp.maximum(m_in, sc[ML_HEADS + h:ML_HEADS + h + 1])
                carry_w = jnp.exp(b_tot + m_in - m_new)
                inc_w = jnp.exp(b_tot + sc[ML_HEADS + h:ML_HEADS + h + 1] - m_new)
                cw2 = jnp.concatenate([carry_w, carry_w], axis=1)
                iw2 = jnp.concatenate([inc_w, inc_w], axis=1)
                new_rows.append(cw2 * c_pair[rs] + iw2 * inc[rs])
                m_rows.append(m_new)
            states.append(jnp.concatenate(new_rows, axis=0))
        o_ref[0, 0, rows, :] = jnp.concatenate(outs, axis=1).astype(BF16)
        for p in range(ML_HEADS // 2):
            c_ref[0, 0, p] = states[p]
        mm_ref[0, 0] = jnp.concatenate(m_rows + [m_all[ML_HEADS:]], axis=0)

    return _pipelined(nc // group, group, intra, carry_state)


def _scan_kernel(*refs, nc):
    hg_in, ml_in, hg_out, ml_out, hg_scr, ml_scr = (
        refs[a:b] for a, b in zip(np.cumsum([0, N_HG[0], N_ML[0], N_HG[1], N_ML[1], N_HG[2]]),
                                  np.cumsum([N_HG[0], N_ML[0], N_HG[1], N_ML[1], N_HG[2], N_ML[2]])))
    _interleave([_hgrn2_program(*hg_in, *hg_out, *hg_scr, nc=nc),
                 _mlstm_program(*ml_in, *ml_out, *ml_scr, nc=nc)])


def _scans(hq, hv, hlf, qk, mv, gates, s0, c0, m0, dmat, pair_valid, wmask, le2, let2):
    bsz, t, hw = hq.shape
    nc = min(SCAN_CHUNKS, t // CHUNK)
    tb = nc * CHUNK
    nblk = t // tb
    npair = ML_HEADS // 2

    def blk(j, d):
        return jnp.where(d == 0, j, nblk - 1 - j)

    tok = lambda b, d, j: (b, blk(j, d), 0)
    tok_d = lambda b, d, j: (b, d, blk(j, d), 0)
    per_dir = lambda a: pl.BlockSpec((1,) + a.shape[1:], lambda b, d, j: (d,) + (0,) * (a.ndim - 1))
    state = lambda a: pl.BlockSpec((1, 1) + a.shape[2:], lambda b, d, j: (b, d) + (0,) * (a.ndim - 2))
    out_tok = pl.BlockSpec((1, 1, tb, hw), tok_d)
    hg_in = [pl.BlockSpec((1, tb, hw), tok), pl.BlockSpec((1, tb, hw), tok), pl.BlockSpec((1, 1, tb, hw), tok_d),
             per_dir(dmat), per_dir(pair_valid), per_dir(wmask), state(s0)]
    ml_in = [pl.BlockSpec((1, tb, qk.shape[-1]), tok), pl.BlockSpec((1, tb, hw), tok),
             pl.BlockSpec((1, nc) + gates.shape[2:], lambda b, d, j: (b, blk(j, d), 0, 0)),
             per_dir(le2), per_dir(let2), state(c0), state(m0)]
    sds = jax.ShapeDtypeStruct
    hg_scr = [pltpu.VMEM((nc,) + s0.shape[2:], F32),
              pltpu.VMEM((nc, CHUNK, hw), BF16),
              pltpu.VMEM((nc, SUBLANE, hw), F32),
              pltpu.VMEM((nc, CHUNK, hw), F32),
              pltpu.VMEM((CHUNK, hw), F32),
              pltpu.VMEM((CHUNK, hw), F32),
              pltpu.VMEM((CHUNK, hw), F32)]
    ml_scr = [pltpu.VMEM((nc, npair, 2 * CHUNK, 2 * ML_V), F32),
              pltpu.VMEM((nc, npair, 2 * ML_QK, 2 * ML_V), F32),
              pltpu.VMEM((nc, npair, 2 * CHUNK, LANE), BF16),
              pltpu.VMEM((nc, CHUNK, hw), F32),
              pltpu.VMEM((nc, CHUNK, hw), F32),
              pltpu.VMEM((nc, SUBLANE, LANE), F32)]
    assert (len(hg_in), 2, len(hg_scr)) == N_HG and (len(ml_in), 3, len(ml_scr)) == N_ML
    return pl.pallas_call(
        functools.partial(_scan_kernel, nc=nc),
        grid=(bsz, 2, nblk),
        in_specs=hg_in + ml_in,
        out_specs=[out_tok, state(s0), out_tok, state(c0), state(m0)],
        out_shape=[sds((bsz, 2, t, hw), BF16), sds(s0.shape, F32),
                   sds((bsz, 2, t, hw), BF16), sds(c0.shape, F32), sds(m0.shape, F32)],
        scratch_shapes=hg_scr + ml_scr,
        compiler_params=_cparams(("parallel", "parallel", "arbitrary")),
        name="scans",
    )(hq, hv, hlf, dmat, pair_valid, wmask, s0, qk, mv, gates, le2, let2, c0, m0)


def _head_rms(x, heads):
    dh = x.shape[-1] // heads
    return jnp.concatenate([_rms(x[:, h * dh:(h + 1) * dh]) for h in range(heads)], axis=1)


def _mix_ffn_kernel(oh_ref, om_ref, hg_ref, mo_ref, x_ref, mod_ref, hnw_ref, mnw_ref, wo_ref,
                    nw_ref, w1_ref, w2_ref, fw_ref, o_ref, a_ref, x_scr, h_scr, *, d_ff, tf):
    mix_gate = mod_ref[0, 0:1, :]
    mods = [mod_ref[0, k:k + 1, :] for k in range(1, 4)]

    def stages(rs):
        def mixed(pr):
            oh = oh_ref[0, 0, pr, :].astype(F32) + oh_ref[0, 1, pr, :].astype(F32)
            om = om_ref[0, 0, pr, :].astype(F32) + om_ref[0, 1, pr, :].astype(F32)
            hg_out = _head_rms(oh, HG_HEADS) * hnw_ref[...] * _silu(hg_ref[0, pr, :].astype(F32))
            ml_out = _sigmoid(mo_ref[0, pr, :].astype(F32)) * (_head_rms(om, ML_HEADS) * mnw_ref[...])
            merged = jnp.concatenate([hg_out, ml_out], axis=1).astype(BF16)
            return x_ref[0, pr, :] + mix_gate * _dot(merged, wo_ref[...])

        def put(y):
            o_ref[0, rs, :] = _rms(y) * fw_ref[...]

        return _Stages(_ffn_stages(mixed, put, mods, nw_ref, w1_ref, w2_ref, a_ref, x_scr, h_scr,
                                   rs, d_ff, tf), _ffn_lead(rs, d_ff, tf))

    _staggered([stages(rs) for rs in _sub_tiles(x_ref.shape[1])])


def _mix_ffn(oh, om, hg, mo, x, mods, hnw, mnw, w_out, nw, w1, w2, fw):
    bsz, t, d = x.shape
    hw = hg.shape[-1]
    d_ff = w2.shape[0]
    tm = min(TOKEN_TILE, t)
    tok = lambda b, i: (b, i, 0)
    tok2 = lambda b, i: (b, 0, i, 0)
    const = lambda b, i: (0, 0)
    resident = lambda a: pl.BlockSpec(a.shape, const, pipeline_mode=pl.Buffered(1))
    return pl.pallas_call(
        functools.partial(_mix_ffn_kernel, d_ff=d_ff, tf=FF_TILE),
        grid=(bsz, t // tm),
        in_specs=[pl.BlockSpec((1, 2, tm, hw), tok2), pl.BlockSpec((1, 2, tm, hw), tok2),
                  pl.BlockSpec((1, tm, hw), tok), pl.BlockSpec((1, tm, hw), tok),
                  pl.BlockSpec((1, tm, d), tok),
                  pl.BlockSpec((1, 4, d), lambda b, i: (b, 0, 0)),
                  pl.BlockSpec((1, hw), const), pl.BlockSpec((1, hw), const), resident(w_out),
                  pl.BlockSpec((1, d), const), resident(w1), resident(w2), pl.BlockSpec((1, d), const)],
        out_specs=pl.BlockSpec((1, tm, d), tok),
        out_shape=jax.ShapeDtypeStruct((bsz, t, d), F32),
        scratch_shapes=[pltpu.VMEM((tm, d_ff), BF16),
                        pltpu.VMEM((tm, d), F32),
                        pltpu.VMEM((tm, d), BF16)],
        compiler_params=_cparams(("parallel", "parallel")),
        name="mix_ffn_final",
    )(oh, om, hg, mo, x, mods, hnw, mnw, w_out, nw, w1, w2, fw)


def kernel(x, c, ctx, c_ctx, w_mod, b_mod, norm1_w, ffn1_w1, ffn1_w2, norm2_w, w_in, ml_gate_b,
           ml_conv_w, ml_conv_b, hg_lb_logits, hg_norm_w, ml_norm_w, w_out, norm3_w, ffn2_w1, ffn2_w2,
           final_norm_w):
    bsz, seq, d = x.shape
    n_ctx = ctx.shape[1]
    assert w_mod.shape[0] == 1, "single-layer kernel"
    assert seq % max(TOKEN_TILE, SCAN_CHUNKS * CHUNK) == 0 and n_ctx % CHUNK == 0 and GRID_W == CHUNK
    hw = HG_HEADS * HG_DIM
    ng = ml_gate_b.shape[-1]

    rows = -(-(bsz + 1) // SUBLANE) * SUBLANE
    cvec = jnp.zeros((rows, d), F32).at[:bsz].set(c).at[bsz].set(c_ctx)
    mods = _modulation(cvec, w_mod[0], b_mod[0][None, :]).reshape(rows, N_MOD, d)
    lat_row = lambda b: b
    ctx_row = lambda b: bsz

    row = lambda a: a.reshape(1, -1)
    w1a, w2a = ffn1_w1[0], ffn1_w2[0]
    x1 = _ffn(x, mods[:, 0:3], lat_row, row(norm1_w[0]), w1a, w2a)
    s1 = _ffn(ctx, mods[:, 0:3], ctx_row, row(norm1_w[0]), w1a, w2a)

    w_in_b = w_in[0].T
    gate_order = np.arange(ng).reshape(2, 2, ML_HEADS // 2, 2).transpose(3, 0, 1, 2).reshape(-1)
    w_gate_t = w_in[0][:, 8 * hw:].T[gate_order].astype(BF16)
    gate_b = ml_gate_b[0][gate_order].reshape(ng, 1)
    dmat, pair_valid, wmask = (jnp.asarray(a) for a in _hgrn2_masks())
    le2, let2 = (jnp.asarray(a) for a in _mlstm_masks())
    conv_w, conv_b = ml_conv_w[0], row(ml_conv_b[0])
    npair = ML_HEADS // 2

    def mixer_scans(s, mod_row, width, s0, c0, m0):
        hq, hv, hg, hlf, mqk, mv, mo, gates = _inproj(
            s, mods[:, 3:5], mod_row, row(norm2_w[0]), w_in_b, w_gate_t, gate_b, hg_lb_logits)
        qk = _conv(mqk, conv_w, conv_b, width)
        oh, s_fin, om, c_fin, m_fin = _scans(hq, hv, hlf, qk, mv, gates, s0, c0, m0,
                                             dmat, pair_valid, wmask, le2, let2)
        return oh, om, hg, mo, s_fin, c_fin, m_fin

    s0 = jnp.zeros((bsz, 2, HG_HEADS, HG_DIM, HG_DIM), F32)
    c0 = jnp.zeros((bsz, 2, npair, 2 * ML_QK, 2 * ML_V), F32)
    m0 = jnp.zeros((bsz, 2, SUBLANE, LANE), F32)
    _, _, _, _, s_ctx, c_ctx_state, m_ctx = mixer_scans(s1, ctx_row, n_ctx, s0, c0, m0)
    oh, om, hg, mo, _, _, _ = mixer_scans(x1, lat_row, GRID_W, s_ctx, c_ctx_state, m_ctx)

    return _mix_ffn(oh, om, hg, mo, x1, mods[:, 5:9], row(hg_norm_w[0]), row(ml_norm_w[0]),
                    w_out[0].astype(BF16), row(norm3_w[0]), ffn2_w1[0], ffn2_w2[0], row(final_norm_w))
```

```python
import functools

import numpy as np
import jax
import jax.numpy as jnp
from jax import lax
from jax.experimental import pallas as pl
from jax.experimental.pallas import tpu as pltpu

F32 = jnp.float32
BF16 = jnp.bfloat16

EPS = 1e-6
CHUNK = 64
GRID_W = 64
N_MOD = 9
HG_HEADS = 4
HG_DIM = 128
ML_HEADS = 4
ML_QK = 64
ML_V = 128
SUB = 16
N_SUB = CHUNK // SUB
GROUP = 4
ML_GROUP = 4
SCAN_CHUNKS = 8
TOKEN_TILE = 512
ROW_SUB = 256
FF_TILE = 256
PIECE = 128
LOG2E = 1.4426950408889634
EXP2_CLAMP = 115.0
LANE = 128
SUBLANE = 8
VMEM_LIMIT = 56 * 1024 * 1024


def _cparams(sem):
    return pltpu.CompilerParams(dimension_semantics=sem, vmem_limit_bytes=VMEM_LIMIT)


def _dot(a, b):
    return jnp.dot(a, b, preferred_element_type=F32)


def _dot_nt(a, b):
    return lax.dot_general(a, b, (((1,), (1,)), ((), ())), preferred_element_type=F32)


def _dot_tn(a, b):
    return lax.dot_general(a, b, (((0,), (0,)), ((), ())), preferred_element_type=F32)


def _sigmoid(x):
    return 1.0 / (1.0 + jnp.exp(-x))


def _silu(x):
    return x * _sigmoid(x)


def _log_sigmoid(x):
    return jnp.minimum(x, 0.0) - jnp.log(1.0 + jnp.exp(-jnp.abs(x)))


def _split3(x):
    hi = x.astype(BF16)
    r = x - hi.astype(F32)
    mid = r.astype(BF16)
    lo = (r - mid.astype(F32)).astype(BF16)
    return hi, mid, lo


def _split2(x):
    hi = x.astype(BF16)
    return hi, (x - hi.astype(F32)).astype(BF16)


def _rms(x):
    return x * lax.rsqrt(jnp.mean(x * x, axis=-1, keepdims=True) + EPS)


def _mod_kernel(c_ref, w_ref, b_ref, o_ref):
    a = _silu(c_ref[...])
    a_hi = a.astype(BF16)
    a_lo = (a - a_hi.astype(F32)).astype(BF16)
    w = w_ref[...]
    w_hi = w.astype(BF16)
    w_lo = (w - w_hi.astype(F32)).astype(BF16)
    o_ref[...] = _dot(a_hi, w_hi) + _dot(a_hi, w_lo) + _dot(a_lo, w_hi) + b_ref[...]


def _modulation(cvec, w_mod, b_mod):
    rows, d = cvec.shape
    n = w_mod.shape[1]
    tn = 1024
    return pl.pallas_call(
        _mod_kernel,
        grid=(n // tn,),
        in_specs=[pl.BlockSpec((rows, d), lambda j: (0, 0)),
                  pl.BlockSpec((d, tn), lambda j: (0, j)),
                  pl.BlockSpec((1, tn), lambda j: (0, j))],
        out_specs=pl.BlockSpec((rows, tn), lambda j: (0, j)),
        out_shape=jax.ShapeDtypeStruct((rows, n), F32),
        compiler_params=_cparams(("arbitrary",)),
        name="modulation",
    )(cvec, w_mod, b_mod)


def _staggered(tiles):
    for _ in range(tiles[0].lead):
        next(tiles[0].gen)
    live = [t.gen for t in tiles]
    while live:
        live = [g for g in live if next(g, None) is not None]


class _Stages:
    def __init__(self, gen, lead):
        self.gen, self.lead = gen, lead


def _ffn_stages(get_x, put_y, mods, nw_ref, w1_ref, w2_ref, a_ref, x_scr, h_scr, rs, d_ff, tf):
    shift, scale, gate = mods
    for p0 in range(rs.start, rs.stop, PIECE):
        pr = slice(p0, min(p0 + PIECE, rs.stop))
        x = get_x(pr)
        x_scr[pr, :] = x
        h_scr[pr, :] = (_rms(x) * nw_ref[...] * (1.0 + scale) + shift).astype(BF16)
        yield True
    h = h_scr[rs, :]
    for c in range(d_ff // tf):
        g = _dot(h, w1_ref[:, c * tf:(c + 1) * tf].astype(BF16))
        u = _dot(h, w1_ref[:, d_ff + c * tf:d_ff + (c + 1) * tf].astype(BF16))
        a_ref[rs, c * tf:(c + 1) * tf] = (_silu(g) * u).astype(BF16)
        yield True
    y = x_scr[rs, :] + 0.5 * gate * _dot(a_ref[rs, :], w2_ref[...].astype(BF16))
    yield True
    put_y(y)
    yield True


def _sub_tiles(rows):
    sub = min(ROW_SUB, rows)
    return [slice(r0, r0 + sub) for r0 in range(0, rows, sub)]


def _ffn_lead(rs, d_ff, tf):
    return -(-(rs.stop - rs.start) // PIECE) + d_ff // tf // 2


def _ffn_kernel(x_ref, mod_ref, nw_ref, w1_ref, w2_ref, o_ref, a_ref, x_scr, h_scr, *, d_ff, tf):
    mods = [mod_ref[0, k:k + 1, :] for k in range(3)]

    def stages(rs):
        def put(y):
            o_ref[0, rs, :] = y
        return _Stages(_ffn_stages(lambda pr: x_ref[0, pr, :], put, mods, nw_ref, w1_ref, w2_ref, a_ref,
                                   x_scr, h_scr, rs, d_ff, tf), _ffn_lead(rs, d_ff, tf))

    _staggered([stages(rs) for rs in _sub_tiles(x_ref.shape[1])])


def _ffn(s, mods, mod_row, nw, w1, w2):
    bsz, t, d = s.shape
    d_ff = w2.shape[0]
    tm = min(TOKEN_TILE, t)
    const = lambda b, i: (0, 0)
    return pl.pallas_call(
        functools.partial(_ffn_kernel, d_ff=d_ff, tf=FF_TILE),
        grid=(bsz, t // tm),
        in_specs=[pl.BlockSpec((1, tm, d), lambda b, i: (b, i, 0)),
                  pl.BlockSpec((1, 3, d), lambda b, i: (mod_row(b), 0, 0)),
                  pl.BlockSpec((1, d), const),
                  pl.BlockSpec((d, 2 * d_ff), const, pipeline_mode=pl.Buffered(1)),
                  pl.BlockSpec((d_ff, d), const, pipeline_mode=pl.Buffered(1))],
        out_specs=pl.BlockSpec((1, tm, d), lambda b, i: (b, i, 0)),
        out_shape=jax.ShapeDtypeStruct((bsz, t, d), F32),
        scratch_shapes=[pltpu.VMEM((tm, d_ff), BF16),
                        pltpu.VMEM((tm, d), F32),
                        pltpu.VMEM((tm, d), BF16)],
        compiler_params=_cparams(("parallel", "parallel")),
        name="ffn",
    )(s, mods, nw, w1, w2)


def _inproj_kernel(x_ref, mod_ref, nw_ref, w_ref, wg_ref, gb_ref, lbl_ref,
                   hq_ref, hv_ref, hg_ref, hlf_ref, mqk_ref, mv_ref, mo_ref, gc_ref, *, hw):
    shift = mod_ref[0, 0:1, :]
    scale = mod_ref[0, 1:2, :]
    lbl = lbl_ref[...]
    e = jnp.exp(lbl - jnp.max(lbl, axis=0, keepdims=True))
    lb = e[0] / jnp.sum(e, axis=0)
    first_half = lax.broadcasted_iota(jnp.int32, (SUBLANE, LANE), 1) < CHUNK
    def stages(rs):
        h = (_rms(x_ref[0, rs, :]) * nw_ref[...] * (1.0 + scale) + shift).astype(BF16)
        yield True

        def proj(k):
            return _dot_nt(h, w_ref[k * hw:(k + 1) * hw, :].astype(BF16))

        hq_ref[0, rs, :] = _silu(proj(0)) * (HG_DIM ** -0.5)
        yield True
        hv_ref[0, rs, :] = proj(1).astype(BF16)
        yield True
        hg_ref[0, rs, :] = proj(2).astype(BF16)
        yield True
        for d in range(2):
            lbd = lb[d:d + 1, :]
            hlf_ref[0, d, rs, :] = jnp.log(lbd + (1.0 - lbd) * _sigmoid(proj(3 + d))) * LOG2E
            yield True
        mqk_ref[0, rs, :] = proj(5).astype(BF16)
        yield True
        mv_ref[0, rs, :] = proj(6).astype(BF16)
        yield True
        mo_ref[0, rs, :] = proj(7).astype(BF16)
        gt = _dot_nt(wg_ref[...], h) + gb_ref[...]
        for m in range((rs.stop - rs.start) // LANE):
            top = gt[0:8, m * LANE:(m + 1) * LANE]
            bot = gt[8:16, m * LANE:(m + 1) * LANE]
            c0 = rs.start // CHUNK + 2 * m
            gc_ref[0, c0] = jnp.where(first_half, top, pltpu.roll(bot, CHUNK, 1))
            gc_ref[0, c0 + 1] = jnp.where(first_half, pltpu.roll(top, CHUNK, 1), bot)
        yield True

    _staggered([_Stages(stages(rs), 8) for rs in _sub_tiles(x_ref.shape[1])])


def _inproj(s, mods, mod_row, nw, w_in, w_gate_t, gate_b, lb_logits):
    bsz, t, d = s.shape
    hw = HG_HEADS * HG_DIM
    ng = gate_b.shape[0]
    tm = min(TOKEN_TILE, t)
    tok4 = lambda b, i: (b, i, 0, 0)
    const = lambda b, i: (0, 0)
    tok = lambda b, i: (b, i, 0)
    tok2 = lambda b, i: (b, 0, i, 0)
    f = lambda dt, *shape: jax.ShapeDtypeStruct(shape, dt)
    return pl.pallas_call(
        functools.partial(_inproj_kernel, hw=hw),
        grid=(bsz, t // tm),
        in_specs=[pl.BlockSpec((1, tm, d), tok),
                  pl.BlockSpec((1, 2, d), lambda b, i: (mod_row(b), 0, 0)),
                  pl.BlockSpec((1, d), const),
                  pl.BlockSpec(w_in.shape, const, pipeline_mode=pl.Buffered(1)),
                  pl.BlockSpec(w_gate_t.shape, const),
                  pl.BlockSpec((ng, 1), const),
                  pl.BlockSpec(lb_logits.shape, lambda b, i: (0, 0, 0))],
        out_specs=[pl.BlockSpec((1, tm, hw), tok), pl.BlockSpec((1, tm, hw), tok),
                   pl.BlockSpec((1, tm, hw), tok),
                   pl.BlockSpec((1, 2, tm, hw), tok2),
                   pl.BlockSpec((1, tm, hw), tok), pl.BlockSpec((1, tm, hw), tok),
                   pl.BlockSpec((1, tm, hw), tok), pl.BlockSpec((1, tm // CHUNK, ng // 2, LANE), tok4)],
        out_shape=[f(F32, bsz, t, hw), f(BF16, bsz, t, hw), f(BF16, bsz, t, hw),
                   f(F32, bsz, 2, t, hw),
                   f(BF16, bsz, t, hw), f(BF16, bsz, t, hw), f(BF16, bsz, t, hw),
                   f(F32, bsz, t // CHUNK, ng // 2, LANE)],
        compiler_params=_cparams(("parallel", "parallel")),
        name="inproj",
    )(s, mods, nw, w_in, w_gate_t, gate_b, lb_logits)


def _conv_kernel(x_ref, w_ref, b_ref, o_ref, *, width, q_tiles):
    x = x_ref[0].astype(F32)
    t = x.shape[0]
    col = lax.broadcasted_iota(jnp.int32, x.shape, 0) % width
    xl = jnp.where(col == 0, 0.0, pltpu.roll(x, 1, 0))
    xr = jnp.where(col == width - 1, 0.0, pltpu.roll(x, t - 1, 0))

    def row(di):
        return (w_ref[di, 0:1, :] * xl + w_ref[di, 1:2, :] * x + w_ref[di, 2:3, :] * xr)

    y = row(1)
    if t > width:
        pad = jnp.zeros((width, x.shape[1]), F32)
        y = y + jnp.concatenate([pad, row(0)[:t - width]], axis=0)
        y = y + jnp.concatenate([row(2)[width:], pad], axis=0)
    y = _silu(y + b_ref[...])
    qscale = jnp.where(pl.program_id(1) < q_tiles, ML_QK ** -0.5, 1.0)
    o_ref[0] = (y * qscale).astype(BF16)


def _conv(x, w, b, width):
    bsz, t, ch = x.shape
    return pl.pallas_call(
        functools.partial(_conv_kernel, width=width, q_tiles=ML_HEADS * ML_QK // LANE),
        grid=(bsz, ch // LANE),
        in_specs=[pl.BlockSpec((1, t, LANE), lambda b_, c: (b_, 0, c)),
                  pl.BlockSpec((3, 3, LANE), lambda b_, c: (0, 0, c)),
                  pl.BlockSpec((1, LANE), lambda b_, c: (0, c))],
        out_specs=pl.BlockSpec((1, t, LANE), lambda b_, c: (b_, 0, c)),
        out_shape=jax.ShapeDtypeStruct((bsz, t, ch), BF16),
        compiler_params=_cparams(("parallel", "parallel")),
        name="conv",
    )(x, w, b)


def _positions():
    t = np.arange(CHUNK)
    return [t, CHUNK - 1 - t]


def _hgrn2_masks():
    ds, vs, ws = [], [], []
    for p in _positions():
        le = (p[None, :] <= p[:, None]).astype(np.float32)
        order = [p[SUB * j] // SUB for j in range(N_SUB)]
        ref = np.zeros((32, CHUNK), np.float32)
        valid = np.zeros((N_SUB * N_SUB, 1), np.float32)
        for j in range(N_SUB):
            ref[j] = p <= SUB * order[j] + SUB // 2 - 1
            ref[8 + j] = p <= SUB * order[j] + SUB - 1
        ref[N_SUB] = 1.0
        for j in range(N_SUB):
            for i in range(N_SUB):
                if order[i] < order[j]:
                    ref[16 + N_SUB * j + i] = ref[j] - ref[8 + i]
                    valid[N_SUB * j + i] = 1.0
        ds.append(np.concatenate([le, ref], axis=0))
        vs.append(np.broadcast_to(valid, (N_SUB * N_SUB, HG_HEADS * HG_DIM)))
        tok_blk = np.arange(CHUNK)[:, None] // SUB
        ws.append(np.concatenate([(tok_blk == j) * le for j in range(N_SUB)], axis=1))
    return (np.stack(ds).astype(np.float32), np.stack(vs).astype(np.float32),
            np.stack(ws).astype(np.float32))


def _mlstm_masks():
    le2s, let2s = [], []
    for p in _positions():
        le = (p[None, :] <= p[:, None]).astype(np.float32)
        le2s.append(np.concatenate([le, le], axis=1))
        let2 = np.zeros((2 * CHUNK, 2 * CHUNK), np.float32)
        let2[:CHUNK, :CHUNK] = le.T
        let2[CHUNK:, CHUNK:] = le.T
        let2s.append(let2)
    return np.stack(le2s), np.stack(let2s)


def _pipelined(n_groups, group, stages_of, step_of):
    queue = []
    for g in range(n_groups):
        for _ in stages_of(g):
            if queue:
                queue.pop(0)()
            yield True
        while queue:
            queue.pop(0)()
            yield True
        queue = [functools.partial(step_of, g * group + i) for i in range(group)]
    for step in queue:
        step()
        yield True


def _interleave(programs):
    live = list(programs)
    while live:
        live = [p for p in live if next(p, None) is not None]


def _chunk_rows(c, nc, direction):
    cc = jnp.where(direction == 0, c, nc - 1 - c)
    return pl.multiple_of(cc * CHUNK, CHUNK)


def _hgrn2_program(q_ref, v_ref, lf_ref, d_ref, pv_ref, w_ref, s0_ref, o_ref, st_ref,
                   u_ref, qin_ref, g_ref, oi_ref, bx_ref, kx_ref, vx_ref, *, nc):
    direction = pl.program_id(1)

    @pl.when(pl.program_id(2) == 0)
    def _():
        st_ref[...] = s0_ref[...]

    dmat = d_ref[0].astype(BF16)
    pair_valid = pv_ref[0]
    wmask = w_ref[0] > 0.0
    L = CHUNK
    hw = q_ref.shape[-1]
    upper_rows = lax.broadcasted_iota(jnp.int32, (L, 2 * L), 0) < L // 2

    group = min(GROUP, nc)

    def intra(grp):
        cs = [grp * group + i for i in range(group)]
        rows = [pl.ds(_chunk_rows(c, nc, direction), L) for c in cs]
        items = [(i, h) for i in range(group) for h in range(HG_HEADS)]
        lanes = [slice(h * HG_DIM, (h + 1) * HG_DIM) for h in range(HG_HEADS)]
        lfs = [lf_ref[0, 0, r, :] for r in rows]
        parts = [_split2(lf) for lf in lfs]
        e_all = sum(_dot(dmat, jnp.concatenate([parts[i][n] for i in range(group)], axis=1))
                    for n in range(2))
        yield
        q_mids, k_mids, k_outs, vbs, spans = [], [], [], [], []
        for i in range(group):
            e = e_all[:, i * hw:(i + 1) * hw]
            q = q_ref[0, rows[i], :]
            k = 1.0 - jnp.exp2(lfs[i])
            b = e[0:L]
            b_tot = e[L + N_SUB:L + N_SUB + 1]
            sub_rows = lambda r0: jnp.concatenate(
                [jnp.broadcast_to(e[r0 + j:r0 + j + 1], (SUB, hw)) for j in range(N_SUB)], axis=0)
            mid_rows = sub_rows(L)
            end_rows = sub_rows(L + 8)
            to_mid = b - mid_rows
            spans.append(jnp.max(jnp.abs(to_mid)))
            q_mids.append((q * jnp.exp2(jnp.minimum(to_mid, EXP2_CLAMP))).astype(BF16))
            k_diag = k * jnp.exp2(jnp.minimum(-to_mid, EXP2_CLAMP))
            k_end = k * jnp.exp2(end_rows - b)
            cross = jnp.exp2(e[L + 16:L + 32]) * pair_valid
            slabs = []
            for j in range(N_SUB):
                for ib in range(N_SUB):
                    rs = slice(ib * SUB, (ib + 1) * SUB)
                    row = N_SUB * j + ib
                    slabs.append(k_diag[rs] if ib == j else k_end[rs] * cross[row:row + 1])
            k_mids.append(jnp.concatenate(slabs, axis=0).astype(BF16))
            k_outs.append((k * jnp.exp2(b_tot - b)).astype(BF16))
            vbs.append(v_ref[0, rows[i], :])
            qin_ref[cs[i]] = (q * jnp.exp2(b)).astype(BF16)
            g_ref[cs[i]] = jnp.broadcast_to(jnp.exp2(b_tot), (SUBLANE, hw))
            yield
        scores = [_dot_nt(q_mids[i][:, lanes[h]], k_mids[i][:, lanes[h]]) for i, h in items]
        yield
        a_s = [jnp.where(wmask, s, 0.0) for s in scores]
        a_s = [jnp.where(upper_rows, a[:, :2 * L], a[:, 2 * L:]).astype(BF16) for a in a_s]
        v2s = [jnp.concatenate([vb, vb], axis=0) for vb in vbs]
        outs = [_dot(a_s[n], v2s[i][:, lanes[h]]) for n, (i, h) in enumerate(items)]
        yield
        for i in range(group):
            oi_ref[cs[i]] = jnp.concatenate(outs[i * HG_HEADS:(i + 1) * HG_HEADS], axis=1)
        yield
        for i, h in items:
            u_ref[cs[i], h] = _dot_tn(vbs[i][:, lanes[h]], k_outs[i][:, lanes[h]])

        @pl.when(functools.reduce(jnp.maximum, spans) > EXP2_CLAMP)
        def _():
            for i in range(group):
                exact_intra(cs[i], rows[i])
        yield

    def exact_intra(c, rows):
        lf = lf_ref[0, 0, rows, :]
        bx_ref[...] = sum(_dot(dmat[0:L], x) for x in _split3(lf))
        kx_ref[...] = 1.0 - jnp.exp2(lf)
        vx_ref[...] = v_ref[0, rows, :].astype(F32)
        q = q_ref[0, rows, :]
        tok = lax.broadcasted_iota(jnp.int32, (L, 1), 0)
        pos = jnp.where(direction == 0, tok, L - 1 - tok)

        def add_source(s, acc):
            b_s = bx_ref[pl.ds(s, 1), :]
            z = q * kx_ref[pl.ds(s, 1), :] * jnp.exp2(jnp.minimum(bx_ref[...] - b_s, 0.0))
            v_s = vx_ref[pl.ds(s, 1), :]
            seen = pos >= jnp.where(direction == 0, s, L - 1 - s)
            cols = [jnp.where(seen, jnp.sum(z[:, ln], axis=1, keepdims=True), 0.0) * v_s[:, ln]
                    for ln in (slice(h * HG_DIM, (h + 1) * HG_DIM) for h in range(HG_HEADS))]
            return acc + jnp.concatenate(cols, axis=1)

        oi_ref[c] = lax.fori_loop(0, L, add_source, jnp.zeros((L, hw), F32))

    def carry_state(c):
        rows = pl.ds(_chunk_rows(c, nc, direction), L)
        q_in = qin_ref[c]
        g = g_ref[c][0:1]
        outs, states = [], []
        for h in range(HG_HEADS):
            ln = slice(h * HG_DIM, (h + 1) * HG_DIM)
            st = st_ref[0, 0, h]
            outs.append(_dot_nt(q_in[:, ln], st.astype(BF16)))
            states.append(g[:, ln] * st + u_ref[c, h])
        o_ref[0, 0, rows, :] = (oi_ref[c] + jnp.concatenate(outs, axis=1)).astype(BF16)
        for h in range(HG_HEADS):
            st_ref[0, 0, h] = states[h]

    return _pipelined(nc // group, group, intra, carry_state)


N_HG = (7, 2, 7)
N_ML = (7, 3, 5)


def _mlstm_program(qk_ref, v_ref, g_ref, le2_ref, let2_ref, c0_ref, m0_ref,
                   o_ref, c_ref, mm_ref,
                   al_ref, u_ref, ct_ref, mloc_ref, sc_ref, *, nc):
    direction = pl.program_id(1)

    @pl.when(pl.program_id(2) == 0)
    def _():
        c_ref[...] = c0_ref[...]
        mm_ref[...] = m0_ref[...]

    le2 = le2_ref[0].astype(BF16)
    let2 = let2_ref[0].astype(BF16)
    causal = le2_ref[0] > 0.0
    L = CHUNK
    neg = -jnp.inf
    lane = lax.broadcasted_iota(jnp.int32, (1, LANE), 1)
    seg = [lane < L, lane >= L]
    lane_t = lax.broadcasted_iota(jnp.int32, (L, LANE), 1) < L
    eye = (lax.broadcasted_iota(jnp.int32, (LANE, LANE), 0)
           == lax.broadcasted_iota(jnp.int32, (LANE, LANE), 1))
    npair = ML_HEADS // 2
    ones_v = jnp.ones((L, ML_V), BF16)
    zb = jnp.zeros((L, LANE), BF16)

    def by_head(x):
        return jnp.concatenate([jnp.where(lane_t, x, zb), jnp.where(lane_t, zb, x)], axis=0)

    def v_stack(vb, p):
        return jnp.concatenate(
            [jnp.concatenate([vb[:, 2 * p * ML_V:(2 * p + 1) * ML_V], ones_v], axis=1),
             jnp.concatenate([vb[:, (2 * p + 1) * ML_V:(2 * p + 2) * ML_V], ones_v], axis=1)], axis=0)

    group = min(ML_GROUP, nc)

    def intra(grp):
        cs = [grp * group + i for i in range(group)]
        ccs = [jnp.where(direction == 0, c, nc - 1 - c) for c in cs]
        rows = [pl.ds(pl.multiple_of(cc * L, L), L) for cc in ccs]
        pairs = [(i, p) for i in range(group) for p in range(npair)]
        gs = [jnp.where(direction == 0, g_ref[0, cc][0:4], g_ref[0, cc][4:8]) for cc in ccs]
        pad = [jnp.zeros((-2 * group % SUBLANE, LANE), F32)] if 2 * group % SUBLANE else []
        i2 = jnp.concatenate([g[0:2] for g in gs] + pad, axis=0)
        lf2 = _log_sigmoid(jnp.concatenate([g[2:4] for g in gs] + pad, axis=0))
        parts = _split3(lf2)
        r2 = i2 - sum(_dot(x, let2) for x in parts)
        g_a = jnp.max(jnp.where(seg[0], r2, neg), axis=1, keepdims=True)
        g_b = jnp.max(jnp.where(seg[1], r2, neg), axis=1, keepdims=True)
        e_w = jnp.exp(r2 - jnp.where(seg[0], g_a, g_b))
        bt_a = jnp.sum(jnp.where(seg[0], lf2, 0.0), axis=1, keepdims=True)
        bt_b = jnp.sum(jnp.where(seg[1], lf2, 0.0), axis=1, keepdims=True)
        yield

        def head_rows(x, i):
            zero = jnp.zeros_like(x[0:1])
            return jnp.concatenate(
                [jnp.broadcast_to(jnp.where(seg[h % 2], x[2 * i + h // 2:2 * i + h // 2 + 1], zero),
                                  (L, LANE)) for h in range(ML_HEADS)], axis=0)

        c_ts = [sum(_dot_nt(le2, head_rows(x, i)) for x in parts[:2]) for i in range(group)]
        for i in range(group):
            sc_ref[cs[i]] = jnp.concatenate(
                [jnp.broadcast_to(x[2 * i + p:2 * i + p + 1], (1, LANE))
                 for x2 in ((bt_a, bt_b), (g_a, g_b)) for p in range(npair) for x in x2], axis=0)
        yield
        qks = [qk_ref[0, r, :] for r in rows]
        vbs = [v_ref[0, r, :] for r in rows]
        q_ps = [qks[i][:, p * LANE:(p + 1) * LANE] for i, p in pairs]
        k_ps = [qks[i][:, (npair + p) * LANE:(npair + p + 1) * LANE] for i, p in pairs]
        kbs = [by_head(k) for k in k_ps]
        scores = [_dot_nt(q, kb) for q, kb in zip(q_ps, kbs)]
        yield
        for n, (i, p) in enumerate(pairs):
            t1 = c_ts[i][:, p * LANE:(p + 1) * LANE]
            d_log = jnp.where(causal, t1 + r2[2 * i + p:2 * i + p + 1], neg)
            m_a = jnp.max(jnp.where(lane_t, d_log, neg), axis=1, keepdims=True)
            m_b = jnp.max(jnp.where(lane_t, neg, d_log), axis=1, keepdims=True)
            m_loc = jnp.where(lane_t, m_a, m_b)
            ct_ref[cs[i], p] = t1
            mloc_ref[cs[i], p] = m_loc
            al_ref[cs[i], p] = scores[n] * jnp.exp(d_log - m_loc)
        yield
        v_stacks = [v_stack(vbs[i], p) for i, p in pairs]
        yield
        diags = [jnp.where(eye, jnp.broadcast_to(e_w[2 * i + p:2 * i + p + 1], (LANE, LANE)), 0.0).astype(BF16)
                 for i, p in pairs]
        kb_ws = [_dot(d, kb).astype(BF16) for d, kb in zip(diags, kbs)]
        for n, (i, p) in enumerate(pairs):
            u_ref[cs[i], p] = _dot_tn(kb_ws[n], v_stacks[n])
        yield

    def carry_state(c):
        cc = jnp.where(direction == 0, c, nc - 1 - c)
        rows = pl.ds(pl.multiple_of(cc * L, L), L)
        m_all = mm_ref[0, 0]
        sc = sc_ref[c]
        qk = qk_ref[0, rows, :]
        vb = v_ref[0, rows, :]
        outs, states, m_rows = [], [], []
        for p in range(ML_HEADS // 2):
            c_pair = c_ref[0, 0, p]
            inc = u_ref[c, p]
            b_t = ct_ref[c, p]
            m_loc = mloc_ref[c, p]
            m_in2 = jnp.where(seg[0], m_all[2 * p:2 * p + 1], m_all[2 * p + 1:2 * p + 2])
            m_t = jnp.maximum(m_loc, b_t + m_in2)
            lhs_a = (jnp.exp(m_loc - m_t) * al_ref[c, p]).astype(BF16)
            lhs_q = (jnp.exp(b_t + m_in2 - m_t) * qk[:, p * LANE:(p + 1) * LANE].astype(F32)).astype(BF16)
            lhs = jnp.concatenate([by_head(lhs_a), by_head(lhs_q)], axis=1)
            tot = _dot(lhs, jnp.concatenate([v_stack(vb, p), c_pair.astype(BF16)], axis=0))
            floor = jnp.exp(-m_t)
            floor_r = pltpu.roll(floor, L, 1)
            new_rows = []
            for hh in range(2):
                h = 2 * p + hh
                rs = slice(hh * L, (hh + 1) * L)
                m_in = m_all[h:h + 1]
                floor_h = jnp.where(lane_t, floor, floor_r) if hh == 0 else jnp.where(lane_t, floor_r, floor)
                outs.append(tot[rs, :ML_V] / jnp.maximum(jnp.abs(tot[rs, ML_V:]), floor_h))
                b_tot = sc[h:h + 1]
                m_new = b_tot + jnp.maximum(m_in, sc[ML_HEADS + h:ML_HEADS + h + 1])
                carry_w = jnp.exp(b_tot + m_in - m_new)
                inc_w = jnp.exp(b_tot + sc[ML_HEADS + h:ML_HEADS + h + 1] - m_new)
                cw2 = jnp.concatenate([carry_w, carry_w], axis=1)
                iw2 = jnp.concatenate([inc_w, inc_w], axis=1)
                new_rows.append(cw2 * c_pair[rs] + iw2 * inc[rs])
                m_rows.append(m_new)
            states.append(jnp.concatenate(new_rows, axis=0))
        o_ref[0, 0, rows, :] = jnp.concatenate(outs, axis=1).astype(BF16)
        for p in range(ML_HEADS // 2):
            c_ref[0, 0, p] = states[p]
        mm_ref[0, 0] = jnp.concatenate(m_rows + [m_all[ML_HEADS:]], axis=0)

    return _pipelined(nc // group, group, intra, carry_state)


def _scan_kernel(*refs, nc):
    hg_in, ml_in, hg_out, ml_out, hg_scr, ml_scr = (
        refs[a:b] for a, b in zip(np.cumsum([0, N_HG[0], N_ML[0], N_HG[1], N_ML[1], N_HG[2]]),
                                  np.cumsum([N_HG[0], N_ML[0], N_HG[1], N_ML[1], N_HG[2], N_ML[2]])))
    _interleave([_hgrn2_program(*hg_in, *hg_out, *hg_scr, nc=nc),
                 _mlstm_program(*ml_in, *ml_out, *ml_scr, nc=nc)])


def _scans(hq, hv, hlf, qk, mv, gates, s0, c0, m0, dmat, pair_valid, wmask, le2, let2):
    bsz, t, hw = hq.shape
    nc = min(SCAN_CHUNKS, t // CHUNK)
    tb = nc * CHUNK
    nblk = t // tb
    npair = ML_HEADS // 2

    def blk(j, d):
        return jnp.where(d == 0, j, nblk - 1 - j)

    tok = lambda b, d, j: (b, blk(j, d), 0)
    tok_d = lambda b, d, j: (b, d, blk(j, d), 0)
    per_dir = lambda a: pl.BlockSpec((1,) + a.shape[1:], lambda b, d, j: (d,) + (0,) * (a.ndim - 1))
    state = lambda a: pl.BlockSpec((1, 1) + a.shape[2:], lambda b, d, j: (b, d) + (0,) * (a.ndim - 2))
    out_tok = pl.BlockSpec((1, 1, tb, hw), tok_d)
    hg_in = [pl.BlockSpec((1, tb, hw), tok), pl.BlockSpec((1, tb, hw), tok), pl.BlockSpec((1, 1, tb, hw), tok_d),
             per_dir(dmat), per_dir(pair_valid), per_dir(wmask), state(s0)]
    ml_in = [pl.BlockSpec((1, tb, qk.shape[-1]), tok), pl.BlockSpec((1, tb, hw), tok),
             pl.BlockSpec((1, nc) + gates.shape[2:], lambda b, d, j: (b, blk(j, d), 0, 0)),
             per_dir(le2), per_dir(let2), state(c0), state(m0)]
    sds = jax.ShapeDtypeStruct
    hg_scr = [pltpu.VMEM((nc,) + s0.shape[2:], F32),
              pltpu.VMEM((nc, CHUNK, hw), BF16),
              pltpu.VMEM((nc, SUBLANE, hw), F32),
              pltpu.VMEM((nc, CHUNK, hw), F32),
              pltpu.VMEM((CHUNK, hw), F32),
              pltpu.VMEM((CHUNK, hw), F32),
              pltpu.VMEM((CHUNK, hw), F32)]
    ml_scr = [pltpu.VMEM((nc, npair, CHUNK, LANE), F32),
              pltpu.VMEM((nc, npair, 2 * ML_QK, 2 * ML_V), F32),
              pltpu.VMEM((nc, npair, CHUNK, LANE), F32),
              pltpu.VMEM((nc, npair, CHUNK, LANE), F32),
              pltpu.VMEM((nc, SUBLANE, LANE), F32)]
    assert (len(hg_in), 2, len(hg_scr)) == N_HG and (len(ml_in), 3, len(ml_scr)) == N_ML
    return pl.pallas_call(
        functools.partial(_scan_kernel, nc=nc),
        grid=(bsz, 2, nblk),
        in_specs=hg_in + ml_in,
        out_specs=[out_tok, state(s0), out_tok, state(c0), state(m0)],
        out_shape=[sds((bsz, 2, t, hw), BF16), sds(s0.shape, F32),
                   sds((bsz, 2, t, hw), BF16), sds(c0.shape, F32), sds(m0.shape, F32)],
        scratch_shapes=hg_scr + ml_scr,
        compiler_params=_cparams(("parallel", "parallel", "arbitrary")),
        name="scans",
    )(hq, hv, hlf, dmat, pair_valid, wmask, s0, qk, mv, gates, le2, let2, c0, m0)


def _head_rms(x, heads):
    dh = x.shape[-1] // heads
    return jnp.concatenate([_rms(x[:, h * dh:(h + 1) * dh]) for h in range(heads)], axis=1)


def _mix_ffn_kernel(oh_ref, om_ref, hg_ref, mo_ref, x_ref, mod_ref, hnw_ref, mnw_ref, wo_ref,
                    nw_ref, w1_ref, w2_ref, fw_ref, o_ref, a_ref, x_scr, h_scr, *, d_ff, tf):
    mix_gate = mod_ref[0, 0:1, :]
    mods = [mod_ref[0, k:k + 1, :] for k in range(1, 4)]

    def stages(rs):
        def mixed(pr):
            oh = oh_ref[0, 0, pr, :].astype(F32) + oh_ref[0, 1, pr, :].astype(F32)
            om = om_ref[0, 0, pr, :].astype(F32) + om_ref[0, 1, pr, :].astype(F32)
            hg_out = _head_rms(oh, HG_HEADS) * hnw_ref[...] * _silu(hg_ref[0, pr, :].astype(F32))
            ml_out = _sigmoid(mo_ref[0, pr, :].astype(F32)) * (_head_rms(om, ML_HEADS) * mnw_ref[...])
            merged = jnp.concatenate([hg_out, ml_out], axis=1).astype(BF16)
            return x_ref[0, pr, :] + mix_gate * _dot(merged, wo_ref[...])

        def put(y):
            o_ref[0, rs, :] = _rms(y) * fw_ref[...]

        return _Stages(_ffn_stages(mixed, put, mods, nw_ref, w1_ref, w2_ref, a_ref, x_scr, h_scr,
                                   rs, d_ff, tf), _ffn_lead(rs, d_ff, tf))

    _staggered([stages(rs) for rs in _sub_tiles(x_ref.shape[1])])


def _mix_ffn(oh, om, hg, mo, x, mods, hnw, mnw, w_out, nw, w1, w2, fw):
    bsz, t, d = x.shape
    hw = hg.shape[-1]
    d_ff = w2.shape[0]
    tm = min(TOKEN_TILE, t)
    tok = lambda b, i: (b, i, 0)
    tok2 = lambda b, i: (b, 0, i, 0)
    const = lambda b, i: (0, 0)
    resident = lambda a: pl.BlockSpec(a.shape, const, pipeline_mode=pl.Buffered(1))
    return pl.pallas_call(
        functools.partial(_mix_ffn_kernel, d_ff=d_ff, tf=FF_TILE),
        grid=(bsz, t // tm),
        in_specs=[pl.BlockSpec((1, 2, tm, hw), tok2), pl.BlockSpec((1, 2, tm, hw), tok2),
                  pl.BlockSpec((1, tm, hw), tok), pl.BlockSpec((1, tm, hw), tok),
                  pl.BlockSpec((1, tm, d), tok),
                  pl.BlockSpec((1, 4, d), lambda b, i: (b, 0, 0)),
                  pl.BlockSpec((1, hw), const), pl.BlockSpec((1, hw), const), resident(w_out),
                  pl.BlockSpec((1, d), const), resident(w1), resident(w2), pl.BlockSpec((1, d), const)],
        out_specs=pl.BlockSpec((1, tm, d), tok),
        out_shape=jax.ShapeDtypeStruct((bsz, t, d), F32),
        scratch_shapes=[pltpu.VMEM((tm, d_ff), BF16),
                        pltpu.VMEM((tm, d), F32),
                        pltpu.VMEM((tm, d), BF16)],
        compiler_params=_cparams(("parallel", "parallel")),
        name="mix_ffn_final",
    )(oh, om, hg, mo, x, mods, hnw, mnw, w_out, nw, w1, w2, fw)


def kernel(x, c, ctx, c_ctx, w_mod, b_mod, norm1_w, ffn1_w1, ffn1_w2, norm2_w, w_in, ml_gate_b,
           ml_conv_w, ml_conv_b, hg_lb_logits, hg_norm_w, ml_norm_w, w_out, norm3_w, ffn2_w1, ffn2_w2,
           final_norm_w):
    bsz, seq, d = x.shape
    n_ctx = ctx.shape[1]
    assert w_mod.shape[0] == 1, "single-layer kernel"
    assert seq % max(TOKEN_TILE, SCAN_CHUNKS * CHUNK) == 0 and n_ctx % CHUNK == 0 and GRID_W == CHUNK
    hw = HG_HEADS * HG_DIM
    ng = ml_gate_b.shape[-1]

    rows = -(-(bsz + 1) // SUBLANE) * SUBLANE
    cvec = jnp.zeros((rows, d), F32).at[:bsz].set(c).at[bsz].set(c_ctx)
    mods = _modulation(cvec, w_mod[0], b_mod[0][None, :]).reshape(rows, N_MOD, d)
    lat_row = lambda b: b
    ctx_row = lambda b: bsz

    row = lambda a: a.reshape(1, -1)
    w1a, w2a = ffn1_w1[0], ffn1_w2[0]
    x1 = _ffn(x, mods[:, 0:3], lat_row, row(norm1_w[0]), w1a, w2a)
    s1 = _ffn(ctx, mods[:, 0:3], ctx_row, row(norm1_w[0]), w1a, w2a)

    w_in_b = w_in[0].T
    gate_order = np.arange(ng).reshape(2, 2, ML_HEADS // 2, 2).transpose(3, 0, 1, 2).reshape(-1)
    w_gate_t = w_in[0][:, 8 * hw:].T[gate_order].astype(BF16)
    gate_b = ml_gate_b[0][gate_order].reshape(ng, 1)
    dmat, pair_valid, wmask = (jnp.asarray(a) for a in _hgrn2_masks())
    le2, let2 = (jnp.asarray(a) for a in _mlstm_masks())
    conv_w, conv_b = ml_conv_w[0], row(ml_conv_b[0])
    npair = ML_HEADS // 2

    def mixer_scans(s, mod_row, width, s0, c0, m0):
        hq, hv, hg, hlf, mqk, mv, mo, gates = _inproj(
            s, mods[:, 3:5], mod_row, row(norm2_w[0]), w_in_b, w_gate_t, gate_b, hg_lb_logits)
        qk = _conv(mqk, conv_w, conv_b, width)
        oh, s_fin, om, c_fin, m_fin = _scans(hq, hv, hlf, qk, mv, gates, s0, c0, m0,
                                             dmat, pair_valid, wmask, le2, let2)
        return oh, om, hg, mo, s_fin, c_fin, m_fin

    s0 = jnp.zeros((bsz, 2, HG_HEADS, HG_DIM, HG_DIM), F32)
    c0 = jnp.zeros((bsz, 2, npair, 2 * ML_QK, 2 * ML_V), F32)
    m0 = jnp.zeros((bsz, 2, SUBLANE, LANE), F32)
    _, _, _, _, s_ctx, c_ctx_state, m_ctx = mixer_scans(s1, ctx_row, n_ctx, s0, c0, m0)
    oh, om, hg, mo, _, _, _ = mixer_scans(x1, lat_row, GRID_W, s_ctx, c_ctx_state, m_ctx)

    return _mix_ffn(oh, om, hg, mo, x1, mods[:, 5:9], row(hg_norm_w[0]), row(ml_norm_w[0]),
                    w_out[0].astype(BF16), row(norm3_w[0]), ffn2_w1[0], ffn2_w2[0], row(final_norm_w))
```

```python
import functools

import numpy as np
import jax
import jax.numpy as jnp
from jax import lax
from jax.experimental import pallas as pl
from jax.experimental.pallas import tpu as pltpu

F32 = jnp.float32
BF16 = jnp.bfloat16

EPS = 1e-6
CHUNK = 64
GRID_W = 64
N_MOD = 9
HG_HEADS = 4
HG_DIM = 128
ML_HEADS = 4
ML_QK = 64
ML_V = 128
SUB = 16
N_SUB = CHUNK // SUB
GROUP = 4
ML_GROUP = 4
SCAN_CHUNKS = 8
TOKEN_TILE = 512
ROW_SUB = 256
FF_TILE = 256
PIECE = 128
LOG2E = 1.4426950408889634
EXP2_CLAMP = 115.0
LANE = 128
SUBLANE = 8
VMEM_LIMIT = 56 * 1024 * 1024


def _cparams(sem):
    return pltpu.CompilerParams(dimension_semantics=sem, vmem_limit_bytes=VMEM_LIMIT)


def _dot(a, b):
    return jnp.dot(a, b, preferred_element_type=F32)


def _dot_nt(a, b):
    return lax.dot_general(a, b, (((1,), (1,)), ((), ())), preferred_element_type=F32)


def _dot_tn(a, b):
    return lax.dot_general(a, b, (((0,), (0,)), ((), ())), preferred_element_type=F32)


def _sigmoid(x):
    return 1.0 / (1.0 + jnp.exp(-x))


def _silu(x):
    return x * _sigmoid(x)


def _log_sigmoid(x):
    return jnp.minimum(x, 0.0) - jnp.log(1.0 + jnp.exp(-jnp.abs(x)))


def _split3(x):
    hi = x.astype(BF16)
    r = x - hi.astype(F32)
    mid = r.astype(BF16)
    lo = (r - mid.astype(F32)).astype(BF16)
    return hi, mid, lo


def _split2(x):
    hi = x.astype(BF16)
    return hi, (x - hi.astype(F32)).astype(BF16)


def _rms(x):
    return x * lax.rsqrt(jnp.mean(x * x, axis=-1, keepdims=True) + EPS)


def _mod_kernel(c_ref, w_ref, b_ref, o_ref):
    a = _silu(c_ref[...])
    a_hi = a.astype(BF16)
    a_lo = (a - a_hi.astype(F32)).astype(BF16)
    w = w_ref[...]
    w_hi = w.astype(BF16)
    w_lo = (w - w_hi.astype(F32)).astype(BF16)
    o_ref[...] = _dot(a_hi, w_hi) + _dot(a_hi, w_lo) + _dot(a_lo, w_hi) + b_ref[...]


def _modulation(cvec, w_mod, b_mod):
    rows, d = cvec.shape
    n = w_mod.shape[1]
    tn = 1024
    return pl.pallas_call(
        _mod_kernel,
        grid=(n // tn,),
        in_specs=[pl.BlockSpec((rows, d), lambda j: (0, 0)),
                  pl.BlockSpec((d, tn), lambda j: (0, j)),
                  pl.BlockSpec((1, tn), lambda j: (0, j))],
        out_specs=pl.BlockSpec((rows, tn), lambda j: (0, j)),
        out_shape=jax.ShapeDtypeStruct((rows, n), F32),
        compiler_params=_cparams(("arbitrary",)),
        name="modulation",
    )(cvec, w_mod, b_mod)


def _staggered(tiles):
    for _ in range(tiles[0].lead):
        next(tiles[0].gen)
    live = [t.gen for t in tiles]
    while live:
        live = [g for g in live if next(g, None) is not None]


class _Stages:
    def __init__(self, gen, lead):
        self.gen, self.lead = gen, lead


def _ffn_stages(get_x, put_y, mods, nw_ref, w1_ref, w2_ref, a_ref, x_scr, h_scr, rs, d_ff, tf):
    shift, scale, gate = mods
    for p0 in range(rs.start, rs.stop, PIECE):
        pr = slice(p0, min(p0 + PIECE, rs.stop))
        x = get_x(pr)
        x_scr[pr, :] = x
        h_scr[pr, :] = (_rms(x) * nw_ref[...] * (1.0 + scale) + shift).astype(BF16)
        yield True
    h = h_scr[rs, :]
    for c in range(d_ff // tf):
        g = _dot(h, w1_ref[:, c * tf:(c + 1) * tf].astype(BF16))
        u = _dot(h, w1_ref[:, d_ff + c * tf:d_ff + (c + 1) * tf].astype(BF16))
        a_ref[rs, c * tf:(c + 1) * tf] = (_silu(g) * u).astype(BF16)
        yield True
    y = x_scr[rs, :] + 0.5 * gate * _dot(a_ref[rs, :], w2_ref[...].astype(BF16))
    yield True
    put_y(y)
    yield True


def _sub_tiles(rows):
    sub = min(ROW_SUB, rows)
    return [slice(r0, r0 + sub) for r0 in range(0, rows, sub)]


def _ffn_lead(rs, d_ff, tf):
    return -(-(rs.stop - rs.start) // PIECE) + d_ff // tf // 2


def _ffn_kernel(x_ref, mod_ref, nw_ref, w1_ref, w2_ref, o_ref, a_ref, x_scr, h_scr, *, d_ff, tf):
    mods = [mod_ref[0, k:k + 1, :] for k in range(3)]

    def stages(rs):
        def put(y):
            o_ref[0, rs, :] = y
        return _Stages(_ffn_stages(lambda pr: x_ref[0, pr, :], put, mods, nw_ref, w1_ref, w2_ref, a_ref,
                                   x_scr, h_scr, rs, d_ff, tf), _ffn_lead(rs, d_ff, tf))

    _staggered([stages(rs) for rs in _sub_tiles(x_ref.shape[1])])


def _ffn(s, mods, mod_row, nw, w1, w2):
    bsz, t, d = s.shape
    d_ff = w2.shape[0]
    tm = min(TOKEN_TILE, t)
    const = lambda b, i: (0, 0)
    return pl.pallas_call(
        functools.partial(_ffn_kernel, d_ff=d_ff, tf=FF_TILE),
        grid=(bsz, t // tm),
        in_specs=[pl.BlockSpec((1, tm, d), lambda b, i: (b, i, 0)),
                  pl.BlockSpec((1, 3, d), lambda b, i: (mod_row(b), 0, 0)),
                  pl.BlockSpec((1, d), const),
                  pl.BlockSpec((d, 2 * d_ff), const, pipeline_mode=pl.Buffered(1)),
                  pl.BlockSpec((d_ff, d), const, pipeline_mode=pl.Buffered(1))],
        out_specs=pl.BlockSpec((1, tm, d), lambda b, i: (b, i, 0)),
        out_shape=jax.ShapeDtypeStruct((bsz, t, d), F32),
        scratch_shapes=[pltpu.VMEM((tm, d_ff), BF16),
                        pltpu.VMEM((tm, d), F32),
                        pltpu.VMEM((tm, d), BF16)],
        compiler_params=_cparams(("parallel", "parallel")),
        name="ffn",
    )(s, mods, nw, w1, w2)


def _inproj_kernel(x_ref, mod_ref, nw_ref, w_ref, wg_ref, gb_ref, lbl_ref,
                   hq_ref, hv_ref, hg_ref, hlf_ref, mqk_ref, mv_ref, mo_ref, gc_ref, *, hw):
    shift = mod_ref[0, 0:1, :]
    scale = mod_ref[0, 1:2, :]
    lbl = lbl_ref[...]
    e = jnp.exp(lbl - jnp.max(lbl, axis=0, keepdims=True))
    lb = e[0] / jnp.sum(e, axis=0)
    first_half = lax.broadcasted_iota(jnp.int32, (SUBLANE, LANE), 1) < CHUNK
    def stages(rs):
        h = (_rms(x_ref[0, rs, :]) * nw_ref[...] * (1.0 + scale) + shift).astype(BF16)
        yield True

        def proj(k):
            return _dot_nt(h, w_ref[k * hw:(k + 1) * hw, :].astype(BF16))

        hq_ref[0, rs, :] = _silu(proj(0)) * (HG_DIM ** -0.5)
        yield True
        hv_ref[0, rs, :] = proj(1).astype(BF16)
        yield True
        hg_ref[0, rs, :] = proj(2).astype(BF16)
        yield True
        for d in range(2):
            lbd = lb[d:d + 1, :]
            hlf_ref[0, d, rs, :] = jnp.log(lbd + (1.0 - lbd) * _sigmoid(proj(3 + d))) * LOG2E
            yield True
        mqk_ref[0, rs, :] = proj(5).astype(BF16)
        yield True
        mv_ref[0, rs, :] = proj(6).astype(BF16)
        yield True
        mo_ref[0, rs, :] = proj(7).astype(BF16)
        gt = _dot_nt(wg_ref[...], h) + gb_ref[...]
        for m in range((rs.stop - rs.start) // LANE):
            top = gt[0:8, m * LANE:(m + 1) * LANE]
            bot = gt[8:16, m * LANE:(m + 1) * LANE]
            c0 = rs.start // CHUNK + 2 * m
            gc_ref[0, c0] = jnp.where(first_half, top, pltpu.roll(bot, CHUNK, 1))
            gc_ref[0, c0 + 1] = jnp.where(first_half, pltpu.roll(top, CHUNK, 1), bot)
        yield True

    _staggered([_Stages(stages(rs), 8) for rs in _sub_tiles(x_ref.shape[1])])


def _inproj(s, mods, mod_row, nw, w_in, w_gate_t, gate_b, lb_logits):
    bsz, t, d = s.shape
    hw = HG_HEADS * HG_DIM
    ng = gate_b.shape[0]
    tm = min(TOKEN_TILE, t)
    tok4 = lambda b, i: (b, i, 0, 0)
    const = lambda b, i: (0, 0)
    tok = lambda b, i: (b, i, 0)
    tok2 = lambda b, i: (b, 0, i, 0)
    f = lambda dt, *shape: jax.ShapeDtypeStruct(shape, dt)
    return pl.pallas_call(
        functools.partial(_inproj_kernel, hw=hw),
        grid=(bsz, t // tm),
        in_specs=[pl.BlockSpec((1, tm, d), tok),
                  pl.BlockSpec((1, 2, d), lambda b, i: (mod_row(b), 0, 0)),
                  pl.BlockSpec((1, d), const),
                  pl.BlockSpec(w_in.shape, const, pipeline_mode=pl.Buffered(1)),
                  pl.BlockSpec(w_gate_t.shape, const),
                  pl.BlockSpec((ng, 1), const),
                  pl.BlockSpec(lb_logits.shape, lambda b, i: (0, 0, 0))],
        out_specs=[pl.BlockSpec((1, tm, hw), tok), pl.BlockSpec((1, tm, hw), tok),
                   pl.BlockSpec((1, tm, hw), tok),
                   pl.BlockSpec((1, 2, tm, hw), tok2),
                   pl.BlockSpec((1, tm, hw), tok), pl.BlockSpec((1, tm, hw), tok),
                   pl.BlockSpec((1, tm, hw), tok), pl.BlockSpec((1, tm // CHUNK, ng // 2, LANE), tok4)],
        out_shape=[f(F32, bsz, t, hw), f(BF16, bsz, t, hw), f(BF16, bsz, t, hw),
                   f(F32, bsz, 2, t, hw),
                   f(BF16, bsz, t, hw), f(BF16, bsz, t, hw), f(BF16, bsz, t, hw),
                   f(F32, bsz, t // CHUNK, ng // 2, LANE)],
        compiler_params=_cparams(("parallel", "parallel")),
        name="inproj",
    )(s, mods, nw, w_in, w_gate_t, gate_b, lb_logits)


def _conv_kernel(x_ref, w_ref, b_ref, o_ref, *, width, q_tiles):
    x = x_ref[0].astype(F32)
    t = x.shape[0]
    col = lax.broadcasted_iota(jnp.int32, x.shape, 0) % width
    xl = jnp.where(col == 0, 0.0, pltpu.roll(x, 1, 0))
    xr = jnp.where(col == width - 1, 0.0, pltpu.roll(x, t - 1, 0))

    def row(di):
        return (w_ref[di, 0:1, :] * xl + w_ref[di, 1:2, :] * x + w_ref[di, 2:3, :] * xr)

    y = row(1)
    if t > width:
        pad = jnp.zeros((width, x.shape[1]), F32)
        y = y + jnp.concatenate([pad, row(0)[:t - width]], axis=0)
        y = y + jnp.concatenate([row(2)[width:], pad], axis=0)
    y = _silu(y + b_ref[...])
    qscale = jnp.where(pl.program_id(1) < q_tiles, ML_QK ** -0.5, 1.0)
    o_ref[0] = (y * qscale).astype(BF16)


def _conv(x, w, b, width):
    bsz, t, ch = x.shape
    return pl.pallas_call(
        functools.partial(_conv_kernel, width=width, q_tiles=ML_HEADS * ML_QK // LANE),
        grid=(bsz, ch // LANE),
        in_specs=[pl.BlockSpec((1, t, LANE), lambda b_, c: (b_, 0, c)),
                  pl.BlockSpec((3, 3, LANE), lambda b_, c: (0, 0, c)),
                  pl.BlockSpec((1, LANE), lambda b_, c: (0, c))],
        out_specs=pl.BlockSpec((1, t, LANE), lambda b_, c: (b_, 0, c)),
        out_shape=jax.ShapeDtypeStruct((bsz, t, ch), BF16),
        compiler_params=_cparams(("parallel", "parallel")),
        name="conv",
    )(x, w, b)


def _positions():
    t = np.arange(CHUNK)
    return [t, CHUNK - 1 - t]


def _hgrn2_masks():
    ds, vs, ws = [], [], []
    for p in _positions():
        le = (p[None, :] <= p[:, None]).astype(np.float32)
        order = [p[SUB * j] // SUB for j in range(N_SUB)]
        ref = np.zeros((32, CHUNK), np.float32)
        valid = np.zeros((N_SUB * N_SUB, 1), np.float32)
        for j in range(N_SUB):
            ref[j] = p <= SUB * order[j] + SUB // 2 - 1
            ref[8 + j] = p <= SUB * order[j] + SUB - 1
        ref[N_SUB] = 1.0
        for j in range(N_SUB):
            for i in range(N_SUB):
                if order[i] < order[j]:
                    ref[16 + N_SUB * j + i] = ref[j] - ref[8 + i]
                    valid[N_SUB * j + i] = 1.0
        ds.append(np.concatenate([le, ref], axis=0))
        vs.append(np.broadcast_to(valid, (N_SUB * N_SUB, HG_HEADS * HG_DIM)))
        tok_blk = np.arange(CHUNK)[:, None] // SUB
        ws.append(np.concatenate([(tok_blk == j) * le for j in range(N_SUB)], axis=1))
    return (np.stack(ds).astype(np.float32), np.stack(vs).astype(np.float32),
            np.stack(ws).astype(np.float32))


def _mlstm_masks():
    le2s, let2s = [], []
    for p in _positions():
        le = (p[None, :] <= p[:, None]).astype(np.float32)
        le2s.append(np.concatenate([le, le], axis=1))
        let2 = np.zeros((2 * CHUNK, 2 * CHUNK), np.float32)
        let2[:CHUNK, :CHUNK] = le.T
        let2[CHUNK:, CHUNK:] = le.T
        let2s.append(let2)
    return np.stack(le2s), np.stack(let2s)


def _pipelined(n_groups, group, stages_of, step_of):
    queue = []
    for g in range(n_groups):
        for _ in stages_of(g):
            if queue:
                queue.pop(0)()
            yield True
        while queue:
            queue.pop(0)()
            yield True
        queue = [functools.partial(step_of, g * group + i) for i in range(group)]
    for step in queue:
        step()
        yield True


def _interleave(programs):
    live = list(programs)
    while live:
        live = [p for p in live if next(p, None) is not None]


def _chunk_rows(c, nc, direction):
    cc = jnp.where(direction == 0, c, nc - 1 - c)
    return pl.multiple_of(cc * CHUNK, CHUNK)


def _hgrn2_program(q_ref, v_ref, lf_ref, d_ref, pv_ref, w_ref, s0_ref, o_ref, st_ref,
                   u_ref, qin_ref, g_ref, oi_ref, bx_ref, kx_ref, vx_ref, *, nc):
    direction = pl.program_id(1)

    @pl.when(pl.program_id(2) == 0)
    def _():
        st_ref[...] = s0_ref[...]

    dmat = d_ref[0].astype(BF16)
    pair_valid = pv_ref[0]
    wmask = w_ref[0] > 0.0
    L = CHUNK
    hw = q_ref.shape[-1]
    upper_rows = lax.broadcasted_iota(jnp.int32, (L, 2 * L), 0) < L // 2

    group = min(GROUP, nc)

    def intra(grp):
        cs = [grp * group + i for i in range(group)]
        rows = [pl.ds(_chunk_rows(c, nc, direction), L) for c in cs]
        items = [(i, h) for i in range(group) for h in range(HG_HEADS)]
        lanes = [slice(h * HG_DIM, (h + 1) * HG_DIM) for h in range(HG_HEADS)]
        lfs = [lf_ref[0, 0, r, :] for r in rows]
        parts = [_split2(lf) for lf in lfs]
        e_all = sum(_dot(dmat, jnp.concatenate([parts[i][n] for i in range(group)], axis=1))
                    for n in range(2))
        yield
        q_mids, k_mids, k_outs, vbs, spans = [], [], [], [], []
        for i in range(group):
            e = e_all[:, i * hw:(i + 1) * hw]
            q = q_ref[0, rows[i], :]
            k = 1.0 - jnp.exp2(lfs[i])
            b = e[0:L]
            b_tot = e[L + N_SUB:L + N_SUB + 1]
            sub_rows = lambda r0: jnp.concatenate(
                [jnp.broadcast_to(e[r0 + j:r0 + j + 1], (SUB, hw)) for j in range(N_SUB)], axis=0)
            mid_rows = sub_rows(L)
            end_rows = sub_rows(L + 8)
            to_mid = b - mid_rows
            spans.append(jnp.max(jnp.abs(to_mid)))
            q_mids.append((q * jnp.exp2(jnp.minimum(to_mid, EXP2_CLAMP))).astype(BF16))
            k_diag = k * jnp.exp2(jnp.minimum(-to_mid, EXP2_CLAMP))
            k_end = k * jnp.exp2(end_rows - b)
            cross = jnp.exp2(e[L + 16:L + 32]) * pair_valid
            slabs = []
            for j in range(N_SUB):
                for ib in range(N_SUB):
                    rs = slice(ib * SUB, (ib + 1) * SUB)
                    row = N_SUB * j + ib
                    slabs.append(k_diag[rs] if ib == j else k_end[rs] * cross[row:row + 1])
            k_mids.append(jnp.concatenate(slabs, axis=0).astype(BF16))
            k_outs.append((k * jnp.exp2(b_tot - b)).astype(BF16))
            vbs.append(v_ref[0, rows[i], :])
            qin_ref[cs[i]] = (q * jnp.exp2(b)).astype(BF16)
            g_ref[cs[i]] = jnp.broadcast_to(jnp.exp2(b_tot), (SUBLANE, hw))
            yield
        scores = [_dot_nt(q_mids[i][:, lanes[h]], k_mids[i][:, lanes[h]]) for i, h in items]
        yield
        a_s = [jnp.where(wmask, s, 0.0) for s in scores]
        a_s = [jnp.where(upper_rows, a[:, :2 * L], a[:, 2 * L:]).astype(BF16) for a in a_s]
        v2s = [jnp.concatenate([vb, vb], axis=0) for vb in vbs]
        outs = [_dot(a_s[n], v2s[i][:, lanes[h]]) for n, (i, h) in enumerate(items)]
        yield
        for i in range(group):
            oi_ref[cs[i]] = jnp.concatenate(outs[i * HG_HEADS:(i + 1) * HG_HEADS], axis=1)
        yield
        for i, h in items:
            u_ref[cs[i], h] = _dot_tn(vbs[i][:, lanes[h]], k_outs[i][:, lanes[h]])

        @pl.when(functools.reduce(jnp.maximum, spans) > EXP2_CLAMP)
        def _():
            for i in range(group):
                exact_intra(cs[i], rows[i])
        yield

    def exact_intra(c, rows):
        lf = lf_ref[0, 0, rows, :]
        bx_ref[...] = sum(_dot(dmat[0:L], x) for x in _split3(lf))
        kx_ref[...] = 1.0 - jnp.exp2(lf)
        vx_ref[...] = v_ref[0, rows, :].astype(F32)
        q = q_ref[0, rows, :]
        tok = lax.broadcasted_iota(jnp.int32, (L, 1), 0)
        pos = jnp.where(direction == 0, tok, L - 1 - tok)

        def add_source(s, acc):
            b_s = bx_ref[pl.ds(s, 1), :]
            z = q * kx_ref[pl.ds(s, 1), :] * jnp.exp2(jnp.minimum(bx_ref[...] - b_s, 0.0))
            v_s = vx_ref[pl.ds(s, 1), :]
            seen = pos >= jnp.where(direction == 0, s, L - 1 - s)
            cols = [jnp.where(seen, jnp.sum(z[:, ln], axis=1, keepdims=True), 0.0) * v_s[:, ln]
                    for ln in (slice(h * HG_DIM, (h + 1) * HG_DIM) for h in range(HG_HEADS))]
            return acc + jnp.concatenate(cols, axis=1)

        oi_ref[c] = lax.fori_loop(0, L, add_source, jnp.zeros((L, hw), F32))

    def carry_state(c):
        rows = pl.ds(_chunk_rows(c, nc, direction), L)
        q_in = qin_ref[c]
        g = g_ref[c][0:1]
        outs, states = [], []
        for h in range(HG_HEADS):
            ln = slice(h * HG_DIM, (h + 1) * HG_DIM)
            st = st_ref[0, 0, h]
            outs.append(_dot_nt(q_in[:, ln], st.astype(BF16)))
            states.append(g[:, ln] * st + u_ref[c, h])
        o_ref[0, 0, rows, :] = (oi_ref[c] + jnp.concatenate(outs, axis=1)).astype(BF16)
        for h in range(HG_HEADS):
            st_ref[0, 0, h] = states[h]

    return _pipelined(nc // group, group, intra, carry_state)


N_HG = (7, 2, 7)
N_ML = (7, 3, 5)


def _mlstm_program(qk_ref, v_ref, g_ref, le2_ref, let2_ref, c0_ref, m0_ref,
                   o_ref, c_ref, mm_ref,
                   al_ref, u_ref, ct_ref, mloc_ref, sc_ref, *, nc):
    direction = pl.program_id(1)

    @pl.when(pl.program_id(2) == 0)
    def _():
        c_ref[...] = c0_ref[...]
        mm_ref[...] = m0_ref[...]

    le2 = le2_ref[0].astype(BF16)
    let2 = let2_ref[0].astype(BF16)
    causal = le2_ref[0] > 0.0
    L = CHUNK
    neg = -jnp.inf
    lane = lax.broadcasted_iota(jnp.int32, (1, LANE), 1)
    seg = [lane < L, lane >= L]
    lane_t = lax.broadcasted_iota(jnp.int32, (L, LANE), 1) < L
    npair = ML_HEADS // 2
    ones_v = jnp.ones((L, ML_V), BF16)
    zb = jnp.zeros((L, LANE), BF16)

    def by_head(x):
        return jnp.concatenate([jnp.where(lane_t, x, zb), jnp.where(lane_t, zb, x)], axis=0)

    def v_stack(vb, p):
        return jnp.concatenate(
            [jnp.concatenate([vb[:, 2 * p * ML_V:(2 * p + 1) * ML_V], ones_v], axis=1),
             jnp.concatenate([vb[:, (2 * p + 1) * ML_V:(2 * p + 2) * ML_V], ones_v], axis=1)], axis=0)

    group = min(ML_GROUP, nc)

    def intra(grp):
        cs = [grp * group + i for i in range(group)]
        ccs = [jnp.where(direction == 0, c, nc - 1 - c) for c in cs]
        rows = [pl.ds(pl.multiple_of(cc * L, L), L) for cc in ccs]
        pairs = [(i, p) for i in range(group) for p in range(npair)]
        gs = [jnp.where(direction == 0, g_ref[0, cc][0:4], g_ref[0, cc][4:8]) for cc in ccs]
        pad = [jnp.zeros((-2 * group % SUBLANE, LANE), F32)] if 2 * group % SUBLANE else []
        i2 = jnp.concatenate([g[0:2] for g in gs] + pad, axis=0)
        lf2 = _log_sigmoid(jnp.concatenate([g[2:4] for g in gs] + pad, axis=0))
        parts = _split3(lf2)
        r2 = i2 - sum(_dot(x, let2) for x in parts)
        g_a = jnp.max(jnp.where(seg[0], r2, neg), axis=1, keepdims=True)
        g_b = jnp.max(jnp.where(seg[1], r2, neg), axis=1, keepdims=True)
        e_w = jnp.exp(r2 - jnp.where(seg[0], g_a, g_b))
        bt_a = jnp.sum(jnp.where(seg[0], lf2, 0.0), axis=1, keepdims=True)
        bt_b = jnp.sum(jnp.where(seg[1], lf2, 0.0), axis=1, keepdims=True)
        yield

        def head_rows(x, i):
            zero = jnp.zeros_like(x[0:1])
            return jnp.concatenate(
                [jnp.broadcast_to(jnp.where(seg[h % 2], x[2 * i + h // 2:2 * i + h // 2 + 1], zero),
                                  (L, LANE)) for h in range(ML_HEADS)], axis=0)

        c_ts = [sum(_dot_nt(le2, head_rows(x, i)) for x in parts[:2]) for i in range(group)]
        for i in range(group):
            sc_ref[cs[i]] = jnp.concatenate(
                [jnp.broadcast_to(x[2 * i + p:2 * i + p + 1], (1, LANE))
                 for x2 in ((bt_a, bt_b), (g_a, g_b)) for p in range(npair) for x in x2], axis=0)
        yield
        qks = [qk_ref[0, r, :] for r in rows]
        vbs = [v_ref[0, r, :] for r in rows]
        q_ps = [qks[i][:, p * LANE:(p + 1) * LANE] for i, p in pairs]
        k_ps = [qks[i][:, (npair + p) * LANE:(npair + p + 1) * LANE] for i, p in pairs]
        kbs = [by_head(k) for k in k_ps]
        scores = [_dot_nt(q, kb) for q, kb in zip(q_ps, kbs)]
        yield
        for n, (i, p) in enumerate(pairs):
            t1 = c_ts[i][:, p * LANE:(p + 1) * LANE]
            d_log = jnp.where(causal, t1 + r2[2 * i + p:2 * i + p + 1], neg)
            m_a = jnp.max(jnp.where(lane_t, d_log, neg), axis=1, keepdims=True)
            m_b = jnp.max(jnp.where(lane_t, neg, d_log), axis=1, keepdims=True)
            m_loc = jnp.where(lane_t, m_a, m_b)
            ct_ref[cs[i], p] = t1
            mloc_ref[cs[i], p] = m_loc
            al_ref[cs[i], p] = scores[n] * jnp.exp(d_log - m_loc)
        yield
        v_stacks = [v_stack(vbs[i], p) for i, p in pairs]
        yield
        kts = [(kb.astype(F32).T * e_w[2 * i + p:2 * i + p + 1]).astype(BF16) for kb, (i, p) in zip(kbs, pairs)]
        for n, (i, p) in enumerate(pairs):
            u_ref[cs[i], p] = _dot(kts[n], v_stacks[n])
        yield

    def carry_state(c):
        cc = jnp.where(direction == 0, c, nc - 1 - c)
        rows = pl.ds(pl.multiple_of(cc * L, L), L)
        m_all = mm_ref[0, 0]
        sc = sc_ref[c]
        qk = qk_ref[0, rows, :]
        vb = v_ref[0, rows, :]
        outs, states, m_rows = [], [], []
        for p in range(ML_HEADS // 2):
            c_pair = c_ref[0, 0, p]
            inc = u_ref[c, p]
            b_t = ct_ref[c, p]
            m_loc = mloc_ref[c, p]
            m_in2 = jnp.where(seg[0], m_all[2 * p:2 * p + 1], m_all[2 * p + 1:2 * p + 2])
            m_t = jnp.maximum(m_loc, b_t + m_in2)
            lhs_a = (jnp.exp(m_loc - m_t) * al_ref[c, p]).astype(BF16)
            lhs_q = (jnp.exp(b_t + m_in2 - m_t) * qk[:, p * LANE:(p + 1) * LANE].astype(F32)).astype(BF16)
            lhs = jnp.concatenate([by_head(lhs_a), by_head(lhs_q)], axis=1)
            tot = _dot(lhs, jnp.concatenate([v_stack(vb, p), c_pair.astype(BF16)], axis=0))
            floor = jnp.exp(-m_t)
            floor_r = pltpu.roll(floor, L, 1)
            new_rows = []
            for hh in range(2):
                h = 2 * p + hh
                rs = slice(hh * L, (hh + 1) * L)
                m_in = m_all[h:h + 1]
                floor_h = jnp.where(lane_t, floor, floor_r) if hh == 0 else jnp.where(lane_t, floor_r, floor)
                outs.append(tot[rs, :ML_V] / jnp.maximum(jnp.abs(tot[rs, ML_V:]), floor_h))
                b_tot = sc[h:h + 1]
                m_new = b_tot + jnp.maximum(m_in, sc[ML_HEADS + h:ML_HEADS + h + 1])
                carry_w = jnp.exp(b_tot + m_in - m_new)
                inc_w = jnp.exp(b_tot + sc[ML_HEADS + h:ML_HEADS + h + 1] - m_new)
                cw2 = jnp.concatenate([carry_w, carry_w], axis=1)
                iw2 = jnp.concatenate([inc_w, inc_w], axis=1)
                new_rows.append(cw2 * c_pair[rs] + iw2 * inc[rs])
                m_rows.append(m_new)
            states.append(jnp.concatenate(new_rows, axis=0))
        o_ref[0, 0, rows, :] = jnp.concatenate(outs, axis=1).astype(BF16)
        for p in range(ML_HEADS // 2):
            c_ref[0, 0, p] = states[p]
        mm_ref[0, 0] = jnp.concatenate(m_rows + [m_all[ML_HEADS:]], axis=0)

    return _pipelined(nc // group, group, intra, carry_state)


def _scan_kernel(*refs, nc):
    hg_in, ml_in, hg_out, ml_out, hg_scr, ml_scr = (
        refs[a:b] for a, b in zip(np.cumsum([0, N_HG[0], N_ML[0], N_HG[1], N_ML[1], N_HG[2]]),
                                  np.cumsum([N_HG[0], N_ML[0], N_HG[1], N_ML[1], N_HG[2], N_ML[2]])))
    _interleave([_hgrn2_program(*hg_in, *hg_out, *hg_scr, nc=nc),
                 _mlstm_program(*ml_in, *ml_out, *ml_scr, nc=nc)])


def _scans(hq, hv, hlf, qk, mv, gates, s0, c0, m0, dmat, pair_valid, wmask, le2, let2):
    bsz, t, hw = hq.shape
    nc = min(SCAN_CHUNKS, t // CHUNK)
    tb = nc * CHUNK
    nblk = t // tb
    npair = ML_HEADS // 2

    def blk(j, d):
        return jnp.where(d == 0, j, nblk - 1 - j)

    tok = lambda b, d, j: (b, blk(j, d), 0)
    tok_d = lambda b, d, j: (b, d, blk(j, d), 0)
    per_dir = lambda a: pl.BlockSpec((1,) + a.shape[1:], lambda b, d, j: (d,) + (0,) * (a.ndim - 1))
    state = lambda a: pl.BlockSpec((1, 1) + a.shape[2:], lambda b, d, j: (b, d) + (0,) * (a.ndim - 2))
    out_tok = pl.BlockSpec((1, 1, tb, hw), tok_d)
    hg_in = [pl.BlockSpec((1, tb, hw), tok), pl.BlockSpec((1, tb, hw), tok), pl.BlockSpec((1, 1, tb, hw), tok_d),
             per_dir(dmat), per_dir(pair_valid), per_dir(wmask), state(s0)]
    ml_in = [pl.BlockSpec((1, tb, qk.shape[-1]), tok), pl.BlockSpec((1, tb, hw), tok),
             pl.BlockSpec((1, nc) + gates.shape[2:], lambda b, d, j: (b, blk(j, d), 0, 0)),
             per_dir(le2), per_dir(let2), state(c0), state(m0)]
    sds = jax.ShapeDtypeStruct
    hg_scr = [pltpu.VMEM((nc,) + s0.shape[2:], F32),
              pltpu.VMEM((nc, CHUNK, hw), BF16),
              pltpu.VMEM((nc, SUBLANE, hw), F32),
              pltpu.VMEM((nc, CHUNK, hw), F32),
              pltpu.VMEM((CHUNK, hw), F32),
              pltpu.VMEM((CHUNK, hw), F32),
              pltpu.VMEM((CHUNK, hw), F32)]
    ml_scr = [pltpu.VMEM((nc, npair, CHUNK, LANE), F32),
              pltpu.VMEM((nc, npair, 2 * ML_QK, 2 * ML_V), F32),
              pltpu.VMEM((nc, npair, CHUNK, LANE), F32),
              pltpu.VMEM((nc, npair, CHUNK, LANE), F32),
              pltpu.VMEM((nc, SUBLANE, LANE), F32)]
    assert (len(hg_in), 2, len(hg_scr)) == N_HG and (len(ml_in), 3, len(ml_scr)) == N_ML
    return pl.pallas_call(
        functools.partial(_scan_kernel, nc=nc),
        grid=(bsz, 2, nblk),
        in_specs=hg_in + ml_in,
        out_specs=[out_tok, state(s0), out_tok, state(c0), state(m0)],
        out_shape=[sds((bsz, 2, t, hw), BF16), sds(s0.shape, F32),
                   sds((bsz, 2, t, hw), BF16), sds(c0.shape, F32), sds(m0.shape, F32)],
        scratch_shapes=hg_scr + ml_scr,
        compiler_params=_cparams(("parallel", "parallel", "arbitrary")),
        name="scans",
    )(hq, hv, hlf, dmat, pair_valid, wmask, s0, qk, mv, gates, le2, let2, c0, m0)


def _head_rms(x, heads):
    dh = x.shape[-1] // heads
    return jnp.concatenate([_rms(x[:, h * dh:(h + 1) * dh]) for h in range(heads)], axis=1)


def _mix_ffn_kernel(oh_ref, om_ref, hg_ref, mo_ref, x_ref, mod_ref, hnw_ref, mnw_ref, wo_ref,
                    nw_ref, w1_ref, w2_ref, fw_ref, o_ref, a_ref, x_scr, h_scr, *, d_ff, tf):
    mix_gate = mod_ref[0, 0:1, :]
    mods = [mod_ref[0, k:k + 1, :] for k in range(1, 4)]

    def stages(rs):
        def mixed(pr):
            oh = oh_ref[0, 0, pr, :].astype(F32) + oh_ref[0, 1, pr, :].astype(F32)
            om = om_ref[0, 0, pr, :].astype(F32) + om_ref[0, 1, pr, :].astype(F32)
            hg_out = _head_rms(oh, HG_HEADS) * hnw_ref[...] * _silu(hg_ref[0, pr, :].astype(F32))
            ml_out = _sigmoid(mo_ref[0, pr, :].astype(F32)) * (_head_rms(om, ML_HEADS) * mnw_ref[...])
            merged = jnp.concatenate([hg_out, ml_out], axis=1).astype(BF16)
            return x_ref[0, pr, :] + mix_gate * _dot(merged, wo_ref[...])

        def put(y):
            o_ref[0, rs, :] = _rms(y) * fw_ref[...]

        return _Stages(_ffn_stages(mixed, put, mods, nw_ref, w1_ref, w2_ref, a_ref, x_scr, h_scr,
                                   rs, d_ff, tf), _ffn_lead(rs, d_ff, tf))

    _staggered([stages(rs) for rs in _sub_tiles(x_ref.shape[1])])


def _mix_ffn(oh, om, hg, mo, x, mods, hnw, mnw, w_out, nw, w1, w2, fw):
    bsz, t, d = x.shape
    hw = hg.shape[-1]
    d_ff = w2.shape[0]
    tm = min(TOKEN_TILE, t)
    tok = lambda b, i: (b, i, 0)
    tok2 = lambda b, i: (b, 0, i, 0)
    const = lambda b, i: (0, 0)
    resident = lambda a: pl.BlockSpec(a.shape, const, pipeline_mode=pl.Buffered(1))
    return pl.pallas_call(
        functools.partial(_mix_ffn_kernel, d_ff=d_ff, tf=FF_TILE),
        grid=(bsz, t // tm),
        in_specs=[pl.BlockSpec((1, 2, tm, hw), tok2), pl.BlockSpec((1, 2, tm, hw), tok2),
                  pl.BlockSpec((1, tm, hw), tok), pl.BlockSpec((1, tm, hw), tok),
                  pl.BlockSpec((1, tm, d), tok),
                  pl.BlockSpec((1, 4, d), lambda b, i: (b, 0, 0)),
                  pl.BlockSpec((1, hw), const), pl.BlockSpec((1, hw), const), resident(w_out),
                  pl.BlockSpec((1, d), const), resident(w1), resident(w2), pl.BlockSpec((1, d), const)],
        out_specs=pl.BlockSpec((1, tm, d), tok),
        out_shape=jax.ShapeDtypeStruct((bsz, t, d), F32),
        scratch_shapes=[pltpu.VMEM((tm, d_ff), BF16),
                        pltpu.VMEM((tm, d), F32),
                        pltpu.VMEM((tm, d), BF16)],
        compiler_params=_cparams(("parallel", "parallel")),
        name="mix_ffn_final",
    )(oh, om, hg, mo, x, mods, hnw, mnw, w_out, nw, w1, w2, fw)


def kernel(x, c, ctx, c_ctx, w_mod, b_mod, norm1_w, ffn1_w1, ffn1_w2, norm2_w, w_in, ml_gate_b,
           ml_conv_w, ml_conv_b, hg_lb_logits, hg_norm_w, ml_norm_w, w_out, norm3_w, ffn2_w1, ffn2_w2,
           final_norm_w):
    bsz, seq, d = x.shape
    n_ctx = ctx.shape[1]
    assert w_mod.shape[0] == 1, "single-layer kernel"
    assert seq % max(TOKEN_TILE, SCAN_CHUNKS * CHUNK) == 0 and n_ctx % CHUNK == 0 and GRID_W == CHUNK
    hw = HG_HEADS * HG_DIM
    ng = ml_gate_b.shape[-1]

    rows = -(-(bsz + 1) // SUBLANE) * SUBLANE
    cvec = jnp.zeros((rows, d), F32).at[:bsz].set(c).at[bsz].set(c_ctx)
    mods = _modulation(cvec, w_mod[0], b_mod[0][None, :]).reshape(rows, N_MOD, d)
    lat_row = lambda b: b
    ctx_row = lambda b: bsz

    row = lambda a: a.reshape(1, -1)
    w1a, w2a = ffn1_w1[0], ffn1_w2[0]
    x1 = _ffn(x, mods[:, 0:3], lat_row, row(norm1_w[0]), w1a, w2a)
    s1 = _ffn(ctx, mods[:, 0:3], ctx_row, row(norm1_w[0]), w1a, w2a)

    w_in_b = w_in[0].T
    gate_order = np.arange(ng).reshape(2, 2, ML_HEADS // 2, 2).transpose(3, 0, 1, 2).reshape(-1)
    w_gate_t = w_in[0][:, 8 * hw:].T[gate_order].astype(BF16)
    gate_b = ml_gate_b[0][gate_order].reshape(ng, 1)
    dmat, pair_valid, wmask = (jnp.asarray(a) for a in _hgrn2_masks())
    le2, let2 = (jnp.asarray(a) for a in _mlstm_masks())
    conv_w, conv_b = ml_conv_w[0], row(ml_conv_b[0])
    npair = ML_HEADS // 2

    def mixer_scans(s, mod_row, width, s0, c0, m0):
        hq, hv, hg, hlf, mqk, mv, mo, gates = _inproj(
            s, mods[:, 3:5], mod_row, row(norm2_w[0]), w_in_b, w_gate_t, gate_b, hg_lb_logits)
        qk = _conv(mqk, conv_w, conv_b, width)
        oh, s_fin, om, c_fin, m_fin = _scans(hq, hv, hlf, qk, mv, gates, s0, c0, m0,
                                             dmat, pair_valid, wmask, le2, let2)
        return oh, om, hg, mo, s_fin, c_fin, m_fin

    s0 = jnp.zeros((bsz, 2, HG_HEADS, HG_DIM, HG_DIM), F32)
    c0 = jnp.zeros((bsz, 2, npair, 2 * ML_QK, 2 * ML_V), F32)
    m0 = jnp.zeros((bsz, 2, SUBLANE, LANE), F32)
    _, _, _, _, s_ctx, c_ctx_state, m_ctx = mixer_scans(s1, ctx_row, n_ctx, s0, c0, m0)
    oh, om, hg, mo, _, _, _ = mixer_scans(x1, lat_row, GRID_W, s_ctx, c_ctx_state, m_ctx)

    return _mix_ffn(oh, om, hg, mo, x1, mods[:, 5:9], row(hg_norm_w[0]), row(ml_norm_w[0]),
                    w_out[0].astype(BF16), row(norm3_w[0]), ffn2_w1[0], ffn2_w2[0], row(final_norm_w))
```

```python
import functools

import numpy as np
import jax
import jax.numpy as jnp
from jax import lax
from jax.experimental import pallas as pl
from jax.experimental.pallas import tpu as pltpu

F32 = jnp.float32
BF16 = jnp.bfloat16

EPS = 1e-6
CHUNK = 64
GRID_W = 64
N_MOD = 9
HG_HEADS = 4
HG_DIM = 128
ML_HEADS = 4
ML_QK = 64
ML_V = 128
SUB = 16
N_SUB = CHUNK // SUB
GROUP = 4
ML_GROUP = 4
SCAN_CHUNKS = 8
TOKEN_TILE = 512
ROW_SUB = 256
FF_TILE = 256
PIECE = 128
LOG2E = 1.4426950408889634
EXP2_CLAMP = 115.0
LANE = 128
SUBLANE = 8
VMEM_LIMIT = 56 * 1024 * 1024


def _cparams(sem):
    return pltpu.CompilerParams(dimension_semantics=sem, vmem_limit_bytes=VMEM_LIMIT)


def _dot(a, b):
    return jnp.dot(a, b, preferred_element_type=F32)


def _dot_nt(a, b):
    return lax.dot_general(a, b, (((1,), (1,)), ((), ())), preferred_element_type=F32)


def _dot_tn(a, b):
    return lax.dot_general(a, b, (((0,), (0,)), ((), ())), preferred_element_type=F32)


def _sigmoid(x):
    return 1.0 / (1.0 + jnp.exp(-x))


def _silu(x):
    return x * _sigmoid(x)


def _log_sigmoid(x):
    return jnp.minimum(x, 0.0) - jnp.log(1.0 + jnp.exp(-jnp.abs(x)))


def _split3(x):
    hi = x.astype(BF16)
    r = x - hi.astype(F32)
    mid = r.astype(BF16)
    lo = (r - mid.astype(F32)).astype(BF16)
    return hi, mid, lo


def _split2(x):
    hi = x.astype(BF16)
    return hi, (x - hi.astype(F32)).astype(BF16)


def _rms(x):
    return x * lax.rsqrt(jnp.mean(x * x, axis=-1, keepdims=True) + EPS)


def _mod_kernel(c_ref, w_ref, b_ref, o_ref):
    a = _silu(c_ref[...])
    a_hi = a.astype(BF16)
    a_lo = (a - a_hi.astype(F32)).astype(BF16)
    w = w_ref[...]
    w_hi = w.astype(BF16)
    w_lo = (w - w_hi.astype(F32)).astype(BF16)
    o_ref[...] = _dot(a_hi, w_hi) + _dot(a_hi, w_lo) + _dot(a_lo, w_hi) + b_ref[...]


def _modulation(cvec, w_mod, b_mod):
    rows, d = cvec.shape
    n = w_mod.shape[1]
    tn = 1024
    return pl.pallas_call(
        _mod_kernel,
        grid=(n // tn,),
        in_specs=[pl.BlockSpec((rows, d), lambda j: (0, 0)),
                  pl.BlockSpec((d, tn), lambda j: (0, j)),
                  pl.BlockSpec((1, tn), lambda j: (0, j))],
        out_specs=pl.BlockSpec((rows, tn), lambda j: (0, j)),
        out_shape=jax.ShapeDtypeStruct((rows, n), F32),
        compiler_params=_cparams(("arbitrary",)),
        name="modulation",
    )(cvec, w_mod, b_mod)


def _staggered(tiles):
    for _ in range(tiles[0].lead):
        next(tiles[0].gen)
    live = [t.gen for t in tiles]
    while live:
        live = [g for g in live if next(g, None) is not None]


class _Stages:
    def __init__(self, gen, lead):
        self.gen, self.lead = gen, lead


def _ffn_stages(get_x, put_y, mods, nw_ref, w1_ref, w2_ref, a_ref, x_scr, h_scr, rs, d_ff, tf):
    shift, scale, gate = mods
    for p0 in range(rs.start, rs.stop, PIECE):
        pr = slice(p0, min(p0 + PIECE, rs.stop))
        x = get_x(pr)
        x_scr[pr, :] = x
        h_scr[pr, :] = (_rms(x) * nw_ref[...] * (1.0 + scale) + shift).astype(BF16)
        yield True
    h = h_scr[rs, :]
    for c in range(d_ff // tf):
        g = _dot(h, w1_ref[:, c * tf:(c + 1) * tf].astype(BF16))
        u = _dot(h, w1_ref[:, d_ff + c * tf:d_ff + (c + 1) * tf].astype(BF16))
        a_ref[rs, c * tf:(c + 1) * tf] = (_silu(g) * u).astype(BF16)
        yield True
    y = x_scr[rs, :] + 0.5 * gate * _dot(a_ref[rs, :], w2_ref[...].astype(BF16))
    yield True
    put_y(y)
    yield True


def _sub_tiles(rows):
    sub = min(ROW_SUB, rows)
    return [slice(r0, r0 + sub) for r0 in range(0, rows, sub)]


def _ffn_lead(rs, d_ff, tf):
    return -(-(rs.stop - rs.start) // PIECE) + d_ff // tf // 2


def _ffn_kernel(x_ref, mod_ref, nw_ref, w1_ref, w2_ref, o_ref, a_ref, x_scr, h_scr, *, d_ff, tf):
    mods = [mod_ref[0, k:k + 1, :] for k in range(3)]

    def stages(rs):
        def put(y):
            o_ref[0, rs, :] = y
        return _Stages(_ffn_stages(lambda pr: x_ref[0, pr, :], put, mods, nw_ref, w1_ref, w2_ref, a_ref,
                                   x_scr, h_scr, rs, d_ff, tf), _ffn_lead(rs, d_ff, tf))

    _staggered([stages(rs) for rs in _sub_tiles(x_ref.shape[1])])


def _ffn(s, mods, mod_row, nw, w1, w2):
    bsz, t, d = s.shape
    d_ff = w2.shape[0]
    tm = min(TOKEN_TILE, t)
    const = lambda b, i: (0, 0)
    return pl.pallas_call(
        functools.partial(_ffn_kernel, d_ff=d_ff, tf=FF_TILE),
        grid=(bsz, t // tm),
        in_specs=[pl.BlockSpec((1, tm, d), lambda b, i: (b, i, 0)),
                  pl.BlockSpec((1, 3, d), lambda b, i: (mod_row(b), 0, 0)),
                  pl.BlockSpec((1, d), const),
                  pl.BlockSpec((d, 2 * d_ff), const, pipeline_mode=pl.Buffered(1)),
                  pl.BlockSpec((d_ff, d), const, pipeline_mode=pl.Buffered(1))],
        out_specs=pl.BlockSpec((1, tm, d), lambda b, i: (b, i, 0)),
        out_shape=jax.ShapeDtypeStruct((bsz, t, d), F32),
        scratch_shapes=[pltpu.VMEM((tm, d_ff), BF16),
                        pltpu.VMEM((tm, d), F32),
                        pltpu.VMEM((tm, d), BF16)],
        compiler_params=_cparams(("parallel", "parallel")),
        name="ffn",
    )(s, mods, nw, w1, w2)


def _inproj_kernel(x_ref, mod_ref, nw_ref, w_ref, wg_ref, gb_ref, lbl_ref,
                   hq_ref, hv_ref, hg_ref, hlf_ref, mqk_ref, mv_ref, mo_ref, gc_ref, *, hw):
    shift = mod_ref[0, 0:1, :]
    scale = mod_ref[0, 1:2, :]
    lbl = lbl_ref[...]
    e = jnp.exp(lbl - jnp.max(lbl, axis=0, keepdims=True))
    lb = e[0] / jnp.sum(e, axis=0)
    first_half = lax.broadcasted_iota(jnp.int32, (SUBLANE, LANE), 1) < CHUNK
    def stages(rs):
        h = (_rms(x_ref[0, rs, :]) * nw_ref[...] * (1.0 + scale) + shift).astype(BF16)
        yield True

        def proj(k):
            return _dot_nt(h, w_ref[k * hw:(k + 1) * hw, :].astype(BF16))

        hq_ref[0, rs, :] = _silu(proj(0)) * (HG_DIM ** -0.5)
        yield True
        hv_ref[0, rs, :] = proj(1).astype(BF16)
        yield True
        hg_ref[0, rs, :] = proj(2).astype(BF16)
        yield True
        for d in range(2):
            lbd = lb[d:d + 1, :]
            hlf_ref[0, d, rs, :] = jnp.log(lbd + (1.0 - lbd) * _sigmoid(proj(3 + d))) * LOG2E
            yield True
        mqk_ref[0, rs, :] = proj(5).astype(BF16)
        yield True
        mv_ref[0, rs, :] = proj(6).astype(BF16)
        yield True
        mo_ref[0, rs, :] = proj(7).astype(BF16)
        gt = _dot_nt(wg_ref[...], h) + gb_ref[...]
        for m in range((rs.stop - rs.start) // LANE):
            top = gt[0:8, m * LANE:(m + 1) * LANE]
            bot = gt[8:16, m * LANE:(m + 1) * LANE]
            c0 = rs.start // CHUNK + 2 * m
            gc_ref[0, c0] = jnp.where(first_half, top, pltpu.roll(bot, CHUNK, 1))
            gc_ref[0, c0 + 1] = jnp.where(first_half, pltpu.roll(top, CHUNK, 1), bot)
        yield True

    _staggered([_Stages(stages(rs), 8) for rs in _sub_tiles(x_ref.shape[1])])


def _inproj(s, mods, mod_row, nw, w_in, w_gate_t, gate_b, lb_logits):
    bsz, t, d = s.shape
    hw = HG_HEADS * HG_DIM
    ng = gate_b.shape[0]
    tm = min(TOKEN_TILE, t)
    tok4 = lambda b, i: (b, i, 0, 0)
    const = lambda b, i: (0, 0)
    tok = lambda b, i: (b, i, 0)
    tok2 = lambda b, i: (b, 0, i, 0)
    f = lambda dt, *shape: jax.ShapeDtypeStruct(shape, dt)
    return pl.pallas_call(
        functools.partial(_inproj_kernel, hw=hw),
        grid=(bsz, t // tm),
        in_specs=[pl.BlockSpec((1, tm, d), tok),
                  pl.BlockSpec((1, 2, d), lambda b, i: (mod_row(b), 0, 0)),
                  pl.BlockSpec((1, d), const),
                  pl.BlockSpec(w_in.shape, const, pipeline_mode=pl.Buffered(1)),
                  pl.BlockSpec(w_gate_t.shape, const),
                  pl.BlockSpec((ng, 1), const),
                  pl.BlockSpec(lb_logits.shape, lambda b, i: (0, 0, 0))],
        out_specs=[pl.BlockSpec((1, tm, hw), tok), pl.BlockSpec((1, tm, hw), tok),
                   pl.BlockSpec((1, tm, hw), tok),
                   pl.BlockSpec((1, 2, tm, hw), tok2),
                   pl.BlockSpec((1, tm, hw), tok), pl.BlockSpec((1, tm, hw), tok),
                   pl.BlockSpec((1, tm, hw), tok), pl.BlockSpec((1, tm // CHUNK, ng // 2, LANE), tok4)],
        out_shape=[f(F32, bsz, t, hw), f(BF16, bsz, t, hw), f(BF16, bsz, t, hw),
                   f(F32, bsz, 2, t, hw),
                   f(BF16, bsz, t, hw), f(BF16, bsz, t, hw), f(BF16, bsz, t, hw),
                   f(F32, bsz, t // CHUNK, ng // 2, LANE)],
        compiler_params=_cparams(("parallel", "parallel")),
        name="inproj",
    )(s, mods, nw, w_in, w_gate_t, gate_b, lb_logits)


def _conv_kernel(x_ref, w_ref, b_ref, o_ref, *, width, q_tiles):
    x = x_ref[0].astype(F32)
    t = x.shape[0]
    col = lax.broadcasted_iota(jnp.int32, x.shape, 0) % width
    xl = jnp.where(col == 0, 0.0, pltpu.roll(x, 1, 0))
    xr = jnp.where(col == width - 1, 0.0, pltpu.roll(x, t - 1, 0))

    def row(di):
        return (w_ref[di, 0:1, :] * xl + w_ref[di, 1:2, :] * x + w_ref[di, 2:3, :] * xr)

    y = row(1)
    if t > width:
        pad = jnp.zeros((width, x.shape[1]), F32)
        y = y + jnp.concatenate([pad, row(0)[:t - width]], axis=0)
        y = y + jnp.concatenate([row(2)[width:], pad], axis=0)
    y = _silu(y + b_ref[...])
    qscale = jnp.where(pl.program_id(1) < q_tiles, ML_QK ** -0.5, 1.0)
    o_ref[0] = (y * qscale).astype(BF16)


def _conv(x, w, b, width):
    bsz, t, ch = x.shape
    return pl.pallas_call(
        functools.partial(_conv_kernel, width=width, q_tiles=ML_HEADS * ML_QK // LANE),
        grid=(bsz, ch // LANE),
        in_specs=[pl.BlockSpec((1, t, LANE), lambda b_, c: (b_, 0, c)),
                  pl.BlockSpec((3, 3, LANE), lambda b_, c: (0, 0, c)),
                  pl.BlockSpec((1, LANE), lambda b_, c: (0, c))],
        out_specs=pl.BlockSpec((1, t, LANE), lambda b_, c: (b_, 0, c)),
        out_shape=jax.ShapeDtypeStruct((bsz, t, ch), BF16),
        compiler_params=_cparams(("parallel", "parallel")),
        name="conv",
    )(x, w, b)


def _positions():
    t = np.arange(CHUNK)
    return [t, CHUNK - 1 - t]


def _hgrn2_masks():
    ds, vs, ws = [], [], []
    for p in _positions():
        le = (p[None, :] <= p[:, None]).astype(np.float32)
        order = [p[SUB * j] // SUB for j in range(N_SUB)]
        ref = np.zeros((32, CHUNK), np.float32)
        valid = np.zeros((N_SUB * N_SUB, 1), np.float32)
        for j in range(N_SUB):
            ref[j] = p <= SUB * order[j] + SUB // 2 - 1
            ref[8 + j] = p <= SUB * order[j] + SUB - 1
        ref[N_SUB] = 1.0
        for j in range(N_SUB):
            for i in range(N_SUB):
                if order[i] < order[j]:
                    ref[16 + N_SUB * j + i] = ref[j] - ref[8 + i]
                    valid[N_SUB * j + i] = 1.0
        ds.append(np.concatenate([le, ref], axis=0))
        vs.append(np.broadcast_to(valid, (N_SUB * N_SUB, HG_HEADS * HG_DIM)))
        tok_blk = np.arange(CHUNK)[:, None] // SUB
        ws.append(np.concatenate([(tok_blk == j) * le for j in range(N_SUB)], axis=1))
    return (np.stack(ds).astype(np.float32), np.stack(vs).astype(np.float32),
            np.stack(ws).astype(np.float32))


def _mlstm_masks():
    le2s, let2s = [], []
    for p in _positions():
        le = (p[None, :] <= p[:, None]).astype(np.float32)
        le2s.append(np.concatenate([le, le], axis=1))
        let2 = np.zeros((2 * CHUNK, 2 * CHUNK), np.float32)
        let2[:CHUNK, :CHUNK] = le.T
        let2[CHUNK:, CHUNK:] = le.T
        let2s.append(let2)
    return np.stack(le2s), np.stack(let2s)


def _pipelined(n_groups, group, stages_of, step_of):
    queue = []
    for g in range(n_groups):
        for _ in stages_of(g):
            if queue:
                queue.pop(0)()
            yield True
        while queue:
            queue.pop(0)()
            yield True
        queue = [functools.partial(step_of, g * group + i) for i in range(group)]
    for step in queue:
        step()
        yield True


def _interleave(programs):
    live = list(programs)
    while live:
        live = [p for p in live if next(p, None) is not None]


def _chunk_rows(c, nc, direction):
    cc = jnp.where(direction == 0, c, nc - 1 - c)
    return pl.multiple_of(cc * CHUNK, CHUNK)


def _hgrn2_program(q_ref, v_ref, lf_ref, d_ref, pv_ref, w_ref, s0_ref, o_ref, st_ref,
                   u_ref, qin_ref, g_ref, oi_ref, bx_ref, kx_ref, vx_ref, *, nc):
    direction = pl.program_id(1)

    @pl.when(pl.program_id(2) == 0)
    def _():
        st_ref[...] = s0_ref[...]

    dmat = d_ref[0].astype(BF16)
    pair_valid = pv_ref[0]
    wmask = w_ref[0] > 0.0
    L = CHUNK
    hw = q_ref.shape[-1]
    upper_rows = lax.broadcasted_iota(jnp.int32, (L, 2 * L), 0) < L // 2

    group = min(GROUP, nc)

    def intra(grp):
        cs = [grp * group + i for i in range(group)]
        rows = [pl.ds(_chunk_rows(c, nc, direction), L) for c in cs]
        items = [(i, h) for i in range(group) for h in range(HG_HEADS)]
        lanes = [slice(h * HG_DIM, (h + 1) * HG_DIM) for h in range(HG_HEADS)]
        lfs = [lf_ref[0, 0, r, :] for r in rows]
        parts = [_split2(lf) for lf in lfs]
        e_all = sum(_dot(dmat, jnp.concatenate([parts[i][n] for i in range(group)], axis=1))
                    for n in range(2))
        yield
        q_mids, k_mids, k_outs, vbs, spans = [], [], [], [], []
        for i in range(group):
            e = e_all[:, i * hw:(i + 1) * hw]
            q = q_ref[0, rows[i], :]
            k = 1.0 - jnp.exp2(lfs[i])
            b = e[0:L]
            b_tot = e[L + N_SUB:L + N_SUB + 1]
            sub_rows = lambda r0: jnp.concatenate(
                [jnp.broadcast_to(e[r0 + j:r0 + j + 1], (SUB, hw)) for j in range(N_SUB)], axis=0)
            mid_rows = sub_rows(L)
            end_rows = sub_rows(L + 8)
            to_mid = b - mid_rows
            spans.append(jnp.max(jnp.abs(to_mid)))
            q_mids.append((q * jnp.exp2(jnp.minimum(to_mid, EXP2_CLAMP))).astype(BF16))
            k_diag = k * jnp.exp2(jnp.minimum(-to_mid, EXP2_CLAMP))
            k_end = k * jnp.exp2(end_rows - b)
            cross = jnp.exp2(e[L + 16:L + 32]) * pair_valid
            slabs = []
            for j in range(N_SUB):
                for ib in range(N_SUB):
                    rs = slice(ib * SUB, (ib + 1) * SUB)
                    row = N_SUB * j + ib
                    slabs.append(k_diag[rs] if ib == j else k_end[rs] * cross[row:row + 1])
            k_mids.append(jnp.concatenate(slabs, axis=0).astype(BF16))
            k_outs.append((k * jnp.exp2(b_tot - b)).astype(BF16))
            vbs.append(v_ref[0, rows[i], :])
            qin_ref[cs[i]] = (q * jnp.exp2(b)).astype(BF16)
            g_ref[cs[i]] = jnp.broadcast_to(jnp.exp2(b_tot), (SUBLANE, hw))
            yield
        scores = [_dot_nt(q_mids[i][:, lanes[h]], k_mids[i][:, lanes[h]]) for i, h in items]
        yield
        a_s = [jnp.where(wmask, s, 0.0) for s in scores]
        a_s = [jnp.where(upper_rows, a[:, :2 * L], a[:, 2 * L:]).astype(BF16) for a in a_s]
        v2s = [jnp.concatenate([vb, vb], axis=0) for vb in vbs]
        outs = [_dot(a_s[n], v2s[i][:, lanes[h]]) for n, (i, h) in enumerate(items)]
        yield
        for i in range(group):
            oi_ref[cs[i]] = jnp.concatenate(outs[i * HG_HEADS:(i + 1) * HG_HEADS], axis=1)
        yield
        for i, h in items:
            u_ref[cs[i], h] = _dot_tn(vbs[i][:, lanes[h]], k_outs[i][:, lanes[h]])

        @pl.when(functools.reduce(jnp.maximum, spans) > EXP2_CLAMP)
        def _():
            for i in range(group):
                exact_intra(cs[i], rows[i])
        yield

    def exact_intra(c, rows):
        lf = lf_ref[0, 0, rows, :]
        bx_ref[...] = sum(_dot(dmat[0:L], x) for x in _split3(lf))
        kx_ref[...] = 1.0 - jnp.exp2(lf)
        vx_ref[...] = v_ref[0, rows, :].astype(F32)
        q = q_ref[0, rows, :]
        tok = lax.broadcasted_iota(jnp.int32, (L, 1), 0)
        pos = jnp.where(direction == 0, tok, L - 1 - tok)

        def add_source(s, acc):
            b_s = bx_ref[pl.ds(s, 1), :]
            z = q * kx_ref[pl.ds(s, 1), :] * jnp.exp2(jnp.minimum(bx_ref[...] - b_s, 0.0))
            v_s = vx_ref[pl.ds(s, 1), :]
            seen = pos >= jnp.where(direction == 0, s, L - 1 - s)
            cols = [jnp.where(seen, jnp.sum(z[:, ln], axis=1, keepdims=True), 0.0) * v_s[:, ln]
                    for ln in (slice(h * HG_DIM, (h + 1) * HG_DIM) for h in range(HG_HEADS))]
            return acc + jnp.concatenate(cols, axis=1)

        oi_ref[c] = lax.fori_loop(0, L, add_source, jnp.zeros((L, hw), F32))

    def carry_state(c):
        rows = pl.ds(_chunk_rows(c, nc, direction), L)
        q_in = qin_ref[c]
        g = g_ref[c][0:1]
        outs, states = [], []
        for h in range(HG_HEADS):
            ln = slice(h * HG_DIM, (h + 1) * HG_DIM)
            st = st_ref[0, 0, h]
            outs.append(_dot_nt(q_in[:, ln], st.astype(BF16)))
            states.append(g[:, ln] * st + u_ref[c, h])
        o_ref[0, 0, rows, :] = (oi_ref[c] + jnp.concatenate(outs, axis=1)).astype(BF16)
        for h in range(HG_HEADS):
            st_ref[0, 0, h] = states[h]

    return _pipelined(nc // group, group, intra, carry_state)


N_HG = (7, 2, 7)
N_ML = (7, 3, 5)


def _mlstm_program(qk_ref, v_ref, g_ref, le2_ref, let2_ref, c0_ref, m0_ref,
                   o_ref, c_ref, mm_ref,
                   al_ref, u_ref, ct_ref, mloc_ref, sc_ref, *, nc):
    direction = pl.program_id(1)

    @pl.when(pl.program_id(2) == 0)
    def _():
        c_ref[...] = c0_ref[...]
        mm_ref[...] = m0_ref[...]

    let2 = let2_ref[0].astype(BF16)
    causal = le2_ref[0] > 0.0
    L = CHUNK
    neg = -jnp.inf
    lane = lax.broadcasted_iota(jnp.int32, (1, LANE), 1)
    seg = [lane < L, lane >= L]
    lane_t = lax.broadcasted_iota(jnp.int32, (L, LANE), 1) < L
    npair = ML_HEADS // 2
    ones_v = jnp.ones((L, ML_V), BF16)
    zb = jnp.zeros((L, LANE), BF16)

    def by_head(x):
        return jnp.concatenate([jnp.where(lane_t, x, zb), jnp.where(lane_t, zb, x)], axis=0)

    def v_stack(vb, p):
        return jnp.concatenate(
            [jnp.concatenate([vb[:, 2 * p * ML_V:(2 * p + 1) * ML_V], ones_v], axis=1),
             jnp.concatenate([vb[:, (2 * p + 1) * ML_V:(2 * p + 2) * ML_V], ones_v], axis=1)], axis=0)

    group = min(ML_GROUP, nc)

    def intra(grp):
        cs = [grp * group + i for i in range(group)]
        ccs = [jnp.where(direction == 0, c, nc - 1 - c) for c in cs]
        rows = [pl.ds(pl.multiple_of(cc * L, L), L) for cc in ccs]
        pairs = [(i, p) for i in range(group) for p in range(npair)]
        gs = [jnp.where(direction == 0, g_ref[0, cc][0:4], g_ref[0, cc][4:8]) for cc in ccs]
        pad = [jnp.zeros((-2 * group % SUBLANE, LANE), F32)] if 2 * group % SUBLANE else []
        i2 = jnp.concatenate([g[0:2] for g in gs] + pad, axis=0)
        lf2 = _log_sigmoid(jnp.concatenate([g[2:4] for g in gs] + pad, axis=0))
        parts = _split3(lf2)
        b2 = sum(_dot(x, let2) for x in parts)
        r2 = i2 - b2
        g_a = jnp.max(jnp.where(seg[0], r2, neg), axis=1, keepdims=True)
        g_b = jnp.max(jnp.where(seg[1], r2, neg), axis=1, keepdims=True)
        e_w = jnp.exp(r2 - jnp.where(seg[0], g_a, g_b))
        bt_a = jnp.sum(jnp.where(seg[0], lf2, 0.0), axis=1, keepdims=True)
        bt_b = jnp.sum(jnp.where(seg[1], lf2, 0.0), axis=1, keepdims=True)
        yield

        def as_columns(row):
            cols = jnp.broadcast_to(row, (LANE, LANE)).T
            return jnp.where(lane_t, cols[0:L], cols[L:2 * L])

        b_ts = [as_columns(b2[2 * i + p:2 * i + p + 1]) for i, p in pairs]
        for i in range(group):
            sc_ref[cs[i]] = jnp.concatenate(
                [jnp.broadcast_to(x[2 * i + p:2 * i + p + 1], (1, LANE))
                 for x2 in ((bt_a, bt_b), (g_a, g_b)) for p in range(npair) for x in x2], axis=0)
        yield
        qks = [qk_ref[0, r, :] for r in rows]
        vbs = [v_ref[0, r, :] for r in rows]
        q_ps = [qks[i][:, p * LANE:(p + 1) * LANE] for i, p in pairs]
        k_ps = [qks[i][:, (npair + p) * LANE:(npair + p + 1) * LANE] for i, p in pairs]
        kbs = [by_head(k) for k in k_ps]
        scores = [_dot_nt(q, kb) for q, kb in zip(q_ps, kbs)]
        yield
        for n, (i, p) in enumerate(pairs):
            t1 = b_ts[n]
            d_log = jnp.where(causal, t1 + r2[2 * i + p:2 * i + p + 1], neg)
            m_a = jnp.max(jnp.where(lane_t, d_log, neg), axis=1, keepdims=True)
            m_b = jnp.max(jnp.where(lane_t, neg, d_log), axis=1, keepdims=True)
            m_loc = jnp.where(lane_t, m_a, m_b)
            ct_ref[cs[i], p] = t1
            mloc_ref[cs[i], p] = m_loc
            al_ref[cs[i], p] = scores[n] * jnp.exp(d_log - m_loc)
        yield
        v_stacks = [v_stack(vbs[i], p) for i, p in pairs]
        yield
        kts = [(kb.astype(F32).T * e_w[2 * i + p:2 * i + p + 1]).astype(BF16) for kb, (i, p) in zip(kbs, pairs)]
        for n, (i, p) in enumerate(pairs):
            u_ref[cs[i], p] = _dot(kts[n], v_stacks[n])
        yield

    def carry_state(c):
        cc = jnp.where(direction == 0, c, nc - 1 - c)
        rows = pl.ds(pl.multiple_of(cc * L, L), L)
        m_all = mm_ref[0, 0]
        sc = sc_ref[c]
        qk = qk_ref[0, rows, :]
        vb = v_ref[0, rows, :]
        outs, states, m_rows = [], [], []
        for p in range(ML_HEADS // 2):
            c_pair = c_ref[0, 0, p]
            inc = u_ref[c, p]
            b_t = ct_ref[c, p]
            m_loc = mloc_ref[c, p]
            m_in2 = jnp.where(seg[0], m_all[2 * p:2 * p + 1], m_all[2 * p + 1:2 * p + 2])
            m_t = jnp.maximum(m_loc, b_t + m_in2)
            lhs_a = (jnp.exp(m_loc - m_t) * al_ref[c, p]).astype(BF16)
            lhs_q = (jnp.exp(b_t + m_in2 - m_t) * qk[:, p * LANE:(p + 1) * LANE].astype(F32)).astype(BF16)
            lhs = jnp.concatenate([by_head(lhs_a), by_head(lhs_q)], axis=1)
            tot = _dot(lhs, jnp.concatenate([v_stack(vb, p), c_pair.astype(BF16)], axis=0))
            floor = jnp.exp(-m_t)
            floor_r = pltpu.roll(floor, L, 1)
            new_rows = []
            for hh in range(2):
                h = 2 * p + hh
                rs = slice(hh * L, (hh + 1) * L)
                m_in = m_all[h:h + 1]
                floor_h = jnp.where(lane_t, floor, floor_r) if hh == 0 else jnp.where(lane_t, floor_r, floor)
                outs.append(tot[rs, :ML_V] / jnp.maximum(jnp.abs(tot[rs, ML_V:]), floor_h))
                b_tot = sc[h:h + 1]
                m_new = b_tot + jnp.maximum(m_in, sc[ML_HEADS + h:ML_HEADS + h + 1])
                carry_w = jnp.exp(b_tot + m_in - m_new)
                inc_w = jnp.exp(b_tot + sc[ML_HEADS + h:ML_HEADS + h + 1] - m_new)
                cw2 = jnp.concatenate([carry_w, carry_w], axis=1)
                iw2 = jnp.concatenate([inc_w, inc_w], axis=1)
                new_rows.append(cw2 * c_pair[rs] + iw2 * inc[rs])
                m_rows.append(m_new)
            states.append(jnp.concatenate(new_rows, axis=0))
        o_ref[0, 0, rows, :] = jnp.concatenate(outs, axis=1).astype(BF16)
        for p in range(ML_HEADS // 2):
            c_ref[0, 0, p] = states[p]
        mm_ref[0, 0] = jnp.concatenate(m_rows + [m_all[ML_HEADS:]], axis=0)

    return _pipelined(nc // group, group, intra, carry_state)


def _scan_kernel(*refs, nc):
    hg_in, ml_in, hg_out, ml_out, hg_scr, ml_scr = (
        refs[a:b] for a, b in zip(np.cumsum([0, N_HG[0], N_ML[0], N_HG[1], N_ML[1], N_HG[2]]),
                                  np.cumsum([N_HG[0], N_ML[0], N_HG[1], N_ML[1], N_HG[2], N_ML[2]])))
    _interleave([_hgrn2_program(*hg_in, *hg_out, *hg_scr, nc=nc),
                 _mlstm_program(*ml_in, *ml_out, *ml_scr, nc=nc)])


def _scans(hq, hv, hlf, qk, mv, gates, s0, c0, m0, dmat, pair_valid, wmask, le2, let2):
    bsz, t, hw = hq.shape
    nc = min(SCAN_CHUNKS, t // CHUNK)
    tb = nc * CHUNK
    nblk = t // tb
    npair = ML_HEADS // 2

    def blk(j, d):
        return jnp.where(d == 0, j, nblk - 1 - j)

    tok = lambda b, d, j: (b, blk(j, d), 0)
    tok_d = lambda b, d, j: (b, d, blk(j, d), 0)
    per_dir = lambda a: pl.BlockSpec((1,) + a.shape[1:], lambda b, d, j: (d,) + (0,) * (a.ndim - 1))
    state = lambda a: pl.BlockSpec((1, 1) + a.shape[2:], lambda b, d, j: (b, d) + (0,) * (a.ndim - 2))
    out_tok = pl.BlockSpec((1, 1, tb, hw), tok_d)
    hg_in = [pl.BlockSpec((1, tb, hw), tok), pl.BlockSpec((1, tb, hw), tok), pl.BlockSpec((1, 1, tb, hw), tok_d),
             per_dir(dmat), per_dir(pair_valid), per_dir(wmask), state(s0)]
    ml_in = [pl.BlockSpec((1, tb, qk.shape[-1]), tok), pl.BlockSpec((1, tb, hw), tok),
             pl.BlockSpec((1, nc) + gates.shape[2:], lambda b, d, j: (b, blk(j, d), 0, 0)),
             per_dir(le2), per_dir(let2), state(c0), state(m0)]
    sds = jax.ShapeDtypeStruct
    hg_scr = [pltpu.VMEM((nc,) + s0.shape[2:], F32),
              pltpu.VMEM((nc, CHUNK, hw), BF16),
              pltpu.VMEM((nc, SUBLANE, hw), F32),
              pltpu.VMEM((nc, CHUNK, hw), F32),
              pltpu.VMEM((CHUNK, hw), F32),
              pltpu.VMEM((CHUNK, hw), F32),
              pltpu.VMEM((CHUNK, hw), F32)]
    ml_scr = [pltpu.VMEM((nc, npair, CHUNK, LANE), F32),
              pltpu.VMEM((nc, npair, 2 * ML_QK, 2 * ML_V), F32),
              pltpu.VMEM((nc, npair, CHUNK, LANE), F32),
              pltpu.VMEM((nc, npair, CHUNK, LANE), F32),
              pltpu.VMEM((nc, SUBLANE, LANE), F32)]
    assert (len(hg_in), 2, len(hg_scr)) == N_HG and (len(ml_in), 3, len(ml_scr)) == N_ML
    return pl.pallas_call(
        functools.partial(_scan_kernel, nc=nc),
        grid=(bsz, 2, nblk),
        in_specs=hg_in + ml_in,
        out_specs=[out_tok, state(s0), out_tok, state(c0), state(m0)],
        out_shape=[sds((bsz, 2, t, hw), BF16), sds(s0.shape, F32),
                   sds((bsz, 2, t, hw), BF16), sds(c0.shape, F32), sds(m0.shape, F32)],
        scratch_shapes=hg_scr + ml_scr,
        compiler_params=_cparams(("parallel", "parallel", "arbitrary")),
        name="scans",
    )(hq, hv, hlf, dmat, pair_valid, wmask, s0, qk, mv, gates, le2, let2, c0, m0)


def _head_rms(x, heads):
    dh = x.shape[-1] // heads
    return jnp.concatenate([_rms(x[:, h * dh:(h + 1) * dh]) for h in range(heads)], axis=1)


def _mix_ffn_kernel(oh_ref, om_ref, hg_ref, mo_ref, x_ref, mod_ref, hnw_ref, mnw_ref, wo_ref,
                    nw_ref, w1_ref, w2_ref, fw_ref, o_ref, a_ref, x_scr, h_scr, *, d_ff, tf):
    mix_gate = mod_ref[0, 0:1, :]
    mods = [mod_ref[0, k:k + 1, :] for k in range(1, 4)]

    def stages(rs):
        def mixed(pr):
            oh = oh_ref[0, 0, pr, :].astype(F32) + oh_ref[0, 1, pr, :].astype(F32)
            om = om_ref[0, 0, pr, :].astype(F32) + om_ref[0, 1, pr, :].astype(F32)
            hg_out = _head_rms(oh, HG_HEADS) * hnw_ref[...] * _silu(hg_ref[0, pr, :].astype(F32))
            ml_out = _sigmoid(mo_ref[0, pr, :].astype(F32)) * (_head_rms(om, ML_HEADS) * mnw_ref[...])
            merged = jnp.concatenate([hg_out, ml_out], axis=1).astype(BF16)
            return x_ref[0, pr, :] + mix_gate * _dot(merged, wo_ref[...])

        def put(y):
            o_ref[0, rs, :] = _rms(y) * fw_ref[...]

        return _Stages(_ffn_stages(mixed, put, mods, nw_ref, w1_ref, w2_ref, a_ref, x_scr, h_scr,
                                   rs, d_ff, tf), _ffn_lead(rs, d_ff, tf))

    _staggered([stages(rs) for rs in _sub_tiles(x_ref.shape[1])])


def _mix_ffn(oh, om, hg, mo, x, mods, hnw, mnw, w_out, nw, w1, w2, fw):
    bsz, t, d = x.shape
    hw = hg.shape[-1]
    d_ff = w2.shape[0]
    tm = min(TOKEN_TILE, t)
    tok = lambda b, i: (b, i, 0)
    tok2 = lambda b, i: (b, 0, i, 0)
    const = lambda b, i: (0, 0)
    resident = lambda a: pl.BlockSpec(a.shape, const, pipeline_mode=pl.Buffered(1))
    return pl.pallas_call(
        functools.partial(_mix_ffn_kernel, d_ff=d_ff, tf=FF_TILE),
        grid=(bsz, t // tm),
        in_specs=[pl.BlockSpec((1, 2, tm, hw), tok2), pl.BlockSpec((1, 2, tm, hw), tok2),
                  pl.BlockSpec((1, tm, hw), tok), pl.BlockSpec((1, tm, hw), tok),
                  pl.BlockSpec((1, tm, d), tok),
                  pl.BlockSpec((1, 4, d), lambda b, i: (b, 0, 0)),
                  pl.BlockSpec((1, hw), const), pl.BlockSpec((1, hw), const), resident(w_out),
                  pl.BlockSpec((1, d), const), resident(w1), resident(w2), pl.BlockSpec((1, d), const)],
        out_specs=pl.BlockSpec((1, tm, d), tok),
        out_shape=jax.ShapeDtypeStruct((bsz, t, d), F32),
        scratch_shapes=[pltpu.VMEM((tm, d_ff), BF16),
                        pltpu.VMEM((tm, d), F32),
                        pltpu.VMEM((tm, d), BF16)],
        compiler_params=_cparams(("parallel", "parallel")),
        name="mix_ffn_final",
    )(oh, om, hg, mo, x, mods, hnw, mnw, w_out, nw, w1, w2, fw)


def kernel(x, c, ctx, c_ctx, w_mod, b_mod, norm1_w, ffn1_w1, ffn1_w2, norm2_w, w_in, ml_gate_b,
           ml_conv_w, ml_conv_b, hg_lb_logits, hg_norm_w, ml_norm_w, w_out, norm3_w, ffn2_w1, ffn2_w2,
           final_norm_w):
    bsz, seq, d = x.shape
    n_ctx = ctx.shape[1]
    assert w_mod.shape[0] == 1, "single-layer kernel"
    assert seq % max(TOKEN_TILE, SCAN_CHUNKS * CHUNK) == 0 and n_ctx % CHUNK == 0 and GRID_W == CHUNK
    hw = HG_HEADS * HG_DIM
    ng = ml_gate_b.shape[-1]

    rows = -(-(bsz + 1) // SUBLANE) * SUBLANE
    cvec = jnp.zeros((rows, d), F32).at[:bsz].set(c).at[bsz].set(c_ctx)
    mods = _modulation(cvec, w_mod[0], b_mod[0][None, :]).reshape(rows, N_MOD, d)
    lat_row = lambda b: b
    ctx_row = lambda b: bsz

    row = lambda a: a.reshape(1, -1)
    w1a, w2a = ffn1_w1[0], ffn1_w2[0]
    x1 = _ffn(x, mods[:, 0:3], lat_row, row(norm1_w[0]), w1a, w2a)
    s1 = _ffn(ctx, mods[:, 0:3], ctx_row, row(norm1_w[0]), w1a, w2a)

    w_in_b = w_in[0].T
    gate_order = np.arange(ng).reshape(2, 2, ML_HEADS // 2, 2).transpose(3, 0, 1, 2).reshape(-1)
    w_gate_t = w_in[0][:, 8 * hw:].T[gate_order].astype(BF16)
    gate_b = ml_gate_b[0][gate_order].reshape(ng, 1)
    dmat, pair_valid, wmask = (jnp.asarray(a) for a in _hgrn2_masks())
    le2, let2 = (jnp.asarray(a) for a in _mlstm_masks())
    conv_w, conv_b = ml_conv_w[0], row(ml_conv_b[0])
    npair = ML_HEADS // 2

    def mixer_scans(s, mod_row, width, s0, c0, m0):
        hq, hv, hg, hlf, mqk, mv, mo, gates = _inproj(
            s, mods[:, 3:5], mod_row, row(norm2_w[0]), w_in_b, w_gate_t, gate_b, hg_lb_logits)
        qk = _conv(mqk, conv_w, conv_b, width)
        oh, s_fin, om, c_fin, m_fin = _scans(hq, hv, hlf, qk, mv, gates, s0, c0, m0,
                                             dmat, pair_valid, wmask, le2, let2)
        return oh, om, hg, mo, s_fin, c_fin, m_fin

    s0 = jnp.zeros((bsz, 2, HG_HEADS, HG_DIM, HG_DIM), F32)
    c0 = jnp.zeros((bsz, 2, npair, 2 * ML_QK, 2 * ML_V), F32)
    m0 = jnp.zeros((bsz, 2, SUBLANE, LANE), F32)
    _, _, _, _, s_ctx, c_ctx_state, m_ctx = mixer_scans(s1, ctx_row, n_ctx, s0, c0, m0)
    oh, om, hg, mo, _, _, _ = mixer_scans(x1, lat_row, GRID_W, s_ctx, c_ctx_state, m_ctx)

    return _mix_ffn(oh, om, hg, mo, x1, mods[:, 5:9], row(hg_norm_w[0]), row(ml_norm_w[0]),
                    w_out[0].astype(BF16), row(norm3_w[0]), ffn2_w1[0], ffn2_w2[0], row(final_norm_w))
```

```python
import functools

import numpy as np
import jax
import jax.numpy as jnp
from jax import lax
from jax.experimental import pallas as pl
from jax.experimental.pallas import tpu as pltpu

F32 = jnp.float32
BF16 = jnp.bfloat16

EPS = 1e-6
CHUNK = 64
GRID_W = 64
N_MOD = 9
HG_HEADS = 4
HG_DIM = 128
ML_HEADS = 4
ML_QK = 64
ML_V = 128
SUB = 16
N_SUB = CHUNK // SUB
GROUP = 4
ML_GROUP = 4
SCAN_CHUNKS = 8
TOKEN_TILE = 512
ROW_SUB = 256
FF_TILE = 256
PIECE = 128
LOG2E = 1.4426950408889634
EXP2_CLAMP = 115.0
LANE = 128
SUBLANE = 8
VMEM_LIMIT = 56 * 1024 * 1024


def _cparams(sem):
    return pltpu.CompilerParams(dimension_semantics=sem, vmem_limit_bytes=VMEM_LIMIT)


def _dot(a, b):
    return jnp.dot(a, b, preferred_element_type=F32)


def _dot_nt(a, b):
    return lax.dot_general(a, b, (((1,), (1,)), ((), ())), preferred_element_type=F32)


def _dot_tn(a, b):
    return lax.dot_general(a, b, (((0,), (0,)), ((), ())), preferred_element_type=F32)


def _sigmoid(x):
    return 1.0 / (1.0 + jnp.exp(-x))


def _silu(x):
    return x * _sigmoid(x)


def _log_sigmoid(x):
    return jnp.minimum(x, 0.0) - jnp.log(1.0 + jnp.exp(-jnp.abs(x)))


def _split3(x):
    hi = x.astype(BF16)
    r = x - hi.astype(F32)
    mid = r.astype(BF16)
    lo = (r - mid.astype(F32)).astype(BF16)
    return hi, mid, lo


def _split2(x):
    hi = x.astype(BF16)
    return hi, (x - hi.astype(F32)).astype(BF16)


def _rms(x):
    return x * lax.rsqrt(jnp.mean(x * x, axis=-1, keepdims=True) + EPS)


def _mod_kernel(c_ref, w_ref, b_ref, o_ref):
    a = _silu(c_ref[...])
    a_hi = a.astype(BF16)
    a_lo = (a - a_hi.astype(F32)).astype(BF16)
    w = w_ref[...]
    w_hi = w.astype(BF16)
    w_lo = (w - w_hi.astype(F32)).astype(BF16)
    o_ref[...] = _dot(a_hi, w_hi) + _dot(a_hi, w_lo) + _dot(a_lo, w_hi) + b_ref[...]


def _modulation(cvec, w_mod, b_mod):
    rows, d = cvec.shape
    n = w_mod.shape[1]
    tn = 1024
    return pl.pallas_call(
        _mod_kernel,
        grid=(n // tn,),
        in_specs=[pl.BlockSpec((rows, d), lambda j: (0, 0)),
                  pl.BlockSpec((d, tn), lambda j: (0, j)),
                  pl.BlockSpec((1, tn), lambda j: (0, j))],
        out_specs=pl.BlockSpec((rows, tn), lambda j: (0, j)),
        out_shape=jax.ShapeDtypeStruct((rows, n), F32),
        compiler_params=_cparams(("arbitrary",)),
        name="modulation",
    )(cvec, w_mod, b_mod)


def _staggered(tiles):
    for _ in range(tiles[0].lead):
        next(tiles[0].gen)
    live = [t.gen for t in tiles]
    while live:
        live = [g for g in live if next(g, None) is not None]


class _Stages:
    def __init__(self, gen, lead):
        self.gen, self.lead = gen, lead


def _ffn_stages(get_x, put_y, mods, nw_ref, w1_ref, w2_ref, a_ref, x_scr, h_scr, rs, d_ff, tf):
    shift, scale, gate = mods
    for p0 in range(rs.start, rs.stop, PIECE):
        pr = slice(p0, min(p0 + PIECE, rs.stop))
        x = get_x(pr)
        x_scr[pr, :] = x
        h_scr[pr, :] = (_rms(x) * nw_ref[...] * (1.0 + scale) + shift).astype(BF16)
        yield True
    h = h_scr[rs, :]
    for c in range(d_ff // tf):
        g = _dot(h, w1_ref[:, c * tf:(c + 1) * tf].astype(BF16))
        u = _dot(h, w1_ref[:, d_ff + c * tf:d_ff + (c + 1) * tf].astype(BF16))
        a_ref[rs, c * tf:(c + 1) * tf] = (_silu(g) * u).astype(BF16)
        yield True
    y = x_scr[rs, :] + 0.5 * gate * _dot(a_ref[rs, :], w2_ref[...].astype(BF16))
    yield True
    put_y(y)
    yield True


def _sub_tiles(rows):
    sub = min(ROW_SUB, rows)
    return [slice(r0, r0 + sub) for r0 in range(0, rows, sub)]


def _ffn_lead(rs, d_ff, tf):
    return -(-(rs.stop - rs.start) // PIECE) + d_ff // tf // 2


def _ffn_kernel(x_ref, mod_ref, nw_ref, w1_ref, w2_ref, o_ref, a_ref, x_scr, h_scr, *, d_ff, tf):
    mods = [mod_ref[0, k:k + 1, :] for k in range(3)]

    def stages(rs):
        def put(y):
            o_ref[0, rs, :] = y
        return _Stages(_ffn_stages(lambda pr: x_ref[0, pr, :], put, mods, nw_ref, w1_ref, w2_ref, a_ref,
                                   x_scr, h_scr, rs, d_ff, tf), _ffn_lead(rs, d_ff, tf))

    _staggered([stages(rs) for rs in _sub_tiles(x_ref.shape[1])])


def _ffn(s, mods, mod_row, nw, w1, w2):
    bsz, t, d = s.shape
    d_ff = w2.shape[0]
    tm = min(TOKEN_TILE, t)
    const = lambda b, i: (0, 0)
    return pl.pallas_call(
        functools.partial(_ffn_kernel, d_ff=d_ff, tf=FF_TILE),
        grid=(bsz, t // tm),
        in_specs=[pl.BlockSpec((1, tm, d), lambda b, i: (b, i, 0)),
                  pl.BlockSpec((1, 3, d), lambda b, i: (mod_row(b), 0, 0)),
                  pl.BlockSpec((1, d), const),
                  pl.BlockSpec((d, 2 * d_ff), const, pipeline_mode=pl.Buffered(1)),
                  pl.BlockSpec((d_ff, d), const, pipeline_mode=pl.Buffered(1))],
        out_specs=pl.BlockSpec((1, tm, d), lambda b, i: (b, i, 0)),
        out_shape=jax.ShapeDtypeStruct((bsz, t, d), F32),
        scratch_shapes=[pltpu.VMEM((tm, d_ff), BF16),
                        pltpu.VMEM((tm, d), F32),
                        pltpu.VMEM((tm, d), BF16)],
        compiler_params=_cparams(("parallel", "parallel")),
        name="ffn",
    )(s, mods, nw, w1, w2)


def _inproj_kernel(x_ref, mod_ref, nw_ref, w_ref, wg_ref, gb_ref, lbl_ref,
                   hq_ref, hv_ref, hg_ref, hlf_ref, mqk_ref, mv_ref, mo_ref, gc_ref, *, hw):
    shift = mod_ref[0, 0:1, :]
    scale = mod_ref[0, 1:2, :]
    lbl = lbl_ref[...]
    e = jnp.exp(lbl - jnp.max(lbl, axis=0, keepdims=True))
    lb = e[0] / jnp.sum(e, axis=0)
    first_half = lax.broadcasted_iota(jnp.int32, (SUBLANE, LANE), 1) < CHUNK
    def stages(rs):
        h = (_rms(x_ref[0, rs, :]) * nw_ref[...] * (1.0 + scale) + shift).astype(BF16)
        yield True

        def proj(k):
            return _dot_nt(h, w_ref[k * hw:(k + 1) * hw, :].astype(BF16))

        hq_ref[0, rs, :] = _silu(proj(0)) * (HG_DIM ** -0.5)
        yield True
        hv_ref[0, rs, :] = proj(1).astype(BF16)
        yield True
        hg_ref[0, rs, :] = proj(2).astype(BF16)
        yield True
        for d in range(2):
            lbd = lb[d:d + 1, :]
            hlf_ref[d, 0, rs, :] = jnp.log(lbd + (1.0 - lbd) * _sigmoid(proj(3 + d))) * LOG2E
            yield True
        mqk_ref[0, rs, :] = proj(5).astype(BF16)
        yield True
        mv_ref[0, rs, :] = proj(6).astype(BF16)
        yield True
        mo_ref[0, rs, :] = proj(7).astype(BF16)
        gt = _dot_nt(wg_ref[...], h) + gb_ref[...]
        for m in range((rs.stop - rs.start) // LANE):
            top = gt[0:8, m * LANE:(m + 1) * LANE]
            bot = gt[8:16, m * LANE:(m + 1) * LANE]
            c0 = rs.start // CHUNK + 2 * m
            gc_ref[0, c0] = jnp.where(first_half, top, pltpu.roll(bot, CHUNK, 1))
            gc_ref[0, c0 + 1] = jnp.where(first_half, pltpu.roll(top, CHUNK, 1), bot)
        yield True

    _staggered([_Stages(stages(rs), 8) for rs in _sub_tiles(x_ref.shape[1])])


def _inproj(s, mods, mod_row, nw, w_in, w_gate_t, gate_b, lb_logits):
    bsz, t, d = s.shape
    hw = HG_HEADS * HG_DIM
    ng = gate_b.shape[0]
    tm = min(TOKEN_TILE, t)
    tok4 = lambda b, i: (b, i, 0, 0)
    const = lambda b, i: (0, 0)
    tok = lambda b, i: (b, i, 0)
    tok2 = lambda b, i: (0, b, i, 0)
    f = lambda dt, *shape: jax.ShapeDtypeStruct(shape, dt)
    return pl.pallas_call(
        functools.partial(_inproj_kernel, hw=hw),
        grid=(bsz, t // tm),
        in_specs=[pl.BlockSpec((1, tm, d), tok),
                  pl.BlockSpec((1, 2, d), lambda b, i: (mod_row(b), 0, 0)),
                  pl.BlockSpec((1, d), const),
                  pl.BlockSpec(w_in.shape, const, pipeline_mode=pl.Buffered(1)),
                  pl.BlockSpec(w_gate_t.shape, const),
                  pl.BlockSpec((ng, 1), const),
                  pl.BlockSpec(lb_logits.shape, lambda b, i: (0, 0, 0))],
        out_specs=[pl.BlockSpec((1, tm, hw), tok), pl.BlockSpec((1, tm, hw), tok),
                   pl.BlockSpec((1, tm, hw), tok),
                   pl.BlockSpec((2, 1, tm, hw), tok2),
                   pl.BlockSpec((1, tm, hw), tok), pl.BlockSpec((1, tm, hw), tok),
                   pl.BlockSpec((1, tm, hw), tok), pl.BlockSpec((1, tm // CHUNK, ng // 2, LANE), tok4)],
        out_shape=[f(F32, bsz, t, hw), f(BF16, bsz, t, hw), f(BF16, bsz, t, hw),
                   f(F32, 2, bsz, t, hw),
                   f(BF16, bsz, t, hw), f(BF16, bsz, t, hw), f(BF16, bsz, t, hw),
                   f(F32, bsz, t // CHUNK, ng // 2, LANE)],
        compiler_params=_cparams(("parallel", "parallel")),
        name="inproj",
    )(s, mods, nw, w_in, w_gate_t, gate_b, lb_logits)


def _conv_kernel(x_ref, w_ref, b_ref, o_ref, *, width, q_tiles):
    x = x_ref[0].astype(F32)
    t = x.shape[0]
    col = lax.broadcasted_iota(jnp.int32, x.shape, 0) % width
    xl = jnp.where(col == 0, 0.0, pltpu.roll(x, 1, 0))
    xr = jnp.where(col == width - 1, 0.0, pltpu.roll(x, t - 1, 0))

    def row(di):
        return (w_ref[di, 0:1, :] * xl + w_ref[di, 1:2, :] * x + w_ref[di, 2:3, :] * xr)

    y = row(1)
    if t > width:
        pad = jnp.zeros((width, x.shape[1]), F32)
        y = y + jnp.concatenate([pad, row(0)[:t - width]], axis=0)
        y = y + jnp.concatenate([row(2)[width:], pad], axis=0)
    y = _silu(y + b_ref[...])
    qscale = jnp.where(pl.program_id(1) < q_tiles, ML_QK ** -0.5, 1.0)
    o_ref[0] = (y * qscale).astype(BF16)


def _conv(x, w, b, width):
    bsz, t, ch = x.shape
    return pl.pallas_call(
        functools.partial(_conv_kernel, width=width, q_tiles=ML_HEADS * ML_QK // LANE),
        grid=(bsz, ch // LANE),
        in_specs=[pl.BlockSpec((1, t, LANE), lambda b_, c: (b_, 0, c)),
                  pl.BlockSpec((3, 3, LANE), lambda b_, c: (0, 0, c)),
                  pl.BlockSpec((1, LANE), lambda b_, c: (0, c))],
        out_specs=pl.BlockSpec((1, t, LANE), lambda b_, c: (b_, 0, c)),
        out_shape=jax.ShapeDtypeStruct((bsz, t, ch), BF16),
        compiler_params=_cparams(("parallel", "parallel")),
        name="conv",
    )(x, w, b)


def _positions():
    t = np.arange(CHUNK)
    return [t, CHUNK - 1 - t]


def _hgrn2_masks():
    ds, vs, ws = [], [], []
    for p in _positions():
        le = (p[None, :] <= p[:, None]).astype(np.float32)
        order = [p[SUB * j] // SUB for j in range(N_SUB)]
        ref = np.zeros((32, CHUNK), np.float32)
        valid = np.zeros((N_SUB * N_SUB, 1), np.float32)
        for j in range(N_SUB):
            ref[j] = p <= SUB * order[j] + SUB // 2 - 1
            ref[8 + j] = p <= SUB * order[j] + SUB - 1
        ref[N_SUB] = 1.0
        for j in range(N_SUB):
            for i in range(N_SUB):
                if order[i] < order[j]:
                    ref[16 + N_SUB * j + i] = ref[j] - ref[8 + i]
                    valid[N_SUB * j + i] = 1.0
        ds.append(np.concatenate([le, ref], axis=0))
        vs.append(np.broadcast_to(valid, (N_SUB * N_SUB, HG_HEADS * HG_DIM)))
        tok_blk = np.arange(CHUNK)[:, None] // SUB
        ws.append(np.concatenate([(tok_blk == j) * le for j in range(N_SUB)], axis=1))
    return (np.stack(ds).astype(np.float32), np.stack(vs).astype(np.float32),
            np.stack(ws).astype(np.float32))


def _mlstm_masks():
    le2s, let2s = [], []
    for p in _positions():
        le = (p[None, :] <= p[:, None]).astype(np.float32)
        le2s.append(np.concatenate([le, le], axis=1))
        let2 = np.zeros((2 * CHUNK, 2 * CHUNK), np.float32)
        let2[:CHUNK, :CHUNK] = le.T
        let2[CHUNK:, CHUNK:] = le.T
        let2s.append(let2)
    return np.stack(le2s), np.stack(let2s)


def _pipelined(n_groups, group, stages_of, step_of):
    queue = []
    for g in range(n_groups):
        for _ in stages_of(g):
            if queue:
                queue.pop(0)()
            yield True
        while queue:
            queue.pop(0)()
            yield True
        queue = [functools.partial(step_of, g * group + i) for i in range(group)]
    for step in queue:
        step()
        yield True


def _interleave(programs):
    live = list(programs)
    while live:
        live = [p for p in live if next(p, None) is not None]


def _chunk_rows(c, nc, direction):
    cc = jnp.where(direction == 0, c, nc - 1 - c)
    return pl.multiple_of(cc * CHUNK, CHUNK)


def _hgrn2_program(q_ref, v_ref, lf_ref, d_ref, pv_ref, w_ref, s0_ref, o_ref, st_ref,
                   u_ref, qin_ref, g_ref, oi_ref, bx_ref, kx_ref, vx_ref, *, nc):
    direction = pl.program_id(1)

    @pl.when(pl.program_id(2) == 0)
    def _():
        st_ref[...] = s0_ref[...]

    dmat = d_ref[0].astype(BF16)
    pair_valid = pv_ref[0]
    wmask = w_ref[0] > 0.0
    L = CHUNK
    hw = q_ref.shape[-1]
    upper_rows = lax.broadcasted_iota(jnp.int32, (L, 2 * L), 0) < L // 2

    group = min(GROUP, nc)

    def intra(grp):
        cs = [grp * group + i for i in range(group)]
        rows = [pl.ds(_chunk_rows(c, nc, direction), L) for c in cs]
        items = [(i, h) for i in range(group) for h in range(HG_HEADS)]
        lanes = [slice(h * HG_DIM, (h + 1) * HG_DIM) for h in range(HG_HEADS)]
        lfs = [lf_ref[0, 0, r, :] for r in rows]
        parts = [_split2(lf) for lf in lfs]
        e_all = sum(_dot(dmat, jnp.concatenate([parts[i][n] for i in range(group)], axis=1))
                    for n in range(2))
        yield
        q_mids, k_mids, k_outs, vbs, spans = [], [], [], [], []
        for i in range(group):
            e = e_all[:, i * hw:(i + 1) * hw]
            q = q_ref[0, rows[i], :]
            k = 1.0 - jnp.exp2(lfs[i])
            b = e[0:L]
            b_tot = e[L + N_SUB:L + N_SUB + 1]
            sub_rows = lambda r0: jnp.concatenate(
                [jnp.broadcast_to(e[r0 + j:r0 + j + 1], (SUB, hw)) for j in range(N_SUB)], axis=0)
            mid_rows = sub_rows(L)
            end_rows = sub_rows(L + 8)
            to_mid = b - mid_rows
            spans.append(jnp.max(jnp.abs(to_mid)))
            q_mids.append((q * jnp.exp2(jnp.minimum(to_mid, EXP2_CLAMP))).astype(BF16))
            k_diag = k * jnp.exp2(jnp.minimum(-to_mid, EXP2_CLAMP))
            k_end = k * jnp.exp2(end_rows - b)
            cross = jnp.exp2(e[L + 16:L + 32]) * pair_valid
            slabs = []
            for j in range(N_SUB):
                for ib in range(N_SUB):
                    rs = slice(ib * SUB, (ib + 1) * SUB)
                    row = N_SUB * j + ib
                    slabs.append(k_diag[rs] if ib == j else k_end[rs] * cross[row:row + 1])
            k_mids.append(jnp.concatenate(slabs, axis=0).astype(BF16))
            k_outs.append((k * jnp.exp2(b_tot - b)).astype(BF16))
            vbs.append(v_ref[0, rows[i], :])
            qin_ref[cs[i]] = (q * jnp.exp2(b)).astype(BF16)
            g_ref[cs[i]] = jnp.broadcast_to(jnp.exp2(b_tot), (SUBLANE, hw))
            yield
        scores = [_dot_nt(q_mids[i][:, lanes[h]], k_mids[i][:, lanes[h]]) for i, h in items]
        yield
        a_s = [jnp.where(wmask, s, 0.0) for s in scores]
        a_s = [jnp.where(upper_rows, a[:, :2 * L], a[:, 2 * L:]).astype(BF16) for a in a_s]
        v2s = [jnp.concatenate([vb, vb], axis=0) for vb in vbs]
        outs = [_dot(a_s[n], v2s[i][:, lanes[h]]) for n, (i, h) in enumerate(items)]
        yield
        for i in range(group):
            oi_ref[cs[i]] = jnp.concatenate(outs[i * HG_HEADS:(i + 1) * HG_HEADS], axis=1)
        yield
        for i, h in items:
            u_ref[cs[i], h] = _dot_tn(vbs[i][:, lanes[h]], k_outs[i][:, lanes[h]])

        @pl.when(functools.reduce(jnp.maximum, spans) > EXP2_CLAMP)
        def _():
            for i in range(group):
                exact_intra(cs[i], rows[i])
        yield

    def exact_intra(c, rows):
        lf = lf_ref[0, 0, rows, :]
        bx_ref[...] = sum(_dot(dmat[0:L], x) for x in _split3(lf))
        kx_ref[...] = 1.0 - jnp.exp2(lf)
        vx_ref[...] = v_ref[0, rows, :].astype(F32)
        q = q_ref[0, rows, :]
        tok = lax.broadcasted_iota(jnp.int32, (L, 1), 0)
        pos = jnp.where(direction == 0, tok, L - 1 - tok)

        def add_source(s, acc):
            b_s = bx_ref[pl.ds(s, 1), :]
            z = q * kx_ref[pl.ds(s, 1), :] * jnp.exp2(jnp.minimum(bx_ref[...] - b_s, 0.0))
            v_s = vx_ref[pl.ds(s, 1), :]
            seen = pos >= jnp.where(direction == 0, s, L - 1 - s)
            cols = [jnp.where(seen, jnp.sum(z[:, ln], axis=1, keepdims=True), 0.0) * v_s[:, ln]
                    for ln in (slice(h * HG_DIM, (h + 1) * HG_DIM) for h in range(HG_HEADS))]
            return acc + jnp.concatenate(cols, axis=1)

        oi_ref[c] = lax.fori_loop(0, L, add_source, jnp.zeros((L, hw), F32))

    def carry_state(c):
        rows = pl.ds(_chunk_rows(c, nc, direction), L)
        q_in = qin_ref[c]
        g = g_ref[c][0:1]
        outs, states = [], []
        for h in range(HG_HEADS):
            ln = slice(h * HG_DIM, (h + 1) * HG_DIM)
            st = st_ref[0, 0, h]
            outs.append(_dot_nt(q_in[:, ln], st.astype(BF16)))
            states.append(g[:, ln] * st + u_ref[c, h])
        o_ref[0, 0, rows, :] = (oi_ref[c] + jnp.concatenate(outs, axis=1)).astype(BF16)
        for h in range(HG_HEADS):
            st_ref[0, 0, h] = states[h]

    return _pipelined(nc // group, group, intra, carry_state)


N_HG = (7, 2, 7)
N_ML = (7, 3, 5)


def _mlstm_program(qk_ref, v_ref, g_ref, le2_ref, let2_ref, c0_ref, m0_ref,
                   o_ref, c_ref, mm_ref,
                   al_ref, u_ref, ct_ref, mloc_ref, sc_ref, *, nc):
    direction = pl.program_id(1)

    @pl.when(pl.program_id(2) == 0)
    def _():
        c_ref[...] = c0_ref[...]
        mm_ref[...] = m0_ref[...]

    let2 = let2_ref[0].astype(BF16)
    causal = le2_ref[0] > 0.0
    L = CHUNK
    neg = -jnp.inf
    lane = lax.broadcasted_iota(jnp.int32, (1, LANE), 1)
    seg = [lane < L, lane >= L]
    lane_t = lax.broadcasted_iota(jnp.int32, (L, LANE), 1) < L
    npair = ML_HEADS // 2
    ones_v = jnp.ones((L, ML_V), BF16)
    zb = jnp.zeros((L, LANE), BF16)

    def by_head(x):
        return jnp.concatenate([jnp.where(lane_t, x, zb), jnp.where(lane_t, zb, x)], axis=0)

    def v_stack(vb, p):
        return jnp.concatenate(
            [jnp.concatenate([vb[:, 2 * p * ML_V:(2 * p + 1) * ML_V], ones_v], axis=1),
             jnp.concatenate([vb[:, (2 * p + 1) * ML_V:(2 * p + 2) * ML_V], ones_v], axis=1)], axis=0)

    group = min(ML_GROUP, nc)

    def intra(grp):
        cs = [grp * group + i for i in range(group)]
        ccs = [jnp.where(direction == 0, c, nc - 1 - c) for c in cs]
        rows = [pl.ds(pl.multiple_of(cc * L, L), L) for cc in ccs]
        pairs = [(i, p) for i in range(group) for p in range(npair)]
        gs = [jnp.where(direction == 0, g_ref[0, cc][0:4], g_ref[0, cc][4:8]) for cc in ccs]
        pad = [jnp.zeros((-2 * group % SUBLANE, LANE), F32)] if 2 * group % SUBLANE else []
        i2 = jnp.concatenate([g[0:2] for g in gs] + pad, axis=0)
        lf2 = _log_sigmoid(jnp.concatenate([g[2:4] for g in gs] + pad, axis=0))
        parts = _split3(lf2)
        b2 = sum(_dot(x, let2) for x in parts)
        r2 = i2 - b2
        g_a = jnp.max(jnp.where(seg[0], r2, neg), axis=1, keepdims=True)
        g_b = jnp.max(jnp.where(seg[1], r2, neg), axis=1, keepdims=True)
        e_w = jnp.exp(r2 - jnp.where(seg[0], g_a, g_b))
        bt_a = jnp.sum(jnp.where(seg[0], lf2, 0.0), axis=1, keepdims=True)
        bt_b = jnp.sum(jnp.where(seg[1], lf2, 0.0), axis=1, keepdims=True)
        yield

        def as_columns(row):
            cols = jnp.broadcast_to(row, (LANE, LANE)).T
            return jnp.where(lane_t, cols[0:L], cols[L:2 * L])

        b_ts = [as_columns(b2[2 * i + p:2 * i + p + 1]) for i, p in pairs]
        for i in range(group):
            sc_ref[cs[i]] = jnp.concatenate(
                [jnp.broadcast_to(x[2 * i + p:2 * i + p + 1], (1, LANE))
                 for x2 in ((bt_a, bt_b), (g_a, g_b)) for p in range(npair) for x in x2], axis=0)
        yield
        qks = [qk_ref[0, r, :] for r in rows]
        vbs = [v_ref[0, r, :] for r in rows]
        q_ps = [qks[i][:, p * LANE:(p + 1) * LANE] for i, p in pairs]
        k_ps = [qks[i][:, (npair + p) * LANE:(npair + p + 1) * LANE] for i, p in pairs]
        kbs = [by_head(k) for k in k_ps]
        scores = [_dot_nt(q, kb) for q, kb in zip(q_ps, kbs)]
        yield
        for n, (i, p) in enumerate(pairs):
            t1 = b_ts[n]
            d_log = jnp.where(causal, t1 + r2[2 * i + p:2 * i + p + 1], neg)
            m_a = jnp.max(jnp.where(lane_t, d_log, neg), axis=1, keepdims=True)
            m_b = jnp.max(jnp.where(lane_t, neg, d_log), axis=1, keepdims=True)
            m_loc = jnp.where(lane_t, m_a, m_b)
            ct_ref[cs[i], p] = t1
            mloc_ref[cs[i], p] = m_loc
            al_ref[cs[i], p] = scores[n] * jnp.exp(d_log - m_loc)
        yield
        v_stacks = [v_stack(vbs[i], p) for i, p in pairs]
        yield
        kts = [(kb.astype(F32).T * e_w[2 * i + p:2 * i + p + 1]).astype(BF16) for kb, (i, p) in zip(kbs, pairs)]
        for n, (i, p) in enumerate(pairs):
            u_ref[cs[i], p] = _dot(kts[n], v_stacks[n])
        yield

    def carry_state(c):
        cc = jnp.where(direction == 0, c, nc - 1 - c)
        rows = pl.ds(pl.multiple_of(cc * L, L), L)
        m_all = mm_ref[0, 0]
        sc = sc_ref[c]
        qk = qk_ref[0, rows, :]
        vb = v_ref[0, rows, :]
        outs, states, m_rows = [], [], []
        for p in range(ML_HEADS // 2):
            c_pair = c_ref[0, 0, p]
            inc = u_ref[c, p]
            b_t = ct_ref[c, p]
            m_loc = mloc_ref[c, p]
            m_in2 = jnp.where(seg[0], m_all[2 * p:2 * p + 1], m_all[2 * p + 1:2 * p + 2])
            m_t = jnp.maximum(m_loc, b_t + m_in2)
            lhs_a = (jnp.exp(m_loc - m_t) * al_ref[c, p]).astype(BF16)
            lhs_q = (jnp.exp(b_t + m_in2 - m_t) * qk[:, p * LANE:(p + 1) * LANE].astype(F32)).astype(BF16)
            lhs = jnp.concatenate([by_head(lhs_a), by_head(lhs_q)], axis=1)
            tot = _dot(lhs, jnp.concatenate([v_stack(vb, p), c_pair.astype(BF16)], axis=0))
            floor = jnp.exp(-m_t)
            floor_r = pltpu.roll(floor, L, 1)
            new_rows = []
            for hh in range(2):
                h = 2 * p + hh
                rs = slice(hh * L, (hh + 1) * L)
                m_in = m_all[h:h + 1]
                floor_h = jnp.where(lane_t, floor, floor_r) if hh == 0 else jnp.where(lane_t, floor_r, floor)
                outs.append(tot[rs, :ML_V] / jnp.maximum(jnp.abs(tot[rs, ML_V:]), floor_h))
                b_tot = sc[h:h + 1]
                m_new = b_tot + jnp.maximum(m_in, sc[ML_HEADS + h:ML_HEADS + h + 1])
                carry_w = jnp.exp(b_tot + m_in - m_new)
                inc_w = jnp.exp(b_tot + sc[ML_HEADS + h:ML_HEADS + h + 1] - m_new)
                cw2 = jnp.concatenate([carry_w, carry_w], axis=1)
                iw2 = jnp.concatenate([inc_w, inc_w], axis=1)
                new_rows.append(cw2 * c_pair[rs] + iw2 * inc[rs])
                m_rows.append(m_new)
            states.append(jnp.concatenate(new_rows, axis=0))
        o_ref[0, 0, rows, :] = jnp.concatenate(outs, axis=1).astype(BF16)
        for p in range(ML_HEADS // 2):
            c_ref[0, 0, p] = states[p]
        mm_ref[0, 0] = jnp.concatenate(m_rows + [m_all[ML_HEADS:]], axis=0)

    return _pipelined(nc // group, group, intra, carry_state)


def _scan_kernel(*refs, nc):
    hg_in, ml_in, hg_out, ml_out, hg_scr, ml_scr = (
        refs[a:b] for a, b in zip(np.cumsum([0, N_HG[0], N_ML[0], N_HG[1], N_ML[1], N_HG[2]]),
                                  np.cumsum([N_HG[0], N_ML[0], N_HG[1], N_ML[1], N_HG[2], N_ML[2]])))
    _interleave([_hgrn2_program(*hg_in, *hg_out, *hg_scr, nc=nc),
                 _mlstm_program(*ml_in, *ml_out, *ml_scr, nc=nc)])


def _scans(hq, hv, hlf, qk, mv, gates, s0, c0, m0, dmat, pair_valid, wmask, le2, let2):
    bsz, t, hw = hq.shape
    nc = min(SCAN_CHUNKS, t // CHUNK)
    tb = nc * CHUNK
    nblk = t // tb
    npair = ML_HEADS // 2

    def blk(j, d):
        return jnp.where(d == 0, j, nblk - 1 - j)

    tok = lambda b, d, j: (b, blk(j, d), 0)
    tok_d = lambda b, d, j: (b, d, blk(j, d), 0)
    per_dir = lambda a: pl.BlockSpec((1,) + a.shape[1:], lambda b, d, j: (d,) + (0,) * (a.ndim - 1))
    state = lambda a: pl.BlockSpec((1, 1) + a.shape[2:], lambda b, d, j: (b, d) + (0,) * (a.ndim - 2))
    out_tok = pl.BlockSpec((1, 1, tb, hw), tok_d)
    hg_in = [pl.BlockSpec((1, tb, hw), tok), pl.BlockSpec((1, tb, hw), tok),
             pl.BlockSpec((1, 1, tb, hw), lambda b, d, j: (d, b, blk(j, d), 0)),
             per_dir(dmat), per_dir(pair_valid), per_dir(wmask), state(s0)]
    ml_in = [pl.BlockSpec((1, tb, qk.shape[-1]), tok), pl.BlockSpec((1, tb, hw), tok),
             pl.BlockSpec((1, nc) + gates.shape[2:], lambda b, d, j: (b, blk(j, d), 0, 0)),
             per_dir(le2), per_dir(let2), state(c0), state(m0)]
    sds = jax.ShapeDtypeStruct
    hg_scr = [pltpu.VMEM((nc,) + s0.shape[2:], F32),
              pltpu.VMEM((nc, CHUNK, hw), BF16),
              pltpu.VMEM((nc, SUBLANE, hw), F32),
              pltpu.VMEM((nc, CHUNK, hw), F32),
              pltpu.VMEM((CHUNK, hw), F32),
              pltpu.VMEM((CHUNK, hw), F32),
              pltpu.VMEM((CHUNK, hw), F32)]
    ml_scr = [pltpu.VMEM((nc, npair, CHUNK, LANE), F32),
              pltpu.VMEM((nc, npair, 2 * ML_QK, 2 * ML_V), F32),
              pltpu.VMEM((nc, npair, CHUNK, LANE), F32),
              pltpu.VMEM((nc, npair, CHUNK, LANE), F32),
              pltpu.VMEM((nc, SUBLANE, LANE), F32)]
    assert (len(hg_in), 2, len(hg_scr)) == N_HG and (len(ml_in), 3, len(ml_scr)) == N_ML
    return pl.pallas_call(
        functools.partial(_scan_kernel, nc=nc),
        grid=(bsz, 2, nblk),
        in_specs=hg_in + ml_in,
        out_specs=[out_tok, state(s0), out_tok, state(c0), state(m0)],
        out_shape=[sds((bsz, 2, t, hw), BF16), sds(s0.shape, F32),
                   sds((bsz, 2, t, hw), BF16), sds(c0.shape, F32), sds(m0.shape, F32)],
        scratch_shapes=hg_scr + ml_scr,
        compiler_params=_cparams(("parallel", "parallel", "arbitrary")),
        name="scans",
    )(hq, hv, hlf, dmat, pair_valid, wmask, s0, qk, mv, gates, le2, let2, c0, m0)


def _head_rms(x, heads):
    dh = x.shape[-1] // heads
    return jnp.concatenate([_rms(x[:, h * dh:(h + 1) * dh]) for h in range(heads)], axis=1)


def _mix_ffn_kernel(oh_ref, om_ref, hg_ref, mo_ref, x_ref, mod_ref, hnw_ref, mnw_ref, wo_ref,
                    nw_ref, w1_ref, w2_ref, fw_ref, o_ref, a_ref, x_scr, h_scr, *, d_ff, tf):
    mix_gate = mod_ref[0, 0:1, :]
    mods = [mod_ref[0, k:k + 1, :] for k in range(1, 4)]

    def stages(rs):
        def mixed(pr):
            oh = oh_ref[0, 0, pr, :].astype(F32) + oh_ref[0, 1, pr, :].astype(F32)
            om = om_ref[0, 0, pr, :].astype(F32) + om_ref[0, 1, pr, :].astype(F32)
            hg_out = _head_rms(oh, HG_HEADS) * hnw_ref[...] * _silu(hg_ref[0, pr, :].astype(F32))
            ml_out = _sigmoid(mo_ref[0, pr, :].astype(F32)) * (_head_rms(om, ML_HEADS) * mnw_ref[...])
            merged = jnp.concatenate([hg_out, ml_out], axis=1).astype(BF16)
            return x_ref[0, pr, :] + mix_gate * _dot(merged, wo_ref[...])

        def put(y):
            o_ref[0, rs, :] = _rms(y) * fw_ref[...]

        return _Stages(_ffn_stages(mixed, put, mods, nw_ref, w1_ref, w2_ref, a_ref, x_scr, h_scr,
                                   rs, d_ff, tf), _ffn_lead(rs, d_ff, tf))

    _staggered([stages(rs) for rs in _sub_tiles(x_ref.shape[1])])


def _mix_ffn(oh, om, hg, mo, x, mods, hnw, mnw, w_out, nw, w1, w2, fw):
    bsz, t, d = x.shape
    hw = hg.shape[-1]
    d_ff = w2.shape[0]
    tm = min(TOKEN_TILE, t)
    tok = lambda b, i: (b, i, 0)
    tok2 = lambda b, i: (b, 0, i, 0)
    const = lambda b, i: (0, 0)
    resident = lambda a: pl.BlockSpec(a.shape, const, pipeline_mode=pl.Buffered(1))
    return pl.pallas_call(
        functools.partial(_mix_ffn_kernel, d_ff=d_ff, tf=FF_TILE),
        grid=(bsz, t // tm),
        in_specs=[pl.BlockSpec((1, 2, tm, hw), tok2), pl.BlockSpec((1, 2, tm, hw), tok2),
                  pl.BlockSpec((1, tm, hw), tok), pl.BlockSpec((1, tm, hw), tok),
                  pl.BlockSpec((1, tm, d), tok),
                  pl.BlockSpec((1, 4, d), lambda b, i: (b, 0, 0)),
                  pl.BlockSpec((1, hw), const), pl.BlockSpec((1, hw), const), resident(w_out),
                  pl.BlockSpec((1, d), const), resident(w1), resident(w2), pl.BlockSpec((1, d), const)],
        out_specs=pl.BlockSpec((1, tm, d), tok),
        out_shape=jax.ShapeDtypeStruct((bsz, t, d), F32),
        scratch_shapes=[pltpu.VMEM((tm, d_ff), BF16),
                        pltpu.VMEM((tm, d), F32),
                        pltpu.VMEM((tm, d), BF16)],
        compiler_params=_cparams(("parallel", "parallel")),
        name="mix_ffn_final",
    )(oh, om, hg, mo, x, mods, hnw, mnw, w_out, nw, w1, w2, fw)


def kernel(x, c, ctx, c_ctx, w_mod, b_mod, norm1_w, ffn1_w1, ffn1_w2, norm2_w, w_in, ml_gate_b,
           ml_conv_w, ml_conv_b, hg_lb_logits, hg_norm_w, ml_norm_w, w_out, norm3_w, ffn2_w1, ffn2_w2,
           final_norm_w):
    bsz, seq, d = x.shape
    n_ctx = ctx.shape[1]
    assert w_mod.shape[0] == 1, "single-layer kernel"
    assert seq % max(TOKEN_TILE, SCAN_CHUNKS * CHUNK) == 0 and n_ctx % CHUNK == 0 and GRID_W == CHUNK
    hw = HG_HEADS * HG_DIM
    ng = ml_gate_b.shape[-1]

    rows = -(-(bsz + 1) // SUBLANE) * SUBLANE
    cvec = jnp.zeros((rows, d), F32).at[:bsz].set(c).at[bsz].set(c_ctx)
    mods = _modulation(cvec, w_mod[0], b_mod[0][None, :]).reshape(rows, N_MOD, d)
    lat_row = lambda b: b
    ctx_row = lambda b: bsz

    row = lambda a: a.reshape(1, -1)
    w1a, w2a = ffn1_w1[0], ffn1_w2[0]
    x1 = _ffn(x, mods[:, 0:3], lat_row, row(norm1_w[0]), w1a, w2a)
    s1 = _ffn(ctx.reshape(1, bsz * n_ctx, d), mods[:, 0:3], ctx_row, row(norm1_w[0]), w1a, w2a)
    s1 = s1.reshape(bsz, n_ctx, d)

    w_in_b = w_in[0].T
    gate_order = np.arange(ng).reshape(2, 2, ML_HEADS // 2, 2).transpose(3, 0, 1, 2).reshape(-1)
    w_gate_t = w_in[0][:, 8 * hw:].T[gate_order].astype(BF16)
    gate_b = ml_gate_b[0][gate_order].reshape(ng, 1)
    dmat, pair_valid, wmask = (jnp.asarray(a) for a in _hgrn2_masks())
    le2, let2 = (jnp.asarray(a) for a in _mlstm_masks())
    conv_w, conv_b = ml_conv_w[0], row(ml_conv_b[0])
    npair = ML_HEADS // 2

    def mixer_scans(s, mod_row, width, s0, c0, m0):
        t = s.shape[1]
        shared = mod_row is ctx_row
        s_in = s.reshape(1, bsz * t, d) if shared else s
        outs = _inproj(s_in, mods[:, 3:5], mod_row, row(norm2_w[0]), w_in_b, w_gate_t, gate_b, hg_lb_logits)
        hq, hv, hg, hlf, mqk, mv, mo, gates = outs
        if shared:
            hq, hv, hg, mqk, mv, mo = (a.reshape(bsz, t, hw) for a in (hq, hv, hg, mqk, mv, mo))
            hlf = hlf.reshape(2, bsz, t, hw)
            gates = gates.reshape(bsz, t // CHUNK, ng // 2, LANE)
        qk = _conv(mqk, conv_w, conv_b, width)
        oh, s_fin, om, c_fin, m_fin = _scans(hq, hv, hlf, qk, mv, gates, s0, c0, m0,
                                             dmat, pair_valid, wmask, le2, let2)
        return oh, om, hg, mo, s_fin, c_fin, m_fin

    s0 = jnp.zeros((bsz, 2, HG_HEADS, HG_DIM, HG_DIM), F32)
    c0 = jnp.zeros((bsz, 2, npair, 2 * ML_QK, 2 * ML_V), F32)
    m0 = jnp.zeros((bsz, 2, SUBLANE, LANE), F32)
    _, _, _, _, s_ctx, c_ctx_state, m_ctx = mixer_scans(s1, ctx_row, n_ctx, s0, c0, m0)
    oh, om, hg, mo, _, _, _ = mixer_scans(x1, lat_row, GRID_W, s_ctx, c_ctx_state, m_ctx)

    return _mix_ffn(oh, om, hg, mo, x1, mods[:, 5:9], row(hg_norm_w[0]), row(ml_norm_w[0]),
                    w_out[0].astype(BF16), row(norm3_w[0]), ffn2_w1[0], ffn2_w2[0], row(final_norm_w))
```

```python
import functools

import numpy as np
import jax
import jax.numpy as jnp
from jax import lax
from jax.experimental import pallas as pl
from jax.experimental.pallas import tpu as pltpu

F32 = jnp.float32
BF16 = jnp.bfloat16

EPS = 1e-6
CHUNK = 64
GRID_W = 64
N_MOD = 9
HG_HEADS = 4
HG_DIM = 128
ML_HEADS = 4
ML_QK = 64
ML_V = 128
SUB = 16
N_SUB = CHUNK // SUB
GROUP = 4
ML_GROUP = 4
SCAN_CHUNKS = 8
TOKEN_TILE = 512
ROW_SUB = 256
FF_TILE = 256
PIECE = 128
LOG2E = 1.4426950408889634
EXP2_CLAMP = 115.0
LANE = 128
SUBLANE = 8
VMEM_LIMIT = 56 * 1024 * 1024


def _cparams(sem):
    return pltpu.CompilerParams(dimension_semantics=sem, vmem_limit_bytes=VMEM_LIMIT)


def _dot(a, b):
    return jnp.dot(a, b, preferred_element_type=F32)


def _dot_nt(a, b):
    return lax.dot_general(a, b, (((1,), (1,)), ((), ())), preferred_element_type=F32)


def _dot_tn(a, b):
    return lax.dot_general(a, b, (((0,), (0,)), ((), ())), preferred_element_type=F32)


def _sigmoid(x):
    return 1.0 / (1.0 + jnp.exp(-x))


def _silu(x):
    return x * _sigmoid(x)


def _log_sigmoid(x):
    return jnp.minimum(x, 0.0) - jnp.log(1.0 + jnp.exp(-jnp.abs(x)))


def _split3(x):
    hi = x.astype(BF16)
    r = x - hi.astype(F32)
    mid = r.astype(BF16)
    lo = (r - mid.astype(F32)).astype(BF16)
    return hi, mid, lo


def _split2(x):
    hi = x.astype(BF16)
    return hi, (x - hi.astype(F32)).astype(BF16)


def _rms(x):
    return x * lax.rsqrt(jnp.mean(x * x, axis=-1, keepdims=True) + EPS)


def _mod_kernel(c_ref, w_ref, b_ref, o_ref):
    a = _silu(c_ref[...])
    a_hi = a.astype(BF16)
    a_lo = (a - a_hi.astype(F32)).astype(BF16)
    w = w_ref[...]
    w_hi = w.astype(BF16)
    w_lo = (w - w_hi.astype(F32)).astype(BF16)
    o_ref[...] = _dot(a_hi, w_hi) + _dot(a_hi, w_lo) + _dot(a_lo, w_hi) + b_ref[...]


def _modulation(cvec, w_mod, b_mod):
    rows, d = cvec.shape
    n = w_mod.shape[1]
    tn = 1024
    return pl.pallas_call(
        _mod_kernel,
        grid=(n // tn,),
        in_specs=[pl.BlockSpec((rows, d), lambda j: (0, 0)),
                  pl.BlockSpec((d, tn), lambda j: (0, j)),
                  pl.BlockSpec((1, tn), lambda j: (0, j))],
        out_specs=pl.BlockSpec((rows, tn), lambda j: (0, j)),
        out_shape=jax.ShapeDtypeStruct((rows, n), F32),
        compiler_params=_cparams(("arbitrary",)),
        name="modulation",
    )(cvec, w_mod, b_mod)


def _staggered(tiles):
    for _ in range(tiles[0].lead):
        next(tiles[0].gen)
    live = [t.gen for t in tiles]
    while live:
        live = [g for g in live if next(g, None) is not None]


class _Stages:
    def __init__(self, gen, lead):
        self.gen, self.lead = gen, lead


def _ffn_stages(get_x, put_y, mods, nw_ref, w1_ref, w2_ref, a_ref, x_scr, h_scr, rs, d_ff, tf):
    shift, scale, gate = mods
    for p0 in range(rs.start, rs.stop, PIECE):
        pr = slice(p0, min(p0 + PIECE, rs.stop))
        x = get_x(pr)
        x_scr[pr, :] = x
        h_scr[pr, :] = (_rms(x) * nw_ref[...] * (1.0 + scale) + shift).astype(BF16)
        yield True
    h = h_scr[rs, :]
    for c in range(d_ff // tf):
        g = _dot(h, w1_ref[:, c * tf:(c + 1) * tf].astype(BF16))
        u = _dot(h, w1_ref[:, d_ff + c * tf:d_ff + (c + 1) * tf].astype(BF16))
        a_ref[rs, c * tf:(c + 1) * tf] = (_silu(g) * u).astype(BF16)
        yield True
    y = x_scr[rs, :] + 0.5 * gate * _dot(a_ref[rs, :], w2_ref[...].astype(BF16))
    yield True
    put_y(y)
    yield True


def _sub_tiles(rows):
    sub = min(ROW_SUB, rows)
    return [slice(r0, r0 + sub) for r0 in range(0, rows, sub)]


def _ffn_lead(rs, d_ff, tf):
    return -(-(rs.stop - rs.start) // PIECE) + d_ff // tf // 2


def _ffn_kernel(x_ref, mod_ref, nw_ref, w1_ref, w2_ref, o_ref, a_ref, x_scr, h_scr, *, d_ff, tf):
    mods = [mod_ref[0, k:k + 1, :] for k in range(3)]

    def stages(rs):
        def put(y):
            o_ref[0, rs, :] = y
        return _Stages(_ffn_stages(lambda pr: x_ref[0, pr, :], put, mods, nw_ref, w1_ref, w2_ref, a_ref,
                                   x_scr, h_scr, rs, d_ff, tf), _ffn_lead(rs, d_ff, tf))

    _staggered([stages(rs) for rs in _sub_tiles(x_ref.shape[1])])


def _ffn(s, mods, mod_row, nw, w1, w2):
    bsz, t, d = s.shape
    d_ff = w2.shape[0]
    tm = min(TOKEN_TILE, t)
    const = lambda b, i: (0, 0)
    return pl.pallas_call(
        functools.partial(_ffn_kernel, d_ff=d_ff, tf=FF_TILE),
        grid=(bsz, t // tm),
        in_specs=[pl.BlockSpec((1, tm, d), lambda b, i: (b, i, 0)),
                  pl.BlockSpec((1, 3, d), lambda b, i: (mod_row(b), 0, 0)),
                  pl.BlockSpec((1, d), const),
                  pl.BlockSpec((d, 2 * d_ff), const, pipeline_mode=pl.Buffered(1)),
                  pl.BlockSpec((d_ff, d), const, pipeline_mode=pl.Buffered(1))],
        out_specs=pl.BlockSpec((1, tm, d), lambda b, i: (b, i, 0)),
        out_shape=jax.ShapeDtypeStruct((bsz, t, d), F32),
        scratch_shapes=[pltpu.VMEM((tm, d_ff), BF16),
                        pltpu.VMEM((tm, d), F32),
                        pltpu.VMEM((tm, d), BF16)],
        compiler_params=_cparams(("parallel", "parallel")),
        name="ffn",
    )(s, mods, nw, w1, w2)


def _inproj_kernel(x_ref, mod_ref, nw_ref, w_ref, wg_ref, gb_ref, lbl_ref,
                   hq_ref, hv_ref, hg_ref, hlf_ref, mqk_ref, mv_ref, mo_ref, gc_ref, *, hw):
    shift = mod_ref[0, 0:1, :]
    scale = mod_ref[0, 1:2, :]
    lbl = lbl_ref[...]
    e = jnp.exp(lbl - jnp.max(lbl, axis=0, keepdims=True))
    lb = e[0] / jnp.sum(e, axis=0)
    first_half = lax.broadcasted_iota(jnp.int32, (SUBLANE, LANE), 1) < CHUNK
    def stages(rs):
        h = (_rms(x_ref[0, rs, :]) * nw_ref[...] * (1.0 + scale) + shift).astype(BF16)
        yield True

        def proj(k):
            return _dot_nt(h, w_ref[k * hw:(k + 1) * hw, :].astype(BF16))

        hq_ref[0, rs, :] = _silu(proj(0)) * (HG_DIM ** -0.5)
        yield True
        hv_ref[0, rs, :] = proj(1).astype(BF16)
        yield True
        hg_ref[0, rs, :] = proj(2).astype(BF16)
        yield True
        for d in range(2):
            lbd = lb[d:d + 1, :]
            hlf_ref[0, d, rs, :] = jnp.log(lbd + (1.0 - lbd) * _sigmoid(proj(3 + d))) * LOG2E
            yield True
        mqk_ref[0, rs, :] = proj(5).astype(BF16)
        yield True
        mv_ref[0, rs, :] = proj(6).astype(BF16)
        yield True
        mo_ref[0, rs, :] = proj(7).astype(BF16)
        gt = _dot_nt(wg_ref[...], h) + gb_ref[...]
        for m in range((rs.stop - rs.start) // LANE):
            top = gt[0:8, m * LANE:(m + 1) * LANE]
            bot = gt[8:16, m * LANE:(m + 1) * LANE]
            c0 = rs.start // CHUNK + 2 * m
            gc_ref[0, c0] = jnp.where(first_half, top, pltpu.roll(bot, CHUNK, 1))
            gc_ref[0, c0 + 1] = jnp.where(first_half, pltpu.roll(top, CHUNK, 1), bot)
        yield True

    _staggered([_Stages(stages(rs), 8) for rs in _sub_tiles(x_ref.shape[1])])


def _inproj(s, mods, mod_row, nw, w_in, w_gate_t, gate_b, lb_logits):
    bsz, t, d = s.shape
    hw = HG_HEADS * HG_DIM
    ng = gate_b.shape[0]
    tm = min(TOKEN_TILE, t)
    tok4 = lambda b, i: (b, i, 0, 0)
    const = lambda b, i: (0, 0)
    tok = lambda b, i: (b, i, 0)
    tok2 = lambda b, i: (b, 0, i, 0)
    f = lambda dt, *shape: jax.ShapeDtypeStruct(shape, dt)
    return pl.pallas_call(
        functools.partial(_inproj_kernel, hw=hw),
        grid=(bsz, t // tm),
        in_specs=[pl.BlockSpec((1, tm, d), tok),
                  pl.BlockSpec((1, 2, d), lambda b, i: (mod_row(b), 0, 0)),
                  pl.BlockSpec((1, d), const),
                  pl.BlockSpec(w_in.shape, const, pipeline_mode=pl.Buffered(1)),
                  pl.BlockSpec(w_gate_t.shape, const),
                  pl.BlockSpec((ng, 1), const),
                  pl.BlockSpec(lb_logits.shape, lambda b, i: (0, 0, 0))],
        out_specs=[pl.BlockSpec((1, tm, hw), tok), pl.BlockSpec((1, tm, hw), tok),
                   pl.BlockSpec((1, tm, hw), tok),
                   pl.BlockSpec((1, 2, tm, hw), tok2),
                   pl.BlockSpec((1, tm, hw), tok), pl.BlockSpec((1, tm, hw), tok),
                   pl.BlockSpec((1, tm, hw), tok), pl.BlockSpec((1, tm // CHUNK, ng // 2, LANE), tok4)],
        out_shape=[f(F32, bsz, t, hw), f(BF16, bsz, t, hw), f(BF16, bsz, t, hw),
                   f(F32, bsz, 2, t, hw),
                   f(BF16, bsz, t, hw), f(BF16, bsz, t, hw), f(BF16, bsz, t, hw),
                   f(F32, bsz, t // CHUNK, ng // 2, LANE)],
        compiler_params=_cparams(("parallel", "parallel")),
        name="inproj",
    )(s, mods, nw, w_in, w_gate_t, gate_b, lb_logits)


def _conv_kernel(x_ref, w_ref, b_ref, o_ref, *, width, q_tiles):
    x = x_ref[0].astype(F32)
    t = x.shape[0]
    col = lax.broadcasted_iota(jnp.int32, x.shape, 0) % width
    xl = jnp.where(col == 0, 0.0, pltpu.roll(x, 1, 0))
    xr = jnp.where(col == width - 1, 0.0, pltpu.roll(x, t - 1, 0))

    def row(di):
        return (w_ref[di, 0:1, :] * xl + w_ref[di, 1:2, :] * x + w_ref[di, 2:3, :] * xr)

    y = row(1)
    if t > width:
        pad = jnp.zeros((width, x.shape[1]), F32)
        y = y + jnp.concatenate([pad, row(0)[:t - width]], axis=0)
        y = y + jnp.concatenate([row(2)[width:], pad], axis=0)
    y = _silu(y + b_ref[...])
    qscale = jnp.where(pl.program_id(1) < q_tiles, ML_QK ** -0.5, 1.0)
    o_ref[0] = (y * qscale).astype(BF16)


def _conv(x, w, b, width):
    bsz, t, ch = x.shape
    return pl.pallas_call(
        functools.partial(_conv_kernel, width=width, q_tiles=ML_HEADS * ML_QK // LANE),
        grid=(bsz, ch // LANE),
        in_specs=[pl.BlockSpec((1, t, LANE), lambda b_, c: (b_, 0, c)),
                  pl.BlockSpec((3, 3, LANE), lambda b_, c: (0, 0, c)),
                  pl.BlockSpec((1, LANE), lambda b_, c: (0, c))],
        out_specs=pl.BlockSpec((1, t, LANE), lambda b_, c: (b_, 0, c)),
        out_shape=jax.ShapeDtypeStruct((bsz, t, ch), BF16),
        compiler_params=_cparams(("parallel", "parallel")),
        name="conv",
    )(x, w, b)


def _positions():
    t = np.arange(CHUNK)
    return [t, CHUNK - 1 - t]


def _hgrn2_masks():
    ds, vs, ws = [], [], []
    for p in _positions():
        le = (p[None, :] <= p[:, None]).astype(np.float32)
        order = [p[SUB * j] // SUB for j in range(N_SUB)]
        ref = np.zeros((32, CHUNK), np.float32)
        valid = np.zeros((N_SUB * N_SUB, 1), np.float32)
        for j in range(N_SUB):
            ref[j] = p <= SUB * order[j] + SUB // 2 - 1
            ref[8 + j] = p <= SUB * order[j] + SUB - 1
        ref[N_SUB] = 1.0
        for j in range(N_SUB):
            for i in range(N_SUB):
                if order[i] < order[j]:
                    ref[16 + N_SUB * j + i] = ref[j] - ref[8 + i]
                    valid[N_SUB * j + i] = 1.0
        ds.append(np.concatenate([le, ref], axis=0))
        vs.append(np.broadcast_to(valid, (N_SUB * N_SUB, HG_HEADS * HG_DIM)))
        tok_blk = np.arange(CHUNK)[:, None] // SUB
        ws.append(np.concatenate([(tok_blk == j) * le for j in range(N_SUB)], axis=1))
    return (np.stack(ds).astype(np.float32), np.stack(vs).astype(np.float32),
            np.stack(ws).astype(np.float32))


def _mlstm_masks():
    le2s, let2s = [], []
    for p in _positions():
        le = (p[None, :] <= p[:, None]).astype(np.float32)
        le2s.append(np.concatenate([le, le], axis=1))
        let2 = np.zeros((2 * CHUNK, 2 * CHUNK), np.float32)
        let2[:CHUNK, :CHUNK] = le.T
        let2[CHUNK:, CHUNK:] = le.T
        let2s.append(let2)
    return np.stack(le2s), np.stack(let2s)


def _pipelined(n_groups, group, stages_of, step_of):
    queue = []
    for g in range(n_groups):
        for _ in stages_of(g):
            if queue:
                queue.pop(0)()
            yield True
        while queue:
            queue.pop(0)()
            yield True
        queue = [functools.partial(step_of, g * group + i) for i in range(group)]
    for step in queue:
        step()
        yield True


def _interleave(programs):
    live = list(programs)
    while live:
        live = [p for p in live if next(p, None) is not None]


def _chunk_rows(c, nc, direction):
    cc = jnp.where(direction == 0, c, nc - 1 - c)
    return pl.multiple_of(cc * CHUNK, CHUNK)


def _hgrn2_program(q_ref, v_ref, lf_ref, d_ref, pv_ref, w_ref, s0_ref, o_ref, st_ref,
                   u_ref, qin_ref, g_ref, oi_ref, a_ref, vt_ref, bx_ref, kx_ref, vx_ref, *, nc):
    direction = pl.program_id(1)

    @pl.when(pl.program_id(2) == 0)
    def _():
        st_ref[...] = s0_ref[...]

    dmat = d_ref[0].astype(BF16)
    pair_valid = pv_ref[0]
    wmask = w_ref[0] > 0.0
    L = CHUNK
    hw = q_ref.shape[-1]
    upper_rows = lax.broadcasted_iota(jnp.int32, (L, 2 * L), 0) < L // 2

    group = min(GROUP, nc)

    def intra(grp):
        cs = [grp * group + i for i in range(group)]
        rows = [pl.ds(_chunk_rows(c, nc, direction), L) for c in cs]
        items = [(i, h) for i in range(group) for h in range(HG_HEADS)]
        lanes = [slice(h * HG_DIM, (h + 1) * HG_DIM) for h in range(HG_HEADS)]
        lfs = [lf_ref[0, 0, r, :] for r in rows]
        parts = [_split2(lf) for lf in lfs]
        e_all = sum(_dot(dmat, jnp.concatenate([parts[i][n] for i in range(group)], axis=1))
                    for n in range(2))
        yield
        q_mids, k_mids, k_outs, vbs, spans = [], [], [], [], []
        for i in range(group):
            e = e_all[:, i * hw:(i + 1) * hw]
            q = q_ref[0, rows[i], :]
            k = 1.0 - jnp.exp2(lfs[i])
            b = e[0:L]
            b_tot = e[L + N_SUB:L + N_SUB + 1]
            sub_rows = lambda r0: jnp.concatenate(
                [jnp.broadcast_to(e[r0 + j:r0 + j + 1], (SUB, hw)) for j in range(N_SUB)], axis=0)
            mid_rows = sub_rows(L)
            end_rows = sub_rows(L + 8)
            to_mid = b - mid_rows
            spans.append(jnp.max(jnp.abs(to_mid)))
            q_mids.append((q * jnp.exp2(jnp.minimum(to_mid, EXP2_CLAMP))).astype(BF16))
            k_diag = k * jnp.exp2(jnp.minimum(-to_mid, EXP2_CLAMP))
            k_end = k * jnp.exp2(end_rows - b)
            cross = jnp.exp2(e[L + 16:L + 32]) * pair_valid
            slabs = []
            for j in range(N_SUB):
                for ib in range(N_SUB):
                    rs = slice(ib * SUB, (ib + 1) * SUB)
                    row = N_SUB * j + ib
                    slabs.append(k_diag[rs] if ib == j else k_end[rs] * cross[row:row + 1])
            k_mids.append(jnp.concatenate(slabs, axis=0).astype(BF16))
            k_outs.append((k * jnp.exp2(b_tot - b)).astype(BF16))
            vbs.append(v_ref[0, rows[i], :])
            qin_ref[cs[i]] = (q * jnp.exp2(b)).astype(BF16)
            g_ref[cs[i]] = jnp.broadcast_to(jnp.exp2(b_tot), (SUBLANE, hw))
            yield
        scores = [_dot_nt(q_mids[i][:, lanes[h]], k_mids[i][:, lanes[h]]) for i, h in items]
        yield
        a_s = [jnp.where(wmask, s, 0.0) for s in scores]
        a_s = [jnp.where(upper_rows, a[:, :2 * L], a[:, 2 * L:]).astype(BF16) for a in a_s]
        for i in range(group):
            a_ref[cs[i]] = jnp.concatenate(a_s[i * HG_HEADS:(i + 1) * HG_HEADS], axis=1)
            oi_ref[cs[i]] = jnp.zeros((L, hw), F32)
        yield
        for i, h in items:
            v_h = vbs[i][:, lanes[h]].astype(F32)
            vt_ref[cs[i], h] = jnp.concatenate([v_h, v_h], axis=0).T.astype(BF16)
        yield
        for i, h in items:
            u_ref[cs[i], h] = _dot_tn(vbs[i][:, lanes[h]], k_outs[i][:, lanes[h]])

        @pl.when(functools.reduce(jnp.maximum, spans) > EXP2_CLAMP)
        def _():
            for i in range(group):
                exact_intra(cs[i], rows[i])
                a_ref[cs[i]] = jnp.zeros((L, hw), BF16)
        yield

    def exact_intra(c, rows):
        lf = lf_ref[0, 0, rows, :]
        bx_ref[...] = sum(_dot(dmat[0:L], x) for x in _split3(lf))
        kx_ref[...] = 1.0 - jnp.exp2(lf)
        vx_ref[...] = v_ref[0, rows, :].astype(F32)
        q = q_ref[0, rows, :]
        tok = lax.broadcasted_iota(jnp.int32, (L, 1), 0)
        pos = jnp.where(direction == 0, tok, L - 1 - tok)

        def add_source(s, acc):
            b_s = bx_ref[pl.ds(s, 1), :]
            z = q * kx_ref[pl.ds(s, 1), :] * jnp.exp2(jnp.minimum(bx_ref[...] - b_s, 0.0))
            v_s = vx_ref[pl.ds(s, 1), :]
            seen = pos >= jnp.where(direction == 0, s, L - 1 - s)
            cols = [jnp.where(seen, jnp.sum(z[:, ln], axis=1, keepdims=True), 0.0) * v_s[:, ln]
                    for ln in (slice(h * HG_DIM, (h + 1) * HG_DIM) for h in range(HG_HEADS))]
            return acc + jnp.concatenate(cols, axis=1)

        oi_ref[c] = lax.fori_loop(0, L, add_source, jnp.zeros((L, hw), F32))

    def carry_state(c):
        rows = pl.ds(_chunk_rows(c, nc, direction), L)
        q_in = qin_ref[c]
        a = a_ref[c]
        g = g_ref[c][0:1]
        outs, states = [], []
        for h in range(HG_HEADS):
            ln = slice(h * HG_DIM, (h + 1) * HG_DIM)
            st = st_ref[0, 0, h]
            outs.append(_dot_nt(jnp.concatenate([a[:, ln], q_in[:, ln]], axis=1),
                                jnp.concatenate([vt_ref[c, h], st.astype(BF16)], axis=1)))
            states.append(g[:, ln] * st + u_ref[c, h])
        o_ref[0, 0, rows, :] = (oi_ref[c] + jnp.concatenate(outs, axis=1)).astype(BF16)
        for h in range(HG_HEADS):
            st_ref[0, 0, h] = states[h]

    return _pipelined(nc // group, group, intra, carry_state)


N_HG = (7, 2, 9)
N_ML = (7, 3, 5)


def _mlstm_program(qk_ref, v_ref, g_ref, le2_ref, let2_ref, c0_ref, m0_ref,
                   o_ref, c_ref, mm_ref,
                   al_ref, u_ref, ct_ref, mloc_ref, sc_ref, *, nc):
    direction = pl.program_id(1)

    @pl.when(pl.program_id(2) == 0)
    def _():
        c_ref[...] = c0_ref[...]
        mm_ref[...] = m0_ref[...]

    let2 = let2_ref[0].astype(BF16)
    causal = le2_ref[0] > 0.0
    L = CHUNK
    neg = -jnp.inf
    lane = lax.broadcasted_iota(jnp.int32, (1, LANE), 1)
    seg = [lane < L, lane >= L]
    lane_t = lax.broadcasted_iota(jnp.int32, (L, LANE), 1) < L
    npair = ML_HEADS // 2
    ones_v = jnp.ones((L, ML_V), BF16)
    zb = jnp.zeros((L, LANE), BF16)

    def by_head(x):
        return jnp.concatenate([jnp.where(lane_t, x, zb), jnp.where(lane_t, zb, x)], axis=0)

    def v_stack(vb, p):
        return jnp.concatenate(
            [jnp.concatenate([vb[:, 2 * p * ML_V:(2 * p + 1) * ML_V], ones_v], axis=1),
             jnp.concatenate([vb[:, (2 * p + 1) * ML_V:(2 * p + 2) * ML_V], ones_v], axis=1)], axis=0)

    group = min(ML_GROUP, nc)

    def intra(grp):
        cs = [grp * group + i for i in range(group)]
        ccs = [jnp.where(direction == 0, c, nc - 1 - c) for c in cs]
        rows = [pl.ds(pl.multiple_of(cc * L, L), L) for cc in ccs]
        pairs = [(i, p) for i in range(group) for p in range(npair)]
        gs = [jnp.where(direction == 0, g_ref[0, cc][0:4], g_ref[0, cc][4:8]) for cc in ccs]
        pad = [jnp.zeros((-2 * group % SUBLANE, LANE), F32)] if 2 * group % SUBLANE else []
        i2 = jnp.concatenate([g[0:2] for g in gs] + pad, axis=0)
        lf2 = _log_sigmoid(jnp.concatenate([g[2:4] for g in gs] + pad, axis=0))
        parts = _split3(lf2)
        b2 = sum(_dot(x, let2) for x in parts)
        r2 = i2 - b2
        g_a = jnp.max(jnp.where(seg[0], r2, neg), axis=1, keepdims=True)
        g_b = jnp.max(jnp.where(seg[1], r2, neg), axis=1, keepdims=True)
        e_w = jnp.exp(r2 - jnp.where(seg[0], g_a, g_b))
        bt_a = jnp.sum(jnp.where(seg[0], lf2, 0.0), axis=1, keepdims=True)
        bt_b = jnp.sum(jnp.where(seg[1], lf2, 0.0), axis=1, keepdims=True)
        yield

        def as_columns(row):
            cols = jnp.broadcast_to(row, (LANE, LANE)).T
            return jnp.where(lane_t, cols[0:L], cols[L:2 * L])

        b_ts = [as_columns(b2[2 * i + p:2 * i + p + 1]) for i, p in pairs]
        for i in range(group):
            sc_ref[cs[i]] = jnp.concatenate(
                [jnp.broadcast_to(x[2 * i + p:2 * i + p + 1], (1, LANE))
                 for x2 in ((bt_a, bt_b), (g_a, g_b)) for p in range(npair) for x in x2], axis=0)
        yield
        qks = [qk_ref[0, r, :] for r in rows]
        vbs = [v_ref[0, r, :] for r in rows]
        q_ps = [qks[i][:, p * LANE:(p + 1) * LANE] for i, p in pairs]
        k_ps = [qks[i][:, (npair + p) * LANE:(npair + p + 1) * LANE] for i, p in pairs]
        kbs = [by_head(k) for k in k_ps]
        scores = [_dot_nt(q, kb) for q, kb in zip(q_ps, kbs)]
        yield
        for n, (i, p) in enumerate(pairs):
            t1 = b_ts[n]
            d_log = jnp.where(causal, t1 + r2[2 * i + p:2 * i + p + 1], neg)
            m_a = jnp.max(jnp.where(lane_t, d_log, neg), axis=1, keepdims=True)
            m_b = jnp.max(jnp.where(lane_t, neg, d_log), axis=1, keepdims=True)
            m_loc = jnp.where(lane_t, m_a, m_b)
            ct_ref[cs[i], p] = t1
            mloc_ref[cs[i], p] = m_loc
            al_ref[cs[i], p] = scores[n] * jnp.exp(d_log - m_loc)
        yield
        v_stacks = [v_stack(vbs[i], p) for i, p in pairs]
        yield
        kts = [(kb.astype(F32).T * e_w[2 * i + p:2 * i + p + 1]).astype(BF16) for kb, (i, p) in zip(kbs, pairs)]
        for n, (i, p) in enumerate(pairs):
            u_ref[cs[i], p] = _dot(kts[n], v_stacks[n])
        yield

    def carry_state(c):
        cc = jnp.where(direction == 0, c, nc - 1 - c)
        rows = pl.ds(pl.multiple_of(cc * L, L), L)
        m_all = mm_ref[0, 0]
        sc = sc_ref[c]
        qk = qk_ref[0, rows, :]
        vb = v_ref[0, rows, :]
        outs, states, m_rows = [], [], []
        for p in range(ML_HEADS // 2):
            c_pair = c_ref[0, 0, p]
            inc = u_ref[c, p]
            b_t = ct_ref[c, p]
            m_loc = mloc_ref[c, p]
            m_in2 = jnp.where(seg[0], m_all[2 * p:2 * p + 1], m_all[2 * p + 1:2 * p + 2])
            m_t = jnp.maximum(m_loc, b_t + m_in2)
            lhs_a = (jnp.exp(m_loc - m_t) * al_ref[c, p]).astype(BF16)
            lhs_q = (jnp.exp(b_t + m_in2 - m_t) * qk[:, p * LANE:(p + 1) * LANE].astype(F32)).astype(BF16)
            lhs = jnp.concatenate([by_head(lhs_a), by_head(lhs_q)], axis=1)
            tot = _dot(lhs, jnp.concatenate([v_stack(vb, p), c_pair.astype(BF16)], axis=0))
            floor = jnp.exp(-m_t)
            floor_r = pltpu.roll(floor, L, 1)
            new_rows = []
            for hh in range(2):
                h = 2 * p + hh
                rs = slice(hh * L, (hh + 1) * L)
                m_in = m_all[h:h + 1]
                floor_h = jnp.where(lane_t, floor, floor_r) if hh == 0 else jnp.where(lane_t, floor_r, floor)
                outs.append(tot[rs, :ML_V] / jnp.maximum(jnp.abs(tot[rs, ML_V:]), floor_h))
                b_tot = sc[h:h + 1]
                m_new = b_tot + jnp.maximum(m_in, sc[ML_HEADS + h:ML_HEADS + h + 1])
                carry_w = jnp.exp(b_tot + m_in - m_new)
                inc_w = jnp.exp(b_tot + sc[ML_HEADS + h:ML_HEADS + h + 1] - m_new)
                cw2 = jnp.concatenate([carry_w, carry_w], axis=1)
                iw2 = jnp.concatenate([inc_w, inc_w], axis=1)
                new_rows.append(cw2 * c_pair[rs] + iw2 * inc[rs])
                m_rows.append(m_new)
            states.append(jnp.concatenate(new_rows, axis=0))
        o_ref[0, 0, rows, :] = jnp.concatenate(outs, axis=1).astype(BF16)
        for p in range(ML_HEADS // 2):
            c_ref[0, 0, p] = states[p]
        mm_ref[0, 0] = jnp.concatenate(m_rows + [m_all[ML_HEADS:]], axis=0)

    return _pipelined(nc // group, group, intra, carry_state)


def _scan_kernel(*refs, nc):
    hg_in, ml_in, hg_out, ml_out, hg_scr, ml_scr = (
        refs[a:b] for a, b in zip(np.cumsum([0, N_HG[0], N_ML[0], N_HG[1], N_ML[1], N_HG[2]]),
                                  np.cumsum([N_HG[0], N_ML[0], N_HG[1], N_ML[1], N_HG[2], N_ML[2]])))
    _interleave([_hgrn2_program(*hg_in, *hg_out, *hg_scr, nc=nc),
                 _mlstm_program(*ml_in, *ml_out, *ml_scr, nc=nc)])


def _scans(hq, hv, hlf, qk, mv, gates, s0, c0, m0, dmat, pair_valid, wmask, le2, let2):
    bsz, t, hw = hq.shape
    nc = min(SCAN_CHUNKS, t // CHUNK)
    tb = nc * CHUNK
    nblk = t // tb
    npair = ML_HEADS // 2

    def blk(j, d):
        return jnp.where(d == 0, j, nblk - 1 - j)

    tok = lambda b, d, j: (b, blk(j, d), 0)
    tok_d = lambda b, d, j: (b, d, blk(j, d), 0)
    per_dir = lambda a: pl.BlockSpec((1,) + a.shape[1:], lambda b, d, j: (d,) + (0,) * (a.ndim - 1))
    state = lambda a: pl.BlockSpec((1, 1) + a.shape[2:], lambda b, d, j: (b, d) + (0,) * (a.ndim - 2))
    out_tok = pl.BlockSpec((1, 1, tb, hw), tok_d)
    hg_in = [pl.BlockSpec((1, tb, hw), tok), pl.BlockSpec((1, tb, hw), tok), pl.BlockSpec((1, 1, tb, hw), tok_d),
             per_dir(dmat), per_dir(pair_valid), per_dir(wmask), state(s0)]
    ml_in = [pl.BlockSpec((1, tb, qk.shape[-1]), tok), pl.BlockSpec((1, tb, hw), tok),
             pl.BlockSpec((1, nc) + gates.shape[2:], lambda b, d, j: (b, blk(j, d), 0, 0)),
             per_dir(le2), per_dir(let2), state(c0), state(m0)]
    sds = jax.ShapeDtypeStruct
    hg_scr = [pltpu.VMEM((nc,) + s0.shape[2:], F32),
              pltpu.VMEM((nc, CHUNK, hw), BF16),
              pltpu.VMEM((nc, SUBLANE, hw), F32),
              pltpu.VMEM((nc, CHUNK, hw), F32),
              pltpu.VMEM((nc, CHUNK, hw), BF16),
              pltpu.VMEM((nc,) + s0.shape[2:], BF16),
              pltpu.VMEM((CHUNK, hw), F32),
              pltpu.VMEM((CHUNK, hw), F32),
              pltpu.VMEM((CHUNK, hw), F32)]
    ml_scr = [pltpu.VMEM((nc, npair, CHUNK, LANE), F32),
              pltpu.VMEM((nc, npair, 2 * ML_QK, 2 * ML_V), F32),
              pltpu.VMEM((nc, npair, CHUNK, LANE), F32),
              pltpu.VMEM((nc, npair, CHUNK, LANE), F32),
              pltpu.VMEM((nc, SUBLANE, LANE), F32)]
    assert (len(hg_in), 2, len(hg_scr)) == N_HG and (len(ml_in), 3, len(ml_scr)) == N_ML
    return pl.pallas_call(
        functools.partial(_scan_kernel, nc=nc),
        grid=(bsz, 2, nblk),
        in_specs=hg_in + ml_in,
        out_specs=[out_tok, state(s0), out_tok, state(c0), state(m0)],
        out_shape=[sds((bsz, 2, t, hw), BF16), sds(s0.shape, F32),
                   sds((bsz, 2, t, hw), BF16), sds(c0.shape, F32), sds(m0.shape, F32)],
        scratch_shapes=hg_scr + ml_scr,
        compiler_params=_cparams(("parallel", "parallel", "arbitrary")),
        name="scans",
    )(hq, hv, hlf, dmat, pair_valid, wmask, s0, qk, mv, gates, le2, let2, c0, m0)


def _head_rms(x, heads):
    dh = x.shape[-1] // heads
    return jnp.concatenate([_rms(x[:, h * dh:(h + 1) * dh]) for h in range(heads)], axis=1)


def _mix_ffn_kernel(oh_ref, om_ref, hg_ref, mo_ref, x_ref, mod_ref, hnw_ref, mnw_ref, wo_ref,
                    nw_ref, w1_ref, w2_ref, fw_ref, o_ref, a_ref, x_scr, h_scr, *, d_ff, tf):
    mix_gate = mod_ref[0, 0:1, :]
    mods = [mod_ref[0, k:k + 1, :] for k in range(1, 4)]

    def stages(rs):
        def mixed(pr):
            oh = oh_ref[0, 0, pr, :].astype(F32) + oh_ref[0, 1, pr, :].astype(F32)
            om = om_ref[0, 0, pr, :].astype(F32) + om_ref[0, 1, pr, :].astype(F32)
            hg_out = _head_rms(oh, HG_HEADS) * hnw_ref[...] * _silu(hg_ref[0, pr, :].astype(F32))
            ml_out = _sigmoid(mo_ref[0, pr, :].astype(F32)) * (_head_rms(om, ML_HEADS) * mnw_ref[...])
            merged = jnp.concatenate([hg_out, ml_out], axis=1).astype(BF16)
            return x_ref[0, pr, :] + mix_gate * _dot(merged, wo_ref[...])

        def put(y):
            o_ref[0, rs, :] = _rms(y) * fw_ref[...]

        return _Stages(_ffn_stages(mixed, put, mods, nw_ref, w1_ref, w2_ref, a_ref, x_scr, h_scr,
                                   rs, d_ff, tf), _ffn_lead(rs, d_ff, tf))

    _staggered([stages(rs) for rs in _sub_tiles(x_ref.shape[1])])


def _mix_ffn(oh, om, hg, mo, x, mods, hnw, mnw, w_out, nw, w1, w2, fw):
    bsz, t, d = x.shape
    hw = hg.shape[-1]
    d_ff = w2.shape[0]
    tm = min(TOKEN_TILE, t)
    tok = lambda b, i: (b, i, 0)
    tok2 = lambda b, i: (b, 0, i, 0)
    const = lambda b, i: (0, 0)
    resident = lambda a: pl.BlockSpec(a.shape, const, pipeline_mode=pl.Buffered(1))
    return pl.pallas_call(
        functools.partial(_mix_ffn_kernel, d_ff=d_ff, tf=FF_TILE),
        grid=(bsz, t // tm),
        in_specs=[pl.BlockSpec((1, 2, tm, hw), tok2), pl.BlockSpec((1, 2, tm, hw), tok2),
                  pl.BlockSpec((1, tm, hw), tok), pl.BlockSpec((1, tm, hw), tok),
                  pl.BlockSpec((1, tm, d), tok),
                  pl.BlockSpec((1, 4, d), lambda b, i: (b, 0, 0)),
                  pl.BlockSpec((1, hw), const), pl.BlockSpec((1, hw), const), resident(w_out),
                  pl.BlockSpec((1, d), const), resident(w1), resident(w2), pl.BlockSpec((1, d), const)],
        out_specs=pl.BlockSpec((1, tm, d), tok),
        out_shape=jax.ShapeDtypeStruct((bsz, t, d), F32),
        scratch_shapes=[pltpu.VMEM((tm, d_ff), BF16),
                        pltpu.VMEM((tm, d), F32),
                        pltpu.VMEM((tm, d), BF16)],
        compiler_params=_cparams(("parallel", "parallel")),
        name="mix_ffn_final",
    )(oh, om, hg, mo, x, mods, hnw, mnw, w_out, nw, w1, w2, fw)


def kernel(x, c, ctx, c_ctx, w_mod, b_mod, norm1_w, ffn1_w1, ffn1_w2, norm2_w, w_in, ml_gate_b,
           ml_conv_w, ml_conv_b, hg_lb_logits, hg_norm_w, ml_norm_w, w_out, norm3_w, ffn2_w1, ffn2_w2,
           final_norm_w):
    bsz, seq, d = x.shape
    n_ctx = ctx.shape[1]
    assert w_mod.shape[0] == 1, "single-layer kernel"
    assert seq % max(TOKEN_TILE, SCAN_CHUNKS * CHUNK) == 0 and n_ctx % CHUNK == 0 and GRID_W == CHUNK
    hw = HG_HEADS * HG_DIM
    ng = ml_gate_b.shape[-1]

    rows = -(-(bsz + 1) // SUBLANE) * SUBLANE
    cvec = jnp.zeros((rows, d), F32).at[:bsz].set(c).at[bsz].set(c_ctx)
    mods = _modulation(cvec, w_mod[0], b_mod[0][None, :]).reshape(rows, N_MOD, d)
    lat_row = lambda b: b
    ctx_row = lambda b: bsz

    row = lambda a: a.reshape(1, -1)
    w1a, w2a = ffn1_w1[0], ffn1_w2[0]
    x1 = _ffn(x, mods[:, 0:3], lat_row, row(norm1_w[0]), w1a, w2a)
    s1 = _ffn(ctx, mods[:, 0:3], ctx_row, row(norm1_w[0]), w1a, w2a)

    w_in_b = w_in[0].T
    gate_order = np.arange(ng).reshape(2, 2, ML_HEADS // 2, 2).transpose(3, 0, 1, 2).reshape(-1)
    w_gate_t = w_in[0][:, 8 * hw:].T[gate_order].astype(BF16)
    gate_b = ml_gate_b[0][gate_order].reshape(ng, 1)
    dmat, pair_valid, wmask = (jnp.asarray(a) for a in _hgrn2_masks())
    le2, let2 = (jnp.asarray(a) for a in _mlstm_masks())
    conv_w, conv_b = ml_conv_w[0], row(ml_conv_b[0])
    npair = ML_HEADS // 2

    def mixer_scans(s, mod_row, width, s0, c0, m0):
        hq, hv, hg, hlf, mqk, mv, mo, gates = _inproj(
            s, mods[:, 3:5], mod_row, row(norm2_w[0]), w_in_b, w_gate_t, gate_b, hg_lb_logits)
        qk = _conv(mqk, conv_w, conv_b, width)
        oh, s_fin, om, c_fin, m_fin = _scans(hq, hv, hlf, qk, mv, gates, s0, c0, m0,
                                             dmat, pair_valid, wmask, le2, let2)
        return oh, om, hg, mo, s_fin, c_fin, m_fin

    s0 = jnp.zeros((bsz, 2, HG_HEADS, HG_DIM, HG_DIM), F32)
    c0 = jnp.zeros((bsz, 2, npair, 2 * ML_QK, 2 * ML_V), F32)
    m0 = jnp.zeros((bsz, 2, SUBLANE, LANE), F32)
    _, _, _, _, s_ctx, c_ctx_state, m_ctx = mixer_scans(s1, ctx_row, n_ctx, s0, c0, m0)
    oh, om, hg, mo, _, _, _ = mixer_scans(x1, lat_row, GRID_W, s_ctx, c_ctx_state, m_ctx)

    return _mix_ffn(oh, om, hg, mo, x1, mods[:, 5:9], row(hg_norm_w[0]), row(ml_norm_w[0]),
                    w_out[0].astype(BF16), row(norm3_w[0]), ffn2_w1[0], ffn2_w2[0], row(final_norm_w))
```

```python
import functools

import numpy as np
import jax
import jax.numpy as jnp
from jax import lax
from jax.experimental import pallas as pl
from jax.experimental.pallas import tpu as pltpu

F32 = jnp.float32
BF16 = jnp.bfloat16

EPS = 1e-6
CHUNK = 64
GRID_W = 64
N_MOD = 9
HG_HEADS = 4
HG_DIM = 128
ML_HEADS = 4
ML_QK = 64
ML_V = 128
SUB = 16
N_SUB = CHUNK // SUB
GROUP = 4
ML_GROUP = 4
SCAN_CHUNKS = 8
TOKEN_TILE = 512
ROW_SUB = 256
FF_TILE = 256
PIECE = 128
LOG2E = 1.4426950408889634
EXP2_CLAMP = 115.0
LANE = 128
SUBLANE = 8
VMEM_LIMIT = 56 * 1024 * 1024


def _cparams(sem):
    return pltpu.CompilerParams(dimension_semantics=sem, vmem_limit_bytes=VMEM_LIMIT)


def _dot(a, b):
    return jnp.dot(a, b, preferred_element_type=F32)


def _dot_nt(a, b):
    return lax.dot_general(a, b, (((1,), (1,)), ((), ())), preferred_element_type=F32)


def _dot_tn(a, b):
    return lax.dot_general(a, b, (((0,), (0,)), ((), ())), preferred_element_type=F32)


def _sigmoid(x):
    return 1.0 / (1.0 + jnp.exp(-x))


def _silu(x):
    return x * _sigmoid(x)


def _log_sigmoid(x):
    return jnp.minimum(x, 0.0) - jnp.log(1.0 + jnp.exp(-jnp.abs(x)))


def _split3(x):
    hi = x.astype(BF16)
    r = x - hi.astype(F32)
    mid = r.astype(BF16)
    lo = (r - mid.astype(F32)).astype(BF16)
    return hi, mid, lo


def _split2(x):
    hi = x.astype(BF16)
    return hi, (x - hi.astype(F32)).astype(BF16)


def _rms(x):
    return x * lax.rsqrt(jnp.mean(x * x, axis=-1, keepdims=True) + EPS)


def _mod_kernel(c_ref, w_ref, b_ref, o_ref):
    a = _silu(c_ref[...])
    a_hi = a.astype(BF16)
    a_lo = (a - a_hi.astype(F32)).astype(BF16)
    w = w_ref[...]
    w_hi = w.astype(BF16)
    w_lo = (w - w_hi.astype(F32)).astype(BF16)
    o_ref[...] = _dot(a_hi, w_hi) + _dot(a_hi, w_lo) + _dot(a_lo, w_hi) + b_ref[...]


def _modulation(cvec, w_mod, b_mod):
    rows, d = cvec.shape
    n = w_mod.shape[1]
    tn = 1024
    return pl.pallas_call(
        _mod_kernel,
        grid=(n // tn,),
        in_specs=[pl.BlockSpec((rows, d), lambda j: (0, 0)),
                  pl.BlockSpec((d, tn), lambda j: (0, j)),
                  pl.BlockSpec((1, tn), lambda j: (0, j))],
        out_specs=pl.BlockSpec((rows, tn), lambda j: (0, j)),
        out_shape=jax.ShapeDtypeStruct((rows, n), F32),
        compiler_params=_cparams(("arbitrary",)),
        name="modulation",
    )(cvec, w_mod, b_mod)


def _staggered(tiles):
    for _ in range(tiles[0].lead):
        next(tiles[0].gen)
    live = [t.gen for t in tiles]
    while live:
        live = [g for g in live if next(g, None) is not None]


class _Stages:
    def __init__(self, gen, lead):
        self.gen, self.lead = gen, lead


def _ffn_stages(get_x, put_y, mods, nw_ref, w1_ref, w2_ref, a_ref, x_scr, h_scr, rs, d_ff, tf):
    shift, scale, gate = mods
    for p0 in range(rs.start, rs.stop, PIECE):
        pr = slice(p0, min(p0 + PIECE, rs.stop))
        x = get_x(pr)
        x_scr[pr, :] = x
        h_scr[pr, :] = (_rms(x) * nw_ref[...] * (1.0 + scale) + shift).astype(BF16)
        yield True
    h = h_scr[rs, :]
    for c in range(d_ff // tf):
        g = _dot(h, w1_ref[:, c * tf:(c + 1) * tf].astype(BF16))
        u = _dot(h, w1_ref[:, d_ff + c * tf:d_ff + (c + 1) * tf].astype(BF16))
        a_ref[rs, c * tf:(c + 1) * tf] = (_silu(g) * u).astype(BF16)
        yield True
    y = x_scr[rs, :] + 0.5 * gate * _dot(a_ref[rs, :], w2_ref[...].astype(BF16))
    yield True
    put_y(y)
    yield True


def _sub_tiles(rows):
    sub = min(ROW_SUB, rows)
    return [slice(r0, r0 + sub) for r0 in range(0, rows, sub)]


def _ffn_lead(rs, d_ff, tf):
    return -(-(rs.stop - rs.start) // PIECE) + d_ff // tf // 2


def _ffn_kernel(x_ref, mod_ref, nw_ref, w1_ref, w2_ref, o_ref, a_ref, x_scr, h_scr, *, d_ff, tf):
    mods = [mod_ref[0, k:k + 1, :] for k in range(3)]

    def stages(rs):
        def put(y):
            o_ref[0, rs, :] = y
        return _Stages(_ffn_stages(lambda pr: x_ref[0, pr, :], put, mods, nw_ref, w1_ref, w2_ref, a_ref,
                                   x_scr, h_scr, rs, d_ff, tf), _ffn_lead(rs, d_ff, tf))

    _staggered([stages(rs) for rs in _sub_tiles(x_ref.shape[1])])


def _ffn(s, mods, mod_row, nw, w1, w2):
    bsz, t, d = s.shape
    d_ff = w2.shape[0]
    tm = min(TOKEN_TILE, t)
    const = lambda b, i: (0, 0)
    return pl.pallas_call(
        functools.partial(_ffn_kernel, d_ff=d_ff, tf=FF_TILE),
        grid=(bsz, t // tm),
        in_specs=[pl.BlockSpec((1, tm, d), lambda b, i: (b, i, 0)),
                  pl.BlockSpec((1, 3, d), lambda b, i: (mod_row(b), 0, 0)),
                  pl.BlockSpec((1, d), const),
                  pl.BlockSpec((d, 2 * d_ff), const, pipeline_mode=pl.Buffered(1)),
                  pl.BlockSpec((d_ff, d), const, pipeline_mode=pl.Buffered(1))],
        out_specs=pl.BlockSpec((1, tm, d), lambda b, i: (b, i, 0)),
        out_shape=jax.ShapeDtypeStruct((bsz, t, d), F32),
        scratch_shapes=[pltpu.VMEM((tm, d_ff), BF16),
                        pltpu.VMEM((tm, d), F32),
                        pltpu.VMEM((tm, d), BF16)],
        compiler_params=_cparams(("parallel", "parallel")),
        name="ffn",
    )(s, mods, nw, w1, w2)


def _inproj_kernel(x_ref, mod_ref, nw_ref, w_ref, wg_ref, gb_ref, lbl_ref,
                   hq_ref, hv_ref, hg_ref, hlf_ref, mqk_ref, mv_ref, mo_ref, gc_ref, *, hw):
    shift = mod_ref[0, 0:1, :]
    scale = mod_ref[0, 1:2, :]
    lbl = lbl_ref[...]
    e = jnp.exp(lbl - jnp.max(lbl, axis=0, keepdims=True))
    lb = e[0] / jnp.sum(e, axis=0)
    first_half = lax.broadcasted_iota(jnp.int32, (SUBLANE, LANE), 1) < CHUNK
    def stages(rs):
        h = (_rms(x_ref[0, rs, :]) * nw_ref[...] * (1.0 + scale) + shift).astype(BF16)
        yield True

        def proj(k):
            return _dot_nt(h, w_ref[k * hw:(k + 1) * hw, :].astype(BF16))

        hq_ref[0, rs, :] = _silu(proj(0)) * (HG_DIM ** -0.5)
        yield True
        hv_ref[0, rs, :] = proj(1).astype(BF16)
        yield True
        hg_ref[0, rs, :] = proj(2).astype(BF16)
        yield True
        for d in range(2):
            lbd = lb[d:d + 1, :]
            hlf_ref[0, d, rs, :] = jnp.log(lbd + (1.0 - lbd) * _sigmoid(proj(3 + d))) * LOG2E
            yield True
        mqk_ref[0, rs, :] = proj(5).astype(BF16)
        yield True
        mv_ref[0, rs, :] = proj(6).astype(BF16)
        yield True
        mo_ref[0, rs, :] = proj(7).astype(BF16)
        gt = _dot_nt(wg_ref[...], h) + gb_ref[...]
        for m in range((rs.stop - rs.start) // LANE):
            top = gt[0:8, m * LANE:(m + 1) * LANE]
            bot = gt[8:16, m * LANE:(m + 1) * LANE]
            c0 = rs.start // CHUNK + 2 * m
            gc_ref[0, c0] = jnp.where(first_half, top, pltpu.roll(bot, CHUNK, 1))
            gc_ref[0, c0 + 1] = jnp.where(first_half, pltpu.roll(top, CHUNK, 1), bot)
        yield True

    _staggered([_Stages(stages(rs), 8) for rs in _sub_tiles(x_ref.shape[1])])


def _inproj(s, mods, mod_row, nw, w_in, w_gate_t, gate_b, lb_logits):
    bsz, t, d = s.shape
    hw = HG_HEADS * HG_DIM
    ng = gate_b.shape[0]
    tm = min(TOKEN_TILE, t)
    tok4 = lambda b, i: (b, i, 0, 0)
    const = lambda b, i: (0, 0)
    tok = lambda b, i: (b, i, 0)
    tok2 = lambda b, i: (b, 0, i, 0)
    f = lambda dt, *shape: jax.ShapeDtypeStruct(shape, dt)
    return pl.pallas_call(
        functools.partial(_inproj_kernel, hw=hw),
        grid=(bsz, t // tm),
        in_specs=[pl.BlockSpec((1, tm, d), tok),
                  pl.BlockSpec((1, 2, d), lambda b, i: (mod_row(b), 0, 0)),
                  pl.BlockSpec((1, d), const),
                  pl.BlockSpec(w_in.shape, const, pipeline_mode=pl.Buffered(1)),
                  pl.BlockSpec(w_gate_t.shape, const),
                  pl.BlockSpec((ng, 1), const),
                  pl.BlockSpec(lb_logits.shape, lambda b, i: (0, 0, 0))],
        out_specs=[pl.BlockSpec((1, tm, hw), tok), pl.BlockSpec((1, tm, hw), tok),
                   pl.BlockSpec((1, tm, hw), tok),
                   pl.BlockSpec((1, 2, tm, hw), tok2),
                   pl.BlockSpec((1, tm, hw), tok), pl.BlockSpec((1, tm, hw), tok),
                   pl.BlockSpec((1, tm, hw), tok), pl.BlockSpec((1, tm // CHUNK, ng // 2, LANE), tok4)],
        out_shape=[f(F32, bsz, t, hw), f(BF16, bsz, t, hw), f(BF16, bsz, t, hw),
                   f(F32, bsz, 2, t, hw),
                   f(BF16, bsz, t, hw), f(BF16, bsz, t, hw), f(BF16, bsz, t, hw),
                   f(F32, bsz, t // CHUNK, ng // 2, LANE)],
        compiler_params=_cparams(("parallel", "parallel")),
        name="inproj",
    )(s, mods, nw, w_in, w_gate_t, gate_b, lb_logits)


def _conv_kernel(x_ref, w_ref, b_ref, o_ref, *, width, q_tiles):
    x = x_ref[0].astype(F32)
    t = x.shape[0]
    col = lax.broadcasted_iota(jnp.int32, x.shape, 0) % width
    xl = jnp.where(col == 0, 0.0, pltpu.roll(x, 1, 0))
    xr = jnp.where(col == width - 1, 0.0, pltpu.roll(x, t - 1, 0))

    def row(di):
        return (w_ref[di, 0:1, :] * xl + w_ref[di, 1:2, :] * x + w_ref[di, 2:3, :] * xr)

    y = row(1)
    if t > width:
        pad = jnp.zeros((width, x.shape[1]), F32)
        y = y + jnp.concatenate([pad, row(0)[:t - width]], axis=0)
        y = y + jnp.concatenate([row(2)[width:], pad], axis=0)
    y = _silu(y + b_ref[...])
    qscale = jnp.where(pl.program_id(1) < q_tiles, ML_QK ** -0.5, 1.0)
    o_ref[0] = (y * qscale).astype(BF16)


def _conv(x, w, b, width):
    bsz, t, ch = x.shape
    return pl.pallas_call(
        functools.partial(_conv_kernel, width=width, q_tiles=ML_HEADS * ML_QK // LANE),
        grid=(bsz, ch // LANE),
        in_specs=[pl.BlockSpec((1, t, LANE), lambda b_, c: (b_, 0, c)),
                  pl.BlockSpec((3, 3, LANE), lambda b_, c: (0, 0, c)),
                  pl.BlockSpec((1, LANE), lambda b_, c: (0, c))],
        out_specs=pl.BlockSpec((1, t, LANE), lambda b_, c: (b_, 0, c)),
        out_shape=jax.ShapeDtypeStruct((bsz, t, ch), BF16),
        compiler_params=_cparams(("parallel", "parallel")),
        name="conv",
    )(x, w, b)


def _positions():
    t = np.arange(CHUNK)
    return [t, CHUNK - 1 - t]


def _hgrn2_masks():
    ds, vs, ws = [], [], []
    for p in _positions():
        le = (p[None, :] <= p[:, None]).astype(np.float32)
        order = [p[SUB * j] // SUB for j in range(N_SUB)]
        ref = np.zeros((32, CHUNK), np.float32)
        valid = np.zeros((N_SUB * N_SUB, 1), np.float32)
        for j in range(N_SUB):
            ref[j] = p <= SUB * order[j] + SUB // 2 - 1
            ref[8 + j] = p <= SUB * order[j] + SUB - 1
        ref[N_SUB] = 1.0
        for j in range(N_SUB):
            for i in range(N_SUB):
                if order[i] < order[j]:
                    ref[16 + N_SUB * j + i] = ref[j] - ref[8 + i]
                    valid[N_SUB * j + i] = 1.0
        ds.append(np.concatenate([le, ref], axis=0))
        vs.append(np.broadcast_to(valid, (N_SUB * N_SUB, HG_HEADS * HG_DIM)))
        tok_blk = np.arange(CHUNK)[:, None] // SUB
        ws.append(np.concatenate([(tok_blk == j) * le for j in range(N_SUB)], axis=1))
    return (np.stack(ds).astype(np.float32), np.stack(vs).astype(np.float32),
            np.stack(ws).astype(np.float32))


def _mlstm_masks():
    le2s, let2s = [], []
    for p in _positions():
        le = (p[None, :] <= p[:, None]).astype(np.float32)
        le2s.append(np.concatenate([le, le], axis=1))
        let2 = np.zeros((2 * CHUNK, 2 * CHUNK), np.float32)
        let2[:CHUNK, :CHUNK] = le.T
        let2[CHUNK:, CHUNK:] = le.T
        let2s.append(let2)
    return np.stack(le2s), np.stack(let2s)


def _pipelined(n_groups, group, stages_of, step_of):
    queue = []
    for g in range(n_groups):
        for _ in stages_of(g):
            if queue:
                queue.pop(0)()
            yield True
        while queue:
            queue.pop(0)()
            yield True
        queue = [functools.partial(step_of, g * group + i) for i in range(group)]
    for step in queue:
        step()
        yield True


def _interleave(programs):
    live = list(programs)
    while live:
        live = [p for p in live if next(p, None) is not None]


def _chunk_rows(c, nc, direction):
    cc = jnp.where(direction == 0, c, nc - 1 - c)
    return pl.multiple_of(cc * CHUNK, CHUNK)


def _hgrn2_program(q_ref, v_ref, lf_ref, d_ref, pv_ref, w_ref, s0_ref, o_ref, st_ref,
                   u_ref, qin_ref, g_ref, a_ref, vt_ref, bx_ref, kx_ref, *, nc):
    direction = pl.program_id(1)

    @pl.when(pl.program_id(2) == 0)
    def _():
        st_ref[...] = s0_ref[...]

    dmat = d_ref[0].astype(BF16)
    pair_valid = pv_ref[0]
    wmask = w_ref[0] > 0.0
    L = CHUNK
    hw = q_ref.shape[-1]
    upper_rows = lax.broadcasted_iota(jnp.int32, (L, 2 * L), 0) < L // 2

    group = min(GROUP, nc)

    def intra(grp):
        cs = [grp * group + i for i in range(group)]
        rows = [pl.ds(_chunk_rows(c, nc, direction), L) for c in cs]
        items = [(i, h) for i in range(group) for h in range(HG_HEADS)]
        lanes = [slice(h * HG_DIM, (h + 1) * HG_DIM) for h in range(HG_HEADS)]
        lfs = [lf_ref[0, 0, r, :] for r in rows]
        parts = [_split2(lf) for lf in lfs]
        e_all = sum(_dot(dmat, jnp.concatenate([parts[i][n] for i in range(group)], axis=1))
                    for n in range(2))
        yield
        q_mids, k_mids, k_outs, vbs, spans = [], [], [], [], []
        for i in range(group):
            e = e_all[:, i * hw:(i + 1) * hw]
            q = q_ref[0, rows[i], :]
            k = 1.0 - jnp.exp2(lfs[i])
            b = e[0:L]
            b_tot = e[L + N_SUB:L + N_SUB + 1]
            sub_rows = lambda r0: jnp.concatenate(
                [jnp.broadcast_to(e[r0 + j:r0 + j + 1], (SUB, hw)) for j in range(N_SUB)], axis=0)
            mid_rows = sub_rows(L)
            end_rows = sub_rows(L + 8)
            to_mid = b - mid_rows
            spans.append(jnp.max(jnp.abs(to_mid)))
            q_mids.append((q * jnp.exp2(jnp.minimum(to_mid, EXP2_CLAMP))).astype(BF16))
            k_diag = k * jnp.exp2(jnp.minimum(-to_mid, EXP2_CLAMP))
            k_end = k * jnp.exp2(end_rows - b)
            cross = jnp.exp2(e[L + 16:L + 32]) * pair_valid
            slabs = []
            for j in range(N_SUB):
                for ib in range(N_SUB):
                    rs = slice(ib * SUB, (ib + 1) * SUB)
                    row = N_SUB * j + ib
                    slabs.append(k_diag[rs] if ib == j else k_end[rs] * cross[row:row + 1])
            k_mids.append(jnp.concatenate(slabs, axis=0).astype(BF16))
            k_outs.append((k * jnp.exp2(b_tot - b)).astype(BF16))
            vbs.append(v_ref[0, rows[i], :])
            qin_ref[cs[i]] = (q * jnp.exp2(b)).astype(BF16)
            g_ref[cs[i]] = jnp.broadcast_to(jnp.exp2(b_tot), (SUBLANE, hw))
            yield
        scores = [_dot_nt(q_mids[i][:, lanes[h]], k_mids[i][:, lanes[h]]) for i, h in items]
        yield
        a_s = [jnp.where(wmask, s, 0.0) for s in scores]
        a_s = [jnp.where(upper_rows, a[:, :2 * L], a[:, 2 * L:]).astype(BF16) for a in a_s]
        for i in range(group):
            a_ref[cs[i]] = jnp.concatenate(a_s[i * HG_HEADS:(i + 1) * HG_HEADS], axis=1)
        yield
        for i, h in items:
            v_h = vbs[i][:, lanes[h]].astype(F32)
            vt_ref[cs[i], h] = jnp.concatenate([v_h, v_h], axis=0).T.astype(BF16)
        yield
        for i, h in items:
            u_ref[cs[i], h] = _dot_tn(vbs[i][:, lanes[h]], k_outs[i][:, lanes[h]])

        @pl.when(functools.reduce(jnp.maximum, spans) > EXP2_CLAMP)
        def _():
            for i in range(group):
                exact_intra(cs[i], rows[i])
        yield

    def exact_intra(c, rows):
        lf = lf_ref[0, 0, rows, :]
        bx_ref[...] = sum(_dot(dmat[0:L], x) for x in _split3(lf))
        kx_ref[...] = 1.0 - jnp.exp2(lf)
        q = q_ref[0, rows, :]
        tok = lax.broadcasted_iota(jnp.int32, (L, 1), 0)
        pos = jnp.where(direction == 0, tok, L - 1 - tok)
        lane_in_head = lax.broadcasted_iota(jnp.int32, (1, hw), 1) % HG_DIM

        def add_source(s, acc):
            b_s = bx_ref[pl.ds(s, 1), :]
            z = q * kx_ref[pl.ds(s, 1), :] * jnp.exp2(jnp.minimum(bx_ref[...] - b_s, 0.0))
            seen = pos >= jnp.where(direction == 0, s, L - 1 - s)
            cols = [jnp.broadcast_to(jnp.where(seen, jnp.sum(z[:, ln], axis=1, keepdims=True), 0.0), (L, HG_DIM))
                    for ln in (slice(h * HG_DIM, (h + 1) * HG_DIM) for h in range(HG_HEADS))]
            return jnp.where(lane_in_head == s, jnp.concatenate(cols, axis=1), acc)

        a_ref[c] = lax.fori_loop(0, L, add_source, jnp.zeros((L, hw), F32)).astype(BF16)

    def carry_state(c):
        rows = pl.ds(_chunk_rows(c, nc, direction), L)
        q_in = qin_ref[c]
        a = a_ref[c]
        g = g_ref[c][0:1]
        outs, states = [], []
        for h in range(HG_HEADS):
            ln = slice(h * HG_DIM, (h + 1) * HG_DIM)
            st = st_ref[0, 0, h]
            outs.append(_dot_nt(jnp.concatenate([a[:, ln], q_in[:, ln]], axis=1),
                                jnp.concatenate([vt_ref[c, h], st.astype(BF16)], axis=1)))
            states.append(g[:, ln] * st + u_ref[c, h])
        o_ref[0, 0, rows, :] = jnp.concatenate(outs, axis=1).astype(BF16)
        for h in range(HG_HEADS):
            st_ref[0, 0, h] = states[h]

    return _pipelined(nc // group, group, intra, carry_state)


N_HG = (7, 2, 7)
N_ML = (7, 3, 5)


def _mlstm_program(qk_ref, v_ref, g_ref, le2_ref, let2_ref, c0_ref, m0_ref,
                   o_ref, c_ref, mm_ref,
                   al_ref, u_ref, ct_ref, mloc_ref, sc_ref, *, nc):
    direction = pl.program_id(1)

    @pl.when(pl.program_id(2) == 0)
    def _():
        c_ref[...] = c0_ref[...]
        mm_ref[...] = m0_ref[...]

    let2 = let2_ref[0].astype(BF16)
    causal = le2_ref[0] > 0.0
    L = CHUNK
    neg = -jnp.inf
    lane = lax.broadcasted_iota(jnp.int32, (1, LANE), 1)
    seg = [lane < L, lane >= L]
    lane_t = lax.broadcasted_iota(jnp.int32, (L, LANE), 1) < L
    npair = ML_HEADS // 2
    ones_v = jnp.ones((L, ML_V), BF16)
    zb = jnp.zeros((L, LANE), BF16)

    def by_head(x):
        return jnp.concatenate([jnp.where(lane_t, x, zb), jnp.where(lane_t, zb, x)], axis=0)

    def v_stack(vb, p):
        return jnp.concatenate(
            [jnp.concatenate([vb[:, 2 * p * ML_V:(2 * p + 1) * ML_V], ones_v], axis=1),
             jnp.concatenate([vb[:, (2 * p + 1) * ML_V:(2 * p + 2) * ML_V], ones_v], axis=1)], axis=0)

    group = min(ML_GROUP, nc)

    def intra(grp):
        cs = [grp * group + i for i in range(group)]
        ccs = [jnp.where(direction == 0, c, nc - 1 - c) for c in cs]
        rows = [pl.ds(pl.multiple_of(cc * L, L), L) for cc in ccs]
        pairs = [(i, p) for i in range(group) for p in range(npair)]
        gs = [jnp.where(direction == 0, g_ref[0, cc][0:4], g_ref[0, cc][4:8]) for cc in ccs]
        pad = [jnp.zeros((-2 * group % SUBLANE, LANE), F32)] if 2 * group % SUBLANE else []
        i2 = jnp.concatenate([g[0:2] for g in gs] + pad, axis=0)
        lf2 = _log_sigmoid(jnp.concatenate([g[2:4] for g in gs] + pad, axis=0))
        parts = _split3(lf2)
        b2 = sum(_dot(x, let2) for x in parts)
        r2 = i2 - b2
        g_a = jnp.max(jnp.where(seg[0], r2, neg), axis=1, keepdims=True)
        g_b = jnp.max(jnp.where(seg[1], r2, neg), axis=1, keepdims=True)
        e_w = jnp.exp(r2 - jnp.where(seg[0], g_a, g_b))
        bt_a = jnp.sum(jnp.where(seg[0], lf2, 0.0), axis=1, keepdims=True)
        bt_b = jnp.sum(jnp.where(seg[1], lf2, 0.0), axis=1, keepdims=True)
        yield

        def as_columns(row):
            cols = jnp.broadcast_to(row, (LANE, LANE)).T
            return jnp.where(lane_t, cols[0:L], cols[L:2 * L])

        b_ts = [as_columns(b2[2 * i + p:2 * i + p + 1]) for i, p in pairs]
        for i in range(group):
            sc_ref[cs[i]] = jnp.concatenate(
                [jnp.broadcast_to(x[2 * i + p:2 * i + p + 1], (1, LANE))
                 for x2 in ((bt_a, bt_b), (g_a, g_b)) for p in range(npair) for x in x2], axis=0)
        yield
        qks = [qk_ref[0, r, :] for r in rows]
        vbs = [v_ref[0, r, :] for r in rows]
        q_ps = [qks[i][:, p * LANE:(p + 1) * LANE] for i, p in pairs]
        k_ps = [qks[i][:, (npair + p) * LANE:(npair + p + 1) * LANE] for i, p in pairs]
        kbs = [by_head(k) for k in k_ps]
        scores = [_dot_nt(q, kb) for q, kb in zip(q_ps, kbs)]
        yield
        for n, (i, p) in enumerate(pairs):
            t1 = b_ts[n]
            d_log = jnp.where(causal, t1 + r2[2 * i + p:2 * i + p + 1], neg)
            m_a = jnp.max(jnp.where(lane_t, d_log, neg), axis=1, keepdims=True)
            m_b = jnp.max(jnp.where(lane_t, neg, d_log), axis=1, keepdims=True)
            m_loc = jnp.where(lane_t, m_a, m_b)
            ct_ref[cs[i], p] = t1
            mloc_ref[cs[i], p] = m_loc
            al_ref[cs[i], p] = scores[n] * jnp.exp(d_log - m_loc)
        yield
        v_stacks = [v_stack(vbs[i], p) for i, p in pairs]
        yield
        kts = [(kb.astype(F32).T * e_w[2 * i + p:2 * i + p + 1]).astype(BF16) for kb, (i, p) in zip(kbs, pairs)]
        for n, (i, p) in enumerate(pairs):
            u_ref[cs[i], p] = _dot(kts[n], v_stacks[n])
        yield

    def carry_state(c):
        cc = jnp.where(direction == 0, c, nc - 1 - c)
        rows = pl.ds(pl.multiple_of(cc * L, L), L)
        m_all = mm_ref[0, 0]
        sc = sc_ref[c]
        qk = qk_ref[0, rows, :]
        vb = v_ref[0, rows, :]
        outs, states, m_rows = [], [], []
        for p in range(ML_HEADS // 2):
            c_pair = c_ref[0, 0, p]
            inc = u_ref[c, p]
            b_t = ct_ref[c, p]
            m_loc = mloc_ref[c, p]
            m_in2 = jnp.where(seg[0], m_all[2 * p:2 * p + 1], m_all[2 * p + 1:2 * p + 2])
            m_t = jnp.maximum(m_loc, b_t + m_in2)
            lhs_a = (jnp.exp(m_loc - m_t) * al_ref[c, p]).astype(BF16)
            lhs_q = (jnp.exp(b_t + m_in2 - m_t) * qk[:, p * LANE:(p + 1) * LANE].astype(F32)).astype(BF16)
            lhs = jnp.concatenate([by_head(lhs_a), by_head(lhs_q)], axis=1)
            tot = _dot(lhs, jnp.concatenate([v_stack(vb, p), c_pair.astype(BF16)], axis=0))
            floor = jnp.exp(-m_t)
            floor_r = pltpu.roll(floor, L, 1)
            new_rows = []
            for hh in range(2):
                h = 2 * p + hh
                rs = slice(hh * L, (hh + 1) * L)
                m_in = m_all[h:h + 1]
                floor_h = jnp.where(lane_t, floor, floor_r) if hh == 0 else jnp.where(lane_t, floor_r, floor)
                outs.append(tot[rs, :ML_V] / jnp.maximum(jnp.abs(tot[rs, ML_V:]), floor_h))
                b_tot = sc[h:h + 1]
                m_new = b_tot + jnp.maximum(m_in, sc[ML_HEADS + h:ML_HEADS + h + 1])
                carry_w = jnp.exp(b_tot + m_in - m_new)
                inc_w = jnp.exp(b_tot + sc[ML_HEADS + h:ML_HEADS + h + 1] - m_new)
                cw2 = jnp.concatenate([carry_w, carry_w], axis=1)
                iw2 = jnp.concatenate([inc_w, inc_w], axis=1)
                new_rows.append(cw2 * c_pair[rs] + iw2 * inc[rs])
                m_rows.append(m_new)
            states.append(jnp.concatenate(new_rows, axis=0))
        o_ref[0, 0, rows, :] = jnp.concatenate(outs, axis=1).astype(BF16)
        for p in range(ML_HEADS // 2):
            c_ref[0, 0, p] = states[p]
        mm_ref[0, 0] = jnp.concatenate(m_rows + [m_all[ML_HEADS:]], axis=0)

    return _pipelined(nc // group, group, intra, carry_state)


def _scan_kernel(*refs, nc):
    hg_in, ml_in, hg_out, ml_out, hg_scr, ml_scr = (
        refs[a:b] for a, b in zip(np.cumsum([0, N_HG[0], N_ML[0], N_HG[1], N_ML[1], N_HG[2]]),
                                  np.cumsum([N_HG[0], N_ML[0], N_HG[1], N_ML[1], N_HG[2], N_ML[2]])))
    _interleave([_hgrn2_program(*hg_in, *hg_out, *hg_scr, nc=nc),
                 _mlstm_program(*ml_in, *ml_out, *ml_scr, nc=nc)])


def _scans(hq, hv, hlf, qk, mv, gates, s0, c0, m0, dmat, pair_valid, wmask, le2, let2):
    bsz, t, hw = hq.shape
    nc = min(SCAN_CHUNKS, t // CHUNK)
    tb = nc * CHUNK
    nblk = t // tb
    npair = ML_HEADS // 2

    def blk(j, d):
        return jnp.where(d == 0, j, nblk - 1 - j)

    tok = lambda b, d, j: (b, blk(j, d), 0)
    tok_d = lambda b, d, j: (b, d, blk(j, d), 0)
    per_dir = lambda a: pl.BlockSpec((1,) + a.shape[1:], lambda b, d, j: (d,) + (0,) * (a.ndim - 1))
    state = lambda a: pl.BlockSpec((1, 1) + a.shape[2:], lambda b, d, j: (b, d) + (0,) * (a.ndim - 2))
    out_tok = pl.BlockSpec((1, 1, tb, hw), tok_d)
    hg_in = [pl.BlockSpec((1, tb, hw), tok), pl.BlockSpec((1, tb, hw), tok), pl.BlockSpec((1, 1, tb, hw), tok_d),
             per_dir(dmat), per_dir(pair_valid), per_dir(wmask), state(s0)]
    ml_in = [pl.BlockSpec((1, tb, qk.shape[-1]), tok), pl.BlockSpec((1, tb, hw), tok),
             pl.BlockSpec((1, nc) + gates.shape[2:], lambda b, d, j: (b, blk(j, d), 0, 0)),
             per_dir(le2), per_dir(let2), state(c0), state(m0)]
    sds = jax.ShapeDtypeStruct
    hg_scr = [pltpu.VMEM((nc,) + s0.shape[2:], F32),
              pltpu.VMEM((nc, CHUNK, hw), BF16),
              pltpu.VMEM((nc, SUBLANE, hw), F32),
              pltpu.VMEM((nc, CHUNK, hw), BF16),
              pltpu.VMEM((nc,) + s0.shape[2:], BF16),
              pltpu.VMEM((CHUNK, hw), F32),
              pltpu.VMEM((CHUNK, hw), F32)]
    ml_scr = [pltpu.VMEM((nc, npair, CHUNK, LANE), F32),
              pltpu.VMEM((nc, npair, 2 * ML_QK, 2 * ML_V), F32),
              pltpu.VMEM((nc, npair, CHUNK, LANE), F32),
              pltpu.VMEM((nc, npair, CHUNK, LANE), F32),
              pltpu.VMEM((nc, SUBLANE, LANE), F32)]
    assert (len(hg_in), 2, len(hg_scr)) == N_HG and (len(ml_in), 3, len(ml_scr)) == N_ML
    return pl.pallas_call(
        functools.partial(_scan_kernel, nc=nc),
        grid=(bsz, 2, nblk),
        in_specs=hg_in + ml_in,
        out_specs=[out_tok, state(s0), out_tok, state(c0), state(m0)],
        out_shape=[sds((bsz, 2, t, hw), BF16), sds(s0.shape, F32),
                   sds((bsz, 2, t, hw), BF16), sds(c0.shape, F32), sds(m0.shape, F32)],
        scratch_shapes=hg_scr + ml_scr,
        compiler_params=_cparams(("parallel", "parallel", "arbitrary")),
        name="scans",
    )(hq, hv, hlf, dmat, pair_valid, wmask, s0, qk, mv, gates, le2, let2, c0, m0)


def _head_rms(x, heads):
    dh = x.shape[-1] // heads
    return jnp.concatenate([_rms(x[:, h * dh:(h + 1) * dh]) for h in range(heads)], axis=1)


def _mix_ffn_kernel(oh_ref, om_ref, hg_ref, mo_ref, x_ref, mod_ref, hnw_ref, mnw_ref, wo_ref,
                    nw_ref, w1_ref, w2_ref, fw_ref, o_ref, a_ref, x_scr, h_scr, *, d_ff, tf):
    mix_gate = mod_ref[0, 0:1, :]
    mods = [mod_ref[0, k:k + 1, :] for k in range(1, 4)]

    def stages(rs):
        def mixed(pr):
            oh = oh_ref[0, 0, pr, :].astype(F32) + oh_ref[0, 1, pr, :].astype(F32)
            om = om_ref[0, 0, pr, :].astype(F32) + om_ref[0, 1, pr, :].astype(F32)
            hg_out = _head_rms(oh, HG_HEADS) * hnw_ref[...] * _silu(hg_ref[0, pr, :].astype(F32))
            ml_out = _sigmoid(mo_ref[0, pr, :].astype(F32)) * (_head_rms(om, ML_HEADS) * mnw_ref[...])
            merged = jnp.concatenate([hg_out, ml_out], axis=1).astype(BF16)
            return x_ref[0, pr, :] + mix_gate * _dot(merged, wo_ref[...])

        def put(y):
            o_ref[0, rs, :] = _rms(y) * fw_ref[...]

        return _Stages(_ffn_stages(mixed, put, mods, nw_ref, w1_ref, w2_ref, a_ref, x_scr, h_scr,
                                   rs, d_ff, tf), _ffn_lead(rs, d_ff, tf))

    _staggered([stages(rs) for rs in _sub_tiles(x_ref.shape[1])])


def _mix_ffn(oh, om, hg, mo, x, mods, hnw, mnw, w_out, nw, w1, w2, fw):
    bsz, t, d = x.shape
    hw = hg.shape[-1]
    d_ff = w2.shape[0]
    tm = min(TOKEN_TILE, t)
    tok = lambda b, i: (b, i, 0)
    tok2 = lambda b, i: (b, 0, i, 0)
    const = lambda b, i: (0, 0)
    resident = lambda a: pl.BlockSpec(a.shape, const, pipeline_mode=pl.Buffered(1))
    return pl.pallas_call(
        functools.partial(_mix_ffn_kernel, d_ff=d_ff, tf=FF_TILE),
        grid=(bsz, t // tm),
        in_specs=[pl.BlockSpec((1, 2, tm, hw), tok2), pl.BlockSpec((1, 2, tm, hw), tok2),
                  pl.BlockSpec((1, tm, hw), tok), pl.BlockSpec((1, tm, hw), tok),
                  pl.BlockSpec((1, tm, d), tok),
                  pl.BlockSpec((1, 4, d), lambda b, i: (b, 0, 0)),
                  pl.BlockSpec((1, hw), const), pl.BlockSpec((1, hw), const), resident(w_out),
                  pl.BlockSpec((1, d), const), resident(w1), resident(w2), pl.BlockSpec((1, d), const)],
        out_specs=pl.BlockSpec((1, tm, d), tok),
        out_shape=jax.ShapeDtypeStruct((bsz, t, d), F32),
        scratch_shapes=[pltpu.VMEM((tm, d_ff), BF16),
                        pltpu.VMEM((tm, d), F32),
                        pltpu.VMEM((tm, d), BF16)],
        compiler_params=_cparams(("parallel", "parallel")),
        name="mix_ffn_final",
    )(oh, om, hg, mo, x, mods, hnw, mnw, w_out, nw, w1, w2, fw)


def kernel(x, c, ctx, c_ctx, w_mod, b_mod, norm1_w, ffn1_w1, ffn1_w2, norm2_w, w_in, ml_gate_b,
           ml_conv_w, ml_conv_b, hg_lb_logits, hg_norm_w, ml_norm_w, w_out, norm3_w, ffn2_w1, ffn2_w2,
           final_norm_w):
    bsz, seq, d = x.shape
    n_ctx = ctx.shape[1]
    assert w_mod.shape[0] == 1, "single-layer kernel"
    assert seq % max(TOKEN_TILE, SCAN_CHUNKS * CHUNK) == 0 and n_ctx % CHUNK == 0 and GRID_W == CHUNK
    hw = HG_HEADS * HG_DIM
    ng = ml_gate_b.shape[-1]

    rows = -(-(bsz + 1) // SUBLANE) * SUBLANE
    cvec = jnp.zeros((rows, d), F32).at[:bsz].set(c).at[bsz].set(c_ctx)
    mods = _modulation(cvec, w_mod[0], b_mod[0][None, :]).reshape(rows, N_MOD, d)
    lat_row = lambda b: b
    ctx_row = lambda b: bsz

    row = lambda a: a.reshape(1, -1)
    w1a, w2a = ffn1_w1[0], ffn1_w2[0]
    x1 = _ffn(x, mods[:, 0:3], lat_row, row(norm1_w[0]), w1a, w2a)
    s1 = _ffn(ctx, mods[:, 0:3], ctx_row, row(norm1_w[0]), w1a, w2a)

    w_in_b = w_in[0].T
    gate_order = np.arange(ng).reshape(2, 2, ML_HEADS // 2, 2).transpose(3, 0, 1, 2).reshape(-1)
    w_gate_t = w_in[0][:, 8 * hw:].T[gate_order].astype(BF16)
    gate_b = ml_gate_b[0][gate_order].reshape(ng, 1)
    dmat, pair_valid, wmask = (jnp.asarray(a) for a in _hgrn2_masks())
    le2, let2 = (jnp.asarray(a) for a in _mlstm_masks())
    conv_w, conv_b = ml_conv_w[0], row(ml_conv_b[0])
    npair = ML_HEADS // 2

    def mixer_scans(s, mod_row, width, s0, c0, m0):
        hq, hv, hg, hlf, mqk, mv, mo, gates = _inproj(
            s, mods[:, 3:5], mod_row, row(norm2_w[0]), w_in_b, w_gate_t, gate_b, hg_lb_logits)
        qk = _conv(mqk, conv_w, conv_b, width)
        oh, s_fin, om, c_fin, m_fin = _scans(hq, hv, hlf, qk, mv, gates, s0, c0, m0,
                                             dmat, pair_valid, wmask, le2, let2)
        return oh, om, hg, mo, s_fin, c_fin, m_fin

    s0 = jnp.zeros((bsz, 2, HG_HEADS, HG_DIM, HG_DIM), F32)
    c0 = jnp.zeros((bsz, 2, npair, 2 * ML_QK, 2 * ML_V), F32)
    m0 = jnp.zeros((bsz, 2, SUBLANE, LANE), F32)
    _, _, _, _, s_ctx, c_ctx_state, m_ctx = mixer_scans(s1, ctx_row, n_ctx, s0, c0, m0)
    oh, om, hg, mo, _, _, _ = mixer_scans(x1, lat_row, GRID_W, s_ctx, c_ctx_state, m_ctx)

    return _mix_ffn(oh, om, hg, mo, x1, mods[:, 5:9], row(hg_norm_w[0]), row(ml_norm_w[0]),
                    w_out[0].astype(BF16), row(norm3_w[0]), ffn2_w1[0], ffn2_w2[0], row(final_norm_w))
```

```python
import functools

import numpy as np
import jax
import jax.numpy as jnp
from jax import lax
from jax.experimental import pallas as pl
from jax.experimental.pallas import tpu as pltpu

F32 = jnp.float32
BF16 = jnp.bfloat16

EPS = 1e-6
CHUNK = 64
GRID_W = 64
N_MOD = 9
HG_HEADS = 4
HG_DIM = 128
ML_HEADS = 4
ML_QK = 64
ML_V = 128
SUB = 16
N_SUB = CHUNK // SUB
GROUP = 4
ML_GROUP = 4
SCAN_CHUNKS = 16
TOKEN_TILE = 512
ROW_SUB = 256
FF_TILE = 256
PIECE = 128
LOG2E = 1.4426950408889634
EXP2_CLAMP = 115.0
LANE = 128
SUBLANE = 8
VMEM_LIMIT = 56 * 1024 * 1024


def _cparams(sem):
    return pltpu.CompilerParams(dimension_semantics=sem, vmem_limit_bytes=VMEM_LIMIT)


def _dot(a, b):
    return jnp.dot(a, b, preferred_element_type=F32)


def _dot_nt(a, b):
    return lax.dot_general(a, b, (((1,), (1,)), ((), ())), preferred_element_type=F32)


def _dot_tn(a, b):
    return lax.dot_general(a, b, (((0,), (0,)), ((), ())), preferred_element_type=F32)


def _sigmoid(x):
    return 1.0 / (1.0 + jnp.exp(-x))


def _silu(x):
    return x * _sigmoid(x)


def _log_sigmoid(x):
    return jnp.minimum(x, 0.0) - jnp.log(1.0 + jnp.exp(-jnp.abs(x)))


def _split3(x):
    hi = x.astype(BF16)
    r = x - hi.astype(F32)
    mid = r.astype(BF16)
    lo = (r - mid.astype(F32)).astype(BF16)
    return hi, mid, lo


def _split2(x):
    hi = x.astype(BF16)
    return hi, (x - hi.astype(F32)).astype(BF16)


def _rms(x):
    return x * lax.rsqrt(jnp.mean(x * x, axis=-1, keepdims=True) + EPS)


def _mod_kernel(c_ref, w_ref, b_ref, o_ref):
    a = _silu(c_ref[...])
    a_hi = a.astype(BF16)
    a_lo = (a - a_hi.astype(F32)).astype(BF16)
    w = w_ref[...]
    w_hi = w.astype(BF16)
    w_lo = (w - w_hi.astype(F32)).astype(BF16)
    o_ref[...] = _dot(a_hi, w_hi) + _dot(a_hi, w_lo) + _dot(a_lo, w_hi) + b_ref[...]


def _modulation(cvec, w_mod, b_mod):
    rows, d = cvec.shape
    n = w_mod.shape[1]
    tn = 1024
    return pl.pallas_call(
        _mod_kernel,
        grid=(n // tn,),
        in_specs=[pl.BlockSpec((rows, d), lambda j: (0, 0)),
                  pl.BlockSpec((d, tn), lambda j: (0, j)),
                  pl.BlockSpec((1, tn), lambda j: (0, j))],
        out_specs=pl.BlockSpec((rows, tn), lambda j: (0, j)),
        out_shape=jax.ShapeDtypeStruct((rows, n), F32),
        compiler_params=_cparams(("arbitrary",)),
        name="modulation",
    )(cvec, w_mod, b_mod)


def _staggered(tiles):
    for _ in range(tiles[0].lead):
        next(tiles[0].gen)
    live = [t.gen for t in tiles]
    while live:
        live = [g for g in live if next(g, None) is not None]


class _Stages:
    def __init__(self, gen, lead):
        self.gen, self.lead = gen, lead


def _ffn_stages(get_x, put_y, mods, nw_ref, w1_ref, w2_ref, a_ref, x_scr, h_scr, rs, d_ff, tf):
    shift, scale, gate = mods
    for p0 in range(rs.start, rs.stop, PIECE):
        pr = slice(p0, min(p0 + PIECE, rs.stop))
        x = get_x(pr)
        x_scr[pr, :] = x
        h_scr[pr, :] = (_rms(x) * nw_ref[...] * (1.0 + scale) + shift).astype(BF16)
        yield True
    h = h_scr[rs, :]
    for c in range(d_ff // tf):
        g = _dot(h, w1_ref[:, c * tf:(c + 1) * tf].astype(BF16))
        u = _dot(h, w1_ref[:, d_ff + c * tf:d_ff + (c + 1) * tf].astype(BF16))
        a_ref[rs, c * tf:(c + 1) * tf] = (_silu(g) * u).astype(BF16)
        yield True
    y = x_scr[rs, :] + 0.5 * gate * _dot(a_ref[rs, :], w2_ref[...].astype(BF16))
    yield True
    put_y(y)
    yield True


def _sub_tiles(rows):
    sub = min(ROW_SUB, rows)
    return [slice(r0, r0 + sub) for r0 in range(0, rows, sub)]


def _ffn_lead(rs, d_ff, tf):
    return -(-(rs.stop - rs.start) // PIECE) + d_ff // tf // 2


def _ffn_kernel(x_ref, mod_ref, nw_ref, w1_ref, w2_ref, o_ref, a_ref, x_scr, h_scr, *, d_ff, tf):
    mods = [mod_ref[0, k:k + 1, :] for k in range(3)]

    def stages(rs):
        def put(y):
            o_ref[0, rs, :] = y
        return _Stages(_ffn_stages(lambda pr: x_ref[0, pr, :], put, mods, nw_ref, w1_ref, w2_ref, a_ref,
                                   x_scr, h_scr, rs, d_ff, tf), _ffn_lead(rs, d_ff, tf))

    _staggered([stages(rs) for rs in _sub_tiles(x_ref.shape[1])])


def _ffn(s, mods, mod_row, nw, w1, w2):
    bsz, t, d = s.shape
    d_ff = w2.shape[0]
    tm = min(TOKEN_TILE, t)
    const = lambda b, i: (0, 0)
    return pl.pallas_call(
        functools.partial(_ffn_kernel, d_ff=d_ff, tf=FF_TILE),
        grid=(bsz, t // tm),
        in_specs=[pl.BlockSpec((1, tm, d), lambda b, i: (b, i, 0)),
                  pl.BlockSpec((1, 3, d), lambda b, i: (mod_row(b), 0, 0)),
                  pl.BlockSpec((1, d), const),
                  pl.BlockSpec((d, 2 * d_ff), const, pipeline_mode=pl.Buffered(1)),
                  pl.BlockSpec((d_ff, d), const, pipeline_mode=pl.Buffered(1))],
        out_specs=pl.BlockSpec((1, tm, d), lambda b, i: (b, i, 0)),
        out_shape=jax.ShapeDtypeStruct((bsz, t, d), F32),
        scratch_shapes=[pltpu.VMEM((tm, d_ff), BF16),
                        pltpu.VMEM((tm, d), F32),
                        pltpu.VMEM((tm, d), BF16)],
        compiler_params=_cparams(("parallel", "parallel")),
        name="ffn",
    )(s, mods, nw, w1, w2)


def _inproj_kernel(x_ref, mod_ref, nw_ref, w_ref, wg_ref, gb_ref, lbl_ref,
                   hq_ref, hv_ref, hg_ref, hlf_ref, mqk_ref, mv_ref, mo_ref, gc_ref, *, hw):
    shift = mod_ref[0, 0:1, :]
    scale = mod_ref[0, 1:2, :]
    lbl = lbl_ref[...]
    e = jnp.exp(lbl - jnp.max(lbl, axis=0, keepdims=True))
    lb = e[0] / jnp.sum(e, axis=0)
    first_half = lax.broadcasted_iota(jnp.int32, (SUBLANE, LANE), 1) < CHUNK
    def stages(rs):
        h = (_rms(x_ref[0, rs, :]) * nw_ref[...] * (1.0 + scale) + shift).astype(BF16)
        yield True

        def proj(k):
            return _dot_nt(h, w_ref[k * hw:(k + 1) * hw, :].astype(BF16))

        hq_ref[0, rs, :] = _silu(proj(0)) * (HG_DIM ** -0.5)
        yield True
        hv_ref[0, rs, :] = proj(1).astype(BF16)
        yield True
        hg_ref[0, rs, :] = proj(2).astype(BF16)
        yield True
        for d in range(2):
            lbd = lb[d:d + 1, :]
            hlf_ref[0, d, rs, :] = jnp.log(lbd + (1.0 - lbd) * _sigmoid(proj(3 + d))) * LOG2E
            yield True
        mqk_ref[0, rs, :] = proj(5).astype(BF16)
        yield True
        mv_ref[0, rs, :] = proj(6).astype(BF16)
        yield True
        mo_ref[0, rs, :] = proj(7).astype(BF16)
        gt = _dot_nt(wg_ref[...], h) + gb_ref[...]
        for m in range((rs.stop - rs.start) // LANE):
            top = gt[0:8, m * LANE:(m + 1) * LANE]
            bot = gt[8:16, m * LANE:(m + 1) * LANE]
            c0 = rs.start // CHUNK + 2 * m
            gc_ref[0, c0] = jnp.where(first_half, top, pltpu.roll(bot, CHUNK, 1))
            gc_ref[0, c0 + 1] = jnp.where(first_half, pltpu.roll(top, CHUNK, 1), bot)
        yield True

    _staggered([_Stages(stages(rs), 8) for rs in _sub_tiles(x_ref.shape[1])])


def _inproj(s, mods, mod_row, nw, w_in, w_gate_t, gate_b, lb_logits):
    bsz, t, d = s.shape
    hw = HG_HEADS * HG_DIM
    ng = gate_b.shape[0]
    tm = min(TOKEN_TILE, t)
    tok4 = lambda b, i: (b, i, 0, 0)
    const = lambda b, i: (0, 0)
    tok = lambda b, i: (b, i, 0)
    tok2 = lambda b, i: (b, 0, i, 0)
    f = lambda dt, *shape: jax.ShapeDtypeStruct(shape, dt)
    return pl.pallas_call(
        functools.partial(_inproj_kernel, hw=hw),
        grid=(bsz, t // tm),
        in_specs=[pl.BlockSpec((1, tm, d), tok),
                  pl.BlockSpec((1, 2, d), lambda b, i: (mod_row(b), 0, 0)),
                  pl.BlockSpec((1, d), const),
                  pl.BlockSpec(w_in.shape, const, pipeline_mode=pl.Buffered(1)),
                  pl.BlockSpec(w_gate_t.shape, const),
                  pl.BlockSpec((ng, 1), const),
                  pl.BlockSpec(lb_logits.shape, lambda b, i: (0, 0, 0))],
        out_specs=[pl.BlockSpec((1, tm, hw), tok), pl.BlockSpec((1, tm, hw), tok),
                   pl.BlockSpec((1, tm, hw), tok),
                   pl.BlockSpec((1, 2, tm, hw), tok2),
                   pl.BlockSpec((1, tm, hw), tok), pl.BlockSpec((1, tm, hw), tok),
                   pl.BlockSpec((1, tm, hw), tok), pl.BlockSpec((1, tm // CHUNK, ng // 2, LANE), tok4)],
        out_shape=[f(F32, bsz, t, hw), f(BF16, bsz, t, hw), f(BF16, bsz, t, hw),
                   f(F32, bsz, 2, t, hw),
                   f(BF16, bsz, t, hw), f(BF16, bsz, t, hw), f(BF16, bsz, t, hw),
                   f(F32, bsz, t // CHUNK, ng // 2, LANE)],
        compiler_params=_cparams(("parallel", "parallel")),
        name="inproj",
    )(s, mods, nw, w_in, w_gate_t, gate_b, lb_logits)


def _conv_kernel(x_ref, w_ref, b_ref, o_ref, *, width, q_tiles):
    x = x_ref[0].astype(F32)
    t = x.shape[0]
    col = lax.broadcasted_iota(jnp.int32, x.shape, 0) % width
    xl = jnp.where(col == 0, 0.0, pltpu.roll(x, 1, 0))
    xr = jnp.where(col == width - 1, 0.0, pltpu.roll(x, t - 1, 0))

    def row(di):
        return (w_ref[di, 0:1, :] * xl + w_ref[di, 1:2, :] * x + w_ref[di, 2:3, :] * xr)

    y = row(1)
    if t > width:
        pad = jnp.zeros((width, x.shape[1]), F32)
        y = y + jnp.concatenate([pad, row(0)[:t - width]], axis=0)
        y = y + jnp.concatenate([row(2)[width:], pad], axis=0)
    y = _silu(y + b_ref[...])
    qscale = jnp.where(pl.program_id(1) < q_tiles, ML_QK ** -0.5, 1.0)
    o_ref[0] = (y * qscale).astype(BF16)


def _conv(x, w, b, width):
    bsz, t, ch = x.shape
    return pl.pallas_call(
        functools.partial(_conv_kernel, width=width, q_tiles=ML_HEADS * ML_QK // LANE),
        grid=(bsz, ch // LANE),
        in_specs=[pl.BlockSpec((1, t, LANE), lambda b_, c: (b_, 0, c)),
                  pl.BlockSpec((3, 3, LANE), lambda b_, c: (0, 0, c)),
                  pl.BlockSpec((1, LANE), lambda b_, c: (0, c))],
        out_specs=pl.BlockSpec((1, t, LANE), lambda b_, c: (b_, 0, c)),
        out_shape=jax.ShapeDtypeStruct((bsz, t, ch), BF16),
        compiler_params=_cparams(("parallel", "parallel")),
        name="conv",
    )(x, w, b)


def _positions():
    t = np.arange(CHUNK)
    return [t, CHUNK - 1 - t]


def _hgrn2_masks():
    ds, vs, ws = [], [], []
    for p in _positions():
        le = (p[None, :] <= p[:, None]).astype(np.float32)
        order = [p[SUB * j] // SUB for j in range(N_SUB)]
        ref = np.zeros((32, CHUNK), np.float32)
        valid = np.zeros((N_SUB * N_SUB, 1), np.float32)
        for j in range(N_SUB):
            ref[j] = p <= SUB * order[j] + SUB // 2 - 1
            ref[8 + j] = p <= SUB * order[j] + SUB - 1
        ref[N_SUB] = 1.0
        for j in range(N_SUB):
            for i in range(N_SUB):
                if order[i] < order[j]:
                    ref[16 + N_SUB * j + i] = ref[j] - ref[8 + i]
                    valid[N_SUB * j + i] = 1.0
        ds.append(np.concatenate([le, ref], axis=0))
        vs.append(np.broadcast_to(valid, (N_SUB * N_SUB, HG_HEADS * HG_DIM)))
        tok_blk = np.arange(CHUNK)[:, None] // SUB
        ws.append(np.concatenate([(tok_blk == j) * le for j in range(N_SUB)], axis=1))
    return (np.stack(ds).astype(np.float32), np.stack(vs).astype(np.float32),
            np.stack(ws).astype(np.float32))


def _mlstm_masks():
    le2s, let2s = [], []
    for p in _positions():
        le = (p[None, :] <= p[:, None]).astype(np.float32)
        le2s.append(np.concatenate([le, le], axis=1))
        let2 = np.zeros((2 * CHUNK, 2 * CHUNK), np.float32)
        let2[:CHUNK, :CHUNK] = le.T
        let2[CHUNK:, CHUNK:] = le.T
        let2s.append(let2)
    return np.stack(le2s), np.stack(let2s)


def _pipelined(n_groups, group, stages_of, step_of):
    queue = []
    for g in range(n_groups):
        for _ in stages_of(g):
            if queue:
                queue.pop(0)()
            yield True
        while queue:
            queue.pop(0)()
            yield True
        queue = [functools.partial(step_of, g * group + i) for i in range(group)]
    for step in queue:
        step()
        yield True


def _interleave(programs):
    live = list(programs)
    while live:
        live = [p for p in live if next(p, None) is not None]


def _chunk_rows(c, nc, direction):
    cc = jnp.where(direction == 0, c, nc - 1 - c)
    return pl.multiple_of(cc * CHUNK, CHUNK)


def _hgrn2_program(q_ref, v_ref, lf_ref, d_ref, pv_ref, w_ref, s0_ref, o_ref, st_ref,
                   u_ref, qin_ref, g_ref, a_ref, vt_ref, bx_ref, kx_ref, *, nc):
    direction = pl.program_id(1)

    @pl.when(pl.program_id(2) == 0)
    def _():
        st_ref[...] = s0_ref[...]

    dmat = d_ref[0].astype(BF16)
    pair_valid = pv_ref[0]
    wmask = w_ref[0] > 0.0
    L = CHUNK
    hw = q_ref.shape[-1]
    upper_rows = lax.broadcasted_iota(jnp.int32, (L, 2 * L), 0) < L // 2

    group = min(GROUP, nc)

    def intra(grp):
        cs = [grp * group + i for i in range(group)]
        rows = [pl.ds(_chunk_rows(c, nc, direction), L) for c in cs]
        items = [(i, h) for i in range(group) for h in range(HG_HEADS)]
        lanes = [slice(h * HG_DIM, (h + 1) * HG_DIM) for h in range(HG_HEADS)]
        lfs = [lf_ref[0, 0, r, :] for r in rows]
        parts = [_split2(lf) for lf in lfs]
        e_all = sum(_dot(dmat, jnp.concatenate([parts[i][n] for i in range(group)], axis=1))
                    for n in range(2))
        yield
        q_mids, k_mids, k_outs, vbs, spans = [], [], [], [], []
        for i in range(group):
            e = e_all[:, i * hw:(i + 1) * hw]
            q = q_ref[0, rows[i], :]
            k = 1.0 - jnp.exp2(lfs[i])
            b = e[0:L]
            b_tot = e[L + N_SUB:L + N_SUB + 1]
            sub_rows = lambda r0: jnp.concatenate(
                [jnp.broadcast_to(e[r0 + j:r0 + j + 1], (SUB, hw)) for j in range(N_SUB)], axis=0)
            mid_rows = sub_rows(L)
            end_rows = sub_rows(L + 8)
            to_mid = b - mid_rows
            spans.append(jnp.max(jnp.abs(to_mid)))
            q_mids.append((q * jnp.exp2(jnp.minimum(to_mid, EXP2_CLAMP))).astype(BF16))
            k_diag = k * jnp.exp2(jnp.minimum(-to_mid, EXP2_CLAMP))
            k_end = k * jnp.exp2(end_rows - b)
            cross = jnp.exp2(e[L + 16:L + 32]) * pair_valid
            slabs = []
            for j in range(N_SUB):
                for ib in range(N_SUB):
                    rs = slice(ib * SUB, (ib + 1) * SUB)
                    row = N_SUB * j + ib
                    slabs.append(k_diag[rs] if ib == j else k_end[rs] * cross[row:row + 1])
            k_mids.append(jnp.concatenate(slabs, axis=0).astype(BF16))
            k_outs.append((k * jnp.exp2(b_tot - b)).astype(BF16))
            vbs.append(v_ref[0, rows[i], :])
            qin_ref[cs[i]] = (q * jnp.exp2(b)).astype(BF16)
            g_ref[cs[i]] = jnp.broadcast_to(jnp.exp2(b_tot), (SUBLANE, hw))
            yield
        scores = [_dot_nt(q_mids[i][:, lanes[h]], k_mids[i][:, lanes[h]]) for i, h in items]
        yield
        a_s = [jnp.where(wmask, s, 0.0) for s in scores]
        a_s = [jnp.where(upper_rows, a[:, :2 * L], a[:, 2 * L:]).astype(BF16) for a in a_s]
        for i in range(group):
            a_ref[cs[i]] = jnp.concatenate(a_s[i * HG_HEADS:(i + 1) * HG_HEADS], axis=1)
        yield
        for i, h in items:
            v_h = vbs[i][:, lanes[h]].astype(F32)
            vt_ref[cs[i], h] = jnp.concatenate([v_h, v_h], axis=0).T.astype(BF16)
        yield
        for i, h in items:
            u_ref[cs[i], h] = _dot_tn(vbs[i][:, lanes[h]], k_outs[i][:, lanes[h]])

        @pl.when(functools.reduce(jnp.maximum, spans) > EXP2_CLAMP)
        def _():
            for i in range(group):
                exact_intra(cs[i], rows[i])
        yield

    def exact_intra(c, rows):
        lf = lf_ref[0, 0, rows, :]
        bx_ref[...] = sum(_dot(dmat[0:L], x) for x in _split3(lf))
        kx_ref[...] = 1.0 - jnp.exp2(lf)
        q = q_ref[0, rows, :]
        tok = lax.broadcasted_iota(jnp.int32, (L, 1), 0)
        pos = jnp.where(direction == 0, tok, L - 1 - tok)
        lane_in_head = lax.broadcasted_iota(jnp.int32, (1, hw), 1) % HG_DIM

        def add_source(s, acc):
            b_s = bx_ref[pl.ds(s, 1), :]
            z = q * kx_ref[pl.ds(s, 1), :] * jnp.exp2(jnp.minimum(bx_ref[...] - b_s, 0.0))
            seen = pos >= jnp.where(direction == 0, s, L - 1 - s)
            cols = [jnp.broadcast_to(jnp.where(seen, jnp.sum(z[:, ln], axis=1, keepdims=True), 0.0), (L, HG_DIM))
                    for ln in (slice(h * HG_DIM, (h + 1) * HG_DIM) for h in range(HG_HEADS))]
            return jnp.where(lane_in_head == s, jnp.concatenate(cols, axis=1), acc)

        a_ref[c] = lax.fori_loop(0, L, add_source, jnp.zeros((L, hw), F32)).astype(BF16)

    def carry_state(c):
        rows = pl.ds(_chunk_rows(c, nc, direction), L)
        q_in = qin_ref[c]
        a = a_ref[c]
        g = g_ref[c][0:1]
        outs, states = [], []
        for h in range(HG_HEADS):
            ln = slice(h * HG_DIM, (h + 1) * HG_DIM)
            st = st_ref[0, 0, h]
            outs.append(_dot_nt(jnp.concatenate([a[:, ln], q_in[:, ln]], axis=1),
                                jnp.concatenate([vt_ref[c, h], st.astype(BF16)], axis=1)))
            states.append(g[:, ln] * st + u_ref[c, h])
        o_ref[0, 0, rows, :] = jnp.concatenate(outs, axis=1).astype(BF16)
        for h in range(HG_HEADS):
            st_ref[0, 0, h] = states[h]

    return _pipelined(nc // group, group, intra, carry_state)


N_HG = (7, 2, 7)
N_ML = (7, 3, 5)


def _mlstm_program(qk_ref, v_ref, g_ref, le2_ref, let2_ref, c0_ref, m0_ref,
                   o_ref, c_ref, mm_ref,
                   al_ref, u_ref, ct_ref, mloc_ref, sc_ref, *, nc):
    direction = pl.program_id(1)

    @pl.when(pl.program_id(2) == 0)
    def _():
        c_ref[...] = c0_ref[...]
        mm_ref[...] = m0_ref[...]

    let2 = let2_ref[0].astype(BF16)
    causal = le2_ref[0] > 0.0
    L = CHUNK
    neg = -jnp.inf
    lane = lax.broadcasted_iota(jnp.int32, (1, LANE), 1)
    seg = [lane < L, lane >= L]
    lane_t = lax.broadcasted_iota(jnp.int32, (L, LANE), 1) < L
    npair = ML_HEADS // 2
    ones_v = jnp.ones((L, ML_V), BF16)
    zb = jnp.zeros((L, LANE), BF16)

    def by_head(x):
        return jnp.concatenate([jnp.where(lane_t, x, zb), jnp.where(lane_t, zb, x)], axis=0)

    def v_stack(vb, p):
        return jnp.concatenate(
            [jnp.concatenate([vb[:, 2 * p * ML_V:(2 * p + 1) * ML_V], ones_v], axis=1),
             jnp.concatenate([vb[:, (2 * p + 1) * ML_V:(2 * p + 2) * ML_V], ones_v], axis=1)], axis=0)

    group = min(ML_GROUP, nc)

    def intra(grp):
        cs = [grp * group + i for i in range(group)]
        ccs = [jnp.where(direction == 0, c, nc - 1 - c) for c in cs]
        rows = [pl.ds(pl.multiple_of(cc * L, L), L) for cc in ccs]
        pairs = [(i, p) for i in range(group) for p in range(npair)]
        gs = [jnp.where(direction == 0, g_ref[0, cc][0:4], g_ref[0, cc][4:8]) for cc in ccs]
        pad = [jnp.zeros((-2 * group % SUBLANE, LANE), F32)] if 2 * group % SUBLANE else []
        i2 = jnp.concatenate([g[0:2] for g in gs] + pad, axis=0)
        lf2 = _log_sigmoid(jnp.concatenate([g[2:4] for g in gs] + pad, axis=0))
        parts = _split3(lf2)
        b2 = sum(_dot(x, let2) for x in parts)
        r2 = i2 - b2
        g_a = jnp.max(jnp.where(seg[0], r2, neg), axis=1, keepdims=True)
        g_b = jnp.max(jnp.where(seg[1], r2, neg), axis=1, keepdims=True)
        e_w = jnp.exp(r2 - jnp.where(seg[0], g_a, g_b))
        bt_a = jnp.sum(jnp.where(seg[0], lf2, 0.0), axis=1, keepdims=True)
        bt_b = jnp.sum(jnp.where(seg[1], lf2, 0.0), axis=1, keepdims=True)
        yield

        def as_columns(row):
            cols = jnp.broadcast_to(row, (LANE, LANE)).T
            return jnp.where(lane_t, cols[0:L], cols[L:2 * L])

        b_ts = [as_columns(b2[2 * i + p:2 * i + p + 1]) for i, p in pairs]
        for i in range(group):
            sc_ref[cs[i]] = jnp.concatenate(
                [jnp.broadcast_to(x[2 * i + p:2 * i + p + 1], (1, LANE))
                 for x2 in ((bt_a, bt_b), (g_a, g_b)) for p in range(npair) for x in x2], axis=0)
        yield
        qks = [qk_ref[0, r, :] for r in rows]
        vbs = [v_ref[0, r, :] for r in rows]
        q_ps = [qks[i][:, p * LANE:(p + 1) * LANE] for i, p in pairs]
        k_ps = [qks[i][:, (npair + p) * LANE:(npair + p + 1) * LANE] for i, p in pairs]
        kbs = [by_head(k) for k in k_ps]
        scores = [_dot_nt(q, kb) for q, kb in zip(q_ps, kbs)]
        yield
        for n, (i, p) in enumerate(pairs):
            t1 = b_ts[n]
            d_log = jnp.where(causal, t1 + r2[2 * i + p:2 * i + p + 1], neg)
            m_a = jnp.max(jnp.where(lane_t, d_log, neg), axis=1, keepdims=True)
            m_b = jnp.max(jnp.where(lane_t, neg, d_log), axis=1, keepdims=True)
            m_loc = jnp.where(lane_t, m_a, m_b)
            ct_ref[cs[i], p] = t1
            mloc_ref[cs[i], p] = m_loc
            al_ref[cs[i], p] = scores[n] * jnp.exp(d_log - m_loc)
        yield
        v_stacks = [v_stack(vbs[i], p) for i, p in pairs]
        yield
        kts = [(kb.astype(F32).T * e_w[2 * i + p:2 * i + p + 1]).astype(BF16) for kb, (i, p) in zip(kbs, pairs)]
        for n, (i, p) in enumerate(pairs):
            u_ref[cs[i], p] = _dot(kts[n], v_stacks[n])
        yield

    def carry_state(c):
        cc = jnp.where(direction == 0, c, nc - 1 - c)
        rows = pl.ds(pl.multiple_of(cc * L, L), L)
        m_all = mm_ref[0, 0]
        sc = sc_ref[c]
        qk = qk_ref[0, rows, :]
        vb = v_ref[0, rows, :]
        outs, states, m_rows = [], [], []
        for p in range(ML_HEADS // 2):
            c_pair = c_ref[0, 0, p]
            inc = u_ref[c, p]
            b_t = ct_ref[c, p]
            m_loc = mloc_ref[c, p]
            m_in2 = jnp.where(seg[0], m_all[2 * p:2 * p + 1], m_all[2 * p + 1:2 * p + 2])
            m_t = jnp.maximum(m_loc, b_t + m_in2)
            lhs_a = (jnp.exp(m_loc - m_t) * al_ref[c, p]).astype(BF16)
            lhs_q = (jnp.exp(b_t + m_in2 - m_t) * qk[:, p * LANE:(p + 1) * LANE].astype(F32)).astype(BF16)
            lhs = jnp.concatenate([by_head(lhs_a), by_head(lhs_q)], axis=1)
            tot = _dot(lhs, jnp.concatenate([v_stack(vb, p), c_pair.astype(BF16)], axis=0))
            floor = jnp.exp(-m_t)
            floor_r = pltpu.roll(floor, L, 1)
            new_rows = []
            for hh in range(2):
                h = 2 * p + hh
                rs = slice(hh * L, (hh + 1) * L)
                m_in = m_all[h:h + 1]
                floor_h = jnp.where(lane_t, floor, floor_r) if hh == 0 else jnp.where(lane_t, floor_r, floor)
                outs.append(tot[rs, :ML_V] / jnp.maximum(jnp.abs(tot[rs, ML_V:]), floor_h))
                b_tot = sc[h:h + 1]
                m_new = b_tot + jnp.maximum(m_in, sc[ML_HEADS + h:ML_HEADS + h + 1])
                carry_w = jnp.exp(b_tot + m_in - m_new)
                inc_w = jnp.exp(b_tot + sc[ML_HEADS + h:ML_HEADS + h + 1] - m_new)
                cw2 = jnp.concatenate([carry_w, carry_w], axis=1)
                iw2 = jnp.concatenate([inc_w, inc_w], axis=1)
                new_rows.append(cw2 * c_pair[rs] + iw2 * inc[rs])
                m_rows.append(m_new)
            states.append(jnp.concatenate(new_rows, axis=0))
        o_ref[0, 0, rows, :] = jnp.concatenate(outs, axis=1).astype(BF16)
        for p in range(ML_HEADS // 2):
            c_ref[0, 0, p] = states[p]
        mm_ref[0, 0] = jnp.concatenate(m_rows + [m_all[ML_HEADS:]], axis=0)

    return _pipelined(nc // group, group, intra, carry_state)


def _scan_kernel(*refs, nc):
    hg_in, ml_in, hg_out, ml_out, hg_scr, ml_scr = (
        refs[a:b] for a, b in zip(np.cumsum([0, N_HG[0], N_ML[0], N_HG[1], N_ML[1], N_HG[2]]),
                                  np.cumsum([N_HG[0], N_ML[0], N_HG[1], N_ML[1], N_HG[2], N_ML[2]])))
    _interleave([_hgrn2_program(*hg_in, *hg_out, *hg_scr, nc=nc),
                 _mlstm_program(*ml_in, *ml_out, *ml_scr, nc=nc)])


def _scans(hq, hv, hlf, qk, mv, gates, s0, c0, m0, dmat, pair_valid, wmask, le2, let2):
    bsz, t, hw = hq.shape
    nc = min(SCAN_CHUNKS, t // CHUNK)
    tb = nc * CHUNK
    nblk = t // tb
    npair = ML_HEADS // 2

    def blk(j, d):
        return jnp.where(d == 0, j, nblk - 1 - j)

    tok = lambda b, d, j: (b, blk(j, d), 0)
    tok_d = lambda b, d, j: (b, d, blk(j, d), 0)
    per_dir = lambda a: pl.BlockSpec((1,) + a.shape[1:], lambda b, d, j: (d,) + (0,) * (a.ndim - 1))
    state = lambda a: pl.BlockSpec((1, 1) + a.shape[2:], lambda b, d, j: (b, d) + (0,) * (a.ndim - 2))
    out_tok = pl.BlockSpec((1, 1, tb, hw), tok_d)
    hg_in = [pl.BlockSpec((1, tb, hw), tok), pl.BlockSpec((1, tb, hw), tok), pl.BlockSpec((1, 1, tb, hw), tok_d),
             per_dir(dmat), per_dir(pair_valid), per_dir(wmask), state(s0)]
    ml_in = [pl.BlockSpec((1, tb, qk.shape[-1]), tok), pl.BlockSpec((1, tb, hw), tok),
             pl.BlockSpec((1, nc) + gates.shape[2:], lambda b, d, j: (b, blk(j, d), 0, 0)),
             per_dir(le2), per_dir(let2), state(c0), state(m0)]
    sds = jax.ShapeDtypeStruct
    hg_scr = [pltpu.VMEM((nc,) + s0.shape[2:], F32),
              pltpu.VMEM((nc, CHUNK, hw), BF16),
              pltpu.VMEM((nc, SUBLANE, hw), F32),
              pltpu.VMEM((nc, CHUNK, hw), BF16),
              pltpu.VMEM((nc,) + s0.shape[2:], BF16),
              pltpu.VMEM((CHUNK, hw), F32),
              pltpu.VMEM((CHUNK, hw), F32)]
    ml_scr = [pltpu.VMEM((nc, npair, CHUNK, LANE), F32),
              pltpu.VMEM((nc, npair, 2 * ML_QK, 2 * ML_V), F32),
              pltpu.VMEM((nc, npair, CHUNK, LANE), F32),
              pltpu.VMEM((nc, npair, CHUNK, LANE), F32),
              pltpu.VMEM((nc, SUBLANE, LANE), F32)]
    assert (len(hg_in), 2, len(hg_scr)) == N_HG and (len(ml_in), 3, len(ml_scr)) == N_ML
    return pl.pallas_call(
        functools.partial(_scan_kernel, nc=nc),
        grid=(bsz, 2, nblk),
        in_specs=hg_in + ml_in,
        out_specs=[out_tok, state(s0), out_tok, state(c0), state(m0)],
        out_shape=[sds((bsz, 2, t, hw), BF16), sds(s0.shape, F32),
                   sds((bsz, 2, t, hw), BF16), sds(c0.shape, F32), sds(m0.shape, F32)],
        scratch_shapes=hg_scr + ml_scr,
        compiler_params=_cparams(("parallel", "parallel", "arbitrary")),
        name="scans",
    )(hq, hv, hlf, dmat, pair_valid, wmask, s0, qk, mv, gates, le2, let2, c0, m0)


def _head_rms(x, heads):
    dh = x.shape[-1] // heads
    return jnp.concatenate([_rms(x[:, h * dh:(h + 1) * dh]) for h in range(heads)], axis=1)


def _mix_ffn_kernel(oh_ref, om_ref, hg_ref, mo_ref, x_ref, mod_ref, hnw_ref, mnw_ref, wo_ref,
                    nw_ref, w1_ref, w2_ref, fw_ref, o_ref, a_ref, x_scr, h_scr, *, d_ff, tf):
    mix_gate = mod_ref[0, 0:1, :]
    mods = [mod_ref[0, k:k + 1, :] for k in range(1, 4)]

    def stages(rs):
        def mixed(pr):
            oh = oh_ref[0, 0, pr, :].astype(F32) + oh_ref[0, 1, pr, :].astype(F32)
            om = om_ref[0, 0, pr, :].astype(F32) + om_ref[0, 1, pr, :].astype(F32)
            hg_out = _head_rms(oh, HG_HEADS) * hnw_ref[...] * _silu(hg_ref[0, pr, :].astype(F32))
            ml_out = _sigmoid(mo_ref[0, pr, :].astype(F32)) * (_head_rms(om, ML_HEADS) * mnw_ref[...])
            merged = jnp.concatenate([hg_out, ml_out], axis=1).astype(BF16)
            return x_ref[0, pr, :] + mix_gate * _dot(merged, wo_ref[...])

        def put(y):
            o_ref[0, rs, :] = _rms(y) * fw_ref[...]

        return _Stages(_ffn_stages(mixed, put, mods, nw_ref, w1_ref, w2_ref, a_ref, x_scr, h_scr,
                                   rs, d_ff, tf), _ffn_lead(rs, d_ff, tf))

    _staggered([stages(rs) for rs in _sub_tiles(x_ref.shape[1])])


def _mix_ffn(oh, om, hg, mo, x, mods, hnw, mnw, w_out, nw, w1, w2, fw):
    bsz, t, d = x.shape
    hw = hg.shape[-1]
    d_ff = w2.shape[0]
    tm = min(TOKEN_TILE, t)
    tok = lambda b, i: (b, i, 0)
    tok2 = lambda b, i: (b, 0, i, 0)
    const = lambda b, i: (0, 0)
    resident = lambda a: pl.BlockSpec(a.shape, const, pipeline_mode=pl.Buffered(1))
    return pl.pallas_call(
        functools.partial(_mix_ffn_kernel, d_ff=d_ff, tf=FF_TILE),
        grid=(bsz, t // tm),
        in_specs=[pl.BlockSpec((1, 2, tm, hw), tok2), pl.BlockSpec((1, 2, tm, hw), tok2),
                  pl.BlockSpec((1, tm, hw), tok), pl.BlockSpec((1, tm, hw), tok),
                  pl.BlockSpec((1, tm, d), tok),
                  pl.BlockSpec((1, 4, d), lambda b, i: (b, 0, 0)),
                  pl.BlockSpec((1, hw), const), pl.BlockSpec((1, hw), const), resident(w_out),
                  pl.BlockSpec((1, d), const), resident(w1), resident(w2), pl.BlockSpec((1, d), const)],
        out_specs=pl.BlockSpec((1, tm, d), tok),
        out_shape=jax.ShapeDtypeStruct((bsz, t, d), F32),
        scratch_shapes=[pltpu.VMEM((tm, d_ff), BF16),
                        pltpu.VMEM((tm, d), F32),
                        pltpu.VMEM((tm, d), BF16)],
        compiler_params=_cparams(("parallel", "parallel")),
        name="mix_ffn_final",
    )(oh, om, hg, mo, x, mods, hnw, mnw, w_out, nw, w1, w2, fw)


def kernel(x, c, ctx, c_ctx, w_mod, b_mod, norm1_w, ffn1_w1, ffn1_w2, norm2_w, w_in, ml_gate_b,
           ml_conv_w, ml_conv_b, hg_lb_logits, hg_norm_w, ml_norm_w, w_out, norm3_w, ffn2_w1, ffn2_w2,
           final_norm_w):
    bsz, seq, d = x.shape
    n_ctx = ctx.shape[1]
    assert w_mod.shape[0] == 1, "single-layer kernel"
    assert seq % max(TOKEN_TILE, SCAN_CHUNKS * CHUNK) == 0 and n_ctx % CHUNK == 0 and GRID_W == CHUNK
    hw = HG_HEADS * HG_DIM
    ng = ml_gate_b.shape[-1]

    rows = -(-(bsz + 1) // SUBLANE) * SUBLANE
    cvec = jnp.zeros((rows, d), F32).at[:bsz].set(c).at[bsz].set(c_ctx)
    mods = _modulation(cvec, w_mod[0], b_mod[0][None, :]).reshape(rows, N_MOD, d)
    lat_row = lambda b: b
    ctx_row = lambda b: bsz

    row = lambda a: a.reshape(1, -1)
    w1a, w2a = ffn1_w1[0], ffn1_w2[0]
    x1 = _ffn(x, mods[:, 0:3], lat_row, row(norm1_w[0]), w1a, w2a)
    s1 = _ffn(ctx, mods[:, 0:3], ctx_row, row(norm1_w[0]), w1a, w2a)

    w_in_b = w_in[0].T
    gate_order = np.arange(ng).reshape(2, 2, ML_HEADS // 2, 2).transpose(3, 0, 1, 2).reshape(-1)
    w_gate_t = w_in[0][:, 8 * hw:].T[gate_order].astype(BF16)
    gate_b = ml_gate_b[0][gate_order].reshape(ng, 1)
    dmat, pair_valid, wmask = (jnp.asarray(a) for a in _hgrn2_masks())
    le2, let2 = (jnp.asarray(a) for a in _mlstm_masks())
    conv_w, conv_b = ml_conv_w[0], row(ml_conv_b[0])
    npair = ML_HEADS // 2

    def mixer_scans(s, mod_row, width, s0, c0, m0):
        hq, hv, hg, hlf, mqk, mv, mo, gates = _inproj(
            s, mods[:, 3:5], mod_row, row(norm2_w[0]), w_in_b, w_gate_t, gate_b, hg_lb_logits)
        qk = _conv(mqk, conv_w, conv_b, width)
        oh, s_fin, om, c_fin, m_fin = _scans(hq, hv, hlf, qk, mv, gates, s0, c0, m0,
                                             dmat, pair_valid, wmask, le2, let2)
        return oh, om, hg, mo, s_fin, c_fin, m_fin

    s0 = jnp.zeros((bsz, 2, HG_HEADS, HG_DIM, HG_DIM), F32)
    c0 = jnp.zeros((bsz, 2, npair, 2 * ML_QK, 2 * ML_V), F32)
    m0 = jnp.zeros((bsz, 2, SUBLANE, LANE), F32)
    _, _, _, _, s_ctx, c_ctx_state, m_ctx = mixer_scans(s1, ctx_row, n_ctx, s0, c0, m0)
    oh, om, hg, mo, _, _, _ = mixer_scans(x1, lat_row, GRID_W, s_ctx, c_ctx_state, m_ctx)

    return _mix_ffn(oh, om, hg, mo, x1, mods[:, 5:9], row(hg_norm_w[0]), row(ml_norm_w[0]),
                    w_out[0].astype(BF16), row(norm3_w[0]), ffn2_w1[0], ffn2_w2[0], row(final_norm_w))
```

```python
import functools

import numpy as np
import jax
import jax.numpy as jnp
from jax import lax
from jax.experimental import pallas as pl
from jax.experimental.pallas import tpu as pltpu

F32 = jnp.float32
BF16 = jnp.bfloat16

EPS = 1e-6
CHUNK = 64
GRID_W = 64
N_MOD = 9
HG_HEADS = 4
HG_DIM = 128
ML_HEADS = 4
ML_QK = 64
ML_V = 128
SUB = 16
N_SUB = CHUNK // SUB
GROUP = 4
ML_GROUP = 4
SCAN_CHUNKS = 16
TOKEN_TILE = 512
INPROJ_TILE = 1024
ROW_SUB = 256
FF_TILE = 256
PIECE = 128
LOG2E = 1.4426950408889634
EXP2_CLAMP = 115.0
LANE = 128
SUBLANE = 8
VMEM_LIMIT = 56 * 1024 * 1024


def _cparams(sem):
    return pltpu.CompilerParams(dimension_semantics=sem, vmem_limit_bytes=VMEM_LIMIT)


def _dot(a, b):
    return jnp.dot(a, b, preferred_element_type=F32)


def _dot_nt(a, b):
    return lax.dot_general(a, b, (((1,), (1,)), ((), ())), preferred_element_type=F32)


def _dot_tn(a, b):
    return lax.dot_general(a, b, (((0,), (0,)), ((), ())), preferred_element_type=F32)


def _sigmoid(x):
    return 1.0 / (1.0 + jnp.exp(-x))


def _silu(x):
    return x * _sigmoid(x)


def _log_sigmoid(x):
    return jnp.minimum(x, 0.0) - jnp.log(1.0 + jnp.exp(-jnp.abs(x)))


def _split3(x):
    hi = x.astype(BF16)
    r = x - hi.astype(F32)
    mid = r.astype(BF16)
    lo = (r - mid.astype(F32)).astype(BF16)
    return hi, mid, lo


def _split2(x):
    hi = x.astype(BF16)
    return hi, (x - hi.astype(F32)).astype(BF16)


def _rms(x):
    return x * lax.rsqrt(jnp.mean(x * x, axis=-1, keepdims=True) + EPS)


def _mod_kernel(c_ref, w_ref, b_ref, o_ref):
    a = _silu(c_ref[...])
    a_hi = a.astype(BF16)
    a_lo = (a - a_hi.astype(F32)).astype(BF16)
    w = w_ref[...]
    w_hi = w.astype(BF16)
    w_lo = (w - w_hi.astype(F32)).astype(BF16)
    o_ref[...] = _dot(a_hi, w_hi) + _dot(a_hi, w_lo) + _dot(a_lo, w_hi) + b_ref[...]


def _modulation(cvec, w_mod, b_mod):
    rows, d = cvec.shape
    n = w_mod.shape[1]
    tn = 1024
    return pl.pallas_call(
        _mod_kernel,
        grid=(n // tn,),
        in_specs=[pl.BlockSpec((rows, d), lambda j: (0, 0)),
                  pl.BlockSpec((d, tn), lambda j: (0, j)),
                  pl.BlockSpec((1, tn), lambda j: (0, j))],
        out_specs=pl.BlockSpec((rows, tn), lambda j: (0, j)),
        out_shape=jax.ShapeDtypeStruct((rows, n), F32),
        compiler_params=_cparams(("arbitrary",)),
        name="modulation",
    )(cvec, w_mod, b_mod)


def _staggered(tiles):
    for _ in range(tiles[0].lead):
        next(tiles[0].gen)
    live = [t.gen for t in tiles]
    while live:
        live = [g for g in live if next(g, None) is not None]


class _Stages:
    def __init__(self, gen, lead):
        self.gen, self.lead = gen, lead


def _ffn_stages(get_x, put_y, mods, nw_ref, w1_ref, w2_ref, a_ref, x_scr, h_scr, rs, d_ff, tf):
    shift, scale, gate = mods
    for p0 in range(rs.start, rs.stop, PIECE):
        pr = slice(p0, min(p0 + PIECE, rs.stop))
        x = get_x(pr)
        x_scr[pr, :] = x
        h_scr[pr, :] = (_rms(x) * nw_ref[...] * (1.0 + scale) + shift).astype(BF16)
        yield True
    h = h_scr[rs, :]
    for c in range(d_ff // tf):
        g = _dot(h, w1_ref[:, c * tf:(c + 1) * tf].astype(BF16))
        u = _dot(h, w1_ref[:, d_ff + c * tf:d_ff + (c + 1) * tf].astype(BF16))
        a_ref[rs, c * tf:(c + 1) * tf] = (_silu(g) * u).astype(BF16)
        yield True
    y = x_scr[rs, :] + 0.5 * gate * _dot(a_ref[rs, :], w2_ref[...].astype(BF16))
    yield True
    put_y(y)
    yield True


def _sub_tiles(rows):
    sub = min(ROW_SUB, rows)
    return [slice(r0, r0 + sub) for r0 in range(0, rows, sub)]


def _ffn_lead(rs, d_ff, tf):
    return -(-(rs.stop - rs.start) // PIECE) + d_ff // tf // 2


def _ffn_kernel(x_ref, mod_ref, nw_ref, w1_ref, w2_ref, o_ref, a_ref, x_scr, h_scr, *, d_ff, tf):
    mods = [mod_ref[0, k:k + 1, :] for k in range(3)]

    def stages(rs):
        def put(y):
            o_ref[0, rs, :] = y
        return _Stages(_ffn_stages(lambda pr: x_ref[0, pr, :], put, mods, nw_ref, w1_ref, w2_ref, a_ref,
                                   x_scr, h_scr, rs, d_ff, tf), _ffn_lead(rs, d_ff, tf))

    _staggered([stages(rs) for rs in _sub_tiles(x_ref.shape[1])])


def _ffn(s, mods, mod_row, nw, w1, w2):
    bsz, t, d = s.shape
    d_ff = w2.shape[0]
    tm = min(TOKEN_TILE, t)
    const = lambda b, i: (0, 0)
    return pl.pallas_call(
        functools.partial(_ffn_kernel, d_ff=d_ff, tf=FF_TILE),
        grid=(bsz, t // tm),
        in_specs=[pl.BlockSpec((1, tm, d), lambda b, i: (b, i, 0)),
                  pl.BlockSpec((1, 3, d), lambda b, i: (mod_row(b), 0, 0)),
                  pl.BlockSpec((1, d), const),
                  pl.BlockSpec((d, 2 * d_ff), const, pipeline_mode=pl.Buffered(1)),
                  pl.BlockSpec((d_ff, d), const, pipeline_mode=pl.Buffered(1))],
        out_specs=pl.BlockSpec((1, tm, d), lambda b, i: (b, i, 0)),
        out_shape=jax.ShapeDtypeStruct((bsz, t, d), F32),
        scratch_shapes=[pltpu.VMEM((tm, d_ff), BF16),
                        pltpu.VMEM((tm, d), F32),
                        pltpu.VMEM((tm, d), BF16)],
        compiler_params=_cparams(("parallel", "parallel")),
        name="ffn",
    )(s, mods, nw, w1, w2)


def _inproj_kernel(x_ref, mod_ref, nw_ref, w_ref, wg_ref, gb_ref, lbl_ref,
                   hq_ref, hv_ref, hg_ref, hlf_ref, mqk_ref, mv_ref, mo_ref, gc_ref, *, hw):
    shift = mod_ref[0, 0:1, :]
    scale = mod_ref[0, 1:2, :]
    lbl = lbl_ref[...]
    e = jnp.exp(lbl - jnp.max(lbl, axis=0, keepdims=True))
    lb = e[0] / jnp.sum(e, axis=0)
    first_half = lax.broadcasted_iota(jnp.int32, (SUBLANE, LANE), 1) < CHUNK
    def stages(rs):
        h = (_rms(x_ref[0, rs, :]) * nw_ref[...] * (1.0 + scale) + shift).astype(BF16)
        yield True

        def proj(k):
            return _dot_nt(h, w_ref[k * hw:(k + 1) * hw, :].astype(BF16))

        hq_ref[0, rs, :] = _silu(proj(0)) * (HG_DIM ** -0.5)
        yield True
        hv_ref[0, rs, :] = proj(1).astype(BF16)
        yield True
        hg_ref[0, rs, :] = proj(2).astype(BF16)
        yield True
        for d in range(2):
            lbd = lb[d:d + 1, :]
            hlf_ref[0, d, rs, :] = jnp.log(lbd + (1.0 - lbd) * _sigmoid(proj(3 + d))) * LOG2E
            yield True
        mqk_ref[0, rs, :] = proj(5).astype(BF16)
        yield True
        mv_ref[0, rs, :] = proj(6).astype(BF16)
        yield True
        mo_ref[0, rs, :] = proj(7).astype(BF16)
        gt = _dot_nt(wg_ref[...], h) + gb_ref[...]
        for m in range((rs.stop - rs.start) // LANE):
            top = gt[0:8, m * LANE:(m + 1) * LANE]
            bot = gt[8:16, m * LANE:(m + 1) * LANE]
            c0 = rs.start // CHUNK + 2 * m
            gc_ref[0, c0] = jnp.where(first_half, top, pltpu.roll(bot, CHUNK, 1))
            gc_ref[0, c0 + 1] = jnp.where(first_half, pltpu.roll(top, CHUNK, 1), bot)
        yield True

    _staggered([_Stages(stages(rs), 8) for rs in _sub_tiles(x_ref.shape[1])])


def _inproj(s, mods, mod_row, nw, w_in, w_gate_t, gate_b, lb_logits):
    bsz, t, d = s.shape
    hw = HG_HEADS * HG_DIM
    ng = gate_b.shape[0]
    tm = min(INPROJ_TILE, t)
    tok4 = lambda b, i: (b, i, 0, 0)
    const = lambda b, i: (0, 0)
    tok = lambda b, i: (b, i, 0)
    tok2 = lambda b, i: (b, 0, i, 0)
    f = lambda dt, *shape: jax.ShapeDtypeStruct(shape, dt)
    return pl.pallas_call(
        functools.partial(_inproj_kernel, hw=hw),
        grid=(bsz, t // tm),
        in_specs=[pl.BlockSpec((1, tm, d), tok),
                  pl.BlockSpec((1, 2, d), lambda b, i: (mod_row(b), 0, 0)),
                  pl.BlockSpec((1, d), const),
                  pl.BlockSpec(w_in.shape, const, pipeline_mode=pl.Buffered(1)),
                  pl.BlockSpec(w_gate_t.shape, const),
                  pl.BlockSpec((ng, 1), const),
                  pl.BlockSpec(lb_logits.shape, lambda b, i: (0, 0, 0))],
        out_specs=[pl.BlockSpec((1, tm, hw), tok), pl.BlockSpec((1, tm, hw), tok),
                   pl.BlockSpec((1, tm, hw), tok),
                   pl.BlockSpec((1, 2, tm, hw), tok2),
                   pl.BlockSpec((1, tm, hw), tok), pl.BlockSpec((1, tm, hw), tok),
                   pl.BlockSpec((1, tm, hw), tok), pl.BlockSpec((1, tm // CHUNK, ng // 2, LANE), tok4)],
        out_shape=[f(F32, bsz, t, hw), f(BF16, bsz, t, hw), f(BF16, bsz, t, hw),
                   f(F32, bsz, 2, t, hw),
                   f(BF16, bsz, t, hw), f(BF16, bsz, t, hw), f(BF16, bsz, t, hw),
                   f(F32, bsz, t // CHUNK, ng // 2, LANE)],
        compiler_params=_cparams(("parallel", "parallel")),
        name="inproj",
    )(s, mods, nw, w_in, w_gate_t, gate_b, lb_logits)


def _conv_kernel(x_ref, w_ref, b_ref, o_ref, *, width, q_tiles):
    x = x_ref[0].astype(F32)
    t = x.shape[0]
    col = lax.broadcasted_iota(jnp.int32, x.shape, 0) % width
    xl = jnp.where(col == 0, 0.0, pltpu.roll(x, 1, 0))
    xr = jnp.where(col == width - 1, 0.0, pltpu.roll(x, t - 1, 0))

    def row(di):
        return (w_ref[di, 0:1, :] * xl + w_ref[di, 1:2, :] * x + w_ref[di, 2:3, :] * xr)

    y = row(1)
    if t > width:
        pad = jnp.zeros((width, x.shape[1]), F32)
        y = y + jnp.concatenate([pad, row(0)[:t - width]], axis=0)
        y = y + jnp.concatenate([row(2)[width:], pad], axis=0)
    y = _silu(y + b_ref[...])
    qscale = jnp.where(pl.program_id(1) < q_tiles, ML_QK ** -0.5, 1.0)
    o_ref[0] = (y * qscale).astype(BF16)


def _conv(x, w, b, width):
    bsz, t, ch = x.shape
    return pl.pallas_call(
        functools.partial(_conv_kernel, width=width, q_tiles=ML_HEADS * ML_QK // LANE),
        grid=(bsz, ch // LANE),
        in_specs=[pl.BlockSpec((1, t, LANE), lambda b_, c: (b_, 0, c)),
                  pl.BlockSpec((3, 3, LANE), lambda b_, c: (0, 0, c)),
                  pl.BlockSpec((1, LANE), lambda b_, c: (0, c))],
        out_specs=pl.BlockSpec((1, t, LANE), lambda b_, c: (b_, 0, c)),
        out_shape=jax.ShapeDtypeStruct((bsz, t, ch), BF16),
        compiler_params=_cparams(("parallel", "parallel")),
        name="conv",
    )(x, w, b)


def _positions():
    t = np.arange(CHUNK)
    return [t, CHUNK - 1 - t]


def _hgrn2_masks():
    ds, vs, ws = [], [], []
    for p in _positions():
        le = (p[None, :] <= p[:, None]).astype(np.float32)
        order = [p[SUB * j] // SUB for j in range(N_SUB)]
        ref = np.zeros((32, CHUNK), np.float32)
        valid = np.zeros((N_SUB * N_SUB, 1), np.float32)
        for j in range(N_SUB):
            ref[j] = p <= SUB * order[j] + SUB // 2 - 1
            ref[8 + j] = p <= SUB * order[j] + SUB - 1
        ref[N_SUB] = 1.0
        for j in range(N_SUB):
            for i in range(N_SUB):
                if order[i] < order[j]:
                    ref[16 + N_SUB * j + i] = ref[j] - ref[8 + i]
                    valid[N_SUB * j + i] = 1.0
        ds.append(np.concatenate([le, ref], axis=0))
        vs.append(np.broadcast_to(valid, (N_SUB * N_SUB, HG_HEADS * HG_DIM)))
        tok_blk = np.arange(CHUNK)[:, None] // SUB
        ws.append(np.concatenate([(tok_blk == j) * le for j in range(N_SUB)], axis=1))
    return (np.stack(ds).astype(np.float32), np.stack(vs).astype(np.float32),
            np.stack(ws).astype(np.float32))


def _mlstm_masks():
    le2s, let2s = [], []
    for p in _positions():
        le = (p[None, :] <= p[:, None]).astype(np.float32)
        le2s.append(np.concatenate([le, le], axis=1))
        let2 = np.zeros((2 * CHUNK, 2 * CHUNK), np.float32)
        let2[:CHUNK, :CHUNK] = le.T
        let2[CHUNK:, CHUNK:] = le.T
        let2s.append(let2)
    return np.stack(le2s), np.stack(let2s)


def _chunk_groups(nc, group):
    bounds = [0] + list(range(max(1, group // 2), nc, group)) + [nc]
    return [list(range(a, b)) for a, b in zip(bounds[:-1], bounds[1:]) if b > a]


def _pipelined(chunk_groups, stages_of, step_of):
    queue = []
    for chunks in chunk_groups:
        for _ in stages_of(chunks):
            if queue:
                queue.pop(0)()
            yield True
        while queue:
            queue.pop(0)()
            yield True
        queue = [functools.partial(step_of, c) for c in chunks]
    for step in queue:
        step()
        yield True


def _interleave(programs):
    live = list(programs)
    while live:
        live = [p for p in live if next(p, None) is not None]


def _chunk_rows(c, nc, direction):
    cc = jnp.where(direction == 0, c, nc - 1 - c)
    return pl.multiple_of(cc * CHUNK, CHUNK)


def _hgrn2_program(q_ref, v_ref, lf_ref, d_ref, pv_ref, w_ref, s0_ref, o_ref, st_ref,
                   u_ref, qin_ref, g_ref, a_ref, vt_ref, bx_ref, kx_ref, *, nc):
    direction = pl.program_id(1)

    @pl.when(pl.program_id(2) == 0)
    def _():
        st_ref[...] = s0_ref[...]

    dmat = d_ref[0].astype(BF16)
    pair_valid = pv_ref[0]
    wmask = w_ref[0] > 0.0
    L = CHUNK
    hw = q_ref.shape[-1]
    upper_rows = lax.broadcasted_iota(jnp.int32, (L, 2 * L), 0) < L // 2

    def intra(cs):
        group = len(cs)
        rows = [pl.ds(_chunk_rows(c, nc, direction), L) for c in cs]
        items = [(i, h) for i in range(group) for h in range(HG_HEADS)]
        lanes = [slice(h * HG_DIM, (h + 1) * HG_DIM) for h in range(HG_HEADS)]
        lfs = [lf_ref[0, 0, r, :] for r in rows]
        parts = [_split2(lf) for lf in lfs]
        e_all = sum(_dot(dmat, jnp.concatenate([parts[i][n] for i in range(group)], axis=1))
                    for n in range(2))
        yield
        q_mids, k_mids, k_outs, vbs, spans = [], [], [], [], []
        for i in range(group):
            e = e_all[:, i * hw:(i + 1) * hw]
            q = q_ref[0, rows[i], :]
            k = 1.0 - jnp.exp2(lfs[i])
            b = e[0:L]
            b_tot = e[L + N_SUB:L + N_SUB + 1]
            sub_rows = lambda r0: jnp.concatenate(
                [jnp.broadcast_to(e[r0 + j:r0 + j + 1], (SUB, hw)) for j in range(N_SUB)], axis=0)
            mid_rows = sub_rows(L)
            end_rows = sub_rows(L + 8)
            to_mid = b - mid_rows
            spans.append(jnp.max(jnp.abs(to_mid)))
            q_mids.append((q * jnp.exp2(jnp.minimum(to_mid, EXP2_CLAMP))).astype(BF16))
            k_diag = k * jnp.exp2(jnp.minimum(-to_mid, EXP2_CLAMP))
            k_end = k * jnp.exp2(end_rows - b)
            cross = jnp.exp2(e[L + 16:L + 32]) * pair_valid
            slabs = []
            for j in range(N_SUB):
                for ib in range(N_SUB):
                    rs = slice(ib * SUB, (ib + 1) * SUB)
                    row = N_SUB * j + ib
                    slabs.append(k_diag[rs] if ib == j else k_end[rs] * cross[row:row + 1])
            k_mids.append(jnp.concatenate(slabs, axis=0).astype(BF16))
            k_outs.append((k * jnp.exp2(b_tot - b)).astype(BF16))
            vbs.append(v_ref[0, rows[i], :])
            qin_ref[cs[i]] = (q * jnp.exp2(b)).astype(BF16)
            g_ref[cs[i]] = jnp.broadcast_to(jnp.exp2(b_tot), (SUBLANE, hw))
            yield
        scores = [_dot_nt(q_mids[i][:, lanes[h]], k_mids[i][:, lanes[h]]) for i, h in items]
        yield
        a_s = [jnp.where(wmask, s, 0.0) for s in scores]
        a_s = [jnp.where(upper_rows, a[:, :2 * L], a[:, 2 * L:]).astype(BF16) for a in a_s]
        for i in range(group):
            a_ref[cs[i]] = jnp.concatenate(a_s[i * HG_HEADS:(i + 1) * HG_HEADS], axis=1)
        yield
        for i, h in items:
            v_h = vbs[i][:, lanes[h]].astype(F32)
            vt_ref[cs[i], h] = jnp.concatenate([v_h, v_h], axis=0).T.astype(BF16)
        yield
        for i, h in items:
            u_ref[cs[i], h] = _dot_tn(vbs[i][:, lanes[h]], k_outs[i][:, lanes[h]])

        @pl.when(functools.reduce(jnp.maximum, spans) > EXP2_CLAMP)
        def _():
            for i in range(group):
                exact_intra(cs[i], rows[i])
        yield

    def exact_intra(c, rows):
        lf = lf_ref[0, 0, rows, :]
        bx_ref[...] = sum(_dot(dmat[0:L], x) for x in _split3(lf))
        kx_ref[...] = 1.0 - jnp.exp2(lf)
        q = q_ref[0, rows, :]
        tok = lax.broadcasted_iota(jnp.int32, (L, 1), 0)
        pos = jnp.where(direction == 0, tok, L - 1 - tok)
        lane_in_head = lax.broadcasted_iota(jnp.int32, (1, hw), 1) % HG_DIM

        def add_source(s, acc):
            b_s = bx_ref[pl.ds(s, 1), :]
            z = q * kx_ref[pl.ds(s, 1), :] * jnp.exp2(jnp.minimum(bx_ref[...] - b_s, 0.0))
            seen = pos >= jnp.where(direction == 0, s, L - 1 - s)
            cols = [jnp.broadcast_to(jnp.where(seen, jnp.sum(z[:, ln], axis=1, keepdims=True), 0.0), (L, HG_DIM))
                    for ln in (slice(h * HG_DIM, (h + 1) * HG_DIM) for h in range(HG_HEADS))]
            return jnp.where(lane_in_head == s, jnp.concatenate(cols, axis=1), acc)

        a_ref[c] = lax.fori_loop(0, L, add_source, jnp.zeros((L, hw), F32)).astype(BF16)

    def carry_state(c):
        rows = pl.ds(_chunk_rows(c, nc, direction), L)
        q_in = qin_ref[c]
        a = a_ref[c]
        g = g_ref[c][0:1]
        outs, states = [], []
        for h in range(HG_HEADS):
            ln = slice(h * HG_DIM, (h + 1) * HG_DIM)
            st = st_ref[0, 0, h]
            outs.append(_dot_nt(jnp.concatenate([a[:, ln], q_in[:, ln]], axis=1),
                                jnp.concatenate([vt_ref[c, h], st.astype(BF16)], axis=1)))
            states.append(g[:, ln] * st + u_ref[c, h])
        o_ref[0, 0, rows, :] = jnp.concatenate(outs, axis=1).astype(BF16)
        for h in range(HG_HEADS):
            st_ref[0, 0, h] = states[h]

    return _pipelined(_chunk_groups(nc, GROUP), intra, carry_state)


N_HG = (7, 2, 7)
N_ML = (7, 3, 5)


def _mlstm_program(qk_ref, v_ref, g_ref, le2_ref, let2_ref, c0_ref, m0_ref,
                   o_ref, c_ref, mm_ref,
                   al_ref, u_ref, ct_ref, mloc_ref, sc_ref, *, nc):
    direction = pl.program_id(1)

    @pl.when(pl.program_id(2) == 0)
    def _():
        c_ref[...] = c0_ref[...]
        mm_ref[...] = m0_ref[...]

    let2 = let2_ref[0].astype(BF16)
    causal = le2_ref[0] > 0.0
    L = CHUNK
    neg = -jnp.inf
    lane = lax.broadcasted_iota(jnp.int32, (1, LANE), 1)
    seg = [lane < L, lane >= L]
    lane_t = lax.broadcasted_iota(jnp.int32, (L, LANE), 1) < L
    npair = ML_HEADS // 2
    ones_v = jnp.ones((L, ML_V), BF16)
    zb = jnp.zeros((L, LANE), BF16)

    def by_head(x):
        return jnp.concatenate([jnp.where(lane_t, x, zb), jnp.where(lane_t, zb, x)], axis=0)

    def v_stack(vb, p):
        return jnp.concatenate(
            [jnp.concatenate([vb[:, 2 * p * ML_V:(2 * p + 1) * ML_V], ones_v], axis=1),
             jnp.concatenate([vb[:, (2 * p + 1) * ML_V:(2 * p + 2) * ML_V], ones_v], axis=1)], axis=0)

    def intra(cs):
        group = len(cs)
        ccs = [jnp.where(direction == 0, c, nc - 1 - c) for c in cs]
        rows = [pl.ds(pl.multiple_of(cc * L, L), L) for cc in ccs]
        pairs = [(i, p) for i in range(group) for p in range(npair)]
        gs = [jnp.where(direction == 0, g_ref[0, cc][0:4], g_ref[0, cc][4:8]) for cc in ccs]
        pad = [jnp.zeros((-2 * group % SUBLANE, LANE), F32)] if 2 * group % SUBLANE else []
        i2 = jnp.concatenate([g[0:2] for g in gs] + pad, axis=0)
        lf2 = _log_sigmoid(jnp.concatenate([g[2:4] for g in gs] + pad, axis=0))
        parts = _split3(lf2)
        b2 = sum(_dot(x, let2) for x in parts)
        r2 = i2 - b2
        g_a = jnp.max(jnp.where(seg[0], r2, neg), axis=1, keepdims=True)
        g_b = jnp.max(jnp.where(seg[1], r2, neg), axis=1, keepdims=True)
        e_w = jnp.exp(r2 - jnp.where(seg[0], g_a, g_b))
        bt_a = jnp.sum(jnp.where(seg[0], lf2, 0.0), axis=1, keepdims=True)
        bt_b = jnp.sum(jnp.where(seg[1], lf2, 0.0), axis=1, keepdims=True)
        yield

        def as_columns(row):
            cols = jnp.broadcast_to(row, (LANE, LANE)).T
            return jnp.where(lane_t, cols[0:L], cols[L:2 * L])

        b_ts = [as_columns(b2[2 * i + p:2 * i + p + 1]) for i, p in pairs]
        for i in range(group):
            sc_ref[cs[i]] = jnp.concatenate(
                [jnp.broadcast_to(x[2 * i + p:2 * i + p + 1], (1, LANE))
                 for x2 in ((bt_a, bt_b), (g_a, g_b)) for p in range(npair) for x in x2], axis=0)
        yield
        qks = [qk_ref[0, r, :] for r in rows]
        vbs = [v_ref[0, r, :] for r in rows]
        q_ps = [qks[i][:, p * LANE:(p + 1) * LANE] for i, p in pairs]
        k_ps = [qks[i][:, (npair + p) * LANE:(npair + p + 1) * LANE] for i, p in pairs]
        kbs = [by_head(k) for k in k_ps]
        scores = [_dot_nt(q, kb) for q, kb in zip(q_ps, kbs)]
        yield
        for n, (i, p) in enumerate(pairs):
            t1 = b_ts[n]
            d_log = jnp.where(causal, t1 + r2[2 * i + p:2 * i + p + 1], neg)
            m_a = jnp.max(jnp.where(lane_t, d_log, neg), axis=1, keepdims=True)
            m_b = jnp.max(jnp.where(lane_t, neg, d_log), axis=1, keepdims=True)
            m_loc = jnp.where(lane_t, m_a, m_b)
            ct_ref[cs[i], p] = t1
            mloc_ref[cs[i], p] = m_loc
            al_ref[cs[i], p] = scores[n] * jnp.exp(d_log - m_loc)
        yield
        v_stacks = [v_stack(vbs[i], p) for i, p in pairs]
        yield
        kts = [(kb.astype(F32).T * e_w[2 * i + p:2 * i + p + 1]).astype(BF16) for kb, (i, p) in zip(kbs, pairs)]
        for n, (i, p) in enumerate(pairs):
            u_ref[cs[i], p] = _dot(kts[n], v_stacks[n])
        yield

    def carry_state(c):
        cc = jnp.where(direction == 0, c, nc - 1 - c)
        rows = pl.ds(pl.multiple_of(cc * L, L), L)
        m_all = mm_ref[0, 0]
        sc = sc_ref[c]
        qk = qk_ref[0, rows, :]
        vb = v_ref[0, rows, :]
        outs, states, m_rows = [], [], []
        for p in range(ML_HEADS // 2):
            c_pair = c_ref[0, 0, p]
            inc = u_ref[c, p]
            b_t = ct_ref[c, p]
            m_loc = mloc_ref[c, p]
            m_in2 = jnp.where(seg[0], m_all[2 * p:2 * p + 1], m_all[2 * p + 1:2 * p + 2])
            m_t = jnp.maximum(m_loc, b_t + m_in2)
            lhs_a = (jnp.exp(m_loc - m_t) * al_ref[c, p]).astype(BF16)
            lhs_q = (jnp.exp(b_t + m_in2 - m_t) * qk[:, p * LANE:(p + 1) * LANE].astype(F32)).astype(BF16)
            lhs = jnp.concatenate([by_head(lhs_a), by_head(lhs_q)], axis=1)
            tot = _dot(lhs, jnp.concatenate([v_stack(vb, p), c_pair.astype(BF16)], axis=0))
            floor = jnp.exp(-m_t)
            floor_r = pltpu.roll(floor, L, 1)
            new_rows = []
            for hh in range(2):
                h = 2 * p + hh
                rs = slice(hh * L, (hh + 1) * L)
                m_in = m_all[h:h + 1]
                floor_h = jnp.where(lane_t, floor, floor_r) if hh == 0 else jnp.where(lane_t, floor_r, floor)
                outs.append(tot[rs, :ML_V] / jnp.maximum(jnp.abs(tot[rs, ML_V:]), floor_h))
                b_tot = sc[h:h + 1]
                m_new = b_tot + jnp.maximum(m_in, sc[ML_HEADS + h:ML_HEADS + h + 1])
                carry_w = jnp.exp(b_tot + m_in - m_new)
                inc_w = jnp.exp(b_tot + sc[ML_HEADS + h:ML_HEADS + h + 1] - m_new)
                cw2 = jnp.concatenate([carry_w, carry_w], axis=1)
                iw2 = jnp.concatenate([inc_w, inc_w], axis=1)
                new_rows.append(cw2 * c_pair[rs] + iw2 * inc[rs])
                m_rows.append(m_new)
            states.append(jnp.concatenate(new_rows, axis=0))
        o_ref[0, 0, rows, :] = jnp.concatenate(outs, axis=1).astype(BF16)
        for p in range(ML_HEADS // 2):
            c_ref[0, 0, p] = states[p]
        mm_ref[0, 0] = jnp.concatenate(m_rows + [m_all[ML_HEADS:]], axis=0)

    return _pipelined(_chunk_groups(nc, ML_GROUP), intra, carry_state)


def _scan_kernel(*refs, nc):
    hg_in, ml_in, hg_out, ml_out, hg_scr, ml_scr = (
        refs[a:b] for a, b in zip(np.cumsum([0, N_HG[0], N_ML[0], N_HG[1], N_ML[1], N_HG[2]]),
                                  np.cumsum([N_HG[0], N_ML[0], N_HG[1], N_ML[1], N_HG[2], N_ML[2]])))
    _interleave([_hgrn2_program(*hg_in, *hg_out, *hg_scr, nc=nc),
                 _mlstm_program(*ml_in, *ml_out, *ml_scr, nc=nc)])


def _scans(hq, hv, hlf, qk, mv, gates, s0, c0, m0, dmat, pair_valid, wmask, le2, let2):
    bsz, t, hw = hq.shape
    nc = min(SCAN_CHUNKS, t // CHUNK)
    tb = nc * CHUNK
    nblk = t // tb
    npair = ML_HEADS // 2

    def blk(j, d):
        return jnp.where(d == 0, j, nblk - 1 - j)

    tok = lambda b, d, j: (b, blk(j, d), 0)
    tok_d = lambda b, d, j: (b, d, blk(j, d), 0)
    per_dir = lambda a: pl.BlockSpec((1,) + a.shape[1:], lambda b, d, j: (d,) + (0,) * (a.ndim - 1))
    state = lambda a: pl.BlockSpec((1, 1) + a.shape[2:], lambda b, d, j: (b, d) + (0,) * (a.ndim - 2))
    out_tok = pl.BlockSpec((1, 1, tb, hw), tok_d)
    hg_in = [pl.BlockSpec((1, tb, hw), tok), pl.BlockSpec((1, tb, hw), tok), pl.BlockSpec((1, 1, tb, hw), tok_d),
             per_dir(dmat), per_dir(pair_valid), per_dir(wmask), state(s0)]
    ml_in = [pl.BlockSpec((1, tb, qk.shape[-1]), tok), pl.BlockSpec((1, tb, hw), tok),
             pl.BlockSpec((1, nc) + gates.shape[2:], lambda b, d, j: (b, blk(j, d), 0, 0)),
             per_dir(le2), per_dir(let2), state(c0), state(m0)]
    sds = jax.ShapeDtypeStruct
    hg_scr = [pltpu.VMEM((nc,) + s0.shape[2:], F32),
              pltpu.VMEM((nc, CHUNK, hw), BF16),
              pltpu.VMEM((nc, SUBLANE, hw), F32),
              pltpu.VMEM((nc, CHUNK, hw), BF16),
              pltpu.VMEM((nc,) + s0.shape[2:], BF16),
              pltpu.VMEM((CHUNK, hw), F32),
              pltpu.VMEM((CHUNK, hw), F32)]
    ml_scr = [pltpu.VMEM((nc, npair, CHUNK, LANE), F32),
              pltpu.VMEM((nc, npair, 2 * ML_QK, 2 * ML_V), F32),
              pltpu.VMEM((nc, npair, CHUNK, LANE), F32),
              pltpu.VMEM((nc, npair, CHUNK, LANE), F32),
              pltpu.VMEM((nc, SUBLANE, LANE), F32)]
    assert (len(hg_in), 2, len(hg_scr)) == N_HG and (len(ml_in), 3, len(ml_scr)) == N_ML
    return pl.pallas_call(
        functools.partial(_scan_kernel, nc=nc),
        grid=(bsz, 2, nblk),
        in_specs=hg_in + ml_in,
        out_specs=[out_tok, state(s0), out_tok, state(c0), state(m0)],
        out_shape=[sds((bsz, 2, t, hw), BF16), sds(s0.shape, F32),
                   sds((bsz, 2, t, hw), BF16), sds(c0.shape, F32), sds(m0.shape, F32)],
        scratch_shapes=hg_scr + ml_scr,
        compiler_params=_cparams(("parallel", "parallel", "arbitrary")),
        name="scans",
    )(hq, hv, hlf, dmat, pair_valid, wmask, s0, qk, mv, gates, le2, let2, c0, m0)


def _head_rms(x, heads):
    dh = x.shape[-1] // heads
    return jnp.concatenate([_rms(x[:, h * dh:(h + 1) * dh]) for h in range(heads)], axis=1)


def _mix_ffn_kernel(oh_ref, om_ref, hg_ref, mo_ref, x_ref, mod_ref, hnw_ref, mnw_ref, wo_ref,
                    nw_ref, w1_ref, w2_ref, fw_ref, o_ref, a_ref, x_scr, h_scr, *, d_ff, tf):
    mix_gate = mod_ref[0, 0:1, :]
    mods = [mod_ref[0, k:k + 1, :] for k in range(1, 4)]

    def stages(rs):
        def mixed(pr):
            oh = oh_ref[0, 0, pr, :].astype(F32) + oh_ref[0, 1, pr, :].astype(F32)
            om = om_ref[0, 0, pr, :].astype(F32) + om_ref[0, 1, pr, :].astype(F32)
            hg_out = _head_rms(oh, HG_HEADS) * hnw_ref[...] * _silu(hg_ref[0, pr, :].astype(F32))
            ml_out = _sigmoid(mo_ref[0, pr, :].astype(F32)) * (_head_rms(om, ML_HEADS) * mnw_ref[...])
            merged = jnp.concatenate([hg_out, ml_out], axis=1).astype(BF16)
            return x_ref[0, pr, :] + mix_gate * _dot(merged, wo_ref[...])

        def put(y):
            o_ref[0, rs, :] = _rms(y) * fw_ref[...]

        return _Stages(_ffn_stages(mixed, put, mods, nw_ref, w1_ref, w2_ref, a_ref, x_scr, h_scr,
                                   rs, d_ff, tf), _ffn_lead(rs, d_ff, tf))

    _staggered([stages(rs) for rs in _sub_tiles(x_ref.shape[1])])


def _mix_ffn(oh, om, hg, mo, x, mods, hnw, mnw, w_out, nw, w1, w2, fw):
    bsz, t, d = x.shape
    hw = hg.shape[-1]
    d_ff = w2.shape[0]
    tm = min(TOKEN_TILE, t)
    tok = lambda b, i: (b, i, 0)
    tok2 = lambda b, i: (b, 0, i, 0)
    const = lambda b, i: (0, 0)
    resident = lambda a: pl.BlockSpec(a.shape, const, pipeline_mode=pl.Buffered(1))
    return pl.pallas_call(
        functools.partial(_mix_ffn_kernel, d_ff=d_ff, tf=FF_TILE),
        grid=(bsz, t // tm),
        in_specs=[pl.BlockSpec((1, 2, tm, hw), tok2), pl.BlockSpec((1, 2, tm, hw), tok2),
                  pl.BlockSpec((1, tm, hw), tok), pl.BlockSpec((1, tm, hw), tok),
                  pl.BlockSpec((1, tm, d), tok),
                  pl.BlockSpec((1, 4, d), lambda b, i: (b, 0, 0)),
                  pl.BlockSpec((1, hw), const), pl.BlockSpec((1, hw), const), resident(w_out),
                  pl.BlockSpec((1, d), const), resident(w1), resident(w2), pl.BlockSpec((1, d), const)],
        out_specs=pl.BlockSpec((1, tm, d), tok),
        out_shape=jax.ShapeDtypeStruct((bsz, t, d), F32),
        scratch_shapes=[pltpu.VMEM((tm, d_ff), BF16),
                        pltpu.VMEM((tm, d), F32),
                        pltpu.VMEM((tm, d), BF16)],
        compiler_params=_cparams(("parallel", "parallel")),
        name="mix_ffn_final",
    )(oh, om, hg, mo, x, mods, hnw, mnw, w_out, nw, w1, w2, fw)


def kernel(x, c, ctx, c_ctx, w_mod, b_mod, norm1_w, ffn1_w1, ffn1_w2, norm2_w, w_in, ml_gate_b,
           ml_conv_w, ml_conv_b, hg_lb_logits, hg_norm_w, ml_norm_w, w_out, norm3_w, ffn2_w1, ffn2_w2,
           final_norm_w):
    bsz, seq, d = x.shape
    n_ctx = ctx.shape[1]
    assert w_mod.shape[0] == 1, "single-layer kernel"
    assert seq % max(TOKEN_TILE, SCAN_CHUNKS * CHUNK) == 0 and n_ctx % CHUNK == 0 and GRID_W == CHUNK
    hw = HG_HEADS * HG_DIM
    ng = ml_gate_b.shape[-1]

    rows = -(-(bsz + 1) // SUBLANE) * SUBLANE
    cvec = jnp.zeros((rows, d), F32).at[:bsz].set(c).at[bsz].set(c_ctx)
    mods = _modulation(cvec, w_mod[0], b_mod[0][None, :]).reshape(rows, N_MOD, d)
    lat_row = lambda b: b
    ctx_row = lambda b: bsz

    row = lambda a: a.reshape(1, -1)
    w1a, w2a = ffn1_w1[0], ffn1_w2[0]
    x1 = _ffn(x, mods[:, 0:3], lat_row, row(norm1_w[0]), w1a, w2a)
    s1 = _ffn(ctx, mods[:, 0:3], ctx_row, row(norm1_w[0]), w1a, w2a)

    w_in_b = w_in[0].T
    gate_order = np.arange(ng).reshape(2, 2, ML_HEADS // 2, 2).transpose(3, 0, 1, 2).reshape(-1)
    w_gate_t = w_in[0][:, 8 * hw:].T[gate_order].astype(BF16)
    gate_b = ml_gate_b[0][gate_order].reshape(ng, 1)
    dmat, pair_valid, wmask = (jnp.asarray(a) for a in _hgrn2_masks())
    le2, let2 = (jnp.asarray(a) for a in _mlstm_masks())
    conv_w, conv_b = ml_conv_w[0], row(ml_conv_b[0])
    npair = ML_HEADS // 2

    def mixer_scans(s, mod_row, width, s0, c0, m0):
        hq, hv, hg, hlf, mqk, mv, mo, gates = _inproj(
            s, mods[:, 3:5], mod_row, row(norm2_w[0]), w_in_b, w_gate_t, gate_b, hg_lb_logits)
        qk = _conv(mqk, conv_w, conv_b, width)
        oh, s_fin, om, c_fin, m_fin = _scans(hq, hv, hlf, qk, mv, gates, s0, c0, m0,
                                             dmat, pair_valid, wmask, le2, let2)
        return oh, om, hg, mo, s_fin, c_fin, m_fin

    s0 = jnp.zeros((bsz, 2, HG_HEADS, HG_DIM, HG_DIM), F32)
    c0 = jnp.zeros((bsz, 2, npair, 2 * ML_QK, 2 * ML_V), F32)
    m0 = jnp.zeros((bsz, 2, SUBLANE, LANE), F32)
    _, _, _, _, s_ctx, c_ctx_state, m_ctx = mixer_scans(s1, ctx_row, n_ctx, s0, c0, m0)
    oh, om, hg, mo, _, _, _ = mixer_scans(x1, lat_row, GRID_W, s_ctx, c_ctx_state, m_ctx)

    return _mix_ffn(oh, om, hg, mo, x1, mods[:, 5:9], row(hg_norm_w[0]), row(ml_norm_w[0]),
                    w_out[0].astype(BF16), row(norm3_w[0]), ffn2_w1[0], ffn2_w2[0], row(final_norm_w))
```

```python
import functools

import numpy as np
import jax
import jax.numpy as jnp
from jax import lax
from jax.experimental import pallas as pl
from jax.experimental.pallas import tpu as pltpu

F32 = jnp.float32
BF16 = jnp.bfloat16

EPS = 1e-6
CHUNK = 64
GRID_W = 64
N_MOD = 9
HG_HEADS = 4
HG_DIM = 128
ML_HEADS = 4
ML_QK = 64
ML_V = 128
SUB = 16
N_SUB = CHUNK // SUB
GROUP = 4
ML_GROUP = 4
SCAN_CHUNKS = 16
TOKEN_TILE = 512
INPROJ_TILE = 1024
ROW_SUB = 256
FF_TILE = 256
PIECE = 128
LOG2E = 1.4426950408889634
EXP2_CLAMP = 115.0
LANE = 128
SUBLANE = 8
VMEM_LIMIT = 56 * 1024 * 1024


def _cparams(sem):
    return pltpu.CompilerParams(dimension_semantics=sem, vmem_limit_bytes=VMEM_LIMIT)


def _dot(a, b):
    return jnp.dot(a, b, preferred_element_type=F32)


def _dot_nt(a, b):
    return lax.dot_general(a, b, (((1,), (1,)), ((), ())), preferred_element_type=F32)


def _dot_tn(a, b):
    return lax.dot_general(a, b, (((0,), (0,)), ((), ())), preferred_element_type=F32)


def _sigmoid(x):
    return 1.0 / (1.0 + jnp.exp(-x))


def _silu(x):
    return x * _sigmoid(x)


def _log_sigmoid(x):
    return jnp.minimum(x, 0.0) - jnp.log(1.0 + jnp.exp(-jnp.abs(x)))


def _split3(x):
    hi = x.astype(BF16)
    r = x - hi.astype(F32)
    mid = r.astype(BF16)
    lo = (r - mid.astype(F32)).astype(BF16)
    return hi, mid, lo


def _split2(x):
    hi = x.astype(BF16)
    return hi, (x - hi.astype(F32)).astype(BF16)


def _rms(x):
    return x * lax.rsqrt(jnp.mean(x * x, axis=-1, keepdims=True) + EPS)


def _mod_kernel(c_ref, w_ref, b_ref, o_ref):
    a = _silu(c_ref[...])
    a_hi = a.astype(BF16)
    a_lo = (a - a_hi.astype(F32)).astype(BF16)
    w = w_ref[...]
    w_hi = w.astype(BF16)
    w_lo = (w - w_hi.astype(F32)).astype(BF16)
    o_ref[...] = _dot(a_hi, w_hi) + _dot(a_hi, w_lo) + _dot(a_lo, w_hi) + b_ref[...]


def _modulation(cvec, w_mod, b_mod):
    rows, d = cvec.shape
    n = w_mod.shape[1]
    tn = 1024
    return pl.pallas_call(
        _mod_kernel,
        grid=(n // tn,),
        in_specs=[pl.BlockSpec((rows, d), lambda j: (0, 0)),
                  pl.BlockSpec((d, tn), lambda j: (0, j)),
                  pl.BlockSpec((1, tn), lambda j: (0, j))],
        out_specs=pl.BlockSpec((rows, tn), lambda j: (0, j)),
        out_shape=jax.ShapeDtypeStruct((rows, n), F32),
        compiler_params=_cparams(("arbitrary",)),
        name="modulation",
    )(cvec, w_mod, b_mod)


def _staggered(tiles):
    waiting = list(tiles)
    live = []
    while waiting or live:
        if waiting and (not live or live[-1][1] >= live[-1][2]):
            t = waiting.pop(0)
            live.append([t.gen, 0, t.lead])
        for entry in live:
            entry[1] += 1
        live = [e for e in live if next(e[0], None) is not None]


class _Stages:
    def __init__(self, gen, lead):
        self.gen, self.lead = gen, lead


def _ffn_stages(get_x, put_y, mods, nw_ref, w1_ref, w2_ref, a_ref, x_scr, h_scr, rs, d_ff, tf):
    shift, scale, gate = mods
    for p0 in range(rs.start, rs.stop, PIECE):
        pr = slice(p0, min(p0 + PIECE, rs.stop))
        x = get_x(pr)
        x_scr[pr, :] = x
        h_scr[pr, :] = (_rms(x) * nw_ref[...] * (1.0 + scale) + shift).astype(BF16)
        yield True
    h = h_scr[rs, :]
    for c in range(d_ff // tf):
        g = _dot(h, w1_ref[:, c * tf:(c + 1) * tf].astype(BF16))
        u = _dot(h, w1_ref[:, d_ff + c * tf:d_ff + (c + 1) * tf].astype(BF16))
        a_ref[rs, c * tf:(c + 1) * tf] = (_silu(g) * u).astype(BF16)
        yield True
    y = x_scr[rs, :] + 0.5 * gate * _dot(a_ref[rs, :], w2_ref[...].astype(BF16))
    yield True
    put_y(y)
    yield True


def _sub_tiles(rows):
    sub = min(ROW_SUB, rows)
    return [slice(r0, r0 + sub) for r0 in range(0, rows, sub)]


def _ffn_lead(rs, d_ff, tf):
    return -(-(rs.stop - rs.start) // PIECE) + d_ff // tf // 2


def _ffn_kernel(x_ref, mod_ref, nw_ref, w1_ref, w2_ref, o_ref, a_ref, x_scr, h_scr, *, d_ff, tf):
    mods = [mod_ref[0, k:k + 1, :] for k in range(3)]

    def stages(rs):
        def put(y):
            o_ref[0, rs, :] = y
        return _Stages(_ffn_stages(lambda pr: x_ref[0, pr, :], put, mods, nw_ref, w1_ref, w2_ref, a_ref,
                                   x_scr, h_scr, rs, d_ff, tf), _ffn_lead(rs, d_ff, tf))

    _staggered([stages(rs) for rs in _sub_tiles(x_ref.shape[1])])


def _ffn(s, mods, mod_row, nw, w1, w2):
    bsz, t, d = s.shape
    d_ff = w2.shape[0]
    tm = min(TOKEN_TILE, t)
    const = lambda b, i: (0, 0)
    return pl.pallas_call(
        functools.partial(_ffn_kernel, d_ff=d_ff, tf=FF_TILE),
        grid=(bsz, t // tm),
        in_specs=[pl.BlockSpec((1, tm, d), lambda b, i: (b, i, 0)),
                  pl.BlockSpec((1, 3, d), lambda b, i: (mod_row(b), 0, 0)),
                  pl.BlockSpec((1, d), const),
                  pl.BlockSpec((d, 2 * d_ff), const, pipeline_mode=pl.Buffered(1)),
                  pl.BlockSpec((d_ff, d), const, pipeline_mode=pl.Buffered(1))],
        out_specs=pl.BlockSpec((1, tm, d), lambda b, i: (b, i, 0)),
        out_shape=jax.ShapeDtypeStruct((bsz, t, d), F32),
        scratch_shapes=[pltpu.VMEM((tm, d_ff), BF16),
                        pltpu.VMEM((tm, d), F32),
                        pltpu.VMEM((tm, d), BF16)],
        compiler_params=_cparams(("parallel", "parallel")),
        name="ffn",
    )(s, mods, nw, w1, w2)


def _inproj_kernel(x_ref, mod_ref, nw_ref, w_ref, wg_ref, gb_ref, lbl_ref,
                   hq_ref, hv_ref, hg_ref, hlf_ref, mqk_ref, mv_ref, mo_ref, gc_ref, *, hw):
    shift = mod_ref[0, 0:1, :]
    scale = mod_ref[0, 1:2, :]
    lbl = lbl_ref[...]
    e = jnp.exp(lbl - jnp.max(lbl, axis=0, keepdims=True))
    lb = e[0] / jnp.sum(e, axis=0)
    first_half = lax.broadcasted_iota(jnp.int32, (SUBLANE, LANE), 1) < CHUNK
    def stages(rs):
        h = (_rms(x_ref[0, rs, :]) * nw_ref[...] * (1.0 + scale) + shift).astype(BF16)
        yield True

        def proj(k):
            return _dot_nt(h, w_ref[k * hw:(k + 1) * hw, :].astype(BF16))

        hq_ref[0, rs, :] = _silu(proj(0)) * (HG_DIM ** -0.5)
        yield True
        hv_ref[0, rs, :] = proj(1).astype(BF16)
        yield True
        hg_ref[0, rs, :] = proj(2).astype(BF16)
        yield True
        for d in range(2):
            lbd = lb[d:d + 1, :]
            hlf_ref[0, d, rs, :] = jnp.log(lbd + (1.0 - lbd) * _sigmoid(proj(3 + d))) * LOG2E
            yield True
        mqk_ref[0, rs, :] = proj(5).astype(BF16)
        yield True
        mv_ref[0, rs, :] = proj(6).astype(BF16)
        yield True
        mo_ref[0, rs, :] = proj(7).astype(BF16)
        gt = _dot_nt(wg_ref[...], h) + gb_ref[...]
        for m in range((rs.stop - rs.start) // LANE):
            top = gt[0:8, m * LANE:(m + 1) * LANE]
            bot = gt[8:16, m * LANE:(m + 1) * LANE]
            c0 = rs.start // CHUNK + 2 * m
            gc_ref[0, c0] = jnp.where(first_half, top, pltpu.roll(bot, CHUNK, 1))
            gc_ref[0, c0 + 1] = jnp.where(first_half, pltpu.roll(top, CHUNK, 1), bot)
        yield True

    _staggered([_Stages(stages(rs), 8) for rs in _sub_tiles(x_ref.shape[1])])


def _inproj(s, mods, mod_row, nw, w_in, w_gate_t, gate_b, lb_logits):
    bsz, t, d = s.shape
    hw = HG_HEADS * HG_DIM
    ng = gate_b.shape[0]
    tm = min(INPROJ_TILE, t)
    tok4 = lambda b, i: (b, i, 0, 0)
    const = lambda b, i: (0, 0)
    tok = lambda b, i: (b, i, 0)
    tok2 = lambda b, i: (b, 0, i, 0)
    f = lambda dt, *shape: jax.ShapeDtypeStruct(shape, dt)
    return pl.pallas_call(
        functools.partial(_inproj_kernel, hw=hw),
        grid=(bsz, t // tm),
        in_specs=[pl.BlockSpec((1, tm, d), tok),
                  pl.BlockSpec((1, 2, d), lambda b, i: (mod_row(b), 0, 0)),
                  pl.BlockSpec((1, d), const),
                  pl.BlockSpec(w_in.shape, const, pipeline_mode=pl.Buffered(1)),
                  pl.BlockSpec(w_gate_t.shape, const),
                  pl.BlockSpec((ng, 1), const),
                  pl.BlockSpec(lb_logits.shape, lambda b, i: (0, 0, 0))],
        out_specs=[pl.BlockSpec((1, tm, hw), tok), pl.BlockSpec((1, tm, hw), tok),
                   pl.BlockSpec((1, tm, hw), tok),
                   pl.BlockSpec((1, 2, tm, hw), tok2),
                   pl.BlockSpec((1, tm, hw), tok), pl.BlockSpec((1, tm, hw), tok),
                   pl.BlockSpec((1, tm, hw), tok), pl.BlockSpec((1, tm // CHUNK, ng // 2, LANE), tok4)],
        out_shape=[f(F32, bsz, t, hw), f(BF16, bsz, t, hw), f(BF16, bsz, t, hw),
                   f(F32, bsz, 2, t, hw),
                   f(BF16, bsz, t, hw), f(BF16, bsz, t, hw), f(BF16, bsz, t, hw),
                   f(F32, bsz, t // CHUNK, ng // 2, LANE)],
        compiler_params=_cparams(("parallel", "parallel")),
        name="inproj",
    )(s, mods, nw, w_in, w_gate_t, gate_b, lb_logits)


def _conv_kernel(x_ref, w_ref, b_ref, o_ref, *, width, q_tiles):
    x = x_ref[0].astype(F32)
    t = x.shape[0]
    col = lax.broadcasted_iota(jnp.int32, x.shape, 0) % width
    xl = jnp.where(col == 0, 0.0, pltpu.roll(x, 1, 0))
    xr = jnp.where(col == width - 1, 0.0, pltpu.roll(x, t - 1, 0))

    def row(di):
        return (w_ref[di, 0:1, :] * xl + w_ref[di, 1:2, :] * x + w_ref[di, 2:3, :] * xr)

    y = row(1)
    if t > width:
        pad = jnp.zeros((width, x.shape[1]), F32)
        y = y + jnp.concatenate([pad, row(0)[:t - width]], axis=0)
        y = y + jnp.concatenate([row(2)[width:], pad], axis=0)
    y = _silu(y + b_ref[...])
    qscale = jnp.where(pl.program_id(1) < q_tiles, ML_QK ** -0.5, 1.0)
    o_ref[0] = (y * qscale).astype(BF16)


def _conv(x, w, b, width):
    bsz, t, ch = x.shape
    return pl.pallas_call(
        functools.partial(_conv_kernel, width=width, q_tiles=ML_HEADS * ML_QK // LANE),
        grid=(bsz, ch // LANE),
        in_specs=[pl.BlockSpec((1, t, LANE), lambda b_, c: (b_, 0, c)),
                  pl.BlockSpec((3, 3, LANE), lambda b_, c: (0, 0, c)),
                  pl.BlockSpec((1, LANE), lambda b_, c: (0, c))],
        out_specs=pl.BlockSpec((1, t, LANE), lambda b_, c: (b_, 0, c)),
        out_shape=jax.ShapeDtypeStruct((bsz, t, ch), BF16),
        compiler_params=_cparams(("parallel", "parallel")),
        name="conv",
    )(x, w, b)


def _positions():
    t = np.arange(CHUNK)
    return [t, CHUNK - 1 - t]


def _hgrn2_masks():
    ds, vs, ws = [], [], []
    for p in _positions():
        le = (p[None, :] <= p[:, None]).astype(np.float32)
        order = [p[SUB * j] // SUB for j in range(N_SUB)]
        ref = np.zeros((32, CHUNK), np.float32)
        valid = np.zeros((N_SUB * N_SUB, 1), np.float32)
        for j in range(N_SUB):
            ref[j] = p <= SUB * order[j] + SUB // 2 - 1
            ref[8 + j] = p <= SUB * order[j] + SUB - 1
        ref[N_SUB] = 1.0
        for j in range(N_SUB):
            for i in range(N_SUB):
                if order[i] < order[j]:
                    ref[16 + N_SUB * j + i] = ref[j] - ref[8 + i]
                    valid[N_SUB * j + i] = 1.0
        ds.append(np.concatenate([le, ref], axis=0))
        vs.append(np.broadcast_to(valid, (N_SUB * N_SUB, HG_HEADS * HG_DIM)))
        tok_blk = np.arange(CHUNK)[:, None] // SUB
        ws.append(np.concatenate([(tok_blk == j) * le for j in range(N_SUB)], axis=1))
    return (np.stack(ds).astype(np.float32), np.stack(vs).astype(np.float32),
            np.stack(ws).astype(np.float32))


def _mlstm_masks():
    le2s, let2s = [], []
    for p in _positions():
        le = (p[None, :] <= p[:, None]).astype(np.float32)
        le2s.append(np.concatenate([le, le], axis=1))
        let2 = np.zeros((2 * CHUNK, 2 * CHUNK), np.float32)
        let2[:CHUNK, :CHUNK] = le.T
        let2[CHUNK:, CHUNK:] = le.T
        let2s.append(let2)
    return np.stack(le2s), np.stack(let2s)


def _chunk_groups(nc, group):
    bounds = [0] + list(range(max(1, group // 2), nc, group)) + [nc]
    return [list(range(a, b)) for a, b in zip(bounds[:-1], bounds[1:]) if b > a]


def _pipelined(chunk_groups, stages_of, step_of):
    queue = []
    for chunks in chunk_groups:
        for _ in stages_of(chunks):
            if queue:
                queue.pop(0)()
            yield True
        while queue:
            queue.pop(0)()
            yield True
        queue = [functools.partial(step_of, c) for c in chunks]
    for step in queue:
        step()
        yield True


def _interleave(programs):
    live = list(programs)
    while live:
        live = [p for p in live if next(p, None) is not None]


def _chunk_rows(c, nc, direction):
    cc = jnp.where(direction == 0, c, nc - 1 - c)
    return pl.multiple_of(cc * CHUNK, CHUNK)


def _hgrn2_program(q_ref, v_ref, lf_ref, d_ref, pv_ref, w_ref, s0_ref, o_ref, st_ref,
                   u_ref, qin_ref, g_ref, a_ref, vt_ref, bx_ref, kx_ref, *, nc):
    direction = pl.program_id(1)

    @pl.when(pl.program_id(2) == 0)
    def _():
        st_ref[...] = s0_ref[...]

    dmat = d_ref[0].astype(BF16)
    pair_valid = pv_ref[0]
    wmask = w_ref[0] > 0.0
    L = CHUNK
    hw = q_ref.shape[-1]
    upper_rows = lax.broadcasted_iota(jnp.int32, (L, 2 * L), 0) < L // 2

    def intra(cs):
        group = len(cs)
        rows = [pl.ds(_chunk_rows(c, nc, direction), L) for c in cs]
        items = [(i, h) for i in range(group) for h in range(HG_HEADS)]
        lanes = [slice(h * HG_DIM, (h + 1) * HG_DIM) for h in range(HG_HEADS)]
        lfs = [lf_ref[0, 0, r, :] for r in rows]
        parts = [_split2(lf) for lf in lfs]
        e_all = sum(_dot(dmat, jnp.concatenate([parts[i][n] for i in range(group)], axis=1))
                    for n in range(2))
        yield
        q_mids, k_mids, k_outs, vbs, spans = [], [], [], [], []
        for i in range(group):
            e = e_all[:, i * hw:(i + 1) * hw]
            q = q_ref[0, rows[i], :]
            k = 1.0 - jnp.exp2(lfs[i])
            b = e[0:L]
            b_tot = e[L + N_SUB:L + N_SUB + 1]
            sub_rows = lambda r0: jnp.concatenate(
                [jnp.broadcast_to(e[r0 + j:r0 + j + 1], (SUB, hw)) for j in range(N_SUB)], axis=0)
            mid_rows = sub_rows(L)
            end_rows = sub_rows(L + 8)
            to_mid = b - mid_rows
            spans.append(jnp.max(jnp.abs(to_mid)))
            q_mids.append((q * jnp.exp2(jnp.minimum(to_mid, EXP2_CLAMP))).astype(BF16))
            k_diag = k * jnp.exp2(jnp.minimum(-to_mid, EXP2_CLAMP))
            k_end = k * jnp.exp2(end_rows - b)
            cross = jnp.exp2(e[L + 16:L + 32]) * pair_valid
            slabs = []
            for j in range(N_SUB):
                for ib in range(N_SUB):
                    rs = slice(ib * SUB, (ib + 1) * SUB)
                    row = N_SUB * j + ib
                    slabs.append(k_diag[rs] if ib == j else k_end[rs] * cross[row:row + 1])
            k_mids.append(jnp.concatenate(slabs, axis=0).astype(BF16))
            k_outs.append((k * jnp.exp2(b_tot - b)).astype(BF16))
            vbs.append(v_ref[0, rows[i], :])
            qin_ref[cs[i]] = (q * jnp.exp2(b)).astype(BF16)
            g_ref[cs[i]] = jnp.broadcast_to(jnp.exp2(b_tot), (SUBLANE, hw))
            yield
        scores = [_dot_nt(q_mids[i][:, lanes[h]], k_mids[i][:, lanes[h]]) for i, h in items]
        yield
        a_s = [jnp.where(wmask, s, 0.0) for s in scores]
        a_s = [jnp.where(upper_rows, a[:, :2 * L], a[:, 2 * L:]).astype(BF16) for a in a_s]
        for i in range(group):
            a_ref[cs[i]] = jnp.concatenate(a_s[i * HG_HEADS:(i + 1) * HG_HEADS], axis=1)
        yield
        for i, h in items:
            v_h = vbs[i][:, lanes[h]].astype(F32)
            vt_ref[cs[i], h] = jnp.concatenate([v_h, v_h], axis=0).T.astype(BF16)
        yield
        for i, h in items:
            u_ref[cs[i], h] = _dot_tn(vbs[i][:, lanes[h]], k_outs[i][:, lanes[h]])

        @pl.when(functools.reduce(jnp.maximum, spans) > EXP2_CLAMP)
        def _():
            for i in range(group):
                exact_intra(cs[i], rows[i])
        yield

    def exact_intra(c, rows):
        lf = lf_ref[0, 0, rows, :]
        bx_ref[...] = sum(_dot(dmat[0:L], x) for x in _split3(lf))
        kx_ref[...] = 1.0 - jnp.exp2(lf)
        q = q_ref[0, rows, :]
        tok = lax.broadcasted_iota(jnp.int32, (L, 1), 0)
        pos = jnp.where(direction == 0, tok, L - 1 - tok)
        lane_in_head = lax.broadcasted_iota(jnp.int32, (1, hw), 1) % HG_DIM

        def add_source(s, acc):
            b_s = bx_ref[pl.ds(s, 1), :]
            z = q * kx_ref[pl.ds(s, 1), :] * jnp.exp2(jnp.minimum(bx_ref[...] - b_s, 0.0))
            seen = pos >= jnp.where(direction == 0, s, L - 1 - s)
            cols = [jnp.broadcast_to(jnp.where(seen, jnp.sum(z[:, ln], axis=1, keepdims=True), 0.0), (L, HG_DIM))
                    for ln in (slice(h * HG_DIM, (h + 1) * HG_DIM) for h in range(HG_HEADS))]
            return jnp.where(lane_in_head == s, jnp.concatenate(cols, axis=1), acc)

        a_ref[c] = lax.fori_loop(0, L, add_source, jnp.zeros((L, hw), F32)).astype(BF16)

    def carry_state(c):
        rows = pl.ds(_chunk_rows(c, nc, direction), L)
        q_in = qin_ref[c]
        a = a_ref[c]
        g = g_ref[c][0:1]
        outs, states = [], []
        for h in range(HG_HEADS):
            ln = slice(h * HG_DIM, (h + 1) * HG_DIM)
            st = st_ref[0, 0, h]
            outs.append(_dot_nt(jnp.concatenate([a[:, ln], q_in[:, ln]], axis=1),
                                jnp.concatenate([vt_ref[c, h], st.astype(BF16)], axis=1)))
            states.append(g[:, ln] * st + u_ref[c, h])
        o_ref[0, 0, rows, :] = jnp.concatenate(outs, axis=1).astype(BF16)
        for h in range(HG_HEADS):
            st_ref[0, 0, h] = states[h]

    return _pipelined(_chunk_groups(nc, GROUP), intra, carry_state)


N_HG = (7, 2, 7)
N_ML = (7, 3, 5)


def _mlstm_program(qk_ref, v_ref, g_ref, le2_ref, let2_ref, c0_ref, m0_ref,
                   o_ref, c_ref, mm_ref,
                   al_ref, u_ref, ct_ref, mloc_ref, sc_ref, *, nc):
    direction = pl.program_id(1)

    @pl.when(pl.program_id(2) == 0)
    def _():
        c_ref[...] = c0_ref[...]
        mm_ref[...] = m0_ref[...]

    let2 = let2_ref[0].astype(BF16)
    causal = le2_ref[0] > 0.0
    L = CHUNK
    neg = -jnp.inf
    lane = lax.broadcasted_iota(jnp.int32, (1, LANE), 1)
    seg = [lane < L, lane >= L]
    lane_t = lax.broadcasted_iota(jnp.int32, (L, LANE), 1) < L
    npair = ML_HEADS // 2
    ones_v = jnp.ones((L, ML_V), BF16)
    zb = jnp.zeros((L, LANE), BF16)

    def by_head(x):
        return jnp.concatenate([jnp.where(lane_t, x, zb), jnp.where(lane_t, zb, x)], axis=0)

    def v_stack(vb, p):
        return jnp.concatenate(
            [jnp.concatenate([vb[:, 2 * p * ML_V:(2 * p + 1) * ML_V], ones_v], axis=1),
             jnp.concatenate([vb[:, (2 * p + 1) * ML_V:(2 * p + 2) * ML_V], ones_v], axis=1)], axis=0)

    def intra(cs):
        group = len(cs)
        ccs = [jnp.where(direction == 0, c, nc - 1 - c) for c in cs]
        rows = [pl.ds(pl.multiple_of(cc * L, L), L) for cc in ccs]
        pairs = [(i, p) for i in range(group) for p in range(npair)]
        gs = [jnp.where(direction == 0, g_ref[0, cc][0:4], g_ref[0, cc][4:8]) for cc in ccs]
        pad = [jnp.zeros((-2 * group % SUBLANE, LANE), F32)] if 2 * group % SUBLANE else []
        i2 = jnp.concatenate([g[0:2] for g in gs] + pad, axis=0)
        lf2 = _log_sigmoid(jnp.concatenate([g[2:4] for g in gs] + pad, axis=0))
        parts = _split3(lf2)
        b2 = sum(_dot(x, let2) for x in parts)
        r2 = i2 - b2
        g_a = jnp.max(jnp.where(seg[0], r2, neg), axis=1, keepdims=True)
        g_b = jnp.max(jnp.where(seg[1], r2, neg), axis=1, keepdims=True)
        e_w = jnp.exp(r2 - jnp.where(seg[0], g_a, g_b))
        bt_a = jnp.sum(jnp.where(seg[0], lf2, 0.0), axis=1, keepdims=True)
        bt_b = jnp.sum(jnp.where(seg[1], lf2, 0.0), axis=1, keepdims=True)
        yield

        def as_columns(row):
            cols = jnp.broadcast_to(row, (LANE, LANE)).T
            return jnp.where(lane_t, cols[0:L], cols[L:2 * L])

        b_ts = [as_columns(b2[2 * i + p:2 * i + p + 1]) for i, p in pairs]
        for i in range(group):
            sc_ref[cs[i]] = jnp.concatenate(
                [jnp.broadcast_to(x[2 * i + p:2 * i + p + 1], (1, LANE))
                 for x2 in ((bt_a, bt_b), (g_a, g_b)) for p in range(npair) for x in x2], axis=0)
        yield
        qks = [qk_ref[0, r, :] for r in rows]
        vbs = [v_ref[0, r, :] for r in rows]
        q_ps = [qks[i][:, p * LANE:(p + 1) * LANE] for i, p in pairs]
        k_ps = [qks[i][:, (npair + p) * LANE:(npair + p + 1) * LANE] for i, p in pairs]
        kbs = [by_head(k) for k in k_ps]
        scores = [_dot_nt(q, kb) for q, kb in zip(q_ps, kbs)]
        yield
        for n, (i, p) in enumerate(pairs):
            t1 = b_ts[n]
            d_log = jnp.where(causal, t1 + r2[2 * i + p:2 * i + p + 1], neg)
            m_a = jnp.max(jnp.where(lane_t, d_log, neg), axis=1, keepdims=True)
            m_b = jnp.max(jnp.where(lane_t, neg, d_log), axis=1, keepdims=True)
            m_loc = jnp.where(lane_t, m_a, m_b)
            ct_ref[cs[i], p] = t1
            mloc_ref[cs[i], p] = m_loc
            al_ref[cs[i], p] = scores[n] * jnp.exp(d_log - m_loc)
        yield
        v_stacks = [v_stack(vbs[i], p) for i, p in pairs]
        yield
        kts = [(kb.astype(F32).T * e_w[2 * i + p:2 * i + p + 1]).astype(BF16) for kb, (i, p) in zip(kbs, pairs)]
        for n, (i, p) in enumerate(pairs):
            u_ref[cs[i], p] = _dot(kts[n], v_stacks[n])
        yield

    def carry_state(c):
        cc = jnp.where(direction == 0, c, nc - 1 - c)
        rows = pl.ds(pl.multiple_of(cc * L, L), L)
        m_all = mm_ref[0, 0]
        sc = sc_ref[c]
        qk = qk_ref[0, rows, :]
        vb = v_ref[0, rows, :]
        outs, states, m_rows = [], [], []
        for p in range(ML_HEADS // 2):
            c_pair = c_ref[0, 0, p]
            inc = u_ref[c, p]
            b_t = ct_ref[c, p]
            m_loc = mloc_ref[c, p]
            m_in2 = jnp.where(seg[0], m_all[2 * p:2 * p + 1], m_all[2 * p + 1:2 * p + 2])
            m_t = jnp.maximum(m_loc, b_t + m_in2)
            lhs_a = (jnp.exp(m_loc - m_t) * al_ref[c, p]).astype(BF16)
            lhs_q = (jnp.exp(b_t + m_in2 - m_t) * qk[:, p * LANE:(p + 1) * LANE].astype(F32)).astype(BF16)
            lhs = jnp.concatenate([by_head(lhs_a), by_head(lhs_q)], axis=1)
            tot = _dot(lhs, jnp.concatenate([v_stack(vb, p), c_pair.astype(BF16)], axis=0))
            floor = jnp.exp(-m_t)
            floor_r = pltpu.roll(floor, L, 1)
            new_rows = []
            for hh in range(2):
                h = 2 * p + hh
                rs = slice(hh * L, (hh + 1) * L)
                m_in = m_all[h:h + 1]
                floor_h = jnp.where(lane_t, floor, floor_r) if hh == 0 else jnp.where(lane_t, floor_r, floor)
                outs.append(tot[rs, :ML_V] / jnp.maximum(jnp.abs(tot[rs, ML_V:]), floor_h))
                b_tot = sc[h:h + 1]
                m_new = b_tot + jnp.maximum(m_in, sc[ML_HEADS + h:ML_HEADS + h + 1])
                carry_w = jnp.exp(b_tot + m_in - m_new)
                inc_w = jnp.exp(b_tot + sc[ML_HEADS + h:ML_HEADS + h + 1] - m_new)
                cw2 = jnp.concatenate([carry_w, carry_w], axis=1)
                iw2 = jnp.concatenate([inc_w, inc_w], axis=1)
                new_rows.append(cw2 * c_pair[rs] + iw2 * inc[rs])
                m_rows.append(m_new)
            states.append(jnp.concatenate(new_rows, axis=0))
        o_ref[0, 0, rows, :] = jnp.concatenate(outs, axis=1).astype(BF16)
        for p in range(ML_HEADS // 2):
            c_ref[0, 0, p] = states[p]
        mm_ref[0, 0] = jnp.concatenate(m_rows + [m_all[ML_HEADS:]], axis=0)

    return _pipelined(_chunk_groups(nc, ML_GROUP), intra, carry_state)


def _scan_kernel(*refs, nc):
    hg_in, ml_in, hg_out, ml_out, hg_scr, ml_scr = (
        refs[a:b] for a, b in zip(np.cumsum([0, N_HG[0], N_ML[0], N_HG[1], N_ML[1], N_HG[2]]),
                                  np.cumsum([N_HG[0], N_ML[0], N_HG[1], N_ML[1], N_HG[2], N_ML[2]])))
    _interleave([_hgrn2_program(*hg_in, *hg_out, *hg_scr, nc=nc),
                 _mlstm_program(*ml_in, *ml_out, *ml_scr, nc=nc)])


def _scans(hq, hv, hlf, qk, mv, gates, s0, c0, m0, dmat, pair_valid, wmask, le2, let2):
    bsz, t, hw = hq.shape
    nc = min(SCAN_CHUNKS, t // CHUNK)
    tb = nc * CHUNK
    nblk = t // tb
    npair = ML_HEADS // 2

    def blk(j, d):
        return jnp.where(d == 0, j, nblk - 1 - j)

    tok = lambda b, d, j: (b, blk(j, d), 0)
    tok_d = lambda b, d, j: (b, d, blk(j, d), 0)
    per_dir = lambda a: pl.BlockSpec((1,) + a.shape[1:], lambda b, d, j: (d,) + (0,) * (a.ndim - 1))
    state = lambda a: pl.BlockSpec((1, 1) + a.shape[2:], lambda b, d, j: (b, d) + (0,) * (a.ndim - 2))
    out_tok = pl.BlockSpec((1, 1, tb, hw), tok_d)
    hg_in = [pl.BlockSpec((1, tb, hw), tok), pl.BlockSpec((1, tb, hw), tok), pl.BlockSpec((1, 1, tb, hw), tok_d),
             per_dir(dmat), per_dir(pair_valid), per_dir(wmask), state(s0)]
    ml_in = [pl.BlockSpec((1, tb, qk.shape[-1]), tok), pl.BlockSpec((1, tb, hw), tok),
             pl.BlockSpec((1, nc) + gates.shape[2:], lambda b, d, j: (b, blk(j, d), 0, 0)),
             per_dir(le2), per_dir(let2), state(c0), state(m0)]
    sds = jax.ShapeDtypeStruct
    hg_scr = [pltpu.VMEM((nc,) + s0.shape[2:], F32),
              pltpu.VMEM((nc, CHUNK, hw), BF16),
              pltpu.VMEM((nc, SUBLANE, hw), F32),
              pltpu.VMEM((nc, CHUNK, hw), BF16),
              pltpu.VMEM((nc,) + s0.shape[2:], BF16),
              pltpu.VMEM((CHUNK, hw), F32),
              pltpu.VMEM((CHUNK, hw), F32)]
    ml_scr = [pltpu.VMEM((nc, npair, CHUNK, LANE), F32),
              pltpu.VMEM((nc, npair, 2 * ML_QK, 2 * ML_V), F32),
              pltpu.VMEM((nc, npair, CHUNK, LANE), F32),
              pltpu.VMEM((nc, npair, CHUNK, LANE), F32),
              pltpu.VMEM((nc, SUBLANE, LANE), F32)]
    assert (len(hg_in), 2, len(hg_scr)) == N_HG and (len(ml_in), 3, len(ml_scr)) == N_ML
    return pl.pallas_call(
        functools.partial(_scan_kernel, nc=nc),
        grid=(bsz, 2, nblk),
        in_specs=hg_in + ml_in,
        out_specs=[out_tok, state(s0), out_tok, state(c0), state(m0)],
        out_shape=[sds((bsz, 2, t, hw), BF16), sds(s0.shape, F32),
                   sds((bsz, 2, t, hw), BF16), sds(c0.shape, F32), sds(m0.shape, F32)],
        scratch_shapes=hg_scr + ml_scr,
        compiler_params=_cparams(("parallel", "parallel", "arbitrary")),
        name="scans",
    )(hq, hv, hlf, dmat, pair_valid, wmask, s0, qk, mv, gates, le2, let2, c0, m0)


def _head_rms(x, heads):
    dh = x.shape[-1] // heads
    return jnp.concatenate([_rms(x[:, h * dh:(h + 1) * dh]) for h in range(heads)], axis=1)


def _mix_ffn_kernel(oh_ref, om_ref, hg_ref, mo_ref, x_ref, mod_ref, hnw_ref, mnw_ref, wo_ref,
                    nw_ref, w1_ref, w2_ref, fw_ref, o_ref, a_ref, x_scr, h_scr, *, d_ff, tf):
    mix_gate = mod_ref[0, 0:1, :]
    mods = [mod_ref[0, k:k + 1, :] for k in range(1, 4)]

    def stages(rs):
        def mixed(pr):
            oh = oh_ref[0, 0, pr, :].astype(F32) + oh_ref[0, 1, pr, :].astype(F32)
            om = om_ref[0, 0, pr, :].astype(F32) + om_ref[0, 1, pr, :].astype(F32)
            hg_out = _head_rms(oh, HG_HEADS) * hnw_ref[...] * _silu(hg_ref[0, pr, :].astype(F32))
            ml_out = _sigmoid(mo_ref[0, pr, :].astype(F32)) * (_head_rms(om, ML_HEADS) * mnw_ref[...])
            merged = jnp.concatenate([hg_out, ml_out], axis=1).astype(BF16)
            return x_ref[0, pr, :] + mix_gate * _dot(merged, wo_ref[...])

        def put(y):
            o_ref[0, rs, :] = _rms(y) * fw_ref[...]

        return _Stages(_ffn_stages(mixed, put, mods, nw_ref, w1_ref, w2_ref, a_ref, x_scr, h_scr,
                                   rs, d_ff, tf), _ffn_lead(rs, d_ff, tf))

    _staggered([stages(rs) for rs in _sub_tiles(x_ref.shape[1])])


def _mix_ffn(oh, om, hg, mo, x, mods, hnw, mnw, w_out, nw, w1, w2, fw):
    bsz, t, d = x.shape
    hw = hg.shape[-1]
    d_ff = w2.shape[0]
    tm = min(TOKEN_TILE, t)
    tok = lambda b, i: (b, i, 0)
    tok2 = lambda b, i: (b, 0, i, 0)
    const = lambda b, i: (0, 0)
    resident = lambda a: pl.BlockSpec(a.shape, const, pipeline_mode=pl.Buffered(1))
    return pl.pallas_call(
        functools.partial(_mix_ffn_kernel, d_ff=d_ff, tf=FF_TILE),
        grid=(bsz, t // tm),
        in_specs=[pl.BlockSpec((1, 2, tm, hw), tok2), pl.BlockSpec((1, 2, tm, hw), tok2),
                  pl.BlockSpec((1, tm, hw), tok), pl.BlockSpec((1, tm, hw), tok),
                  pl.BlockSpec((1, tm, d), tok),
                  pl.BlockSpec((1, 4, d), lambda b, i: (b, 0, 0)),
                  pl.BlockSpec((1, hw), const), pl.BlockSpec((1, hw), const), resident(w_out),
                  pl.BlockSpec((1, d), const), resident(w1), resident(w2), pl.BlockSpec((1, d), const)],
        out_specs=pl.BlockSpec((1, tm, d), tok),
        out_shape=jax.ShapeDtypeStruct((bsz, t, d), F32),
        scratch_shapes=[pltpu.VMEM((tm, d_ff), BF16),
                        pltpu.VMEM((tm, d), F32),
                        pltpu.VMEM((tm, d), BF16)],
        compiler_params=_cparams(("parallel", "parallel")),
        name="mix_ffn_final",
    )(oh, om, hg, mo, x, mods, hnw, mnw, w_out, nw, w1, w2, fw)


def kernel(x, c, ctx, c_ctx, w_mod, b_mod, norm1_w, ffn1_w1, ffn1_w2, norm2_w, w_in, ml_gate_b,
           ml_conv_w, ml_conv_b, hg_lb_logits, hg_norm_w, ml_norm_w, w_out, norm3_w, ffn2_w1, ffn2_w2,
           final_norm_w):
    bsz, seq, d = x.shape
    n_ctx = ctx.shape[1]
    assert w_mod.shape[0] == 1, "single-layer kernel"
    assert seq % max(TOKEN_TILE, SCAN_CHUNKS * CHUNK) == 0 and n_ctx % CHUNK == 0 and GRID_W == CHUNK
    hw = HG_HEADS * HG_DIM
    ng = ml_gate_b.shape[-1]

    rows = -(-(bsz + 1) // SUBLANE) * SUBLANE
    cvec = jnp.zeros((rows, d), F32).at[:bsz].set(c).at[bsz].set(c_ctx)
    mods = _modulation(cvec, w_mod[0], b_mod[0][None, :]).reshape(rows, N_MOD, d)
    lat_row = lambda b: b
    ctx_row = lambda b: bsz

    row = lambda a: a.reshape(1, -1)
    w1a, w2a = ffn1_w1[0], ffn1_w2[0]
    x1 = _ffn(x, mods[:, 0:3], lat_row, row(norm1_w[0]), w1a, w2a)
    s1 = _ffn(ctx, mods[:, 0:3], ctx_row, row(norm1_w[0]), w1a, w2a)

    w_in_b = w_in[0].T
    gate_order = np.arange(ng).reshape(2, 2, ML_HEADS // 2, 2).transpose(3, 0, 1, 2).reshape(-1)
    w_gate_t = w_in[0][:, 8 * hw:].T[gate_order].astype(BF16)
    gate_b = ml_gate_b[0][gate_order].reshape(ng, 1)
    dmat, pair_valid, wmask = (jnp.asarray(a) for a in _hgrn2_masks())
    le2, let2 = (jnp.asarray(a) for a in _mlstm_masks())
    conv_w, conv_b = ml_conv_w[0], row(ml_conv_b[0])
    npair = ML_HEADS // 2

    def mixer_scans(s, mod_row, width, s0, c0, m0):
        hq, hv, hg, hlf, mqk, mv, mo, gates = _inproj(
            s, mods[:, 3:5], mod_row, row(norm2_w[0]), w_in_b, w_gate_t, gate_b, hg_lb_logits)
        qk = _conv(mqk, conv_w, conv_b, width)
        oh, s_fin, om, c_fin, m_fin = _scans(hq, hv, hlf, qk, mv, gates, s0, c0, m0,
                                             dmat, pair_valid, wmask, le2, let2)
        return oh, om, hg, mo, s_fin, c_fin, m_fin

    s0 = jnp.zeros((bsz, 2, HG_HEADS, HG_DIM, HG_DIM), F32)
    c0 = jnp.zeros((bsz, 2, npair, 2 * ML_QK, 2 * ML_V), F32)
    m0 = jnp.zeros((bsz, 2, SUBLANE, LANE), F32)
    _, _, _, _, s_ctx, c_ctx_state, m_ctx = mixer_scans(s1, ctx_row, n_ctx, s0, c0, m0)
    oh, om, hg, mo, _, _, _ = mixer_scans(x1, lat_row, GRID_W, s_ctx, c_ctx_state, m_ctx)

    return _mix_ffn(oh, om, hg, mo, x1, mods[:, 5:9], row(hg_norm_w[0]), row(ml_norm_w[0]),
                    w_out[0].astype(BF16), row(norm3_w[0]), ffn2_w1[0], ffn2_w2[0], row(final_norm_w))
```

```python
import functools

import numpy as np
import jax
import jax.numpy as jnp
from jax import lax
from jax.experimental import pallas as pl
from jax.experimental.pallas import tpu as pltpu

F32 = jnp.float32
BF16 = jnp.bfloat16

EPS = 1e-6
CHUNK = 64
GRID_W = 64
N_MOD = 9
HG_HEADS = 4
HG_DIM = 128
ML_HEADS = 4
ML_QK = 64
ML_V = 128
SUB = 16
N_SUB = CHUNK // SUB
GROUP = 4
ML_GROUP = 4
SCAN_CHUNKS = 16
TOKEN_TILE = 512
INPROJ_TILE = 1024
ROW_SUB = 256
FF_TILE = 256
PIECE = 128
LOG2E = 1.4426950408889634
EXP2_CLAMP = 115.0
LANE = 128
SUBLANE = 8
VMEM_LIMIT = 56 * 1024 * 1024


def _cparams(sem):
    return pltpu.CompilerParams(dimension_semantics=sem, vmem_limit_bytes=VMEM_LIMIT)


def _dot(a, b):
    return jnp.dot(a, b, preferred_element_type=F32)


def _dot_nt(a, b):
    return lax.dot_general(a, b, (((1,), (1,)), ((), ())), preferred_element_type=F32)


def _dot_tn(a, b):
    return lax.dot_general(a, b, (((0,), (0,)), ((), ())), preferred_element_type=F32)


def _sigmoid(x):
    return 1.0 / (1.0 + jnp.exp(-x))


def _silu(x):
    return x * _sigmoid(x)


def _log_sigmoid(x):
    return jnp.minimum(x, 0.0) - jnp.log(1.0 + jnp.exp(-jnp.abs(x)))


def _split3(x):
    hi = x.astype(BF16)
    r = x - hi.astype(F32)
    mid = r.astype(BF16)
    lo = (r - mid.astype(F32)).astype(BF16)
    return hi, mid, lo


def _split2(x):
    hi = x.astype(BF16)
    return hi, (x - hi.astype(F32)).astype(BF16)


def _rms(x):
    return x * lax.rsqrt(jnp.mean(x * x, axis=-1, keepdims=True) + EPS)


def _mod_kernel(c_ref, w_ref, b_ref, o_ref):
    a = _silu(c_ref[...])
    a_hi = a.astype(BF16)
    a_lo = (a - a_hi.astype(F32)).astype(BF16)
    w = w_ref[...]
    w_hi = w.astype(BF16)
    w_lo = (w - w_hi.astype(F32)).astype(BF16)
    o_ref[...] = _dot(a_hi, w_hi) + _dot(a_hi, w_lo) + _dot(a_lo, w_hi) + b_ref[...]


def _modulation(cvec, w_mod, b_mod):
    rows, d = cvec.shape
    n = w_mod.shape[1]
    tn = 1024
    return pl.pallas_call(
        _mod_kernel,
        grid=(n // tn,),
        in_specs=[pl.BlockSpec((rows, d), lambda j: (0, 0)),
                  pl.BlockSpec((d, tn), lambda j: (0, j)),
                  pl.BlockSpec((1, tn), lambda j: (0, j))],
        out_specs=pl.BlockSpec((rows, tn), lambda j: (0, j)),
        out_shape=jax.ShapeDtypeStruct((rows, n), F32),
        compiler_params=_cparams(("arbitrary",)),
        name="modulation",
    )(cvec, w_mod, b_mod)


def _staggered(tiles):
    waiting = list(tiles)
    live = []
    while waiting or live:
        if waiting and (not live or live[-1][1] >= live[-1][2]):
            t = waiting.pop(0)
            live.append([t.gen, 0, t.lead])
        for entry in live:
            entry[1] += 1
        live = [e for e in live if next(e[0], None) is not None]


class _Stages:
    def __init__(self, gen, lead):
        self.gen, self.lead = gen, lead


def _ffn_stages(get_x, put_y, mods, nw_ref, w1_ref, w2_ref, a_ref, x_scr, h_scr, rs, d_ff, tf):
    shift, scale, gate = mods
    for p0 in range(rs.start, rs.stop, PIECE):
        pr = slice(p0, min(p0 + PIECE, rs.stop))
        x = get_x(pr)
        x_scr[pr, :] = x
        h_scr[pr, :] = (_rms(x) * nw_ref[...] * (1.0 + scale) + shift).astype(BF16)
        yield True
    h = h_scr[rs, :]
    for c in range(d_ff // tf):
        g = _dot(h, w1_ref[:, c * tf:(c + 1) * tf].astype(BF16))
        u = _dot(h, w1_ref[:, d_ff + c * tf:d_ff + (c + 1) * tf].astype(BF16))
        a_ref[rs, c * tf:(c + 1) * tf] = (_silu(g) * u).astype(BF16)
        yield True
    y = x_scr[rs, :] + 0.5 * gate * _dot(a_ref[rs, :], w2_ref[...].astype(BF16))
    yield True
    put_y(y)
    yield True


def _sub_tiles(rows):
    sub = min(ROW_SUB, rows)
    return [slice(r0, r0 + sub) for r0 in range(0, rows, sub)]


def _ffn_lead(rs, d_ff, tf):
    return -(-(rs.stop - rs.start) // PIECE) + d_ff // tf // 2


def _ffn_kernel(x_ref, mod_ref, nw_ref, w1_ref, w2_ref, o_ref, a_ref, x_scr, h_scr, *, d_ff, tf):
    mods = [mod_ref[0, k:k + 1, :] for k in range(3)]

    def stages(rs):
        def put(y):
            o_ref[0, rs, :] = y
        return _Stages(_ffn_stages(lambda pr: x_ref[0, pr, :], put, mods, nw_ref, w1_ref, w2_ref, a_ref,
                                   x_scr, h_scr, rs, d_ff, tf), _ffn_lead(rs, d_ff, tf))

    _staggered([stages(rs) for rs in _sub_tiles(x_ref.shape[1])])


def _ffn(s, mods, mod_row, nw, w1, w2):
    bsz, t, d = s.shape
    d_ff = w2.shape[0]
    tm = min(TOKEN_TILE, t)
    const = lambda b, i: (0, 0)
    return pl.pallas_call(
        functools.partial(_ffn_kernel, d_ff=d_ff, tf=FF_TILE),
        grid=(bsz, t // tm),
        in_specs=[pl.BlockSpec((1, tm, d), lambda b, i: (b, i, 0)),
                  pl.BlockSpec((1, 3, d), lambda b, i: (mod_row(b), 0, 0)),
                  pl.BlockSpec((1, d), const),
                  pl.BlockSpec((d, 2 * d_ff), const, pipeline_mode=pl.Buffered(1)),
                  pl.BlockSpec((d_ff, d), const, pipeline_mode=pl.Buffered(1))],
        out_specs=pl.BlockSpec((1, tm, d), lambda b, i: (b, i, 0)),
        out_shape=jax.ShapeDtypeStruct((bsz, t, d), F32),
        scratch_shapes=[pltpu.VMEM((tm, d_ff), BF16),
                        pltpu.VMEM((tm, d), F32),
                        pltpu.VMEM((tm, d), BF16)],
        compiler_params=_cparams(("parallel", "parallel")),
        name="ffn",
    )(s, mods, nw, w1, w2)


def _inproj_kernel(x_ref, mod_ref, nw_ref, w_ref, wg_ref, gb_ref, lbl_ref,
                   hq_ref, hv_ref, hg_ref, hlf_ref, mqk_ref, mv_ref, mo_ref, gc_ref, *, hw):
    shift = mod_ref[0, 0:1, :]
    scale = mod_ref[0, 1:2, :]
    lbl = lbl_ref[...]
    e = jnp.exp(lbl - jnp.max(lbl, axis=0, keepdims=True))
    lb = e[0] / jnp.sum(e, axis=0)
    first_half = lax.broadcasted_iota(jnp.int32, (SUBLANE, LANE), 1) < CHUNK
    def stages(rs):
        h = (_rms(x_ref[0, rs, :]) * nw_ref[...] * (1.0 + scale) + shift).astype(BF16)
        yield True

        def proj(k):
            return _dot_nt(h, w_ref[k * hw:(k + 1) * hw, :].astype(BF16))

        hq_ref[0, rs, :] = _silu(proj(0)) * (HG_DIM ** -0.5)
        yield True
        hv_ref[0, rs, :] = proj(1).astype(BF16)
        yield True
        hg_ref[0, rs, :] = proj(2).astype(BF16)
        yield True
        for d in range(2):
            lbd = lb[d:d + 1, :]
            hlf_ref[0, d, rs, :] = jnp.log(lbd + (1.0 - lbd) * _sigmoid(proj(3 + d))) * LOG2E
            yield True
        mqk_ref[0, rs, :] = proj(5).astype(BF16)
        yield True
        mv_ref[0, rs, :] = proj(6).astype(BF16)
        yield True
        mo_ref[0, rs, :] = proj(7).astype(BF16)
        gt = _dot_nt(wg_ref[...], h) + gb_ref[...]
        for m in range((rs.stop - rs.start) // LANE):
            top = gt[0:8, m * LANE:(m + 1) * LANE]
            bot = gt[8:16, m * LANE:(m + 1) * LANE]
            c0 = rs.start // CHUNK + 2 * m
            gc_ref[0, c0] = jnp.where(first_half, top, pltpu.roll(bot, CHUNK, 1))
            gc_ref[0, c0 + 1] = jnp.where(first_half, pltpu.roll(top, CHUNK, 1), bot)
        yield True

    _staggered([_Stages(stages(rs), 8) for rs in _sub_tiles(x_ref.shape[1])])


def _inproj(s, mods, mod_row, nw, w_in, w_gate_t, gate_b, lb_logits):
    bsz, t, d = s.shape
    hw = HG_HEADS * HG_DIM
    ng = gate_b.shape[0]
    tm = min(INPROJ_TILE, t)
    tok4 = lambda b, i: (b, i, 0, 0)
    const = lambda b, i: (0, 0)
    tok = lambda b, i: (b, i, 0)
    tok2 = lambda b, i: (b, 0, i, 0)
    f = lambda dt, *shape: jax.ShapeDtypeStruct(shape, dt)
    return pl.pallas_call(
        functools.partial(_inproj_kernel, hw=hw),
        grid=(bsz, t // tm),
        in_specs=[pl.BlockSpec((1, tm, d), tok),
                  pl.BlockSpec((1, 2, d), lambda b, i: (mod_row(b), 0, 0)),
                  pl.BlockSpec((1, d), const),
                  pl.BlockSpec(w_in.shape, const, pipeline_mode=pl.Buffered(1)),
                  pl.BlockSpec(w_gate_t.shape, const),
                  pl.BlockSpec((ng, 1), const),
                  pl.BlockSpec(lb_logits.shape, lambda b, i: (0, 0, 0))],
        out_specs=[pl.BlockSpec((1, tm, hw), tok), pl.BlockSpec((1, tm, hw), tok),
                   pl.BlockSpec((1, tm, hw), tok),
                   pl.BlockSpec((1, 2, tm, hw), tok2),
                   pl.BlockSpec((1, tm, hw), tok), pl.BlockSpec((1, tm, hw), tok),
                   pl.BlockSpec((1, tm, hw), tok), pl.BlockSpec((1, tm // CHUNK, ng // 2, LANE), tok4)],
        out_shape=[f(F32, bsz, t, hw), f(BF16, bsz, t, hw), f(BF16, bsz, t, hw),
                   f(F32, bsz, 2, t, hw),
                   f(BF16, bsz, t, hw), f(BF16, bsz, t, hw), f(BF16, bsz, t, hw),
                   f(F32, bsz, t // CHUNK, ng // 2, LANE)],
        compiler_params=_cparams(("parallel", "parallel")),
        name="inproj",
    )(s, mods, nw, w_in, w_gate_t, gate_b, lb_logits)


def _conv_kernel(x_ref, w_ref, b_ref, o_ref, *, width, q_width):
    x = x_ref[0].astype(F32)
    t, cb = x.shape
    col = lax.broadcasted_iota(jnp.int32, x.shape, 0) % width
    xl = jnp.where(col == 0, 0.0, pltpu.roll(x, 1, 0))
    xr = jnp.where(col == width - 1, 0.0, pltpu.roll(x, t - 1, 0))

    def row(di):
        return (w_ref[di, 0:1, :] * xl + w_ref[di, 1:2, :] * x + w_ref[di, 2:3, :] * xr)

    y = row(1)
    if t > width:
        pad = jnp.zeros((width, x.shape[1]), F32)
        y = y + jnp.concatenate([pad, row(0)[:t - width]], axis=0)
        y = y + jnp.concatenate([row(2)[width:], pad], axis=0)
    y = _silu(y + b_ref[...])
    channel = pl.program_id(1) * cb + lax.broadcasted_iota(jnp.int32, (1, cb), 1)
    o_ref[0] = (y * jnp.where(channel < q_width, ML_QK ** -0.5, 1.0)).astype(BF16)


def _conv(x, w, b, width):
    bsz, t, ch = x.shape
    cb = ch if t <= TOKEN_TILE else LANE
    return pl.pallas_call(
        functools.partial(_conv_kernel, width=width, q_width=ML_HEADS * ML_QK),
        grid=(bsz, ch // cb),
        in_specs=[pl.BlockSpec((1, t, cb), lambda b_, c: (b_, 0, c)),
                  pl.BlockSpec((3, 3, cb), lambda b_, c: (0, 0, c)),
                  pl.BlockSpec((1, cb), lambda b_, c: (0, c))],
        out_specs=pl.BlockSpec((1, t, cb), lambda b_, c: (b_, 0, c)),
        out_shape=jax.ShapeDtypeStruct((bsz, t, ch), BF16),
        compiler_params=_cparams(("parallel", "parallel")),
        name="conv",
    )(x, w, b)


def _positions():
    t = np.arange(CHUNK)
    return [t, CHUNK - 1 - t]


def _hgrn2_masks():
    ds, vs, ws = [], [], []
    for p in _positions():
        le = (p[None, :] <= p[:, None]).astype(np.float32)
        order = [p[SUB * j] // SUB for j in range(N_SUB)]
        ref = np.zeros((32, CHUNK), np.float32)
        valid = np.zeros((N_SUB * N_SUB, 1), np.float32)
        for j in range(N_SUB):
            ref[j] = p <= SUB * order[j] + SUB // 2 - 1
            ref[8 + j] = p <= SUB * order[j] + SUB - 1
        ref[N_SUB] = 1.0
        for j in range(N_SUB):
            for i in range(N_SUB):
                if order[i] < order[j]:
                    ref[16 + N_SUB * j + i] = ref[j] - ref[8 + i]
                    valid[N_SUB * j + i] = 1.0
        ds.append(np.concatenate([le, ref], axis=0))
        vs.append(np.broadcast_to(valid, (N_SUB * N_SUB, HG_HEADS * HG_DIM)))
        tok_blk = np.arange(CHUNK)[:, None] // SUB
        ws.append(np.concatenate([(tok_blk == j) * le for j in range(N_SUB)], axis=1))
    return (np.stack(ds).astype(np.float32), np.stack(vs).astype(np.float32),
            np.stack(ws).astype(np.float32))


def _mlstm_masks():
    le2s, let2s = [], []
    for p in _positions():
        le = (p[None, :] <= p[:, None]).astype(np.float32)
        le2s.append(np.concatenate([le, le], axis=1))
        let2 = np.zeros((2 * CHUNK, 2 * CHUNK), np.float32)
        let2[:CHUNK, :CHUNK] = le.T
        let2[CHUNK:, CHUNK:] = le.T
        let2s.append(let2)
    return np.stack(le2s), np.stack(let2s)


def _chunk_groups(nc, group):
    bounds = [0] + list(range(max(1, group // 2), nc, group)) + [nc]
    return [list(range(a, b)) for a, b in zip(bounds[:-1], bounds[1:]) if b > a]


def _pipelined(chunk_groups, stages_of, step_of):
    queue = []
    for chunks in chunk_groups:
        for _ in stages_of(chunks):
            if queue:
                queue.pop(0)()
            yield True
        while queue:
            queue.pop(0)()
            yield True
        queue = [functools.partial(step_of, c) for c in chunks]
    for step in queue:
        step()
        yield True


def _interleave(programs):
    live = list(programs)
    while live:
        live = [p for p in live if next(p, None) is not None]


def _chunk_rows(c, nc, direction):
    cc = jnp.where(direction == 0, c, nc - 1 - c)
    return pl.multiple_of(cc * CHUNK, CHUNK)


def _hgrn2_program(q_ref, v_ref, lf_ref, d_ref, pv_ref, w_ref, s0_ref, o_ref, st_ref,
                   u_ref, qin_ref, g_ref, a_ref, vt_ref, bx_ref, kx_ref, *, nc):
    direction = pl.program_id(1)

    @pl.when(pl.program_id(2) == 0)
    def _():
        st_ref[...] = s0_ref[...]

    dmat = d_ref[0].astype(BF16)
    pair_valid = pv_ref[0]
    wmask = w_ref[0] > 0.0
    L = CHUNK
    hw = q_ref.shape[-1]
    upper_rows = lax.broadcasted_iota(jnp.int32, (L, 2 * L), 0) < L // 2

    def intra(cs):
        group = len(cs)
        rows = [pl.ds(_chunk_rows(c, nc, direction), L) for c in cs]
        items = [(i, h) for i in range(group) for h in range(HG_HEADS)]
        lanes = [slice(h * HG_DIM, (h + 1) * HG_DIM) for h in range(HG_HEADS)]
        lfs = [lf_ref[0, 0, r, :] for r in rows]
        parts = [_split2(lf) for lf in lfs]
        e_all = sum(_dot(dmat, jnp.concatenate([parts[i][n] for i in range(group)], axis=1))
                    for n in range(2))
        yield
        q_mids, k_mids, k_outs, vbs, spans = [], [], [], [], []
        for i in range(group):
            e = e_all[:, i * hw:(i + 1) * hw]
            q = q_ref[0, rows[i], :]
            k = 1.0 - jnp.exp2(lfs[i])
            b = e[0:L]
            b_tot = e[L + N_SUB:L + N_SUB + 1]
            sub_rows = lambda r0: jnp.concatenate(
                [jnp.broadcast_to(e[r0 + j:r0 + j + 1], (SUB, hw)) for j in range(N_SUB)], axis=0)
            mid_rows = sub_rows(L)
            end_rows = sub_rows(L + 8)
            to_mid = b - mid_rows
            spans.append(jnp.max(jnp.abs(to_mid)))
            q_mids.append((q * jnp.exp2(jnp.minimum(to_mid, EXP2_CLAMP))).astype(BF16))
            k_diag = k * jnp.exp2(jnp.minimum(-to_mid, EXP2_CLAMP))
            k_end = k * jnp.exp2(end_rows - b)
            cross = jnp.exp2(e[L + 16:L + 32]) * pair_valid
            slabs = []
            for j in range(N_SUB):
                for ib in range(N_SUB):
                    rs = slice(ib * SUB, (ib + 1) * SUB)
                    row = N_SUB * j + ib
                    slabs.append(k_diag[rs] if ib == j else k_end[rs] * cross[row:row + 1])
            k_mids.append(jnp.concatenate(slabs, axis=0).astype(BF16))
            k_outs.append((k * jnp.exp2(b_tot - b)).astype(BF16))
            vbs.append(v_ref[0, rows[i], :])
            qin_ref[cs[i]] = (q * jnp.exp2(b)).astype(BF16)
            g_ref[cs[i]] = jnp.broadcast_to(jnp.exp2(b_tot), (SUBLANE, hw))
            yield
        scores = [_dot_nt(q_mids[i][:, lanes[h]], k_mids[i][:, lanes[h]]) for i, h in items]
        yield
        a_s = [jnp.where(wmask, s, 0.0) for s in scores]
        a_s = [jnp.where(upper_rows, a[:, :2 * L], a[:, 2 * L:]).astype(BF16) for a in a_s]
        for i in range(group):
            a_ref[cs[i]] = jnp.concatenate(a_s[i * HG_HEADS:(i + 1) * HG_HEADS], axis=1)
        yield
        for i, h in items:
            v_h = vbs[i][:, lanes[h]].astype(F32)
            vt_ref[cs[i], h] = jnp.concatenate([v_h, v_h], axis=0).T.astype(BF16)
        yield
        for i, h in items:
            u_ref[cs[i], h] = _dot_tn(vbs[i][:, lanes[h]], k_outs[i][:, lanes[h]])

        @pl.when(functools.reduce(jnp.maximum, spans) > EXP2_CLAMP)
        def _():
            for i in range(group):
                exact_intra(cs[i], rows[i])
        yield

    def exact_intra(c, rows):
        lf = lf_ref[0, 0, rows, :]
        bx_ref[...] = sum(_dot(dmat[0:L], x) for x in _split3(lf))
        kx_ref[...] = 1.0 - jnp.exp2(lf)
        q = q_ref[0, rows, :]
        tok = lax.broadcasted_iota(jnp.int32, (L, 1), 0)
        pos = jnp.where(direction == 0, tok, L - 1 - tok)
        lane_in_head = lax.broadcasted_iota(jnp.int32, (1, hw), 1) % HG_DIM

        def add_source(s, acc):
            b_s = bx_ref[pl.ds(s, 1), :]
            z = q * kx_ref[pl.ds(s, 1), :] * jnp.exp2(jnp.minimum(bx_ref[...] - b_s, 0.0))
            seen = pos >= jnp.where(direction == 0, s, L - 1 - s)
            cols = [jnp.broadcast_to(jnp.where(seen, jnp.sum(z[:, ln], axis=1, keepdims=True), 0.0), (L, HG_DIM))
                    for ln in (slice(h * HG_DIM, (h + 1) * HG_DIM) for h in range(HG_HEADS))]
            return jnp.where(lane_in_head == s, jnp.concatenate(cols, axis=1), acc)

        a_ref[c] = lax.fori_loop(0, L, add_source, jnp.zeros((L, hw), F32)).astype(BF16)

    def carry_state(c):
        rows = pl.ds(_chunk_rows(c, nc, direction), L)
        q_in = qin_ref[c]
        a = a_ref[c]
        g = g_ref[c][0:1]
        outs, states = [], []
        for h in range(HG_HEADS):
            ln = slice(h * HG_DIM, (h + 1) * HG_DIM)
            st = st_ref[0, 0, h]
            outs.append(_dot_nt(jnp.concatenate([a[:, ln], q_in[:, ln]], axis=1),
                                jnp.concatenate([vt_ref[c, h], st.astype(BF16)], axis=1)))
            states.append(g[:, ln] * st + u_ref[c, h])
        o_ref[0, 0, rows, :] = jnp.concatenate(outs, axis=1).astype(BF16)
        for h in range(HG_HEADS):
            st_ref[0, 0, h] = states[h]

    return _pipelined(_chunk_groups(nc, GROUP), intra, carry_state)


N_HG = (7, 2, 7)
N_ML = (7, 3, 5)


def _mlstm_program(qk_ref, v_ref, g_ref, le2_ref, let2_ref, c0_ref, m0_ref,
                   o_ref, c_ref, mm_ref,
                   al_ref, u_ref, ct_ref, mloc_ref, sc_ref, *, nc):
    direction = pl.program_id(1)

    @pl.when(pl.program_id(2) == 0)
    def _():
        c_ref[...] = c0_ref[...]
        mm_ref[...] = m0_ref[...]

    let2 = let2_ref[0].astype(BF16)
    causal = le2_ref[0] > 0.0
    L = CHUNK
    neg = -jnp.inf
    lane = lax.broadcasted_iota(jnp.int32, (1, LANE), 1)
    seg = [lane < L, lane >= L]
    lane_t = lax.broadcasted_iota(jnp.int32, (L, LANE), 1) < L
    npair = ML_HEADS // 2
    ones_v = jnp.ones((L, ML_V), BF16)
    zb = jnp.zeros((L, LANE), BF16)

    def by_head(x):
        return jnp.concatenate([jnp.where(lane_t, x, zb), jnp.where(lane_t, zb, x)], axis=0)

    def v_stack(vb, p):
        return jnp.concatenate(
            [jnp.concatenate([vb[:, 2 * p * ML_V:(2 * p + 1) * ML_V], ones_v], axis=1),
             jnp.concatenate([vb[:, (2 * p + 1) * ML_V:(2 * p + 2) * ML_V], ones_v], axis=1)], axis=0)

    def intra(cs):
        group = len(cs)
        ccs = [jnp.where(direction == 0, c, nc - 1 - c) for c in cs]
        rows = [pl.ds(pl.multiple_of(cc * L, L), L) for cc in ccs]
        pairs = [(i, p) for i in range(group) for p in range(npair)]
        gs = [jnp.where(direction == 0, g_ref[0, cc][0:4], g_ref[0, cc][4:8]) for cc in ccs]
        pad = [jnp.zeros((-2 * group % SUBLANE, LANE), F32)] if 2 * group % SUBLANE else []
        i2 = jnp.concatenate([g[0:2] for g in gs] + pad, axis=0)
        lf2 = _log_sigmoid(jnp.concatenate([g[2:4] for g in gs] + pad, axis=0))
        parts = _split3(lf2)
        b2 = sum(_dot(x, let2) for x in parts)
        r2 = i2 - b2
        g_a = jnp.max(jnp.where(seg[0], r2, neg), axis=1, keepdims=True)
        g_b = jnp.max(jnp.where(seg[1], r2, neg), axis=1, keepdims=True)
        e_w = jnp.exp(r2 - jnp.where(seg[0], g_a, g_b))
        bt_a = jnp.sum(jnp.where(seg[0], lf2, 0.0), axis=1, keepdims=True)
        bt_b = jnp.sum(jnp.where(seg[1], lf2, 0.0), axis=1, keepdims=True)
        yield

        def as_columns(row):
            cols = jnp.broadcast_to(row, (LANE, LANE)).T
            return jnp.where(lane_t, cols[0:L], cols[L:2 * L])

        b_ts = [as_columns(b2[2 * i + p:2 * i + p + 1]) for i, p in pairs]
        for i in range(group):
            sc_ref[cs[i]] = jnp.concatenate(
                [jnp.broadcast_to(x[2 * i + p:2 * i + p + 1], (1, LANE))
                 for x2 in ((bt_a, bt_b), (g_a, g_b)) for p in range(npair) for x in x2], axis=0)
        yield
        qks = [qk_ref[0, r, :] for r in rows]
        vbs = [v_ref[0, r, :] for r in rows]
        q_ps = [qks[i][:, p * LANE:(p + 1) * LANE] for i, p in pairs]
        k_ps = [qks[i][:, (npair + p) * LANE:(npair + p + 1) * LANE] for i, p in pairs]
        kbs = [by_head(k) for k in k_ps]
        scores = [_dot_nt(q, kb) for q, kb in zip(q_ps, kbs)]
        yield
        for n, (i, p) in enumerate(pairs):
            t1 = b_ts[n]
            d_log = jnp.where(causal, t1 + r2[2 * i + p:2 * i + p + 1], neg)
            m_a = jnp.max(jnp.where(lane_t, d_log, neg), axis=1, keepdims=True)
            m_b = jnp.max(jnp.where(lane_t, neg, d_log), axis=1, keepdims=True)
            m_loc = jnp.where(lane_t, m_a, m_b)
            ct_ref[cs[i], p] = t1
            mloc_ref[cs[i], p] = m_loc
            al_ref[cs[i], p] = scores[n] * jnp.exp(d_log - m_loc)
        yield
        v_stacks = [v_stack(vbs[i], p) for i, p in pairs]
        yield
        kts = [(kb.astype(F32).T * e_w[2 * i + p:2 * i + p + 1]).astype(BF16) for kb, (i, p) in zip(kbs, pairs)]
        for n, (i, p) in enumerate(pairs):
            u_ref[cs[i], p] = _dot(kts[n], v_stacks[n])
        yield

    def carry_state(c):
        cc = jnp.where(direction == 0, c, nc - 1 - c)
        rows = pl.ds(pl.multiple_of(cc * L, L), L)
        m_all = mm_ref[0, 0]
        sc = sc_ref[c]
        qk = qk_ref[0, rows, :]
        vb = v_ref[0, rows, :]
        outs, states, m_rows = [], [], []
        for p in range(ML_HEADS // 2):
            c_pair = c_ref[0, 0, p]
            inc = u_ref[c, p]
            b_t = ct_ref[c, p]
            m_loc = mloc_ref[c, p]
            m_in2 = jnp.where(seg[0], m_all[2 * p:2 * p + 1], m_all[2 * p + 1:2 * p + 2])
            m_t = jnp.maximum(m_loc, b_t + m_in2)
            lhs_a = (jnp.exp(m_loc - m_t) * al_ref[c, p]).astype(BF16)
            lhs_q = (jnp.exp(b_t + m_in2 - m_t) * qk[:, p * LANE:(p + 1) * LANE].astype(F32)).astype(BF16)
            lhs = jnp.concatenate([by_head(lhs_a), by_head(lhs_q)], axis=1)
            tot = _dot(lhs, jnp.concatenate([v_stack(vb, p), c_pair.astype(BF16)], axis=0))
            floor = jnp.exp(-m_t)
            floor_r = pltpu.roll(floor, L, 1)
            new_rows = []
            for hh in range(2):
                h = 2 * p + hh
                rs = slice(hh * L, (hh + 1) * L)
                m_in = m_all[h:h + 1]
                floor_h = jnp.where(lane_t, floor, floor_r) if hh == 0 else jnp.where(lane_t, floor_r, floor)
                outs.append(tot[rs, :ML_V] / jnp.maximum(jnp.abs(tot[rs, ML_V:]), floor_h))
                b_tot = sc[h:h + 1]
                m_new = b_tot + jnp.maximum(m_in, sc[ML_HEADS + h:ML_HEADS + h + 1])
                carry_w = jnp.exp(b_tot + m_in - m_new)
                inc_w = jnp.exp(b_tot + sc[ML_HEADS + h:ML_HEADS + h + 1] - m_new)
                cw2 = jnp.concatenate([carry_w, carry_w], axis=1)
                iw2 = jnp.concatenate([inc_w, inc_w], axis=1)
                new_rows.append(cw2 * c_pair[rs] + iw2 * inc[rs])
                m_rows.append(m_new)
            states.append(jnp.concatenate(new_rows, axis=0))
        o_ref[0, 0, rows, :] = jnp.concatenate(outs, axis=1).astype(BF16)
        for p in range(ML_HEADS // 2):
            c_ref[0, 0, p] = states[p]
        mm_ref[0, 0] = jnp.concatenate(m_rows + [m_all[ML_HEADS:]], axis=0)

    return _pipelined(_chunk_groups(nc, ML_GROUP), intra, carry_state)


def _scan_kernel(*refs, nc):
    hg_in, ml_in, hg_out, ml_out, hg_scr, ml_scr = (
        refs[a:b] for a, b in zip(np.cumsum([0, N_HG[0], N_ML[0], N_HG[1], N_ML[1], N_HG[2]]),
                                  np.cumsum([N_HG[0], N_ML[0], N_HG[1], N_ML[1], N_HG[2], N_ML[2]])))
    _interleave([_hgrn2_program(*hg_in, *hg_out, *hg_scr, nc=nc),
                 _mlstm_program(*ml_in, *ml_out, *ml_scr, nc=nc)])


def _scans(hq, hv, hlf, qk, mv, gates, s0, c0, m0, dmat, pair_valid, wmask, le2, let2):
    bsz, t, hw = hq.shape
    nc = min(SCAN_CHUNKS, t // CHUNK)
    tb = nc * CHUNK
    nblk = t // tb
    npair = ML_HEADS // 2

    def blk(j, d):
        return jnp.where(d == 0, j, nblk - 1 - j)

    tok = lambda b, d, j: (b, blk(j, d), 0)
    tok_d = lambda b, d, j: (b, d, blk(j, d), 0)
    per_dir = lambda a: pl.BlockSpec((1,) + a.shape[1:], lambda b, d, j: (d,) + (0,) * (a.ndim - 1))
    state = lambda a: pl.BlockSpec((1, 1) + a.shape[2:], lambda b, d, j: (b, d) + (0,) * (a.ndim - 2))
    out_tok = pl.BlockSpec((1, 1, tb, hw), tok_d)
    hg_in = [pl.BlockSpec((1, tb, hw), tok), pl.BlockSpec((1, tb, hw), tok), pl.BlockSpec((1, 1, tb, hw), tok_d),
             per_dir(dmat), per_dir(pair_valid), per_dir(wmask), state(s0)]
    ml_in = [pl.BlockSpec((1, tb, qk.shape[-1]), tok), pl.BlockSpec((1, tb, hw), tok),
             pl.BlockSpec((1, nc) + gates.shape[2:], lambda b, d, j: (b, blk(j, d), 0, 0)),
             per_dir(le2), per_dir(let2), state(c0), state(m0)]
    sds = jax.ShapeDtypeStruct
    hg_scr = [pltpu.VMEM((nc,) + s0.shape[2:], F32),
              pltpu.VMEM((nc, CHUNK, hw), BF16),
              pltpu.VMEM((nc, SUBLANE, hw), F32),
              pltpu.VMEM((nc, CHUNK, hw), BF16),
              pltpu.VMEM((nc,) + s0.shape[2:], BF16),
              pltpu.VMEM((CHUNK, hw), F32),
              pltpu.VMEM((CHUNK, hw), F32)]
    ml_scr = [pltpu.VMEM((nc, npair, CHUNK, LANE), F32),
              pltpu.VMEM((nc, npair, 2 * ML_QK, 2 * ML_V), F32),
              pltpu.VMEM((nc, npair, CHUNK, LANE), F32),
              pltpu.VMEM((nc, npair, CHUNK, LANE), F32),
              pltpu.VMEM((nc, SUBLANE, LANE), F32)]
    assert (len(hg_in), 2, len(hg_scr)) == N_HG and (len(ml_in), 3, len(ml_scr)) == N_ML
    return pl.pallas_call(
        functools.partial(_scan_kernel, nc=nc),
        grid=(bsz, 2, nblk),
        in_specs=hg_in + ml_in,
        out_specs=[out_tok, state(s0), out_tok, state(c0), state(m0)],
        out_shape=[sds((bsz, 2, t, hw), BF16), sds(s0.shape, F32),
                   sds((bsz, 2, t, hw), BF16), sds(c0.shape, F32), sds(m0.shape, F32)],
        scratch_shapes=hg_scr + ml_scr,
        compiler_params=_cparams(("parallel", "parallel", "arbitrary")),
        name="scans",
    )(hq, hv, hlf, dmat, pair_valid, wmask, s0, qk, mv, gates, le2, let2, c0, m0)


def _head_rms(x, heads):
    dh = x.shape[-1] // heads
    return jnp.concatenate([_rms(x[:, h * dh:(h + 1) * dh]) for h in range(heads)], axis=1)


def _mix_ffn_kernel(oh_ref, om_ref, hg_ref, mo_ref, x_ref, mod_ref, hnw_ref, mnw_ref, wo_ref,
                    nw_ref, w1_ref, w2_ref, fw_ref, o_ref, a_ref, x_scr, h_scr, *, d_ff, tf):
    mix_gate = mod_ref[0, 0:1, :]
    mods = [mod_ref[0, k:k + 1, :] for k in range(1, 4)]

    def stages(rs):
        def mixed(pr):
            oh = oh_ref[0, 0, pr, :].astype(F32) + oh_ref[0, 1, pr, :].astype(F32)
            om = om_ref[0, 0, pr, :].astype(F32) + om_ref[0, 1, pr, :].astype(F32)
            hg_out = _head_rms(oh, HG_HEADS) * hnw_ref[...] * _silu(hg_ref[0, pr, :].astype(F32))
            ml_out = _sigmoid(mo_ref[0, pr, :].astype(F32)) * (_head_rms(om, ML_HEADS) * mnw_ref[...])
            merged = jnp.concatenate([hg_out, ml_out], axis=1).astype(BF16)
            return x_ref[0, pr, :] + mix_gate * _dot(merged, wo_ref[...])

        def put(y):
            o_ref[0, rs, :] = _rms(y) * fw_ref[...]

        return _Stages(_ffn_stages(mixed, put, mods, nw_ref, w1_ref, w2_ref, a_ref, x_scr, h_scr,
                                   rs, d_ff, tf), _ffn_lead(rs, d_ff, tf))

    _staggered([stages(rs) for rs in _sub_tiles(x_ref.shape[1])])


def _mix_ffn(oh, om, hg, mo, x, mods, hnw, mnw, w_out, nw, w1, w2, fw):
    bsz, t, d = x.shape
    hw = hg.shape[-1]
    d_ff = w2.shape[0]
    tm = min(TOKEN_TILE, t)
    tok = lambda b, i: (b, i, 0)
    tok2 = lambda b, i: (b, 0, i, 0)
    const = lambda b, i: (0, 0)
    resident = lambda a: pl.BlockSpec(a.shape, const, pipeline_mode=pl.Buffered(1))
    return pl.pallas_call(
        functools.partial(_mix_ffn_kernel, d_ff=d_ff, tf=FF_TILE),
        grid=(bsz, t // tm),
        in_specs=[pl.BlockSpec((1, 2, tm, hw), tok2), pl.BlockSpec((1, 2, tm, hw), tok2),
                  pl.BlockSpec((1, tm, hw), tok), pl.BlockSpec((1, tm, hw), tok),
                  pl.BlockSpec((1, tm, d), tok),
                  pl.BlockSpec((1, 4, d), lambda b, i: (b, 0, 0)),
                  pl.BlockSpec((1, hw), const), pl.BlockSpec((1, hw), const), resident(w_out),
                  pl.BlockSpec((1, d), const), resident(w1), resident(w2), pl.BlockSpec((1, d), const)],
        out_specs=pl.BlockSpec((1, tm, d), tok),
        out_shape=jax.ShapeDtypeStruct((bsz, t, d), F32),
        scratch_shapes=[pltpu.VMEM((tm, d_ff), BF16),
                        pltpu.VMEM((tm, d), F32),
                        pltpu.VMEM((tm, d), BF16)],
        compiler_params=_cparams(("parallel", "parallel")),
        name="mix_ffn_final",
    )(oh, om, hg, mo, x, mods, hnw, mnw, w_out, nw, w1, w2, fw)


def kernel(x, c, ctx, c_ctx, w_mod, b_mod, norm1_w, ffn1_w1, ffn1_w2, norm2_w, w_in, ml_gate_b,
           ml_conv_w, ml_conv_b, hg_lb_logits, hg_norm_w, ml_norm_w, w_out, norm3_w, ffn2_w1, ffn2_w2,
           final_norm_w):
    bsz, seq, d = x.shape
    n_ctx = ctx.shape[1]
    assert w_mod.shape[0] == 1, "single-layer kernel"
    assert seq % max(TOKEN_TILE, SCAN_CHUNKS * CHUNK) == 0 and n_ctx % CHUNK == 0 and GRID_W == CHUNK
    hw = HG_HEADS * HG_DIM
    ng = ml_gate_b.shape[-1]

    rows = -(-(bsz + 1) // SUBLANE) * SUBLANE
    cvec = jnp.zeros((rows, d), F32).at[:bsz].set(c).at[bsz].set(c_ctx)
    mods = _modulation(cvec, w_mod[0], b_mod[0][None, :]).reshape(rows, N_MOD, d)
    lat_row = lambda b: b
    ctx_row = lambda b: bsz

    row = lambda a: a.reshape(1, -1)
    w1a, w2a = ffn1_w1[0], ffn1_w2[0]
    x1 = _ffn(x, mods[:, 0:3], lat_row, row(norm1_w[0]), w1a, w2a)
    s1 = _ffn(ctx, mods[:, 0:3], ctx_row, row(norm1_w[0]), w1a, w2a)

    w_in_b = w_in[0].T
    gate_order = np.arange(ng).reshape(2, 2, ML_HEADS // 2, 2).transpose(3, 0, 1, 2).reshape(-1)
    w_gate_t = w_in[0][:, 8 * hw:].T[gate_order].astype(BF16)
    gate_b = ml_gate_b[0][gate_order].reshape(ng, 1)
    dmat, pair_valid, wmask = (jnp.asarray(a) for a in _hgrn2_masks())
    le2, let2 = (jnp.asarray(a) for a in _mlstm_masks())
    conv_w, conv_b = ml_conv_w[0], row(ml_conv_b[0])
    npair = ML_HEADS // 2

    def mixer_scans(s, mod_row, width, s0, c0, m0):
        hq, hv, hg, hlf, mqk, mv, mo, gates = _inproj(
            s, mods[:, 3:5], mod_row, row(norm2_w[0]), w_in_b, w_gate_t, gate_b, hg_lb_logits)
        qk = _conv(mqk, conv_w, conv_b, width)
        oh, s_fin, om, c_fin, m_fin = _scans(hq, hv, hlf, qk, mv, gates, s0, c0, m0,
                                             dmat, pair_valid, wmask, le2, let2)
        return oh, om, hg, mo, s_fin, c_fin, m_fin

    s0 = jnp.zeros((bsz, 2, HG_HEADS, HG_DIM, HG_DIM), F32)
    c0 = jnp.zeros((bsz, 2, npair, 2 * ML_QK, 2 * ML_V), F32)
    m0 = jnp.zeros((bsz, 2, SUBLANE, LANE), F32)
    _, _, _, _, s_ctx, c_ctx_state, m_ctx = mixer_scans(s1, ctx_row, n_ctx, s0, c0, m0)
    oh, om, hg, mo, _, _, _ = mixer_scans(x1, lat_row, GRID_W, s_ctx, c_ctx_state, m_ctx)

    return _mix_ffn(oh, om, hg, mo, x1, mods[:, 5:9], row(hg_norm_w[0]), row(ml_norm_w[0]),
                    w_out[0].astype(BF16), row(norm3_w[0]), ffn2_w1[0], ffn2_w2[0], row(final_norm_w))
```
